```python
import math
import jax
import jax.numpy as jnp
from jax import lax
import numpy as np


D_MODEL = 1024
BATCH = 4
SEQ = 4096
DEPTH = 4

A_HEADS = 8
A_HEAD_DIM = 128
IDX_HEADS = 8
IDX_DIM = 64
TOPK_MAX = 256
Q_BLOCK = 128
REL_BUCKETS = 32
REL_MAX_DIST = 128
B_HEADS = 8
B_HEAD_DIM = D_MODEL // B_HEADS
B_CHUNK = 64
C_HEADS = 4
C_QK_DIM = 256
C_V_DIM = 512
C_CHUNK = 128
N_GROUPS = 4
EXPERTS_PER_GROUP = 8
N_EXPERTS = N_GROUPS * EXPERTS_PER_GROUP
EXPERT_TOPK = 2
D_EXPERT = 512
EPS = 1e-6

A_WIDTH = A_HEADS * A_HEAD_DIM
B_WIDTH = B_HEADS * B_HEAD_DIM
C_QK_WIDTH = C_HEADS * C_QK_DIM
C_V_WIDTH = C_HEADS * C_V_DIM
IN_SPLITS = (A_WIDTH, A_WIDTH, A_WIDTH, IDX_HEADS * IDX_DIM, IDX_DIM, IDX_HEADS,
             B_WIDTH, B_WIDTH, B_WIDTH, B_WIDTH,
             C_QK_WIDTH, C_QK_WIDTH, C_V_WIDTH, C_V_WIDTH,
             D_MODEL, D_MODEL, D_MODEL)
N_IN = sum(IN_SPLITS)

kernel_name = 'hybrid_dsa_hgrn2_retnet_hmoe'


def rmsnorm(x, g=None):
    xf = x.astype(jnp.float32)
    y = xf * lax.rsqrt(jnp.mean(xf * xf, axis=-1, keepdims=True) + EPS)
    if g is not None:
        y = y * g.astype(jnp.float32)
    return y.astype(x.dtype)


def t5_bucket(rel):
    max_exact = REL_BUCKETS // 2
    relf = jnp.maximum(rel, 1).astype(jnp.float32)
    large = max_exact + (jnp.log(relf / max_exact) / math.log(REL_MAX_DIST / max_exact)
                         * (REL_BUCKETS - max_exact)).astype(jnp.int32)
    large = jnp.minimum(large, REL_BUCKETS - 1)
    return jnp.where(rel < max_exact, rel, large)


def split_cols(proj):
    offs = np.cumsum(np.array(IN_SPLITS))[:-1].tolist()
    return jnp.split(proj, offs, axis=-1)


def dsa_attention(q, k, v, q_idx, k_idx, w_idx, rel_bias):
    bsz, seq = q.shape[0], q.shape[1]
    top_k = min(TOPK_MAX, seq // 4)
    n_blk = seq // Q_BLOCK
    s_pos = jnp.arange(seq, dtype=jnp.int32)
    k_idx32 = k_idx.astype(jnp.float32)
    idx_scale = (IDX_HEADS * IDX_DIM) ** -0.5

    def to_blocks(a):
        return jnp.moveaxis(a.reshape((bsz, n_blk, Q_BLOCK) + a.shape[2:]), 1, 0)

    def one_block(args):
        qb, qib, wb, t0 = args
        t_pos = t0 + jnp.arange(Q_BLOCK, dtype=jnp.int32)
        rel = jax.nn.relu(jnp.einsum('bqhd,bsd->bqhs', qib.astype(jnp.float32), k_idx32))
        score = jnp.einsum('bqhs,bqh->bqs', rel, wb.astype(jnp.float32) * idx_scale)
        causal = s_pos[None, :] <= t_pos[:, None]
        score = jnp.where(causal[None], score, -jnp.inf)
        _, sel = lax.top_k(score, top_k)
        valid = sel <= t_pos[None, :, None]
        k_sel = jax.vmap(lambda kb, ib: kb[ib])(k, sel)
        v_sel = jax.vmap(lambda vb, ib: vb[ib])(v, sel)
        logits = jnp.einsum('bqhd,bqkhd->bqhk', qb, k_sel).astype(jnp.float32) * (A_HEAD_DIM ** -0.5)
        bucket = t5_bucket(jnp.maximum(t_pos[None, :, None] - sel, 0))
        logits = logits + jnp.transpose(rel_bias[bucket], (0, 1, 3, 2)).astype(jnp.float32)
        logits = jnp.where(valid[:, :, None, :], logits, -jnp.inf)
        p = jax.nn.softmax(logits, axis=-1).astype(v.dtype)
        return jnp.einsum('bqhk,bqkhd->bqhd', p, v_sel)

    t0s = jnp.arange(n_blk, dtype=jnp.int32) * Q_BLOCK
    out = lax.map(one_block, (to_blocks(q), to_blocks(q_idx), to_blocks(w_idx), t0s))
    return jnp.moveaxis(out, 0, 1).reshape(bsz, seq, A_WIDTH)


def hgrn2(q, f_pre, i, g_out, lb, norm_g):
    bsz, seq = q.shape[0], q.shape[1]
    nc = seq // B_CHUNK
    lb = lb.reshape(B_HEADS, B_HEAD_DIM).astype(jnp.float32)
    f = lb + (1.0 - lb) * jax.nn.sigmoid(f_pre.astype(jnp.float32))
    log_f = jnp.log(f)
    kk = 1.0 - f

    def chunks(a):
        return jnp.transpose(a.reshape(bsz, nc, B_CHUNK, B_HEADS, -1), (1, 0, 3, 2, 4))

    causal = jnp.tril(jnp.ones((B_CHUNK, B_CHUNK), dtype=bool))

    def step(state, inp):
        qt, kt, vt, gt = inp
        b = jnp.cumsum(gt, axis=2)
        diff = b[:, :, :, None, :] - b[:, :, None, :, :]
        decay = jnp.exp(jnp.where(causal[None, None, :, :, None], diff, -jnp.inf))
        attn = jnp.einsum('bhtd,bhsd,bhtsd->bhts', qt, kt, decay)
        o = (jnp.einsum('bhts,bhsv->bhtv', attn, vt)
             + jnp.einsum('bhtd,bhdv->bhtv', qt * jnp.exp(b), state))
        b_last = b[:, :, -1:, :]
        state = (jnp.exp(b_last[:, :, 0, :, None]) * state
                 + jnp.einsum('bhsd,bhsv->bhdv', kt * jnp.exp(b_last - b), vt))
        return state, o

    s0 = jnp.zeros((bsz, B_HEADS, B_HEAD_DIM, B_HEAD_DIM), jnp.float32)
    _, o = lax.scan(step, s0, (chunks(q.astype(jnp.float32)), chunks(kk),
                               chunks(i.astype(jnp.float32)), chunks(log_f)))
    o = jnp.transpose(o, (1, 0, 3, 2, 4)).reshape(bsz, seq, B_HEADS, B_HEAD_DIM)
    o = rmsnorm(o, norm_g) * jax.nn.silu(g_out.astype(jnp.float32))
    return o.reshape(bsz, seq, B_WIDTH).astype(q.dtype)


def retention(q, k, v, g):
    bsz, seq = q.shape[0], q.shape[1]
    nc = seq // C_CHUNK
    pos = jnp.arange(seq, dtype=jnp.float32)
    theta = jnp.repeat(1.0 / (10000.0 ** jnp.linspace(0.0, 1.0, C_QK_DIM // 2)), 2)
    ang = pos[:, None] * theta[None, :]
    sin = jnp.sin(ang)[None, :, None, :]
    cos = jnp.cos(ang)[None, :, None, :]

    def rot(a):
        a2 = jnp.stack([-a[..., 1::2], a[..., 0::2]], axis=-1).reshape(a.shape)
        return a * cos + a2 * sin

    qr = rot(q.astype(jnp.float32))
    kr = rot(k.astype(jnp.float32)) * (C_QK_DIM ** -0.5)
    log_gamma = jnp.log(1.0 - 2.0 ** (-5.0 - jnp.arange(C_HEADS, dtype=jnp.float32)))
    idx = jnp.arange(C_CHUNK, dtype=jnp.float32)
    causal = idx[:, None] >= idx[None, :]
    intra_decay = jnp.exp(jnp.where(causal[None], (idx[:, None] - idx[None, :])[None] * log_gamma[:, None, None], -jnp.inf))
    q_decay = jnp.exp((idx + 1.0)[None, :] * log_gamma[:, None])[..., None]
    k_decay = jnp.exp((C_CHUNK - 1.0 - idx)[None, :] * log_gamma[:, None])[..., None]
    chunk_decay = jnp.exp(C_CHUNK * log_gamma)[:, None, None]

    def chunks(a):
        return jnp.transpose(a.reshape(bsz, nc, C_CHUNK, C_HEADS, -1), (1, 0, 3, 2, 4))

    def step(state, inp):
        qt, kt, vt = inp
        attn = jnp.einsum('bhtd,bhsd->bhts', qt, kt) * intra_decay
        o = (jnp.einsum('bhts,bhsv->bhtv', attn, vt)
             + jnp.einsum('bhtd,bhdv->bhtv', qt * q_decay, state))
        state = chunk_decay * state + jnp.einsum('bhsd,bhsv->bhdv', kt * k_decay, vt)
        return state, o

    s0 = jnp.zeros((bsz, C_HEADS, C_QK_DIM, C_V_DIM), jnp.float32)
    _, o = lax.scan(step, s0, (chunks(qr), chunks(kr), chunks(v.astype(jnp.float32))))
    o = jnp.transpose(o, (1, 0, 3, 2, 4)).reshape(bsz, seq, C_HEADS, C_V_DIM)
    o = rmsnorm(o).reshape(bsz, seq, C_V_WIDTH)
    return (jax.nn.silu(g.astype(jnp.float32)) * o).astype(q.dtype)


def hier_moe(h, rg_w, rg_b, re_w, re_b, w_gate, w_up, w_down):
    bsz, seq, dm = h.shape
    x = h.reshape(-1, dm)
    group_logits = (x @ rg_w + rg_b).astype(jnp.float32)
    group_p = jax.nn.softmax(group_logits, axis=-1)
    g_sel = jnp.argmax(group_logits, axis=-1)
    g_prob = jnp.take_along_axis(group_p, g_sel[:, None], axis=-1)
    exp_logits = (x @ re_w + re_b).astype(jnp.float32).reshape(-1, N_GROUPS, EXPERTS_PER_GROUP)
    in_group = jnp.take_along_axis(exp_logits, g_sel[:, None, None], axis=1)[:, 0]
    top_v, top_i = lax.top_k(in_group, EXPERT_TOPK)
    top_w = jax.nn.softmax(top_v, axis=-1) * g_prob
    w_group = jnp.sum(jax.nn.one_hot(top_i, EXPERTS_PER_GROUP, dtype=jnp.float32) * top_w[..., None], axis=1)
    gates = (jax.nn.one_hot(g_sel, N_GROUPS, dtype=jnp.float32)[:, :, None] * w_group[:, None, :]).astype(x.dtype)
    wg = w_gate.reshape(N_GROUPS, EXPERTS_PER_GROUP, dm, D_EXPERT)
    wu = w_up.reshape(N_GROUPS, EXPERTS_PER_GROUP, dm, D_EXPERT)
    wd = w_down.reshape(N_GROUPS, EXPERTS_PER_GROUP, D_EXPERT, dm)
    y = jnp.zeros_like(x)
    for gi in range(N_GROUPS):
        hg = jax.nn.silu(jnp.einsum('nd,edf->nef', x, wg[gi])) * jnp.einsum('nd,edf->nef', x, wu[gi])
        y = y + jnp.einsum('nef,efd->nd', hg * gates[:, gi, :, None], wd[gi])
    return y.reshape(bsz, seq, dm)


def hybrid_layer(x, c, lb, rel_bias, norm1_g, norm2_g, ada_w, ada_b, w_in, hgrn_norm_g,
                 w_branch_a, w_branch_b, w_branch_c, w_out, rg_w, rg_b, re_w, re_b,
                 e_gate, e_up, e_down):
    bsz, seq, _ = x.shape
    mod = (jax.nn.silu(c) @ ada_w + ada_b)[:, None, :]
    sh1, sc1, gt1, sh2, sc2, gt2 = jnp.split(mod, 6, axis=-1)
    h = rmsnorm(x, norm1_g) * (1.0 + sc1) + sh1
    (aq, ak, av, iq, ik, iw, bq, bf, bi, bg, cq, ck, cv, cg, ga, gb, gc) = split_cols(h @ w_in)
    o_a = dsa_attention(aq.reshape(bsz, seq, A_HEADS, A_HEAD_DIM),
                        ak.reshape(bsz, seq, A_HEADS, A_HEAD_DIM),
                        av.reshape(bsz, seq, A_HEADS, A_HEAD_DIM),
                        iq.reshape(bsz, seq, IDX_HEADS, IDX_DIM), ik, iw, rel_bias)
    hs = (bsz, seq, B_HEADS, B_HEAD_DIM)
    o_b = hgrn2(bq.reshape(hs), bf.reshape(hs), bi.reshape(hs), bg.reshape(hs), lb, hgrn_norm_g)
    o_c = retention(cq.reshape(bsz, seq, C_HEADS, C_QK_DIM), ck.reshape(bsz, seq, C_HEADS, C_QK_DIM),
                    cv.reshape(bsz, seq, C_HEADS, C_V_DIM), cg)
    merged = (jax.nn.sigmoid(ga) * (o_a @ w_branch_a)
              + jax.nn.sigmoid(gb) * (o_b @ w_branch_b)
              + jax.nn.sigmoid(gc) * (o_c @ w_branch_c))
    x = x + gt1 * (merged @ w_out)
    h2 = rmsnorm(x, norm2_g) * (1.0 + sc2) + sh2
    x = x + gt2 * hier_moe(h2, rg_w, rg_b, re_w, re_b, e_gate, e_up, e_down)
    return x


def setup_inputs(seed: int = 0) -> dict:
    key = jax.random.key(seed)
    ks = jax.random.split(key, 22)
    D = D_MODEL

    def nrm(k, shape, scale):
        return jax.random.normal(k, shape, jnp.float32) * scale

    return {
        'x': nrm(ks[0], (BATCH, SEQ, D), 1.0),
        'c': nrm(ks[1], (BATCH, D), 1.0),
        'rel_bias': nrm(ks[2], (REL_BUCKETS, A_HEADS), 0.5),
        'hgrn_lb_raw': nrm(ks[3], (DEPTH, B_WIDTH), 0.5),
        'norm1_g': 1.0 + nrm(ks[4], (DEPTH, D), 0.05),
        'norm2_g': 1.0 + nrm(ks[5], (DEPTH, D), 0.05),
        'ada_w': nrm(ks[6], (DEPTH, D, 6 * D), 0.5 * D ** -0.5),
        'ada_b': nrm(ks[7], (DEPTH, 6 * D), 0.02),
        'w_in': nrm(ks[8], (DEPTH, D, N_IN), D ** -0.5),
        'hgrn_norm_g': 1.0 + nrm(ks[9], (DEPTH, B_HEAD_DIM), 0.05),
        'w_branch_a': nrm(ks[10], (DEPTH, A_WIDTH, D), A_WIDTH ** -0.5),
        'w_branch_b': nrm(ks[11], (DEPTH, B_WIDTH, D), B_WIDTH ** -0.5),
        'w_branch_c': nrm(ks[12], (DEPTH, C_V_WIDTH, D), C_V_WIDTH ** -0.5),
        'w_out': nrm(ks[13], (DEPTH, D, D), D ** -0.5),
        'router_group_w': nrm(ks[14], (DEPTH, D, N_GROUPS), D ** -0.5),
        'router_group_b': nrm(ks[15], (DEPTH, N_GROUPS), 0.01),
        'router_expert_w': nrm(ks[16], (DEPTH, D, N_EXPERTS), D ** -0.5),
        'router_expert_b': nrm(ks[17], (DEPTH, N_EXPERTS), 0.01),
        'expert_w_gate': nrm(ks[18], (DEPTH, N_EXPERTS, D, D_EXPERT), D ** -0.5),
        'expert_w_up': nrm(ks[19], (DEPTH, N_EXPERTS, D, D_EXPERT), D ** -0.5),
        'expert_w_down': nrm(ks[20], (DEPTH, N_EXPERTS, D_EXPERT, D), D_EXPERT ** -0.5),
        'final_norm_g': 1.0 + nrm(ks[21], (D,), 0.05),
    }


def reference(x, c, rel_bias, hgrn_lb_raw, norm1_g, norm2_g, ada_w, ada_b, w_in, hgrn_norm_g,
              w_branch_a, w_branch_b, w_branch_c, w_out, router_group_w, router_group_b,
              router_expert_w, router_expert_b, expert_w_gate, expert_w_up, expert_w_down,
              final_norm_g):
    lb_soft = jax.nn.softmax(hgrn_lb_raw.astype(jnp.float32), axis=0)
    lb_all = jnp.cumsum(lb_soft, axis=0) - lb_soft[0:1]
    for l in range(DEPTH):
        x = hybrid_layer(x, c, lb_all[l], rel_bias, norm1_g[l], norm2_g[l], ada_w[l], ada_b[l],
                         w_in[l], hgrn_norm_g[l], w_branch_a[l], w_branch_b[l], w_branch_c[l],
                         w_out[l], router_group_w[l], router_group_b[l], router_expert_w[l],
                         router_expert_b[l], expert_w_gate[l], expert_w_up[l], expert_w_down[l])
    return rmsnorm(x, final_norm_g)
```

```python
import functools
import math

import jax
import jax.numpy as jnp
from jax import lax
from jax.experimental import pallas as pl
from jax.experimental.pallas import tpu as pltpu

F32 = jnp.float32
BF16 = jnp.bfloat16

D_MODEL = 1024
DEPTH = 4
A_HEADS = 8
A_HEAD_DIM = 128
IDX_HEADS = 8
IDX_DIM = 64
TOPK_MAX = 256
REL_BUCKETS = 32
REL_MAX_DIST = 128
B_HEADS = 8
B_HEAD_DIM = 128
C_HEADS = 4
C_QK_DIM = 256
C_V_DIM = 512
N_GROUPS = 4
EXPERTS_PER_GROUP = 8
N_EXPERTS = 32
D_EXPERT = 512
EPS = 1e-6

A_WIDTH = A_HEADS * A_HEAD_DIM
B_WIDTH = B_HEADS * B_HEAD_DIM
C_QK_WIDTH = C_HEADS * C_QK_DIM
C_V_WIDTH = C_HEADS * C_V_DIM
IN_SPLITS = (A_WIDTH, A_WIDTH, A_WIDTH, IDX_HEADS * IDX_DIM, IDX_DIM, IDX_HEADS,
             B_WIDTH, B_WIDTH, B_WIDTH, B_WIDTH,
             C_QK_WIDTH, C_QK_WIDTH, C_V_WIDTH, C_V_WIDTH,
             D_MODEL, D_MODEL, D_MODEL)

LANES = 128
VMEM_LIMIT = 56 * 1024 * 1024

P16_CV, P16_CG = 0, 2
P16_AQ, P16_AK, P16_AV, P16_BQ, P16_BI, P16_BG, P16_CQ, P16_CK, P16_GA, P16_GB, P16_GC = range(4, 15)
P32_BF = 0
P32_IQ = 1024
P32_IK = 2048
P32_IW = 2176
P32_WIDTH = 2304

DSA_TQ = 256
HGRN_L = 256
HGRN_C = 64
HGRN_SB = 16
RET_C = 128
KEY_NEG_INF = -2139095041
INT_MIN = -2147483648

NT_DIMS = (((1,), (1,)), ((), ()))
TN_DIMS = (((0,), (0,)), ((), ()))


def _cparams(sem):
    return pltpu.CompilerParams(dimension_semantics=sem, vmem_limit_bytes=VMEM_LIMIT)


def _silu(x):
    return x * jax.nn.sigmoid(x)


def _lb_kernel(raw_ref, o_ref):
    raw = raw_ref[...]
    m = jnp.max(raw, axis=0, keepdims=True)
    e = jnp.exp(raw - m)
    soft = e / jnp.sum(e, axis=0, keepdims=True)
    run = jnp.zeros_like(soft[0:1])
    for l in range(raw.shape[0]):
        run = run + soft[l:l + 1]
        o_ref[l:l + 1, :] = run - soft[0:1]


def _hgrn_lower_bounds(raw):
    return pl.pallas_call(
        _lb_kernel, out_shape=jax.ShapeDtypeStruct(raw.shape, F32), name="hgrn_lb")(raw)


def _ada_kernel(c_ref, w_ref, b_ref, o_ref):
    a = _silu(c_ref[...])
    o_ref[0] = jnp.dot(a, w_ref[0], precision=lax.Precision.HIGHEST,
                       preferred_element_type=F32) + b_ref[0]


def _ada_mod(c_pad, ada_w, ada_b):
    depth = ada_w.shape[0]
    rows = c_pad.shape[0]
    return pl.pallas_call(
        _ada_kernel,
        grid=(depth, 6),
        in_specs=[pl.BlockSpec((rows, D_MODEL), lambda l, j: (0, 0)),
                  pl.BlockSpec((1, D_MODEL, D_MODEL), lambda l, j: (l, 0, j)),
                  pl.BlockSpec((1, 1, D_MODEL), lambda l, j: (l, 0, j))],
        out_specs=pl.BlockSpec((1, rows, D_MODEL), lambda l, j: (l, 0, j)),
        out_shape=jax.ShapeDtypeStruct((depth, rows, 6 * D_MODEL), F32),
        compiler_params=_cparams(("arbitrary", "arbitrary")),
        name="ada_mod",
    )(c_pad, ada_w, ada_b.reshape(depth, 1, 6 * D_MODEL))


def _rms_mod(x, g, sc, sh):
    ms = jnp.mean(x * x, axis=-1, keepdims=True)
    return (x * lax.rsqrt(ms + EPS) * g) * (1.0 + sc) + sh


def _proj_kernel(x_ref, mod_ref, g_ref, w_ref, o_ref, h_ref, *, tiles_per_batch):
    i = pl.program_id(0)

    @pl.when(pl.program_id(1) == 0)
    def _():
        b = i // tiles_per_batch
        sh = mod_ref[pl.ds(b, 1), 0:D_MODEL]
        sc = mod_ref[pl.ds(b, 1), D_MODEL:2 * D_MODEL]
        h_ref[...] = _rms_mod(x_ref[...], g_ref[...], sc, sh).astype(BF16)

    o_ref[...] = jnp.dot(h_ref[...], w_ref[...], preferred_element_type=F32).astype(o_ref.dtype)


def _norm_project(x2, mod_l, g, w, out_dtype, tn, seq, name):
    n = x2.shape[0]
    width = w.shape[1]
    tm = min(1024, seq)
    return pl.pallas_call(
        functools.partial(_proj_kernel, tiles_per_batch=seq // tm),
        grid=(n // tm, width // tn),
        in_specs=[pl.BlockSpec((tm, D_MODEL), lambda i, j: (i, 0)),
                  pl.BlockSpec(mod_l.shape, lambda i, j: (0, 0)),
                  pl.BlockSpec((1, D_MODEL), lambda i, j: (0, 0)),
                  pl.BlockSpec((D_MODEL, tn), lambda i, j: (0, j))],
        out_specs=pl.BlockSpec((tm, tn), lambda i, j: (i, j)),
        out_shape=jax.ShapeDtypeStruct((n, width), out_dtype),
        scratch_shapes=[pltpu.VMEM((tm, D_MODEL), BF16)],
        compiler_params=_cparams(("arbitrary", "arbitrary")),
        name=name,
    )(x2, mod_l, g.reshape(1, D_MODEL), w)


def _dsa_kernel(q_ref, iq_ref, iw_ref, k_ref, v_ref, ik_ref, bias_ref, o_ref,
                key_ref, madd_ref, wb_ref, th_ref, *, tq, topk):
    qi = pl.program_id(1)
    nck = qi + 1
    neg_inf = -jnp.inf
    half = tq // 2
    idx_scale = (IDX_HEADS * IDX_DIM) ** -0.5

    iw = iw_ref[...] * idx_scale
    for h in range(IDX_HEADS):
        wb_ref[h] = jnp.broadcast_to(iw[:, h:h + 1], (tq, tq))

    row = lax.broadcasted_iota(jnp.int32, (tq, tq), 0)
    col = lax.broadcasted_iota(jnp.int32, (tq, tq), 1)

    def score_chunk(c, carry):
        off = pl.multiple_of(c * tq, tq)
        ikc = ik_ref[pl.ds(off, tq), :].astype(BF16)
        acc = jnp.zeros((tq, tq), F32)
        for h in range(IDX_HEADS):
            qh = iq_ref[:, h * LANES:(h + 1) * LANES].astype(BF16)
            s = lax.dot_general(qh, ikc, NT_DIMS, preferred_element_type=F32)
            acc = acc + jnp.maximum(s, 0.0) * wb_ref[h]
        acc = jnp.where(acc == 0.0, 0.0, acc)
        acc = jnp.where(col + (c - qi) * tq <= row, acc, neg_inf)
        kb = pltpu.bitcast(acc, jnp.int32)
        key_ref[c] = jnp.where(kb < 0, kb ^ jnp.int32(0x7FFFFFFF), kb)
        return carry

    lax.fori_loop(0, nck, score_chunk, 0)

    for rb in range(2):
        r0 = rb * half

        def bit_step(i, theta):
            cand = theta + jnp.left_shift(jnp.int32(1), 31 - i)
            cand_b = jnp.broadcast_to(cand, (half, LANES))

            def count_chunk(c, acc):
                for lh in range(tq // LANES):
                    kc = key_ref[c, r0:r0 + half, lh * LANES:(lh + 1) * LANES]
                    acc = acc + jnp.where(kc >= cand_b, 1.0, 0.0)
                return acc

            acc = lax.fori_loop(0, nck, count_chunk, jnp.zeros((half, LANES), F32))
            cnt = jnp.sum(acc, axis=-1, keepdims=True)
            return jnp.where(cnt >= topk, cand, theta)

        theta = lax.fori_loop(0, 32, bit_step, jnp.full((half, 1), INT_MIN, jnp.int32))
        theta = jnp.maximum(theta, KEY_NEG_INF + 1)
        th_ref[r0:r0 + half, :] = jnp.broadcast_to(theta, (half, LANES))

    def mask_chunk(c, accs):
        out = []
        for rb in range(2):
            r0 = rb * half
            thb = th_ref[r0:r0 + half, :]
            part = accs[rb]
            for lh in range(tq // LANES):
                kc = key_ref[c, r0:r0 + half, lh * LANES:(lh + 1) * LANES]
                ge = kc >= thb
                madd_ref[c, r0:r0 + half, lh * LANES:(lh + 1) * LANES] = jnp.where(ge, 0.0, neg_inf)
                part = part + jnp.where(ge, 1.0, 0.0)
            out.append(part)
        return tuple(out)

    zero_half = jnp.zeros((half, LANES), F32)
    cnt_lo, cnt_hi = lax.fori_loop(0, nck, mask_chunk, (zero_half, zero_half))
    cnt_ge = jnp.sum(jnp.concatenate([cnt_lo, cnt_hi], axis=0), axis=-1, keepdims=True)
    has_tie = jnp.max(cnt_ge) > topk

    @pl.when(has_tie)
    def _():
        thb2 = jnp.concatenate([th_ref[...], th_ref[...]], axis=1)

        def gt_chunk(c, acc):
            return acc + jnp.sum(jnp.where(key_ref[c] > thb2, 1.0, 0.0), axis=-1, keepdims=True)

        cnt_gt = lax.fori_loop(0, nck, gt_chunk, jnp.zeros((tq, 1), F32))
        need_eq = topk - cnt_gt
        incl = jnp.where(row <= col, 1.0, 0.0).astype(BF16)

        def tie_chunk(c, run):
            kc = key_ref[c]
            eq = kc == thb2
            eqf = jnp.where(eq, 1.0, 0.0)
            pref = jnp.dot(eqf.astype(BF16), incl, preferred_element_type=F32) + run
            eq_add = jnp.where(pref <= need_eq, 0.0, neg_inf)
            madd_ref[c] = jnp.where(eq, eq_add, jnp.where(kc > thb2, 0.0, neg_inf))
            return run + jnp.sum(eqf, axis=-1, keepdims=True)

        lax.fori_loop(0, nck, tie_chunk, jnp.zeros((tq, 1), F32))

    for h in range(A_HEADS):
        hs = slice(h * A_HEAD_DIM, (h + 1) * A_HEAD_DIM)
        qh = q_ref[:, hs]

        def logit_chunk(c, m):
            off = pl.multiple_of(c * tq, tq)
            kc = k_ref[pl.ds(off, tq), hs]
            s = lax.dot_general(qh, kc, NT_DIMS, preferred_element_type=F32)
            s = s + madd_ref[c] + bias_ref[h, jnp.minimum(qi - c, 2)]
            key_ref[c] = pltpu.bitcast(s, jnp.int32)
            return jnp.maximum(m, jnp.max(s, axis=-1, keepdims=True))

        m = lax.fori_loop(0, nck, logit_chunk, jnp.full((tq, 1), neg_inf, F32))

        def pv_chunk(c, carry):
            l, acc = carry
            off = pl.multiple_of(c * tq, tq)
            p = jnp.exp(pltpu.bitcast(key_ref[c], F32) - m)
            l = l + jnp.sum(p, axis=-1, keepdims=True)
            acc = acc + jnp.dot(p.astype(BF16), v_ref[pl.ds(off, tq), hs],
                                preferred_element_type=F32)
            return l, acc

        l, acc = lax.fori_loop(0, nck, pv_chunk,
                               (jnp.zeros((tq, 1), F32), jnp.zeros((tq, A_HEAD_DIM), F32)))
        o_ref[:, hs] = (acc / l).astype(o_ref.dtype)


def _dsa_attention(p16, p32, bias_tiles, bsz, seq):
    tq = min(DSA_TQ, seq)
    nq = seq // tq
    topk = min(TOPK_MAX, seq // 4)
    n = bsz * seq
    one = pl.Buffered(1)
    return pl.pallas_call(
        functools.partial(_dsa_kernel, tq=tq, topk=topk),
        grid=(bsz, nq),
        in_specs=[
            pl.BlockSpec((tq, A_WIDTH), lambda b, i: (b * nq + i, P16_AQ)),
            pl.BlockSpec((tq, 1024), lambda b, i: (b * nq + i, P32_IQ // 1024)),
            pl.BlockSpec((tq, LANES), lambda b, i: (b * nq + i, P32_IW // LANES)),
            pl.BlockSpec((seq, A_WIDTH), lambda b, i: (b, P16_AK), pipeline_mode=one),
            pl.BlockSpec((seq, A_WIDTH), lambda b, i: (b, P16_AV), pipeline_mode=one),
            pl.BlockSpec((seq, LANES), lambda b, i: (b, P32_IK // LANES), pipeline_mode=one),
            pl.BlockSpec(bias_tiles.shape, lambda b, i: (0, 0, 0, 0), pipeline_mode=one),
        ],
        out_specs=pl.BlockSpec((tq, A_WIDTH), lambda b, i: (b * nq + i, 0)),
        out_shape=jax.ShapeDtypeStruct((n, A_WIDTH), BF16),
        scratch_shapes=[pltpu.VMEM((nq, tq, tq), jnp.int32),
                        pltpu.VMEM((nq, tq, tq), F32),
                        pltpu.VMEM((IDX_HEADS, tq, tq), F32),
                        pltpu.VMEM((tq, LANES), jnp.int32)],
        compiler_params=_cparams(("arbitrary", "arbitrary")),
        name="dsa_attention",
    )(p16, p32, p32, p16, p16, p32, bias_tiles)


def _t5_bucket(rel):
    max_exact = REL_BUCKETS // 2
    relf = jnp.maximum(rel, 1).astype(F32)
    large = max_exact + (jnp.log(relf / max_exact) / math.log(REL_MAX_DIST / max_exact)
                         * (REL_BUCKETS - max_exact)).astype(jnp.int32)
    large = jnp.minimum(large, REL_BUCKETS - 1)
    return jnp.where(rel < max_exact, rel, large)


def _bias_tiles(rel_bias, tq):
    assert tq >= REL_MAX_DIST
    r = jnp.arange(tq, dtype=jnp.int32)[:, None]
    c = jnp.arange(tq, dtype=jnp.int32)[None, :]
    tiles = []
    for lag in range(2):
        dist = jnp.maximum(lag * tq + r - c, 0)
        tiles.append(rel_bias[_t5_bucket(dist)])
    tiles = jnp.stack(tiles, axis=0) - rel_bias[REL_BUCKETS - 1][None, None, None, :]
    tiles = jnp.concatenate([tiles, jnp.zeros_like(tiles[:1])], axis=0)
    return jnp.transpose(tiles, (3, 0, 1, 2)).astype(F32)


def _hgrn_kernel(q_ref, f_ref, i_ref, g_ref, lb_ref, ng_ref, tril_ref, o_ref, st_ref, *, rows):
    @pl.when(pl.program_id(2) == 0)
    def _():
        st_ref[...] = jnp.zeros_like(st_ref)

    lb = lb_ref[0]
    f = lb + (1.0 - lb) * jax.nn.sigmoid(f_ref[...])
    logf = jnp.log(f)
    kk = 1.0 - f
    g1 = logf.astype(BF16)
    r1 = logf - g1.astype(F32)
    g2 = r1.astype(BF16)
    g3 = (r1 - g2.astype(F32)).astype(BF16)
    tril = tril_ref[...]
    bcum = (jnp.dot(tril, g1, preferred_element_type=F32)
            + jnp.dot(tril, g2, preferred_element_type=F32)
            + jnp.dot(tril, g3, preferred_element_type=F32))

    srow = lax.broadcasted_iota(jnp.int32, (HGRN_C, B_HEAD_DIM), 0)
    trow = lax.broadcasted_iota(jnp.int32, (HGRN_SB, HGRN_C), 0)
    scol = lax.broadcasted_iota(jnp.int32, (HGRN_SB, HGRN_C), 1)
    ng = ng_ref[...]

    for n in range(rows // HGRN_C):
        cs = slice(n * HGRN_C, (n + 1) * HGRN_C)
        bc = bcum[cs]
        qc = q_ref[cs, :].astype(F32)
        kc = kk[cs]
        vc = i_ref[cs, :]
        st = st_ref[...]
        qb = (qc * jnp.exp(bc)).astype(BF16)
        o_inter = lax.dot_general(qb, st.astype(BF16), NT_DIMS, preferred_element_type=F32)
        a_rows = []
        for sb in range(HGRN_C // HGRN_SB):
            s0 = sb * HGRN_SB
            beta = bc[s0 - 1:s0] if sb > 0 else jnp.zeros((1, B_HEAD_DIM), F32)
            qs = (qc[s0:s0 + HGRN_SB] * jnp.exp(bc[s0:s0 + HGRN_SB] - beta)).astype(BF16)
            expo = jnp.where(srow < s0 + HGRN_SB, beta - bc, -jnp.inf)
            ks = (kc * jnp.exp(expo)).astype(BF16)
            a = lax.dot_general(qs, ks, NT_DIMS, preferred_element_type=F32)
            a_rows.append(jnp.where(scol <= trow + s0, a, 0.0))
        attn = jnp.concatenate(a_rows, axis=0).astype(BF16)
        o = o_inter + jnp.dot(attn, vc, preferred_element_type=F32)
        blast = bc[HGRN_C - 1:HGRN_C]
        kdec = (kc * jnp.exp(blast - bc)).astype(BF16)
        st_ref[...] = st * jnp.exp(blast) + lax.dot_general(vc, kdec, TN_DIMS,
                                                            preferred_element_type=F32)
        ms = jnp.mean(o * o, axis=-1, keepdims=True)
        on = o * lax.rsqrt(ms + EPS) * ng
        o_ref[cs, :] = (on * _silu(g_ref[cs, :].astype(F32))).astype(o_ref.dtype)


def _hgrn2(p16, p32, lb_l, norm_g, bsz, seq):
    rows = min(HGRN_L, seq)
    nj = seq // rows
    n = bsz * seq
    r = jnp.arange(rows, dtype=jnp.int32)
    tril = ((r[:, None] >= r[None, :]) & (r[:, None] // HGRN_C == r[None, :] // HGRN_C)).astype(BF16)
    hb = B_WIDTH // LANES

    def col(base):
        return lambda b, h, j: (b * nj + j, base * hb + h)

    return pl.pallas_call(
        functools.partial(_hgrn_kernel, rows=rows),
        grid=(bsz, B_HEADS, nj),
        in_specs=[
            pl.BlockSpec((rows, LANES), col(P16_BQ)),
            pl.BlockSpec((rows, LANES), lambda b, h, j: (b * nj + j, P32_BF // LANES + h)),
            pl.BlockSpec((rows, LANES), col(P16_BI)),
            pl.BlockSpec((rows, LANES), col(P16_BG)),
            pl.BlockSpec((1, 1, LANES), lambda b, h, j: (h, 0, 0)),
            pl.BlockSpec((1, LANES), lambda b, h, j: (0, 0)),
            pl.BlockSpec((rows, rows), lambda b, h, j: (0, 0)),
        ],
        out_specs=pl.BlockSpec((rows, LANES), lambda b, h, j: (b * nj + j, h)),
        out_shape=jax.ShapeDtypeStruct((n, B_WIDTH), BF16),
        scratch_shapes=[pltpu.VMEM((B_HEAD_DIM, B_HEAD_DIM), F32)],
        compiler_params=_cparams(("arbitrary", "arbitrary", "arbitrary")),
        name="hgrn2",
    )(p16, p32, p16, p16, lb_l.reshape(B_HEADS, 1, B_HEAD_DIM), norm_g.reshape(1, B_HEAD_DIM), tril)


def _ret_kernel(q_ref, k_ref, v_ref, g_ref, cos_ref, sin_ref, idec_ref, qdec_ref, kdec_ref, cdec_ref,
                o_ref, st_ref):
    @pl.when(pl.program_id(2) == 0)
    def _():
        st_ref[...] = jnp.zeros_like(st_ref)

    cos = cos_ref[...]
    sin = sin_ref[...]
    hd = C_QK_DIM // 2

    def rot(a):
        ae, ao = a[:, :hd], a[:, hd:]
        return jnp.concatenate([ae * cos - ao * sin, ao * cos + ae * sin], axis=1)

    qr = rot(q_ref[...].astype(F32))
    kr = rot(k_ref[...].astype(F32))
    v = v_ref[...]
    st = st_ref[...]
    attn = lax.dot_general(qr.astype(BF16), kr.astype(BF16), NT_DIMS,
                           preferred_element_type=F32) * idec_ref[0]
    o = (jnp.dot(attn.astype(BF16), v, preferred_element_type=F32)
         + jnp.dot((qr * qdec_ref[0]).astype(BF16), st.astype(BF16), preferred_element_type=F32))
    krd_t = (kr * kdec_ref[0]).T.astype(BF16)
    st_ref[...] = cdec_ref[0, 0:1, :] * st + jnp.dot(krd_t, v, preferred_element_type=F32)
    ms = jnp.mean(o * o, axis=-1, keepdims=True)
    o_ref[...] = (_silu(g_ref[...].astype(F32)) * (o * lax.rsqrt(ms + EPS))).astype(o_ref.dtype)


def _retention_tables(seq):
    pos = jnp.arange(seq, dtype=F32)
    theta = 1.0 / (10000.0 ** jnp.linspace(0.0, 1.0, C_QK_DIM // 2))
    ang = pos[:, None] * theta[None, :]
    log_gamma = jnp.log(1.0 - 2.0 ** (-5.0 - jnp.arange(C_HEADS, dtype=F32)))
    idx = jnp.arange(RET_C, dtype=F32)
    causal = idx[:, None] >= idx[None, :]
    idec = jnp.exp(jnp.where(causal[None], (idx[:, None] - idx[None, :])[None] * log_gamma[:, None, None],
                             -jnp.inf))
    qdec = jnp.exp((idx + 1.0)[None, :] * log_gamma[:, None])[..., None]
    kdec = jnp.exp((RET_C - 1.0 - idx)[None, :] * log_gamma[:, None])[..., None]
    cdec = jnp.exp(RET_C * log_gamma)[:, None, None]
    return (jnp.cos(ang), jnp.sin(ang), idec,
            jnp.broadcast_to(qdec, (C_HEADS, RET_C, C_QK_DIM)),
            jnp.broadcast_to(kdec, (C_HEADS, RET_C, C_QK_DIM)),
            jnp.broadcast_to(cdec, (C_HEADS, 8, C_V_DIM)))


def _retention(p16, tables, bsz, seq):
    cos, sin, idec, qdec, kdec, cdec = tables
    nj = seq // RET_C
    n = bsz * seq
    qk_b = 1024 // C_QK_DIM
    v_b = 1024 // C_V_DIM
    return pl.pallas_call(
        _ret_kernel,
        grid=(bsz, C_HEADS, nj),
        in_specs=[
            pl.BlockSpec((RET_C, C_QK_DIM), lambda b, h, j: (b * nj + j, P16_CQ * qk_b + h)),
            pl.BlockSpec((RET_C, C_QK_DIM), lambda b, h, j: (b * nj + j, P16_CK * qk_b + h)),
            pl.BlockSpec((RET_C, C_V_DIM), lambda b, h, j: (b * nj + j, P16_CV * v_b + h)),
            pl.BlockSpec((RET_C, C_V_DIM), lambda b, h, j: (b * nj + j, P16_CG * v_b + h)),
            pl.BlockSpec((RET_C, C_QK_DIM // 2), lambda b, h, j: (j, 0)),
            pl.BlockSpec((RET_C, C_QK_DIM // 2), lambda b, h, j: (j, 0)),
            pl.BlockSpec((1, RET_C, RET_C), lambda b, h, j: (h, 0, 0)),
            pl.BlockSpec((1, RET_C, C_QK_DIM), lambda b, h, j: (h, 0, 0)),
            pl.BlockSpec((1, RET_C, C_QK_DIM), lambda b, h, j: (h, 0, 0)),
            pl.BlockSpec((1, 8, C_V_DIM), lambda b, h, j: (h, 0, 0)),
        ],
        out_specs=pl.BlockSpec((RET_C, C_V_DIM), lambda b, h, j: (b * nj + j, h)),
        out_shape=jax.ShapeDtypeStruct((n, C_V_WIDTH), BF16),
        scratch_shapes=[pltpu.VMEM((C_QK_DIM, C_V_DIM), F32)],
        compiler_params=_cparams(("arbitrary", "arbitrary", "arbitrary")),
        name="retention",
    )(p16, p16, p16, p16, cos, sin, idec, qdec, kdec, cdec)


def _merge_kernel(oa_ref, ob_ref, oc_ref, ga_ref, gb_ref, gc_ref, x_ref, mod_ref, g2_ref,
                  wa_ref, wb_ref, wc_ref, wo_ref, wrh_ref, wrl_ref, br_ref,
                  x1_ref, h2_ref, gate_ref, *, tiles_per_batch):
    b = pl.program_id(0) // tiles_per_batch

    def gated(o_ref, w_ref, g_ref):
        y = jnp.dot(o_ref[...], w_ref[...], preferred_element_type=F32)
        return jax.nn.sigmoid(g_ref[...].astype(F32)) * y

    merged = gated(oa_ref, wa_ref, ga_ref) + gated(ob_ref, wb_ref, gb_ref) + gated(oc_ref, wc_ref, gc_ref)
    y = jnp.dot(merged.astype(BF16), wo_ref[...], preferred_element_type=F32)
    gt1 = mod_ref[pl.ds(b, 1), 2 * D_MODEL:3 * D_MODEL]
    x1 = x_ref[...] + gt1 * y
    x1_ref[...] = x1
    sh2 = mod_ref[pl.ds(b, 1), 3 * D_MODEL:4 * D_MODEL]
    sc2 = mod_ref[pl.ds(b, 1), 4 * D_MODEL:5 * D_MODEL]
    h2 = _rms_mod(x1, g2_ref[...], sc2, sh2)
    h_hi = h2.astype(BF16)
    h2_ref[...] = h_hi
    h_lo = (h2 - h_hi.astype(F32)).astype(BF16)
    logits = (jnp.dot(h_hi, wrh_ref[...], preferred_element_type=F32)
              + jnp.dot(h_lo, wrh_ref[...], preferred_element_type=F32)
              + jnp.dot(h_hi, wrl_ref[...], preferred_element_type=F32)) + br_ref[...]
    lane = lax.broadcasted_iota(jnp.int32, logits.shape, 1).astype(F32)
    neg_inf = -jnp.inf

    def first_argmax(vals):
        top = jnp.max(vals, axis=-1, keepdims=True)
        idx = jnp.min(jnp.where(vals == top, lane, float(LANES)), axis=-1, keepdims=True)
        return top, idx

    gl = jnp.where(lane < N_GROUPS, logits, neg_inf)
    gmax, gsel = first_argmax(gl)
    gprob = 1.0 / jnp.sum(jnp.exp(gl - gmax), axis=-1, keepdims=True)
    lo = N_GROUPS + EXPERTS_PER_GROUP * gsel
    el = jnp.where((lane >= lo) & (lane < lo + EXPERTS_PER_GROUP), logits, neg_inf)
    v1, i1 = first_argmax(el)
    el2 = jnp.where(lane == i1, neg_inf, el)
    v2, i2 = first_argmax(el2)
    e2 = jnp.exp(v2 - v1)
    den = 1.0 + e2
    gate_ref[...] = jnp.where(lane == i1, gprob / den, jnp.where(lane == i2, gprob * (e2 / den), 0.0))


def _merge(o_a, o_b, o_c, p16, x2, mod_l, g2, wa, wb, wc, wo, wr_hi, wr_lo, br, seq):
    n = x2.shape[0]
    tm = min(512, seq)
    one = pl.Buffered(1)

    def rows(width, cb=0):
        return pl.BlockSpec((tm, width), lambda i: (i, cb))

    def whole(a):
        return pl.BlockSpec(a.shape, lambda i: (0,) * a.ndim, pipeline_mode=one)

    return pl.pallas_call(
        functools.partial(_merge_kernel, tiles_per_batch=seq // tm),
        grid=(n // tm,),
        in_specs=[rows(A_WIDTH), rows(B_WIDTH), rows(C_V_WIDTH),
                  rows(D_MODEL, P16_GA), rows(D_MODEL, P16_GB), rows(D_MODEL, P16_GC),
                  rows(D_MODEL), whole(mod_l), pl.BlockSpec((1, D_MODEL), lambda i: (0, 0)),
                  whole(wa), whole(wb), whole(wc), whole(wo), whole(wr_hi), whole(wr_lo), whole(br)],
        out_specs=[rows(D_MODEL), rows(D_MODEL), rows(LANES)],
        out_shape=[jax.ShapeDtypeStruct((n, D_MODEL), F32),
                   jax.ShapeDtypeStruct((n, D_MODEL), BF16),
                   jax.ShapeDtypeStruct((n, LANES), F32)],
        compiler_params=_cparams(("arbitrary",)),
        name="merge_route",
    )(o_a, o_b, o_c, p16, p16, p16, x2, mod_l, g2.reshape(1, D_MODEL), wa, wb, wc, wo, wr_hi, wr_lo, br)


def _moe_kernel(h_ref, gate_ref, x1_ref, mod_ref, wg_ref, wu_ref, wd_ref, o_ref, acc_ref, *,
                tiles_per_batch):
    e = pl.program_id(1)

    @pl.when(e == 0)
    def _():
        acc_ref[...] = jnp.zeros_like(acc_ref)

    h = h_ref[...]
    a = jnp.dot(h, wg_ref[0].astype(BF16), preferred_element_type=F32)
    u = jnp.dot(h, wu_ref[0].astype(BF16), preferred_element_type=F32)
    lane = lax.broadcasted_iota(jnp.int32, gate_ref.shape, 1)
    gcol = jnp.sum(jnp.where(lane == e + N_GROUPS, gate_ref[...], 0.0), axis=-1, keepdims=True)
    hm = (_silu(a) * u * gcol).astype(BF16)
    acc_ref[...] += jnp.dot(hm, wd_ref[0].astype(BF16), preferred_element_type=F32)

    @pl.when(e == N_EXPERTS - 1)
    def _():
        b = pl.program_id(0) // tiles_per_batch
        gt2 = mod_ref[pl.ds(b, 1), 5 * D_MODEL:6 * D_MODEL]
        o_ref[...] = x1_ref[...] + gt2 * acc_ref[...]


def _moe(h2, gates, x1, mod_l, wg, wu, wd, seq):
    n = h2.shape[0]
    tm = min(1024, seq)
    return pl.pallas_call(
        functools.partial(_moe_kernel, tiles_per_batch=seq // tm),
        grid=(n // tm, N_EXPERTS),
        in_specs=[pl.BlockSpec((tm, D_MODEL), lambda i, e: (i, 0)),
                  pl.BlockSpec((tm, LANES), lambda i, e: (i, 0)),
                  pl.BlockSpec((tm, D_MODEL), lambda i, e: (i, 0), pipeline_mode=pl.Buffered(1)),
                  pl.BlockSpec(mod_l.shape, lambda i, e: (0, 0)),
                  pl.BlockSpec((1, D_MODEL, D_EXPERT), lambda i, e: (e, 0, 0)),
                  pl.BlockSpec((1, D_MODEL, D_EXPERT), lambda i, e: (e, 0, 0)),
                  pl.BlockSpec((1, D_EXPERT, D_MODEL), lambda i, e: (e, 0, 0))],
        out_specs=pl.BlockSpec((tm, D_MODEL), lambda i, e: (i, 0)),
        out_shape=jax.ShapeDtypeStruct((n, D_MODEL), F32),
        scratch_shapes=[pltpu.VMEM((tm, D_MODEL), F32)],
        compiler_params=_cparams(("arbitrary", "arbitrary")),
        name="experts",
    )(h2, gates, x1, mod_l, wg, wu, wd)


def _final_norm_kernel(x_ref, g_ref, o_ref):
    x = x_ref[...]
    ms = jnp.mean(x * x, axis=-1, keepdims=True)
    o_ref[...] = x * lax.rsqrt(ms + EPS) * g_ref[...]


def _final_norm(x2, g, seq):
    n = x2.shape[0]
    tm = min(1024, seq)
    return pl.pallas_call(
        _final_norm_kernel,
        grid=(n // tm,),
        in_specs=[pl.BlockSpec((tm, D_MODEL), lambda i: (i, 0)),
                  pl.BlockSpec((1, D_MODEL), lambda i: (0, 0))],
        out_specs=pl.BlockSpec((tm, D_MODEL), lambda i: (i, 0)),
        out_shape=jax.ShapeDtypeStruct((n, D_MODEL), F32),
        compiler_params=_cparams(("arbitrary",)),
        name="final_norm",
    )(x2, g.reshape(1, D_MODEL))


def _pack_w_in(w_in_l):
    offs = [0]
    for s in IN_SPLITS:
        offs.append(offs[-1] + s)
    (aq, ak, av, iq, ik, iw, bq, bf, bi, bg, cq, ck, cv, cg, ga, gb, gc) = [
        w_in_l[:, offs[i]:offs[i + 1]] for i in range(len(IN_SPLITS))]

    def even_odd(w):
        w4 = w.reshape(D_MODEL, C_HEADS, C_QK_DIM // 2, 2)
        return jnp.concatenate([w4[..., 0], w4[..., 1]], axis=-1).reshape(D_MODEL, C_QK_WIDTH)

    w16 = jnp.concatenate(
        [cv, cg, aq * (A_HEAD_DIM ** -0.5), ak, av, bq, bi, bg, even_odd(cq),
         even_odd(ck) * (C_QK_DIM ** -0.5), ga, gb, gc], axis=1).astype(BF16)
    iq_p = jnp.pad(iq.reshape(D_MODEL, IDX_HEADS, IDX_DIM),
                   ((0, 0), (0, 0), (0, LANES - IDX_DIM))).reshape(D_MODEL, IDX_HEADS * LANES)
    w32 = jnp.concatenate(
        [bf, iq_p, jnp.pad(ik, ((0, 0), (0, LANES - IDX_DIM))),
         jnp.pad(iw, ((0, 0), (0, LANES - IDX_HEADS)))], axis=1).astype(BF16)
    return w16, w32


def _split_bf16(w):
    hi = w.astype(BF16)
    return hi, (w - hi.astype(F32)).astype(BF16)


def kernel(x, c, rel_bias, hgrn_lb_raw, norm1_g, norm2_g, ada_w, ada_b, w_in, hgrn_norm_g, w_branch_a,
           w_branch_b, w_branch_c, w_out, router_group_w, router_group_b, router_expert_w,
           router_expert_b, expert_w_gate, expert_w_up, expert_w_down, final_norm_g):
    bsz, seq, _ = x.shape
    depth = w_in.shape[0]
    n = bsz * seq
    x2 = x.reshape(n, D_MODEL)

    lb_all = _hgrn_lower_bounds(hgrn_lb_raw)
    c_pad = jnp.pad(c, ((0, (-bsz) % 8), (0, 0)))
    mod = _ada_mod(c_pad, ada_w, ada_b)
    bias_tiles = _bias_tiles(rel_bias, min(DSA_TQ, seq))
    ret_tables = _retention_tables(seq)

    for l in range(depth):
        w16, w32 = _pack_w_in(w_in[l])
        p16 = _norm_project(x2, mod[l], norm1_g[l], w16, BF16, 1024, seq, "proj_bf16")
        p32 = _norm_project(x2, mod[l], norm1_g[l], w32, F32, 768, seq, "proj_f32")
        o_a = _dsa_attention(p16, p32, bias_tiles, bsz, seq)
        o_b = _hgrn2(p16, p32, lb_all[l], hgrn_norm_g[l], bsz, seq)
        o_c = _retention(p16, ret_tables, bsz, seq)
        wr = jnp.concatenate([router_group_w[l], router_expert_w[l],
                              jnp.zeros((D_MODEL, LANES - N_GROUPS - N_EXPERTS), F32)], axis=1)
        br = jnp.concatenate([router_group_b[l], router_expert_b[l],
                              jnp.zeros((LANES - N_GROUPS - N_EXPERTS,), F32)]).reshape(1, LANES)
        wr_hi, wr_lo = _split_bf16(wr)
        x1, h2, gates = _merge(o_a, o_b, o_c, p16, x2, mod[l], norm2_g[l],
                               w_branch_a[l].astype(BF16), w_branch_b[l].astype(BF16),
                               w_branch_c[l].astype(BF16), w_out[l].astype(BF16),
                               wr_hi, wr_lo, br, seq)
        x2 = _moe(h2, gates, x1, mod[l], expert_w_gate[l], expert_w_up[l], expert_w_down[l], seq)

    return _final_norm(x2, final_norm_g, seq).reshape(bsz, seq, D_MODEL)
```

```python
import functools
import math

import jax
import jax.numpy as jnp
from jax import lax
from jax.experimental import pallas as pl
from jax.experimental.pallas import tpu as pltpu

F32 = jnp.float32
BF16 = jnp.bfloat16

D_MODEL = 1024
A_HEADS = 8
A_HEAD_DIM = 128
IDX_HEADS = 8
IDX_DIM = 64
TOPK_MAX = 256
REL_BUCKETS = 32
REL_MAX_DIST = 128
B_HEADS = 8
B_HEAD_DIM = 128
C_HEADS = 4
C_QK_DIM = 256
C_V_DIM = 512
N_GROUPS = 4
EXPERTS_PER_GROUP = 8
N_EXPERTS = 32
D_EXPERT = 512
EPS = 1e-6

A_WIDTH = A_HEADS * A_HEAD_DIM
B_WIDTH = B_HEADS * B_HEAD_DIM
C_QK_WIDTH = C_HEADS * C_QK_DIM
C_V_WIDTH = C_HEADS * C_V_DIM
IN_SPLITS = (A_WIDTH, A_WIDTH, A_WIDTH, IDX_HEADS * IDX_DIM, IDX_DIM, IDX_HEADS,
             B_WIDTH, B_WIDTH, B_WIDTH, B_WIDTH,
             C_QK_WIDTH, C_QK_WIDTH, C_V_WIDTH, C_V_WIDTH,
             D_MODEL, D_MODEL, D_MODEL)

LANES = 128
VMEM_LIMIT = 56 * 1024 * 1024

P16_CV, P16_CG = 0, 2
P16_AQ, P16_AK, P16_BQ, P16_BI, P16_BG, P16_CQ, P16_CK, P16_GA, P16_GB, P16_GC = range(4, 14)
P32_BF = 0
P32_IQ = 1024
P32_IK = 2048
P32_IW = 2176

DSA_TQ = 256
HGRN_L = 256
HGRN_C = 64
HGRN_SB = 16
RET_C = 128
KEY_NEG_INF = -2139095041
INT_MIN = -2147483648
MASK_NEG = -1e30
COUNT_CHAINS = 4

NT_DIMS = (((1,), (1,)), ((), ()))
TN_DIMS = (((0,), (0,)), ((), ()))


def _cparams(sem):
    return pltpu.CompilerParams(dimension_semantics=sem, vmem_limit_bytes=VMEM_LIMIT)


def _silu(x):
    return x * jax.nn.sigmoid(x)


def _lb_kernel(raw_ref, o_ref):
    raw = raw_ref[...]
    m = jnp.max(raw, axis=0, keepdims=True)
    e = jnp.exp(raw - m)
    soft = e / jnp.sum(e, axis=0, keepdims=True)
    run = jnp.zeros_like(soft[0:1])
    for l in range(raw.shape[0]):
        run = run + soft[l:l + 1]
        o_ref[l:l + 1, :] = run - soft[0:1]


def _hgrn_lower_bounds(raw):
    return pl.pallas_call(
        _lb_kernel, out_shape=jax.ShapeDtypeStruct(raw.shape, F32), name="hgrn_lb")(raw)


def _ada_kernel(c_ref, w_ref, b_ref, o_ref):
    a = _silu(c_ref[...])
    o_ref[0] = jnp.dot(a, w_ref[0], precision=lax.Precision.HIGHEST,
                       preferred_element_type=F32) + b_ref[0]


def _ada_mod(c_pad, ada_w, ada_b):
    depth = ada_w.shape[0]
    rows = c_pad.shape[0]
    return pl.pallas_call(
        _ada_kernel,
        grid=(depth, 6),
        in_specs=[pl.BlockSpec((rows, D_MODEL), lambda l, j: (0, 0)),
                  pl.BlockSpec((1, D_MODEL, D_MODEL), lambda l, j: (l, 0, j)),
                  pl.BlockSpec((1, 1, D_MODEL), lambda l, j: (l, 0, j))],
        out_specs=pl.BlockSpec((1, rows, D_MODEL), lambda l, j: (l, 0, j)),
        out_shape=jax.ShapeDtypeStruct((depth, rows, 6 * D_MODEL), F32),
        compiler_params=_cparams(("arbitrary", "arbitrary")),
        name="ada_mod",
    )(c_pad, ada_w, ada_b.reshape(depth, 1, 6 * D_MODEL))


def _rms_mod(x, g, sc, sh):
    ms = jnp.mean(x * x, axis=-1, keepdims=True)
    return (x * lax.rsqrt(ms + EPS) * g) * (1.0 + sc) + sh


def _norm1(x_ref, mod_ref, g_ref, b):
    sh = mod_ref[pl.ds(b, 1), 0:D_MODEL]
    sc = mod_ref[pl.ds(b, 1), D_MODEL:2 * D_MODEL]
    return _rms_mod(x_ref[...], g_ref[...], sc, sh).astype(BF16)


def _proj_kernel(x_ref, mod_ref, g_ref, w_ref, o_ref, h_ref, *, tiles_per_batch):
    @pl.when(pl.program_id(1) == 0)
    def _():
        h_ref[...] = _norm1(x_ref, mod_ref, g_ref, pl.program_id(0) // tiles_per_batch)

    o_ref[...] = jnp.dot(h_ref[...], w_ref[...], preferred_element_type=F32).astype(o_ref.dtype)


def _norm_project(x2, mod_l, g, w, out_dtype, tn, seq, name):
    n = x2.shape[0]
    width = w.shape[1]
    tm = min(1024, seq)
    return pl.pallas_call(
        functools.partial(_proj_kernel, tiles_per_batch=seq // tm),
        grid=(n // tm, width // tn),
        in_specs=[pl.BlockSpec((tm, D_MODEL), lambda i, j: (i, 0)),
                  pl.BlockSpec(mod_l.shape, lambda i, j: (0, 0)),
                  pl.BlockSpec((1, D_MODEL), lambda i, j: (0, 0)),
                  pl.BlockSpec((D_MODEL, tn), lambda i, j: (0, j))],
        out_specs=pl.BlockSpec((tm, tn), lambda i, j: (i, j)),
        out_shape=jax.ShapeDtypeStruct((n, width), out_dtype),
        scratch_shapes=[pltpu.VMEM((tm, D_MODEL), BF16)],
        compiler_params=_cparams(("arbitrary", "arbitrary")),
        name=name,
    )(x2, mod_l, g.reshape(1, D_MODEL), w)


def _proj_t_kernel(x_ref, mod_ref, g_ref, wt_ref, o_ref, *, tiles_per_batch, chunk):
    h = _norm1(x_ref, mod_ref, g_ref, pl.program_id(0) // tiles_per_batch)
    res = lax.dot_general(wt_ref[...], h, NT_DIMS, preferred_element_type=F32)
    for ci in range(o_ref.shape[0]):
        o_ref[ci] = res[:, ci * chunk:(ci + 1) * chunk].astype(o_ref.dtype)


def _norm_project_t(x2, mod_l, g, wt, chunk, seq, name):
    n = x2.shape[0]
    cols = wt.shape[0]
    tm = min(1024, seq)
    return pl.pallas_call(
        functools.partial(_proj_t_kernel, tiles_per_batch=seq // tm, chunk=chunk),
        grid=(n // tm,),
        in_specs=[pl.BlockSpec((tm, D_MODEL), lambda i: (i, 0)),
                  pl.BlockSpec(mod_l.shape, lambda i: (0, 0)),
                  pl.BlockSpec((1, D_MODEL), lambda i: (0, 0)),
                  pl.BlockSpec((cols, D_MODEL), lambda i: (0, 0))],
        out_specs=pl.BlockSpec((tm // chunk, cols, chunk), lambda i: (i, 0, 0)),
        out_shape=jax.ShapeDtypeStruct((n // chunk, cols, chunk), BF16),
        compiler_params=_cparams(("arbitrary",)),
        name=name,
    )(x2, mod_l, g.reshape(1, D_MODEL), wt)


def _dsa_kernel(q_ref, iq_ref, iw_ref, k_ref, vt_ref, ik_ref, bias_ref, o_ref,
                key_ref, madd_ref, qt_ref, iqt_ref, iwt_ref, m_ref, l_ref, acc_ref, s_ref, *, tq, topk):
    qi = pl.program_id(1)
    nck = qi + 1
    idx_scale = (IDX_HEADS * IDX_DIM) ** -0.5

    for h in range(A_HEADS):
        hs = slice(h * LANES, (h + 1) * LANES)
        qt_ref[hs, :] = q_ref[:, hs].astype(F32).T.astype(BF16)
        iqt_ref[hs, :] = iq_ref[:, hs].T.astype(BF16)
    iwt_ref[...] = (iw_ref[...] * idx_scale).T

    krow = lax.broadcasted_iota(jnp.int32, (tq, tq), 0)
    qcol = lax.broadcasted_iota(jnp.int32, (tq, tq), 1)

    def score_chunk(c, carry):
        off = pl.multiple_of(c * tq, tq)
        ikc = ik_ref[pl.ds(off, tq), :].astype(BF16)
        acc = jnp.zeros((tq, tq), F32)
        for h in range(IDX_HEADS):
            s = jnp.dot(ikc, iqt_ref[h * LANES:(h + 1) * LANES, :], preferred_element_type=F32)
            acc = acc + jnp.maximum(s, 0.0) * iwt_ref[h:h + 1, :]
        acc = jnp.where(acc == 0.0, 0.0, acc)
        acc = jnp.where(krow + (c - qi) * tq <= qcol, acc, -jnp.inf)
        kb = pltpu.bitcast(acc, jnp.int32)
        key_ref[c] = jnp.where(kb < 0, kb ^ jnp.int32(0x7FFFFFFF), kb)
        return carry

    lax.fori_loop(0, nck, score_chunk, 0)

    def count(pred_fn):
        def body(c, parts):
            hit = jnp.where(pred_fn(key_ref[c]), 1.0, 0.0)
            parts = list(parts)
            for r in range(tq // 8):
                parts[r % COUNT_CHAINS] = parts[r % COUNT_CHAINS] + hit[r * 8:(r + 1) * 8, :]
            return tuple(parts)

        parts = lax.fori_loop(0, nck, body, (jnp.zeros((8, tq), F32),) * COUNT_CHAINS)
        return jnp.sum(sum(parts), axis=0, keepdims=True)

    def bit_step(i, theta):
        cand = theta + jnp.left_shift(jnp.int32(1), 31 - i)
        return jnp.where(count(lambda kc: kc >= cand) >= topk, cand, theta)

    theta = lax.fori_loop(0, 32, bit_step, jnp.full((1, tq), INT_MIN, jnp.int32))
    theta = jnp.maximum(theta, KEY_NEG_INF + 1)

    def mask_chunk(c, cnt):
        ge = key_ref[c] >= theta
        madd_ref[c] = jnp.where(ge, 0.0, MASK_NEG)
        return cnt + jnp.sum(jnp.where(ge, 1.0, 0.0), axis=0, keepdims=True)

    cnt_ge = lax.fori_loop(0, nck, mask_chunk, jnp.zeros((1, tq), F32))

    @pl.when(jnp.max(cnt_ge) > topk)
    def _():
        need_eq = topk - count(lambda kc: kc > theta)
        incl = jnp.where(krow >= qcol, 1.0, 0.0).astype(BF16)

        def tie_chunk(c, run):
            kc = key_ref[c]
            eq = kc == theta
            eqf = jnp.where(eq, 1.0, 0.0)
            pref = jnp.dot(incl, eqf.astype(BF16), preferred_element_type=F32) + run
            eq_add = jnp.where(pref <= need_eq, 0.0, MASK_NEG)
            madd_ref[c] = jnp.where(eq, eq_add, jnp.where(kc > theta, 0.0, MASK_NEG))
            return run + jnp.sum(eqf, axis=0, keepdims=True)

        lax.fori_loop(0, nck, tie_chunk, jnp.zeros((1, tq), F32))

    m_ref[...] = jnp.full(m_ref.shape, -jnp.inf, F32)
    l_ref[...] = jnp.zeros(l_ref.shape, F32)
    acc_ref[...] = jnp.zeros(acc_ref.shape, F32)

    def attend(c, lag):
        off = pl.multiple_of(c * tq, tq)
        head_slices = [slice(h * A_HEAD_DIM, (h + 1) * A_HEAD_DIM) for h in range(A_HEADS)]
        for h, hs in enumerate(head_slices):
            s_ref[h] = jnp.dot(k_ref[pl.ds(off, tq), hs], qt_ref[hs, :], preferred_element_type=F32)
        for h, hs in enumerate(head_slices):
            s = s_ref[h] + madd_ref[c]
            if lag is not None:
                s = s + bias_ref[h, lag]
            m_old = m_ref[h]
            m_new = jnp.maximum(m_old, jnp.max(s, axis=0, keepdims=True))
            alpha = jnp.exp(m_old - m_new)
            p = jnp.exp(s - m_new)
            l_ref[h] = alpha * l_ref[h] + jnp.sum(p, axis=0, keepdims=True)
            acc_ref[h] = alpha * acc_ref[h] + jnp.dot(vt_ref[c, hs, :], p.astype(BF16),
                                                      preferred_element_type=F32)
            m_ref[h] = m_new

    def far_chunk(c, carry):
        attend(c, None)
        return carry

    lax.fori_loop(0, jnp.maximum(qi - 1, 0), far_chunk, 0)

    @pl.when(qi >= 1)
    def _():
        attend(qi - 1, 1)

    attend(qi, 0)

    for h in range(A_HEADS):
        o = acc_ref[h] * (1.0 / l_ref[h])
        o_ref[:, h * A_HEAD_DIM:(h + 1) * A_HEAD_DIM] = o.T.astype(o_ref.dtype)


def _dsa_attention(p16, p32, vt, bias_tiles, bsz, seq):
    tq = min(DSA_TQ, seq)
    nq = seq // tq
    topk = min(TOPK_MAX, seq // 4)
    n = bsz * seq
    one = pl.Buffered(1)
    return pl.pallas_call(
        functools.partial(_dsa_kernel, tq=tq, topk=topk),
        grid=(bsz, nq),
        in_specs=[
            pl.BlockSpec((tq, A_WIDTH), lambda b, i: (b * nq + i, P16_AQ)),
            pl.BlockSpec((tq, 1024), lambda b, i: (b * nq + i, P32_IQ // 1024)),
            pl.BlockSpec((tq, LANES), lambda b, i: (b * nq + i, P32_IW // LANES)),
            pl.BlockSpec((seq, A_WIDTH), lambda b, i: (b, P16_AK), pipeline_mode=one),
            pl.BlockSpec((nq, A_WIDTH, tq), lambda b, i: (b, 0, 0), pipeline_mode=one),
            pl.BlockSpec((seq, LANES), lambda b, i: (b, P32_IK // LANES), pipeline_mode=one),
            pl.BlockSpec(bias_tiles.shape, lambda b, i: (0, 0, 0, 0), pipeline_mode=one),
        ],
        out_specs=pl.BlockSpec((tq, A_WIDTH), lambda b, i: (b * nq + i, 0)),
        out_shape=jax.ShapeDtypeStruct((n, A_WIDTH), BF16),
        scratch_shapes=[pltpu.VMEM((nq, tq, tq), jnp.int32),
                        pltpu.VMEM((nq, tq, tq), F32),
                        pltpu.VMEM((A_WIDTH, tq), BF16),
                        pltpu.VMEM((IDX_HEADS * LANES, tq), BF16),
                        pltpu.VMEM((LANES, tq), F32),
                        pltpu.VMEM((A_HEADS, 1, tq), F32),
                        pltpu.VMEM((A_HEADS, 1, tq), F32),
                        pltpu.VMEM((A_HEADS, A_HEAD_DIM, tq), F32),
                        pltpu.VMEM((A_HEADS, tq, tq), F32)],
        compiler_params=_cparams(("arbitrary", "arbitrary")),
        name="dsa_attention",
    )(p16, p32, p32, p16, vt, p32, bias_tiles)


def _t5_bucket(rel):
    max_exact = REL_BUCKETS // 2
    relf = jnp.maximum(rel, 1).astype(F32)
    large = max_exact + (jnp.log(relf / max_exact) / math.log(REL_MAX_DIST / max_exact)
                         * (REL_BUCKETS - max_exact)).astype(jnp.int32)
    large = jnp.minimum(large, REL_BUCKETS - 1)
    return jnp.where(rel < max_exact, rel, large)


def _bias_tiles(rel_bias, tq):
    assert tq >= REL_MAX_DIST
    key = jnp.arange(tq, dtype=jnp.int32)[:, None]
    qry = jnp.arange(tq, dtype=jnp.int32)[None, :]
    tiles = [rel_bias[_t5_bucket(jnp.maximum(lag * tq + qry - key, 0))] for lag in range(2)]
    tiles = jnp.stack(tiles, axis=0) - rel_bias[REL_BUCKETS - 1][None, None, None, :]
    return jnp.transpose(tiles, (3, 0, 1, 2)).astype(F32)


def _hgrn_kernel(q_ref, f_ref, i_ref, g_ref, lb_ref, ng_ref, tril_ref, o_ref, st_ref, *, rows):
    @pl.when(pl.program_id(2) == 0)
    def _():
        st_ref[...] = jnp.zeros_like(st_ref)

    lb = lb_ref[0]
    f = lb + (1.0 - lb) * jax.nn.sigmoid(f_ref[...])
    logf = jnp.log(f)
    kk = 1.0 - f
    g1 = logf.astype(BF16)
    r1 = logf - g1.astype(F32)
    g2 = r1.astype(BF16)
    g3 = (r1 - g2.astype(F32)).astype(BF16)
    tril = tril_ref[...]
    bcum = (jnp.dot(tril, g1, preferred_element_type=F32)
            + jnp.dot(tril, g2, preferred_element_type=F32)
            + jnp.dot(tril, g3, preferred_element_type=F32))

    srow = lax.broadcasted_iota(jnp.int32, (HGRN_C, B_HEAD_DIM), 0)
    trow = lax.broadcasted_iota(jnp.int32, (HGRN_SB, HGRN_C), 0)
    scol = lax.broadcasted_iota(jnp.int32, (HGRN_SB, HGRN_C), 1)
    ng = ng_ref[...]

    for n in range(rows // HGRN_C):
        cs = slice(n * HGRN_C, (n + 1) * HGRN_C)
        bc = bcum[cs]
        qc = q_ref[cs, :].astype(F32)
        kc = kk[cs]
        vc = i_ref[cs, :]
        st = st_ref[...]
        qb = (qc * jnp.exp(bc)).astype(BF16)
        o_inter = lax.dot_general(qb, st.astype(BF16), NT_DIMS, preferred_element_type=F32)
        a_rows = []
        for sb in range(HGRN_C // HGRN_SB):
            s0 = sb * HGRN_SB
            beta = bc[s0 - 1:s0] if sb > 0 else jnp.zeros((1, B_HEAD_DIM), F32)
            qs = (qc[s0:s0 + HGRN_SB] * jnp.exp(bc[s0:s0 + HGRN_SB] - beta)).astype(BF16)
            expo = jnp.where(srow < s0 + HGRN_SB, beta - bc, -jnp.inf)
            ks = (kc * jnp.exp(expo)).astype(BF16)
            a = lax.dot_general(qs, ks, NT_DIMS, preferred_element_type=F32)
            a_rows.append(jnp.where(scol <= trow + s0, a, 0.0))
        attn = jnp.concatenate(a_rows, axis=0).astype(BF16)
        o = o_inter + jnp.dot(attn, vc, preferred_element_type=F32)
        blast = bc[HGRN_C - 1:HGRN_C]
        kdec = (kc * jnp.exp(blast - bc)).astype(BF16)
        st_ref[...] = st * jnp.exp(blast) + lax.dot_general(vc, kdec, TN_DIMS,
                                                            preferred_element_type=F32)
        ms = jnp.mean(o * o, axis=-1, keepdims=True)
        on = o * lax.rsqrt(ms + EPS) * ng
        o_ref[cs, :] = (on * _silu(g_ref[cs, :].astype(F32))).astype(o_ref.dtype)


def _hgrn2(p16, p32, lb_l, norm_g, bsz, seq):
    rows = min(HGRN_L, seq)
    nj = seq // rows
    n = bsz * seq
    r = jnp.arange(rows, dtype=jnp.int32)
    tril = ((r[:, None] >= r[None, :]) & (r[:, None] // HGRN_C == r[None, :] // HGRN_C)).astype(BF16)
    hb = B_WIDTH // LANES

    def col(base):
        return lambda b, h, j: (b * nj + j, base * hb + h)

    return pl.pallas_call(
        functools.partial(_hgrn_kernel, rows=rows),
        grid=(bsz, B_HEADS, nj),
        in_specs=[
            pl.BlockSpec((rows, LANES), col(P16_BQ)),
            pl.BlockSpec((rows, LANES), lambda b, h, j: (b * nj + j, P32_BF // LANES + h)),
            pl.BlockSpec((rows, LANES), col(P16_BI)),
            pl.BlockSpec((rows, LANES), col(P16_BG)),
            pl.BlockSpec((1, 1, LANES), lambda b, h, j: (h, 0, 0)),
            pl.BlockSpec((1, LANES), lambda b, h, j: (0, 0)),
            pl.BlockSpec((rows, rows), lambda b, h, j: (0, 0)),
        ],
        out_specs=pl.BlockSpec((rows, LANES), lambda b, h, j: (b * nj + j, h)),
        out_shape=jax.ShapeDtypeStruct((n, B_WIDTH), BF16),
        scratch_shapes=[pltpu.VMEM((B_HEAD_DIM, B_HEAD_DIM), F32)],
        compiler_params=_cparams(("arbitrary", "arbitrary", "arbitrary")),
        name="hgrn2",
    )(p16, p32, p16, p16, lb_l.reshape(B_HEADS, 1, B_HEAD_DIM), norm_g.reshape(1, B_HEAD_DIM), tril)


def _ret_kernel(q_ref, k_ref, v_ref, g_ref, cos_ref, sin_ref, idec_ref, qdec_ref, kdec_ref, cdec_ref,
                o_ref, st_ref):
    @pl.when(pl.program_id(2) == 0)
    def _():
        st_ref[...] = jnp.zeros_like(st_ref)

    cos = cos_ref[...]
    sin = sin_ref[...]
    hd = C_QK_DIM // 2

    def rot(a):
        ae, ao = a[:, :hd], a[:, hd:]
        return jnp.concatenate([ae * cos - ao * sin, ao * cos + ae * sin], axis=1)

    qr = rot(q_ref[...].astype(F32))
    kr = rot(k_ref[...].astype(F32))
    v = v_ref[...]
    st = st_ref[...]
    attn = lax.dot_general(qr.astype(BF16), kr.astype(BF16), NT_DIMS,
                           preferred_element_type=F32) * idec_ref[0]
    o = (jnp.dot(attn.astype(BF16), v, preferred_element_type=F32)
         + jnp.dot((qr * qdec_ref[0]).astype(BF16), st.astype(BF16), preferred_element_type=F32))
    krd_t = (kr * kdec_ref[0]).T.astype(BF16)
    st_ref[...] = cdec_ref[0, 0:1, :] * st + jnp.dot(krd_t, v, preferred_element_type=F32)
    ms = jnp.mean(o * o, axis=-1, keepdims=True)
    o_ref[...] = (_silu(g_ref[...].astype(F32)) * (o * lax.rsqrt(ms + EPS))).astype(o_ref.dtype)


def _retention_tables(seq):
    pos = jnp.arange(seq, dtype=F32)
    theta = 1.0 / (10000.0 ** jnp.linspace(0.0, 1.0, C_QK_DIM // 2))
    ang = pos[:, None] * theta[None, :]
    log_gamma = jnp.log(1.0 - 2.0 ** (-5.0 - jnp.arange(C_HEADS, dtype=F32)))
    idx = jnp.arange(RET_C, dtype=F32)
    causal = idx[:, None] >= idx[None, :]
    idec = jnp.exp(jnp.where(causal[None], (idx[:, None] - idx[None, :])[None] * log_gamma[:, None, None],
                             -jnp.inf))
    qdec = jnp.exp((idx + 1.0)[None, :] * log_gamma[:, None])[..., None]
    kdec = jnp.exp((RET_C - 1.0 - idx)[None, :] * log_gamma[:, None])[..., None]
    cdec = jnp.exp(RET_C * log_gamma)[:, None, None]
    return (jnp.cos(ang), jnp.sin(ang), idec,
            jnp.broadcast_to(qdec, (C_HEADS, RET_C, C_QK_DIM)),
            jnp.broadcast_to(kdec, (C_HEADS, RET_C, C_QK_DIM)),
            jnp.broadcast_to(cdec, (C_HEADS, 8, C_V_DIM)))


def _retention(p16, tables, bsz, seq):
    cos, sin, idec, qdec, kdec, cdec = tables
    nj = seq // RET_C
    n = bsz * seq
    qk_b = 1024 // C_QK_DIM
    v_b = 1024 // C_V_DIM
    return pl.pallas_call(
        _ret_kernel,
        grid=(bsz, C_HEADS, nj),
        in_specs=[
            pl.BlockSpec((RET_C, C_QK_DIM), lambda b, h, j: (b * nj + j, P16_CQ * qk_b + h)),
            pl.BlockSpec((RET_C, C_QK_DIM), lambda b, h, j: (b * nj + j, P16_CK * qk_b + h)),
            pl.BlockSpec((RET_C, C_V_DIM), lambda b, h, j: (b * nj + j, P16_CV * v_b + h)),
            pl.BlockSpec((RET_C, C_V_DIM), lambda b, h, j: (b * nj + j, P16_CG * v_b + h)),
            pl.BlockSpec((RET_C, C_QK_DIM // 2), lambda b, h, j: (j, 0)),
            pl.BlockSpec((RET_C, C_QK_DIM // 2), lambda b, h, j: (j, 0)),
            pl.BlockSpec((1, RET_C, RET_C), lambda b, h, j: (h, 0, 0)),
            pl.BlockSpec((1, RET_C, C_QK_DIM), lambda b, h, j: (h, 0, 0)),
            pl.BlockSpec((1, RET_C, C_QK_DIM), lambda b, h, j: (h, 0, 0)),
            pl.BlockSpec((1, 8, C_V_DIM), lambda b, h, j: (h, 0, 0)),
        ],
        out_specs=pl.BlockSpec((RET_C, C_V_DIM), lambda b, h, j: (b * nj + j, h)),
        out_shape=jax.ShapeDtypeStruct((n, C_V_WIDTH), BF16),
        scratch_shapes=[pltpu.VMEM((C_QK_DIM, C_V_DIM), F32)],
        compiler_params=_cparams(("arbitrary", "arbitrary", "arbitrary")),
        name="retention",
    )(p16, p16, p16, p16, cos, sin, idec, qdec, kdec, cdec)


def _merge_kernel(oa_ref, ob_ref, oc_ref, ga_ref, gb_ref, gc_ref, x_ref, mod_ref, g2_ref,
                  wa_ref, wb_ref, wc_ref, wo_ref, wrh_ref, wrl_ref, br_ref,
                  x1_ref, h2_ref, gate_ref, *, tiles_per_batch):
    b = pl.program_id(0) // tiles_per_batch

    def gated(o_ref, w_ref, g_ref):
        y = jnp.dot(o_ref[...], w_ref[...], preferred_element_type=F32)
        return jax.nn.sigmoid(g_ref[...].astype(F32)) * y

    merged = gated(oa_ref, wa_ref, ga_ref) + gated(ob_ref, wb_ref, gb_ref) + gated(oc_ref, wc_ref, gc_ref)
    y = jnp.dot(merged.astype(BF16), wo_ref[...], preferred_element_type=F32)
    gt1 = mod_ref[pl.ds(b, 1), 2 * D_MODEL:3 * D_MODEL]
    x1 = x_ref[...] + gt1 * y
    x1_ref[...] = x1
    sh2 = mod_ref[pl.ds(b, 1), 3 * D_MODEL:4 * D_MODEL]
    sc2 = mod_ref[pl.ds(b, 1), 4 * D_MODEL:5 * D_MODEL]
    h2 = _rms_mod(x1, g2_ref[...], sc2, sh2)
    h_hi = h2.astype(BF16)
    h2_ref[...] = h_hi
    h_lo = (h2 - h_hi.astype(F32)).astype(BF16)
    logits = (jnp.dot(h_hi, wrh_ref[...], preferred_element_type=F32)
              + jnp.dot(h_lo, wrh_ref[...], preferred_element_type=F32)
              + jnp.dot(h_hi, wrl_ref[...], preferred_element_type=F32)) + br_ref[...]
    lane = lax.broadcasted_iota(jnp.int32, logits.shape, 1).astype(F32)
    neg_inf = -jnp.inf

    def first_argmax(vals):
        top = jnp.max(vals, axis=-1, keepdims=True)
        idx = jnp.min(jnp.where(vals == top, lane, float(LANES)), axis=-1, keepdims=True)
        return top, idx

    gl = jnp.where(lane < N_GROUPS, logits, neg_inf)
    gmax, gsel = first_argmax(gl)
    gprob = 1.0 / jnp.sum(jnp.exp(gl - gmax), axis=-1, keepdims=True)
    lo = N_GROUPS + EXPERTS_PER_GROUP * gsel
    el = jnp.where((lane >= lo) & (lane < lo + EXPERTS_PER_GROUP), logits, neg_inf)
    v1, i1 = first_argmax(el)
    el2 = jnp.where(lane == i1, neg_inf, el)
    v2, i2 = first_argmax(el2)
    e2 = jnp.exp(v2 - v1)
    den = 1.0 + e2
    gate_ref[...] = jnp.where(lane == i1, gprob / den, jnp.where(lane == i2, gprob * (e2 / den), 0.0))


def _merge(o_a, o_b, o_c, p16, x2, mod_l, g2, wa, wb, wc, wo, wr_hi, wr_lo, br, seq):
    n = x2.shape[0]
    tm = min(512, seq)
    one = pl.Buffered(1)

    def rows(width, cb=0):
        return pl.BlockSpec((tm, width), lambda i: (i, cb))

    def whole(a):
        return pl.BlockSpec(a.shape, lambda i: (0,) * a.ndim, pipeline_mode=one)

    return pl.pallas_call(
        functools.partial(_merge_kernel, tiles_per_batch=seq // tm),
        grid=(n // tm,),
        in_specs=[rows(A_WIDTH), rows(B_WIDTH), rows(C_V_WIDTH),
                  rows(D_MODEL, P16_GA), rows(D_MODEL, P16_GB), rows(D_MODEL, P16_GC),
                  rows(D_MODEL), whole(mod_l), pl.BlockSpec((1, D_MODEL), lambda i: (0, 0)),
                  whole(wa), whole(wb), whole(wc), whole(wo), whole(wr_hi), whole(wr_lo), whole(br)],
        out_specs=[rows(D_MODEL), rows(D_MODEL), rows(LANES)],
        out_shape=[jax.ShapeDtypeStruct((n, D_MODEL), F32),
                   jax.ShapeDtypeStruct((n, D_MODEL), BF16),
                   jax.ShapeDtypeStruct((n, LANES), F32)],
        compiler_params=_cparams(("arbitrary",)),
        name="merge_route",
    )(o_a, o_b, o_c, p16, p16, p16, x2, mod_l, g2.reshape(1, D_MODEL), wa, wb, wc, wo, wr_hi, wr_lo, br)


def _moe_kernel(h_ref, gate_ref, x1_ref, mod_ref, wg_ref, wu_ref, wd_ref, o_ref, acc_ref, *,
                tiles_per_batch):
    e = pl.program_id(1)

    @pl.when(e == 0)
    def _():
        acc_ref[...] = jnp.zeros_like(acc_ref)

    h = h_ref[...]
    a = jnp.dot(h, wg_ref[0].astype(BF16), preferred_element_type=F32)
    u = jnp.dot(h, wu_ref[0].astype(BF16), preferred_element_type=F32)
    lane = lax.broadcasted_iota(jnp.int32, gate_ref.shape, 1)
    gcol = jnp.sum(jnp.where(lane == e + N_GROUPS, gate_ref[...], 0.0), axis=-1, keepdims=True)
    hm = (_silu(a) * u * gcol).astype(BF16)
    acc_ref[...] += jnp.dot(hm, wd_ref[0].astype(BF16), preferred_element_type=F32)

    @pl.when(e == N_EXPERTS - 1)
    def _():
        b = pl.program_id(0) // tiles_per_batch
        gt2 = mod_ref[pl.ds(b, 1), 5 * D_MODEL:6 * D_MODEL]
        o_ref[...] = x1_ref[...] + gt2 * acc_ref[...]


def _moe(h2, gates, x1, mod_l, wg, wu, wd, seq):
    n = h2.shape[0]
    tm = min(1024, seq)
    return pl.pallas_call(
        functools.partial(_moe_kernel, tiles_per_batch=seq // tm),
        grid=(n // tm, N_EXPERTS),
        in_specs=[pl.BlockSpec((tm, D_MODEL), lambda i, e: (i, 0)),
                  pl.BlockSpec((tm, LANES), lambda i, e: (i, 0)),
                  pl.BlockSpec((tm, D_MODEL), lambda i, e: (i, 0), pipeline_mode=pl.Buffered(1)),
                  pl.BlockSpec(mod_l.shape, lambda i, e: (0, 0)),
                  pl.BlockSpec((1, D_MODEL, D_EXPERT), lambda i, e: (e, 0, 0)),
                  pl.BlockSpec((1, D_MODEL, D_EXPERT), lambda i, e: (e, 0, 0)),
                  pl.BlockSpec((1, D_EXPERT, D_MODEL), lambda i, e: (e, 0, 0))],
        out_specs=pl.BlockSpec((tm, D_MODEL), lambda i, e: (i, 0)),
        out_shape=jax.ShapeDtypeStruct((n, D_MODEL), F32),
        scratch_shapes=[pltpu.VMEM((tm, D_MODEL), F32)],
        compiler_params=_cparams(("arbitrary", "arbitrary")),
        name="experts",
    )(h2, gates, x1, mod_l, wg, wu, wd)


def _final_norm_kernel(x_ref, g_ref, o_ref):
    x = x_ref[...]
    ms = jnp.mean(x * x, axis=-1, keepdims=True)
    o_ref[...] = x * lax.rsqrt(ms + EPS) * g_ref[...]


def _final_norm(x2, g, seq):
    n = x2.shape[0]
    tm = min(1024, seq)
    return pl.pallas_call(
        _final_norm_kernel,
        grid=(n // tm,),
        in_specs=[pl.BlockSpec((tm, D_MODEL), lambda i: (i, 0)),
                  pl.BlockSpec((1, D_MODEL), lambda i: (0, 0))],
        out_specs=pl.BlockSpec((tm, D_MODEL), lambda i: (i, 0)),
        out_shape=jax.ShapeDtypeStruct((n, D_MODEL), F32),
        compiler_params=_cparams(("arbitrary",)),
        name="final_norm",
    )(x2, g.reshape(1, D_MODEL))


def _pack_w_in(w_in_l):
    offs = [0]
    for s in IN_SPLITS:
        offs.append(offs[-1] + s)
    (aq, ak, av, iq, ik, iw, bq, bf, bi, bg, cq, ck, cv, cg, ga, gb, gc) = [
        w_in_l[:, offs[i]:offs[i + 1]] for i in range(len(IN_SPLITS))]

    def even_odd(w):
        w4 = w.reshape(D_MODEL, C_HEADS, C_QK_DIM // 2, 2)
        return jnp.concatenate([w4[..., 0], w4[..., 1]], axis=-1).reshape(D_MODEL, C_QK_WIDTH)

    w16 = jnp.concatenate(
        [cv, cg, aq * (A_HEAD_DIM ** -0.5), ak, bq, bi, bg, even_odd(cq),
         even_odd(ck) * (C_QK_DIM ** -0.5), ga, gb, gc], axis=1).astype(BF16)
    iq_p = jnp.pad(iq.reshape(D_MODEL, IDX_HEADS, IDX_DIM),
                   ((0, 0), (0, 0), (0, LANES - IDX_DIM))).reshape(D_MODEL, IDX_HEADS * LANES)
    w32 = jnp.concatenate(
        [bf, iq_p, jnp.pad(ik, ((0, 0), (0, LANES - IDX_DIM))),
         jnp.pad(iw, ((0, 0), (0, LANES - IDX_HEADS)))], axis=1).astype(BF16)
    return w16, w32, av.T.astype(BF16)


def _split_bf16(w):
    hi = w.astype(BF16)
    return hi, (w - hi.astype(F32)).astype(BF16)


def kernel(x, c, rel_bias, hgrn_lb_raw, norm1_g, norm2_g, ada_w, ada_b, w_in, hgrn_norm_g, w_branch_a,
           w_branch_b, w_branch_c, w_out, router_group_w, router_group_b, router_expert_w,
           router_expert_b, expert_w_gate, expert_w_up, expert_w_down, final_norm_g):
    bsz, seq, _ = x.shape
    depth = w_in.shape[0]
    n = bsz * seq
    x2 = x.reshape(n, D_MODEL)
    tq = min(DSA_TQ, seq)

    lb_all = _hgrn_lower_bounds(hgrn_lb_raw)
    c_pad = jnp.pad(c, ((0, (-bsz) % 8), (0, 0)))
    mod = _ada_mod(c_pad, ada_w, ada_b)
    bias_tiles = _bias_tiles(rel_bias, tq)
    ret_tables = _retention_tables(seq)

    for l in range(depth):
        w16, w32, wvt = _pack_w_in(w_in[l])
        p16 = _norm_project(x2, mod[l], norm1_g[l], w16, BF16, 1024, seq, "proj_bf16")
        p32 = _norm_project(x2, mod[l], norm1_g[l], w32, F32, 768, seq, "proj_f32")
        vt = _norm_project_t(x2, mod[l], norm1_g[l], wvt, tq, seq, "proj_vt")
        o_a = _dsa_attention(p16, p32, vt, bias_tiles, bsz, seq)
        o_b = _hgrn2(p16, p32, lb_all[l], hgrn_norm_g[l], bsz, seq)
        o_c = _retention(p16, ret_tables, bsz, seq)
        wr = jnp.concatenate([router_group_w[l], router_expert_w[l],
                              jnp.zeros((D_MODEL, LANES - N_GROUPS - N_EXPERTS), F32)], axis=1)
        br = jnp.concatenate([router_group_b[l], router_expert_b[l],
                              jnp.zeros((LANES - N_GROUPS - N_EXPERTS,), F32)]).reshape(1, LANES)
        wr_hi, wr_lo = _split_bf16(wr)
        x1, h2, gates = _merge(o_a, o_b, o_c, p16, x2, mod[l], norm2_g[l],
                               w_branch_a[l].astype(BF16), w_branch_b[l].astype(BF16),
                               w_branch_c[l].astype(BF16), w_out[l].astype(BF16),
                               wr_hi, wr_lo, br, seq)
        x2 = _moe(h2, gates, x1, mod[l], expert_w_gate[l], expert_w_up[l], expert_w_down[l], seq)

    return _final_norm(x2, final_norm_g, seq).reshape(bsz, seq, D_MODEL)
```

```python
import functools
import math

import jax
import jax.numpy as jnp
from jax import lax
from jax.experimental import pallas as pl
from jax.experimental.pallas import tpu as pltpu
from jax.experimental.pallas import tpu_sc as plsc

F32 = jnp.float32
BF16 = jnp.bfloat16

D_MODEL = 1024
A_HEADS = 8
A_HEAD_DIM = 128
IDX_HEADS = 8
IDX_DIM = 64
TOPK_MAX = 256
REL_BUCKETS = 32
REL_MAX_DIST = 128
B_HEADS = 8
B_HEAD_DIM = 128
C_HEADS = 4
C_QK_DIM = 256
C_V_DIM = 512
N_GROUPS = 4
EXPERTS_PER_GROUP = 8
N_EXPERTS = 32
D_EXPERT = 512
EPS = 1e-6

A_WIDTH = A_HEADS * A_HEAD_DIM
B_WIDTH = B_HEADS * B_HEAD_DIM
C_QK_WIDTH = C_HEADS * C_QK_DIM
C_V_WIDTH = C_HEADS * C_V_DIM
IN_SPLITS = (A_WIDTH, A_WIDTH, A_WIDTH, IDX_HEADS * IDX_DIM, IDX_DIM, IDX_HEADS,
             B_WIDTH, B_WIDTH, B_WIDTH, B_WIDTH,
             C_QK_WIDTH, C_QK_WIDTH, C_V_WIDTH, C_V_WIDTH,
             D_MODEL, D_MODEL, D_MODEL)

LANES = 128
VMEM_LIMIT = 56 * 1024 * 1024

P16_CV, P16_CG = 0, 2
P16_AQ, P16_AK, P16_BQ, P16_BI, P16_BG, P16_CQ, P16_CK, P16_GA, P16_GB, P16_GC = range(4, 14)
P32_BF = 0
P32_IQ = 1024
P32_IK = 2048
P32_IW = 2176

DSA_TQ = 256
HGRN_L = 256
HGRN_C = 64
HGRN_SB = 16
RET_C = 128
KEY_NEG_INF = -2139095041
INT_MIN = -2147483648
MASK_NEG = -1e30
COUNT_CHAINS = 4
MOE_TM = 256
SC_WINDOW = 128
SC_SUB = 256

NT_DIMS = (((1,), (1,)), ((), ()))
TN_DIMS = (((0,), (0,)), ((), ()))


def _cparams(sem):
    return pltpu.CompilerParams(dimension_semantics=sem, vmem_limit_bytes=VMEM_LIMIT)


def _silu(x):
    return x * jax.nn.sigmoid(x)


def _lb_kernel(raw_ref, o_ref):
    raw = raw_ref[...]
    m = jnp.max(raw, axis=0, keepdims=True)
    e = jnp.exp(raw - m)
    soft = e / jnp.sum(e, axis=0, keepdims=True)
    run = jnp.zeros_like(soft[0:1])
    for l in range(raw.shape[0]):
        run = run + soft[l:l + 1]
        o_ref[l:l + 1, :] = run - soft[0:1]


def _hgrn_lower_bounds(raw):
    return pl.pallas_call(
        _lb_kernel, out_shape=jax.ShapeDtypeStruct(raw.shape, F32), name="hgrn_lb")(raw)


def _ada_kernel(c_ref, w_ref, b_ref, o_ref):
    a = _silu(c_ref[...])
    o_ref[0] = jnp.dot(a, w_ref[0], precision=lax.Precision.HIGHEST,
                       preferred_element_type=F32) + b_ref[0]


def _ada_mod(c_pad, ada_w, ada_b):
    depth = ada_w.shape[0]
    rows = c_pad.shape[0]
    return pl.pallas_call(
        _ada_kernel,
        grid=(depth, 6),
        in_specs=[pl.BlockSpec((rows, D_MODEL), lambda l, j: (0, 0)),
                  pl.BlockSpec((1, D_MODEL, D_MODEL), lambda l, j: (l, 0, j)),
                  pl.BlockSpec((1, 1, D_MODEL), lambda l, j: (l, 0, j))],
        out_specs=pl.BlockSpec((1, rows, D_MODEL), lambda l, j: (l, 0, j)),
        out_shape=jax.ShapeDtypeStruct((depth, rows, 6 * D_MODEL), F32),
        compiler_params=_cparams(("arbitrary", "arbitrary")),
        name="ada_mod",
    )(c_pad, ada_w, ada_b.reshape(depth, 1, 6 * D_MODEL))


def _rms_mod(x, g, sc, sh):
    ms = jnp.mean(x * x, axis=-1, keepdims=True)
    return (x * lax.rsqrt(ms + EPS) * g) * (1.0 + sc) + sh


def _norm1(x_ref, mod_ref, g_ref, b):
    sh = mod_ref[pl.ds(b, 1), 0:D_MODEL]
    sc = mod_ref[pl.ds(b, 1), D_MODEL:2 * D_MODEL]
    return _rms_mod(x_ref[...], g_ref[...], sc, sh).astype(BF16)


def _proj_kernel(x_ref, mod_ref, g_ref, w_ref, o_ref, h_ref, *, tiles_per_batch):
    @pl.when(pl.program_id(1) == 0)
    def _():
        h_ref[...] = _norm1(x_ref, mod_ref, g_ref, pl.program_id(0) // tiles_per_batch)

    o_ref[...] = jnp.dot(h_ref[...], w_ref[...], preferred_element_type=F32).astype(o_ref.dtype)


def _norm_project(x2, mod_l, g, w, out_dtype, tn, seq, name):
    n = x2.shape[0]
    width = w.shape[1]
    tm = min(1024, seq)
    return pl.pallas_call(
        functools.partial(_proj_kernel, tiles_per_batch=seq // tm),
        grid=(n // tm, width // tn),
        in_specs=[pl.BlockSpec((tm, D_MODEL), lambda i, j: (i, 0)),
                  pl.BlockSpec(mod_l.shape, lambda i, j: (0, 0)),
                  pl.BlockSpec((1, D_MODEL), lambda i, j: (0, 0)),
                  pl.BlockSpec((D_MODEL, tn), lambda i, j: (0, j))],
        out_specs=pl.BlockSpec((tm, tn), lambda i, j: (i, j)),
        out_shape=jax.ShapeDtypeStruct((n, width), out_dtype),
        scratch_shapes=[pltpu.VMEM((tm, D_MODEL), BF16)],
        compiler_params=_cparams(("arbitrary", "arbitrary")),
        name=name,
    )(x2, mod_l, g.reshape(1, D_MODEL), w)


def _proj_t_kernel(x_ref, mod_ref, g_ref, wt_ref, o_ref, *, tiles_per_batch, chunk):
    h = _norm1(x_ref, mod_ref, g_ref, pl.program_id(0) // tiles_per_batch)
    res = lax.dot_general(wt_ref[...], h, NT_DIMS, preferred_element_type=F32)
    for ci in range(o_ref.shape[0]):
        o_ref[ci] = res[:, ci * chunk:(ci + 1) * chunk].astype(o_ref.dtype)


def _norm_project_t(x2, mod_l, g, wt, chunk, seq, name):
    n = x2.shape[0]
    cols = wt.shape[0]
    tm = min(1024, seq)
    return pl.pallas_call(
        functools.partial(_proj_t_kernel, tiles_per_batch=seq // tm, chunk=chunk),
        grid=(n // tm,),
        in_specs=[pl.BlockSpec((tm, D_MODEL), lambda i: (i, 0)),
                  pl.BlockSpec(mod_l.shape, lambda i: (0, 0)),
                  pl.BlockSpec((1, D_MODEL), lambda i: (0, 0)),
                  pl.BlockSpec((cols, D_MODEL), lambda i: (0, 0))],
        out_specs=pl.BlockSpec((tm // chunk, cols, chunk), lambda i: (i, 0, 0)),
        out_shape=jax.ShapeDtypeStruct((n // chunk, cols, chunk), BF16),
        compiler_params=_cparams(("arbitrary",)),
        name=name,
    )(x2, mod_l, g.reshape(1, D_MODEL), wt)


def _dsa_kernel(q_ref, iq_ref, iw_ref, k_ref, vt_ref, ik_ref, bias_ref, o_ref,
                key_ref, madd_ref, qt_ref, iqt_ref, iwt_ref, m_ref, l_ref, acc_ref, s_ref, *, tq, topk):
    qi = pl.program_id(1)
    nck = qi + 1
    idx_scale = (IDX_HEADS * IDX_DIM) ** -0.5

    for h in range(A_HEADS):
        hs = slice(h * LANES, (h + 1) * LANES)
        qt_ref[hs, :] = q_ref[:, hs].astype(F32).T.astype(BF16)
        iqt_ref[hs, :] = iq_ref[:, hs].T.astype(BF16)
    iwt_ref[...] = (iw_ref[...] * idx_scale).T

    krow = lax.broadcasted_iota(jnp.int32, (tq, tq), 0)
    qcol = lax.broadcasted_iota(jnp.int32, (tq, tq), 1)

    def score_chunk(c, carry):
        off = pl.multiple_of(c * tq, tq)
        ikc = ik_ref[pl.ds(off, tq), :].astype(BF16)
        acc = jnp.zeros((tq, tq), F32)
        for h in range(IDX_HEADS):
            s = jnp.dot(ikc, iqt_ref[h * LANES:(h + 1) * LANES, :], preferred_element_type=F32)
            acc = acc + jnp.maximum(s, 0.0) * iwt_ref[h:h + 1, :]
        acc = jnp.where(acc == 0.0, 0.0, acc)
        acc = jnp.where(krow + (c - qi) * tq <= qcol, acc, -jnp.inf)
        kb = pltpu.bitcast(acc, jnp.int32)
        key_ref[c] = jnp.where(kb < 0, kb ^ jnp.int32(0x7FFFFFFF), kb)
        return carry

    lax.fori_loop(0, nck, score_chunk, 0)

    def count(pred_fn):
        def body(c, parts):
            hit = jnp.where(pred_fn(key_ref[c]), 1.0, 0.0)
            parts = list(parts)
            for r in range(tq // 8):
                parts[r % COUNT_CHAINS] = parts[r % COUNT_CHAINS] + hit[r * 8:(r + 1) * 8, :]
            return tuple(parts)

        parts = lax.fori_loop(0, nck, body, (jnp.zeros((8, tq), F32),) * COUNT_CHAINS)
        return jnp.sum(sum(parts), axis=0, keepdims=True)

    def bit_step(i, theta):
        cand = theta + jnp.left_shift(jnp.int32(1), 31 - i)
        return jnp.where(count(lambda kc: kc >= cand) >= topk, cand, theta)

    theta = lax.fori_loop(0, 32, bit_step, jnp.full((1, tq), INT_MIN, jnp.int32))
    theta = jnp.maximum(theta, KEY_NEG_INF + 1)

    def mask_chunk(c, cnt):
        ge = key_ref[c] >= theta
        madd_ref[c] = jnp.where(ge, 0.0, MASK_NEG)
        return cnt + jnp.sum(jnp.where(ge, 1.0, 0.0), axis=0, keepdims=True)

    cnt_ge = lax.fori_loop(0, nck, mask_chunk, jnp.zeros((1, tq), F32))

    @pl.when(jnp.max(cnt_ge) > topk)
    def _():
        need_eq = topk - count(lambda kc: kc > theta)
        incl = jnp.where(krow >= qcol, 1.0, 0.0).astype(BF16)

        def tie_chunk(c, run):
            kc = key_ref[c]
            eq = kc == theta
            eqf = jnp.where(eq, 1.0, 0.0)
            pref = jnp.dot(incl, eqf.astype(BF16), preferred_element_type=F32) + run
            eq_add = jnp.where(pref <= need_eq, 0.0, MASK_NEG)
            madd_ref[c] = jnp.where(eq, eq_add, jnp.where(kc > theta, 0.0, MASK_NEG))
            return run + jnp.sum(eqf, axis=0, keepdims=True)

        lax.fori_loop(0, nck, tie_chunk, jnp.zeros((1, tq), F32))

    m_ref[...] = jnp.full(m_ref.shape, -jnp.inf, F32)
    l_ref[...] = jnp.zeros(l_ref.shape, F32)
    acc_ref[...] = jnp.zeros(acc_ref.shape, F32)

    def attend(c, lag):
        off = pl.multiple_of(c * tq, tq)
        head_slices = [slice(h * A_HEAD_DIM, (h + 1) * A_HEAD_DIM) for h in range(A_HEADS)]
        for h, hs in enumerate(head_slices):
            s_ref[h] = jnp.dot(k_ref[pl.ds(off, tq), hs], qt_ref[hs, :], preferred_element_type=F32)
        for h, hs in enumerate(head_slices):
            s = s_ref[h] + madd_ref[c]
            if lag is not None:
                s = s + bias_ref[h, lag]
            m_old = m_ref[h]
            m_new = jnp.maximum(m_old, jnp.max(s, axis=0, keepdims=True))
            alpha = jnp.exp(m_old - m_new)
            p = jnp.exp(s - m_new)
            l_ref[h] = alpha * l_ref[h] + jnp.sum(p, axis=0, keepdims=True)
            acc_ref[h] = alpha * acc_ref[h] + jnp.dot(vt_ref[c, hs, :], p.astype(BF16),
                                                      preferred_element_type=F32)
            m_ref[h] = m_new

    def far_chunk(c, carry):
        attend(c, None)
        return carry

    lax.fori_loop(0, jnp.maximum(qi - 1, 0), far_chunk, 0)

    @pl.when(qi >= 1)
    def _():
        attend(qi - 1, 1)

    attend(qi, 0)

    for h in range(A_HEADS):
        o = acc_ref[h] * (1.0 / l_ref[h])
        o_ref[:, h * A_HEAD_DIM:(h + 1) * A_HEAD_DIM] = o.T.astype(o_ref.dtype)


def _dsa_attention(p16, p32, vt, bias_tiles, bsz, seq):
    tq = min(DSA_TQ, seq)
    nq = seq // tq
    topk = min(TOPK_MAX, seq // 4)
    n = bsz * seq
    one = pl.Buffered(1)
    return pl.pallas_call(
        functools.partial(_dsa_kernel, tq=tq, topk=topk),
        grid=(bsz, nq),
        in_specs=[
            pl.BlockSpec((tq, A_WIDTH), lambda b, i: (b * nq + i, P16_AQ)),
            pl.BlockSpec((tq, 1024), lambda b, i: (b * nq + i, P32_IQ // 1024)),
            pl.BlockSpec((tq, LANES), lambda b, i: (b * nq + i, P32_IW // LANES)),
            pl.BlockSpec((seq, A_WIDTH), lambda b, i: (b, P16_AK), pipeline_mode=one),
            pl.BlockSpec((nq, A_WIDTH, tq), lambda b, i: (b, 0, 0), pipeline_mode=one),
            pl.BlockSpec((seq, LANES), lambda b, i: (b, P32_IK // LANES), pipeline_mode=one),
            pl.BlockSpec(bias_tiles.shape, lambda b, i: (0, 0, 0, 0), pipeline_mode=one),
        ],
        out_specs=pl.BlockSpec((tq, A_WIDTH), lambda b, i: (b * nq + i, 0)),
        out_shape=jax.ShapeDtypeStruct((n, A_WIDTH), BF16),
        scratch_shapes=[pltpu.VMEM((nq, tq, tq), jnp.int32),
                        pltpu.VMEM((nq, tq, tq), F32),
                        pltpu.VMEM((A_WIDTH, tq), BF16),
                        pltpu.VMEM((IDX_HEADS * LANES, tq), BF16),
                        pltpu.VMEM((LANES, tq), F32),
                        pltpu.VMEM((A_HEADS, 1, tq), F32),
                        pltpu.VMEM((A_HEADS, 1, tq), F32),
                        pltpu.VMEM((A_HEADS, A_HEAD_DIM, tq), F32),
                        pltpu.VMEM((A_HEADS, tq, tq), F32)],
        compiler_params=_cparams(("arbitrary", "arbitrary")),
        name="dsa_attention",
    )(p16, p32, p32, p16, vt, p32, bias_tiles)


def _t5_bucket(rel):
    max_exact = REL_BUCKETS // 2
    relf = jnp.maximum(rel, 1).astype(F32)
    large = max_exact + (jnp.log(relf / max_exact) / math.log(REL_MAX_DIST / max_exact)
                         * (REL_BUCKETS - max_exact)).astype(jnp.int32)
    large = jnp.minimum(large, REL_BUCKETS - 1)
    return jnp.where(rel < max_exact, rel, large)


def _bias_tiles(rel_bias, tq):
    assert tq >= REL_MAX_DIST
    key = jnp.arange(tq, dtype=jnp.int32)[:, None]
    qry = jnp.arange(tq, dtype=jnp.int32)[None, :]
    tiles = [rel_bias[_t5_bucket(jnp.maximum(lag * tq + qry - key, 0))] for lag in range(2)]
    tiles = jnp.stack(tiles, axis=0) - rel_bias[REL_BUCKETS - 1][None, None, None, :]
    return jnp.transpose(tiles, (3, 0, 1, 2)).astype(F32)


def _hgrn_kernel(q_ref, f_ref, i_ref, g_ref, lb_ref, ng_ref, tril_ref, o_ref, st_ref, *, rows):
    @pl.when(pl.program_id(2) == 0)
    def _():
        st_ref[...] = jnp.zeros_like(st_ref)

    lb = lb_ref[0]
    f = lb + (1.0 - lb) * jax.nn.sigmoid(f_ref[...])
    logf = jnp.log(f)
    kk = 1.0 - f
    g1 = logf.astype(BF16)
    r1 = logf - g1.astype(F32)
    g2 = r1.astype(BF16)
    g3 = (r1 - g2.astype(F32)).astype(BF16)
    tril = tril_ref[...]
    bcum = (jnp.dot(tril, g1, preferred_element_type=F32)
            + jnp.dot(tril, g2, preferred_element_type=F32)
            + jnp.dot(tril, g3, preferred_element_type=F32))

    srow = lax.broadcasted_iota(jnp.int32, (HGRN_C, B_HEAD_DIM), 0)
    trow = lax.broadcasted_iota(jnp.int32, (HGRN_SB, HGRN_C), 0)
    scol = lax.broadcasted_iota(jnp.int32, (HGRN_SB, HGRN_C), 1)
    ng = ng_ref[...]

    for n in range(rows // HGRN_C):
        cs = slice(n * HGRN_C, (n + 1) * HGRN_C)
        bc = bcum[cs]
        qc = q_ref[cs, :].astype(F32)
        kc = kk[cs]
        vc = i_ref[cs, :]
        st = st_ref[...]
        qb = (qc * jnp.exp(bc)).astype(BF16)
        o_inter = lax.dot_general(qb, st.astype(BF16), NT_DIMS, preferred_element_type=F32)
        a_rows = []
        for sb in range(HGRN_C // HGRN_SB):
            s0 = sb * HGRN_SB
            beta = bc[s0 - 1:s0] if sb > 0 else jnp.zeros((1, B_HEAD_DIM), F32)
            qs = (qc[s0:s0 + HGRN_SB] * jnp.exp(bc[s0:s0 + HGRN_SB] - beta)).astype(BF16)
            expo = jnp.where(srow < s0 + HGRN_SB, beta - bc, -jnp.inf)
            ks = (kc * jnp.exp(expo)).astype(BF16)
            a = lax.dot_general(qs, ks, NT_DIMS, preferred_element_type=F32)
            a_rows.append(jnp.where(scol <= trow + s0, a, 0.0))
        attn = jnp.concatenate(a_rows, axis=0).astype(BF16)
        o = o_inter + jnp.dot(attn, vc, preferred_element_type=F32)
        blast = bc[HGRN_C - 1:HGRN_C]
        kdec = (kc * jnp.exp(blast - bc)).astype(BF16)
        st_ref[...] = st * jnp.exp(blast) + lax.dot_general(vc, kdec, TN_DIMS,
                                                            preferred_element_type=F32)
        ms = jnp.mean(o * o, axis=-1, keepdims=True)
        on = o * lax.rsqrt(ms + EPS) * ng
        o_ref[cs, :] = (on * _silu(g_ref[cs, :].astype(F32))).astype(o_ref.dtype)


def _hgrn2(p16, p32, lb_l, norm_g, bsz, seq):
    rows = min(HGRN_L, seq)
    nj = seq // rows
    n = bsz * seq
    r = jnp.arange(rows, dtype=jnp.int32)
    tril = ((r[:, None] >= r[None, :]) & (r[:, None] // HGRN_C == r[None, :] // HGRN_C)).astype(BF16)
    hb = B_WIDTH // LANES

    def col(base):
        return lambda b, h, j: (b * nj + j, base * hb + h)

    return pl.pallas_call(
        functools.partial(_hgrn_kernel, rows=rows),
        grid=(bsz, B_HEADS, nj),
        in_specs=[
            pl.BlockSpec((rows, LANES), col(P16_BQ)),
            pl.BlockSpec((rows, LANES), lambda b, h, j: (b * nj + j, P32_BF // LANES + h)),
            pl.BlockSpec((rows, LANES), col(P16_BI)),
            pl.BlockSpec((rows, LANES), col(P16_BG)),
            pl.BlockSpec((1, 1, LANES), lambda b, h, j: (h, 0, 0)),
            pl.BlockSpec((1, LANES), lambda b, h, j: (0, 0)),
            pl.BlockSpec((rows, rows), lambda b, h, j: (0, 0)),
        ],
        out_specs=pl.BlockSpec((rows, LANES), lambda b, h, j: (b * nj + j, h)),
        out_shape=jax.ShapeDtypeStruct((n, B_WIDTH), BF16),
        scratch_shapes=[pltpu.VMEM((B_HEAD_DIM, B_HEAD_DIM), F32)],
        compiler_params=_cparams(("arbitrary", "arbitrary", "arbitrary")),
        name="hgrn2",
    )(p16, p32, p16, p16, lb_l.reshape(B_HEADS, 1, B_HEAD_DIM), norm_g.reshape(1, B_HEAD_DIM), tril)


def _ret_kernel(q_ref, k_ref, v_ref, g_ref, cos_ref, sin_ref, idec_ref, qdec_ref, kdec_ref, cdec_ref,
                o_ref, st_ref):
    @pl.when(pl.program_id(2) == 0)
    def _():
        st_ref[...] = jnp.zeros_like(st_ref)

    cos = cos_ref[...]
    sin = sin_ref[...]
    hd = C_QK_DIM // 2

    def rot(a):
        ae, ao = a[:, :hd], a[:, hd:]
        return jnp.concatenate([ae * cos - ao * sin, ao * cos + ae * sin], axis=1)

    qr = rot(q_ref[...].astype(F32))
    kr = rot(k_ref[...].astype(F32))
    v = v_ref[...]
    st = st_ref[...]
    attn = lax.dot_general(qr.astype(BF16), kr.astype(BF16), NT_DIMS,
                           preferred_element_type=F32) * idec_ref[0]
    o = (jnp.dot(attn.astype(BF16), v, preferred_element_type=F32)
         + jnp.dot((qr * qdec_ref[0]).astype(BF16), st.astype(BF16), preferred_element_type=F32))
    krd_t = (kr * kdec_ref[0]).T.astype(BF16)
    st_ref[...] = cdec_ref[0, 0:1, :] * st + jnp.dot(krd_t, v, preferred_element_type=F32)
    ms = jnp.mean(o * o, axis=-1, keepdims=True)
    o_ref[...] = (_silu(g_ref[...].astype(F32)) * (o * lax.rsqrt(ms + EPS))).astype(o_ref.dtype)


def _retention_tables(seq):
    pos = jnp.arange(seq, dtype=F32)
    theta = 1.0 / (10000.0 ** jnp.linspace(0.0, 1.0, C_QK_DIM // 2))
    ang = pos[:, None] * theta[None, :]
    log_gamma = jnp.log(1.0 - 2.0 ** (-5.0 - jnp.arange(C_HEADS, dtype=F32)))
    idx = jnp.arange(RET_C, dtype=F32)
    causal = idx[:, None] >= idx[None, :]
    idec = jnp.exp(jnp.where(causal[None], (idx[:, None] - idx[None, :])[None] * log_gamma[:, None, None],
                             -jnp.inf))
    qdec = jnp.exp((idx + 1.0)[None, :] * log_gamma[:, None])[..., None]
    kdec = jnp.exp((RET_C - 1.0 - idx)[None, :] * log_gamma[:, None])[..., None]
    cdec = jnp.exp(RET_C * log_gamma)[:, None, None]
    return (jnp.cos(ang), jnp.sin(ang), idec,
            jnp.broadcast_to(qdec, (C_HEADS, RET_C, C_QK_DIM)),
            jnp.broadcast_to(kdec, (C_HEADS, RET_C, C_QK_DIM)),
            jnp.broadcast_to(cdec, (C_HEADS, 8, C_V_DIM)))


def _retention(p16, tables, bsz, seq):
    cos, sin, idec, qdec, kdec, cdec = tables
    nj = seq // RET_C
    n = bsz * seq
    qk_b = 1024 // C_QK_DIM
    v_b = 1024 // C_V_DIM
    return pl.pallas_call(
        _ret_kernel,
        grid=(bsz, C_HEADS, nj),
        in_specs=[
            pl.BlockSpec((RET_C, C_QK_DIM), lambda b, h, j: (b * nj + j, P16_CQ * qk_b + h)),
            pl.BlockSpec((RET_C, C_QK_DIM), lambda b, h, j: (b * nj + j, P16_CK * qk_b + h)),
            pl.BlockSpec((RET_C, C_V_DIM), lambda b, h, j: (b * nj + j, P16_CV * v_b + h)),
            pl.BlockSpec((RET_C, C_V_DIM), lambda b, h, j: (b * nj + j, P16_CG * v_b + h)),
            pl.BlockSpec((RET_C, C_QK_DIM // 2), lambda b, h, j: (j, 0)),
            pl.BlockSpec((RET_C, C_QK_DIM // 2), lambda b, h, j: (j, 0)),
            pl.BlockSpec((1, RET_C, RET_C), lambda b, h, j: (h, 0, 0)),
            pl.BlockSpec((1, RET_C, C_QK_DIM), lambda b, h, j: (h, 0, 0)),
            pl.BlockSpec((1, RET_C, C_QK_DIM), lambda b, h, j: (h, 0, 0)),
            pl.BlockSpec((1, 8, C_V_DIM), lambda b, h, j: (h, 0, 0)),
        ],
        out_specs=pl.BlockSpec((RET_C, C_V_DIM), lambda b, h, j: (b * nj + j, h)),
        out_shape=jax.ShapeDtypeStruct((n, C_V_WIDTH), BF16),
        scratch_shapes=[pltpu.VMEM((C_QK_DIM, C_V_DIM), F32)],
        compiler_params=_cparams(("arbitrary", "arbitrary", "arbitrary")),
        name="retention",
    )(p16, p16, p16, p16, cos, sin, idec, qdec, kdec, cdec)


def _merge_kernel(oa_ref, ob_ref, oc_ref, ga_ref, gb_ref, gc_ref, x_ref, mod_ref, g2_ref,
                  wa_ref, wb_ref, wc_ref, wo_ref, wrh_ref, wrl_ref, br_ref,
                  x1_ref, h2_ref, route_ref, *, tiles_per_batch):
    b = pl.program_id(0) // tiles_per_batch

    def gated(o_ref, w_ref, g_ref):
        y = jnp.dot(o_ref[...], w_ref[...], preferred_element_type=F32)
        return jax.nn.sigmoid(g_ref[...].astype(F32)) * y

    merged = gated(oa_ref, wa_ref, ga_ref) + gated(ob_ref, wb_ref, gb_ref) + gated(oc_ref, wc_ref, gc_ref)
    y = jnp.dot(merged.astype(BF16), wo_ref[...], preferred_element_type=F32)
    gt1 = mod_ref[pl.ds(b, 1), 2 * D_MODEL:3 * D_MODEL]
    x1 = x_ref[...] + gt1 * y
    x1_ref[...] = x1
    sh2 = mod_ref[pl.ds(b, 1), 3 * D_MODEL:4 * D_MODEL]
    sc2 = mod_ref[pl.ds(b, 1), 4 * D_MODEL:5 * D_MODEL]
    h2 = _rms_mod(x1, g2_ref[...], sc2, sh2)
    h_hi = h2.astype(BF16)
    h2_ref[...] = h_hi.astype(F32)
    h_lo = (h2 - h_hi.astype(F32)).astype(BF16)
    logits = (jnp.dot(h_hi, wrh_ref[...], preferred_element_type=F32)
              + jnp.dot(h_lo, wrh_ref[...], preferred_element_type=F32)
              + jnp.dot(h_hi, wrl_ref[...], preferred_element_type=F32)) + br_ref[...]
    lane = lax.broadcasted_iota(jnp.int32, logits.shape, 1).astype(F32)
    neg_inf = -jnp.inf

    def first_argmax(vals):
        top = jnp.max(vals, axis=-1, keepdims=True)
        idx = jnp.min(jnp.where(vals == top, lane, float(LANES)), axis=-1, keepdims=True)
        return top, idx

    gl = jnp.where(lane < N_GROUPS, logits, neg_inf)
    gmax, gsel = first_argmax(gl)
    gprob = 1.0 / jnp.sum(jnp.exp(gl - gmax), axis=-1, keepdims=True)
    lo = N_GROUPS + EXPERTS_PER_GROUP * gsel
    el = jnp.where((lane >= lo) & (lane < lo + EXPERTS_PER_GROUP), logits, neg_inf)
    v1, i1 = first_argmax(el)
    el2 = jnp.where(lane == i1, neg_inf, el)
    v2, i2 = first_argmax(el2)
    e2 = jnp.exp(v2 - v1)
    den = 1.0 + e2
    route_ref[...] = jnp.where(lane == 0.0, i1 - N_GROUPS,
                               jnp.where(lane == 1.0, i2 - N_GROUPS,
                                         jnp.where(lane == 2.0, gprob / den,
                                                   jnp.where(lane == 3.0, gprob * (e2 / den), 0.0))))


def _merge(o_a, o_b, o_c, p16, x2, mod_l, g2, wa, wb, wc, wo, wr_hi, wr_lo, br, seq):
    n = x2.shape[0]
    tm = min(512, seq)
    one = pl.Buffered(1)

    def rows(width, cb=0):
        return pl.BlockSpec((tm, width), lambda i: (i, cb))

    def whole(a):
        return pl.BlockSpec(a.shape, lambda i: (0,) * a.ndim, pipeline_mode=one)

    return pl.pallas_call(
        functools.partial(_merge_kernel, tiles_per_batch=seq // tm),
        grid=(n // tm,),
        in_specs=[rows(A_WIDTH), rows(B_WIDTH), rows(C_V_WIDTH),
                  rows(D_MODEL, P16_GA), rows(D_MODEL, P16_GB), rows(D_MODEL, P16_GC),
                  rows(D_MODEL), whole(mod_l), pl.BlockSpec((1, D_MODEL), lambda i: (0, 0)),
                  whole(wa), whole(wb), whole(wc), whole(wo), whole(wr_hi), whole(wr_lo), whole(br)],
        out_specs=[rows(D_MODEL), rows(D_MODEL), rows(LANES)],
        out_shape=[jax.ShapeDtypeStruct((n, D_MODEL), F32),
                   jax.ShapeDtypeStruct((n, D_MODEL), F32),
                   jax.ShapeDtypeStruct((n, LANES), F32)],
        compiler_params=_cparams(("arbitrary",)),
        name="merge_route",
    )(o_a, o_b, o_c, p16, p16, p16, x2, mod_l, g2.reshape(1, D_MODEL), wa, wb, wc, wo, wr_hi, wr_lo, br)


def _lane_pick(vals, lane, idx):
    return jnp.sum(jnp.where(lane == idx, vals, 0.0), axis=-1, keepdims=True)


def _rank_kernel(route_ref, rk_ref, cnt_ref, run_ref):
    @pl.when(pl.program_id(0) == 0)
    def _():
        run_ref[...] = jnp.zeros_like(run_ref)

    route = route_ref[...]
    tb = route.shape[0]
    lane = lax.broadcasted_iota(jnp.int32, route.shape, 1).astype(F32)
    e1 = _lane_pick(route, lane, 0.0)
    e2 = _lane_pick(route, lane, 1.0)
    sel = jnp.where((lane == e1) | (lane == e2), 1.0, 0.0)
    r = lax.broadcasted_iota(jnp.int32, (tb, tb), 0)
    c = lax.broadcasted_iota(jnp.int32, (tb, tb), 1)
    before = jnp.where(c < r, 1.0, 0.0).astype(BF16)
    rank = jnp.dot(before, sel.astype(BF16), preferred_element_type=F32) + run_ref[0:1, :]
    rk_ref[...] = jnp.where(lane == 0.0, _lane_pick(rank, lane, e1),
                            jnp.where(lane == 1.0, _lane_pick(rank, lane, e2), 0.0))
    run_ref[...] = run_ref[...] + jnp.sum(sel, axis=0, keepdims=True)
    cnt_ref[...] = run_ref[...]


def _expert_ranks(route):
    n = route.shape[0]
    tb = min(256, n)
    return pl.pallas_call(
        _rank_kernel,
        grid=(n // tb,),
        in_specs=[pl.BlockSpec((tb, LANES), lambda i: (i, 0))],
        out_specs=[pl.BlockSpec((tb, LANES), lambda i: (i, 0)),
                   pl.BlockSpec((8, LANES), lambda i: (0, 0))],
        out_shape=[jax.ShapeDtypeStruct((n, LANES), F32), jax.ShapeDtypeStruct((8, LANES), F32)],
        scratch_shapes=[pltpu.VMEM((8, LANES), F32)],
        compiler_params=_cparams(("arbitrary",)),
        name="expert_ranks",
    )(route)


def _plan_kernel(cnt_ref, route_ref, rk_ref, pos_ref, tmap_ref):
    lane_i = lax.broadcasted_iota(jnp.int32, (8, LANES), 1)
    cnt = jnp.where(lane_i < N_EXPERTS, cnt_ref[...], 0.0)
    padded = jnp.floor((cnt + (MOE_TM - 1)) * (1.0 / MOE_TM)) * MOE_TM
    r = lax.broadcasted_iota(jnp.int32, (LANES, LANES), 0)
    c = lax.broadcasted_iota(jnp.int32, (LANES, LANES), 1)
    base = jnp.dot(padded, jnp.where(r < c, 1.0, 0.0), precision=lax.Precision.HIGHEST,
                   preferred_element_type=F32)

    route = route_ref[...]
    lane = lax.broadcasted_iota(jnp.int32, route.shape, 1).astype(F32)
    rk = rk_ref[...]
    base_row = base[0:1, :]
    pos1 = _lane_pick(base_row, lane, _lane_pick(route, lane, 0.0)) + _lane_pick(rk, lane, 0.0)
    pos2 = _lane_pick(base_row, lane, _lane_pick(route, lane, 1.0)) + _lane_pick(rk, lane, 1.0)
    pos_ref[...] = jnp.where(lane == 0.0, pos1, jnp.where(lane == 1.0, pos2, 0.0)).astype(jnp.int32)

    @pl.when(pl.program_id(0) == 0)
    def _():
        nt = tmap_ref.shape[0]
        tlane = lax.broadcasted_iota(jnp.int32, (nt, LANES), 1)
        start = (lax.broadcasted_iota(jnp.int32, (nt, LANES), 0) * MOE_TM).astype(F32)
        end_row = jnp.where(tlane < N_EXPERTS, base_row + padded[0:1, :], 3e38)
        expert = jnp.sum(jnp.where(end_row <= start, 1.0, 0.0), axis=-1, keepdims=True)
        expert_c = jnp.minimum(expert, N_EXPERTS - 1.0)
        tl = tlane.astype(F32)
        left = _lane_pick(cnt[0:1, :], tl, expert_c) - (start[:, 0:1] - _lane_pick(base_row, tl, expert_c))
        valid = jnp.where(expert < N_EXPERTS, jnp.clip(left, 0.0, float(MOE_TM)), 0.0)
        tmap_ref[...] = jnp.where(tlane == 0, expert_c, jnp.where(tlane == 1, valid, 0.0)).astype(jnp.int32)


def _expert_plan(cnt, route, rk, n_tiles):
    n = route.shape[0]
    tb = min(256, n)
    nt_pad = -(-n_tiles // 8) * 8
    return pl.pallas_call(
        _plan_kernel,
        grid=(n // tb,),
        in_specs=[pl.BlockSpec((8, LANES), lambda i: (0, 0)),
                  pl.BlockSpec((tb, LANES), lambda i: (i, 0)),
                  pl.BlockSpec((tb, LANES), lambda i: (i, 0))],
        out_specs=[pl.BlockSpec((tb, LANES), lambda i: (i, 0)),
                   pl.BlockSpec((nt_pad, LANES), lambda i: (0, 0))],
        out_shape=[jax.ShapeDtypeStruct((n, LANES), jnp.int32),
                   jax.ShapeDtypeStruct((nt_pad, LANES), jnp.int32)],
        compiler_params=_cparams(("arbitrary",)),
        name="expert_plan",
    )(cnt, route, rk)


def _sc_mesh():
    return plsc.VectorSubcoreMesh(core_axis_name="c", subcore_axis_name="s")


def _sc_scatter_rows(src, idx, out_rows):
    m = idx.shape[0]
    n_src_win = src.shape[0] // SC_WINDOW

    @functools.partial(pl.kernel, out_type=jax.ShapeDtypeStruct((out_rows, src.shape[1]), src.dtype),
                       mesh=_sc_mesh(), scratch_types=[])
    def scatter(x_hbm, i_hbm, o_hbm):
        def body(x_vmem, i_vmem):
            pltpu.sync_copy(x_vmem, o_hbm.at[i_vmem.at[0]])

        pltpu.emit_pipeline(
            body, grid=(m // SC_WINDOW,),
            in_specs=[pl.BlockSpec((SC_WINDOW, src.shape[1]), lambda i: (i % n_src_win, 0)),
                      pl.BlockSpec((1, SC_WINDOW), lambda i: (0, i))],
            out_specs=[], core_axis_name=("c", "s"),
            dimension_semantics=(pltpu.PARALLEL,))(x_hbm, i_hbm)

    return scatter(src, idx.reshape(1, m))


def _sc_gather_rows(table, idx):
    m = idx.shape[0]

    @functools.partial(pl.kernel, out_type=jax.ShapeDtypeStruct((m, table.shape[1]), table.dtype),
                       mesh=_sc_mesh(), scratch_types=[])
    def gather(x_hbm, i_hbm, o_hbm):
        def body(i_vmem, o_vmem):
            pltpu.sync_copy(x_hbm.at[i_vmem.at[0]], o_vmem)

        pltpu.emit_pipeline(
            body, grid=(m // SC_WINDOW,),
            in_specs=[pl.BlockSpec((1, SC_WINDOW), lambda i: (0, i))],
            out_specs=[pl.BlockSpec((SC_WINDOW, table.shape[1]), lambda i: (i, 0))],
            core_axis_name=("c", "s"),
            dimension_semantics=(pltpu.PARALLEL,))(i_hbm, o_hbm)

    return gather(table, idx.reshape(1, m))


def _sub_row_index(pos):
    pieces = D_MODEL // SC_SUB
    return (pos[:, None] * pieces + jnp.arange(pieces, dtype=jnp.int32)[None, :]).reshape(-1)


def _grouped_kernel(te_ref, tv_ref, x_ref, wg_ref, wu_ref, wd_ref, o_ref):
    valid = tv_ref[pl.program_id(0)]

    @pl.when(valid > 0)
    def _():
        row = lax.broadcasted_iota(jnp.int32, x_ref.shape, 0)
        x = jnp.where(row < valid, x_ref[...], 0.0).astype(BF16)
        a = jnp.dot(x, wg_ref[0].astype(BF16), preferred_element_type=F32)
        u = jnp.dot(x, wu_ref[0].astype(BF16), preferred_element_type=F32)
        hm = (_silu(a) * u).astype(BF16)
        o_ref[...] = jnp.dot(hm, wd_ref[0].astype(BF16), preferred_element_type=F32)

    @pl.when(valid <= 0)
    def _():
        o_ref[...] = jnp.zeros_like(o_ref)


def _grouped_experts(tile_expert, tile_valid, xs, wg, wu, wd):
    n_tiles = tile_expert.shape[0]
    return pl.pallas_call(
        _grouped_kernel,
        grid_spec=pltpu.PrefetchScalarGridSpec(
            num_scalar_prefetch=2,
            grid=(n_tiles,),
            in_specs=[pl.BlockSpec((MOE_TM, D_MODEL), lambda i, te, tv: (i, 0)),
                      pl.BlockSpec((1, D_MODEL, D_EXPERT), lambda i, te, tv: (te[i], 0, 0)),
                      pl.BlockSpec((1, D_MODEL, D_EXPERT), lambda i, te, tv: (te[i], 0, 0)),
                      pl.BlockSpec((1, D_EXPERT, D_MODEL), lambda i, te, tv: (te[i], 0, 0))],
            out_specs=pl.BlockSpec((MOE_TM, D_MODEL), lambda i, te, tv: (i, 0))),
        out_shape=jax.ShapeDtypeStruct(xs.shape, F32),
        compiler_params=_cparams(("arbitrary",)),
        name="grouped_experts",
    )(tile_expert, tile_valid, xs, wg, wu, wd)


def _combine_kernel(x1_ref, y_ref, route_ref, mod_ref, o_ref, *, tiles_per_batch):
    b = pl.program_id(0) // tiles_per_batch
    gt2 = mod_ref[pl.ds(b, 1), 5 * D_MODEL:6 * D_MODEL]
    route = route_ref[...]
    lane = lax.broadcasted_iota(jnp.int32, route.shape, 1).astype(F32)
    y = _lane_pick(route, lane, 2.0) * y_ref[0] + _lane_pick(route, lane, 3.0) * y_ref[1]
    o_ref[...] = x1_ref[...] + gt2 * y


def _combine(x1, y2, route, mod_l, seq):
    n = x1.shape[0]
    tm = min(1024, seq)
    return pl.pallas_call(
        functools.partial(_combine_kernel, tiles_per_batch=seq // tm),
        grid=(n // tm,),
        in_specs=[pl.BlockSpec((tm, D_MODEL), lambda i: (i, 0)),
                  pl.BlockSpec((2, tm, D_MODEL), lambda i: (0, i, 0)),
                  pl.BlockSpec((tm, LANES), lambda i: (i, 0)),
                  pl.BlockSpec(mod_l.shape, lambda i: (0, 0))],
        out_specs=pl.BlockSpec((tm, D_MODEL), lambda i: (i, 0)),
        out_shape=jax.ShapeDtypeStruct((n, D_MODEL), F32),
        compiler_params=_cparams(("arbitrary",)),
        name="moe_combine",
    )(x1, y2, route, mod_l)


def _moe(h2, route, x1, mod_l, wg, wu, wd, seq):
    n = h2.shape[0]
    n_tiles = (2 * n) // MOE_TM + N_EXPERTS
    rk, cnt = _expert_ranks(route)
    pos, tmap = _expert_plan(cnt, route, rk, n_tiles)
    idx = jnp.concatenate([_sub_row_index(pos[:, 0]), _sub_row_index(pos[:, 1])])
    xs = _sc_scatter_rows(h2.reshape(-1, SC_SUB), idx, n_tiles * MOE_TM * (D_MODEL // SC_SUB))
    ys = _grouped_experts(tmap[:n_tiles, 0], tmap[:n_tiles, 1], xs.reshape(n_tiles * MOE_TM, D_MODEL),
                          wg, wu, wd)
    y2 = _sc_gather_rows(ys.reshape(-1, SC_SUB), idx).reshape(2, n, D_MODEL)
    return _combine(x1, y2, route, mod_l, seq)


def _final_norm_kernel(x_ref, g_ref, o_ref):
    x = x_ref[...]
    ms = jnp.mean(x * x, axis=-1, keepdims=True)
    o_ref[...] = x * lax.rsqrt(ms + EPS) * g_ref[...]


def _final_norm(x2, g, seq):
    n = x2.shape[0]
    tm = min(1024, seq)
    return pl.pallas_call(
        _final_norm_kernel,
        grid=(n // tm,),
        in_specs=[pl.BlockSpec((tm, D_MODEL), lambda i: (i, 0)),
                  pl.BlockSpec((1, D_MODEL), lambda i: (0, 0))],
        out_specs=pl.BlockSpec((tm, D_MODEL), lambda i: (i, 0)),
        out_shape=jax.ShapeDtypeStruct((n, D_MODEL), F32),
        compiler_params=_cparams(("arbitrary",)),
        name="final_norm",
    )(x2, g.reshape(1, D_MODEL))


def _pack_w_in(w_in_l):
    offs = [0]
    for s in IN_SPLITS:
        offs.append(offs[-1] + s)
    (aq, ak, av, iq, ik, iw, bq, bf, bi, bg, cq, ck, cv, cg, ga, gb, gc) = [
        w_in_l[:, offs[i]:offs[i + 1]] for i in range(len(IN_SPLITS))]

    def even_odd(w):
        w4 = w.reshape(D_MODEL, C_HEADS, C_QK_DIM // 2, 2)
        return jnp.concatenate([w4[..., 0], w4[..., 1]], axis=-1).reshape(D_MODEL, C_QK_WIDTH)

    w16 = jnp.concatenate(
        [cv, cg, aq * (A_HEAD_DIM ** -0.5), ak, bq, bi, bg, even_odd(cq),
         even_odd(ck) * (C_QK_DIM ** -0.5), ga, gb, gc], axis=1).astype(BF16)
    iq_p = jnp.pad(iq.reshape(D_MODEL, IDX_HEADS, IDX_DIM),
                   ((0, 0), (0, 0), (0, LANES - IDX_DIM))).reshape(D_MODEL, IDX_HEADS * LANES)
    w32 = jnp.concatenate(
        [bf, iq_p, jnp.pad(ik, ((0, 0), (0, LANES - IDX_DIM))),
         jnp.pad(iw, ((0, 0), (0, LANES - IDX_HEADS)))], axis=1).astype(BF16)
    return w16, w32, av.T.astype(BF16)


def _split_bf16(w):
    hi = w.astype(BF16)
    return hi, (w - hi.astype(F32)).astype(BF16)


def kernel(x, c, rel_bias, hgrn_lb_raw, norm1_g, norm2_g, ada_w, ada_b, w_in, hgrn_norm_g, w_branch_a,
           w_branch_b, w_branch_c, w_out, router_group_w, router_group_b, router_expert_w,
           router_expert_b, expert_w_gate, expert_w_up, expert_w_down, final_norm_g):
    bsz, seq, _ = x.shape
    depth = w_in.shape[0]
    n = bsz * seq
    x2 = x.reshape(n, D_MODEL)
    tq = min(DSA_TQ, seq)

    lb_all = _hgrn_lower_bounds(hgrn_lb_raw)
    c_pad = jnp.pad(c, ((0, (-bsz) % 8), (0, 0)))
    mod = _ada_mod(c_pad, ada_w, ada_b)
    bias_tiles = _bias_tiles(rel_bias, tq)
    ret_tables = _retention_tables(seq)

    for l in range(depth):
        w16, w32, wvt = _pack_w_in(w_in[l])
        p16 = _norm_project(x2, mod[l], norm1_g[l], w16, BF16, 1024, seq, "proj_bf16")
        p32 = _norm_project(x2, mod[l], norm1_g[l], w32, F32, 768, seq, "proj_f32")
        vt = _norm_project_t(x2, mod[l], norm1_g[l], wvt, tq, seq, "proj_vt")
        o_a = _dsa_attention(p16, p32, vt, bias_tiles, bsz, seq)
        o_b = _hgrn2(p16, p32, lb_all[l], hgrn_norm_g[l], bsz, seq)
        o_c = _retention(p16, ret_tables, bsz, seq)
        wr = jnp.concatenate([router_group_w[l], router_expert_w[l],
                              jnp.zeros((D_MODEL, LANES - N_GROUPS - N_EXPERTS), F32)], axis=1)
        br = jnp.concatenate([router_group_b[l], router_expert_b[l],
                              jnp.zeros((LANES - N_GROUPS - N_EXPERTS,), F32)]).reshape(1, LANES)
        wr_hi, wr_lo = _split_bf16(wr)
        x1, h2, route = _merge(o_a, o_b, o_c, p16, x2, mod[l], norm2_g[l],
                               w_branch_a[l].astype(BF16), w_branch_b[l].astype(BF16),
                               w_branch_c[l].astype(BF16), w_out[l].astype(BF16),
                               wr_hi, wr_lo, br, seq)
        x2 = _moe(h2, route, x1, mod[l], expert_w_gate[l], expert_w_up[l], expert_w_down[l], seq)

    return _final_norm(x2, final_norm_g, seq).reshape(bsz, seq, D_MODEL)
```

```python
import functools
import math

import jax
import jax.numpy as jnp
from jax import lax
from jax.experimental import pallas as pl
from jax.experimental.pallas import tpu as pltpu
from jax.experimental.pallas import tpu_sc as plsc

F32 = jnp.float32
BF16 = jnp.bfloat16

D_MODEL = 1024
A_HEADS = 8
A_HEAD_DIM = 128
IDX_HEADS = 8
IDX_DIM = 64
TOPK_MAX = 256
REL_BUCKETS = 32
REL_MAX_DIST = 128
B_HEADS = 8
B_HEAD_DIM = 128
C_HEADS = 4
C_QK_DIM = 256
C_V_DIM = 512
N_GROUPS = 4
EXPERTS_PER_GROUP = 8
N_EXPERTS = 32
D_EXPERT = 512
EPS = 1e-6

A_WIDTH = A_HEADS * A_HEAD_DIM
B_WIDTH = B_HEADS * B_HEAD_DIM
C_QK_WIDTH = C_HEADS * C_QK_DIM
C_V_WIDTH = C_HEADS * C_V_DIM
IN_SPLITS = (A_WIDTH, A_WIDTH, A_WIDTH, IDX_HEADS * IDX_DIM, IDX_DIM, IDX_HEADS,
             B_WIDTH, B_WIDTH, B_WIDTH, B_WIDTH,
             C_QK_WIDTH, C_QK_WIDTH, C_V_WIDTH, C_V_WIDTH,
             D_MODEL, D_MODEL, D_MODEL)

LANES = 128
VMEM_LIMIT = 56 * 1024 * 1024

P16_CV, P16_CG = 0, 2
P16_AQ, P16_AK, P16_BQ, P16_BI, P16_BG, P16_CQ, P16_CK, P16_GA, P16_GB, P16_GC = range(4, 14)
P32_BF = 0
P32_IQ = 1024
P32_IK = 2048
P32_IW = 2176

DSA_TQ = 256
HGRN_L = 256
HGRN_C = 64
HGRN_SB = 16
RET_C = 128
KEY_NEG_INF = -2139095041
INT_MIN = -2147483648
MASK_NEG = -1e30
COUNT_CHAINS = 4
MOE_TM = 256
SC_WINDOW = 128
SC_SUB = 256
SC_PIECES = D_MODEL // SC_SUB

NT_DIMS = (((1,), (1,)), ((), ()))
TN_DIMS = (((0,), (0,)), ((), ()))


def _cparams(sem):
    return pltpu.CompilerParams(dimension_semantics=sem, vmem_limit_bytes=VMEM_LIMIT)


def _silu(x):
    return x * jax.nn.sigmoid(x)


def _lb_kernel(raw_ref, o_ref):
    raw = raw_ref[...]
    m = jnp.max(raw, axis=0, keepdims=True)
    e = jnp.exp(raw - m)
    soft = e / jnp.sum(e, axis=0, keepdims=True)
    run = jnp.zeros_like(soft[0:1])
    for l in range(raw.shape[0]):
        run = run + soft[l:l + 1]
        o_ref[l:l + 1, :] = run - soft[0:1]


def _hgrn_lower_bounds(raw):
    return pl.pallas_call(
        _lb_kernel, out_shape=jax.ShapeDtypeStruct(raw.shape, F32), name="hgrn_lb")(raw)


def _ada_kernel(c_ref, w_ref, b_ref, o_ref):
    a = _silu(c_ref[...])
    o_ref[0] = jnp.dot(a, w_ref[0], precision=lax.Precision.HIGHEST,
                       preferred_element_type=F32) + b_ref[0]


def _ada_mod(c_pad, ada_w, ada_b):
    depth = ada_w.shape[0]
    rows = c_pad.shape[0]
    return pl.pallas_call(
        _ada_kernel,
        grid=(depth, 6),
        in_specs=[pl.BlockSpec((rows, D_MODEL), lambda l, j: (0, 0)),
                  pl.BlockSpec((1, D_MODEL, D_MODEL), lambda l, j: (l, 0, j)),
                  pl.BlockSpec((1, 1, D_MODEL), lambda l, j: (l, 0, j))],
        out_specs=pl.BlockSpec((1, rows, D_MODEL), lambda l, j: (l, 0, j)),
        out_shape=jax.ShapeDtypeStruct((depth, rows, 6 * D_MODEL), F32),
        compiler_params=_cparams(("arbitrary", "arbitrary")),
        name="ada_mod",
    )(c_pad, ada_w, ada_b.reshape(depth, 1, 6 * D_MODEL))


def _rms_mod(x, g, sc, sh):
    ms = jnp.mean(x * x, axis=-1, keepdims=True)
    return (x * lax.rsqrt(ms + EPS) * g) * (1.0 + sc) + sh


def _norm1(x_ref, mod_ref, g_ref, b):
    sh = mod_ref[pl.ds(b, 1), 0:D_MODEL]
    sc = mod_ref[pl.ds(b, 1), D_MODEL:2 * D_MODEL]
    return _rms_mod(x_ref[...], g_ref[...], sc, sh).astype(BF16)


def _proj_kernel(x_ref, mod_ref, g_ref, w_ref, o_ref, h_ref, *, tiles_per_batch):
    @pl.when(pl.program_id(1) == 0)
    def _():
        h_ref[...] = _norm1(x_ref, mod_ref, g_ref, pl.program_id(0) // tiles_per_batch)

    o_ref[...] = jnp.dot(h_ref[...], w_ref[...], preferred_element_type=F32).astype(o_ref.dtype)


def _norm_project(x2, mod_l, g, w, out_dtype, tn, seq, name):
    n = x2.shape[0]
    width = w.shape[1]
    tm = min(1024, seq)
    return pl.pallas_call(
        functools.partial(_proj_kernel, tiles_per_batch=seq // tm),
        grid=(n // tm, width // tn),
        in_specs=[pl.BlockSpec((tm, D_MODEL), lambda i, j: (i, 0)),
                  pl.BlockSpec(mod_l.shape, lambda i, j: (0, 0)),
                  pl.BlockSpec((1, D_MODEL), lambda i, j: (0, 0)),
                  pl.BlockSpec((D_MODEL, tn), lambda i, j: (0, j))],
        out_specs=pl.BlockSpec((tm, tn), lambda i, j: (i, j)),
        out_shape=jax.ShapeDtypeStruct((n, width), out_dtype),
        scratch_shapes=[pltpu.VMEM((tm, D_MODEL), BF16)],
        compiler_params=_cparams(("arbitrary", "arbitrary")),
        name=name,
    )(x2, mod_l, g.reshape(1, D_MODEL), w)


def _proj_t_kernel(x_ref, mod_ref, g_ref, wt_ref, o_ref, *, tiles_per_batch, chunk):
    h = _norm1(x_ref, mod_ref, g_ref, pl.program_id(0) // tiles_per_batch)
    res = lax.dot_general(wt_ref[...], h, NT_DIMS, preferred_element_type=F32)
    for ci in range(o_ref.shape[0]):
        o_ref[ci] = res[:, ci * chunk:(ci + 1) * chunk].astype(o_ref.dtype)


def _norm_project_t(x2, mod_l, g, wt, chunk, seq, name):
    n = x2.shape[0]
    cols = wt.shape[0]
    tm = min(1024, seq)
    return pl.pallas_call(
        functools.partial(_proj_t_kernel, tiles_per_batch=seq // tm, chunk=chunk),
        grid=(n // tm,),
        in_specs=[pl.BlockSpec((tm, D_MODEL), lambda i: (i, 0)),
                  pl.BlockSpec(mod_l.shape, lambda i: (0, 0)),
                  pl.BlockSpec((1, D_MODEL), lambda i: (0, 0)),
                  pl.BlockSpec((cols, D_MODEL), lambda i: (0, 0))],
        out_specs=pl.BlockSpec((tm // chunk, cols, chunk), lambda i: (i, 0, 0)),
        out_shape=jax.ShapeDtypeStruct((n // chunk, cols, chunk), BF16),
        compiler_params=_cparams(("arbitrary",)),
        name=name,
    )(x2, mod_l, g.reshape(1, D_MODEL), wt)


def _dsa_kernel(q_ref, iq_ref, iw_ref, k_ref, vt_ref, ik_ref, bias_ref, o_ref,
                key_ref, madd_ref, qt_ref, iqt_ref, iwt_ref, m_ref, l_ref, acc_ref, s_ref, *, tq, topk):
    qi = pl.program_id(1)
    nck = qi + 1
    idx_scale = (IDX_HEADS * IDX_DIM) ** -0.5

    for h in range(A_HEADS):
        hs = slice(h * LANES, (h + 1) * LANES)
        qt_ref[hs, :] = q_ref[:, hs].astype(F32).T.astype(BF16)
        iqt_ref[hs, :] = iq_ref[:, hs].T.astype(BF16)
    iwt_ref[...] = (iw_ref[...] * idx_scale).T

    krow = lax.broadcasted_iota(jnp.int32, (tq, tq), 0)
    qcol = lax.broadcasted_iota(jnp.int32, (tq, tq), 1)

    def score_chunk(c, carry):
        off = pl.multiple_of(c * tq, tq)
        ikc = ik_ref[pl.ds(off, tq), :].astype(BF16)
        acc = jnp.zeros((tq, tq), F32)
        for h in range(IDX_HEADS):
            s = jnp.dot(ikc, iqt_ref[h * LANES:(h + 1) * LANES, :], preferred_element_type=F32)
            acc = acc + jnp.maximum(s, 0.0) * iwt_ref[h:h + 1, :]
        acc = jnp.where(acc == 0.0, 0.0, acc)
        acc = jnp.where(krow + (c - qi) * tq <= qcol, acc, -jnp.inf)
        kb = pltpu.bitcast(acc, jnp.int32)
        key_ref[c] = jnp.where(kb < 0, kb ^ jnp.int32(0x7FFFFFFF), kb)
        return carry

    lax.fori_loop(0, nck, score_chunk, 0)

    def count(pred_fn):
        def body(c, parts):
            hit = jnp.where(pred_fn(key_ref[c]), 1.0, 0.0)
            parts = list(parts)
            for r in range(tq // 8):
                parts[r % COUNT_CHAINS] = parts[r % COUNT_CHAINS] + hit[r * 8:(r + 1) * 8, :]
            return tuple(parts)

        parts = lax.fori_loop(0, nck, body, (jnp.zeros((8, tq), F32),) * COUNT_CHAINS)
        return jnp.sum(sum(parts), axis=0, keepdims=True)

    def bit_step(i, theta):
        cand = theta + jnp.left_shift(jnp.int32(1), 31 - i)
        return jnp.where(count(lambda kc: kc >= cand) >= topk, cand, theta)

    theta = lax.fori_loop(0, 32, bit_step, jnp.full((1, tq), INT_MIN, jnp.int32))
    theta = jnp.maximum(theta, KEY_NEG_INF + 1)

    def mask_chunk(c, cnt):
        ge = key_ref[c] >= theta
        madd_ref[c] = jnp.where(ge, 0.0, MASK_NEG)
        return cnt + jnp.sum(jnp.where(ge, 1.0, 0.0), axis=0, keepdims=True)

    cnt_ge = lax.fori_loop(0, nck, mask_chunk, jnp.zeros((1, tq), F32))

    @pl.when(jnp.max(cnt_ge) > topk)
    def _():
        need_eq = topk - count(lambda kc: kc > theta)
        incl = jnp.where(krow >= qcol, 1.0, 0.0).astype(BF16)

        def tie_chunk(c, run):
            kc = key_ref[c]
            eq = kc == theta
            eqf = jnp.where(eq, 1.0, 0.0)
            pref = jnp.dot(incl, eqf.astype(BF16), preferred_element_type=F32) + run
            eq_add = jnp.where(pref <= need_eq, 0.0, MASK_NEG)
            madd_ref[c] = jnp.where(eq, eq_add, jnp.where(kc > theta, 0.0, MASK_NEG))
            return run + jnp.sum(eqf, axis=0, keepdims=True)

        lax.fori_loop(0, nck, tie_chunk, jnp.zeros((1, tq), F32))

    m_ref[...] = jnp.full(m_ref.shape, -jnp.inf, F32)
    l_ref[...] = jnp.zeros(l_ref.shape, F32)
    acc_ref[...] = jnp.zeros(acc_ref.shape, F32)

    def attend(c, lag):
        off = pl.multiple_of(c * tq, tq)
        head_slices = [slice(h * A_HEAD_DIM, (h + 1) * A_HEAD_DIM) for h in range(A_HEADS)]
        for h, hs in enumerate(head_slices):
            s_ref[h] = jnp.dot(k_ref[pl.ds(off, tq), hs], qt_ref[hs, :], preferred_element_type=F32)
        for h, hs in enumerate(head_slices):
            s = s_ref[h] + madd_ref[c]
            if lag is not None:
                s = s + bias_ref[h, lag]
            m_old = m_ref[h]
            m_new = jnp.maximum(m_old, jnp.max(s, axis=0, keepdims=True))
            alpha = jnp.exp(m_old - m_new)
            p = jnp.exp(s - m_new)
            l_ref[h] = alpha * l_ref[h] + jnp.sum(p, axis=0, keepdims=True)
            acc_ref[h] = alpha * acc_ref[h] + jnp.dot(vt_ref[c, hs, :], p.astype(BF16),
                                                      preferred_element_type=F32)
            m_ref[h] = m_new

    def far_chunk(c, carry):
        attend(c, None)
        return carry

    lax.fori_loop(0, jnp.maximum(qi - 1, 0), far_chunk, 0)

    @pl.when(qi >= 1)
    def _():
        attend(qi - 1, 1)

    attend(qi, 0)

    for h in range(A_HEADS):
        o = acc_ref[h] * (1.0 / l_ref[h])
        o_ref[:, h * A_HEAD_DIM:(h + 1) * A_HEAD_DIM] = o.T.astype(o_ref.dtype)


def _dsa_attention(p16, p32, vt, bias_tiles, bsz, seq):
    tq = min(DSA_TQ, seq)
    nq = seq // tq
    topk = min(TOPK_MAX, seq // 4)
    n = bsz * seq
    one = pl.Buffered(1)
    return pl.pallas_call(
        functools.partial(_dsa_kernel, tq=tq, topk=topk),
        grid=(bsz, nq),
        in_specs=[
            pl.BlockSpec((tq, A_WIDTH), lambda b, i: (b * nq + i, P16_AQ)),
            pl.BlockSpec((tq, 1024), lambda b, i: (b * nq + i, P32_IQ // 1024)),
            pl.BlockSpec((tq, LANES), lambda b, i: (b * nq + i, P32_IW // LANES)),
            pl.BlockSpec((seq, A_WIDTH), lambda b, i: (b, P16_AK), pipeline_mode=one),
            pl.BlockSpec((nq, A_WIDTH, tq), lambda b, i: (b, 0, 0), pipeline_mode=one),
            pl.BlockSpec((seq, LANES), lambda b, i: (b, P32_IK // LANES), pipeline_mode=one),
            pl.BlockSpec(bias_tiles.shape, lambda b, i: (0, 0, 0, 0), pipeline_mode=one),
        ],
        out_specs=pl.BlockSpec((tq, A_WIDTH), lambda b, i: (b * nq + i, 0)),
        out_shape=jax.ShapeDtypeStruct((n, A_WIDTH), BF16),
        scratch_shapes=[pltpu.VMEM((nq, tq, tq), jnp.int32),
                        pltpu.VMEM((nq, tq, tq), F32),
                        pltpu.VMEM((A_WIDTH, tq), BF16),
                        pltpu.VMEM((IDX_HEADS * LANES, tq), BF16),
                        pltpu.VMEM((LANES, tq), F32),
                        pltpu.VMEM((A_HEADS, 1, tq), F32),
                        pltpu.VMEM((A_HEADS, 1, tq), F32),
                        pltpu.VMEM((A_HEADS, A_HEAD_DIM, tq), F32),
                        pltpu.VMEM((A_HEADS, tq, tq), F32)],
        compiler_params=_cparams(("arbitrary", "arbitrary")),
        name="dsa_attention",
    )(p16, p32, p32, p16, vt, p32, bias_tiles)


def _t5_bucket(rel):
    max_exact = REL_BUCKETS // 2
    relf = jnp.maximum(rel, 1).astype(F32)
    large = max_exact + (jnp.log(relf / max_exact) / math.log(REL_MAX_DIST / max_exact)
                         * (REL_BUCKETS - max_exact)).astype(jnp.int32)
    large = jnp.minimum(large, REL_BUCKETS - 1)
    return jnp.where(rel < max_exact, rel, large)


def _bias_tiles(rel_bias, tq):
    assert tq >= REL_MAX_DIST
    key = jnp.arange(tq, dtype=jnp.int32)[:, None]
    qry = jnp.arange(tq, dtype=jnp.int32)[None, :]
    tiles = [rel_bias[_t5_bucket(jnp.maximum(lag * tq + qry - key, 0))] for lag in range(2)]
    tiles = jnp.stack(tiles, axis=0) - rel_bias[REL_BUCKETS - 1][None, None, None, :]
    return jnp.transpose(tiles, (3, 0, 1, 2)).astype(F32)


def _hgrn_kernel(q_ref, f_ref, i_ref, g_ref, lb_ref, ng_ref, tril_ref, o_ref, st_ref, *, rows):
    @pl.when(pl.program_id(2) == 0)
    def _():
        st_ref[...] = jnp.zeros_like(st_ref)

    lb = lb_ref[0]
    f = lb + (1.0 - lb) * jax.nn.sigmoid(f_ref[...])
    logf = jnp.log(f)
    kk = 1.0 - f
    g1 = logf.astype(BF16)
    r1 = logf - g1.astype(F32)
    g2 = r1.astype(BF16)
    g3 = (r1 - g2.astype(F32)).astype(BF16)
    tril = tril_ref[...]
    bcum = (jnp.dot(tril, g1, preferred_element_type=F32)
            + jnp.dot(tril, g2, preferred_element_type=F32)
            + jnp.dot(tril, g3, preferred_element_type=F32))

    srow = lax.broadcasted_iota(jnp.int32, (HGRN_C, B_HEAD_DIM), 0)
    trow = lax.broadcasted_iota(jnp.int32, (HGRN_SB, HGRN_C), 0)
    scol = lax.broadcasted_iota(jnp.int32, (HGRN_SB, HGRN_C), 1)
    ng = ng_ref[...]

    for n in range(rows // HGRN_C):
        cs = slice(n * HGRN_C, (n + 1) * HGRN_C)
        bc = bcum[cs]
        qc = q_ref[cs, :].astype(F32)
        kc = kk[cs]
        vc = i_ref[cs, :]
        st = st_ref[...]
        qb = (qc * jnp.exp(bc)).astype(BF16)
        o_inter = lax.dot_general(qb, st.astype(BF16), NT_DIMS, preferred_element_type=F32)
        a_rows = []
        for sb in range(HGRN_C // HGRN_SB):
            s0 = sb * HGRN_SB
            beta = bc[s0 - 1:s0] if sb > 0 else jnp.zeros((1, B_HEAD_DIM), F32)
            qs = (qc[s0:s0 + HGRN_SB] * jnp.exp(bc[s0:s0 + HGRN_SB] - beta)).astype(BF16)
            expo = jnp.where(srow < s0 + HGRN_SB, beta - bc, -jnp.inf)
            ks = (kc * jnp.exp(expo)).astype(BF16)
            a = lax.dot_general(qs, ks, NT_DIMS, preferred_element_type=F32)
            a_rows.append(jnp.where(scol <= trow + s0, a, 0.0))
        attn = jnp.concatenate(a_rows, axis=0).astype(BF16)
        o = o_inter + jnp.dot(attn, vc, preferred_element_type=F32)
        blast = bc[HGRN_C - 1:HGRN_C]
        kdec = (kc * jnp.exp(blast - bc)).astype(BF16)
        st_ref[...] = st * jnp.exp(blast) + lax.dot_general(vc, kdec, TN_DIMS,
                                                            preferred_element_type=F32)
        ms = jnp.mean(o * o, axis=-1, keepdims=True)
        on = o * lax.rsqrt(ms + EPS) * ng
        o_ref[cs, :] = (on * _silu(g_ref[cs, :].astype(F32))).astype(o_ref.dtype)


def _hgrn2(p16, p32, lb_l, norm_g, bsz, seq):
    rows = min(HGRN_L, seq)
    nj = seq // rows
    n = bsz * seq
    r = jnp.arange(rows, dtype=jnp.int32)
    tril = ((r[:, None] >= r[None, :]) & (r[:, None] // HGRN_C == r[None, :] // HGRN_C)).astype(BF16)
    hb = B_WIDTH // LANES

    def col(base):
        return lambda b, h, j: (b * nj + j, base * hb + h)

    return pl.pallas_call(
        functools.partial(_hgrn_kernel, rows=rows),
        grid=(bsz, B_HEADS, nj),
        in_specs=[
            pl.BlockSpec((rows, LANES), col(P16_BQ)),
            pl.BlockSpec((rows, LANES), lambda b, h, j: (b * nj + j, P32_BF // LANES + h)),
            pl.BlockSpec((rows, LANES), col(P16_BI)),
            pl.BlockSpec((rows, LANES), col(P16_BG)),
            pl.BlockSpec((1, 1, LANES), lambda b, h, j: (h, 0, 0)),
            pl.BlockSpec((1, LANES), lambda b, h, j: (0, 0)),
            pl.BlockSpec((rows, rows), lambda b, h, j: (0, 0)),
        ],
        out_specs=pl.BlockSpec((rows, LANES), lambda b, h, j: (b * nj + j, h)),
        out_shape=jax.ShapeDtypeStruct((n, B_WIDTH), BF16),
        scratch_shapes=[pltpu.VMEM((B_HEAD_DIM, B_HEAD_DIM), F32)],
        compiler_params=_cparams(("arbitrary", "arbitrary", "arbitrary")),
        name="hgrn2",
    )(p16, p32, p16, p16, lb_l.reshape(B_HEADS, 1, B_HEAD_DIM), norm_g.reshape(1, B_HEAD_DIM), tril)


def _ret_kernel(q_ref, k_ref, v_ref, g_ref, cos_ref, sin_ref, idec_ref, qdec_ref, kdec_ref, cdec_ref,
                o_ref, st_ref):
    @pl.when(pl.program_id(2) == 0)
    def _():
        st_ref[...] = jnp.zeros_like(st_ref)

    cos = cos_ref[...]
    sin_signed = sin_ref[...]
    even = lax.broadcasted_iota(jnp.int32, cos.shape, 1) % 2 == 0

    def rot(a):
        swapped = jnp.where(even, pltpu.roll(a, C_QK_DIM - 1, 1), pltpu.roll(a, 1, 1))
        return a * cos + swapped * sin_signed

    qr = rot(q_ref[...].astype(F32))
    kr = rot(k_ref[...].astype(F32))
    v = v_ref[...]
    st = st_ref[...]
    attn = lax.dot_general(qr.astype(BF16), kr.astype(BF16), NT_DIMS,
                           preferred_element_type=F32) * idec_ref[0]
    o = (jnp.dot(attn.astype(BF16), v, preferred_element_type=F32)
         + jnp.dot((qr * qdec_ref[0]).astype(BF16), st.astype(BF16), preferred_element_type=F32))
    krd_t = (kr * kdec_ref[0]).T.astype(BF16)
    st_ref[...] = cdec_ref[0, 0:1, :] * st + jnp.dot(krd_t, v, preferred_element_type=F32)
    ms = jnp.mean(o * o, axis=-1, keepdims=True)
    o_ref[...] = (_silu(g_ref[...].astype(F32)) * (o * lax.rsqrt(ms + EPS))).astype(o_ref.dtype)


def _retention_tables(seq):
    pos = jnp.arange(seq, dtype=F32)
    theta = jnp.repeat(1.0 / (10000.0 ** jnp.linspace(0.0, 1.0, C_QK_DIM // 2)), 2)
    ang = pos[:, None] * theta[None, :]
    pair_sign = jnp.where(jnp.arange(C_QK_DIM) % 2 == 0, -1.0, 1.0)
    log_gamma = jnp.log(1.0 - 2.0 ** (-5.0 - jnp.arange(C_HEADS, dtype=F32)))
    idx = jnp.arange(RET_C, dtype=F32)
    causal = idx[:, None] >= idx[None, :]
    idec = jnp.exp(jnp.where(causal[None], (idx[:, None] - idx[None, :])[None] * log_gamma[:, None, None],
                             -jnp.inf))
    qdec = jnp.exp((idx + 1.0)[None, :] * log_gamma[:, None])[..., None]
    kdec = jnp.exp((RET_C - 1.0 - idx)[None, :] * log_gamma[:, None])[..., None]
    cdec = jnp.exp(RET_C * log_gamma)[:, None, None]
    return (jnp.cos(ang), jnp.sin(ang) * pair_sign[None, :], idec,
            jnp.broadcast_to(qdec, (C_HEADS, RET_C, C_QK_DIM)),
            jnp.broadcast_to(kdec, (C_HEADS, RET_C, C_QK_DIM)),
            jnp.broadcast_to(cdec, (C_HEADS, 8, C_V_DIM)))


def _retention(p16, tables, bsz, seq):
    cos, sin, idec, qdec, kdec, cdec = tables
    nj = seq // RET_C
    n = bsz * seq
    qk_b = 1024 // C_QK_DIM
    v_b = 1024 // C_V_DIM
    return pl.pallas_call(
        _ret_kernel,
        grid=(bsz, C_HEADS, nj),
        in_specs=[
            pl.BlockSpec((RET_C, C_QK_DIM), lambda b, h, j: (b * nj + j, P16_CQ * qk_b + h)),
            pl.BlockSpec((RET_C, C_QK_DIM), lambda b, h, j: (b * nj + j, P16_CK * qk_b + h)),
            pl.BlockSpec((RET_C, C_V_DIM), lambda b, h, j: (b * nj + j, P16_CV * v_b + h)),
            pl.BlockSpec((RET_C, C_V_DIM), lambda b, h, j: (b * nj + j, P16_CG * v_b + h)),
            pl.BlockSpec((RET_C, C_QK_DIM), lambda b, h, j: (j, 0)),
            pl.BlockSpec((RET_C, C_QK_DIM), lambda b, h, j: (j, 0)),
            pl.BlockSpec((1, RET_C, RET_C), lambda b, h, j: (h, 0, 0)),
            pl.BlockSpec((1, RET_C, C_QK_DIM), lambda b, h, j: (h, 0, 0)),
            pl.BlockSpec((1, RET_C, C_QK_DIM), lambda b, h, j: (h, 0, 0)),
            pl.BlockSpec((1, 8, C_V_DIM), lambda b, h, j: (h, 0, 0)),
        ],
        out_specs=pl.BlockSpec((RET_C, C_V_DIM), lambda b, h, j: (b * nj + j, h)),
        out_shape=jax.ShapeDtypeStruct((n, C_V_WIDTH), BF16),
        scratch_shapes=[pltpu.VMEM((C_QK_DIM, C_V_DIM), F32)],
        compiler_params=_cparams(("arbitrary", "arbitrary", "arbitrary")),
        name="retention",
    )(p16, p16, p16, p16, cos, sin, idec, qdec, kdec, cdec)


def _merge_kernel(oa_ref, ob_ref, oc_ref, ga_ref, gb_ref, gc_ref, x_ref, mod_ref, g2_ref,
                  wa_ref, wb_ref, wc_ref, wo_ref, wrh_ref, wrl_ref, br_ref,
                  x1_ref, h2_ref, route_ref, *, tiles_per_batch):
    b = pl.program_id(0) // tiles_per_batch

    def gated(o_ref, w_ref, g_ref):
        y = jnp.dot(o_ref[...], w_ref[...], preferred_element_type=F32)
        return jax.nn.sigmoid(g_ref[...].astype(F32)) * y

    merged = gated(oa_ref, wa_ref, ga_ref) + gated(ob_ref, wb_ref, gb_ref) + gated(oc_ref, wc_ref, gc_ref)
    y = jnp.dot(merged.astype(BF16), wo_ref[...], preferred_element_type=F32)
    gt1 = mod_ref[pl.ds(b, 1), 2 * D_MODEL:3 * D_MODEL]
    x1 = x_ref[...] + gt1 * y
    x1_ref[...] = x1
    sh2 = mod_ref[pl.ds(b, 1), 3 * D_MODEL:4 * D_MODEL]
    sc2 = mod_ref[pl.ds(b, 1), 4 * D_MODEL:5 * D_MODEL]
    h2 = _rms_mod(x1, g2_ref[...], sc2, sh2)
    h_hi = h2.astype(BF16)
    h_f = h_hi.astype(F32)
    for j in range(h2_ref.shape[0]):
        h2_ref[j] = h_f[:, j * SC_SUB:(j + 1) * SC_SUB]
    h_lo = (h2 - h_hi.astype(F32)).astype(BF16)
    logits = (jnp.dot(h_hi, wrh_ref[...], preferred_element_type=F32)
              + jnp.dot(h_lo, wrh_ref[...], preferred_element_type=F32)
              + jnp.dot(h_hi, wrl_ref[...], preferred_element_type=F32)) + br_ref[...]
    lane = lax.broadcasted_iota(jnp.int32, logits.shape, 1).astype(F32)
    neg_inf = -jnp.inf

    def first_argmax(vals):
        top = jnp.max(vals, axis=-1, keepdims=True)
        idx = jnp.min(jnp.where(vals == top, lane, float(LANES)), axis=-1, keepdims=True)
        return top, idx

    gl = jnp.where(lane < N_GROUPS, logits, neg_inf)
    gmax, gsel = first_argmax(gl)
    gprob = 1.0 / jnp.sum(jnp.exp(gl - gmax), axis=-1, keepdims=True)
    lo = N_GROUPS + EXPERTS_PER_GROUP * gsel
    el = jnp.where((lane >= lo) & (lane < lo + EXPERTS_PER_GROUP), logits, neg_inf)
    v1, i1 = first_argmax(el)
    el2 = jnp.where(lane == i1, neg_inf, el)
    v2, i2 = first_argmax(el2)
    e2 = jnp.exp(v2 - v1)
    den = 1.0 + e2
    route_ref[...] = jnp.where(lane == 0.0, i1 - N_GROUPS,
                               jnp.where(lane == 1.0, i2 - N_GROUPS,
                                         jnp.where(lane == 2.0, gprob / den,
                                                   jnp.where(lane == 3.0, gprob * (e2 / den), 0.0))))


def _merge(o_a, o_b, o_c, p16, x2, mod_l, g2, wa, wb, wc, wo, wr_hi, wr_lo, br, seq):
    n = x2.shape[0]
    tm = min(512, seq)
    one = pl.Buffered(1)

    def rows(width, cb=0):
        return pl.BlockSpec((tm, width), lambda i: (i, cb))

    def whole(a):
        return pl.BlockSpec(a.shape, lambda i: (0,) * a.ndim, pipeline_mode=one)

    return pl.pallas_call(
        functools.partial(_merge_kernel, tiles_per_batch=seq // tm),
        grid=(n // tm,),
        in_specs=[rows(A_WIDTH), rows(B_WIDTH), rows(C_V_WIDTH),
                  rows(D_MODEL, P16_GA), rows(D_MODEL, P16_GB), rows(D_MODEL, P16_GC),
                  rows(D_MODEL), whole(mod_l), pl.BlockSpec((1, D_MODEL), lambda i: (0, 0)),
                  whole(wa), whole(wb), whole(wc), whole(wo), whole(wr_hi), whole(wr_lo), whole(br)],
        out_specs=[rows(D_MODEL), pl.BlockSpec((SC_PIECES, tm, SC_SUB), lambda i: (0, i, 0)), rows(LANES)],
        out_shape=[jax.ShapeDtypeStruct((n, D_MODEL), F32),
                   jax.ShapeDtypeStruct((SC_PIECES, n, SC_SUB), F32),
                   jax.ShapeDtypeStruct((n, LANES), F32)],
        compiler_params=_cparams(("arbitrary",)),
        name="merge_route",
    )(o_a, o_b, o_c, p16, p16, p16, x2, mod_l, g2.reshape(1, D_MODEL), wa, wb, wc, wo, wr_hi, wr_lo, br)


def _lane_pick(vals, lane, idx):
    return jnp.sum(jnp.where(lane == idx, vals, 0.0), axis=-1, keepdims=True)


def _rank_kernel(route_ref, rk_ref, cnt_ref, run_ref):
    @pl.when(pl.program_id(0) == 0)
    def _():
        run_ref[...] = jnp.zeros_like(run_ref)

    route = route_ref[...]
    tb = route.shape[0]
    lane = lax.broadcasted_iota(jnp.int32, route.shape, 1).astype(F32)
    e1 = _lane_pick(route, lane, 0.0)
    e2 = _lane_pick(route, lane, 1.0)
    sel = jnp.where((lane == e1) | (lane == e2), 1.0, 0.0)
    r = lax.broadcasted_iota(jnp.int32, (tb, tb), 0)
    c = lax.broadcasted_iota(jnp.int32, (tb, tb), 1)
    before = jnp.where(c < r, 1.0, 0.0).astype(BF16)
    rank = jnp.dot(before, sel.astype(BF16), preferred_element_type=F32) + run_ref[0:1, :]
    rk_ref[...] = jnp.where(lane == 0.0, _lane_pick(rank, lane, e1),
                            jnp.where(lane == 1.0, _lane_pick(rank, lane, e2), 0.0))
    run_ref[...] = run_ref[...] + jnp.sum(sel, axis=0, keepdims=True)
    cnt_ref[...] = run_ref[...]


def _expert_ranks(route):
    n = route.shape[0]
    tb = min(256, n)
    return pl.pallas_call(
        _rank_kernel,
        grid=(n // tb,),
        in_specs=[pl.BlockSpec((tb, LANES), lambda i: (i, 0))],
        out_specs=[pl.BlockSpec((tb, LANES), lambda i: (i, 0)),
                   pl.BlockSpec((8, LANES), lambda i: (0, 0))],
        out_shape=[jax.ShapeDtypeStruct((n, LANES), F32), jax.ShapeDtypeStruct((8, LANES), F32)],
        scratch_shapes=[pltpu.VMEM((8, LANES), F32)],
        compiler_params=_cparams(("arbitrary",)),
        name="expert_ranks",
    )(route)


def _plan_kernel(cnt_ref, route_ref, rk_ref, pos_ref, tmap_ref):
    lane_i = lax.broadcasted_iota(jnp.int32, (8, LANES), 1)
    cnt = jnp.where(lane_i < N_EXPERTS, cnt_ref[...], 0.0)
    padded = jnp.floor((cnt + (MOE_TM - 1)) * (1.0 / MOE_TM)) * MOE_TM
    r = lax.broadcasted_iota(jnp.int32, (LANES, LANES), 0)
    c = lax.broadcasted_iota(jnp.int32, (LANES, LANES), 1)
    base = jnp.dot(padded, jnp.where(r < c, 1.0, 0.0), precision=lax.Precision.HIGHEST,
                   preferred_element_type=F32)

    route = route_ref[...]
    lane = lax.broadcasted_iota(jnp.int32, route.shape, 1).astype(F32)
    rk = rk_ref[...]
    base_row = base[0:1, :]
    pos1 = _lane_pick(base_row, lane, _lane_pick(route, lane, 0.0)) + _lane_pick(rk, lane, 0.0)
    pos2 = _lane_pick(base_row, lane, _lane_pick(route, lane, 1.0)) + _lane_pick(rk, lane, 1.0)
    pos_ref[...] = jnp.where(lane == 0.0, pos1, jnp.where(lane == 1.0, pos2, 0.0)).astype(jnp.int32)

    @pl.when(pl.program_id(0) == 0)
    def _():
        nt = tmap_ref.shape[0]
        tlane = lax.broadcasted_iota(jnp.int32, (nt, LANES), 1)
        start = (lax.broadcasted_iota(jnp.int32, (nt, LANES), 0) * MOE_TM).astype(F32)
        end_row = jnp.where(tlane < N_EXPERTS, base_row + padded[0:1, :], 3e38)
        expert = jnp.sum(jnp.where(end_row <= start, 1.0, 0.0), axis=-1, keepdims=True)
        expert_c = jnp.minimum(expert, N_EXPERTS - 1.0)
        tl = tlane.astype(F32)
        left = _lane_pick(cnt[0:1, :], tl, expert_c) - (start[:, 0:1] - _lane_pick(base_row, tl, expert_c))
        valid = jnp.where(expert < N_EXPERTS, jnp.clip(left, 0.0, float(MOE_TM)), 0.0)
        tmap_ref[...] = jnp.where(tlane == 0, expert_c, jnp.where(tlane == 1, valid, 0.0)).astype(jnp.int32)


def _expert_plan(cnt, route, rk, n_tiles):
    n = route.shape[0]
    tb = min(256, n)
    nt_pad = -(-n_tiles // 8) * 8
    return pl.pallas_call(
        _plan_kernel,
        grid=(n // tb,),
        in_specs=[pl.BlockSpec((8, LANES), lambda i: (0, 0)),
                  pl.BlockSpec((tb, LANES), lambda i: (i, 0)),
                  pl.BlockSpec((tb, LANES), lambda i: (i, 0))],
        out_specs=[pl.BlockSpec((tb, LANES), lambda i: (i, 0)),
                   pl.BlockSpec((nt_pad, LANES), lambda i: (0, 0))],
        out_shape=[jax.ShapeDtypeStruct((n, LANES), jnp.int32),
                   jax.ShapeDtypeStruct((nt_pad, LANES), jnp.int32)],
        compiler_params=_cparams(("arbitrary",)),
        name="expert_plan",
    )(cnt, route, rk)


def _sc_mesh():
    return plsc.VectorSubcoreMesh(core_axis_name="c", subcore_axis_name="s")


def _sc_scatter_rows(src, idx, out_rows):
    m = idx.shape[0]
    n_src_win = src.shape[0] // SC_WINDOW

    @functools.partial(pl.kernel, out_type=jax.ShapeDtypeStruct((out_rows, src.shape[1]), src.dtype),
                       mesh=_sc_mesh(), scratch_types=[])
    def scatter(x_hbm, i_hbm, o_hbm):
        def body(x_vmem, i_vmem):
            pltpu.sync_copy(x_vmem, o_hbm.at[i_vmem.at[0]])

        pltpu.emit_pipeline(
            body, grid=(m // SC_WINDOW,),
            in_specs=[pl.BlockSpec((SC_WINDOW, src.shape[1]), lambda i: (i % n_src_win, 0)),
                      pl.BlockSpec((1, SC_WINDOW), lambda i: (0, i))],
            out_specs=[], core_axis_name=("c", "s"),
            dimension_semantics=(pltpu.PARALLEL,))(x_hbm, i_hbm)

    return scatter(src, idx.reshape(1, m))


def _sc_gather_rows(table, idx):
    m = idx.shape[0]

    @functools.partial(pl.kernel, out_type=jax.ShapeDtypeStruct((m, table.shape[1]), table.dtype),
                       mesh=_sc_mesh(), scratch_types=[])
    def gather(x_hbm, i_hbm, o_hbm):
        def body(i_vmem, o_vmem):
            pltpu.sync_copy(x_hbm.at[i_vmem.at[0]], o_vmem)

        pltpu.emit_pipeline(
            body, grid=(m // SC_WINDOW,),
            in_specs=[pl.BlockSpec((1, SC_WINDOW), lambda i: (0, i))],
            out_specs=[pl.BlockSpec((SC_WINDOW, table.shape[1]), lambda i: (i, 0))],
            core_axis_name=("c", "s"),
            dimension_semantics=(pltpu.PARALLEL,))(i_hbm, o_hbm)

    return gather(table, idx.reshape(1, m))


def _piece_row_index(pos, rows):
    return (jnp.arange(SC_PIECES, dtype=jnp.int32)[:, None] * rows + pos[None, :]).reshape(-1)


def _grouped_kernel(te_ref, tv_ref, x_ref, wg_ref, wu_ref, wd_ref, o_ref):
    valid = tv_ref[pl.program_id(0)]

    @pl.when(valid > 0)
    def _():
        x = jnp.concatenate([x_ref[j] for j in range(SC_PIECES)], axis=1)
        row = lax.broadcasted_iota(jnp.int32, x.shape, 0)
        x = jnp.where(row < valid, x, 0.0).astype(BF16)
        a = jnp.dot(x, wg_ref[0].astype(BF16), preferred_element_type=F32)
        u = jnp.dot(x, wu_ref[0].astype(BF16), preferred_element_type=F32)
        hm = (_silu(a) * u).astype(BF16)
        o = jnp.dot(hm, wd_ref[0].astype(BF16), preferred_element_type=F32)
        for j in range(SC_PIECES):
            o_ref[j] = o[:, j * SC_SUB:(j + 1) * SC_SUB]

    @pl.when(valid <= 0)
    def _():
        o_ref[...] = jnp.zeros_like(o_ref)


def _grouped_experts(tile_expert, tile_valid, xs, wg, wu, wd):
    n_tiles = tile_expert.shape[0]
    rows_block = pl.BlockSpec((SC_PIECES, MOE_TM, SC_SUB), lambda i, te, tv: (0, i, 0))
    return pl.pallas_call(
        _grouped_kernel,
        grid_spec=pltpu.PrefetchScalarGridSpec(
            num_scalar_prefetch=2,
            grid=(n_tiles,),
            in_specs=[rows_block,
                      pl.BlockSpec((1, D_MODEL, D_EXPERT), lambda i, te, tv: (te[i], 0, 0)),
                      pl.BlockSpec((1, D_MODEL, D_EXPERT), lambda i, te, tv: (te[i], 0, 0)),
                      pl.BlockSpec((1, D_EXPERT, D_MODEL), lambda i, te, tv: (te[i], 0, 0))],
            out_specs=rows_block),
        out_shape=jax.ShapeDtypeStruct(xs.shape, F32),
        compiler_params=_cparams(("arbitrary",)),
        name="grouped_experts",
    )(tile_expert, tile_valid, xs, wg, wu, wd)


def _combine_kernel(x1_ref, y_ref, route_ref, mod_ref, o_ref, *, tiles_per_batch):
    b = pl.program_id(0) // tiles_per_batch
    gt2 = mod_ref[pl.ds(b, 1), 5 * D_MODEL:6 * D_MODEL]
    route = route_ref[...]
    lane = lax.broadcasted_iota(jnp.int32, route.shape, 1).astype(F32)
    w1 = _lane_pick(route, lane, 2.0)
    w2 = _lane_pick(route, lane, 3.0)
    y = jnp.concatenate([w1 * y_ref[0, j] + w2 * y_ref[1, j] for j in range(SC_PIECES)], axis=1)
    o_ref[...] = x1_ref[...] + gt2 * y


def _combine(x1, y2, route, mod_l, seq):
    n = x1.shape[0]
    tm = min(1024, seq)
    return pl.pallas_call(
        functools.partial(_combine_kernel, tiles_per_batch=seq // tm),
        grid=(n // tm,),
        in_specs=[pl.BlockSpec((tm, D_MODEL), lambda i: (i, 0)),
                  pl.BlockSpec((2, SC_PIECES, tm, SC_SUB), lambda i: (0, 0, i, 0)),
                  pl.BlockSpec((tm, LANES), lambda i: (i, 0)),
                  pl.BlockSpec(mod_l.shape, lambda i: (0, 0))],
        out_specs=pl.BlockSpec((tm, D_MODEL), lambda i: (i, 0)),
        out_shape=jax.ShapeDtypeStruct((n, D_MODEL), F32),
        compiler_params=_cparams(("arbitrary",)),
        name="moe_combine",
    )(x1, y2, route, mod_l)


def _moe(h2, route, x1, mod_l, wg, wu, wd, seq):
    n = h2.shape[1]
    n_tiles = (2 * n) // MOE_TM + N_EXPERTS
    rows = n_tiles * MOE_TM
    rk, cnt = _expert_ranks(route)
    pos, tmap = _expert_plan(cnt, route, rk, n_tiles)
    idx = jnp.concatenate([_piece_row_index(pos[:, 0], rows), _piece_row_index(pos[:, 1], rows)])
    xs = _sc_scatter_rows(h2.reshape(SC_PIECES * n, SC_SUB), idx, SC_PIECES * rows)
    ys = _grouped_experts(tmap[:n_tiles, 0], tmap[:n_tiles, 1], xs.reshape(SC_PIECES, rows, SC_SUB),
                          wg, wu, wd)
    y2 = _sc_gather_rows(ys.reshape(SC_PIECES * rows, SC_SUB), idx).reshape(2, SC_PIECES, n, SC_SUB)
    return _combine(x1, y2, route, mod_l, seq)


def _final_norm_kernel(x_ref, g_ref, o_ref):
    x = x_ref[...]
    ms = jnp.mean(x * x, axis=-1, keepdims=True)
    o_ref[...] = x * lax.rsqrt(ms + EPS) * g_ref[...]


def _final_norm(x2, g, seq):
    n = x2.shape[0]
    tm = min(1024, seq)
    return pl.pallas_call(
        _final_norm_kernel,
        grid=(n // tm,),
        in_specs=[pl.BlockSpec((tm, D_MODEL), lambda i: (i, 0)),
                  pl.BlockSpec((1, D_MODEL), lambda i: (0, 0))],
        out_specs=pl.BlockSpec((tm, D_MODEL), lambda i: (i, 0)),
        out_shape=jax.ShapeDtypeStruct((n, D_MODEL), F32),
        compiler_params=_cparams(("arbitrary",)),
        name="final_norm",
    )(x2, g.reshape(1, D_MODEL))


def _pack_w_in(w_in_l):
    offs = [0]
    for s in IN_SPLITS:
        offs.append(offs[-1] + s)
    (aq, ak, av, iq, ik, iw, bq, bf, bi, bg, cq, ck, cv, cg, ga, gb, gc) = [
        w_in_l[:, offs[i]:offs[i + 1]] for i in range(len(IN_SPLITS))]

    w16 = jnp.concatenate(
        [cv, cg, aq * (A_HEAD_DIM ** -0.5), ak, bq, bi, bg, cq, ck * (C_QK_DIM ** -0.5), ga, gb, gc],
        axis=1).astype(BF16)
    iq_p = jnp.pad(iq.reshape(D_MODEL, IDX_HEADS, IDX_DIM),
                   ((0, 0), (0, 0), (0, LANES - IDX_DIM))).reshape(D_MODEL, IDX_HEADS * LANES)
    w32 = jnp.concatenate(
        [bf, iq_p, jnp.pad(ik, ((0, 0), (0, LANES - IDX_DIM))),
         jnp.pad(iw, ((0, 0), (0, LANES - IDX_HEADS)))], axis=1).astype(BF16)
    return w16, w32, av.T.astype(BF16)


def _split_bf16(w):
    hi = w.astype(BF16)
    return hi, (w - hi.astype(F32)).astype(BF16)


def kernel(x, c, rel_bias, hgrn_lb_raw, norm1_g, norm2_g, ada_w, ada_b, w_in, hgrn_norm_g, w_branch_a,
           w_branch_b, w_branch_c, w_out, router_group_w, router_group_b, router_expert_w,
           router_expert_b, expert_w_gate, expert_w_up, expert_w_down, final_norm_g):
    bsz, seq, _ = x.shape
    depth = w_in.shape[0]
    n = bsz * seq
    x2 = x.reshape(n, D_MODEL)
    tq = min(DSA_TQ, seq)

    lb_all = _hgrn_lower_bounds(hgrn_lb_raw)
    c_pad = jnp.pad(c, ((0, (-bsz) % 8), (0, 0)))
    mod = _ada_mod(c_pad, ada_w, ada_b)
    bias_tiles = _bias_tiles(rel_bias, tq)
    ret_tables = _retention_tables(seq)

    for l in range(depth):
        w16, w32, wvt = _pack_w_in(w_in[l])
        p16 = _norm_project(x2, mod[l], norm1_g[l], w16, BF16, 1024, seq, "proj_bf16")
        p32 = _norm_project(x2, mod[l], norm1_g[l], w32, F32, 768, seq, "proj_f32")
        vt = _norm_project_t(x2, mod[l], norm1_g[l], wvt, tq, seq, "proj_vt")
        o_a = _dsa_attention(p16, p32, vt, bias_tiles, bsz, seq)
        o_b = _hgrn2(p16, p32, lb_all[l], hgrn_norm_g[l], bsz, seq)
        o_c = _retention(p16, ret_tables, bsz, seq)
        wr = jnp.concatenate([router_group_w[l], router_expert_w[l],
                              jnp.zeros((D_MODEL, LANES - N_GROUPS - N_EXPERTS), F32)], axis=1)
        br = jnp.concatenate([router_group_b[l], router_expert_b[l],
                              jnp.zeros((LANES - N_GROUPS - N_EXPERTS,), F32)]).reshape(1, LANES)
        wr_hi, wr_lo = _split_bf16(wr)
        x1, h2, route = _merge(o_a, o_b, o_c, p16, x2, mod[l], norm2_g[l],
                               w_branch_a[l].astype(BF16), w_branch_b[l].astype(BF16),
                               w_branch_c[l].astype(BF16), w_out[l].astype(BF16),
                               wr_hi, wr_lo, br, seq)
        x2 = _moe(h2, route, x1, mod[l], expert_w_gate[l], expert_w_up[l], expert_w_down[l], seq)

    return _final_norm(x2, final_norm_g, seq).reshape(bsz, seq, D_MODEL)
```

```python
import functools
import math

import jax
import jax.numpy as jnp
from jax import lax
from jax.experimental import pallas as pl
from jax.experimental.pallas import tpu as pltpu
from jax.experimental.pallas import tpu_sc as plsc

F32 = jnp.float32
BF16 = jnp.bfloat16

D_MODEL = 1024
A_HEADS = 8
A_HEAD_DIM = 128
IDX_HEADS = 8
IDX_DIM = 64
TOPK_MAX = 256
REL_BUCKETS = 32
REL_MAX_DIST = 128
B_HEADS = 8
B_HEAD_DIM = 128
C_HEADS = 4
C_QK_DIM = 256
C_V_DIM = 512
N_GROUPS = 4
EXPERTS_PER_GROUP = 8
N_EXPERTS = 32
D_EXPERT = 512
EPS = 1e-6

A_WIDTH = A_HEADS * A_HEAD_DIM
B_WIDTH = B_HEADS * B_HEAD_DIM
C_QK_WIDTH = C_HEADS * C_QK_DIM
C_V_WIDTH = C_HEADS * C_V_DIM
IN_SPLITS = (A_WIDTH, A_WIDTH, A_WIDTH, IDX_HEADS * IDX_DIM, IDX_DIM, IDX_HEADS,
             B_WIDTH, B_WIDTH, B_WIDTH, B_WIDTH,
             C_QK_WIDTH, C_QK_WIDTH, C_V_WIDTH, C_V_WIDTH,
             D_MODEL, D_MODEL, D_MODEL)

LANES = 128
VMEM_LIMIT = 56 * 1024 * 1024

P16_CV, P16_CG = 0, 2
P16_AQ, P16_AK, P16_BQ, P16_BI, P16_BG, P16_CQ, P16_CK, P16_GA, P16_GB, P16_GC = range(4, 14)
P32_BF = 0
P32_IQ = 1024
P32_IK = 2048
P32_IW = 2176

DSA_TQ = 256
HGRN_L = 256
HGRN_C = 64
HGRN_SB = 16
RET_C = 128
KEY_NEG_INF = -2139095041
INT_MIN = -2147483648
MASK_NEG = -1e30
COUNT_CHAINS = 4
MOE_TM = 256
SC_WINDOW = 128
SC_SUB = 256
SC_PIECES = D_MODEL // SC_SUB

NT_DIMS = (((1,), (1,)), ((), ()))
TN_DIMS = (((0,), (0,)), ((), ()))


def _cparams(sem):
    return pltpu.CompilerParams(dimension_semantics=sem, vmem_limit_bytes=VMEM_LIMIT)


def _silu(x):
    return x * jax.nn.sigmoid(x)


def _lb_kernel(raw_ref, o_ref):
    raw = raw_ref[...]
    m = jnp.max(raw, axis=0, keepdims=True)
    e = jnp.exp(raw - m)
    soft = e / jnp.sum(e, axis=0, keepdims=True)
    run = jnp.zeros_like(soft[0:1])
    for l in range(raw.shape[0]):
        run = run + soft[l:l + 1]
        o_ref[l:l + 1, :] = run - soft[0:1]


def _hgrn_lower_bounds(raw):
    return pl.pallas_call(
        _lb_kernel, out_shape=jax.ShapeDtypeStruct(raw.shape, F32), name="hgrn_lb")(raw)


def _ada_kernel(c_ref, w_ref, b_ref, o_ref):
    a = _silu(c_ref[...])
    o_ref[0] = jnp.dot(a, w_ref[0], precision=lax.Precision.HIGHEST,
                       preferred_element_type=F32) + b_ref[0]


def _ada_mod(c_pad, ada_w, ada_b):
    depth = ada_w.shape[0]
    rows = c_pad.shape[0]
    return pl.pallas_call(
        _ada_kernel,
        grid=(depth, 6),
        in_specs=[pl.BlockSpec((rows, D_MODEL), lambda l, j: (0, 0)),
                  pl.BlockSpec((1, D_MODEL, D_MODEL), lambda l, j: (l, 0, j)),
                  pl.BlockSpec((1, 1, D_MODEL), lambda l, j: (l, 0, j))],
        out_specs=pl.BlockSpec((1, rows, D_MODEL), lambda l, j: (l, 0, j)),
        out_shape=jax.ShapeDtypeStruct((depth, rows, 6 * D_MODEL), F32),
        compiler_params=_cparams(("arbitrary", "arbitrary")),
        name="ada_mod",
    )(c_pad, ada_w, ada_b.reshape(depth, 1, 6 * D_MODEL))


def _rms_mod(x, g, sc, sh):
    ms = jnp.mean(x * x, axis=-1, keepdims=True)
    return (x * lax.rsqrt(ms + EPS) * g) * (1.0 + sc) + sh


def _norm1(x_ref, mod_ref, g_ref, b):
    sh = mod_ref[pl.ds(b, 1), 0:D_MODEL]
    sc = mod_ref[pl.ds(b, 1), D_MODEL:2 * D_MODEL]
    return _rms_mod(x_ref[...], g_ref[...], sc, sh).astype(BF16)


def _proj_kernel(x_ref, mod_ref, g_ref, w_ref, o_ref, h_ref, *, tiles_per_batch):
    @pl.when(pl.program_id(1) == 0)
    def _():
        h_ref[...] = _norm1(x_ref, mod_ref, g_ref, pl.program_id(0) // tiles_per_batch)

    o_ref[...] = jnp.dot(h_ref[...], w_ref[...], preferred_element_type=F32).astype(o_ref.dtype)


def _norm_project(x2, mod_l, g, w, out_dtype, tn, seq, name):
    n = x2.shape[0]
    width = w.shape[1]
    tm = min(1024, seq)
    return pl.pallas_call(
        functools.partial(_proj_kernel, tiles_per_batch=seq // tm),
        grid=(n // tm, width // tn),
        in_specs=[pl.BlockSpec((tm, D_MODEL), lambda i, j: (i, 0)),
                  pl.BlockSpec(mod_l.shape, lambda i, j: (0, 0)),
                  pl.BlockSpec((1, D_MODEL), lambda i, j: (0, 0)),
                  pl.BlockSpec((D_MODEL, tn), lambda i, j: (0, j))],
        out_specs=pl.BlockSpec((tm, tn), lambda i, j: (i, j)),
        out_shape=jax.ShapeDtypeStruct((n, width), out_dtype),
        scratch_shapes=[pltpu.VMEM((tm, D_MODEL), BF16)],
        compiler_params=_cparams(("arbitrary", "arbitrary")),
        name=name,
    )(x2, mod_l, g.reshape(1, D_MODEL), w)


def _proj_t_kernel(x_ref, mod_ref, g_ref, wt_ref, o_ref, *, tiles_per_batch, chunk):
    h = _norm1(x_ref, mod_ref, g_ref, pl.program_id(0) // tiles_per_batch)
    res = lax.dot_general(wt_ref[...], h, NT_DIMS, preferred_element_type=F32)
    for ci in range(o_ref.shape[0]):
        o_ref[ci] = res[:, ci * chunk:(ci + 1) * chunk].astype(o_ref.dtype)


def _norm_project_t(x2, mod_l, g, wt, chunk, seq, name):
    n = x2.shape[0]
    cols = wt.shape[0]
    tm = min(1024, seq)
    return pl.pallas_call(
        functools.partial(_proj_t_kernel, tiles_per_batch=seq // tm, chunk=chunk),
        grid=(n // tm,),
        in_specs=[pl.BlockSpec((tm, D_MODEL), lambda i: (i, 0)),
                  pl.BlockSpec(mod_l.shape, lambda i: (0, 0)),
                  pl.BlockSpec((1, D_MODEL), lambda i: (0, 0)),
                  pl.BlockSpec((cols, D_MODEL), lambda i: (0, 0))],
        out_specs=pl.BlockSpec((tm // chunk, cols, chunk), lambda i: (i, 0, 0)),
        out_shape=jax.ShapeDtypeStruct((n // chunk, cols, chunk), BF16),
        compiler_params=_cparams(("arbitrary",)),
        name=name,
    )(x2, mod_l, g.reshape(1, D_MODEL), wt)


def _dsa_kernel(q_ref, iq_ref, iw_ref, k_ref, vt_ref, ik_ref, bias_ref, o_ref,
                key_ref, madd_ref, qt_ref, iqt_ref, iwt_ref, m_ref, l_ref, acc_ref, s_ref, *, tq, topk):
    qi = pl.program_id(1)
    nck = qi + 1
    idx_scale = (IDX_HEADS * IDX_DIM) ** -0.5

    for h in range(A_HEADS):
        hs = slice(h * LANES, (h + 1) * LANES)
        qt_ref[hs, :] = q_ref[:, hs].astype(F32).T.astype(BF16)
        iqt_ref[hs, :] = iq_ref[:, hs].T.astype(BF16)
    iwt_ref[...] = (iw_ref[...] * idx_scale).T

    krow = lax.broadcasted_iota(jnp.int32, (tq, tq), 0)
    qcol = lax.broadcasted_iota(jnp.int32, (tq, tq), 1)

    def score_chunk(c, carry):
        off = pl.multiple_of(c * tq, tq)
        ikc = ik_ref[pl.ds(off, tq), :].astype(BF16)
        acc = jnp.zeros((tq, tq), F32)
        for h in range(IDX_HEADS):
            s = jnp.dot(ikc, iqt_ref[h * LANES:(h + 1) * LANES, :], preferred_element_type=F32)
            acc = acc + jnp.maximum(s, 0.0) * iwt_ref[h:h + 1, :]
        acc = jnp.where(acc == 0.0, 0.0, acc)
        acc = jnp.where(krow + (c - qi) * tq <= qcol, acc, -jnp.inf)
        kb = pltpu.bitcast(acc, jnp.int32)
        key_ref[c] = jnp.where(kb < 0, kb ^ jnp.int32(0x7FFFFFFF), kb)
        return carry

    lax.fori_loop(0, nck, score_chunk, 0)

    def count(pred_fn):
        def body(c, parts):
            hit = jnp.where(pred_fn(key_ref[c]), 1.0, 0.0)
            parts = list(parts)
            for r in range(tq // 8):
                parts[r % COUNT_CHAINS] = parts[r % COUNT_CHAINS] + hit[r * 8:(r + 1) * 8, :]
            return tuple(parts)

        parts = lax.fori_loop(0, nck, body, (jnp.zeros((8, tq), F32),) * COUNT_CHAINS)
        return jnp.sum(sum(parts), axis=0, keepdims=True)

    def bit_step(i, theta):
        cand = theta + jnp.left_shift(jnp.int32(1), 31 - i)
        return jnp.where(count(lambda kc: kc >= cand) >= topk, cand, theta)

    theta = lax.fori_loop(0, 32, bit_step, jnp.full((1, tq), INT_MIN, jnp.int32))
    theta = jnp.maximum(theta, KEY_NEG_INF + 1)

    def mask_chunk(c, cnt):
        ge = key_ref[c] >= theta
        madd_ref[c] = jnp.where(ge, 0.0, MASK_NEG)
        return cnt + jnp.sum(jnp.where(ge, 1.0, 0.0), axis=0, keepdims=True)

    cnt_ge = lax.fori_loop(0, nck, mask_chunk, jnp.zeros((1, tq), F32))

    @pl.when(jnp.max(cnt_ge) > topk)
    def _():
        need_eq = topk - count(lambda kc: kc > theta)
        incl = jnp.where(krow >= qcol, 1.0, 0.0).astype(BF16)

        def tie_chunk(c, run):
            kc = key_ref[c]
            eq = kc == theta
            eqf = jnp.where(eq, 1.0, 0.0)
            pref = jnp.dot(incl, eqf.astype(BF16), preferred_element_type=F32) + run
            eq_add = jnp.where(pref <= need_eq, 0.0, MASK_NEG)
            madd_ref[c] = jnp.where(eq, eq_add, jnp.where(kc > theta, 0.0, MASK_NEG))
            return run + jnp.sum(eqf, axis=0, keepdims=True)

        lax.fori_loop(0, nck, tie_chunk, jnp.zeros((1, tq), F32))

    m_ref[...] = jnp.full(m_ref.shape, -jnp.inf, F32)
    l_ref[...] = jnp.zeros(l_ref.shape, F32)
    acc_ref[...] = jnp.zeros(acc_ref.shape, F32)

    def attend(c, lag):
        off = pl.multiple_of(c * tq, tq)
        head_slices = [slice(h * A_HEAD_DIM, (h + 1) * A_HEAD_DIM) for h in range(A_HEADS)]
        for h, hs in enumerate(head_slices):
            s_ref[h] = jnp.dot(k_ref[pl.ds(off, tq), hs], qt_ref[hs, :], preferred_element_type=F32)
        for h, hs in enumerate(head_slices):
            s = s_ref[h] + madd_ref[c]
            if lag is not None:
                s = s + bias_ref[h, lag]
            m_old = m_ref[h]
            m_new = jnp.maximum(m_old, jnp.max(s, axis=0, keepdims=True))
            alpha = jnp.exp(m_old - m_new)
            p = jnp.exp(s - m_new)
            l_ref[h] = alpha * l_ref[h] + jnp.sum(p, axis=0, keepdims=True)
            acc_ref[h] = alpha * acc_ref[h] + jnp.dot(vt_ref[c, hs, :], p.astype(BF16),
                                                      preferred_element_type=F32)
            m_ref[h] = m_new

    def far_chunk(c, carry):
        attend(c, None)
        return carry

    lax.fori_loop(0, jnp.maximum(qi - 1, 0), far_chunk, 0)

    @pl.when(qi >= 1)
    def _():
        attend(qi - 1, 1)

    attend(qi, 0)

    for h in range(A_HEADS):
        o = acc_ref[h] * (1.0 / l_ref[h])
        o_ref[:, h * A_HEAD_DIM:(h + 1) * A_HEAD_DIM] = o.T.astype(o_ref.dtype)


def _dsa_attention(p16, p32, vt, bias_tiles, bsz, seq):
    tq = min(DSA_TQ, seq)
    nq = seq // tq
    topk = min(TOPK_MAX, seq // 4)
    n = bsz * seq
    one = pl.Buffered(1)
    return pl.pallas_call(
        functools.partial(_dsa_kernel, tq=tq, topk=topk),
        grid=(bsz, nq),
        in_specs=[
            pl.BlockSpec((tq, A_WIDTH), lambda b, i: (b * nq + i, P16_AQ)),
            pl.BlockSpec((tq, 1024), lambda b, i: (b * nq + i, P32_IQ // 1024)),
            pl.BlockSpec((tq, LANES), lambda b, i: (b * nq + i, P32_IW // LANES)),
            pl.BlockSpec((seq, A_WIDTH), lambda b, i: (b, P16_AK), pipeline_mode=one),
            pl.BlockSpec((nq, A_WIDTH, tq), lambda b, i: (b, 0, 0), pipeline_mode=one),
            pl.BlockSpec((seq, LANES), lambda b, i: (b, P32_IK // LANES), pipeline_mode=one),
            pl.BlockSpec(bias_tiles.shape, lambda b, i: (0, 0, 0, 0), pipeline_mode=one),
        ],
        out_specs=pl.BlockSpec((tq, A_WIDTH), lambda b, i: (b * nq + i, 0)),
        out_shape=jax.ShapeDtypeStruct((n, A_WIDTH), BF16),
        scratch_shapes=[pltpu.VMEM((nq, tq, tq), jnp.int32),
                        pltpu.VMEM((nq, tq, tq), F32),
                        pltpu.VMEM((A_WIDTH, tq), BF16),
                        pltpu.VMEM((IDX_HEADS * LANES, tq), BF16),
                        pltpu.VMEM((LANES, tq), F32),
                        pltpu.VMEM((A_HEADS, 1, tq), F32),
                        pltpu.VMEM((A_HEADS, 1, tq), F32),
                        pltpu.VMEM((A_HEADS, A_HEAD_DIM, tq), F32),
                        pltpu.VMEM((A_HEADS, tq, tq), F32)],
        compiler_params=_cparams(("arbitrary", "arbitrary")),
        name="dsa_attention",
    )(p16, p32, p32, p16, vt, p32, bias_tiles)


def _t5_bucket(rel):
    max_exact = REL_BUCKETS // 2
    relf = jnp.maximum(rel, 1).astype(F32)
    large = max_exact + (jnp.log(relf / max_exact) / math.log(REL_MAX_DIST / max_exact)
                         * (REL_BUCKETS - max_exact)).astype(jnp.int32)
    large = jnp.minimum(large, REL_BUCKETS - 1)
    return jnp.where(rel < max_exact, rel, large)


def _bias_tiles(rel_bias, tq):
    assert tq >= REL_MAX_DIST
    key = jnp.arange(tq, dtype=jnp.int32)[:, None]
    qry = jnp.arange(tq, dtype=jnp.int32)[None, :]
    bucket = jnp.stack([_t5_bucket(jnp.maximum(lag * tq + qry - key, 0)) for lag in range(2)])
    rel = (rel_bias - rel_bias[REL_BUCKETS - 1:REL_BUCKETS]).astype(F32)
    onehot = bucket[None] == jnp.arange(REL_BUCKETS, dtype=jnp.int32)[:, None, None, None]
    return jnp.sum(jnp.where(onehot[:, None], rel[:, :, None, None, None], 0.0), axis=0)


def _hgrn_kernel(q_ref, f_ref, i_ref, g_ref, lb_ref, ng_ref, tril_ref, o_ref, st_ref, *, rows):
    @pl.when(pl.program_id(1) == 0)
    def _():
        st_ref[...] = jnp.zeros_like(st_ref)

    lb = lb_ref[...]
    f = lb + (1.0 - lb) * jax.nn.sigmoid(f_ref[...])
    logf = jnp.log(f)
    kk = 1.0 - f
    g1 = logf.astype(BF16)
    r1 = logf - g1.astype(F32)
    g2 = r1.astype(BF16)
    g3 = (r1 - g2.astype(F32)).astype(BF16)
    tril = tril_ref[...]
    bcum = (jnp.dot(tril, g1, preferred_element_type=F32)
            + jnp.dot(tril, g2, preferred_element_type=F32)
            + jnp.dot(tril, g3, preferred_element_type=F32))

    srow = lax.broadcasted_iota(jnp.int32, (HGRN_C, B_HEAD_DIM), 0)
    trow = lax.broadcasted_iota(jnp.int32, (HGRN_SB, HGRN_C), 0)
    scol = lax.broadcasted_iota(jnp.int32, (HGRN_SB, HGRN_C), 1)
    ng = ng_ref[...]
    q = q_ref[...].astype(F32)
    qb_all = (q * jnp.exp(bcum)).astype(BF16)

    tiles = [(n, h) for n in range(rows // HGRN_C) for h in range(B_HEADS)]

    def rs(n):
        return slice(n * HGRN_C, (n + 1) * HGRN_C)

    def hs(h):
        return slice(h * B_HEAD_DIM, (h + 1) * B_HEAD_DIM)

    a_parts = {}
    for n, h in tiles:
        bc, qc, kc = bcum[rs(n), hs(h)], q[rs(n), hs(h)], kk[rs(n), hs(h)]
        for sb in range(HGRN_C // HGRN_SB):
            s0 = sb * HGRN_SB
            beta = bc[s0 - 1:s0] if sb > 0 else jnp.zeros((1, B_HEAD_DIM), F32)
            qs = (qc[s0:s0 + HGRN_SB] * jnp.exp(bc[s0:s0 + HGRN_SB] - beta)).astype(BF16)
            expo = jnp.where(srow < s0 + HGRN_SB, beta - bc, -jnp.inf)
            ks = (kc * jnp.exp(expo)).astype(BF16)
            a_parts[n, h, sb] = lax.dot_general(qs, ks, NT_DIMS, preferred_element_type=F32)
    intra, upd, dec = {}, {}, {}
    for n, h in tiles:
        a_rows = [jnp.where(scol <= trow + sb * HGRN_SB, a_parts[n, h, sb], 0.0)
                  for sb in range(HGRN_C // HGRN_SB)]
        attn = jnp.concatenate(a_rows, axis=0).astype(BF16)
        vc = i_ref[rs(n), hs(h)]
        intra[n, h] = jnp.dot(attn, vc, preferred_element_type=F32)
        bc = bcum[rs(n), hs(h)]
        blast = bc[HGRN_C - 1:HGRN_C]
        kdec = (kk[rs(n), hs(h)] * jnp.exp(blast - bc)).astype(BF16)
        upd[n, h] = lax.dot_general(vc, kdec, TN_DIMS, preferred_element_type=F32)
        dec[n, h] = jnp.exp(blast)
    for n, h in tiles:
        st = st_ref[h]
        o = intra[n, h] + lax.dot_general(qb_all[rs(n), hs(h)], st.astype(BF16), NT_DIMS,
                                          preferred_element_type=F32)
        st_ref[h] = st * dec[n, h] + upd[n, h]
        ms = jnp.mean(o * o, axis=-1, keepdims=True)
        on = o * lax.rsqrt(ms + EPS) * ng
        o_ref[rs(n), hs(h)] = (on * _silu(g_ref[rs(n), hs(h)].astype(F32))).astype(o_ref.dtype)


def _hgrn2(p16, p32, lb_l, norm_g, bsz, seq):
    rows = min(HGRN_L, seq)
    nj = seq // rows
    n = bsz * seq
    r = jnp.arange(rows, dtype=jnp.int32)
    tril = ((r[:, None] >= r[None, :]) & (r[:, None] // HGRN_C == r[None, :] // HGRN_C)).astype(BF16)

    def col(base):
        return lambda b, j: (b * nj + j, base)

    return pl.pallas_call(
        functools.partial(_hgrn_kernel, rows=rows),
        grid=(bsz, nj),
        in_specs=[
            pl.BlockSpec((rows, B_WIDTH), col(P16_BQ)),
            pl.BlockSpec((rows, B_WIDTH), col(P32_BF // B_WIDTH)),
            pl.BlockSpec((rows, B_WIDTH), col(P16_BI)),
            pl.BlockSpec((rows, B_WIDTH), col(P16_BG)),
            pl.BlockSpec((1, B_WIDTH), lambda b, j: (0, 0)),
            pl.BlockSpec((1, B_HEAD_DIM), lambda b, j: (0, 0)),
            pl.BlockSpec((rows, rows), lambda b, j: (0, 0)),
        ],
        out_specs=pl.BlockSpec((rows, B_WIDTH), lambda b, j: (b * nj + j, 0)),
        out_shape=jax.ShapeDtypeStruct((n, B_WIDTH), BF16),
        scratch_shapes=[pltpu.VMEM((B_HEADS, B_HEAD_DIM, B_HEAD_DIM), F32)],
        compiler_params=_cparams(("arbitrary", "arbitrary")),
        name="hgrn2",
    )(p16, p32, p16, p16, lb_l.reshape(1, B_WIDTH), norm_g.reshape(1, B_HEAD_DIM), tril)


def _ret_kernel(q_ref, k_ref, v_ref, g_ref, cos_ref, sin_ref, idec_ref, qdec_ref, kdec_ref, cdec_ref,
                o_ref, st_ref):
    @pl.when(pl.program_id(1) == 0)
    def _():
        st_ref[...] = jnp.zeros_like(st_ref)

    heads = range(C_HEADS)
    cos = jnp.concatenate([cos_ref[...]] * C_HEADS, axis=1)
    sin_signed = jnp.concatenate([sin_ref[...]] * C_HEADS, axis=1)
    even = lax.broadcasted_iota(jnp.int32, cos.shape, 1) % 2 == 0

    def rot(a):
        swapped = jnp.where(even, pltpu.roll(a, C_QK_WIDTH - 1, 1), pltpu.roll(a, 1, 1))
        return a * cos + swapped * sin_signed

    qr = rot(q_ref[...].astype(F32))
    kr = rot(k_ref[...].astype(F32))
    qk = [slice(h * C_QK_DIM, (h + 1) * C_QK_DIM) for h in heads]
    vs = [slice(h * C_V_DIM, (h + 1) * C_V_DIM) for h in heads]
    attn = [lax.dot_general(qr[:, qk[h]].astype(BF16), kr[:, qk[h]].astype(BF16), NT_DIMS,
                            preferred_element_type=F32) * idec_ref[h] for h in heads]
    inter = [jnp.dot((qr[:, qk[h]] * qdec_ref[h]).astype(BF16), st_ref[h].astype(BF16),
                     preferred_element_type=F32) for h in heads]
    intra = [jnp.dot(attn[h].astype(BF16), v_ref[:, vs[h]], preferred_element_type=F32) for h in heads]
    upd = [jnp.dot((kr[:, qk[h]] * kdec_ref[h]).T.astype(BF16), v_ref[:, vs[h]],
                   preferred_element_type=F32) for h in heads]
    for h in heads:
        st_ref[h] = cdec_ref[h, 0:1, :] * st_ref[h] + upd[h]
        o = intra[h] + inter[h]
        ms = jnp.mean(o * o, axis=-1, keepdims=True)
        o_ref[:, vs[h]] = (_silu(g_ref[:, vs[h]].astype(F32)) * (o * lax.rsqrt(ms + EPS))).astype(o_ref.dtype)


def _retention_tables(seq):
    pos = jnp.arange(seq, dtype=F32)
    theta = jnp.repeat(1.0 / (10000.0 ** jnp.linspace(0.0, 1.0, C_QK_DIM // 2)), 2)
    ang = pos[:, None] * theta[None, :]
    pair_sign = jnp.where(jnp.arange(C_QK_DIM) % 2 == 0, -1.0, 1.0)
    log_gamma = jnp.log(1.0 - 2.0 ** (-5.0 - jnp.arange(C_HEADS, dtype=F32)))
    idx = jnp.arange(RET_C, dtype=F32)
    causal = idx[:, None] >= idx[None, :]
    idec = jnp.exp(jnp.where(causal[None], (idx[:, None] - idx[None, :])[None] * log_gamma[:, None, None],
                             -jnp.inf))
    qdec = jnp.exp((idx + 1.0)[None, :] * log_gamma[:, None])[..., None]
    kdec = jnp.exp((RET_C - 1.0 - idx)[None, :] * log_gamma[:, None])[..., None]
    cdec = jnp.exp(RET_C * log_gamma)[:, None, None]
    return (jnp.cos(ang), jnp.sin(ang) * pair_sign[None, :], idec,
            jnp.broadcast_to(qdec, (C_HEADS, RET_C, C_QK_DIM)),
            jnp.broadcast_to(kdec, (C_HEADS, RET_C, C_QK_DIM)),
            jnp.broadcast_to(cdec, (C_HEADS, 8, C_V_DIM)))


def _retention(p16, tables, bsz, seq):
    cos, sin, idec, qdec, kdec, cdec = tables
    nj = seq // RET_C
    n = bsz * seq
    v_blk = C_V_WIDTH // 1024

    def whole(a):
        return pl.BlockSpec(a.shape, lambda b, j: (0,) * a.ndim)

    return pl.pallas_call(
        _ret_kernel,
        grid=(bsz, nj),
        in_specs=[
            pl.BlockSpec((RET_C, C_QK_WIDTH), lambda b, j: (b * nj + j, P16_CQ)),
            pl.BlockSpec((RET_C, C_QK_WIDTH), lambda b, j: (b * nj + j, P16_CK)),
            pl.BlockSpec((RET_C, C_V_WIDTH), lambda b, j: (b * nj + j, P16_CV // v_blk)),
            pl.BlockSpec((RET_C, C_V_WIDTH), lambda b, j: (b * nj + j, P16_CG // v_blk)),
            pl.BlockSpec((RET_C, C_QK_DIM), lambda b, j: (j, 0)),
            pl.BlockSpec((RET_C, C_QK_DIM), lambda b, j: (j, 0)),
            whole(idec), whole(qdec), whole(kdec), whole(cdec),
        ],
        out_specs=pl.BlockSpec((RET_C, C_V_WIDTH), lambda b, j: (b * nj + j, 0)),
        out_shape=jax.ShapeDtypeStruct((n, C_V_WIDTH), BF16),
        scratch_shapes=[pltpu.VMEM((C_HEADS, C_QK_DIM, C_V_DIM), F32)],
        compiler_params=_cparams(("arbitrary", "arbitrary")),
        name="retention",
    )(p16, p16, p16, p16, cos, sin, idec, qdec, kdec, cdec)


def _merge_kernel(oa_ref, ob_ref, oc_ref, ga_ref, gb_ref, gc_ref, x_ref, mod_ref, g2_ref,
                  wa_ref, wb_ref, wc_ref, wo_ref, wrh_ref, wrl_ref, br_ref,
                  x1_ref, h2_ref, route_ref, *, tiles_per_batch):
    b = pl.program_id(0) // tiles_per_batch

    def gated(o_ref, w_ref, g_ref):
        y = jnp.dot(o_ref[...], w_ref[...], preferred_element_type=F32)
        return jax.nn.sigmoid(g_ref[...].astype(F32)) * y

    merged = gated(oa_ref, wa_ref, ga_ref) + gated(ob_ref, wb_ref, gb_ref) + gated(oc_ref, wc_ref, gc_ref)
    y = jnp.dot(merged.astype(BF16), wo_ref[...], preferred_element_type=F32)
    gt1 = mod_ref[pl.ds(b, 1), 2 * D_MODEL:3 * D_MODEL]
    x1 = x_ref[...] + gt1 * y
    x1_ref[...] = x1
    sh2 = mod_ref[pl.ds(b, 1), 3 * D_MODEL:4 * D_MODEL]
    sc2 = mod_ref[pl.ds(b, 1), 4 * D_MODEL:5 * D_MODEL]
    h2 = _rms_mod(x1, g2_ref[...], sc2, sh2)
    h_hi = h2.astype(BF16)
    h_f = h_hi.astype(F32)
    for j in range(h2_ref.shape[0]):
        h2_ref[j] = h_f[:, j * SC_SUB:(j + 1) * SC_SUB]
    h_lo = (h2 - h_hi.astype(F32)).astype(BF16)
    logits = (jnp.dot(h_hi, wrh_ref[...], preferred_element_type=F32)
              + jnp.dot(h_lo, wrh_ref[...], preferred_element_type=F32)
              + jnp.dot(h_hi, wrl_ref[...], preferred_element_type=F32)) + br_ref[...]
    lane = lax.broadcasted_iota(jnp.int32, logits.shape, 1).astype(F32)
    neg_inf = -jnp.inf

    def first_argmax(vals):
        top = jnp.max(vals, axis=-1, keepdims=True)
        idx = jnp.min(jnp.where(vals == top, lane, float(LANES)), axis=-1, keepdims=True)
        return top, idx

    gl = jnp.where(lane < N_GROUPS, logits, neg_inf)
    gmax, gsel = first_argmax(gl)
    gprob = 1.0 / jnp.sum(jnp.exp(gl - gmax), axis=-1, keepdims=True)
    lo = N_GROUPS + EXPERTS_PER_GROUP * gsel
    el = jnp.where((lane >= lo) & (lane < lo + EXPERTS_PER_GROUP), logits, neg_inf)
    v1, i1 = first_argmax(el)
    el2 = jnp.where(lane == i1, neg_inf, el)
    v2, i2 = first_argmax(el2)
    e2 = jnp.exp(v2 - v1)
    den = 1.0 + e2
    route_ref[...] = jnp.where(lane == 0.0, i1 - N_GROUPS,
                               jnp.where(lane == 1.0, i2 - N_GROUPS,
                                         jnp.where(lane == 2.0, gprob / den,
                                                   jnp.where(lane == 3.0, gprob * (e2 / den), 0.0))))


def _merge(o_a, o_b, o_c, p16, x2, mod_l, g2, wa, wb, wc, wo, wr_hi, wr_lo, br, seq):
    n = x2.shape[0]
    tm = min(512, seq)
    one = pl.Buffered(1)

    def rows(width, cb=0):
        return pl.BlockSpec((tm, width), lambda i: (i, cb))

    def whole(a):
        return pl.BlockSpec(a.shape, lambda i: (0,) * a.ndim, pipeline_mode=one)

    return pl.pallas_call(
        functools.partial(_merge_kernel, tiles_per_batch=seq // tm),
        grid=(n // tm,),
        in_specs=[rows(A_WIDTH), rows(B_WIDTH), rows(C_V_WIDTH),
                  rows(D_MODEL, P16_GA), rows(D_MODEL, P16_GB), rows(D_MODEL, P16_GC),
                  rows(D_MODEL), whole(mod_l), pl.BlockSpec((1, D_MODEL), lambda i: (0, 0)),
                  whole(wa), whole(wb), whole(wc), whole(wo), whole(wr_hi), whole(wr_lo), whole(br)],
        out_specs=[rows(D_MODEL), pl.BlockSpec((SC_PIECES, tm, SC_SUB), lambda i: (0, i, 0)), rows(LANES)],
        out_shape=[jax.ShapeDtypeStruct((n, D_MODEL), F32),
                   jax.ShapeDtypeStruct((SC_PIECES, n, SC_SUB), F32),
                   jax.ShapeDtypeStruct((n, LANES), F32)],
        compiler_params=_cparams(("arbitrary",)),
        name="merge_route",
    )(o_a, o_b, o_c, p16, p16, p16, x2, mod_l, g2.reshape(1, D_MODEL), wa, wb, wc, wo, wr_hi, wr_lo, br)


def _lane_pick(vals, lane, idx):
    return jnp.sum(jnp.where(lane == idx, vals, 0.0), axis=-1, keepdims=True)


def _rank_kernel(route_ref, rk_ref, cnt_ref, run_ref):
    @pl.when(pl.program_id(0) == 0)
    def _():
        run_ref[...] = jnp.zeros_like(run_ref)

    route = route_ref[...]
    tb = route.shape[0]
    lane = lax.broadcasted_iota(jnp.int32, route.shape, 1).astype(F32)
    e1 = _lane_pick(route, lane, 0.0)
    e2 = _lane_pick(route, lane, 1.0)
    sel = jnp.where((lane == e1) | (lane == e2), 1.0, 0.0)
    r = lax.broadcasted_iota(jnp.int32, (tb, tb), 0)
    c = lax.broadcasted_iota(jnp.int32, (tb, tb), 1)
    before = jnp.where(c < r, 1.0, 0.0).astype(BF16)
    rank = jnp.dot(before, sel.astype(BF16), preferred_element_type=F32) + run_ref[0:1, :]
    rk_ref[...] = jnp.where(lane == 0.0, _lane_pick(rank, lane, e1),
                            jnp.where(lane == 1.0, _lane_pick(rank, lane, e2), 0.0))
    run_ref[...] = run_ref[...] + jnp.sum(sel, axis=0, keepdims=True)
    cnt_ref[...] = run_ref[...]


def _expert_ranks(route):
    n = route.shape[0]
    tb = min(256, n)
    return pl.pallas_call(
        _rank_kernel,
        grid=(n // tb,),
        in_specs=[pl.BlockSpec((tb, LANES), lambda i: (i, 0))],
        out_specs=[pl.BlockSpec((tb, LANES), lambda i: (i, 0)),
                   pl.BlockSpec((8, LANES), lambda i: (0, 0))],
        out_shape=[jax.ShapeDtypeStruct((n, LANES), F32), jax.ShapeDtypeStruct((8, LANES), F32)],
        scratch_shapes=[pltpu.VMEM((8, LANES), F32)],
        compiler_params=_cparams(("arbitrary",)),
        name="expert_ranks",
    )(route)


def _plan_kernel(cnt_ref, route_ref, rk_ref, pos_ref, tmap_ref):
    lane_i = lax.broadcasted_iota(jnp.int32, (8, LANES), 1)
    cnt = jnp.where(lane_i < N_EXPERTS, cnt_ref[...], 0.0)
    padded = jnp.floor((cnt + (MOE_TM - 1)) * (1.0 / MOE_TM)) * MOE_TM
    r = lax.broadcasted_iota(jnp.int32, (LANES, LANES), 0)
    c = lax.broadcasted_iota(jnp.int32, (LANES, LANES), 1)
    base = jnp.dot(padded, jnp.where(r < c, 1.0, 0.0), precision=lax.Precision.HIGHEST,
                   preferred_element_type=F32)

    route = route_ref[...]
    lane = lax.broadcasted_iota(jnp.int32, route.shape, 1).astype(F32)
    rk = rk_ref[...]
    base_row = base[0:1, :]
    pos1 = _lane_pick(base_row, lane, _lane_pick(route, lane, 0.0)) + _lane_pick(rk, lane, 0.0)
    pos2 = _lane_pick(base_row, lane, _lane_pick(route, lane, 1.0)) + _lane_pick(rk, lane, 1.0)
    pos_ref[...] = jnp.where(lane == 0.0, pos1, jnp.where(lane == 1.0, pos2, 0.0)).astype(jnp.int32)

    @pl.when(pl.program_id(0) == 0)
    def _():
        nt = tmap_ref.shape[0]
        tlane = lax.broadcasted_iota(jnp.int32, (nt, LANES), 1)
        start = (lax.broadcasted_iota(jnp.int32, (nt, LANES), 0) * MOE_TM).astype(F32)
        end_row = jnp.where(tlane < N_EXPERTS, base_row + padded[0:1, :], 3e38)
        expert = jnp.sum(jnp.where(end_row <= start, 1.0, 0.0), axis=-1, keepdims=True)
        expert_c = jnp.minimum(expert, N_EXPERTS - 1.0)
        tl = tlane.astype(F32)
        left = _lane_pick(cnt[0:1, :], tl, expert_c) - (start[:, 0:1] - _lane_pick(base_row, tl, expert_c))
        valid = jnp.where(expert < N_EXPERTS, jnp.clip(left, 0.0, float(MOE_TM)), 0.0)
        tmap_ref[...] = jnp.where(tlane == 0, expert_c, jnp.where(tlane == 1, valid, 0.0)).astype(jnp.int32)


def _expert_plan(cnt, route, rk, n_tiles):
    n = route.shape[0]
    tb = min(256, n)
    nt_pad = -(-n_tiles // 8) * 8
    return pl.pallas_call(
        _plan_kernel,
        grid=(n // tb,),
        in_specs=[pl.BlockSpec((8, LANES), lambda i: (0, 0)),
                  pl.BlockSpec((tb, LANES), lambda i: (i, 0)),
                  pl.BlockSpec((tb, LANES), lambda i: (i, 0))],
        out_specs=[pl.BlockSpec((tb, LANES), lambda i: (i, 0)),
                   pl.BlockSpec((nt_pad, LANES), lambda i: (0, 0))],
        out_shape=[jax.ShapeDtypeStruct((n, LANES), jnp.int32),
                   jax.ShapeDtypeStruct((nt_pad, LANES), jnp.int32)],
        compiler_params=_cparams(("arbitrary",)),
        name="expert_plan",
    )(cnt, route, rk)


def _sc_mesh():
    return plsc.VectorSubcoreMesh(core_axis_name="c", subcore_axis_name="s")


def _sc_scatter_rows(src, idx, out_rows):
    m = idx.shape[0]
    n_src_win = src.shape[0] // SC_WINDOW

    @functools.partial(pl.kernel, out_type=jax.ShapeDtypeStruct((out_rows, src.shape[1]), src.dtype),
                       mesh=_sc_mesh(), scratch_types=[])
    def scatter(x_hbm, i_hbm, o_hbm):
        def body(x_vmem, i_vmem):
            pltpu.sync_copy(x_vmem, o_hbm.at[i_vmem.at[0]])

        pltpu.emit_pipeline(
            body, grid=(m // SC_WINDOW,),
            in_specs=[pl.BlockSpec((SC_WINDOW, src.shape[1]), lambda i: (i % n_src_win, 0)),
                      pl.BlockSpec((1, SC_WINDOW), lambda i: (0, i))],
            out_specs=[], core_axis_name=("c", "s"),
            dimension_semantics=(pltpu.PARALLEL,))(x_hbm, i_hbm)

    return scatter(src, idx.reshape(1, m))


def _sc_gather_rows(table, idx):
    m = idx.shape[0]

    @functools.partial(pl.kernel, out_type=jax.ShapeDtypeStruct((m, table.shape[1]), table.dtype),
                       mesh=_sc_mesh(), scratch_types=[])
    def gather(x_hbm, i_hbm, o_hbm):
        def body(i_vmem, o_vmem):
            pltpu.sync_copy(x_hbm.at[i_vmem.at[0]], o_vmem)

        pltpu.emit_pipeline(
            body, grid=(m // SC_WINDOW,),
            in_specs=[pl.BlockSpec((1, SC_WINDOW), lambda i: (0, i))],
            out_specs=[pl.BlockSpec((SC_WINDOW, table.shape[1]), lambda i: (i, 0))],
            core_axis_name=("c", "s"),
            dimension_semantics=(pltpu.PARALLEL,))(i_hbm, o_hbm)

    return gather(table, idx.reshape(1, m))


def _piece_row_index(pos, rows):
    return (jnp.arange(SC_PIECES, dtype=jnp.int32)[:, None] * rows + pos[None, :]).reshape(-1)


def _grouped_kernel(te_ref, tv_ref, x_ref, wg_ref, wu_ref, wd_ref, o_ref):
    valid = tv_ref[pl.program_id(0)]

    @pl.when(valid > 0)
    def _():
        x = jnp.concatenate([x_ref[j] for j in range(SC_PIECES)], axis=1)
        row = lax.broadcasted_iota(jnp.int32, x.shape, 0)
        x = jnp.where(row < valid, x, 0.0).astype(BF16)
        a = jnp.dot(x, wg_ref[0].astype(BF16), preferred_element_type=F32)
        u = jnp.dot(x, wu_ref[0].astype(BF16), preferred_element_type=F32)
        hm = (_silu(a) * u).astype(BF16)
        o = jnp.dot(hm, wd_ref[0].astype(BF16), preferred_element_type=F32)
        for j in range(SC_PIECES):
            o_ref[j] = o[:, j * SC_SUB:(j + 1) * SC_SUB]

    @pl.when(valid <= 0)
    def _():
        o_ref[...] = jnp.zeros_like(o_ref)


def _grouped_experts(tile_expert, tile_valid, xs, wg, wu, wd):
    n_tiles = tile_expert.shape[0]
    rows_block = pl.BlockSpec((SC_PIECES, MOE_TM, SC_SUB), lambda i, te, tv: (0, i, 0))
    return pl.pallas_call(
        _grouped_kernel,
        grid_spec=pltpu.PrefetchScalarGridSpec(
            num_scalar_prefetch=2,
            grid=(n_tiles,),
            in_specs=[rows_block,
                      pl.BlockSpec((1, D_MODEL, D_EXPERT), lambda i, te, tv: (te[i], 0, 0)),
                      pl.BlockSpec((1, D_MODEL, D_EXPERT), lambda i, te, tv: (te[i], 0, 0)),
                      pl.BlockSpec((1, D_EXPERT, D_MODEL), lambda i, te, tv: (te[i], 0, 0))],
            out_specs=rows_block),
        out_shape=jax.ShapeDtypeStruct(xs.shape, F32),
        compiler_params=_cparams(("arbitrary",)),
        name="grouped_experts",
    )(tile_expert, tile_valid, xs, wg, wu, wd)


def _combine_kernel(x1_ref, y_ref, route_ref, mod_ref, o_ref, *, tiles_per_batch):
    b = pl.program_id(0) // tiles_per_batch
    gt2 = mod_ref[pl.ds(b, 1), 5 * D_MODEL:6 * D_MODEL]
    route = route_ref[...]
    lane = lax.broadcasted_iota(jnp.int32, route.shape, 1).astype(F32)
    w1 = _lane_pick(route, lane, 2.0)
    w2 = _lane_pick(route, lane, 3.0)
    y = jnp.concatenate([w1 * y_ref[0, j] + w2 * y_ref[1, j] for j in range(SC_PIECES)], axis=1)
    o_ref[...] = x1_ref[...] + gt2 * y


def _combine(x1, y2, route, mod_l, seq):
    n = x1.shape[0]
    tm = min(1024, seq)
    return pl.pallas_call(
        functools.partial(_combine_kernel, tiles_per_batch=seq // tm),
        grid=(n // tm,),
        in_specs=[pl.BlockSpec((tm, D_MODEL), lambda i: (i, 0)),
                  pl.BlockSpec((2, SC_PIECES, tm, SC_SUB), lambda i: (0, 0, i, 0)),
                  pl.BlockSpec((tm, LANES), lambda i: (i, 0)),
                  pl.BlockSpec(mod_l.shape, lambda i: (0, 0))],
        out_specs=pl.BlockSpec((tm, D_MODEL), lambda i: (i, 0)),
        out_shape=jax.ShapeDtypeStruct((n, D_MODEL), F32),
        compiler_params=_cparams(("arbitrary",)),
        name="moe_combine",
    )(x1, y2, route, mod_l)


def _moe(h2, route, x1, mod_l, wg, wu, wd, seq):
    n = h2.shape[1]
    n_tiles = (2 * n) // MOE_TM + N_EXPERTS
    rows = n_tiles * MOE_TM
    rk, cnt = _expert_ranks(route)
    pos, tmap = _expert_plan(cnt, route, rk, n_tiles)
    idx = jnp.concatenate([_piece_row_index(pos[:, 0], rows), _piece_row_index(pos[:, 1], rows)])
    xs = _sc_scatter_rows(h2.reshape(SC_PIECES * n, SC_SUB), idx, SC_PIECES * rows)
    ys = _grouped_experts(tmap[:n_tiles, 0], tmap[:n_tiles, 1], xs.reshape(SC_PIECES, rows, SC_SUB),
                          wg, wu, wd)
    y2 = _sc_gather_rows(ys.reshape(SC_PIECES * rows, SC_SUB), idx).reshape(2, SC_PIECES, n, SC_SUB)
    return _combine(x1, y2, route, mod_l, seq)


def _final_norm_kernel(x_ref, g_ref, o_ref):
    x = x_ref[...]
    ms = jnp.mean(x * x, axis=-1, keepdims=True)
    o_ref[...] = x * lax.rsqrt(ms + EPS) * g_ref[...]


def _final_norm(x2, g, seq):
    n = x2.shape[0]
    tm = min(1024, seq)
    return pl.pallas_call(
        _final_norm_kernel,
        grid=(n // tm,),
        in_specs=[pl.BlockSpec((tm, D_MODEL), lambda i: (i, 0)),
                  pl.BlockSpec((1, D_MODEL), lambda i: (0, 0))],
        out_specs=pl.BlockSpec((tm, D_MODEL), lambda i: (i, 0)),
        out_shape=jax.ShapeDtypeStruct((n, D_MODEL), F32),
        compiler_params=_cparams(("arbitrary",)),
        name="final_norm",
    )(x2, g.reshape(1, D_MODEL))


def _pack_w_in(w_in_l):
    offs = [0]
    for s in IN_SPLITS:
        offs.append(offs[-1] + s)
    (aq, ak, av, iq, ik, iw, bq, bf, bi, bg, cq, ck, cv, cg, ga, gb, gc) = [
        w_in_l[:, offs[i]:offs[i + 1]] for i in range(len(IN_SPLITS))]

    w16 = jnp.concatenate(
        [cv, cg, aq * (A_HEAD_DIM ** -0.5), ak, bq, bi, bg, cq, ck * (C_QK_DIM ** -0.5), ga, gb, gc],
        axis=1).astype(BF16)
    iq_p = jnp.pad(iq.reshape(D_MODEL, IDX_HEADS, IDX_DIM),
                   ((0, 0), (0, 0), (0, LANES - IDX_DIM))).reshape(D_MODEL, IDX_HEADS * LANES)
    w32 = jnp.concatenate(
        [bf, iq_p, jnp.pad(ik, ((0, 0), (0, LANES - IDX_DIM))),
         jnp.pad(iw, ((0, 0), (0, LANES - IDX_HEADS)))], axis=1).astype(BF16)
    return w16, w32, av.T.astype(BF16)


def _split_bf16(w):
    hi = w.astype(BF16)
    return hi, (w - hi.astype(F32)).astype(BF16)


def kernel(x, c, rel_bias, hgrn_lb_raw, norm1_g, norm2_g, ada_w, ada_b, w_in, hgrn_norm_g, w_branch_a,
           w_branch_b, w_branch_c, w_out, router_group_w, router_group_b, router_expert_w,
           router_expert_b, expert_w_gate, expert_w_up, expert_w_down, final_norm_g):
    bsz, seq, _ = x.shape
    depth = w_in.shape[0]
    n = bsz * seq
    x2 = x.reshape(n, D_MODEL)
    tq = min(DSA_TQ, seq)

    lb_all = _hgrn_lower_bounds(hgrn_lb_raw)
    c_pad = jnp.pad(c, ((0, (-bsz) % 8), (0, 0)))
    mod = _ada_mod(c_pad, ada_w, ada_b)
    bias_tiles = _bias_tiles(rel_bias, tq)
    ret_tables = _retention_tables(seq)

    for l in range(depth):
        w16, w32, wvt = _pack_w_in(w_in[l])
        p16 = _norm_project(x2, mod[l], norm1_g[l], w16, BF16, 1024, seq, "proj_bf16")
        p32 = _norm_project(x2, mod[l], norm1_g[l], w32, F32, 768, seq, "proj_f32")
        vt = _norm_project_t(x2, mod[l], norm1_g[l], wvt, tq, seq, "proj_vt")
        o_a = _dsa_attention(p16, p32, vt, bias_tiles, bsz, seq)
        o_b = _hgrn2(p16, p32, lb_all[l], hgrn_norm_g[l], bsz, seq)
        o_c = _retention(p16, ret_tables, bsz, seq)
        wr = jnp.concatenate([router_group_w[l], router_expert_w[l],
                              jnp.zeros((D_MODEL, LANES - N_GROUPS - N_EXPERTS), F32)], axis=1)
        br = jnp.concatenate([router_group_b[l], router_expert_b[l],
                              jnp.zeros((LANES - N_GROUPS - N_EXPERTS,), F32)]).reshape(1, LANES)
        wr_hi, wr_lo = _split_bf16(wr)
        x1, h2, route = _merge(o_a, o_b, o_c, p16, x2, mod[l], norm2_g[l],
                               w_branch_a[l].astype(BF16), w_branch_b[l].astype(BF16),
                               w_branch_c[l].astype(BF16), w_out[l].astype(BF16),
                               wr_hi, wr_lo, br, seq)
        x2 = _moe(h2, route, x1, mod[l], expert_w_gate[l], expert_w_up[l], expert_w_down[l], seq)

    return _final_norm(x2, final_norm_g, seq).reshape(bsz, seq, D_MODEL)
```

```python
import functools
import math

import jax
import jax.numpy as jnp
from jax import lax
from jax.experimental import pallas as pl
from jax.experimental.pallas import tpu as pltpu
from jax.experimental.pallas import tpu_sc as plsc

F32 = jnp.float32
BF16 = jnp.bfloat16

D_MODEL = 1024
A_HEADS = 8
A_HEAD_DIM = 128
IDX_HEADS = 8
IDX_DIM = 64
TOPK_MAX = 256
REL_BUCKETS = 32
REL_MAX_DIST = 128
B_HEADS = 8
B_HEAD_DIM = 128
C_HEADS = 4
C_QK_DIM = 256
C_V_DIM = 512
N_GROUPS = 4
EXPERTS_PER_GROUP = 8
N_EXPERTS = 32
D_EXPERT = 512
EPS = 1e-6

A_WIDTH = A_HEADS * A_HEAD_DIM
B_WIDTH = B_HEADS * B_HEAD_DIM
C_QK_WIDTH = C_HEADS * C_QK_DIM
C_V_WIDTH = C_HEADS * C_V_DIM
IN_SPLITS = (A_WIDTH, A_WIDTH, A_WIDTH, IDX_HEADS * IDX_DIM, IDX_DIM, IDX_HEADS,
             B_WIDTH, B_WIDTH, B_WIDTH, B_WIDTH,
             C_QK_WIDTH, C_QK_WIDTH, C_V_WIDTH, C_V_WIDTH,
             D_MODEL, D_MODEL, D_MODEL)

LANES = 128
BF16_ROWS = 16
VMEM_LIMIT = 56 * 1024 * 1024

P16_CV, P16_CG = 0, 2
P16_AQ, P16_AK, P16_BQ, P16_BI, P16_BG, P16_CQ, P16_CK, P16_GA, P16_GB, P16_GC = range(4, 14)
P32_BF = 0
P32_IQ = 1024
P32_IK = 2048
P32_IW = 2176

DSA_TQ = 256
HGRN_L = 256
HGRN_C = 64
HGRN_SB = 16
RET_C = 128
KEY_NEG_INF = -2139095041
HALF_BIAS = 32768
MASK_NEG = -1e30
LOG2_E = math.log2(math.e)
COUNT_CHAINS = 4
MOE_TM = 256
SC_WINDOW = 128
SC_SUB = 256
SC_PIECES = D_MODEL // SC_SUB

NT_DIMS = (((1,), (1,)), ((), ()))
TN_DIMS = (((0,), (0,)), ((), ()))


def _cparams(sem):
    return pltpu.CompilerParams(dimension_semantics=sem, vmem_limit_bytes=VMEM_LIMIT)


def _silu(x):
    return x * jax.nn.sigmoid(x)


def _lb_kernel(raw_ref, o_ref):
    raw = raw_ref[...]
    m = jnp.max(raw, axis=0, keepdims=True)
    e = jnp.exp(raw - m)
    soft = e / jnp.sum(e, axis=0, keepdims=True)
    run = jnp.zeros_like(soft[0:1])
    for l in range(raw.shape[0]):
        run = run + soft[l:l + 1]
        o_ref[l:l + 1, :] = run - soft[0:1]


def _hgrn_lower_bounds(raw):
    return pl.pallas_call(
        _lb_kernel, out_shape=jax.ShapeDtypeStruct(raw.shape, F32), name="hgrn_lb")(raw)


def _ada_kernel(c_ref, w_ref, b_ref, o_ref):
    a = _silu(c_ref[...])
    o_ref[0] = jnp.dot(a, w_ref[0], precision=lax.Precision.HIGHEST,
                       preferred_element_type=F32) + b_ref[0]


def _ada_mod(c_pad, ada_w, ada_b):
    depth = ada_w.shape[0]
    rows = c_pad.shape[0]
    return pl.pallas_call(
        _ada_kernel,
        grid=(depth, 6),
        in_specs=[pl.BlockSpec((rows, D_MODEL), lambda l, j: (0, 0)),
                  pl.BlockSpec((1, D_MODEL, D_MODEL), lambda l, j: (l, 0, j)),
                  pl.BlockSpec((1, 1, D_MODEL), lambda l, j: (l, 0, j))],
        out_specs=pl.BlockSpec((1, rows, D_MODEL), lambda l, j: (l, 0, j)),
        out_shape=jax.ShapeDtypeStruct((depth, rows, 6 * D_MODEL), F32),
        compiler_params=_cparams(("arbitrary", "arbitrary")),
        name="ada_mod",
    )(c_pad, ada_w, ada_b.reshape(depth, 1, 6 * D_MODEL))


def _rms_mod(x, g, sc, sh):
    ms = jnp.mean(x * x, axis=-1, keepdims=True)
    return (x * lax.rsqrt(ms + EPS) * g) * (1.0 + sc) + sh


def _norm1(x_ref, mod_ref, g_ref, b):
    sh = mod_ref[pl.ds(b, 1), 0:D_MODEL]
    sc = mod_ref[pl.ds(b, 1), D_MODEL:2 * D_MODEL]
    return _rms_mod(x_ref[...], g_ref[...], sc, sh).astype(BF16)


def _proj_kernel(x_ref, mod_ref, g_ref, w_ref, o_ref, h_ref, *, tiles_per_batch):
    @pl.when(pl.program_id(1) == 0)
    def _():
        h_ref[...] = _norm1(x_ref, mod_ref, g_ref, pl.program_id(0) // tiles_per_batch)

    o_ref[...] = jnp.dot(h_ref[...], w_ref[...], preferred_element_type=F32).astype(o_ref.dtype)


def _norm_project(x2, mod_l, g, w, out_dtype, tn, seq, name):
    n = x2.shape[0]
    width = w.shape[1]
    tm = min(1024, seq)
    return pl.pallas_call(
        functools.partial(_proj_kernel, tiles_per_batch=seq // tm),
        grid=(n // tm, width // tn),
        in_specs=[pl.BlockSpec((tm, D_MODEL), lambda i, j: (i, 0)),
                  pl.BlockSpec(mod_l.shape, lambda i, j: (0, 0)),
                  pl.BlockSpec((1, D_MODEL), lambda i, j: (0, 0)),
                  pl.BlockSpec((D_MODEL, tn), lambda i, j: (0, j))],
        out_specs=pl.BlockSpec((tm, tn), lambda i, j: (i, j)),
        out_shape=jax.ShapeDtypeStruct((n, width), out_dtype),
        scratch_shapes=[pltpu.VMEM((tm, D_MODEL), BF16)],
        compiler_params=_cparams(("arbitrary", "arbitrary")),
        name=name,
    )(x2, mod_l, g.reshape(1, D_MODEL), w)


def _proj_t_kernel(x_ref, mod_ref, g_ref, wt_ref, o_ref, *, tiles_per_batch, chunk):
    h = _norm1(x_ref, mod_ref, g_ref, pl.program_id(0) // tiles_per_batch)
    res = lax.dot_general(wt_ref[...], h, NT_DIMS, preferred_element_type=F32)
    for ci in range(o_ref.shape[0]):
        o_ref[ci] = res[:, ci * chunk:(ci + 1) * chunk].astype(o_ref.dtype)


def _norm_project_t(x2, mod_l, g, wt, chunk, seq, name):
    n = x2.shape[0]
    cols = wt.shape[0]
    tm = min(1024, seq)
    return pl.pallas_call(
        functools.partial(_proj_t_kernel, tiles_per_batch=seq // tm, chunk=chunk),
        grid=(n // tm,),
        in_specs=[pl.BlockSpec((tm, D_MODEL), lambda i: (i, 0)),
                  pl.BlockSpec(mod_l.shape, lambda i: (0, 0)),
                  pl.BlockSpec((1, D_MODEL), lambda i: (0, 0)),
                  pl.BlockSpec((cols, D_MODEL), lambda i: (0, 0))],
        out_specs=pl.BlockSpec((tm // chunk, cols, chunk), lambda i: (i, 0, 0)),
        out_shape=jax.ShapeDtypeStruct((n // chunk, cols, chunk), BF16),
        compiler_params=_cparams(("arbitrary",)),
        name=name,
    )(x2, mod_l, g.reshape(1, D_MODEL), wt)


def _dsa_kernel(q_ref, iq_ref, iw_ref, k_ref, vt_ref, ik_ref, bias_ref, o_ref,
                key_ref, hi_ref, lo_ref, madd_ref, qt_ref, iqt_ref, iwt_ref, m_ref, l_ref, acc_ref, s_ref,
                *, tq, topk):
    qi = pl.program_id(1)
    nck = qi + 1
    idx_scale = (IDX_HEADS * IDX_DIM) ** -0.5

    for h in range(A_HEADS):
        hs = slice(h * LANES, (h + 1) * LANES)
        qt_ref[hs, :] = q_ref[:, hs].astype(F32).T.astype(BF16)
        iqt_ref[hs, :] = iq_ref[:, hs].T.astype(BF16)
    iwt_ref[...] = (iw_ref[...] * idx_scale).T

    krow = lax.broadcasted_iota(jnp.int32, (tq, tq), 0)
    qcol = lax.broadcasted_iota(jnp.int32, (tq, tq), 1)

    def score_chunk(c, carry):
        off = pl.multiple_of(c * tq, tq)
        ikc = ik_ref[pl.ds(off, tq), :].astype(BF16)
        acc = jnp.zeros((tq, tq), F32)
        for h in range(IDX_HEADS):
            s = jnp.dot(ikc, iqt_ref[h * LANES:(h + 1) * LANES, :], preferred_element_type=F32)
            acc = acc + jnp.maximum(s, 0.0) * iwt_ref[h:h + 1, :]
        acc = jnp.where(acc == 0.0, 0.0, acc)
        acc = jnp.where(krow + (c - qi) * tq <= qcol, acc, -jnp.inf)
        kb = pltpu.bitcast(acc, jnp.int32)
        key = jnp.where(kb < 0, kb ^ jnp.int32(0x7FFFFFFF), kb)
        key_ref[c] = key
        hi_ref[c] = jnp.right_shift(key, 16).astype(jnp.int16)
        lo_ref[c] = ((key & 0xFFFF) - HALF_BIAS).astype(jnp.int16)
        return carry

    lax.fori_loop(0, nck, score_chunk, 0)

    def count(ref, pred_fn, rows, zero, one):
        def body(c, parts):
            hit = jnp.where(pred_fn(ref[c]), one, zero)
            parts = list(parts)
            for r in range(tq // rows):
                parts[r % COUNT_CHAINS] = parts[r % COUNT_CHAINS] + hit[r * rows:(r + 1) * rows, :]
            return tuple(parts)

        parts = lax.fori_loop(0, nck, body, (jnp.full((rows, tq), zero),) * COUNT_CHAINS)
        return jnp.sum(sum(p.astype(F32) for p in parts), axis=0, keepdims=True)

    def count16(ref, pred_fn):
        return count(ref, pred_fn, 16, jnp.int16(0), jnp.int16(1))

    def count32(pred_fn):
        return count(key_ref, pred_fn, 8, jnp.float32(0.0), jnp.float32(1.0))

    def bisect16(ref, target):
        def bit_step(i, theta):
            cand = theta + jnp.left_shift(jnp.int32(1), 15 - i)
            cand16 = cand.astype(jnp.int16)
            return jnp.where(count16(ref, lambda k: k >= cand16) >= target, cand, theta)

        return lax.fori_loop(0, 16, bit_step, jnp.full((1, tq), -HALF_BIAS, jnp.int32))

    theta_hi = bisect16(hi_ref, float(topk))
    theta_hi16 = theta_hi.astype(jnp.int16)
    need_lo = topk - count16(hi_ref, lambda k: k > theta_hi16)

    def bucket_chunk(c, carry):
        lo_ref[c] = jnp.where(hi_ref[c] == theta_hi16, lo_ref[c], jnp.int16(-HALF_BIAS))
        return carry

    lax.fori_loop(0, nck, bucket_chunk, 0)
    theta_lo = bisect16(lo_ref, need_lo)
    theta = theta_hi * (2 * HALF_BIAS) + (theta_lo + HALF_BIAS)
    theta = jnp.maximum(theta, KEY_NEG_INF + 1)

    def mask_chunk(c, cnt):
        ge = key_ref[c] >= theta
        madd_ref[c] = jnp.where(ge, 0.0, MASK_NEG)
        return cnt + jnp.sum(jnp.where(ge, 1.0, 0.0), axis=0, keepdims=True)

    cnt_ge = lax.fori_loop(0, nck, mask_chunk, jnp.zeros((1, tq), F32))

    @pl.when(jnp.max(cnt_ge) > topk)
    def _():
        need_eq = topk - count32(lambda kc: kc > theta)
        incl = jnp.where(krow >= qcol, 1.0, 0.0).astype(BF16)

        def tie_chunk(c, run):
            kc = key_ref[c]
            eq = kc == theta
            eqf = jnp.where(eq, 1.0, 0.0)
            pref = jnp.dot(incl, eqf.astype(BF16), preferred_element_type=F32) + run
            eq_add = jnp.where(pref <= need_eq, 0.0, MASK_NEG)
            madd_ref[c] = jnp.where(eq, eq_add, jnp.where(kc > theta, 0.0, MASK_NEG))
            return run + jnp.sum(eqf, axis=0, keepdims=True)

        lax.fori_loop(0, nck, tie_chunk, jnp.zeros((1, tq), F32))

    m_ref[...] = jnp.full(m_ref.shape, -jnp.inf, F32)
    l_ref[...] = jnp.zeros(l_ref.shape, F32)
    acc_ref[...] = jnp.zeros(acc_ref.shape, F32)

    ones_rows = jnp.ones((BF16_ROWS, tq), BF16)

    head_slices = [slice(h * A_HEAD_DIM, (h + 1) * A_HEAD_DIM) for h in range(A_HEADS)]

    def logits(c, h):
        off = pl.multiple_of(c * tq, tq)
        s_ref[h] = jnp.dot(k_ref[pl.ds(off, tq), head_slices[h]], qt_ref[head_slices[h], :],
                           preferred_element_type=F32)

    def attend(c, h, lag):
        hs = head_slices[h]
        s = s_ref[h] + madd_ref[c]
        if lag is not None:
            s = s + bias_ref[h, lag]
        m_old = m_ref[h]
        m_new = jnp.maximum(m_old, jnp.max(s, axis=0, keepdims=True))
        alpha = jnp.exp2(m_old - m_new)
        p = jnp.exp2(s - m_new).astype(BF16)
        pv = jnp.dot(jnp.concatenate([vt_ref[c, hs, :], ones_rows], axis=0), p,
                     preferred_element_type=F32)
        l_ref[h] = alpha * l_ref[h] + pv[A_HEAD_DIM:A_HEAD_DIM + 1]
        acc_ref[h] = alpha * acc_ref[h] + pv[:A_HEAD_DIM]
        m_ref[h] = m_new

    def step(c, lag, prefetch):
        for h in range(A_HEADS):
            attend(c, h, lag)
            if prefetch:
                logits(c + 1, h)

    for h in range(A_HEADS):
        logits(0, h)

    def far_chunk(c, carry):
        step(c, None, True)
        return carry

    lax.fori_loop(0, jnp.maximum(qi - 1, 0), far_chunk, 0)

    @pl.when(qi >= 1)
    def _():
        step(qi - 1, 1, True)

    step(qi, 0, False)

    for h in range(A_HEADS):
        o = acc_ref[h] * (1.0 / l_ref[h])
        o_ref[:, h * A_HEAD_DIM:(h + 1) * A_HEAD_DIM] = o.T.astype(o_ref.dtype)


def _dsa_attention(p16, p32, vt, bias_tiles, bsz, seq):
    tq = min(DSA_TQ, seq)
    nq = seq // tq
    topk = min(TOPK_MAX, seq // 4)
    n = bsz * seq
    one = pl.Buffered(1)
    return pl.pallas_call(
        functools.partial(_dsa_kernel, tq=tq, topk=topk),
        grid=(bsz, nq),
        in_specs=[
            pl.BlockSpec((tq, A_WIDTH), lambda b, i: (b * nq + i, P16_AQ)),
            pl.BlockSpec((tq, 1024), lambda b, i: (b * nq + i, P32_IQ // 1024)),
            pl.BlockSpec((tq, LANES), lambda b, i: (b * nq + i, P32_IW // LANES)),
            pl.BlockSpec((seq, A_WIDTH), lambda b, i: (b, P16_AK), pipeline_mode=one),
            pl.BlockSpec((nq, A_WIDTH, tq), lambda b, i: (b, 0, 0), pipeline_mode=one),
            pl.BlockSpec((seq, LANES), lambda b, i: (b, P32_IK // LANES), pipeline_mode=one),
            pl.BlockSpec(bias_tiles.shape, lambda b, i: (0, 0, 0, 0), pipeline_mode=one),
        ],
        out_specs=pl.BlockSpec((tq, A_WIDTH), lambda b, i: (b * nq + i, 0)),
        out_shape=jax.ShapeDtypeStruct((n, A_WIDTH), BF16),
        scratch_shapes=[pltpu.VMEM((nq, tq, tq), jnp.int32),
                        pltpu.VMEM((nq, tq, tq), jnp.int16),
                        pltpu.VMEM((nq, tq, tq), jnp.int16),
                        pltpu.VMEM((nq, tq, tq), F32),
                        pltpu.VMEM((A_WIDTH, tq), BF16),
                        pltpu.VMEM((IDX_HEADS * LANES, tq), BF16),
                        pltpu.VMEM((LANES, tq), F32),
                        pltpu.VMEM((A_HEADS, 1, tq), F32),
                        pltpu.VMEM((A_HEADS, 1, tq), F32),
                        pltpu.VMEM((A_HEADS, A_HEAD_DIM, tq), F32),
                        pltpu.VMEM((A_HEADS, tq, tq), F32)],
        compiler_params=_cparams(("arbitrary", "arbitrary")),
        name="dsa_attention",
    )(p16, p32, p32, p16, vt, p32, bias_tiles)


def _t5_bucket(rel):
    max_exact = REL_BUCKETS // 2
    relf = jnp.maximum(rel, 1).astype(F32)
    large = max_exact + (jnp.log(relf / max_exact) / math.log(REL_MAX_DIST / max_exact)
                         * (REL_BUCKETS - max_exact)).astype(jnp.int32)
    large = jnp.minimum(large, REL_BUCKETS - 1)
    return jnp.where(rel < max_exact, rel, large)


def _bias_tiles(rel_bias, tq):
    assert tq >= REL_MAX_DIST
    key = jnp.arange(tq, dtype=jnp.int32)[:, None]
    qry = jnp.arange(tq, dtype=jnp.int32)[None, :]
    bucket = jnp.stack([_t5_bucket(jnp.maximum(lag * tq + qry - key, 0)) for lag in range(2)])
    rel = ((rel_bias - rel_bias[REL_BUCKETS - 1:REL_BUCKETS]) * LOG2_E).astype(F32)
    onehot = bucket[None] == jnp.arange(REL_BUCKETS, dtype=jnp.int32)[:, None, None, None]
    return jnp.sum(jnp.where(onehot[:, None], rel[:, :, None, None, None], 0.0), axis=0)


def _hgrn_kernel(q_ref, f_ref, i_ref, g_ref, lb_ref, ng_ref, tril_ref, o_ref, st_ref, *, rows):
    @pl.when(pl.program_id(1) == 0)
    def _():
        st_ref[...] = jnp.zeros_like(st_ref)

    lb = lb_ref[...]
    f = lb + (1.0 - lb) * jax.nn.sigmoid(f_ref[...])
    logf = jnp.log(f)
    kk = 1.0 - f
    g1 = logf.astype(BF16)
    r1 = logf - g1.astype(F32)
    g2 = r1.astype(BF16)
    g3 = (r1 - g2.astype(F32)).astype(BF16)
    tril = tril_ref[...]
    bcum = (jnp.dot(tril, g1, preferred_element_type=F32)
            + jnp.dot(tril, g2, preferred_element_type=F32)
            + jnp.dot(tril, g3, preferred_element_type=F32))

    srow = lax.broadcasted_iota(jnp.int32, (HGRN_C, B_HEAD_DIM), 0)
    trow = lax.broadcasted_iota(jnp.int32, (HGRN_SB, HGRN_C), 0)
    scol = lax.broadcasted_iota(jnp.int32, (HGRN_SB, HGRN_C), 1)
    ng = ng_ref[...]
    q = q_ref[...].astype(F32)
    qb_all = (q * jnp.exp(bcum)).astype(BF16)

    tiles = [(n, h) for n in range(rows // HGRN_C) for h in range(B_HEADS)]

    def rs(n):
        return slice(n * HGRN_C, (n + 1) * HGRN_C)

    def hs(h):
        return slice(h * B_HEAD_DIM, (h + 1) * B_HEAD_DIM)

    a_parts = {}
    for n, h in tiles:
        bc, qc, kc = bcum[rs(n), hs(h)], q[rs(n), hs(h)], kk[rs(n), hs(h)]
        for sb in range(HGRN_C // HGRN_SB):
            s0 = sb * HGRN_SB
            beta = bc[s0 - 1:s0] if sb > 0 else jnp.zeros((1, B_HEAD_DIM), F32)
            qs = (qc[s0:s0 + HGRN_SB] * jnp.exp(bc[s0:s0 + HGRN_SB] - beta)).astype(BF16)
            expo = jnp.where(srow < s0 + HGRN_SB, beta - bc, -jnp.inf)
            ks = (kc * jnp.exp(expo)).astype(BF16)
            a_parts[n, h, sb] = lax.dot_general(qs, ks, NT_DIMS, preferred_element_type=F32)
    intra, upd, dec = {}, {}, {}
    for n, h in tiles:
        a_rows = [jnp.where(scol <= trow + sb * HGRN_SB, a_parts[n, h, sb], 0.0)
                  for sb in range(HGRN_C // HGRN_SB)]
        attn = jnp.concatenate(a_rows, axis=0).astype(BF16)
        vc = i_ref[rs(n), hs(h)]
        intra[n, h] = jnp.dot(attn, vc, preferred_element_type=F32)
        bc = bcum[rs(n), hs(h)]
        blast = bc[HGRN_C - 1:HGRN_C]
        kdec = (kk[rs(n), hs(h)] * jnp.exp(blast - bc)).astype(BF16)
        upd[n, h] = lax.dot_general(vc, kdec, TN_DIMS, preferred_element_type=F32)
        dec[n, h] = jnp.exp(blast)
    for n, h in tiles:
        st = st_ref[h]
        o = intra[n, h] + lax.dot_general(qb_all[rs(n), hs(h)], st.astype(BF16), NT_DIMS,
                                          preferred_element_type=F32)
        st_ref[h] = st * dec[n, h] + upd[n, h]
        ms = jnp.mean(o * o, axis=-1, keepdims=True)
        on = o * lax.rsqrt(ms + EPS) * ng
        o_ref[rs(n), hs(h)] = (on * _silu(g_ref[rs(n), hs(h)].astype(F32))).astype(o_ref.dtype)


def _hgrn2(p16, p32, lb_l, norm_g, bsz, seq):
    rows = min(HGRN_L, seq)
    nj = seq // rows
    n = bsz * seq
    r = jnp.arange(rows, dtype=jnp.int32)
    tril = ((r[:, None] >= r[None, :]) & (r[:, None] // HGRN_C == r[None, :] // HGRN_C)).astype(BF16)

    def col(base):
        return lambda b, j: (b * nj + j, base)

    return pl.pallas_call(
        functools.partial(_hgrn_kernel, rows=rows),
        grid=(bsz, nj),
        in_specs=[
            pl.BlockSpec((rows, B_WIDTH), col(P16_BQ)),
            pl.BlockSpec((rows, B_WIDTH), col(P32_BF // B_WIDTH)),
            pl.BlockSpec((rows, B_WIDTH), col(P16_BI)),
            pl.BlockSpec((rows, B_WIDTH), col(P16_BG)),
            pl.BlockSpec((1, B_WIDTH), lambda b, j: (0, 0)),
            pl.BlockSpec((1, B_HEAD_DIM), lambda b, j: (0, 0)),
            pl.BlockSpec((rows, rows), lambda b, j: (0, 0)),
        ],
        out_specs=pl.BlockSpec((rows, B_WIDTH), lambda b, j: (b * nj + j, 0)),
        out_shape=jax.ShapeDtypeStruct((n, B_WIDTH), BF16),
        scratch_shapes=[pltpu.VMEM((B_HEADS, B_HEAD_DIM, B_HEAD_DIM), F32)],
        compiler_params=_cparams(("arbitrary", "arbitrary")),
        name="hgrn2",
    )(p16, p32, p16, p16, lb_l.reshape(1, B_WIDTH), norm_g.reshape(1, B_HEAD_DIM), tril)


def _ret_kernel(q_ref, k_ref, v_ref, g_ref, cos_ref, sin_ref, idec_ref, qdec_ref, kdec_ref, cdec_ref,
                o_ref, st_ref):
    @pl.when(pl.program_id(1) == 0)
    def _():
        st_ref[...] = jnp.zeros_like(st_ref)

    heads = range(C_HEADS)
    cos = jnp.concatenate([cos_ref[...]] * C_HEADS, axis=1)
    sin_signed = jnp.concatenate([sin_ref[...]] * C_HEADS, axis=1)
    even = lax.broadcasted_iota(jnp.int32, cos.shape, 1) % 2 == 0

    def rot(a):
        swapped = jnp.where(even, pltpu.roll(a, C_QK_WIDTH - 1, 1), pltpu.roll(a, 1, 1))
        return a * cos + swapped * sin_signed

    qr = rot(q_ref[...].astype(F32))
    kr = rot(k_ref[...].astype(F32))
    qk = [slice(h * C_QK_DIM, (h + 1) * C_QK_DIM) for h in heads]
    vs = [slice(h * C_V_DIM, (h + 1) * C_V_DIM) for h in heads]
    attn = [lax.dot_general(qr[:, qk[h]].astype(BF16), kr[:, qk[h]].astype(BF16), NT_DIMS,
                            preferred_element_type=F32) * idec_ref[h] for h in heads]
    inter = [jnp.dot((qr[:, qk[h]] * qdec_ref[h]).astype(BF16), st_ref[h].astype(BF16),
                     preferred_element_type=F32) for h in heads]
    intra = [jnp.dot(attn[h].astype(BF16), v_ref[:, vs[h]], preferred_element_type=F32) for h in heads]
    upd = [jnp.dot((kr[:, qk[h]] * kdec_ref[h]).T.astype(BF16), v_ref[:, vs[h]],
                   preferred_element_type=F32) for h in heads]
    for h in heads:
        st_ref[h] = cdec_ref[h, 0:1, :] * st_ref[h] + upd[h]
        o = intra[h] + inter[h]
        ms = jnp.mean(o * o, axis=-1, keepdims=True)
        o_ref[:, vs[h]] = (_silu(g_ref[:, vs[h]].astype(F32)) * (o * lax.rsqrt(ms + EPS))).astype(o_ref.dtype)


def _retention_tables(seq):
    pos = jnp.arange(seq, dtype=F32)
    theta = jnp.repeat(1.0 / (10000.0 ** jnp.linspace(0.0, 1.0, C_QK_DIM // 2)), 2)
    ang = pos[:, None] * theta[None, :]
    pair_sign = jnp.where(jnp.arange(C_QK_DIM) % 2 == 0, -1.0, 1.0)
    log_gamma = jnp.log(1.0 - 2.0 ** (-5.0 - jnp.arange(C_HEADS, dtype=F32)))
    idx = jnp.arange(RET_C, dtype=F32)
    causal = idx[:, None] >= idx[None, :]
    idec = jnp.exp(jnp.where(causal[None], (idx[:, None] - idx[None, :])[None] * log_gamma[:, None, None],
                             -jnp.inf))
    qdec = jnp.exp((idx + 1.0)[None, :] * log_gamma[:, None])[..., None]
    kdec = jnp.exp((RET_C - 1.0 - idx)[None, :] * log_gamma[:, None])[..., None]
    cdec = jnp.exp(RET_C * log_gamma)[:, None, None]
    return (jnp.cos(ang), jnp.sin(ang) * pair_sign[None, :], idec,
            jnp.broadcast_to(qdec, (C_HEADS, RET_C, C_QK_DIM)),
            jnp.broadcast_to(kdec, (C_HEADS, RET_C, C_QK_DIM)),
            jnp.broadcast_to(cdec, (C_HEADS, 8, C_V_DIM)))


def _retention(p16, tables, bsz, seq):
    cos, sin, idec, qdec, kdec, cdec = tables
    nj = seq // RET_C
    n = bsz * seq
    v_blk = C_V_WIDTH // 1024

    def whole(a):
        return pl.BlockSpec(a.shape, lambda b, j: (0,) * a.ndim)

    return pl.pallas_call(
        _ret_kernel,
        grid=(bsz, nj),
        in_specs=[
            pl.BlockSpec((RET_C, C_QK_WIDTH), lambda b, j: (b * nj + j, P16_CQ)),
            pl.BlockSpec((RET_C, C_QK_WIDTH), lambda b, j: (b * nj + j, P16_CK)),
            pl.BlockSpec((RET_C, C_V_WIDTH), lambda b, j: (b * nj + j, P16_CV // v_blk)),
            pl.BlockSpec((RET_C, C_V_WIDTH), lambda b, j: (b * nj + j, P16_CG // v_blk)),
            pl.BlockSpec((RET_C, C_QK_DIM), lambda b, j: (j, 0)),
            pl.BlockSpec((RET_C, C_QK_DIM), lambda b, j: (j, 0)),
            whole(idec), whole(qdec), whole(kdec), whole(cdec),
        ],
        out_specs=pl.BlockSpec((RET_C, C_V_WIDTH), lambda b, j: (b * nj + j, 0)),
        out_shape=jax.ShapeDtypeStruct((n, C_V_WIDTH), BF16),
        scratch_shapes=[pltpu.VMEM((C_HEADS, C_QK_DIM, C_V_DIM), F32)],
        compiler_params=_cparams(("arbitrary", "arbitrary")),
        name="retention",
    )(p16, p16, p16, p16, cos, sin, idec, qdec, kdec, cdec)


def _merge_kernel(oa_ref, ob_ref, oc_ref, ga_ref, gb_ref, gc_ref, x_ref, mod_ref, g2_ref,
                  wa_ref, wb_ref, wc_ref, wo_ref, wrh_ref, wrl_ref, br_ref,
                  x1_ref, h2_ref, route_ref, *, tiles_per_batch):
    b = pl.program_id(0) // tiles_per_batch

    def gated(o_ref, w_ref, g_ref):
        y = jnp.dot(o_ref[...], w_ref[...], preferred_element_type=F32)
        return jax.nn.sigmoid(g_ref[...].astype(F32)) * y

    merged = gated(oa_ref, wa_ref, ga_ref) + gated(ob_ref, wb_ref, gb_ref) + gated(oc_ref, wc_ref, gc_ref)
    y = jnp.dot(merged.astype(BF16), wo_ref[...], preferred_element_type=F32)
    gt1 = mod_ref[pl.ds(b, 1), 2 * D_MODEL:3 * D_MODEL]
    x1 = x_ref[...] + gt1 * y
    x1_ref[...] = x1
    sh2 = mod_ref[pl.ds(b, 1), 3 * D_MODEL:4 * D_MODEL]
    sc2 = mod_ref[pl.ds(b, 1), 4 * D_MODEL:5 * D_MODEL]
    h2 = _rms_mod(x1, g2_ref[...], sc2, sh2)
    h_hi = h2.astype(BF16)
    h_f = h_hi.astype(F32)
    for j in range(h2_ref.shape[0]):
        h2_ref[j] = h_f[:, j * SC_SUB:(j + 1) * SC_SUB]
    h_lo = (h2 - h_hi.astype(F32)).astype(BF16)
    logits = (jnp.dot(h_hi, wrh_ref[...], preferred_element_type=F32)
              + jnp.dot(h_lo, wrh_ref[...], preferred_element_type=F32)
              + jnp.dot(h_hi, wrl_ref[...], preferred_element_type=F32)) + br_ref[...]
    lane = lax.broadcasted_iota(jnp.int32, logits.shape, 1).astype(F32)
    neg_inf = -jnp.inf

    def first_argmax(vals):
        top = jnp.max(vals, axis=-1, keepdims=True)
        idx = jnp.min(jnp.where(vals == top, lane, float(LANES)), axis=-1, keepdims=True)
        return top, idx

    gl = jnp.where(lane < N_GROUPS, logits, neg_inf)
    gmax, gsel = first_argmax(gl)
    gprob = 1.0 / jnp.sum(jnp.exp(gl - gmax), axis=-1, keepdims=True)
    lo = N_GROUPS + EXPERTS_PER_GROUP * gsel
    el = jnp.where((lane >= lo) & (lane < lo + EXPERTS_PER_GROUP), logits, neg_inf)
    v1, i1 = first_argmax(el)
    el2 = jnp.where(lane == i1, neg_inf, el)
    v2, i2 = first_argmax(el2)
    e2 = jnp.exp(v2 - v1)
    den = 1.0 + e2
    route_ref[...] = jnp.where(lane == 0.0, i1 - N_GROUPS,
                               jnp.where(lane == 1.0, i2 - N_GROUPS,
                                         jnp.where(lane == 2.0, gprob / den,
                                                   jnp.where(lane == 3.0, gprob * (e2 / den), 0.0))))


def _merge(o_a, o_b, o_c, p16, x2, mod_l, g2, wa, wb, wc, wo, wr_hi, wr_lo, br, seq):
    n = x2.shape[0]
    tm = min(512, seq)
    one = pl.Buffered(1)

    def rows(width, cb=0):
        return pl.BlockSpec((tm, width), lambda i: (i, cb))

    def whole(a):
        return pl.BlockSpec(a.shape, lambda i: (0,) * a.ndim, pipeline_mode=one)

    return pl.pallas_call(
        functools.partial(_merge_kernel, tiles_per_batch=seq // tm),
        grid=(n // tm,),
        in_specs=[rows(A_WIDTH), rows(B_WIDTH), rows(C_V_WIDTH),
                  rows(D_MODEL, P16_GA), rows(D_MODEL, P16_GB), rows(D_MODEL, P16_GC),
                  rows(D_MODEL), whole(mod_l), pl.BlockSpec((1, D_MODEL), lambda i: (0, 0)),
                  whole(wa), whole(wb), whole(wc), whole(wo), whole(wr_hi), whole(wr_lo), whole(br)],
        out_specs=[rows(D_MODEL), pl.BlockSpec((SC_PIECES, tm, SC_SUB), lambda i: (0, i, 0)), rows(LANES)],
        out_shape=[jax.ShapeDtypeStruct((n, D_MODEL), F32),
                   jax.ShapeDtypeStruct((SC_PIECES, n, SC_SUB), F32),
                   jax.ShapeDtypeStruct((n, LANES), F32)],
        compiler_params=_cparams(("arbitrary",)),
        name="merge_route",
    )(o_a, o_b, o_c, p16, p16, p16, x2, mod_l, g2.reshape(1, D_MODEL), wa, wb, wc, wo, wr_hi, wr_lo, br)


def _lane_pick(vals, lane, idx):
    return jnp.sum(jnp.where(lane == idx, vals, 0.0), axis=-1, keepdims=True)


def _rank_kernel(route_ref, rk_ref, cnt_ref, run_ref):
    @pl.when(pl.program_id(0) == 0)
    def _():
        run_ref[...] = jnp.zeros_like(run_ref)

    route = route_ref[...]
    tb = route.shape[0]
    lane = lax.broadcasted_iota(jnp.int32, route.shape, 1).astype(F32)
    e1 = _lane_pick(route, lane, 0.0)
    e2 = _lane_pick(route, lane, 1.0)
    sel = jnp.where((lane == e1) | (lane == e2), 1.0, 0.0)
    r = lax.broadcasted_iota(jnp.int32, (tb, tb), 0)
    c = lax.broadcasted_iota(jnp.int32, (tb, tb), 1)
    before = jnp.where(c < r, 1.0, 0.0).astype(BF16)
    rank = jnp.dot(before, sel.astype(BF16), preferred_element_type=F32) + run_ref[0:1, :]
    rk_ref[...] = jnp.where(lane == 0.0, _lane_pick(rank, lane, e1),
                            jnp.where(lane == 1.0, _lane_pick(rank, lane, e2), 0.0))
    run_ref[...] = run_ref[...] + jnp.sum(sel, axis=0, keepdims=True)
    cnt_ref[...] = run_ref[...]


def _expert_ranks(route):
    n = route.shape[0]
    tb = min(256, n)
    return pl.pallas_call(
        _rank_kernel,
        grid=(n // tb,),
        in_specs=[pl.BlockSpec((tb, LANES), lambda i: (i, 0))],
        out_specs=[pl.BlockSpec((tb, LANES), lambda i: (i, 0)),
                   pl.BlockSpec((8, LANES), lambda i: (0, 0))],
        out_shape=[jax.ShapeDtypeStruct((n, LANES), F32), jax.ShapeDtypeStruct((8, LANES), F32)],
        scratch_shapes=[pltpu.VMEM((8, LANES), F32)],
        compiler_params=_cparams(("arbitrary",)),
        name="expert_ranks",
    )(route)


def _plan_kernel(cnt_ref, route_ref, rk_ref, pos_ref, tmap_ref):
    lane_i = lax.broadcasted_iota(jnp.int32, (8, LANES), 1)
    cnt = jnp.where(lane_i < N_EXPERTS, cnt_ref[...], 0.0)
    padded = jnp.floor((cnt + (MOE_TM - 1)) * (1.0 / MOE_TM)) * MOE_TM
    r = lax.broadcasted_iota(jnp.int32, (LANES, LANES), 0)
    c = lax.broadcasted_iota(jnp.int32, (LANES, LANES), 1)
    base = jnp.dot(padded, jnp.where(r < c, 1.0, 0.0), precision=lax.Precision.HIGHEST,
                   preferred_element_type=F32)

    route = route_ref[...]
    lane = lax.broadcasted_iota(jnp.int32, route.shape, 1).astype(F32)
    rk = rk_ref[...]
    base_row = base[0:1, :]
    pos1 = _lane_pick(base_row, lane, _lane_pick(route, lane, 0.0)) + _lane_pick(rk, lane, 0.0)
    pos2 = _lane_pick(base_row, lane, _lane_pick(route, lane, 1.0)) + _lane_pick(rk, lane, 1.0)
    pos_ref[...] = jnp.where(lane == 0.0, pos1, jnp.where(lane == 1.0, pos2, 0.0)).astype(jnp.int32)

    @pl.when(pl.program_id(0) == 0)
    def _():
        nt = tmap_ref.shape[0]
        tlane = lax.broadcasted_iota(jnp.int32, (nt, LANES), 1)
        start = (lax.broadcasted_iota(jnp.int32, (nt, LANES), 0) * MOE_TM).astype(F32)
        end_row = jnp.where(tlane < N_EXPERTS, base_row + padded[0:1, :], 3e38)
        expert = jnp.sum(jnp.where(end_row <= start, 1.0, 0.0), axis=-1, keepdims=True)
        expert_c = jnp.minimum(expert, N_EXPERTS - 1.0)
        tl = tlane.astype(F32)
        left = _lane_pick(cnt[0:1, :], tl, expert_c) - (start[:, 0:1] - _lane_pick(base_row, tl, expert_c))
        valid = jnp.where(expert < N_EXPERTS, jnp.clip(left, 0.0, float(MOE_TM)), 0.0)
        tmap_ref[...] = jnp.where(tlane == 0, expert_c, jnp.where(tlane == 1, valid, 0.0)).astype(jnp.int32)


def _expert_plan(cnt, route, rk, n_tiles):
    n = route.shape[0]
    tb = min(256, n)
    nt_pad = -(-n_tiles // 8) * 8
    return pl.pallas_call(
        _plan_kernel,
        grid=(n // tb,),
        in_specs=[pl.BlockSpec((8, LANES), lambda i: (0, 0)),
                  pl.BlockSpec((tb, LANES), lambda i: (i, 0)),
                  pl.BlockSpec((tb, LANES), lambda i: (i, 0))],
        out_specs=[pl.BlockSpec((tb, LANES), lambda i: (i, 0)),
                   pl.BlockSpec((nt_pad, LANES), lambda i: (0, 0))],
        out_shape=[jax.ShapeDtypeStruct((n, LANES), jnp.int32),
                   jax.ShapeDtypeStruct((nt_pad, LANES), jnp.int32)],
        compiler_params=_cparams(("arbitrary",)),
        name="expert_plan",
    )(cnt, route, rk)


def _sc_mesh():
    return plsc.VectorSubcoreMesh(core_axis_name="c", subcore_axis_name="s")


def _sc_scatter_rows(src, idx, out_rows):
    m = idx.shape[0]
    n_src_win = src.shape[0] // SC_WINDOW

    @functools.partial(pl.kernel, out_type=jax.ShapeDtypeStruct((out_rows, src.shape[1]), src.dtype),
                       mesh=_sc_mesh(), scratch_types=[])
    def scatter(x_hbm, i_hbm, o_hbm):
        def body(x_vmem, i_vmem):
            pltpu.sync_copy(x_vmem, o_hbm.at[i_vmem.at[0]])

        pltpu.emit_pipeline(
            body, grid=(m // SC_WINDOW,),
            in_specs=[pl.BlockSpec((SC_WINDOW, src.shape[1]), lambda i: (i % n_src_win, 0)),
                      pl.BlockSpec((1, SC_WINDOW), lambda i: (0, i))],
            out_specs=[], core_axis_name=("c", "s"),
            dimension_semantics=(pltpu.PARALLEL,))(x_hbm, i_hbm)

    return scatter(src, idx.reshape(1, m))


def _sc_gather_rows(table, idx):
    m = idx.shape[0]

    @functools.partial(pl.kernel, out_type=jax.ShapeDtypeStruct((m, table.shape[1]), table.dtype),
                       mesh=_sc_mesh(), scratch_types=[])
    def gather(x_hbm, i_hbm, o_hbm):
        def body(i_vmem, o_vmem):
            pltpu.sync_copy(x_hbm.at[i_vmem.at[0]], o_vmem)

        pltpu.emit_pipeline(
            body, grid=(m // SC_WINDOW,),
            in_specs=[pl.BlockSpec((1, SC_WINDOW), lambda i: (0, i))],
            out_specs=[pl.BlockSpec((SC_WINDOW, table.shape[1]), lambda i: (i, 0))],
            core_axis_name=("c", "s"),
            dimension_semantics=(pltpu.PARALLEL,))(i_hbm, o_hbm)

    return gather(table, idx.reshape(1, m))


def _piece_row_index(pos, rows):
    return (jnp.arange(SC_PIECES, dtype=jnp.int32)[:, None] * rows + pos[None, :]).reshape(-1)


def _grouped_kernel(te_ref, tv_ref, x_ref, wg_ref, wu_ref, wd_ref, o_ref):
    valid = tv_ref[pl.program_id(0)]

    @pl.when(valid > 0)
    def _():
        x = jnp.concatenate([x_ref[j] for j in range(SC_PIECES)], axis=1)
        row = lax.broadcasted_iota(jnp.int32, x.shape, 0)
        x = jnp.where(row < valid, x, 0.0).astype(BF16)
        a = jnp.dot(x, wg_ref[0].astype(BF16), preferred_element_type=F32)
        u = jnp.dot(x, wu_ref[0].astype(BF16), preferred_element_type=F32)
        hm = (_silu(a) * u).astype(BF16)
        o = jnp.dot(hm, wd_ref[0].astype(BF16), preferred_element_type=F32)
        for j in range(SC_PIECES):
            o_ref[j] = o[:, j * SC_SUB:(j + 1) * SC_SUB]

    @pl.when(valid <= 0)
    def _():
        o_ref[...] = jnp.zeros_like(o_ref)


def _grouped_experts(tile_expert, tile_valid, xs, wg, wu, wd):
    n_tiles = tile_expert.shape[0]
    rows_block = pl.BlockSpec((SC_PIECES, MOE_TM, SC_SUB), lambda i, te, tv: (0, i, 0))
    return pl.pallas_call(
        _grouped_kernel,
        grid_spec=pltpu.PrefetchScalarGridSpec(
            num_scalar_prefetch=2,
            grid=(n_tiles,),
            in_specs=[rows_block,
                      pl.BlockSpec((1, D_MODEL, D_EXPERT), lambda i, te, tv: (te[i], 0, 0)),
                      pl.BlockSpec((1, D_MODEL, D_EXPERT), lambda i, te, tv: (te[i], 0, 0)),
                      pl.BlockSpec((1, D_EXPERT, D_MODEL), lambda i, te, tv: (te[i], 0, 0))],
            out_specs=rows_block),
        out_shape=jax.ShapeDtypeStruct(xs.shape, F32),
        compiler_params=_cparams(("arbitrary",)),
        name="grouped_experts",
    )(tile_expert, tile_valid, xs, wg, wu, wd)


def _combine_kernel(x1_ref, y_ref, route_ref, mod_ref, o_ref, *, tiles_per_batch):
    b = pl.program_id(0) // tiles_per_batch
    gt2 = mod_ref[pl.ds(b, 1), 5 * D_MODEL:6 * D_MODEL]
    route = route_ref[...]
    lane = lax.broadcasted_iota(jnp.int32, route.shape, 1).astype(F32)
    w1 = _lane_pick(route, lane, 2.0)
    w2 = _lane_pick(route, lane, 3.0)
    y = jnp.concatenate([w1 * y_ref[0, j] + w2 * y_ref[1, j] for j in range(SC_PIECES)], axis=1)
    o_ref[...] = x1_ref[...] + gt2 * y


def _combine(x1, y2, route, mod_l, seq):
    n = x1.shape[0]
    tm = min(1024, seq)
    return pl.pallas_call(
        functools.partial(_combine_kernel, tiles_per_batch=seq // tm),
        grid=(n // tm,),
        in_specs=[pl.BlockSpec((tm, D_MODEL), lambda i: (i, 0)),
                  pl.BlockSpec((2, SC_PIECES, tm, SC_SUB), lambda i: (0, 0, i, 0)),
                  pl.BlockSpec((tm, LANES), lambda i: (i, 0)),
                  pl.BlockSpec(mod_l.shape, lambda i: (0, 0))],
        out_specs=pl.BlockSpec((tm, D_MODEL), lambda i: (i, 0)),
        out_shape=jax.ShapeDtypeStruct((n, D_MODEL), F32),
        compiler_params=_cparams(("arbitrary",)),
        name="moe_combine",
    )(x1, y2, route, mod_l)


def _moe(h2, route, x1, mod_l, wg, wu, wd, seq):
    n = h2.shape[1]
    n_tiles = (2 * n) // MOE_TM + N_EXPERTS
    rows = n_tiles * MOE_TM
    rk, cnt = _expert_ranks(route)
    pos, tmap = _expert_plan(cnt, route, rk, n_tiles)
    idx = jnp.concatenate([_piece_row_index(pos[:, 0], rows), _piece_row_index(pos[:, 1], rows)])
    xs = _sc_scatter_rows(h2.reshape(SC_PIECES * n, SC_SUB), idx, SC_PIECES * rows)
    ys = _grouped_experts(tmap[:n_tiles, 0], tmap[:n_tiles, 1], xs.reshape(SC_PIECES, rows, SC_SUB),
                          wg, wu, wd)
    y2 = _sc_gather_rows(ys.reshape(SC_PIECES * rows, SC_SUB), idx).reshape(2, SC_PIECES, n, SC_SUB)
    return _combine(x1, y2, route, mod_l, seq)


def _final_norm_kernel(x_ref, g_ref, o_ref):
    x = x_ref[...]
    ms = jnp.mean(x * x, axis=-1, keepdims=True)
    o_ref[...] = x * lax.rsqrt(ms + EPS) * g_ref[...]


def _final_norm(x2, g, seq):
    n = x2.shape[0]
    tm = min(1024, seq)
    return pl.pallas_call(
        _final_norm_kernel,
        grid=(n // tm,),
        in_specs=[pl.BlockSpec((tm, D_MODEL), lambda i: (i, 0)),
                  pl.BlockSpec((1, D_MODEL), lambda i: (0, 0))],
        out_specs=pl.BlockSpec((tm, D_MODEL), lambda i: (i, 0)),
        out_shape=jax.ShapeDtypeStruct((n, D_MODEL), F32),
        compiler_params=_cparams(("arbitrary",)),
        name="final_norm",
    )(x2, g.reshape(1, D_MODEL))


def _pack_w_in(w_in_l):
    offs = [0]
    for s in IN_SPLITS:
        offs.append(offs[-1] + s)
    (aq, ak, av, iq, ik, iw, bq, bf, bi, bg, cq, ck, cv, cg, ga, gb, gc) = [
        w_in_l[:, offs[i]:offs[i + 1]] for i in range(len(IN_SPLITS))]

    w16 = jnp.concatenate(
        [cv, cg, aq * (A_HEAD_DIM ** -0.5 * LOG2_E), ak, bq, bi, bg, cq, ck * (C_QK_DIM ** -0.5), ga, gb, gc],
        axis=1).astype(BF16)
    iq_p = jnp.pad(iq.reshape(D_MODEL, IDX_HEADS, IDX_DIM),
                   ((0, 0), (0, 0), (0, LANES - IDX_DIM))).reshape(D_MODEL, IDX_HEADS * LANES)
    w32 = jnp.concatenate(
        [bf, iq_p, jnp.pad(ik, ((0, 0), (0, LANES - IDX_DIM))),
         jnp.pad(iw, ((0, 0), (0, LANES - IDX_HEADS)))], axis=1).astype(BF16)
    return w16, w32, av.T.astype(BF16)


def _split_bf16(w):
    hi = w.astype(BF16)
    return hi, (w - hi.astype(F32)).astype(BF16)


def kernel(x, c, rel_bias, hgrn_lb_raw, norm1_g, norm2_g, ada_w, ada_b, w_in, hgrn_norm_g, w_branch_a,
           w_branch_b, w_branch_c, w_out, router_group_w, router_group_b, router_expert_w,
           router_expert_b, expert_w_gate, expert_w_up, expert_w_down, final_norm_g):
    bsz, seq, _ = x.shape
    depth = w_in.shape[0]
    n = bsz * seq
    x2 = x.reshape(n, D_MODEL)
    tq = min(DSA_TQ, seq)

    lb_all = _hgrn_lower_bounds(hgrn_lb_raw)
    c_pad = jnp.pad(c, ((0, (-bsz) % 8), (0, 0)))
    mod = _ada_mod(c_pad, ada_w, ada_b)
    bias_tiles = _bias_tiles(rel_bias, tq)
    ret_tables = _retention_tables(seq)

    for l in range(depth):
        w16, w32, wvt = _pack_w_in(w_in[l])
        p16 = _norm_project(x2, mod[l], norm1_g[l], w16, BF16, 1024, seq, "proj_bf16")
        p32 = _norm_project(x2, mod[l], norm1_g[l], w32, F32, 768, seq, "proj_f32")
        vt = _norm_project_t(x2, mod[l], norm1_g[l], wvt, tq, seq, "proj_vt")
        o_a = _dsa_attention(p16, p32, vt, bias_tiles, bsz, seq)
        o_b = _hgrn2(p16, p32, lb_all[l], hgrn_norm_g[l], bsz, seq)
        o_c = _retention(p16, ret_tables, bsz, seq)
        wr = jnp.concatenate([router_group_w[l], router_expert_w[l],
                              jnp.zeros((D_MODEL, LANES - N_GROUPS - N_EXPERTS), F32)], axis=1)
        br = jnp.concatenate([router_group_b[l], router_expert_b[l],
                              jnp.zeros((LANES - N_GROUPS - N_EXPERTS,), F32)]).reshape(1, LANES)
        wr_hi, wr_lo = _split_bf16(wr)
        x1, h2, route = _merge(o_a, o_b, o_c, p16, x2, mod[l], norm2_g[l],
                               w_branch_a[l].astype(BF16), w_branch_b[l].astype(BF16),
                               w_branch_c[l].astype(BF16), w_out[l].astype(BF16),
                               wr_hi, wr_lo, br, seq)
        x2 = _moe(h2, route, x1, mod[l], expert_w_gate[l], expert_w_up[l], expert_w_down[l], seq)

    return _final_norm(x2, final_norm_g, seq).reshape(bsz, seq, D_MODEL)
```

```python
import functools
import math

import jax
import jax.numpy as jnp
from jax import lax
from jax.experimental import pallas as pl
from jax.experimental.pallas import tpu as pltpu
from jax.experimental.pallas import tpu_sc as plsc

F32 = jnp.float32
BF16 = jnp.bfloat16

D_MODEL = 1024
A_HEADS = 8
A_HEAD_DIM = 128
IDX_HEADS = 8
IDX_DIM = 64
TOPK_MAX = 256
REL_BUCKETS = 32
REL_MAX_DIST = 128
B_HEADS = 8
B_HEAD_DIM = 128
C_HEADS = 4
C_QK_DIM = 256
C_V_DIM = 512
N_GROUPS = 4
EXPERTS_PER_GROUP = 8
N_EXPERTS = 32
D_EXPERT = 512
EPS = 1e-6

A_WIDTH = A_HEADS * A_HEAD_DIM
B_WIDTH = B_HEADS * B_HEAD_DIM
C_QK_WIDTH = C_HEADS * C_QK_DIM
C_V_WIDTH = C_HEADS * C_V_DIM
IN_SPLITS = (A_WIDTH, A_WIDTH, A_WIDTH, IDX_HEADS * IDX_DIM, IDX_DIM, IDX_HEADS,
             B_WIDTH, B_WIDTH, B_WIDTH, B_WIDTH,
             C_QK_WIDTH, C_QK_WIDTH, C_V_WIDTH, C_V_WIDTH,
             D_MODEL, D_MODEL, D_MODEL)

LANES = 128
BF16_ROWS = 16
VMEM_LIMIT = 56 * 1024 * 1024

P16_CV, P16_CG = 0, 2
P16_AQ, P16_AK, P16_BQ, P16_BI, P16_BG, P16_CQ, P16_CK, P16_GA, P16_GB, P16_GC = range(4, 14)
P32_BF = 0
P32_IQ = 1024
P32_IK = 2048
P32_IW = 2176

DSA_TQ = 256
HGRN_L = 256
HGRN_C = 64
HGRN_SB = 16
RET_C = 128
KEY_NEG_INF = -2139095041
HALF_BIAS = 32768
MASK_NEG = -1e30
LOG2_E = math.log2(math.e)
COUNT_CHAINS = 4
MOE_TM = 256
SC_WINDOW = 128
SC_SUB = 256
SC_PIECES = D_MODEL // SC_SUB

NT_DIMS = (((1,), (1,)), ((), ()))
TN_DIMS = (((0,), (0,)), ((), ()))


def _cparams(sem):
    return pltpu.CompilerParams(dimension_semantics=sem, vmem_limit_bytes=VMEM_LIMIT)


def _silu(x):
    return x * jax.nn.sigmoid(x)


def _lb_kernel(raw_ref, o_ref):
    raw = raw_ref[...]
    m = jnp.max(raw, axis=0, keepdims=True)
    e = jnp.exp(raw - m)
    soft = e / jnp.sum(e, axis=0, keepdims=True)
    run = jnp.zeros_like(soft[0:1])
    for l in range(raw.shape[0]):
        run = run + soft[l:l + 1]
        o_ref[l:l + 1, :] = run - soft[0:1]


def _hgrn_lower_bounds(raw):
    return pl.pallas_call(
        _lb_kernel, out_shape=jax.ShapeDtypeStruct(raw.shape, F32), name="hgrn_lb")(raw)


def _ada_kernel(c_ref, w_ref, b_ref, o_ref):
    a = _silu(c_ref[...])
    o_ref[0] = jnp.dot(a, w_ref[0], precision=lax.Precision.HIGHEST,
                       preferred_element_type=F32) + b_ref[0]


def _ada_mod(c_pad, ada_w, ada_b):
    depth = ada_w.shape[0]
    rows = c_pad.shape[0]
    return pl.pallas_call(
        _ada_kernel,
        grid=(depth, 6),
        in_specs=[pl.BlockSpec((rows, D_MODEL), lambda l, j: (0, 0)),
                  pl.BlockSpec((1, D_MODEL, D_MODEL), lambda l, j: (l, 0, j)),
                  pl.BlockSpec((1, 1, D_MODEL), lambda l, j: (l, 0, j))],
        out_specs=pl.BlockSpec((1, rows, D_MODEL), lambda l, j: (l, 0, j)),
        out_shape=jax.ShapeDtypeStruct((depth, rows, 6 * D_MODEL), F32),
        compiler_params=_cparams(("arbitrary", "arbitrary")),
        name="ada_mod",
    )(c_pad, ada_w, ada_b.reshape(depth, 1, 6 * D_MODEL))


def _rms_mod(x, g, sc, sh):
    ms = jnp.mean(x * x, axis=-1, keepdims=True)
    return (x * lax.rsqrt(ms + EPS) * g) * (1.0 + sc) + sh


def _norm1(x_ref, mod_ref, g_ref, b):
    sh = mod_ref[pl.ds(b, 1), 0:D_MODEL]
    sc = mod_ref[pl.ds(b, 1), D_MODEL:2 * D_MODEL]
    return _rms_mod(x_ref[...], g_ref[...], sc, sh).astype(BF16)


def _proj_kernel(x_ref, mod_ref, g_ref, w_ref, o_ref, h_ref, *, tiles_per_batch):
    @pl.when(pl.program_id(1) == 0)
    def _():
        h_ref[...] = _norm1(x_ref, mod_ref, g_ref, pl.program_id(0) // tiles_per_batch)

    o_ref[...] = jnp.dot(h_ref[...], w_ref[...], preferred_element_type=F32).astype(o_ref.dtype)


def _norm_project(x2, mod_l, g, w, out_dtype, tn, seq, name):
    n = x2.shape[0]
    width = w.shape[1]
    tm = min(1024, seq)
    return pl.pallas_call(
        functools.partial(_proj_kernel, tiles_per_batch=seq // tm),
        grid=(n // tm, width // tn),
        in_specs=[pl.BlockSpec((tm, D_MODEL), lambda i, j: (i, 0)),
                  pl.BlockSpec(mod_l.shape, lambda i, j: (0, 0)),
                  pl.BlockSpec((1, D_MODEL), lambda i, j: (0, 0)),
                  pl.BlockSpec((D_MODEL, tn), lambda i, j: (0, j))],
        out_specs=pl.BlockSpec((tm, tn), lambda i, j: (i, j)),
        out_shape=jax.ShapeDtypeStruct((n, width), out_dtype),
        scratch_shapes=[pltpu.VMEM((tm, D_MODEL), BF16)],
        compiler_params=_cparams(("arbitrary", "arbitrary")),
        name=name,
    )(x2, mod_l, g.reshape(1, D_MODEL), w)


def _proj_t_kernel(x_ref, mod_ref, g_ref, wt_ref, o_ref, *, tiles_per_batch, chunk):
    h = _norm1(x_ref, mod_ref, g_ref, pl.program_id(0) // tiles_per_batch)
    res = lax.dot_general(wt_ref[...], h, NT_DIMS, preferred_element_type=F32)
    for ci in range(o_ref.shape[0]):
        o_ref[ci] = res[:, ci * chunk:(ci + 1) * chunk].astype(o_ref.dtype)


def _norm_project_t(x2, mod_l, g, wt, chunk, seq, name):
    n = x2.shape[0]
    cols = wt.shape[0]
    tm = min(1024, seq)
    return pl.pallas_call(
        functools.partial(_proj_t_kernel, tiles_per_batch=seq // tm, chunk=chunk),
        grid=(n // tm,),
        in_specs=[pl.BlockSpec((tm, D_MODEL), lambda i: (i, 0)),
                  pl.BlockSpec(mod_l.shape, lambda i: (0, 0)),
                  pl.BlockSpec((1, D_MODEL), lambda i: (0, 0)),
                  pl.BlockSpec((cols, D_MODEL), lambda i: (0, 0))],
        out_specs=pl.BlockSpec((tm // chunk, cols, chunk), lambda i: (i, 0, 0)),
        out_shape=jax.ShapeDtypeStruct((n // chunk, cols, chunk), BF16),
        compiler_params=_cparams(("arbitrary",)),
        name=name,
    )(x2, mod_l, g.reshape(1, D_MODEL), wt)


def _dsa_kernel(q_ref, iq_ref, iw_ref, k_ref, vt_ref, ik_ref, bias_ref, o_ref,
                key_ref, hi_ref, lo_ref, madd_ref, qt_ref, iqt_ref, iwt_ref, m_ref, l_ref, acc_ref, s_ref,
                *, tq, topk):
    qi = pl.program_id(1)
    nck = qi + 1
    idx_scale = (IDX_HEADS * IDX_DIM) ** -0.5

    for h in range(A_HEADS):
        hs = slice(h * LANES, (h + 1) * LANES)
        qt_ref[hs, :] = q_ref[:, hs].astype(F32).T.astype(BF16)
        iqt_ref[hs, :] = iq_ref[:, hs].T.astype(BF16)
    iwt_ref[...] = (iw_ref[...] * idx_scale).T

    krow = lax.broadcasted_iota(jnp.int32, (tq, tq), 0)
    qcol = lax.broadcasted_iota(jnp.int32, (tq, tq), 1)

    def score_chunk(c, carry):
        off = pl.multiple_of(c * tq, tq)
        ikc = ik_ref[pl.ds(off, tq), :].astype(BF16)
        acc = jnp.zeros((tq, tq), F32)
        for h in range(IDX_HEADS):
            s = jnp.dot(ikc, iqt_ref[h * LANES:(h + 1) * LANES, :], preferred_element_type=F32)
            acc = acc + jnp.maximum(s, 0.0) * iwt_ref[h:h + 1, :]
        acc = jnp.where(acc == 0.0, 0.0, acc)
        acc = jnp.where(krow + (c - qi) * tq <= qcol, acc, -jnp.inf)
        kb = pltpu.bitcast(acc, jnp.int32)
        key = jnp.where(kb < 0, kb ^ jnp.int32(0x7FFFFFFF), kb)
        key_ref[c] = key
        hi_ref[c] = jnp.right_shift(key, 16).astype(jnp.int16)
        lo_ref[c] = ((key & 0xFFFF) - HALF_BIAS).astype(jnp.int16)
        return carry

    def paired_loop(count_, fn):
        def pair(i, carry):
            fn(2 * i, carry)
            fn(2 * i + 1, carry)
            return carry

        lax.fori_loop(0, count_ // 2, pair, 0)

        @pl.when(count_ % 2 == 1)
        def _():
            fn(count_ - 1, 0)

    paired_loop(nck, score_chunk)

    def count(ref, pred_fn, rows, zero, one):
        def body(c, parts):
            hit = jnp.where(pred_fn(ref[c]), one, zero)
            parts = list(parts)
            for r in range(tq // rows):
                parts[r % COUNT_CHAINS] = parts[r % COUNT_CHAINS] + hit[r * rows:(r + 1) * rows, :]
            return tuple(parts)

        parts = lax.fori_loop(0, nck, body, (jnp.full((rows, tq), zero),) * COUNT_CHAINS)
        return jnp.sum(sum(p.astype(F32) for p in parts), axis=0, keepdims=True)

    def count16(ref, pred_fn):
        return count(ref, pred_fn, 16, jnp.int16(0), jnp.int16(1))

    def count32(pred_fn):
        return count(key_ref, pred_fn, 8, jnp.float32(0.0), jnp.float32(1.0))

    def bisect16(ref, target):
        def bit_step(i, theta):
            cand = theta + jnp.left_shift(jnp.int32(1), 15 - i)
            cand16 = cand.astype(jnp.int16)
            return jnp.where(count16(ref, lambda k: k >= cand16) >= target, cand, theta)

        return lax.fori_loop(0, 16, bit_step, jnp.full((1, tq), -HALF_BIAS, jnp.int32))

    theta_hi = bisect16(hi_ref, float(topk))
    theta_hi16 = theta_hi.astype(jnp.int16)
    need_lo = topk - count16(hi_ref, lambda k: k > theta_hi16)

    def bucket_chunk(c, carry):
        lo_ref[c] = jnp.where(hi_ref[c] == theta_hi16, lo_ref[c], jnp.int16(-HALF_BIAS))
        return carry

    lax.fori_loop(0, nck, bucket_chunk, 0)
    theta_lo = bisect16(lo_ref, need_lo)
    theta = theta_hi * (2 * HALF_BIAS) + (theta_lo + HALF_BIAS)
    theta = jnp.maximum(theta, KEY_NEG_INF + 1)

    def mask_chunk(c, cnt):
        ge = key_ref[c] >= theta
        madd_ref[c] = jnp.where(ge, 0.0, MASK_NEG)
        return cnt + jnp.sum(jnp.where(ge, 1.0, 0.0), axis=0, keepdims=True)

    cnt_ge = lax.fori_loop(0, nck, mask_chunk, jnp.zeros((1, tq), F32))

    @pl.when(jnp.max(cnt_ge) > topk)
    def _():
        need_eq = topk - count32(lambda kc: kc > theta)
        incl = jnp.where(krow >= qcol, 1.0, 0.0).astype(BF16)

        def tie_chunk(c, run):
            kc = key_ref[c]
            eq = kc == theta
            eqf = jnp.where(eq, 1.0, 0.0)
            pref = jnp.dot(incl, eqf.astype(BF16), preferred_element_type=F32) + run
            eq_add = jnp.where(pref <= need_eq, 0.0, MASK_NEG)
            madd_ref[c] = jnp.where(eq, eq_add, jnp.where(kc > theta, 0.0, MASK_NEG))
            return run + jnp.sum(eqf, axis=0, keepdims=True)

        lax.fori_loop(0, nck, tie_chunk, jnp.zeros((1, tq), F32))

    m_ref[...] = jnp.full(m_ref.shape, -jnp.inf, F32)
    l_ref[...] = jnp.zeros(l_ref.shape, F32)
    acc_ref[...] = jnp.zeros(acc_ref.shape, F32)

    ones_rows = jnp.ones((BF16_ROWS, tq), BF16)

    head_slices = [slice(h * A_HEAD_DIM, (h + 1) * A_HEAD_DIM) for h in range(A_HEADS)]

    def logits(c, h):
        off = pl.multiple_of(c * tq, tq)
        s_ref[h] = jnp.dot(k_ref[pl.ds(off, tq), head_slices[h]], qt_ref[head_slices[h], :],
                           preferred_element_type=F32)

    def attend(c, h, lag):
        hs = head_slices[h]
        s = s_ref[h] + madd_ref[c]
        if lag is not None:
            s = s + bias_ref[h, lag]
        m_old = m_ref[h]
        m_new = jnp.maximum(m_old, jnp.max(s, axis=0, keepdims=True))
        alpha = jnp.exp2(m_old - m_new)
        p = jnp.exp2(s - m_new).astype(BF16)
        pv = jnp.dot(jnp.concatenate([vt_ref[c, hs, :], ones_rows], axis=0), p,
                     preferred_element_type=F32)
        l_ref[h] = alpha * l_ref[h] + pv[A_HEAD_DIM:A_HEAD_DIM + 1]
        acc_ref[h] = alpha * acc_ref[h] + pv[:A_HEAD_DIM]
        m_ref[h] = m_new

    def step(c, lag, prefetch):
        for h in range(A_HEADS):
            attend(c, h, lag)
            if prefetch:
                logits(c + 1, h)

    for h in range(A_HEADS):
        logits(0, h)

    def far_chunk(c, carry):
        step(c, None, True)
        return carry

    paired_loop(jnp.maximum(qi - 1, 0), far_chunk)

    @pl.when(qi >= 1)
    def _():
        step(qi - 1, 1, True)

    step(qi, 0, False)

    for h in range(A_HEADS):
        o = acc_ref[h] * (1.0 / l_ref[h])
        o_ref[:, h * A_HEAD_DIM:(h + 1) * A_HEAD_DIM] = o.T.astype(o_ref.dtype)


def _dsa_attention(p16, p32, vt, bias_tiles, bsz, seq):
    tq = min(DSA_TQ, seq)
    nq = seq // tq
    topk = min(TOPK_MAX, seq // 4)
    n = bsz * seq
    one = pl.Buffered(1)
    return pl.pallas_call(
        functools.partial(_dsa_kernel, tq=tq, topk=topk),
        grid=(bsz, nq),
        in_specs=[
            pl.BlockSpec((tq, A_WIDTH), lambda b, i: (b * nq + i, P16_AQ)),
            pl.BlockSpec((tq, 1024), lambda b, i: (b * nq + i, P32_IQ // 1024)),
            pl.BlockSpec((tq, LANES), lambda b, i: (b * nq + i, P32_IW // LANES)),
            pl.BlockSpec((seq, A_WIDTH), lambda b, i: (b, P16_AK), pipeline_mode=one),
            pl.BlockSpec((nq, A_WIDTH, tq), lambda b, i: (b, 0, 0), pipeline_mode=one),
            pl.BlockSpec((seq, LANES), lambda b, i: (b, P32_IK // LANES), pipeline_mode=one),
            pl.BlockSpec(bias_tiles.shape, lambda b, i: (0, 0, 0, 0), pipeline_mode=one),
        ],
        out_specs=pl.BlockSpec((tq, A_WIDTH), lambda b, i: (b * nq + i, 0)),
        out_shape=jax.ShapeDtypeStruct((n, A_WIDTH), BF16),
        scratch_shapes=[pltpu.VMEM((nq, tq, tq), jnp.int32),
                        pltpu.VMEM((nq, tq, tq), jnp.int16),
                        pltpu.VMEM((nq, tq, tq), jnp.int16),
                        pltpu.VMEM((nq, tq, tq), F32),
                        pltpu.VMEM((A_WIDTH, tq), BF16),
                        pltpu.VMEM((IDX_HEADS * LANES, tq), BF16),
                        pltpu.VMEM((LANES, tq), F32),
                        pltpu.VMEM((A_HEADS, 1, tq), F32),
                        pltpu.VMEM((A_HEADS, 1, tq), F32),
                        pltpu.VMEM((A_HEADS, A_HEAD_DIM, tq), F32),
                        pltpu.VMEM((A_HEADS, tq, tq), F32)],
        compiler_params=_cparams(("arbitrary", "arbitrary")),
        name="dsa_attention",
    )(p16, p32, p32, p16, vt, p32, bias_tiles)


def _t5_bucket(rel):
    max_exact = REL_BUCKETS // 2
    relf = jnp.maximum(rel, 1).astype(F32)
    large = max_exact + (jnp.log(relf / max_exact) / math.log(REL_MAX_DIST / max_exact)
                         * (REL_BUCKETS - max_exact)).astype(jnp.int32)
    large = jnp.minimum(large, REL_BUCKETS - 1)
    return jnp.where(rel < max_exact, rel, large)


def _bias_tiles(rel_bias, tq):
    assert tq >= REL_MAX_DIST
    key = jnp.arange(tq, dtype=jnp.int32)[:, None]
    qry = jnp.arange(tq, dtype=jnp.int32)[None, :]
    bucket = jnp.stack([_t5_bucket(jnp.maximum(lag * tq + qry - key, 0)) for lag in range(2)])
    rel = ((rel_bias - rel_bias[REL_BUCKETS - 1:REL_BUCKETS]) * LOG2_E).astype(F32)
    onehot = bucket[None] == jnp.arange(REL_BUCKETS, dtype=jnp.int32)[:, None, None, None]
    return jnp.sum(jnp.where(onehot[:, None], rel[:, :, None, None, None], 0.0), axis=0)


def _hgrn_kernel(q_ref, f_ref, i_ref, g_ref, lb_ref, ng_ref, tril_ref, o_ref, st_ref, *, rows):
    @pl.when(pl.program_id(1) == 0)
    def _():
        st_ref[...] = jnp.zeros_like(st_ref)

    lb = lb_ref[...]
    f = lb + (1.0 - lb) * jax.nn.sigmoid(f_ref[...])
    logf = jnp.log(f)
    kk = 1.0 - f
    g1 = logf.astype(BF16)
    r1 = logf - g1.astype(F32)
    g2 = r1.astype(BF16)
    g3 = (r1 - g2.astype(F32)).astype(BF16)
    tril = tril_ref[...]
    bcum = (jnp.dot(tril, g1, preferred_element_type=F32)
            + jnp.dot(tril, g2, preferred_element_type=F32)
            + jnp.dot(tril, g3, preferred_element_type=F32))

    srow = lax.broadcasted_iota(jnp.int32, (HGRN_C, B_HEAD_DIM), 0)
    trow = lax.broadcasted_iota(jnp.int32, (HGRN_SB, HGRN_C), 0)
    scol = lax.broadcasted_iota(jnp.int32, (HGRN_SB, HGRN_C), 1)
    ng = ng_ref[...]
    q = q_ref[...].astype(F32)
    qb_all = (q * jnp.exp(bcum)).astype(BF16)

    tiles = [(n, h) for n in range(rows // HGRN_C) for h in range(B_HEADS)]

    def rs(n):
        return slice(n * HGRN_C, (n + 1) * HGRN_C)

    def hs(h):
        return slice(h * B_HEAD_DIM, (h + 1) * B_HEAD_DIM)

    a_parts = {}
    for n, h in tiles:
        bc, qc, kc = bcum[rs(n), hs(h)], q[rs(n), hs(h)], kk[rs(n), hs(h)]
        for sb in range(HGRN_C // HGRN_SB):
            s0 = sb * HGRN_SB
            beta = bc[s0 - 1:s0] if sb > 0 else jnp.zeros((1, B_HEAD_DIM), F32)
            qs = (qc[s0:s0 + HGRN_SB] * jnp.exp(bc[s0:s0 + HGRN_SB] - beta)).astype(BF16)
            expo = jnp.where(srow < s0 + HGRN_SB, beta - bc, -jnp.inf)
            ks = (kc * jnp.exp(expo)).astype(BF16)
            a_parts[n, h, sb] = lax.dot_general(qs, ks, NT_DIMS, preferred_element_type=F32)
    intra, upd, dec = {}, {}, {}
    for n, h in tiles:
        a_rows = [jnp.where(scol <= trow + sb * HGRN_SB, a_parts[n, h, sb], 0.0)
                  for sb in range(HGRN_C // HGRN_SB)]
        attn = jnp.concatenate(a_rows, axis=0).astype(BF16)
        vc = i_ref[rs(n), hs(h)]
        intra[n, h] = jnp.dot(attn, vc, preferred_element_type=F32)
        bc = bcum[rs(n), hs(h)]
        blast = bc[HGRN_C - 1:HGRN_C]
        kdec = (kk[rs(n), hs(h)] * jnp.exp(blast - bc)).astype(BF16)
        upd[n, h] = lax.dot_general(vc, kdec, TN_DIMS, preferred_element_type=F32)
        dec[n, h] = jnp.exp(blast)
    for n, h in tiles:
        st = st_ref[h]
        o = intra[n, h] + lax.dot_general(qb_all[rs(n), hs(h)], st.astype(BF16), NT_DIMS,
                                          preferred_element_type=F32)
        st_ref[h] = st * dec[n, h] + upd[n, h]
        ms = jnp.mean(o * o, axis=-1, keepdims=True)
        on = o * lax.rsqrt(ms + EPS) * ng
        o_ref[rs(n), hs(h)] = (on * _silu(g_ref[rs(n), hs(h)].astype(F32))).astype(o_ref.dtype)


def _hgrn2(p16, p32, lb_l, norm_g, bsz, seq):
    rows = min(HGRN_L, seq)
    nj = seq // rows
    n = bsz * seq
    r = jnp.arange(rows, dtype=jnp.int32)
    tril = ((r[:, None] >= r[None, :]) & (r[:, None] // HGRN_C == r[None, :] // HGRN_C)).astype(BF16)

    def col(base):
        return lambda b, j: (b * nj + j, base)

    return pl.pallas_call(
        functools.partial(_hgrn_kernel, rows=rows),
        grid=(bsz, nj),
        in_specs=[
            pl.BlockSpec((rows, B_WIDTH), col(P16_BQ)),
            pl.BlockSpec((rows, B_WIDTH), col(P32_BF // B_WIDTH)),
            pl.BlockSpec((rows, B_WIDTH), col(P16_BI)),
            pl.BlockSpec((rows, B_WIDTH), col(P16_BG)),
            pl.BlockSpec((1, B_WIDTH), lambda b, j: (0, 0)),
            pl.BlockSpec((1, B_HEAD_DIM), lambda b, j: (0, 0)),
            pl.BlockSpec((rows, rows), lambda b, j: (0, 0)),
        ],
        out_specs=pl.BlockSpec((rows, B_WIDTH), lambda b, j: (b * nj + j, 0)),
        out_shape=jax.ShapeDtypeStruct((n, B_WIDTH), BF16),
        scratch_shapes=[pltpu.VMEM((B_HEADS, B_HEAD_DIM, B_HEAD_DIM), F32)],
        compiler_params=_cparams(("arbitrary", "arbitrary")),
        name="hgrn2",
    )(p16, p32, p16, p16, lb_l.reshape(1, B_WIDTH), norm_g.reshape(1, B_HEAD_DIM), tril)


def _ret_kernel(q_ref, k_ref, v_ref, g_ref, cos_ref, sin_ref, idec_ref, qdec_ref, kdec_ref, cdec_ref,
                o_ref, st_ref):
    @pl.when(pl.program_id(1) == 0)
    def _():
        st_ref[...] = jnp.zeros_like(st_ref)

    heads = range(C_HEADS)
    cos = jnp.concatenate([cos_ref[...]] * C_HEADS, axis=1)
    sin_signed = jnp.concatenate([sin_ref[...]] * C_HEADS, axis=1)
    even = lax.broadcasted_iota(jnp.int32, cos.shape, 1) % 2 == 0

    def rot(a):
        swapped = jnp.where(even, pltpu.roll(a, C_QK_WIDTH - 1, 1), pltpu.roll(a, 1, 1))
        return a * cos + swapped * sin_signed

    qr = rot(q_ref[...].astype(F32))
    kr = rot(k_ref[...].astype(F32))
    qk = [slice(h * C_QK_DIM, (h + 1) * C_QK_DIM) for h in heads]
    vs = [slice(h * C_V_DIM, (h + 1) * C_V_DIM) for h in heads]
    attn = [lax.dot_general(qr[:, qk[h]].astype(BF16), kr[:, qk[h]].astype(BF16), NT_DIMS,
                            preferred_element_type=F32) * idec_ref[h] for h in heads]
    inter = [jnp.dot((qr[:, qk[h]] * qdec_ref[h]).astype(BF16), st_ref[h].astype(BF16),
                     preferred_element_type=F32) for h in heads]
    intra = [jnp.dot(attn[h].astype(BF16), v_ref[:, vs[h]], preferred_element_type=F32) for h in heads]
    upd = [jnp.dot((kr[:, qk[h]] * kdec_ref[h]).T.astype(BF16), v_ref[:, vs[h]],
                   preferred_element_type=F32) for h in heads]
    for h in heads:
        st_ref[h] = cdec_ref[h, 0:1, :] * st_ref[h] + upd[h]
        o = intra[h] + inter[h]
        ms = jnp.mean(o * o, axis=-1, keepdims=True)
        o_ref[:, vs[h]] = (_silu(g_ref[:, vs[h]].astype(F32)) * (o * lax.rsqrt(ms + EPS))).astype(o_ref.dtype)


def _retention_tables(seq):
    pos = jnp.arange(seq, dtype=F32)
    theta = jnp.repeat(1.0 / (10000.0 ** jnp.linspace(0.0, 1.0, C_QK_DIM // 2)), 2)
    ang = pos[:, None] * theta[None, :]
    pair_sign = jnp.where(jnp.arange(C_QK_DIM) % 2 == 0, -1.0, 1.0)
    log_gamma = jnp.log(1.0 - 2.0 ** (-5.0 - jnp.arange(C_HEADS, dtype=F32)))
    idx = jnp.arange(RET_C, dtype=F32)
    causal = idx[:, None] >= idx[None, :]
    idec = jnp.exp(jnp.where(causal[None], (idx[:, None] - idx[None, :])[None] * log_gamma[:, None, None],
                             -jnp.inf))
    qdec = jnp.exp((idx + 1.0)[None, :] * log_gamma[:, None])[..., None]
    kdec = jnp.exp((RET_C - 1.0 - idx)[None, :] * log_gamma[:, None])[..., None]
    cdec = jnp.exp(RET_C * log_gamma)[:, None, None]
    return (jnp.cos(ang), jnp.sin(ang) * pair_sign[None, :], idec,
            jnp.broadcast_to(qdec, (C_HEADS, RET_C, C_QK_DIM)),
            jnp.broadcast_to(kdec, (C_HEADS, RET_C, C_QK_DIM)),
            jnp.broadcast_to(cdec, (C_HEADS, 8, C_V_DIM)))


def _retention(p16, tables, bsz, seq):
    cos, sin, idec, qdec, kdec, cdec = tables
    nj = seq // RET_C
    n = bsz * seq
    v_blk = C_V_WIDTH // 1024

    def whole(a):
        return pl.BlockSpec(a.shape, lambda b, j: (0,) * a.ndim)

    return pl.pallas_call(
        _ret_kernel,
        grid=(bsz, nj),
        in_specs=[
            pl.BlockSpec((RET_C, C_QK_WIDTH), lambda b, j: (b * nj + j, P16_CQ)),
            pl.BlockSpec((RET_C, C_QK_WIDTH), lambda b, j: (b * nj + j, P16_CK)),
            pl.BlockSpec((RET_C, C_V_WIDTH), lambda b, j: (b * nj + j, P16_CV // v_blk)),
            pl.BlockSpec((RET_C, C_V_WIDTH), lambda b, j: (b * nj + j, P16_CG // v_blk)),
            pl.BlockSpec((RET_C, C_QK_DIM), lambda b, j: (j, 0)),
            pl.BlockSpec((RET_C, C_QK_DIM), lambda b, j: (j, 0)),
            whole(idec), whole(qdec), whole(kdec), whole(cdec),
        ],
        out_specs=pl.BlockSpec((RET_C, C_V_WIDTH), lambda b, j: (b * nj + j, 0)),
        out_shape=jax.ShapeDtypeStruct((n, C_V_WIDTH), BF16),
        scratch_shapes=[pltpu.VMEM((C_HEADS, C_QK_DIM, C_V_DIM), F32)],
        compiler_params=_cparams(("arbitrary", "arbitrary")),
        name="retention",
    )(p16, p16, p16, p16, cos, sin, idec, qdec, kdec, cdec)


def _merge_kernel(oa_ref, ob_ref, oc_ref, ga_ref, gb_ref, gc_ref, x_ref, mod_ref, g2_ref,
                  wa_ref, wb_ref, wc_ref, wo_ref, wrh_ref, wrl_ref, br_ref,
                  x1_ref, h2_ref, route_ref, *, tiles_per_batch):
    b = pl.program_id(0) // tiles_per_batch

    def gated(o_ref, w_ref, g_ref):
        y = jnp.dot(o_ref[...], w_ref[...], preferred_element_type=F32)
        return jax.nn.sigmoid(g_ref[...].astype(F32)) * y

    merged = gated(oa_ref, wa_ref, ga_ref) + gated(ob_ref, wb_ref, gb_ref) + gated(oc_ref, wc_ref, gc_ref)
    y = jnp.dot(merged.astype(BF16), wo_ref[...], preferred_element_type=F32)
    gt1 = mod_ref[pl.ds(b, 1), 2 * D_MODEL:3 * D_MODEL]
    x1 = x_ref[...] + gt1 * y
    x1_ref[...] = x1
    sh2 = mod_ref[pl.ds(b, 1), 3 * D_MODEL:4 * D_MODEL]
    sc2 = mod_ref[pl.ds(b, 1), 4 * D_MODEL:5 * D_MODEL]
    h2 = _rms_mod(x1, g2_ref[...], sc2, sh2)
    h_hi = h2.astype(BF16)
    h_f = h_hi.astype(F32)
    for j in range(h2_ref.shape[0]):
        h2_ref[j] = h_f[:, j * SC_SUB:(j + 1) * SC_SUB]
    h_lo = (h2 - h_hi.astype(F32)).astype(BF16)
    logits = (jnp.dot(h_hi, wrh_ref[...], preferred_element_type=F32)
              + jnp.dot(h_lo, wrh_ref[...], preferred_element_type=F32)
              + jnp.dot(h_hi, wrl_ref[...], preferred_element_type=F32)) + br_ref[...]
    lane = lax.broadcasted_iota(jnp.int32, logits.shape, 1).astype(F32)
    neg_inf = -jnp.inf

    def first_argmax(vals):
        top = jnp.max(vals, axis=-1, keepdims=True)
        idx = jnp.min(jnp.where(vals == top, lane, float(LANES)), axis=-1, keepdims=True)
        return top, idx

    gl = jnp.where(lane < N_GROUPS, logits, neg_inf)
    gmax, gsel = first_argmax(gl)
    gprob = 1.0 / jnp.sum(jnp.exp(gl - gmax), axis=-1, keepdims=True)
    lo = N_GROUPS + EXPERTS_PER_GROUP * gsel
    el = jnp.where((lane >= lo) & (lane < lo + EXPERTS_PER_GROUP), logits, neg_inf)
    v1, i1 = first_argmax(el)
    el2 = jnp.where(lane == i1, neg_inf, el)
    v2, i2 = first_argmax(el2)
    e2 = jnp.exp(v2 - v1)
    den = 1.0 + e2
    route_ref[...] = jnp.where(lane == 0.0, i1 - N_GROUPS,
                               jnp.where(lane == 1.0, i2 - N_GROUPS,
                                         jnp.where(lane == 2.0, gprob / den,
                                                   jnp.where(lane == 3.0, gprob * (e2 / den), 0.0))))


def _merge(o_a, o_b, o_c, p16, x2, mod_l, g2, wa, wb, wc, wo, wr_hi, wr_lo, br, seq):
    n = x2.shape[0]
    tm = min(512, seq)
    one = pl.Buffered(1)

    def rows(width, cb=0):
        return pl.BlockSpec((tm, width), lambda i: (i, cb))

    def whole(a):
        return pl.BlockSpec(a.shape, lambda i: (0,) * a.ndim, pipeline_mode=one)

    return pl.pallas_call(
        functools.partial(_merge_kernel, tiles_per_batch=seq // tm),
        grid=(n // tm,),
        in_specs=[rows(A_WIDTH), rows(B_WIDTH), rows(C_V_WIDTH),
                  rows(D_MODEL, P16_GA), rows(D_MODEL, P16_GB), rows(D_MODEL, P16_GC),
                  rows(D_MODEL), whole(mod_l), pl.BlockSpec((1, D_MODEL), lambda i: (0, 0)),
                  whole(wa), whole(wb), whole(wc), whole(wo), whole(wr_hi), whole(wr_lo), whole(br)],
        out_specs=[rows(D_MODEL), pl.BlockSpec((SC_PIECES, tm, SC_SUB), lambda i: (0, i, 0)), rows(LANES)],
        out_shape=[jax.ShapeDtypeStruct((n, D_MODEL), F32),
                   jax.ShapeDtypeStruct((SC_PIECES, n, SC_SUB), F32),
                   jax.ShapeDtypeStruct((n, LANES), F32)],
        compiler_params=_cparams(("arbitrary",)),
        name="merge_route",
    )(o_a, o_b, o_c, p16, p16, p16, x2, mod_l, g2.reshape(1, D_MODEL), wa, wb, wc, wo, wr_hi, wr_lo, br)


def _lane_pick(vals, lane, idx):
    return jnp.sum(jnp.where(lane == idx, vals, 0.0), axis=-1, keepdims=True)


def _rank_kernel(route_ref, rk_ref, cnt_ref, run_ref):
    @pl.when(pl.program_id(0) == 0)
    def _():
        run_ref[...] = jnp.zeros_like(run_ref)

    route = route_ref[...]
    tb = route.shape[0]
    lane = lax.broadcasted_iota(jnp.int32, route.shape, 1).astype(F32)
    e1 = _lane_pick(route, lane, 0.0)
    e2 = _lane_pick(route, lane, 1.0)
    sel = jnp.where((lane == e1) | (lane == e2), 1.0, 0.0)
    r = lax.broadcasted_iota(jnp.int32, (tb, tb), 0)
    c = lax.broadcasted_iota(jnp.int32, (tb, tb), 1)
    before = jnp.where(c < r, 1.0, 0.0).astype(BF16)
    rank = jnp.dot(before, sel.astype(BF16), preferred_element_type=F32) + run_ref[0:1, :]
    rk_ref[...] = jnp.where(lane == 0.0, _lane_pick(rank, lane, e1),
                            jnp.where(lane == 1.0, _lane_pick(rank, lane, e2), 0.0))
    run_ref[...] = run_ref[...] + jnp.sum(sel, axis=0, keepdims=True)
    cnt_ref[...] = run_ref[...]


def _expert_ranks(route):
    n = route.shape[0]
    tb = min(256, n)
    return pl.pallas_call(
        _rank_kernel,
        grid=(n // tb,),
        in_specs=[pl.BlockSpec((tb, LANES), lambda i: (i, 0))],
        out_specs=[pl.BlockSpec((tb, LANES), lambda i: (i, 0)),
                   pl.BlockSpec((8, LANES), lambda i: (0, 0))],
        out_shape=[jax.ShapeDtypeStruct((n, LANES), F32), jax.ShapeDtypeStruct((8, LANES), F32)],
        scratch_shapes=[pltpu.VMEM((8, LANES), F32)],
        compiler_params=_cparams(("arbitrary",)),
        name="expert_ranks",
    )(route)


def _plan_kernel(cnt_ref, route_ref, rk_ref, pos_ref, tmap_ref):
    lane_i = lax.broadcasted_iota(jnp.int32, (8, LANES), 1)
    cnt = jnp.where(lane_i < N_EXPERTS, cnt_ref[...], 0.0)
    padded = jnp.floor((cnt + (MOE_TM - 1)) * (1.0 / MOE_TM)) * MOE_TM
    r = lax.broadcasted_iota(jnp.int32, (LANES, LANES), 0)
    c = lax.broadcasted_iota(jnp.int32, (LANES, LANES), 1)
    base = jnp.dot(padded, jnp.where(r < c, 1.0, 0.0), precision=lax.Precision.HIGHEST,
                   preferred_element_type=F32)

    route = route_ref[...]
    lane = lax.broadcasted_iota(jnp.int32, route.shape, 1).astype(F32)
    rk = rk_ref[...]
    base_row = base[0:1, :]
    pos1 = _lane_pick(base_row, lane, _lane_pick(route, lane, 0.0)) + _lane_pick(rk, lane, 0.0)
    pos2 = _lane_pick(base_row, lane, _lane_pick(route, lane, 1.0)) + _lane_pick(rk, lane, 1.0)
    pos_ref[...] = jnp.where(lane == 0.0, pos1, jnp.where(lane == 1.0, pos2, 0.0)).astype(jnp.int32)

    @pl.when(pl.program_id(0) == 0)
    def _():
        nt = tmap_ref.shape[0]
        tlane = lax.broadcasted_iota(jnp.int32, (nt, LANES), 1)
        start = (lax.broadcasted_iota(jnp.int32, (nt, LANES), 0) * MOE_TM).astype(F32)
        end_row = jnp.where(tlane < N_EXPERTS, base_row + padded[0:1, :], 3e38)
        expert = jnp.sum(jnp.where(end_row <= start, 1.0, 0.0), axis=-1, keepdims=True)
        expert_c = jnp.minimum(expert, N_EXPERTS - 1.0)
        tl = tlane.astype(F32)
        left = _lane_pick(cnt[0:1, :], tl, expert_c) - (start[:, 0:1] - _lane_pick(base_row, tl, expert_c))
        valid = jnp.where(expert < N_EXPERTS, jnp.clip(left, 0.0, float(MOE_TM)), 0.0)
        tmap_ref[...] = jnp.where(tlane == 0, expert_c, jnp.where(tlane == 1, valid, 0.0)).astype(jnp.int32)


def _expert_plan(cnt, route, rk, n_tiles):
    n = route.shape[0]
    tb = min(256, n)
    nt_pad = -(-n_tiles // 8) * 8
    return pl.pallas_call(
        _plan_kernel,
        grid=(n // tb,),
        in_specs=[pl.BlockSpec((8, LANES), lambda i: (0, 0)),
                  pl.BlockSpec((tb, LANES), lambda i: (i, 0)),
                  pl.BlockSpec((tb, LANES), lambda i: (i, 0))],
        out_specs=[pl.BlockSpec((tb, LANES), lambda i: (i, 0)),
                   pl.BlockSpec((nt_pad, LANES), lambda i: (0, 0))],
        out_shape=[jax.ShapeDtypeStruct((n, LANES), jnp.int32),
                   jax.ShapeDtypeStruct((nt_pad, LANES), jnp.int32)],
        compiler_params=_cparams(("arbitrary",)),
        name="expert_plan",
    )(cnt, route, rk)


def _sc_mesh():
    return plsc.VectorSubcoreMesh(core_axis_name="c", subcore_axis_name="s")


def _sc_scatter_rows(src, idx, out_rows):
    m = idx.shape[0]
    n_src_win = src.shape[0] // SC_WINDOW

    @functools.partial(pl.kernel, out_type=jax.ShapeDtypeStruct((out_rows, src.shape[1]), src.dtype),
                       mesh=_sc_mesh(), scratch_types=[])
    def scatter(x_hbm, i_hbm, o_hbm):
        def body(x_vmem, i_vmem):
            pltpu.sync_copy(x_vmem, o_hbm.at[i_vmem.at[0]])

        pltpu.emit_pipeline(
            body, grid=(m // SC_WINDOW,),
            in_specs=[pl.BlockSpec((SC_WINDOW, src.shape[1]), lambda i: (i % n_src_win, 0)),
                      pl.BlockSpec((1, SC_WINDOW), lambda i: (0, i))],
            out_specs=[], core_axis_name=("c", "s"),
            dimension_semantics=(pltpu.PARALLEL,))(x_hbm, i_hbm)

    return scatter(src, idx.reshape(1, m))


def _sc_gather_rows(table, idx):
    m = idx.shape[0]

    @functools.partial(pl.kernel, out_type=jax.ShapeDtypeStruct((m, table.shape[1]), table.dtype),
                       mesh=_sc_mesh(), scratch_types=[])
    def gather(x_hbm, i_hbm, o_hbm):
        def body(i_vmem, o_vmem):
            pltpu.sync_copy(x_hbm.at[i_vmem.at[0]], o_vmem)

        pltpu.emit_pipeline(
            body, grid=(m // SC_WINDOW,),
            in_specs=[pl.BlockSpec((1, SC_WINDOW), lambda i: (0, i))],
            out_specs=[pl.BlockSpec((SC_WINDOW, table.shape[1]), lambda i: (i, 0))],
            core_axis_name=("c", "s"),
            dimension_semantics=(pltpu.PARALLEL,))(i_hbm, o_hbm)

    return gather(table, idx.reshape(1, m))


def _piece_row_index(pos, rows):
    return (jnp.arange(SC_PIECES, dtype=jnp.int32)[:, None] * rows + pos[None, :]).reshape(-1)


def _grouped_kernel(te_ref, tv_ref, x_ref, wg_ref, wu_ref, wd_ref, o_ref):
    valid = tv_ref[pl.program_id(0)]

    @pl.when(valid > 0)
    def _():
        x = jnp.concatenate([x_ref[j] for j in range(SC_PIECES)], axis=1)
        row = lax.broadcasted_iota(jnp.int32, x.shape, 0)
        x = jnp.where(row < valid, x, 0.0).astype(BF16)
        a = jnp.dot(x, wg_ref[0].astype(BF16), preferred_element_type=F32)
        u = jnp.dot(x, wu_ref[0].astype(BF16), preferred_element_type=F32)
        hm = (_silu(a) * u).astype(BF16)
        o = jnp.dot(hm, wd_ref[0].astype(BF16), preferred_element_type=F32)
        for j in range(SC_PIECES):
            o_ref[j] = o[:, j * SC_SUB:(j + 1) * SC_SUB]

    @pl.when(valid <= 0)
    def _():
        o_ref[...] = jnp.zeros_like(o_ref)


def _grouped_experts(tile_expert, tile_valid, xs, wg, wu, wd):
    n_tiles = tile_expert.shape[0]
    rows_block = pl.BlockSpec((SC_PIECES, MOE_TM, SC_SUB), lambda i, te, tv: (0, i, 0))
    return pl.pallas_call(
        _grouped_kernel,
        grid_spec=pltpu.PrefetchScalarGridSpec(
            num_scalar_prefetch=2,
            grid=(n_tiles,),
            in_specs=[rows_block,
                      pl.BlockSpec((1, D_MODEL, D_EXPERT), lambda i, te, tv: (te[i], 0, 0)),
                      pl.BlockSpec((1, D_MODEL, D_EXPERT), lambda i, te, tv: (te[i], 0, 0)),
                      pl.BlockSpec((1, D_EXPERT, D_MODEL), lambda i, te, tv: (te[i], 0, 0))],
            out_specs=rows_block),
        out_shape=jax.ShapeDtypeStruct(xs.shape, F32),
        compiler_params=_cparams(("arbitrary",)),
        name="grouped_experts",
    )(tile_expert, tile_valid, xs, wg, wu, wd)


def _combine_kernel(x1_ref, y_ref, route_ref, mod_ref, o_ref, *, tiles_per_batch):
    b = pl.program_id(0) // tiles_per_batch
    gt2 = mod_ref[pl.ds(b, 1), 5 * D_MODEL:6 * D_MODEL]
    route = route_ref[...]
    lane = lax.broadcasted_iota(jnp.int32, route.shape, 1).astype(F32)
    w1 = _lane_pick(route, lane, 2.0)
    w2 = _lane_pick(route, lane, 3.0)
    y = jnp.concatenate([w1 * y_ref[0, j] + w2 * y_ref[1, j] for j in range(SC_PIECES)], axis=1)
    o_ref[...] = x1_ref[...] + gt2 * y


def _combine(x1, y2, route, mod_l, seq):
    n = x1.shape[0]
    tm = min(1024, seq)
    return pl.pallas_call(
        functools.partial(_combine_kernel, tiles_per_batch=seq // tm),
        grid=(n // tm,),
        in_specs=[pl.BlockSpec((tm, D_MODEL), lambda i: (i, 0)),
                  pl.BlockSpec((2, SC_PIECES, tm, SC_SUB), lambda i: (0, 0, i, 0)),
                  pl.BlockSpec((tm, LANES), lambda i: (i, 0)),
                  pl.BlockSpec(mod_l.shape, lambda i: (0, 0))],
        out_specs=pl.BlockSpec((tm, D_MODEL), lambda i: (i, 0)),
        out_shape=jax.ShapeDtypeStruct((n, D_MODEL), F32),
        compiler_params=_cparams(("arbitrary",)),
        name="moe_combine",
    )(x1, y2, route, mod_l)


def _moe(h2, route, x1, mod_l, wg, wu, wd, seq):
    n = h2.shape[1]
    n_tiles = (2 * n) // MOE_TM + N_EXPERTS
    rows = n_tiles * MOE_TM
    rk, cnt = _expert_ranks(route)
    pos, tmap = _expert_plan(cnt, route, rk, n_tiles)
    idx = jnp.concatenate([_piece_row_index(pos[:, 0], rows), _piece_row_index(pos[:, 1], rows)])
    xs = _sc_scatter_rows(h2.reshape(SC_PIECES * n, SC_SUB), idx, SC_PIECES * rows)
    ys = _grouped_experts(tmap[:n_tiles, 0], tmap[:n_tiles, 1], xs.reshape(SC_PIECES, rows, SC_SUB),
                          wg, wu, wd)
    y2 = _sc_gather_rows(ys.reshape(SC_PIECES * rows, SC_SUB), idx).reshape(2, SC_PIECES, n, SC_SUB)
    return _combine(x1, y2, route, mod_l, seq)


def _final_norm_kernel(x_ref, g_ref, o_ref):
    x = x_ref[...]
    ms = jnp.mean(x * x, axis=-1, keepdims=True)
    o_ref[...] = x * lax.rsqrt(ms + EPS) * g_ref[...]


def _final_norm(x2, g, seq):
    n = x2.shape[0]
    tm = min(1024, seq)
    return pl.pallas_call(
        _final_norm_kernel,
        grid=(n // tm,),
        in_specs=[pl.BlockSpec((tm, D_MODEL), lambda i: (i, 0)),
                  pl.BlockSpec((1, D_MODEL), lambda i: (0, 0))],
        out_specs=pl.BlockSpec((tm, D_MODEL), lambda i: (i, 0)),
        out_shape=jax.ShapeDtypeStruct((n, D_MODEL), F32),
        compiler_params=_cparams(("arbitrary",)),
        name="final_norm",
    )(x2, g.reshape(1, D_MODEL))


def _pack_w_in(w_in_l):
    offs = [0]
    for s in IN_SPLITS:
        offs.append(offs[-1] + s)
    (aq, ak, av, iq, ik, iw, bq, bf, bi, bg, cq, ck, cv, cg, ga, gb, gc) = [
        w_in_l[:, offs[i]:offs[i + 1]] for i in range(len(IN_SPLITS))]

    w16 = jnp.concatenate(
        [cv, cg, aq * (A_HEAD_DIM ** -0.5 * LOG2_E), ak, bq, bi, bg, cq, ck * (C_QK_DIM ** -0.5), ga, gb, gc],
        axis=1).astype(BF16)
    iq_p = jnp.pad(iq.reshape(D_MODEL, IDX_HEADS, IDX_DIM),
                   ((0, 0), (0, 0), (0, LANES - IDX_DIM))).reshape(D_MODEL, IDX_HEADS * LANES)
    w32 = jnp.concatenate(
        [bf, iq_p, jnp.pad(ik, ((0, 0), (0, LANES - IDX_DIM))),
         jnp.pad(iw, ((0, 0), (0, LANES - IDX_HEADS)))], axis=1).astype(BF16)
    return w16, w32, av.T.astype(BF16)


def _split_bf16(w):
    hi = w.astype(BF16)
    return hi, (w - hi.astype(F32)).astype(BF16)


def kernel(x, c, rel_bias, hgrn_lb_raw, norm1_g, norm2_g, ada_w, ada_b, w_in, hgrn_norm_g, w_branch_a,
           w_branch_b, w_branch_c, w_out, router_group_w, router_group_b, router_expert_w,
           router_expert_b, expert_w_gate, expert_w_up, expert_w_down, final_norm_g):
    bsz, seq, _ = x.shape
    depth = w_in.shape[0]
    n = bsz * seq
    x2 = x.reshape(n, D_MODEL)
    tq = min(DSA_TQ, seq)

    lb_all = _hgrn_lower_bounds(hgrn_lb_raw)
    c_pad = jnp.pad(c, ((0, (-bsz) % 8), (0, 0)))
    mod = _ada_mod(c_pad, ada_w, ada_b)
    bias_tiles = _bias_tiles(rel_bias, tq)
    ret_tables = _retention_tables(seq)

    for l in range(depth):
        w16, w32, wvt = _pack_w_in(w_in[l])
        p16 = _norm_project(x2, mod[l], norm1_g[l], w16, BF16, 1024, seq, "proj_bf16")
        p32 = _norm_project(x2, mod[l], norm1_g[l], w32, F32, 768, seq, "proj_f32")
        vt = _norm_project_t(x2, mod[l], norm1_g[l], wvt, tq, seq, "proj_vt")
        o_a = _dsa_attention(p16, p32, vt, bias_tiles, bsz, seq)
        o_b = _hgrn2(p16, p32, lb_all[l], hgrn_norm_g[l], bsz, seq)
        o_c = _retention(p16, ret_tables, bsz, seq)
        wr = jnp.concatenate([router_group_w[l], router_expert_w[l],
                              jnp.zeros((D_MODEL, LANES - N_GROUPS - N_EXPERTS), F32)], axis=1)
        br = jnp.concatenate([router_group_b[l], router_expert_b[l],
                              jnp.zeros((LANES - N_GROUPS - N_EXPERTS,), F32)]).reshape(1, LANES)
        wr_hi, wr_lo = _split_bf16(wr)
        x1, h2, route = _merge(o_a, o_b, o_c, p16, x2, mod[l], norm2_g[l],
                               w_branch_a[l].astype(BF16), w_branch_b[l].astype(BF16),
                               w_branch_c[l].astype(BF16), w_out[l].astype(BF16),
                               wr_hi, wr_lo, br, seq)
        x2 = _moe(h2, route, x1, mod[l], expert_w_gate[l], expert_w_up[l], expert_w_down[l], seq)

    return _final_norm(x2, final_norm_g, seq).reshape(bsz, seq, D_MODEL)
```

```python
import functools
import math

import jax
import jax.numpy as jnp
from jax import lax
from jax.experimental import pallas as pl
from jax.experimental.pallas import tpu as pltpu
from jax.experimental.pallas import tpu_sc as plsc

F32 = jnp.float32
BF16 = jnp.bfloat16

D_MODEL = 1024
A_HEADS = 8
A_HEAD_DIM = 128
IDX_HEADS = 8
IDX_DIM = 64
TOPK_MAX = 256
REL_BUCKETS = 32
REL_MAX_DIST = 128
B_HEADS = 8
B_HEAD_DIM = 128
C_HEADS = 4
C_QK_DIM = 256
C_V_DIM = 512
N_GROUPS = 4
EXPERTS_PER_GROUP = 8
N_EXPERTS = 32
D_EXPERT = 512
EPS = 1e-6

A_WIDTH = A_HEADS * A_HEAD_DIM
B_WIDTH = B_HEADS * B_HEAD_DIM
C_QK_WIDTH = C_HEADS * C_QK_DIM
C_V_WIDTH = C_HEADS * C_V_DIM
IN_SPLITS = (A_WIDTH, A_WIDTH, A_WIDTH, IDX_HEADS * IDX_DIM, IDX_DIM, IDX_HEADS,
             B_WIDTH, B_WIDTH, B_WIDTH, B_WIDTH,
             C_QK_WIDTH, C_QK_WIDTH, C_V_WIDTH, C_V_WIDTH,
             D_MODEL, D_MODEL, D_MODEL)

LANES = 128
BF16_ROWS = 16
VMEM_LIMIT = 56 * 1024 * 1024

P16_CV, P16_CG = 0, 2
P16_AQ, P16_AK, P16_BQ, P16_BI, P16_BG, P16_CQ, P16_CK, P16_GA, P16_GB, P16_GC = range(4, 14)
P32_BF = 0
P32_IQ = 1024
P32_IK = 2048
P32_IW = 2176

DSA_TQ = 256
HGRN_L = 256
HGRN_C = 64
HGRN_SB = 16
RET_C = 128
KEY_NEG_INF = -2139095041
HALF_BIAS = 32768
MASK_NEG = -1e30
LOG2_E = math.log2(math.e)
COUNT_CHAINS = 4
MOE_TM = 256
SC_WINDOW = 128
SC_SUB = 256
SC_PIECES = D_MODEL // SC_SUB

NT_DIMS = (((1,), (1,)), ((), ()))
TN_DIMS = (((0,), (0,)), ((), ()))


def _cparams(sem):
    return pltpu.CompilerParams(dimension_semantics=sem, vmem_limit_bytes=VMEM_LIMIT)


def _silu(x):
    return x * jax.nn.sigmoid(x)


def _lb_kernel(raw_ref, o_ref):
    raw = raw_ref[...]
    m = jnp.max(raw, axis=0, keepdims=True)
    e = jnp.exp(raw - m)
    soft = e / jnp.sum(e, axis=0, keepdims=True)
    run = jnp.zeros_like(soft[0:1])
    for l in range(raw.shape[0]):
        run = run + soft[l:l + 1]
        o_ref[l:l + 1, :] = run - soft[0:1]


def _hgrn_lower_bounds(raw):
    return pl.pallas_call(
        _lb_kernel, out_shape=jax.ShapeDtypeStruct(raw.shape, F32), name="hgrn_lb")(raw)


def _ada_kernel(c_ref, w_ref, b_ref, o_ref):
    a = _silu(c_ref[...])
    o_ref[0] = jnp.dot(a, w_ref[0], precision=lax.Precision.HIGHEST,
                       preferred_element_type=F32) + b_ref[0]


def _ada_mod(c_pad, ada_w, ada_b):
    depth = ada_w.shape[0]
    rows = c_pad.shape[0]
    return pl.pallas_call(
        _ada_kernel,
        grid=(depth, 6),
        in_specs=[pl.BlockSpec((rows, D_MODEL), lambda l, j: (0, 0)),
                  pl.BlockSpec((1, D_MODEL, D_MODEL), lambda l, j: (l, 0, j)),
                  pl.BlockSpec((1, 1, D_MODEL), lambda l, j: (l, 0, j))],
        out_specs=pl.BlockSpec((1, rows, D_MODEL), lambda l, j: (l, 0, j)),
        out_shape=jax.ShapeDtypeStruct((depth, rows, 6 * D_MODEL), F32),
        compiler_params=_cparams(("arbitrary", "arbitrary")),
        name="ada_mod",
    )(c_pad, ada_w, ada_b.reshape(depth, 1, 6 * D_MODEL))


def _rms_mod(x, g, sc, sh):
    ms = jnp.mean(x * x, axis=-1, keepdims=True)
    return (x * lax.rsqrt(ms + EPS) * g) * (1.0 + sc) + sh


def _norm1(x_ref, mod_ref, g_ref, b):
    sh = mod_ref[pl.ds(b, 1), 0:D_MODEL]
    sc = mod_ref[pl.ds(b, 1), D_MODEL:2 * D_MODEL]
    return _rms_mod(x_ref[...], g_ref[...], sc, sh).astype(BF16)


def _proj_kernel(x_ref, mod_ref, g_ref, w_ref, o_ref, h_ref, *, tiles_per_batch):
    @pl.when(pl.program_id(1) == 0)
    def _():
        h_ref[...] = _norm1(x_ref, mod_ref, g_ref, pl.program_id(0) // tiles_per_batch)

    o_ref[...] = jnp.dot(h_ref[...], w_ref[...], preferred_element_type=F32).astype(o_ref.dtype)


def _norm_project(x2, mod_l, g, w_all, layer, out_dtype, tn, seq, name):
    n = x2.shape[0]
    width = w_all.shape[2]
    tm = min(1024, seq)
    return pl.pallas_call(
        functools.partial(_proj_kernel, tiles_per_batch=seq // tm),
        grid=(n // tm, width // tn),
        in_specs=[pl.BlockSpec((tm, D_MODEL), lambda i, j: (i, 0)),
                  pl.BlockSpec(mod_l.shape, lambda i, j: (0, 0)),
                  pl.BlockSpec((1, D_MODEL), lambda i, j: (0, 0)),
                  pl.BlockSpec((None, D_MODEL, tn), lambda i, j: (layer, 0, j))],
        out_specs=pl.BlockSpec((tm, tn), lambda i, j: (i, j)),
        out_shape=jax.ShapeDtypeStruct((n, width), out_dtype),
        scratch_shapes=[pltpu.VMEM((tm, D_MODEL), BF16)],
        compiler_params=_cparams(("arbitrary", "arbitrary")),
        name=name,
    )(x2, mod_l, g.reshape(1, D_MODEL), w_all)


def _proj_t_kernel(x_ref, mod_ref, g_ref, wt_ref, o_ref, *, tiles_per_batch, chunk):
    h = _norm1(x_ref, mod_ref, g_ref, pl.program_id(0) // tiles_per_batch)
    res = lax.dot_general(wt_ref[...], h, NT_DIMS, preferred_element_type=F32)
    for ci in range(o_ref.shape[0]):
        o_ref[ci] = res[:, ci * chunk:(ci + 1) * chunk].astype(o_ref.dtype)


def _norm_project_t(x2, mod_l, g, wt_all, layer, chunk, seq, name):
    n = x2.shape[0]
    cols = wt_all.shape[1]
    tm = min(1024, seq)
    return pl.pallas_call(
        functools.partial(_proj_t_kernel, tiles_per_batch=seq // tm, chunk=chunk),
        grid=(n // tm,),
        in_specs=[pl.BlockSpec((tm, D_MODEL), lambda i: (i, 0)),
                  pl.BlockSpec(mod_l.shape, lambda i: (0, 0)),
                  pl.BlockSpec((1, D_MODEL), lambda i: (0, 0)),
                  pl.BlockSpec((None, cols, D_MODEL), lambda i: (layer, 0, 0))],
        out_specs=pl.BlockSpec((tm // chunk, cols, chunk), lambda i: (i, 0, 0)),
        out_shape=jax.ShapeDtypeStruct((n // chunk, cols, chunk), BF16),
        compiler_params=_cparams(("arbitrary",)),
        name=name,
    )(x2, mod_l, g.reshape(1, D_MODEL), wt_all)


def _dsa_kernel(q_ref, iq_ref, iw_ref, k_ref, vt_ref, ik_ref, bias_ref, o_ref,
                key_ref, hi_ref, lo_ref, madd_ref, qt_ref, iqt_ref, iwt_ref, m_ref, l_ref, acc_ref, s_ref,
                *, tq, topk):
    qi = pl.program_id(1)
    nck = qi + 1
    idx_scale = (IDX_HEADS * IDX_DIM) ** -0.5

    for h in range(A_HEADS):
        hs = slice(h * LANES, (h + 1) * LANES)
        qt_ref[hs, :] = q_ref[:, hs].astype(F32).T.astype(BF16)
        iqt_ref[hs, :] = iq_ref[:, hs].T.astype(BF16)
    iwt_ref[...] = (iw_ref[...] * idx_scale).T

    krow = lax.broadcasted_iota(jnp.int32, (tq, tq), 0)
    qcol = lax.broadcasted_iota(jnp.int32, (tq, tq), 1)

    def score_chunk(c, carry):
        off = pl.multiple_of(c * tq, tq)
        ikc = ik_ref[pl.ds(off, tq), :].astype(BF16)
        acc = jnp.zeros((tq, tq), F32)
        for h in range(IDX_HEADS):
            s = jnp.dot(ikc, iqt_ref[h * LANES:(h + 1) * LANES, :], preferred_element_type=F32)
            acc = acc + jnp.maximum(s, 0.0) * iwt_ref[h:h + 1, :]
        acc = jnp.where(acc == 0.0, 0.0, acc)
        acc = jnp.where(krow + (c - qi) * tq <= qcol, acc, -jnp.inf)
        kb = pltpu.bitcast(acc, jnp.int32)
        key = jnp.where(kb < 0, kb ^ jnp.int32(0x7FFFFFFF), kb)
        key_ref[c] = key
        hi_ref[c] = jnp.right_shift(key, 16).astype(jnp.int16)
        lo_ref[c] = ((key & 0xFFFF) - HALF_BIAS).astype(jnp.int16)
        return carry

    def paired_loop(count_, fn):
        def pair(i, carry):
            fn(2 * i, carry)
            fn(2 * i + 1, carry)
            return carry

        lax.fori_loop(0, count_ // 2, pair, 0)

        @pl.when(count_ % 2 == 1)
        def _():
            fn(count_ - 1, 0)

    paired_loop(nck, score_chunk)

    def count(ref, pred_fn, rows, zero, one):
        def body(c, parts):
            hit = jnp.where(pred_fn(ref[c]), one, zero)
            parts = list(parts)
            for r in range(tq // rows):
                parts[r % COUNT_CHAINS] = parts[r % COUNT_CHAINS] + hit[r * rows:(r + 1) * rows, :]
            return tuple(parts)

        parts = lax.fori_loop(0, nck, body, (jnp.full((rows, tq), zero),) * COUNT_CHAINS)
        return jnp.sum(sum(p.astype(F32) for p in parts), axis=0, keepdims=True)

    def count16(ref, pred_fn):
        return count(ref, pred_fn, 16, jnp.int16(0), jnp.int16(1))

    def count32(pred_fn):
        return count(key_ref, pred_fn, 8, jnp.float32(0.0), jnp.float32(1.0))

    def bisect16(ref, target):
        def bit_step(i, theta):
            cand = theta + jnp.left_shift(jnp.int32(1), 15 - i)
            cand16 = cand.astype(jnp.int16)
            return jnp.where(count16(ref, lambda k: k >= cand16) >= target, cand, theta)

        return lax.fori_loop(0, 16, bit_step, jnp.full((1, tq), -HALF_BIAS, jnp.int32))

    theta_hi = bisect16(hi_ref, float(topk))
    theta_hi16 = theta_hi.astype(jnp.int16)
    need_lo = topk - count16(hi_ref, lambda k: k > theta_hi16)

    def bucket_chunk(c, carry):
        lo_ref[c] = jnp.where(hi_ref[c] == theta_hi16, lo_ref[c], jnp.int16(-HALF_BIAS))
        return carry

    lax.fori_loop(0, nck, bucket_chunk, 0)
    theta_lo = bisect16(lo_ref, need_lo)
    theta = theta_hi * (2 * HALF_BIAS) + (theta_lo + HALF_BIAS)
    theta = jnp.maximum(theta, KEY_NEG_INF + 1)

    def mask_chunk(c, cnt):
        ge = key_ref[c] >= theta
        madd_ref[c] = jnp.where(ge, 0.0, MASK_NEG)
        return cnt + jnp.sum(jnp.where(ge, 1.0, 0.0), axis=0, keepdims=True)

    cnt_ge = lax.fori_loop(0, nck, mask_chunk, jnp.zeros((1, tq), F32))

    @pl.when(jnp.max(cnt_ge) > topk)
    def _():
        need_eq = topk - count32(lambda kc: kc > theta)
        incl = jnp.where(krow >= qcol, 1.0, 0.0).astype(BF16)

        def tie_chunk(c, run):
            kc = key_ref[c]
            eq = kc == theta
            eqf = jnp.where(eq, 1.0, 0.0)
            pref = jnp.dot(incl, eqf.astype(BF16), preferred_element_type=F32) + run
            eq_add = jnp.where(pref <= need_eq, 0.0, MASK_NEG)
            madd_ref[c] = jnp.where(eq, eq_add, jnp.where(kc > theta, 0.0, MASK_NEG))
            return run + jnp.sum(eqf, axis=0, keepdims=True)

        lax.fori_loop(0, nck, tie_chunk, jnp.zeros((1, tq), F32))

    m_ref[...] = jnp.full(m_ref.shape, -jnp.inf, F32)
    l_ref[...] = jnp.zeros(l_ref.shape, F32)
    acc_ref[...] = jnp.zeros(acc_ref.shape, F32)

    ones_rows = jnp.ones((BF16_ROWS, tq), BF16)

    head_slices = [slice(h * A_HEAD_DIM, (h + 1) * A_HEAD_DIM) for h in range(A_HEADS)]

    def logits(c, h):
        off = pl.multiple_of(c * tq, tq)
        s_ref[h] = jnp.dot(k_ref[pl.ds(off, tq), head_slices[h]], qt_ref[head_slices[h], :],
                           preferred_element_type=F32)

    def attend(c, h, lag):
        hs = head_slices[h]
        s = s_ref[h] + madd_ref[c]
        if lag is not None:
            s = s + bias_ref[h, lag]
        m_old = m_ref[h]
        m_new = jnp.maximum(m_old, jnp.max(s, axis=0, keepdims=True))
        alpha = jnp.exp2(m_old - m_new)
        p = jnp.exp2(s - m_new).astype(BF16)
        pv = jnp.dot(jnp.concatenate([vt_ref[c, hs, :], ones_rows], axis=0), p,
                     preferred_element_type=F32)
        l_ref[h] = alpha * l_ref[h] + pv[A_HEAD_DIM:A_HEAD_DIM + 1]
        acc_ref[h] = alpha * acc_ref[h] + pv[:A_HEAD_DIM]
        m_ref[h] = m_new

    def step(c, lag, prefetch):
        for h in range(A_HEADS):
            attend(c, h, lag)
            if prefetch:
                logits(c + 1, h)

    for h in range(A_HEADS):
        logits(0, h)

    def far_chunk(c, carry):
        step(c, None, True)
        return carry

    paired_loop(jnp.maximum(qi - 1, 0), far_chunk)

    @pl.when(qi >= 1)
    def _():
        step(qi - 1, 1, True)

    step(qi, 0, False)

    for h in range(A_HEADS):
        o = acc_ref[h] * (1.0 / l_ref[h])
        o_ref[:, h * A_HEAD_DIM:(h + 1) * A_HEAD_DIM] = o.T.astype(o_ref.dtype)


def _dsa_attention(p16, p32, vt, bias_tiles, bsz, seq):
    tq = min(DSA_TQ, seq)
    nq = seq // tq
    topk = min(TOPK_MAX, seq // 4)
    n = bsz * seq
    one = pl.Buffered(1)
    return pl.pallas_call(
        functools.partial(_dsa_kernel, tq=tq, topk=topk),
        grid=(bsz, nq),
        in_specs=[
            pl.BlockSpec((tq, A_WIDTH), lambda b, i: (b * nq + i, P16_AQ)),
            pl.BlockSpec((tq, 1024), lambda b, i: (b * nq + i, P32_IQ // 1024)),
            pl.BlockSpec((tq, LANES), lambda b, i: (b * nq + i, P32_IW // LANES)),
            pl.BlockSpec((seq, A_WIDTH), lambda b, i: (b, P16_AK), pipeline_mode=one),
            pl.BlockSpec((nq, A_WIDTH, tq), lambda b, i: (b, 0, 0), pipeline_mode=one),
            pl.BlockSpec((seq, LANES), lambda b, i: (b, P32_IK // LANES), pipeline_mode=one),
            pl.BlockSpec(bias_tiles.shape, lambda b, i: (0, 0, 0, 0), pipeline_mode=one),
        ],
        out_specs=pl.BlockSpec((tq, A_WIDTH), lambda b, i: (b * nq + i, 0)),
        out_shape=jax.ShapeDtypeStruct((n, A_WIDTH), BF16),
        scratch_shapes=[pltpu.VMEM((nq, tq, tq), jnp.int32),
                        pltpu.VMEM((nq, tq, tq), jnp.int16),
                        pltpu.VMEM((nq, tq, tq), jnp.int16),
                        pltpu.VMEM((nq, tq, tq), F32),
                        pltpu.VMEM((A_WIDTH, tq), BF16),
                        pltpu.VMEM((IDX_HEADS * LANES, tq), BF16),
                        pltpu.VMEM((LANES, tq), F32),
                        pltpu.VMEM((A_HEADS, 1, tq), F32),
                        pltpu.VMEM((A_HEADS, 1, tq), F32),
                        pltpu.VMEM((A_HEADS, A_HEAD_DIM, tq), F32),
                        pltpu.VMEM((A_HEADS, tq, tq), F32)],
        compiler_params=_cparams(("arbitrary", "arbitrary")),
        name="dsa_attention",
    )(p16, p32, p32, p16, vt, p32, bias_tiles)


def _t5_bucket(rel):
    max_exact = REL_BUCKETS // 2
    relf = jnp.maximum(rel, 1).astype(F32)
    large = max_exact + (jnp.log(relf / max_exact) / math.log(REL_MAX_DIST / max_exact)
                         * (REL_BUCKETS - max_exact)).astype(jnp.int32)
    large = jnp.minimum(large, REL_BUCKETS - 1)
    return jnp.where(rel < max_exact, rel, large)


def _bias_tiles(rel_bias, tq):
    assert tq >= REL_MAX_DIST
    key = jnp.arange(tq, dtype=jnp.int32)[:, None]
    qry = jnp.arange(tq, dtype=jnp.int32)[None, :]
    bucket = jnp.stack([_t5_bucket(jnp.maximum(lag * tq + qry - key, 0)) for lag in range(2)])
    rel = ((rel_bias - rel_bias[REL_BUCKETS - 1:REL_BUCKETS]) * LOG2_E).astype(F32)
    onehot = bucket[None] == jnp.arange(REL_BUCKETS, dtype=jnp.int32)[:, None, None, None]
    return jnp.sum(jnp.where(onehot[:, None], rel[:, :, None, None, None], 0.0), axis=0)


def _hgrn_kernel(q_ref, f_ref, i_ref, g_ref, lb_ref, ng_ref, tril_ref, o_ref, st_ref, *, rows):
    @pl.when(pl.program_id(1) == 0)
    def _():
        st_ref[...] = jnp.zeros_like(st_ref)

    lb = lb_ref[...]
    f = lb + (1.0 - lb) * jax.nn.sigmoid(f_ref[...])
    logf = jnp.log(f)
    kk = 1.0 - f
    g1 = logf.astype(BF16)
    r1 = logf - g1.astype(F32)
    g2 = r1.astype(BF16)
    g3 = (r1 - g2.astype(F32)).astype(BF16)
    tril = tril_ref[...]
    bcum = (jnp.dot(tril, g1, preferred_element_type=F32)
            + jnp.dot(tril, g2, preferred_element_type=F32)
            + jnp.dot(tril, g3, preferred_element_type=F32))

    srow = lax.broadcasted_iota(jnp.int32, (HGRN_C, B_HEAD_DIM), 0)
    trow = lax.broadcasted_iota(jnp.int32, (HGRN_SB, HGRN_C), 0)
    scol = lax.broadcasted_iota(jnp.int32, (HGRN_SB, HGRN_C), 1)
    ng = ng_ref[...]
    q = q_ref[...].astype(F32)
    qb_all = (q * jnp.exp(bcum)).astype(BF16)

    tiles = [(n, h) for n in range(rows // HGRN_C) for h in range(B_HEADS)]

    def rs(n):
        return slice(n * HGRN_C, (n + 1) * HGRN_C)

    def hs(h):
        return slice(h * B_HEAD_DIM, (h + 1) * B_HEAD_DIM)

    a_parts = {}
    for n, h in tiles:
        bc, qc, kc = bcum[rs(n), hs(h)], q[rs(n), hs(h)], kk[rs(n), hs(h)]
        for sb in range(HGRN_C // HGRN_SB):
            s0 = sb * HGRN_SB
            beta = bc[s0 - 1:s0] if sb > 0 else jnp.zeros((1, B_HEAD_DIM), F32)
            qs = (qc[s0:s0 + HGRN_SB] * jnp.exp(bc[s0:s0 + HGRN_SB] - beta)).astype(BF16)
            expo = jnp.where(srow < s0 + HGRN_SB, beta - bc, -jnp.inf)
            ks = (kc * jnp.exp(expo)).astype(BF16)
            a_parts[n, h, sb] = lax.dot_general(qs, ks, NT_DIMS, preferred_element_type=F32)
    intra, upd, dec = {}, {}, {}
    for n, h in tiles:
        a_rows = [jnp.where(scol <= trow + sb * HGRN_SB, a_parts[n, h, sb], 0.0)
                  for sb in range(HGRN_C // HGRN_SB)]
        attn = jnp.concatenate(a_rows, axis=0).astype(BF16)
        vc = i_ref[rs(n), hs(h)]
        intra[n, h] = jnp.dot(attn, vc, preferred_element_type=F32)
        bc = bcum[rs(n), hs(h)]
        blast = bc[HGRN_C - 1:HGRN_C]
        kdec = (kk[rs(n), hs(h)] * jnp.exp(blast - bc)).astype(BF16)
        upd[n, h] = lax.dot_general(vc, kdec, TN_DIMS, preferred_element_type=F32)
        dec[n, h] = jnp.exp(blast)
    for n, h in tiles:
        st = st_ref[h]
        o = intra[n, h] + lax.dot_general(qb_all[rs(n), hs(h)], st.astype(BF16), NT_DIMS,
                                          preferred_element_type=F32)
        st_ref[h] = st * dec[n, h] + upd[n, h]
        ms = jnp.mean(o * o, axis=-1, keepdims=True)
        on = o * lax.rsqrt(ms + EPS) * ng
        o_ref[rs(n), hs(h)] = (on * _silu(g_ref[rs(n), hs(h)].astype(F32))).astype(o_ref.dtype)


def _hgrn2(p16, p32, lb_l, norm_g, bsz, seq):
    rows = min(HGRN_L, seq)
    nj = seq // rows
    n = bsz * seq
    r = jnp.arange(rows, dtype=jnp.int32)
    tril = ((r[:, None] >= r[None, :]) & (r[:, None] // HGRN_C == r[None, :] // HGRN_C)).astype(BF16)

    def col(base):
        return lambda b, j: (b * nj + j, base)

    return pl.pallas_call(
        functools.partial(_hgrn_kernel, rows=rows),
        grid=(bsz, nj),
        in_specs=[
            pl.BlockSpec((rows, B_WIDTH), col(P16_BQ)),
            pl.BlockSpec((rows, B_WIDTH), col(P32_BF // B_WIDTH)),
            pl.BlockSpec((rows, B_WIDTH), col(P16_BI)),
            pl.BlockSpec((rows, B_WIDTH), col(P16_BG)),
            pl.BlockSpec((1, B_WIDTH), lambda b, j: (0, 0)),
            pl.BlockSpec((1, B_HEAD_DIM), lambda b, j: (0, 0)),
            pl.BlockSpec((rows, rows), lambda b, j: (0, 0)),
        ],
        out_specs=pl.BlockSpec((rows, B_WIDTH), lambda b, j: (b * nj + j, 0)),
        out_shape=jax.ShapeDtypeStruct((n, B_WIDTH), BF16),
        scratch_shapes=[pltpu.VMEM((B_HEADS, B_HEAD_DIM, B_HEAD_DIM), F32)],
        compiler_params=_cparams(("arbitrary", "arbitrary")),
        name="hgrn2",
    )(p16, p32, p16, p16, lb_l.reshape(1, B_WIDTH), norm_g.reshape(1, B_HEAD_DIM), tril)


def _ret_kernel(q_ref, k_ref, v_ref, g_ref, cos_ref, sin_ref, idec_ref, qdec_ref, kdec_ref, cdec_ref,
                o_ref, st_ref):
    @pl.when(pl.program_id(1) == 0)
    def _():
        st_ref[...] = jnp.zeros_like(st_ref)

    heads = range(C_HEADS)
    cos = jnp.concatenate([cos_ref[...]] * C_HEADS, axis=1)
    sin_signed = jnp.concatenate([sin_ref[...]] * C_HEADS, axis=1)
    even = lax.broadcasted_iota(jnp.int32, cos.shape, 1) % 2 == 0

    def rot(a):
        swapped = jnp.where(even, pltpu.roll(a, C_QK_WIDTH - 1, 1), pltpu.roll(a, 1, 1))
        return a * cos + swapped * sin_signed

    qr = rot(q_ref[...].astype(F32))
    kr = rot(k_ref[...].astype(F32))
    qk = [slice(h * C_QK_DIM, (h + 1) * C_QK_DIM) for h in heads]
    vs = [slice(h * C_V_DIM, (h + 1) * C_V_DIM) for h in heads]
    attn = [lax.dot_general(qr[:, qk[h]].astype(BF16), kr[:, qk[h]].astype(BF16), NT_DIMS,
                            preferred_element_type=F32) * idec_ref[h] for h in heads]
    inter = [jnp.dot((qr[:, qk[h]] * qdec_ref[h]).astype(BF16), st_ref[h].astype(BF16),
                     preferred_element_type=F32) for h in heads]
    intra = [jnp.dot(attn[h].astype(BF16), v_ref[:, vs[h]], preferred_element_type=F32) for h in heads]
    upd = [jnp.dot((kr[:, qk[h]] * kdec_ref[h]).T.astype(BF16), v_ref[:, vs[h]],
                   preferred_element_type=F32) for h in heads]
    for h in heads:
        st_ref[h] = cdec_ref[h, 0:1, :] * st_ref[h] + upd[h]
        o = intra[h] + inter[h]
        ms = jnp.mean(o * o, axis=-1, keepdims=True)
        o_ref[:, vs[h]] = (_silu(g_ref[:, vs[h]].astype(F32)) * (o * lax.rsqrt(ms + EPS))).astype(o_ref.dtype)


def _retention_tables(seq):
    pos = jnp.arange(seq, dtype=F32)
    theta = jnp.repeat(1.0 / (10000.0 ** jnp.linspace(0.0, 1.0, C_QK_DIM // 2)), 2)
    ang = pos[:, None] * theta[None, :]
    pair_sign = jnp.where(jnp.arange(C_QK_DIM) % 2 == 0, -1.0, 1.0)
    log_gamma = jnp.log(1.0 - 2.0 ** (-5.0 - jnp.arange(C_HEADS, dtype=F32)))
    idx = jnp.arange(RET_C, dtype=F32)
    causal = idx[:, None] >= idx[None, :]
    idec = jnp.exp(jnp.where(causal[None], (idx[:, None] - idx[None, :])[None] * log_gamma[:, None, None],
                             -jnp.inf))
    qdec = jnp.exp((idx + 1.0)[None, :] * log_gamma[:, None])[..., None]
    kdec = jnp.exp((RET_C - 1.0 - idx)[None, :] * log_gamma[:, None])[..., None]
    cdec = jnp.exp(RET_C * log_gamma)[:, None, None]
    return (jnp.cos(ang), jnp.sin(ang) * pair_sign[None, :], idec,
            jnp.broadcast_to(qdec, (C_HEADS, RET_C, C_QK_DIM)),
            jnp.broadcast_to(kdec, (C_HEADS, RET_C, C_QK_DIM)),
            jnp.broadcast_to(cdec, (C_HEADS, 8, C_V_DIM)))


def _retention(p16, tables, bsz, seq):
    cos, sin, idec, qdec, kdec, cdec = tables
    nj = seq // RET_C
    n = bsz * seq
    v_blk = C_V_WIDTH // 1024

    def whole(a):
        return pl.BlockSpec(a.shape, lambda b, j: (0,) * a.ndim)

    return pl.pallas_call(
        _ret_kernel,
        grid=(bsz, nj),
        in_specs=[
            pl.BlockSpec((RET_C, C_QK_WIDTH), lambda b, j: (b * nj + j, P16_CQ)),
            pl.BlockSpec((RET_C, C_QK_WIDTH), lambda b, j: (b * nj + j, P16_CK)),
            pl.BlockSpec((RET_C, C_V_WIDTH), lambda b, j: (b * nj + j, P16_CV // v_blk)),
            pl.BlockSpec((RET_C, C_V_WIDTH), lambda b, j: (b * nj + j, P16_CG // v_blk)),
            pl.BlockSpec((RET_C, C_QK_DIM), lambda b, j: (j, 0)),
            pl.BlockSpec((RET_C, C_QK_DIM), lambda b, j: (j, 0)),
            whole(idec), whole(qdec), whole(kdec), whole(cdec),
        ],
        out_specs=pl.BlockSpec((RET_C, C_V_WIDTH), lambda b, j: (b * nj + j, 0)),
        out_shape=jax.ShapeDtypeStruct((n, C_V_WIDTH), BF16),
        scratch_shapes=[pltpu.VMEM((C_HEADS, C_QK_DIM, C_V_DIM), F32)],
        compiler_params=_cparams(("arbitrary", "arbitrary")),
        name="retention",
    )(p16, p16, p16, p16, cos, sin, idec, qdec, kdec, cdec)


def _merge_kernel(oa_ref, ob_ref, oc_ref, ga_ref, gb_ref, gc_ref, x_ref, mod_ref, g2_ref,
                  wa_ref, wb_ref, wc_ref, wo_ref, wrh_ref, wrl_ref, br_ref,
                  x1_ref, h2_ref, route_ref, *, tiles_per_batch):
    b = pl.program_id(0) // tiles_per_batch

    def gated(o_ref, w_ref, g_ref):
        y = jnp.dot(o_ref[...], w_ref[...], preferred_element_type=F32)
        return jax.nn.sigmoid(g_ref[...].astype(F32)) * y

    merged = gated(oa_ref, wa_ref, ga_ref) + gated(ob_ref, wb_ref, gb_ref) + gated(oc_ref, wc_ref, gc_ref)
    y = jnp.dot(merged.astype(BF16), wo_ref[...], preferred_element_type=F32)
    gt1 = mod_ref[pl.ds(b, 1), 2 * D_MODEL:3 * D_MODEL]
    x1 = x_ref[...] + gt1 * y
    x1_ref[...] = x1
    sh2 = mod_ref[pl.ds(b, 1), 3 * D_MODEL:4 * D_MODEL]
    sc2 = mod_ref[pl.ds(b, 1), 4 * D_MODEL:5 * D_MODEL]
    h2 = _rms_mod(x1, g2_ref[...], sc2, sh2)
    h_hi = h2.astype(BF16)
    h_f = h_hi.astype(F32)
    for j in range(h2_ref.shape[0]):
        h2_ref[j] = h_f[:, j * SC_SUB:(j + 1) * SC_SUB]
    h_lo = (h2 - h_hi.astype(F32)).astype(BF16)
    logits = (jnp.dot(h_hi, wrh_ref[...], preferred_element_type=F32)
              + jnp.dot(h_lo, wrh_ref[...], preferred_element_type=F32)
              + jnp.dot(h_hi, wrl_ref[...], preferred_element_type=F32)) + br_ref[...]
    lane = lax.broadcasted_iota(jnp.int32, logits.shape, 1).astype(F32)
    neg_inf = -jnp.inf

    def first_argmax(vals):
        top = jnp.max(vals, axis=-1, keepdims=True)
        idx = jnp.min(jnp.where(vals == top, lane, float(LANES)), axis=-1, keepdims=True)
        return top, idx

    gl = jnp.where(lane < N_GROUPS, logits, neg_inf)
    gmax, gsel = first_argmax(gl)
    gprob = 1.0 / jnp.sum(jnp.exp(gl - gmax), axis=-1, keepdims=True)
    lo = N_GROUPS + EXPERTS_PER_GROUP * gsel
    el = jnp.where((lane >= lo) & (lane < lo + EXPERTS_PER_GROUP), logits, neg_inf)
    v1, i1 = first_argmax(el)
    el2 = jnp.where(lane == i1, neg_inf, el)
    v2, i2 = first_argmax(el2)
    e2 = jnp.exp(v2 - v1)
    den = 1.0 + e2
    route_ref[...] = jnp.where(lane == 0.0, i1 - N_GROUPS,
                               jnp.where(lane == 1.0, i2 - N_GROUPS,
                                         jnp.where(lane == 2.0, gprob / den,
                                                   jnp.where(lane == 3.0, gprob * (e2 / den), 0.0))))


def _merge(o_a, o_b, o_c, p16, x2, mod_l, g2, wa, wb, wc, wo, wr_hi, wr_lo, br, seq):
    n = x2.shape[0]
    tm = min(512, seq)
    one = pl.Buffered(1)

    def rows(width, cb=0):
        return pl.BlockSpec((tm, width), lambda i: (i, cb))

    def whole(a):
        return pl.BlockSpec(a.shape, lambda i: (0,) * a.ndim, pipeline_mode=one)

    return pl.pallas_call(
        functools.partial(_merge_kernel, tiles_per_batch=seq // tm),
        grid=(n // tm,),
        in_specs=[rows(A_WIDTH), rows(B_WIDTH), rows(C_V_WIDTH),
                  rows(D_MODEL, P16_GA), rows(D_MODEL, P16_GB), rows(D_MODEL, P16_GC),
                  rows(D_MODEL), whole(mod_l), pl.BlockSpec((1, D_MODEL), lambda i: (0, 0)),
                  whole(wa), whole(wb), whole(wc), whole(wo), whole(wr_hi), whole(wr_lo), whole(br)],
        out_specs=[rows(D_MODEL), pl.BlockSpec((SC_PIECES, tm, SC_SUB), lambda i: (0, i, 0)), rows(LANES)],
        out_shape=[jax.ShapeDtypeStruct((n, D_MODEL), F32),
                   jax.ShapeDtypeStruct((SC_PIECES, n, SC_SUB), F32),
                   jax.ShapeDtypeStruct((n, LANES), F32)],
        compiler_params=_cparams(("arbitrary",)),
        name="merge_route",
    )(o_a, o_b, o_c, p16, p16, p16, x2, mod_l, g2.reshape(1, D_MODEL), wa, wb, wc, wo, wr_hi, wr_lo, br)


def _lane_pick(vals, lane, idx):
    return jnp.sum(jnp.where(lane == idx, vals, 0.0), axis=-1, keepdims=True)


def _rank_kernel(route_ref, rk_ref, cnt_ref, run_ref):
    @pl.when(pl.program_id(0) == 0)
    def _():
        run_ref[...] = jnp.zeros_like(run_ref)

    route = route_ref[...]
    tb = route.shape[0]
    lane = lax.broadcasted_iota(jnp.int32, route.shape, 1).astype(F32)
    e1 = _lane_pick(route, lane, 0.0)
    e2 = _lane_pick(route, lane, 1.0)
    sel = jnp.where((lane == e1) | (lane == e2), 1.0, 0.0)
    r = lax.broadcasted_iota(jnp.int32, (tb, tb), 0)
    c = lax.broadcasted_iota(jnp.int32, (tb, tb), 1)
    before = jnp.where(c < r, 1.0, 0.0).astype(BF16)
    rank = jnp.dot(before, sel.astype(BF16), preferred_element_type=F32) + run_ref[0:1, :]
    rk_ref[...] = jnp.where(lane == 0.0, _lane_pick(rank, lane, e1),
                            jnp.where(lane == 1.0, _lane_pick(rank, lane, e2), 0.0))
    run_ref[...] = run_ref[...] + jnp.sum(sel, axis=0, keepdims=True)
    cnt_ref[...] = run_ref[...]


def _expert_ranks(route):
    n = route.shape[0]
    tb = min(256, n)
    return pl.pallas_call(
        _rank_kernel,
        grid=(n // tb,),
        in_specs=[pl.BlockSpec((tb, LANES), lambda i: (i, 0))],
        out_specs=[pl.BlockSpec((tb, LANES), lambda i: (i, 0)),
                   pl.BlockSpec((8, LANES), lambda i: (0, 0))],
        out_shape=[jax.ShapeDtypeStruct((n, LANES), F32), jax.ShapeDtypeStruct((8, LANES), F32)],
        scratch_shapes=[pltpu.VMEM((8, LANES), F32)],
        compiler_params=_cparams(("arbitrary",)),
        name="expert_ranks",
    )(route)


def _plan_kernel(cnt_ref, route_ref, rk_ref, pos_ref, tmap_ref):
    lane_i = lax.broadcasted_iota(jnp.int32, (8, LANES), 1)
    cnt = jnp.where(lane_i < N_EXPERTS, cnt_ref[...], 0.0)
    padded = jnp.floor((cnt + (MOE_TM - 1)) * (1.0 / MOE_TM)) * MOE_TM
    r = lax.broadcasted_iota(jnp.int32, (LANES, LANES), 0)
    c = lax.broadcasted_iota(jnp.int32, (LANES, LANES), 1)
    base = jnp.dot(padded, jnp.where(r < c, 1.0, 0.0), precision=lax.Precision.HIGHEST,
                   preferred_element_type=F32)

    route = route_ref[...]
    lane = lax.broadcasted_iota(jnp.int32, route.shape, 1).astype(F32)
    rk = rk_ref[...]
    base_row = base[0:1, :]
    pos1 = _lane_pick(base_row, lane, _lane_pick(route, lane, 0.0)) + _lane_pick(rk, lane, 0.0)
    pos2 = _lane_pick(base_row, lane, _lane_pick(route, lane, 1.0)) + _lane_pick(rk, lane, 1.0)
    pos_ref[...] = jnp.where(lane == 0.0, pos1, jnp.where(lane == 1.0, pos2, 0.0)).astype(jnp.int32)

    @pl.when(pl.program_id(0) == 0)
    def _():
        nt = tmap_ref.shape[0]
        tlane = lax.broadcasted_iota(jnp.int32, (nt, LANES), 1)
        start = (lax.broadcasted_iota(jnp.int32, (nt, LANES), 0) * MOE_TM).astype(F32)
        end_row = jnp.where(tlane < N_EXPERTS, base_row + padded[0:1, :], 3e38)
        expert = jnp.sum(jnp.where(end_row <= start, 1.0, 0.0), axis=-1, keepdims=True)
        expert_c = jnp.minimum(expert, N_EXPERTS - 1.0)
        tl = tlane.astype(F32)
        left = _lane_pick(cnt[0:1, :], tl, expert_c) - (start[:, 0:1] - _lane_pick(base_row, tl, expert_c))
        valid = jnp.where(expert < N_EXPERTS, jnp.clip(left, 0.0, float(MOE_TM)), 0.0)
        tmap_ref[...] = jnp.where(tlane == 0, expert_c, jnp.where(tlane == 1, valid, 0.0)).astype(jnp.int32)


def _expert_plan(cnt, route, rk, n_tiles):
    n = route.shape[0]
    tb = min(256, n)
    nt_pad = -(-n_tiles // 8) * 8
    return pl.pallas_call(
        _plan_kernel,
        grid=(n // tb,),
        in_specs=[pl.BlockSpec((8, LANES), lambda i: (0, 0)),
                  pl.BlockSpec((tb, LANES), lambda i: (i, 0)),
                  pl.BlockSpec((tb, LANES), lambda i: (i, 0))],
        out_specs=[pl.BlockSpec((tb, LANES), lambda i: (i, 0)),
                   pl.BlockSpec((nt_pad, LANES), lambda i: (0, 0))],
        out_shape=[jax.ShapeDtypeStruct((n, LANES), jnp.int32),
                   jax.ShapeDtypeStruct((nt_pad, LANES), jnp.int32)],
        compiler_params=_cparams(("arbitrary",)),
        name="expert_plan",
    )(cnt, route, rk)


def _sc_mesh():
    return plsc.VectorSubcoreMesh(core_axis_name="c", subcore_axis_name="s")


def _sc_scatter_rows(src, idx, out_rows):
    m = idx.shape[0]
    n_src_win = src.shape[0] // SC_WINDOW

    @functools.partial(pl.kernel, out_type=jax.ShapeDtypeStruct((out_rows, src.shape[1]), src.dtype),
                       mesh=_sc_mesh(), scratch_types=[])
    def scatter(x_hbm, i_hbm, o_hbm):
        def body(x_vmem, i_vmem):
            pltpu.sync_copy(x_vmem, o_hbm.at[i_vmem.at[0]])

        pltpu.emit_pipeline(
            body, grid=(m // SC_WINDOW,),
            in_specs=[pl.BlockSpec((SC_WINDOW, src.shape[1]), lambda i: (i % n_src_win, 0)),
                      pl.BlockSpec((1, SC_WINDOW), lambda i: (0, i))],
            out_specs=[], core_axis_name=("c", "s"),
            dimension_semantics=(pltpu.PARALLEL,))(x_hbm, i_hbm)

    return scatter(src, idx.reshape(1, m))


def _sc_gather_rows(table, idx):
    m = idx.shape[0]

    @functools.partial(pl.kernel, out_type=jax.ShapeDtypeStruct((m, table.shape[1]), table.dtype),
                       mesh=_sc_mesh(), scratch_types=[])
    def gather(x_hbm, i_hbm, o_hbm):
        def body(i_vmem, o_vmem):
            pltpu.sync_copy(x_hbm.at[i_vmem.at[0]], o_vmem)

        pltpu.emit_pipeline(
            body, grid=(m // SC_WINDOW,),
            in_specs=[pl.BlockSpec((1, SC_WINDOW), lambda i: (0, i))],
            out_specs=[pl.BlockSpec((SC_WINDOW, table.shape[1]), lambda i: (i, 0))],
            core_axis_name=("c", "s"),
            dimension_semantics=(pltpu.PARALLEL,))(i_hbm, o_hbm)

    return gather(table, idx.reshape(1, m))


def _piece_row_index(pos, rows):
    return (jnp.arange(SC_PIECES, dtype=jnp.int32)[:, None] * rows + pos[None, :]).reshape(-1)


def _grouped_kernel(te_ref, tv_ref, x_ref, wg_ref, wu_ref, wd_ref, o_ref):
    valid = tv_ref[pl.program_id(0)]

    @pl.when(valid > 0)
    def _():
        x = jnp.concatenate([x_ref[j] for j in range(SC_PIECES)], axis=1)
        row = lax.broadcasted_iota(jnp.int32, x.shape, 0)
        x = jnp.where(row < valid, x, 0.0).astype(BF16)
        a = jnp.dot(x, wg_ref[0].astype(BF16), preferred_element_type=F32)
        u = jnp.dot(x, wu_ref[0].astype(BF16), preferred_element_type=F32)
        hm = (_silu(a) * u).astype(BF16)
        o = jnp.dot(hm, wd_ref[0].astype(BF16), preferred_element_type=F32)
        for j in range(SC_PIECES):
            o_ref[j] = o[:, j * SC_SUB:(j + 1) * SC_SUB]

    @pl.when(valid <= 0)
    def _():
        o_ref[...] = jnp.zeros_like(o_ref)


def _grouped_experts(tile_expert, tile_valid, xs, wg, wu, wd):
    n_tiles = tile_expert.shape[0]
    rows_block = pl.BlockSpec((SC_PIECES, MOE_TM, SC_SUB), lambda i, te, tv: (0, i, 0))
    return pl.pallas_call(
        _grouped_kernel,
        grid_spec=pltpu.PrefetchScalarGridSpec(
            num_scalar_prefetch=2,
            grid=(n_tiles,),
            in_specs=[rows_block,
                      pl.BlockSpec((1, D_MODEL, D_EXPERT), lambda i, te, tv: (te[i], 0, 0)),
                      pl.BlockSpec((1, D_MODEL, D_EXPERT), lambda i, te, tv: (te[i], 0, 0)),
                      pl.BlockSpec((1, D_EXPERT, D_MODEL), lambda i, te, tv: (te[i], 0, 0))],
            out_specs=rows_block),
        out_shape=jax.ShapeDtypeStruct(xs.shape, F32),
        compiler_params=_cparams(("arbitrary",)),
        name="grouped_experts",
    )(tile_expert, tile_valid, xs, wg, wu, wd)


def _combine_kernel(x1_ref, y_ref, route_ref, mod_ref, o_ref, *, tiles_per_batch):
    b = pl.program_id(0) // tiles_per_batch
    gt2 = mod_ref[pl.ds(b, 1), 5 * D_MODEL:6 * D_MODEL]
    route = route_ref[...]
    lane = lax.broadcasted_iota(jnp.int32, route.shape, 1).astype(F32)
    w1 = _lane_pick(route, lane, 2.0)
    w2 = _lane_pick(route, lane, 3.0)
    y = jnp.concatenate([w1 * y_ref[0, j] + w2 * y_ref[1, j] for j in range(SC_PIECES)], axis=1)
    o_ref[...] = x1_ref[...] + gt2 * y


def _combine(x1, y2, route, mod_l, seq):
    n = x1.shape[0]
    tm = min(1024, seq)
    return pl.pallas_call(
        functools.partial(_combine_kernel, tiles_per_batch=seq // tm),
        grid=(n // tm,),
        in_specs=[pl.BlockSpec((tm, D_MODEL), lambda i: (i, 0)),
                  pl.BlockSpec((2, SC_PIECES, tm, SC_SUB), lambda i: (0, 0, i, 0)),
                  pl.BlockSpec((tm, LANES), lambda i: (i, 0)),
                  pl.BlockSpec(mod_l.shape, lambda i: (0, 0))],
        out_specs=pl.BlockSpec((tm, D_MODEL), lambda i: (i, 0)),
        out_shape=jax.ShapeDtypeStruct((n, D_MODEL), F32),
        compiler_params=_cparams(("arbitrary",)),
        name="moe_combine",
    )(x1, y2, route, mod_l)


def _moe(h2, route, x1, mod_l, wg, wu, wd, seq):
    n = h2.shape[1]
    n_tiles = (2 * n) // MOE_TM + N_EXPERTS
    rows = n_tiles * MOE_TM
    rk, cnt = _expert_ranks(route)
    pos, tmap = _expert_plan(cnt, route, rk, n_tiles)
    idx = jnp.concatenate([_piece_row_index(pos[:, 0], rows), _piece_row_index(pos[:, 1], rows)])
    xs = _sc_scatter_rows(h2.reshape(SC_PIECES * n, SC_SUB), idx, SC_PIECES * rows)
    ys = _grouped_experts(tmap[:n_tiles, 0], tmap[:n_tiles, 1], xs.reshape(SC_PIECES, rows, SC_SUB),
                          wg, wu, wd)
    y2 = _sc_gather_rows(ys.reshape(SC_PIECES * rows, SC_SUB), idx).reshape(2, SC_PIECES, n, SC_SUB)
    return _combine(x1, y2, route, mod_l, seq)


def _final_norm_kernel(x_ref, g_ref, o_ref):
    x = x_ref[...]
    ms = jnp.mean(x * x, axis=-1, keepdims=True)
    o_ref[...] = x * lax.rsqrt(ms + EPS) * g_ref[...]


def _final_norm(x2, g, seq):
    n = x2.shape[0]
    tm = min(1024, seq)
    return pl.pallas_call(
        _final_norm_kernel,
        grid=(n // tm,),
        in_specs=[pl.BlockSpec((tm, D_MODEL), lambda i: (i, 0)),
                  pl.BlockSpec((1, D_MODEL), lambda i: (0, 0))],
        out_specs=pl.BlockSpec((tm, D_MODEL), lambda i: (i, 0)),
        out_shape=jax.ShapeDtypeStruct((n, D_MODEL), F32),
        compiler_params=_cparams(("arbitrary",)),
        name="final_norm",
    )(x2, g.reshape(1, D_MODEL))


_IN_OFFS = [sum(IN_SPLITS[:i]) for i in range(len(IN_SPLITS) + 1)]
(_AQ, _AK, _AV, _IQ, _IK, _IW, _BQ, _BF, _BI, _BG, _CQ, _CK, _CV, _CG, _GA, _GB, _GC) = range(len(IN_SPLITS))


def _pack_kernel(w_ref, w16_ref, w32_ref, wvt_ref):
    def cols(seg):
        return w_ref[0, :, _IN_OFFS[seg]:_IN_OFFS[seg + 1]]

    scale = {_AQ: A_HEAD_DIM ** -0.5 * LOG2_E, _CK: C_QK_DIM ** -0.5}
    at = 0
    for seg in (_CV, _CG, _AQ, _AK, _BQ, _BI, _BG, _CQ, _CK, _GA, _GB, _GC):
        v = cols(seg)
        if seg in scale:
            v = v * scale[seg]
        w16_ref[0, :, at:at + v.shape[1]] = v.astype(BF16)
        at += v.shape[1]

    rows = w_ref.shape[1]
    w32_ref[0, :, P32_BF:P32_BF + B_WIDTH] = cols(_BF).astype(BF16)
    iq = cols(_IQ)
    zeros = jnp.zeros((rows, LANES - IDX_DIM), F32)
    for h in range(IDX_HEADS):
        w32_ref[0, :, P32_IQ + h * LANES:P32_IQ + (h + 1) * LANES] = jnp.concatenate(
            [iq[:, h * IDX_DIM:(h + 1) * IDX_DIM], zeros], axis=1).astype(BF16)
    w32_ref[0, :, P32_IK:P32_IK + LANES] = jnp.concatenate([cols(_IK), zeros], axis=1).astype(BF16)
    w32_ref[0, :, P32_IW:P32_IW + LANES] = jnp.concatenate(
        [cols(_IW), jnp.zeros((rows, LANES - IDX_HEADS), F32)], axis=1).astype(BF16)
    wvt_ref[0] = cols(_AV).T.astype(BF16)


def _pack_w_in(w_in):
    depth, d, width = w_in.shape
    rows = LANES
    w16_width = 14 * 1024
    w32_width = P32_IW + LANES
    return pl.pallas_call(
        _pack_kernel,
        grid=(depth, d // rows),
        in_specs=[pl.BlockSpec((1, rows, width), lambda l, r: (l, r, 0))],
        out_specs=[pl.BlockSpec((1, rows, w16_width), lambda l, r: (l, r, 0)),
                   pl.BlockSpec((1, rows, w32_width), lambda l, r: (l, r, 0)),
                   pl.BlockSpec((1, A_WIDTH, rows), lambda l, r: (l, 0, r))],
        out_shape=[jax.ShapeDtypeStruct((depth, d, w16_width), BF16),
                   jax.ShapeDtypeStruct((depth, d, w32_width), BF16),
                   jax.ShapeDtypeStruct((depth, A_WIDTH, d), BF16)],
        compiler_params=_cparams(("arbitrary", "arbitrary")),
        name="pack_w_in",
    )(w_in)


def _split_bf16(w):
    hi = w.astype(BF16)
    return hi, (w - hi.astype(F32)).astype(BF16)


def kernel(x, c, rel_bias, hgrn_lb_raw, norm1_g, norm2_g, ada_w, ada_b, w_in, hgrn_norm_g, w_branch_a,
           w_branch_b, w_branch_c, w_out, router_group_w, router_group_b, router_expert_w,
           router_expert_b, expert_w_gate, expert_w_up, expert_w_down, final_norm_g):
    bsz, seq, _ = x.shape
    depth = w_in.shape[0]
    n = bsz * seq
    x2 = x.reshape(n, D_MODEL)
    tq = min(DSA_TQ, seq)

    lb_all = _hgrn_lower_bounds(hgrn_lb_raw)
    c_pad = jnp.pad(c, ((0, (-bsz) % 8), (0, 0)))
    mod = _ada_mod(c_pad, ada_w, ada_b)
    bias_tiles = _bias_tiles(rel_bias, tq)
    ret_tables = _retention_tables(seq)
    w16_all, w32_all, wvt_all = _pack_w_in(w_in)

    for l in range(depth):
        p16 = _norm_project(x2, mod[l], norm1_g[l], w16_all, l, BF16, 1024, seq, "proj_bf16")
        p32 = _norm_project(x2, mod[l], norm1_g[l], w32_all, l, F32, 768, seq, "proj_f32")
        vt = _norm_project_t(x2, mod[l], norm1_g[l], wvt_all, l, tq, seq, "proj_vt")
        o_a = _dsa_attention(p16, p32, vt, bias_tiles, bsz, seq)
        o_b = _hgrn2(p16, p32, lb_all[l], hgrn_norm_g[l], bsz, seq)
        o_c = _retention(p16, ret_tables, bsz, seq)
        wr = jnp.concatenate([router_group_w[l], router_expert_w[l],
                              jnp.zeros((D_MODEL, LANES - N_GROUPS - N_EXPERTS), F32)], axis=1)
        br = jnp.concatenate([router_group_b[l], router_expert_b[l],
                              jnp.zeros((LANES - N_GROUPS - N_EXPERTS,), F32)]).reshape(1, LANES)
        wr_hi, wr_lo = _split_bf16(wr)
        x1, h2, route = _merge(o_a, o_b, o_c, p16, x2, mod[l], norm2_g[l],
                               w_branch_a[l].astype(BF16), w_branch_b[l].astype(BF16),
                               w_branch_c[l].astype(BF16), w_out[l].astype(BF16),
                               wr_hi, wr_lo, br, seq)
        x2 = _moe(h2, route, x1, mod[l], expert_w_gate[l], expert_w_up[l], expert_w_down[l], seq)

    return _final_norm(x2, final_norm_g, seq).reshape(bsz, seq, D_MODEL)
```

```python
import functools
import math

import jax
import jax.numpy as jnp
from jax import lax
from jax.experimental import pallas as pl
from jax.experimental.pallas import tpu as pltpu
from jax.experimental.pallas import tpu_sc as plsc

F32 = jnp.float32
BF16 = jnp.bfloat16

D_MODEL = 1024
A_HEADS = 8
A_HEAD_DIM = 128
IDX_HEADS = 8
IDX_DIM = 64
TOPK_MAX = 256
REL_BUCKETS = 32
REL_MAX_DIST = 128
B_HEADS = 8
B_HEAD_DIM = 128
C_HEADS = 4
C_QK_DIM = 256
C_V_DIM = 512
N_GROUPS = 4
EXPERTS_PER_GROUP = 8
N_EXPERTS = 32
D_EXPERT = 512
EPS = 1e-6

A_WIDTH = A_HEADS * A_HEAD_DIM
B_WIDTH = B_HEADS * B_HEAD_DIM
C_QK_WIDTH = C_HEADS * C_QK_DIM
C_V_WIDTH = C_HEADS * C_V_DIM
IN_SPLITS = (A_WIDTH, A_WIDTH, A_WIDTH, IDX_HEADS * IDX_DIM, IDX_DIM, IDX_HEADS,
             B_WIDTH, B_WIDTH, B_WIDTH, B_WIDTH,
             C_QK_WIDTH, C_QK_WIDTH, C_V_WIDTH, C_V_WIDTH,
             D_MODEL, D_MODEL, D_MODEL)

LANES = 128
BF16_ROWS = 16
VMEM_LIMIT = 56 * 1024 * 1024

P16_CV, P16_CG = 0, 2
P16_AQ, P16_AK, P16_BQ, P16_BI, P16_BG, P16_CQ, P16_CK, P16_GA, P16_GB, P16_GC = range(4, 14)
P32_BF = 0
P32_IQ = 1024
P32_IK = 2048
P32_IW = 2176

DSA_TQ = 256
HGRN_L = 256
HGRN_C = 64
HGRN_SB = 16
RET_C = 128
KEY_NEG_INF = -2139095041
HALF_BIAS = 32768
MASK_NEG = -1e30
LOG2_E = math.log2(math.e)
COUNT_CHAINS = 4
MOE_TM = 256
SC_WINDOW = 128
SC_SUB = 256
SC_PIECES = D_MODEL // 2 // SC_SUB

NT_DIMS = (((1,), (1,)), ((), ()))
TN_DIMS = (((0,), (0,)), ((), ()))


def _cparams(sem):
    return pltpu.CompilerParams(dimension_semantics=sem, vmem_limit_bytes=VMEM_LIMIT)


def _silu(x):
    return x * jax.nn.sigmoid(x)


def _pack_bf16_pairs(x):
    k = x.shape[1] // 2
    bits = pltpu.bitcast(x.astype(BF16).astype(F32), jnp.int32)
    return (bits[:, :k] & jnp.int32(-65536)) | lax.shift_right_logical(bits[:, k:], 16)


def _unpack_bf16_pairs(words):
    hi = pltpu.bitcast(words & jnp.int32(-65536), F32)
    lo = pltpu.bitcast(lax.shift_left(words, 16), F32)
    return jnp.concatenate([hi, lo], axis=1)


def _lb_kernel(raw_ref, o_ref):
    raw = raw_ref[...]
    m = jnp.max(raw, axis=0, keepdims=True)
    e = jnp.exp(raw - m)
    soft = e / jnp.sum(e, axis=0, keepdims=True)
    run = jnp.zeros_like(soft[0:1])
    for l in range(raw.shape[0]):
        run = run + soft[l:l + 1]
        o_ref[l:l + 1, :] = run - soft[0:1]


def _hgrn_lower_bounds(raw):
    return pl.pallas_call(
        _lb_kernel, out_shape=jax.ShapeDtypeStruct(raw.shape, F32), name="hgrn_lb")(raw)


def _ada_kernel(c_ref, w_ref, b_ref, o_ref):
    a = _silu(c_ref[...])
    o_ref[0] = jnp.dot(a, w_ref[0], precision=lax.Precision.HIGHEST,
                       preferred_element_type=F32) + b_ref[0]


def _ada_mod(c_pad, ada_w, ada_b):
    depth = ada_w.shape[0]
    rows = c_pad.shape[0]
    return pl.pallas_call(
        _ada_kernel,
        grid=(depth, 6),
        in_specs=[pl.BlockSpec((rows, D_MODEL), lambda l, j: (0, 0)),
                  pl.BlockSpec((1, D_MODEL, D_MODEL), lambda l, j: (l, 0, j)),
                  pl.BlockSpec((1, 1, D_MODEL), lambda l, j: (l, 0, j))],
        out_specs=pl.BlockSpec((1, rows, D_MODEL), lambda l, j: (l, 0, j)),
        out_shape=jax.ShapeDtypeStruct((depth, rows, 6 * D_MODEL), F32),
        compiler_params=_cparams(("arbitrary", "arbitrary")),
        name="ada_mod",
    )(c_pad, ada_w, ada_b.reshape(depth, 1, 6 * D_MODEL))


def _rms_mod(x, g, sc, sh):
    ms = jnp.mean(x * x, axis=-1, keepdims=True)
    return (x * lax.rsqrt(ms + EPS) * g) * (1.0 + sc) + sh


def _norm1(x_ref, mod_ref, g_ref, b):
    sh = mod_ref[pl.ds(b, 1), 0:D_MODEL]
    sc = mod_ref[pl.ds(b, 1), D_MODEL:2 * D_MODEL]
    return _rms_mod(x_ref[...], g_ref[...], sc, sh).astype(BF16)


def _proj_kernel(x_ref, mod_ref, g_ref, w_ref, o_ref, h_ref, *, tiles_per_batch):
    @pl.when(pl.program_id(1) == 0)
    def _():
        h_ref[...] = _norm1(x_ref, mod_ref, g_ref, pl.program_id(0) // tiles_per_batch)

    o_ref[...] = jnp.dot(h_ref[...], w_ref[...], preferred_element_type=F32).astype(o_ref.dtype)


def _norm_project(x2, mod_l, g, w_all, layer, out_dtype, tn, seq, name):
    n = x2.shape[0]
    width = w_all.shape[2]
    tm = min(1024, seq)
    return pl.pallas_call(
        functools.partial(_proj_kernel, tiles_per_batch=seq // tm),
        grid=(n // tm, width // tn),
        in_specs=[pl.BlockSpec((tm, D_MODEL), lambda i, j: (i, 0)),
                  pl.BlockSpec(mod_l.shape, lambda i, j: (0, 0)),
                  pl.BlockSpec((1, D_MODEL), lambda i, j: (0, 0)),
                  pl.BlockSpec((None, D_MODEL, tn), lambda i, j: (layer, 0, j))],
        out_specs=pl.BlockSpec((tm, tn), lambda i, j: (i, j)),
        out_shape=jax.ShapeDtypeStruct((n, width), out_dtype),
        scratch_shapes=[pltpu.VMEM((tm, D_MODEL), BF16)],
        compiler_params=_cparams(("arbitrary", "arbitrary")),
        name=name,
    )(x2, mod_l, g.reshape(1, D_MODEL), w_all)


def _proj_t_kernel(x_ref, mod_ref, g_ref, wt_ref, o_ref, *, tiles_per_batch, chunk):
    h = _norm1(x_ref, mod_ref, g_ref, pl.program_id(0) // tiles_per_batch)
    res = lax.dot_general(wt_ref[...], h, NT_DIMS, preferred_element_type=F32)
    for ci in range(o_ref.shape[0]):
        o_ref[ci] = res[:, ci * chunk:(ci + 1) * chunk].astype(o_ref.dtype)


def _norm_project_t(x2, mod_l, g, wt_all, layer, chunk, seq, name):
    n = x2.shape[0]
    cols = wt_all.shape[1]
    tm = min(1024, seq)
    return pl.pallas_call(
        functools.partial(_proj_t_kernel, tiles_per_batch=seq // tm, chunk=chunk),
        grid=(n // tm,),
        in_specs=[pl.BlockSpec((tm, D_MODEL), lambda i: (i, 0)),
                  pl.BlockSpec(mod_l.shape, lambda i: (0, 0)),
                  pl.BlockSpec((1, D_MODEL), lambda i: (0, 0)),
                  pl.BlockSpec((None, cols, D_MODEL), lambda i: (layer, 0, 0))],
        out_specs=pl.BlockSpec((tm // chunk, cols, chunk), lambda i: (i, 0, 0)),
        out_shape=jax.ShapeDtypeStruct((n // chunk, cols, chunk), BF16),
        compiler_params=_cparams(("arbitrary",)),
        name=name,
    )(x2, mod_l, g.reshape(1, D_MODEL), wt_all)


def _dsa_kernel(q_ref, iq_ref, iw_ref, k_ref, vt_ref, ik_ref, bias_ref, o_ref,
                key_ref, hi_ref, lo_ref, madd_ref, qt_ref, iqt_ref, iwt_ref, m_ref, l_ref, acc_ref, s_ref,
                *, tq, topk):
    qi = pl.program_id(1)
    nck = qi + 1
    idx_scale = (IDX_HEADS * IDX_DIM) ** -0.5

    for h in range(A_HEADS):
        hs = slice(h * LANES, (h + 1) * LANES)
        qt_ref[hs, :] = q_ref[:, hs].astype(F32).T.astype(BF16)
        iqt_ref[hs, :] = iq_ref[:, hs].T.astype(BF16)
    iwt_ref[...] = (iw_ref[...] * idx_scale).T

    krow = lax.broadcasted_iota(jnp.int32, (tq, tq), 0)
    qcol = lax.broadcasted_iota(jnp.int32, (tq, tq), 1)

    def score_chunk(c, carry):
        off = pl.multiple_of(c * tq, tq)
        ikc = ik_ref[pl.ds(off, tq), :].astype(BF16)
        acc = jnp.zeros((tq, tq), F32)
        for h in range(IDX_HEADS):
            s = jnp.dot(ikc, iqt_ref[h * LANES:(h + 1) * LANES, :], preferred_element_type=F32)
            acc = acc + jnp.maximum(s, 0.0) * iwt_ref[h:h + 1, :]
        acc = jnp.where(acc == 0.0, 0.0, acc)
        acc = jnp.where(krow + (c - qi) * tq <= qcol, acc, -jnp.inf)
        kb = pltpu.bitcast(acc, jnp.int32)
        key = jnp.where(kb < 0, kb ^ jnp.int32(0x7FFFFFFF), kb)
        key_ref[c] = key
        hi_ref[c] = jnp.right_shift(key, 16).astype(jnp.int16)
        lo_ref[c] = ((key & 0xFFFF) - HALF_BIAS).astype(jnp.int16)
        return carry

    def paired_loop(count_, fn):
        def pair(i, carry):
            fn(2 * i, carry)
            fn(2 * i + 1, carry)
            return carry

        lax.fori_loop(0, count_ // 2, pair, 0)

        @pl.when(count_ % 2 == 1)
        def _():
            fn(count_ - 1, 0)

    paired_loop(nck, score_chunk)

    def count(ref, pred_fn, rows, zero, one):
        def body(c, parts):
            hit = jnp.where(pred_fn(ref[c]), one, zero)
            parts = list(parts)
            for r in range(tq // rows):
                parts[r % COUNT_CHAINS] = parts[r % COUNT_CHAINS] + hit[r * rows:(r + 1) * rows, :]
            return tuple(parts)

        parts = lax.fori_loop(0, nck, body, (jnp.full((rows, tq), zero),) * COUNT_CHAINS)
        return jnp.sum(sum(p.astype(F32) for p in parts), axis=0, keepdims=True)

    def count16(ref, pred_fn):
        return count(ref, pred_fn, 16, jnp.int16(0), jnp.int16(1))

    def count32(pred_fn):
        return count(key_ref, pred_fn, 8, jnp.float32(0.0), jnp.float32(1.0))

    def bisect16(ref, target):
        def bit_step(i, theta):
            cand = theta + jnp.left_shift(jnp.int32(1), 15 - i)
            cand16 = cand.astype(jnp.int16)
            return jnp.where(count16(ref, lambda k: k >= cand16) >= target, cand, theta)

        return lax.fori_loop(0, 16, bit_step, jnp.full((1, tq), -HALF_BIAS, jnp.int32))

    theta_hi = bisect16(hi_ref, float(topk))
    theta_hi16 = theta_hi.astype(jnp.int16)
    need_lo = topk - count16(hi_ref, lambda k: k > theta_hi16)

    def bucket_chunk(c, carry):
        lo_ref[c] = jnp.where(hi_ref[c] == theta_hi16, lo_ref[c], jnp.int16(-HALF_BIAS))
        return carry

    lax.fori_loop(0, nck, bucket_chunk, 0)
    theta_lo = bisect16(lo_ref, need_lo)
    theta = theta_hi * (2 * HALF_BIAS) + (theta_lo + HALF_BIAS)
    theta = jnp.maximum(theta, KEY_NEG_INF + 1)

    def mask_chunk(c, cnt):
        ge = key_ref[c] >= theta
        madd_ref[c] = jnp.where(ge, 0.0, MASK_NEG)
        return cnt + jnp.sum(jnp.where(ge, 1.0, 0.0), axis=0, keepdims=True)

    cnt_ge = lax.fori_loop(0, nck, mask_chunk, jnp.zeros((1, tq), F32))

    @pl.when(jnp.max(cnt_ge) > topk)
    def _():
        need_eq = topk - count32(lambda kc: kc > theta)
        incl = jnp.where(krow >= qcol, 1.0, 0.0).astype(BF16)

        def tie_chunk(c, run):
            kc = key_ref[c]
            eq = kc == theta
            eqf = jnp.where(eq, 1.0, 0.0)
            pref = jnp.dot(incl, eqf.astype(BF16), preferred_element_type=F32) + run
            eq_add = jnp.where(pref <= need_eq, 0.0, MASK_NEG)
            madd_ref[c] = jnp.where(eq, eq_add, jnp.where(kc > theta, 0.0, MASK_NEG))
            return run + jnp.sum(eqf, axis=0, keepdims=True)

        lax.fori_loop(0, nck, tie_chunk, jnp.zeros((1, tq), F32))

    m_ref[...] = jnp.full(m_ref.shape, -jnp.inf, F32)
    l_ref[...] = jnp.zeros(l_ref.shape, F32)
    acc_ref[...] = jnp.zeros(acc_ref.shape, F32)

    ones_rows = jnp.ones((BF16_ROWS, tq), BF16)

    head_slices = [slice(h * A_HEAD_DIM, (h + 1) * A_HEAD_DIM) for h in range(A_HEADS)]

    def logits(c, h):
        off = pl.multiple_of(c * tq, tq)
        s_ref[h] = jnp.dot(k_ref[pl.ds(off, tq), head_slices[h]], qt_ref[head_slices[h], :],
                           preferred_element_type=F32)

    def attend(c, h, lag):
        hs = head_slices[h]
        s = s_ref[h] + madd_ref[c]
        if lag is not None:
            s = s + bias_ref[h, lag]
        m_old = m_ref[h]
        m_new = jnp.maximum(m_old, jnp.max(s, axis=0, keepdims=True))
        alpha = jnp.exp2(m_old - m_new)
        p = jnp.exp2(s - m_new).astype(BF16)
        pv = jnp.dot(jnp.concatenate([vt_ref[c, hs, :], ones_rows], axis=0), p,
                     preferred_element_type=F32)
        l_ref[h] = alpha * l_ref[h] + pv[A_HEAD_DIM:A_HEAD_DIM + 1]
        acc_ref[h] = alpha * acc_ref[h] + pv[:A_HEAD_DIM]
        m_ref[h] = m_new

    def step(c, lag, prefetch):
        for h in range(A_HEADS):
            attend(c, h, lag)
            if prefetch:
                logits(c + 1, h)

    for h in range(A_HEADS):
        logits(0, h)

    def far_chunk(c, carry):
        step(c, None, True)
        return carry

    paired_loop(jnp.maximum(qi - 1, 0), far_chunk)

    @pl.when(qi >= 1)
    def _():
        step(qi - 1, 1, True)

    step(qi, 0, False)

    for h in range(A_HEADS):
        o = acc_ref[h] * (1.0 / l_ref[h])
        o_ref[:, h * A_HEAD_DIM:(h + 1) * A_HEAD_DIM] = o.T.astype(o_ref.dtype)


def _dsa_attention(p16, p32, vt, bias_tiles, bsz, seq):
    tq = min(DSA_TQ, seq)
    nq = seq // tq
    topk = min(TOPK_MAX, seq // 4)
    n = bsz * seq
    one = pl.Buffered(1)
    return pl.pallas_call(
        functools.partial(_dsa_kernel, tq=tq, topk=topk),
        grid=(bsz, nq),
        in_specs=[
            pl.BlockSpec((tq, A_WIDTH), lambda b, i: (b * nq + i, P16_AQ)),
            pl.BlockSpec((tq, 1024), lambda b, i: (b * nq + i, P32_IQ // 1024)),
            pl.BlockSpec((tq, LANES), lambda b, i: (b * nq + i, P32_IW // LANES)),
            pl.BlockSpec((seq, A_WIDTH), lambda b, i: (b, P16_AK), pipeline_mode=one),
            pl.BlockSpec((nq, A_WIDTH, tq), lambda b, i: (b, 0, 0), pipeline_mode=one),
            pl.BlockSpec((seq, LANES), lambda b, i: (b, P32_IK // LANES), pipeline_mode=one),
            pl.BlockSpec(bias_tiles.shape, lambda b, i: (0, 0, 0, 0), pipeline_mode=one),
        ],
        out_specs=pl.BlockSpec((tq, A_WIDTH), lambda b, i: (b * nq + i, 0)),
        out_shape=jax.ShapeDtypeStruct((n, A_WIDTH), BF16),
        scratch_shapes=[pltpu.VMEM((nq, tq, tq), jnp.int32),
                        pltpu.VMEM((nq, tq, tq), jnp.int16),
                        pltpu.VMEM((nq, tq, tq), jnp.int16),
                        pltpu.VMEM((nq, tq, tq), F32),
                        pltpu.VMEM((A_WIDTH, tq), BF16),
                        pltpu.VMEM((IDX_HEADS * LANES, tq), BF16),
                        pltpu.VMEM((LANES, tq), F32),
                        pltpu.VMEM((A_HEADS, 1, tq), F32),
                        pltpu.VMEM((A_HEADS, 1, tq), F32),
                        pltpu.VMEM((A_HEADS, A_HEAD_DIM, tq), F32),
                        pltpu.VMEM((A_HEADS, tq, tq), F32)],
        compiler_params=_cparams(("arbitrary", "arbitrary")),
        name="dsa_attention",
    )(p16, p32, p32, p16, vt, p32, bias_tiles)


def _t5_bucket(rel):
    max_exact = REL_BUCKETS // 2
    relf = jnp.maximum(rel, 1).astype(F32)
    large = max_exact + (jnp.log(relf / max_exact) / math.log(REL_MAX_DIST / max_exact)
                         * (REL_BUCKETS - max_exact)).astype(jnp.int32)
    large = jnp.minimum(large, REL_BUCKETS - 1)
    return jnp.where(rel < max_exact, rel, large)


def _bias_tiles(rel_bias, tq):
    assert tq >= REL_MAX_DIST
    key = jnp.arange(tq, dtype=jnp.int32)[:, None]
    qry = jnp.arange(tq, dtype=jnp.int32)[None, :]
    bucket = jnp.stack([_t5_bucket(jnp.maximum(lag * tq + qry - key, 0)) for lag in range(2)])
    rel = ((rel_bias - rel_bias[REL_BUCKETS - 1:REL_BUCKETS]) * LOG2_E).astype(F32)
    onehot = bucket[None] == jnp.arange(REL_BUCKETS, dtype=jnp.int32)[:, None, None, None]
    return jnp.sum(jnp.where(onehot[:, None], rel[:, :, None, None, None], 0.0), axis=0)


def _hgrn_kernel(q_ref, f_ref, i_ref, g_ref, lb_ref, ng_ref, tril_ref, o_ref, st_ref, *, rows):
    @pl.when(pl.program_id(1) == 0)
    def _():
        st_ref[...] = jnp.zeros_like(st_ref)

    lb = lb_ref[...]
    f = lb + (1.0 - lb) * jax.nn.sigmoid(f_ref[...])
    logf = jnp.log(f)
    kk = 1.0 - f
    g1 = logf.astype(BF16)
    r1 = logf - g1.astype(F32)
    g2 = r1.astype(BF16)
    g3 = (r1 - g2.astype(F32)).astype(BF16)
    tril = tril_ref[...]
    bcum = (jnp.dot(tril, g1, preferred_element_type=F32)
            + jnp.dot(tril, g2, preferred_element_type=F32)
            + jnp.dot(tril, g3, preferred_element_type=F32))

    srow = lax.broadcasted_iota(jnp.int32, (HGRN_C, B_HEAD_DIM), 0)
    trow = lax.broadcasted_iota(jnp.int32, (HGRN_SB, HGRN_C), 0)
    scol = lax.broadcasted_iota(jnp.int32, (HGRN_SB, HGRN_C), 1)
    ng = ng_ref[...]
    q = q_ref[...].astype(F32)
    qb_all = (q * jnp.exp(bcum)).astype(BF16)

    tiles = [(n, h) for n in range(rows // HGRN_C) for h in range(B_HEADS)]

    def rs(n):
        return slice(n * HGRN_C, (n + 1) * HGRN_C)

    def hs(h):
        return slice(h * B_HEAD_DIM, (h + 1) * B_HEAD_DIM)

    a_parts = {}
    for n, h in tiles:
        bc, qc, kc = bcum[rs(n), hs(h)], q[rs(n), hs(h)], kk[rs(n), hs(h)]
        for sb in range(HGRN_C // HGRN_SB):
            s0 = sb * HGRN_SB
            beta = bc[s0 - 1:s0] if sb > 0 else jnp.zeros((1, B_HEAD_DIM), F32)
            qs = (qc[s0:s0 + HGRN_SB] * jnp.exp(bc[s0:s0 + HGRN_SB] - beta)).astype(BF16)
            expo = jnp.where(srow < s0 + HGRN_SB, beta - bc, -jnp.inf)
            ks = (kc * jnp.exp(expo)).astype(BF16)
            a_parts[n, h, sb] = lax.dot_general(qs, ks, NT_DIMS, preferred_element_type=F32)
    intra, upd, dec = {}, {}, {}
    for n, h in tiles:
        a_rows = [jnp.where(scol <= trow + sb * HGRN_SB, a_parts[n, h, sb], 0.0)
                  for sb in range(HGRN_C // HGRN_SB)]
        attn = jnp.concatenate(a_rows, axis=0).astype(BF16)
        vc = i_ref[rs(n), hs(h)]
        intra[n, h] = jnp.dot(attn, vc, preferred_element_type=F32)
        bc = bcum[rs(n), hs(h)]
        blast = bc[HGRN_C - 1:HGRN_C]
        kdec = (kk[rs(n), hs(h)] * jnp.exp(blast - bc)).astype(BF16)
        upd[n, h] = lax.dot_general(vc, kdec, TN_DIMS, preferred_element_type=F32)
        dec[n, h] = jnp.exp(blast)
    for n, h in tiles:
        st = st_ref[h]
        o = intra[n, h] + lax.dot_general(qb_all[rs(n), hs(h)], st.astype(BF16), NT_DIMS,
                                          preferred_element_type=F32)
        st_ref[h] = st * dec[n, h] + upd[n, h]
        ms = jnp.mean(o * o, axis=-1, keepdims=True)
        on = o * lax.rsqrt(ms + EPS) * ng
        o_ref[rs(n), hs(h)] = (on * _silu(g_ref[rs(n), hs(h)].astype(F32))).astype(o_ref.dtype)


def _hgrn2(p16, p32, lb_l, norm_g, bsz, seq):
    rows = min(HGRN_L, seq)
    nj = seq // rows
    n = bsz * seq
    r = jnp.arange(rows, dtype=jnp.int32)
    tril = ((r[:, None] >= r[None, :]) & (r[:, None] // HGRN_C == r[None, :] // HGRN_C)).astype(BF16)

    def col(base):
        return lambda b, j: (b * nj + j, base)

    return pl.pallas_call(
        functools.partial(_hgrn_kernel, rows=rows),
        grid=(bsz, nj),
        in_specs=[
            pl.BlockSpec((rows, B_WIDTH), col(P16_BQ)),
            pl.BlockSpec((rows, B_WIDTH), col(P32_BF // B_WIDTH)),
            pl.BlockSpec((rows, B_WIDTH), col(P16_BI)),
            pl.BlockSpec((rows, B_WIDTH), col(P16_BG)),
            pl.BlockSpec((1, B_WIDTH), lambda b, j: (0, 0)),
            pl.BlockSpec((1, B_HEAD_DIM), lambda b, j: (0, 0)),
            pl.BlockSpec((rows, rows), lambda b, j: (0, 0)),
        ],
        out_specs=pl.BlockSpec((rows, B_WIDTH), lambda b, j: (b * nj + j, 0)),
        out_shape=jax.ShapeDtypeStruct((n, B_WIDTH), BF16),
        scratch_shapes=[pltpu.VMEM((B_HEADS, B_HEAD_DIM, B_HEAD_DIM), F32)],
        compiler_params=_cparams(("arbitrary", "arbitrary")),
        name="hgrn2",
    )(p16, p32, p16, p16, lb_l.reshape(1, B_WIDTH), norm_g.reshape(1, B_HEAD_DIM), tril)


def _ret_kernel(q_ref, k_ref, v_ref, g_ref, cos_ref, sin_ref, idec_ref, qdec_ref, kdec_ref, cdec_ref,
                o_ref, st_ref):
    @pl.when(pl.program_id(1) == 0)
    def _():
        st_ref[...] = jnp.zeros_like(st_ref)

    heads = range(C_HEADS)
    cos = jnp.concatenate([cos_ref[...]] * C_HEADS, axis=1)
    sin_signed = jnp.concatenate([sin_ref[...]] * C_HEADS, axis=1)
    even = lax.broadcasted_iota(jnp.int32, cos.shape, 1) % 2 == 0

    def rot(a):
        swapped = jnp.where(even, pltpu.roll(a, C_QK_WIDTH - 1, 1), pltpu.roll(a, 1, 1))
        return a * cos + swapped * sin_signed

    qr = rot(q_ref[...].astype(F32))
    kr = rot(k_ref[...].astype(F32))
    qk = [slice(h * C_QK_DIM, (h + 1) * C_QK_DIM) for h in heads]
    vs = [slice(h * C_V_DIM, (h + 1) * C_V_DIM) for h in heads]
    attn = [lax.dot_general(qr[:, qk[h]].astype(BF16), kr[:, qk[h]].astype(BF16), NT_DIMS,
                            preferred_element_type=F32) * idec_ref[h] for h in heads]
    inter = [jnp.dot((qr[:, qk[h]] * qdec_ref[h]).astype(BF16), st_ref[h].astype(BF16),
                     preferred_element_type=F32) for h in heads]
    intra = [jnp.dot(attn[h].astype(BF16), v_ref[:, vs[h]], preferred_element_type=F32) for h in heads]
    upd = [jnp.dot((kr[:, qk[h]] * kdec_ref[h]).T.astype(BF16), v_ref[:, vs[h]],
                   preferred_element_type=F32) for h in heads]
    for h in heads:
        st_ref[h] = cdec_ref[h, 0:1, :] * st_ref[h] + upd[h]
        o = intra[h] + inter[h]
        ms = jnp.mean(o * o, axis=-1, keepdims=True)
        o_ref[:, vs[h]] = (_silu(g_ref[:, vs[h]].astype(F32)) * (o * lax.rsqrt(ms + EPS))).astype(o_ref.dtype)


def _retention_tables(seq):
    pos = jnp.arange(seq, dtype=F32)
    theta = jnp.repeat(1.0 / (10000.0 ** jnp.linspace(0.0, 1.0, C_QK_DIM // 2)), 2)
    ang = pos[:, None] * theta[None, :]
    pair_sign = jnp.where(jnp.arange(C_QK_DIM) % 2 == 0, -1.0, 1.0)
    log_gamma = jnp.log(1.0 - 2.0 ** (-5.0 - jnp.arange(C_HEADS, dtype=F32)))
    idx = jnp.arange(RET_C, dtype=F32)
    causal = idx[:, None] >= idx[None, :]
    idec = jnp.exp(jnp.where(causal[None], (idx[:, None] - idx[None, :])[None] * log_gamma[:, None, None],
                             -jnp.inf))
    qdec = jnp.exp((idx + 1.0)[None, :] * log_gamma[:, None])[..., None]
    kdec = jnp.exp((RET_C - 1.0 - idx)[None, :] * log_gamma[:, None])[..., None]
    cdec = jnp.exp(RET_C * log_gamma)[:, None, None]
    return (jnp.cos(ang), jnp.sin(ang) * pair_sign[None, :], idec,
            jnp.broadcast_to(qdec, (C_HEADS, RET_C, C_QK_DIM)),
            jnp.broadcast_to(kdec, (C_HEADS, RET_C, C_QK_DIM)),
            jnp.broadcast_to(cdec, (C_HEADS, 8, C_V_DIM)))


def _retention(p16, tables, bsz, seq):
    cos, sin, idec, qdec, kdec, cdec = tables
    nj = seq // RET_C
    n = bsz * seq
    v_blk = C_V_WIDTH // 1024

    def whole(a):
        return pl.BlockSpec(a.shape, lambda b, j: (0,) * a.ndim)

    return pl.pallas_call(
        _ret_kernel,
        grid=(bsz, nj),
        in_specs=[
            pl.BlockSpec((RET_C, C_QK_WIDTH), lambda b, j: (b * nj + j, P16_CQ)),
            pl.BlockSpec((RET_C, C_QK_WIDTH), lambda b, j: (b * nj + j, P16_CK)),
            pl.BlockSpec((RET_C, C_V_WIDTH), lambda b, j: (b * nj + j, P16_CV // v_blk)),
            pl.BlockSpec((RET_C, C_V_WIDTH), lambda b, j: (b * nj + j, P16_CG // v_blk)),
            pl.BlockSpec((RET_C, C_QK_DIM), lambda b, j: (j, 0)),
            pl.BlockSpec((RET_C, C_QK_DIM), lambda b, j: (j, 0)),
            whole(idec), whole(qdec), whole(kdec), whole(cdec),
        ],
        out_specs=pl.BlockSpec((RET_C, C_V_WIDTH), lambda b, j: (b * nj + j, 0)),
        out_shape=jax.ShapeDtypeStruct((n, C_V_WIDTH), BF16),
        scratch_shapes=[pltpu.VMEM((C_HEADS, C_QK_DIM, C_V_DIM), F32)],
        compiler_params=_cparams(("arbitrary", "arbitrary")),
        name="retention",
    )(p16, p16, p16, p16, cos, sin, idec, qdec, kdec, cdec)


def _merge_kernel(oa_ref, ob_ref, oc_ref, ga_ref, gb_ref, gc_ref, x_ref, mod_ref, g2_ref,
                  wa_ref, wb_ref, wc_ref, wo_ref, wrh_ref, wrl_ref, br_ref,
                  x1_ref, h2_ref, route_ref, *, tiles_per_batch):
    b = pl.program_id(0) // tiles_per_batch

    def gated(o_ref, w_ref, g_ref):
        y = jnp.dot(o_ref[...], w_ref[...], preferred_element_type=F32)
        return jax.nn.sigmoid(g_ref[...].astype(F32)) * y

    merged = gated(oa_ref, wa_ref, ga_ref) + gated(ob_ref, wb_ref, gb_ref) + gated(oc_ref, wc_ref, gc_ref)
    y = jnp.dot(merged.astype(BF16), wo_ref[...], preferred_element_type=F32)
    gt1 = mod_ref[pl.ds(b, 1), 2 * D_MODEL:3 * D_MODEL]
    x1 = x_ref[...] + gt1 * y
    x1_ref[...] = x1
    sh2 = mod_ref[pl.ds(b, 1), 3 * D_MODEL:4 * D_MODEL]
    sc2 = mod_ref[pl.ds(b, 1), 4 * D_MODEL:5 * D_MODEL]
    h2 = _rms_mod(x1, g2_ref[...], sc2, sh2)
    h_hi = h2.astype(BF16)
    words = _pack_bf16_pairs(h2)
    for j in range(h2_ref.shape[0]):
        h2_ref[j] = words[:, j * SC_SUB:(j + 1) * SC_SUB]
    h_lo = (h2 - h_hi.astype(F32)).astype(BF16)
    logits = (jnp.dot(h_hi, wrh_ref[...], preferred_element_type=F32)
              + jnp.dot(h_lo, wrh_ref[...], preferred_element_type=F32)
              + jnp.dot(h_hi, wrl_ref[...], preferred_element_type=F32)) + br_ref[...]
    lane = lax.broadcasted_iota(jnp.int32, logits.shape, 1).astype(F32)
    neg_inf = -jnp.inf

    def first_argmax(vals):
        top = jnp.max(vals, axis=-1, keepdims=True)
        idx = jnp.min(jnp.where(vals == top, lane, float(LANES)), axis=-1, keepdims=True)
        return top, idx

    gl = jnp.where(lane < N_GROUPS, logits, neg_inf)
    gmax, gsel = first_argmax(gl)
    gprob = 1.0 / jnp.sum(jnp.exp(gl - gmax), axis=-1, keepdims=True)
    lo = N_GROUPS + EXPERTS_PER_GROUP * gsel
    el = jnp.where((lane >= lo) & (lane < lo + EXPERTS_PER_GROUP), logits, neg_inf)
    v1, i1 = first_argmax(el)
    el2 = jnp.where(lane == i1, neg_inf, el)
    v2, i2 = first_argmax(el2)
    e2 = jnp.exp(v2 - v1)
    den = 1.0 + e2
    route_ref[...] = jnp.where(lane == 0.0, i1 - N_GROUPS,
                               jnp.where(lane == 1.0, i2 - N_GROUPS,
                                         jnp.where(lane == 2.0, gprob / den,
                                                   jnp.where(lane == 3.0, gprob * (e2 / den), 0.0))))


def _merge(o_a, o_b, o_c, p16, x2, mod_l, g2, wa, wb, wc, wo, wr_hi, wr_lo, br, seq):
    n = x2.shape[0]
    tm = min(512, seq)
    one = pl.Buffered(1)

    def rows(width, cb=0):
        return pl.BlockSpec((tm, width), lambda i: (i, cb))

    def whole(a):
        return pl.BlockSpec(a.shape, lambda i: (0,) * a.ndim, pipeline_mode=one)

    return pl.pallas_call(
        functools.partial(_merge_kernel, tiles_per_batch=seq // tm),
        grid=(n // tm,),
        in_specs=[rows(A_WIDTH), rows(B_WIDTH), rows(C_V_WIDTH),
                  rows(D_MODEL, P16_GA), rows(D_MODEL, P16_GB), rows(D_MODEL, P16_GC),
                  rows(D_MODEL), whole(mod_l), pl.BlockSpec((1, D_MODEL), lambda i: (0, 0)),
                  whole(wa), whole(wb), whole(wc), whole(wo), whole(wr_hi), whole(wr_lo), whole(br)],
        out_specs=[rows(D_MODEL), pl.BlockSpec((SC_PIECES, tm, SC_SUB), lambda i: (0, i, 0)), rows(LANES)],
        out_shape=[jax.ShapeDtypeStruct((n, D_MODEL), F32),
                   jax.ShapeDtypeStruct((SC_PIECES, n, SC_SUB), jnp.int32),
                   jax.ShapeDtypeStruct((n, LANES), F32)],
        compiler_params=_cparams(("arbitrary",)),
        name="merge_route",
    )(o_a, o_b, o_c, p16, p16, p16, x2, mod_l, g2.reshape(1, D_MODEL), wa, wb, wc, wo, wr_hi, wr_lo, br)


def _lane_pick(vals, lane, idx):
    return jnp.sum(jnp.where(lane == idx, vals, 0.0), axis=-1, keepdims=True)


def _rank_kernel(route_ref, rk_ref, cnt_ref, run_ref):
    @pl.when(pl.program_id(0) == 0)
    def _():
        run_ref[...] = jnp.zeros_like(run_ref)

    route = route_ref[...]
    tb = route.shape[0]
    lane = lax.broadcasted_iota(jnp.int32, route.shape, 1).astype(F32)
    e1 = _lane_pick(route, lane, 0.0)
    e2 = _lane_pick(route, lane, 1.0)
    sel = jnp.where((lane == e1) | (lane == e2), 1.0, 0.0)
    r = lax.broadcasted_iota(jnp.int32, (tb, tb), 0)
    c = lax.broadcasted_iota(jnp.int32, (tb, tb), 1)
    before = jnp.where(c < r, 1.0, 0.0).astype(BF16)
    rank = jnp.dot(before, sel.astype(BF16), preferred_element_type=F32) + run_ref[0:1, :]
    rk_ref[...] = jnp.where(lane == 0.0, _lane_pick(rank, lane, e1),
                            jnp.where(lane == 1.0, _lane_pick(rank, lane, e2), 0.0))
    run_ref[...] = run_ref[...] + jnp.sum(sel, axis=0, keepdims=True)
    cnt_ref[...] = run_ref[...]


def _expert_ranks(route):
    n = route.shape[0]
    tb = min(256, n)
    return pl.pallas_call(
        _rank_kernel,
        grid=(n // tb,),
        in_specs=[pl.BlockSpec((tb, LANES), lambda i: (i, 0))],
        out_specs=[pl.BlockSpec((tb, LANES), lambda i: (i, 0)),
                   pl.BlockSpec((8, LANES), lambda i: (0, 0))],
        out_shape=[jax.ShapeDtypeStruct((n, LANES), F32), jax.ShapeDtypeStruct((8, LANES), F32)],
        scratch_shapes=[pltpu.VMEM((8, LANES), F32)],
        compiler_params=_cparams(("arbitrary",)),
        name="expert_ranks",
    )(route)


def _plan_kernel(cnt_ref, route_ref, rk_ref, pos_ref, tmap_ref):
    lane_i = lax.broadcasted_iota(jnp.int32, (8, LANES), 1)
    cnt = jnp.where(lane_i < N_EXPERTS, cnt_ref[...], 0.0)
    padded = jnp.floor((cnt + (MOE_TM - 1)) * (1.0 / MOE_TM)) * MOE_TM
    r = lax.broadcasted_iota(jnp.int32, (LANES, LANES), 0)
    c = lax.broadcasted_iota(jnp.int32, (LANES, LANES), 1)
    base = jnp.dot(padded, jnp.where(r < c, 1.0, 0.0), precision=lax.Precision.HIGHEST,
                   preferred_element_type=F32)

    route = route_ref[...]
    lane = lax.broadcasted_iota(jnp.int32, route.shape, 1).astype(F32)
    rk = rk_ref[...]
    base_row = base[0:1, :]
    pos1 = _lane_pick(base_row, lane, _lane_pick(route, lane, 0.0)) + _lane_pick(rk, lane, 0.0)
    pos2 = _lane_pick(base_row, lane, _lane_pick(route, lane, 1.0)) + _lane_pick(rk, lane, 1.0)
    pos_ref[...] = jnp.where(lane == 0.0, pos1, jnp.where(lane == 1.0, pos2, 0.0)).astype(jnp.int32)

    @pl.when(pl.program_id(0) == 0)
    def _():
        nt = tmap_ref.shape[0]
        tlane = lax.broadcasted_iota(jnp.int32, (nt, LANES), 1)
        start = (lax.broadcasted_iota(jnp.int32, (nt, LANES), 0) * MOE_TM).astype(F32)
        end_row = jnp.where(tlane < N_EXPERTS, base_row + padded[0:1, :], 3e38)
        expert = jnp.sum(jnp.where(end_row <= start, 1.0, 0.0), axis=-1, keepdims=True)
        expert_c = jnp.minimum(expert, N_EXPERTS - 1.0)
        tl = tlane.astype(F32)
        left = _lane_pick(cnt[0:1, :], tl, expert_c) - (start[:, 0:1] - _lane_pick(base_row, tl, expert_c))
        valid = jnp.where(expert < N_EXPERTS, jnp.clip(left, 0.0, float(MOE_TM)), 0.0)
        tmap_ref[...] = jnp.where(tlane == 0, expert_c, jnp.where(tlane == 1, valid, 0.0)).astype(jnp.int32)


def _expert_plan(cnt, route, rk, n_tiles):
    n = route.shape[0]
    tb = min(256, n)
    nt_pad = -(-n_tiles // 8) * 8
    return pl.pallas_call(
        _plan_kernel,
        grid=(n // tb,),
        in_specs=[pl.BlockSpec((8, LANES), lambda i: (0, 0)),
                  pl.BlockSpec((tb, LANES), lambda i: (i, 0)),
                  pl.BlockSpec((tb, LANES), lambda i: (i, 0))],
        out_specs=[pl.BlockSpec((tb, LANES), lambda i: (i, 0)),
                   pl.BlockSpec((nt_pad, LANES), lambda i: (0, 0))],
        out_shape=[jax.ShapeDtypeStruct((n, LANES), jnp.int32),
                   jax.ShapeDtypeStruct((nt_pad, LANES), jnp.int32)],
        compiler_params=_cparams(("arbitrary",)),
        name="expert_plan",
    )(cnt, route, rk)


def _sc_mesh():
    return plsc.VectorSubcoreMesh(core_axis_name="c", subcore_axis_name="s")


def _sc_scatter_rows(src, idx, out_rows):
    m = idx.shape[0]
    n_src_win = src.shape[0] // SC_WINDOW

    @functools.partial(pl.kernel, out_type=jax.ShapeDtypeStruct((out_rows, src.shape[1]), src.dtype),
                       mesh=_sc_mesh(), scratch_types=[])
    def scatter(x_hbm, i_hbm, o_hbm):
        def body(x_vmem, i_vmem):
            pltpu.sync_copy(x_vmem, o_hbm.at[i_vmem.at[0]])

        pltpu.emit_pipeline(
            body, grid=(m // SC_WINDOW,),
            in_specs=[pl.BlockSpec((SC_WINDOW, src.shape[1]), lambda i: (i % n_src_win, 0)),
                      pl.BlockSpec((1, SC_WINDOW), lambda i: (0, i))],
            out_specs=[], core_axis_name=("c", "s"),
            dimension_semantics=(pltpu.PARALLEL,))(x_hbm, i_hbm)

    return scatter(src, idx.reshape(1, m))


def _sc_gather_rows(table, idx):
    m = idx.shape[0]

    @functools.partial(pl.kernel, out_type=jax.ShapeDtypeStruct((m, table.shape[1]), table.dtype),
                       mesh=_sc_mesh(), scratch_types=[])
    def gather(x_hbm, i_hbm, o_hbm):
        def body(i_vmem, o_vmem):
            pltpu.sync_copy(x_hbm.at[i_vmem.at[0]], o_vmem)

        pltpu.emit_pipeline(
            body, grid=(m // SC_WINDOW,),
            in_specs=[pl.BlockSpec((1, SC_WINDOW), lambda i: (0, i))],
            out_specs=[pl.BlockSpec((SC_WINDOW, table.shape[1]), lambda i: (i, 0))],
            core_axis_name=("c", "s"),
            dimension_semantics=(pltpu.PARALLEL,))(i_hbm, o_hbm)

    return gather(table, idx.reshape(1, m))


def _piece_row_index(pos, rows):
    return (jnp.arange(SC_PIECES, dtype=jnp.int32)[:, None] * rows + pos[None, :]).reshape(-1)


def _grouped_kernel(te_ref, tv_ref, x_ref, wg_ref, wu_ref, wd_ref, o_ref):
    valid = tv_ref[pl.program_id(0)]

    @pl.when(valid > 0)
    def _():
        words = jnp.concatenate([x_ref[j] for j in range(SC_PIECES)], axis=1)
        row = lax.broadcasted_iota(jnp.int32, words.shape, 0)
        words = jnp.where(row < valid, words, 0)
        x = _unpack_bf16_pairs(words).astype(BF16)
        a = jnp.dot(x, wg_ref[0].astype(BF16), preferred_element_type=F32)
        u = jnp.dot(x, wu_ref[0].astype(BF16), preferred_element_type=F32)
        hm = (_silu(a) * u).astype(BF16)
        out = _pack_bf16_pairs(jnp.dot(hm, wd_ref[0].astype(BF16), preferred_element_type=F32))
        for j in range(SC_PIECES):
            o_ref[j] = out[:, j * SC_SUB:(j + 1) * SC_SUB]

    @pl.when(valid <= 0)
    def _():
        o_ref[...] = jnp.zeros_like(o_ref)


def _grouped_experts(tile_expert, tile_valid, xs, wg, wu, wd, layer):
    n_tiles = tile_expert.shape[0]
    rows_block = pl.BlockSpec((SC_PIECES, MOE_TM, SC_SUB), lambda i, te, tv: (0, i, 0))
    return pl.pallas_call(
        _grouped_kernel,
        grid_spec=pltpu.PrefetchScalarGridSpec(
            num_scalar_prefetch=2,
            grid=(n_tiles,),
            in_specs=[rows_block,
                      pl.BlockSpec((None, 1, D_MODEL, D_EXPERT), lambda i, te, tv: (layer, te[i], 0, 0)),
                      pl.BlockSpec((None, 1, D_MODEL, D_EXPERT), lambda i, te, tv: (layer, te[i], 0, 0)),
                      pl.BlockSpec((None, 1, D_EXPERT, D_MODEL), lambda i, te, tv: (layer, te[i], 0, 0))],
            out_specs=rows_block),
        out_shape=jax.ShapeDtypeStruct(xs.shape, jnp.int32),
        compiler_params=_cparams(("arbitrary",)),
        name="grouped_experts",
    )(tile_expert, tile_valid, xs, wg, wu, wd)


def _combine_kernel(x1_ref, y_ref, route_ref, mod_ref, o_ref, *, tiles_per_batch):
    b = pl.program_id(0) // tiles_per_batch
    gt2 = mod_ref[pl.ds(b, 1), 5 * D_MODEL:6 * D_MODEL]
    route = route_ref[...]
    lane = lax.broadcasted_iota(jnp.int32, route.shape, 1).astype(F32)
    w1 = _lane_pick(route, lane, 2.0)
    w2 = _lane_pick(route, lane, 3.0)
    y1 = _unpack_bf16_pairs(jnp.concatenate([y_ref[0, j] for j in range(SC_PIECES)], axis=1))
    y2 = _unpack_bf16_pairs(jnp.concatenate([y_ref[1, j] for j in range(SC_PIECES)], axis=1))
    o_ref[...] = x1_ref[...] + gt2 * (w1 * y1 + w2 * y2)


def _combine(x1, y2, route, mod_l, seq):
    n = x1.shape[0]
    tm = min(1024, seq)
    return pl.pallas_call(
        functools.partial(_combine_kernel, tiles_per_batch=seq // tm),
        grid=(n // tm,),
        in_specs=[pl.BlockSpec((tm, D_MODEL), lambda i: (i, 0)),
                  pl.BlockSpec((2, SC_PIECES, tm, SC_SUB), lambda i: (0, 0, i, 0)),
                  pl.BlockSpec((tm, LANES), lambda i: (i, 0)),
                  pl.BlockSpec(mod_l.shape, lambda i: (0, 0))],
        out_specs=pl.BlockSpec((tm, D_MODEL), lambda i: (i, 0)),
        out_shape=jax.ShapeDtypeStruct((n, D_MODEL), F32),
        compiler_params=_cparams(("arbitrary",)),
        name="moe_combine",
    )(x1, y2, route, mod_l)


def _moe(h2, route, x1, mod_l, wg, wu, wd, layer, seq):
    n = h2.shape[1]
    n_tiles = (2 * n) // MOE_TM + N_EXPERTS
    rows = n_tiles * MOE_TM
    rk, cnt = _expert_ranks(route)
    pos, tmap = _expert_plan(cnt, route, rk, n_tiles)
    idx = jnp.concatenate([_piece_row_index(pos[:, 0], rows), _piece_row_index(pos[:, 1], rows)])
    xs = _sc_scatter_rows(h2.reshape(SC_PIECES * n, SC_SUB), idx, SC_PIECES * rows)
    ys = _grouped_experts(tmap[:n_tiles, 0], tmap[:n_tiles, 1], xs.reshape(SC_PIECES, rows, SC_SUB),
                          wg, wu, wd, layer)
    y2 = _sc_gather_rows(ys.reshape(SC_PIECES * rows, SC_SUB), idx).reshape(2, SC_PIECES, n, SC_SUB)
    return _combine(x1, y2, route, mod_l, seq)


def _final_norm_kernel(x_ref, g_ref, o_ref):
    x = x_ref[...]
    ms = jnp.mean(x * x, axis=-1, keepdims=True)
    o_ref[...] = x * lax.rsqrt(ms + EPS) * g_ref[...]


def _final_norm(x2, g, seq):
    n = x2.shape[0]
    tm = min(1024, seq)
    return pl.pallas_call(
        _final_norm_kernel,
        grid=(n // tm,),
        in_specs=[pl.BlockSpec((tm, D_MODEL), lambda i: (i, 0)),
                  pl.BlockSpec((1, D_MODEL), lambda i: (0, 0))],
        out_specs=pl.BlockSpec((tm, D_MODEL), lambda i: (i, 0)),
        out_shape=jax.ShapeDtypeStruct((n, D_MODEL), F32),
        compiler_params=_cparams(("arbitrary",)),
        name="final_norm",
    )(x2, g.reshape(1, D_MODEL))


_IN_OFFS = [sum(IN_SPLITS[:i]) for i in range(len(IN_SPLITS) + 1)]
(_AQ, _AK, _AV, _IQ, _IK, _IW, _BQ, _BF, _BI, _BG, _CQ, _CK, _CV, _CG, _GA, _GB, _GC) = range(len(IN_SPLITS))


def _pack_kernel(w_ref, w16_ref, w32_ref, wvt_ref):
    def cols(seg):
        return w_ref[0, :, _IN_OFFS[seg]:_IN_OFFS[seg + 1]]

    scale = {_AQ: A_HEAD_DIM ** -0.5 * LOG2_E, _CK: C_QK_DIM ** -0.5}
    at = 0
    for seg in (_CV, _CG, _AQ, _AK, _BQ, _BI, _BG, _CQ, _CK, _GA, _GB, _GC):
        v = cols(seg)
        if seg in scale:
            v = v * scale[seg]
        w16_ref[0, :, at:at + v.shape[1]] = v.astype(BF16)
        at += v.shape[1]

    rows = w_ref.shape[1]
    w32_ref[0, :, P32_BF:P32_BF + B_WIDTH] = cols(_BF).astype(BF16)
    iq = cols(_IQ)
    zeros = jnp.zeros((rows, LANES - IDX_DIM), F32)
    for h in range(IDX_HEADS):
        w32_ref[0, :, P32_IQ + h * LANES:P32_IQ + (h + 1) * LANES] = jnp.concatenate(
            [iq[:, h * IDX_DIM:(h + 1) * IDX_DIM], zeros], axis=1).astype(BF16)
    w32_ref[0, :, P32_IK:P32_IK + LANES] = jnp.concatenate([cols(_IK), zeros], axis=1).astype(BF16)
    w32_ref[0, :, P32_IW:P32_IW + LANES] = jnp.concatenate(
        [cols(_IW), jnp.zeros((rows, LANES - IDX_HEADS), F32)], axis=1).astype(BF16)
    wvt_ref[0] = cols(_AV).T.astype(BF16)


def _pack_w_in(w_in):
    depth, d, width = w_in.shape
    rows = LANES
    w16_width = 14 * 1024
    w32_width = P32_IW + LANES
    return pl.pallas_call(
        _pack_kernel,
        grid=(depth, d // rows),
        in_specs=[pl.BlockSpec((1, rows, width), lambda l, r: (l, r, 0))],
        out_specs=[pl.BlockSpec((1, rows, w16_width), lambda l, r: (l, r, 0)),
                   pl.BlockSpec((1, rows, w32_width), lambda l, r: (l, r, 0)),
                   pl.BlockSpec((1, A_WIDTH, rows), lambda l, r: (l, 0, r))],
        out_shape=[jax.ShapeDtypeStruct((depth, d, w16_width), BF16),
                   jax.ShapeDtypeStruct((depth, d, w32_width), BF16),
                   jax.ShapeDtypeStruct((depth, A_WIDTH, d), BF16)],
        compiler_params=_cparams(("arbitrary", "arbitrary")),
        name="pack_w_in",
    )(w_in)


def _split_bf16(w):
    hi = w.astype(BF16)
    return hi, (w - hi.astype(F32)).astype(BF16)


def kernel(x, c, rel_bias, hgrn_lb_raw, norm1_g, norm2_g, ada_w, ada_b, w_in, hgrn_norm_g, w_branch_a,
           w_branch_b, w_branch_c, w_out, router_group_w, router_group_b, router_expert_w,
           router_expert_b, expert_w_gate, expert_w_up, expert_w_down, final_norm_g):
    bsz, seq, _ = x.shape
    depth = w_in.shape[0]
    n = bsz * seq
    x2 = x.reshape(n, D_MODEL)
    tq = min(DSA_TQ, seq)

    lb_all = _hgrn_lower_bounds(hgrn_lb_raw)
    c_pad = jnp.pad(c, ((0, (-bsz) % 8), (0, 0)))
    mod = _ada_mod(c_pad, ada_w, ada_b)
    bias_tiles = _bias_tiles(rel_bias, tq)
    ret_tables = _retention_tables(seq)
    w16_all, w32_all, wvt_all = _pack_w_in(w_in)

    for l in range(depth):
        p16 = _norm_project(x2, mod[l], norm1_g[l], w16_all, l, BF16, 1024, seq, "proj_bf16")
        p32 = _norm_project(x2, mod[l], norm1_g[l], w32_all, l, F32, 768, seq, "proj_f32")
        vt = _norm_project_t(x2, mod[l], norm1_g[l], wvt_all, l, tq, seq, "proj_vt")
        o_a = _dsa_attention(p16, p32, vt, bias_tiles, bsz, seq)
        o_b = _hgrn2(p16, p32, lb_all[l], hgrn_norm_g[l], bsz, seq)
        o_c = _retention(p16, ret_tables, bsz, seq)
        wr = jnp.concatenate([router_group_w[l], router_expert_w[l],
                              jnp.zeros((D_MODEL, LANES - N_GROUPS - N_EXPERTS), F32)], axis=1)
        br = jnp.concatenate([router_group_b[l], router_expert_b[l],
                              jnp.zeros((LANES - N_GROUPS - N_EXPERTS,), F32)]).reshape(1, LANES)
        wr_hi, wr_lo = _split_bf16(wr)
        x1, h2, route = _merge(o_a, o_b, o_c, p16, x2, mod[l], norm2_g[l],
                               w_branch_a[l].astype(BF16), w_branch_b[l].astype(BF16),
                               w_branch_c[l].astype(BF16), w_out[l].astype(BF16),
                               wr_hi, wr_lo, br, seq)
        x2 = _moe(h2, route, x1, mod[l], expert_w_gate, expert_w_up, expert_w_down, l, seq)

    return _final_norm(x2, final_norm_g, seq).reshape(bsz, seq, D_MODEL)
```

```python
import functools
import math

import jax
import jax.numpy as jnp
from jax import lax
from jax.experimental import pallas as pl
from jax.experimental.pallas import tpu as pltpu
from jax.experimental.pallas import tpu_sc as plsc

F32 = jnp.float32
BF16 = jnp.bfloat16

D_MODEL = 1024
A_HEADS = 8
A_HEAD_DIM = 128
IDX_HEADS = 8
IDX_DIM = 64
TOPK_MAX = 256
REL_BUCKETS = 32
REL_MAX_DIST = 128
B_HEADS = 8
B_HEAD_DIM = 128
C_HEADS = 4
C_QK_DIM = 256
C_V_DIM = 512
N_GROUPS = 4
EXPERTS_PER_GROUP = 8
N_EXPERTS = 32
D_EXPERT = 512
EPS = 1e-6

A_WIDTH = A_HEADS * A_HEAD_DIM
B_WIDTH = B_HEADS * B_HEAD_DIM
C_QK_WIDTH = C_HEADS * C_QK_DIM
C_V_WIDTH = C_HEADS * C_V_DIM
IN_SPLITS = (A_WIDTH, A_WIDTH, A_WIDTH, IDX_HEADS * IDX_DIM, IDX_DIM, IDX_HEADS,
             B_WIDTH, B_WIDTH, B_WIDTH, B_WIDTH,
             C_QK_WIDTH, C_QK_WIDTH, C_V_WIDTH, C_V_WIDTH,
             D_MODEL, D_MODEL, D_MODEL)

LANES = 128
BF16_ROWS = 16
VMEM_LIMIT = 56 * 1024 * 1024

P16_CV, P16_CG = 0, 2
P16_AQ, P16_AK, P16_BQ, P16_BI, P16_BG, P16_CQ, P16_CK, P16_GA, P16_GB, P16_GC = range(4, 14)
P32_BF = 0
P32_IQ = 1024
P32_IK = 2048
P32_IW = 2176

DSA_TQ = 256
HGRN_L = 256
HGRN_C = 64
HGRN_SB = 16
RET_C = 256
KEY_NEG_INF = -2139095041
HALF_BIAS = 32768
MASK_NEG = -1e30
LOG2_E = math.log2(math.e)
COUNT_CHAINS = 4
MOE_TM = 256
PLAN_TB = 1024
SC_WINDOW = 128
SC_SUB = 256
SC_PIECES = D_MODEL // 2 // SC_SUB

NT_DIMS = (((1,), (1,)), ((), ()))
TN_DIMS = (((0,), (0,)), ((), ()))


def _cparams(sem):
    return pltpu.CompilerParams(dimension_semantics=sem, vmem_limit_bytes=VMEM_LIMIT)


def _silu(x):
    return x * jax.nn.sigmoid(x)


def _pack_bf16_pairs(x):
    k = x.shape[1] // 2
    bits = pltpu.bitcast(x.astype(BF16).astype(F32), jnp.int32)
    return (bits[:, :k] & jnp.int32(-65536)) | lax.shift_right_logical(bits[:, k:], 16)


def _unpack_bf16_pairs(words):
    hi = pltpu.bitcast(words & jnp.int32(-65536), F32)
    lo = pltpu.bitcast(lax.shift_left(words, 16), F32)
    return jnp.concatenate([hi, lo], axis=1)


def _lb_kernel(raw_ref, o_ref):
    raw = raw_ref[...]
    m = jnp.max(raw, axis=0, keepdims=True)
    e = jnp.exp(raw - m)
    soft = e / jnp.sum(e, axis=0, keepdims=True)
    run = jnp.zeros_like(soft[0:1])
    for l in range(raw.shape[0]):
        run = run + soft[l:l + 1]
        o_ref[l:l + 1, :] = run - soft[0:1]


def _hgrn_lower_bounds(raw):
    return pl.pallas_call(
        _lb_kernel, out_shape=jax.ShapeDtypeStruct(raw.shape, F32), name="hgrn_lb")(raw)


def _ada_kernel(c_ref, w_ref, b_ref, o_ref):
    a = _silu(c_ref[...])
    o_ref[0] = jnp.dot(a, w_ref[0], precision=lax.Precision.HIGHEST,
                       preferred_element_type=F32) + b_ref[0]


def _ada_mod(c_pad, ada_w, ada_b):
    depth = ada_w.shape[0]
    rows = c_pad.shape[0]
    return pl.pallas_call(
        _ada_kernel,
        grid=(depth, 6),
        in_specs=[pl.BlockSpec((rows, D_MODEL), lambda l, j: (0, 0)),
                  pl.BlockSpec((1, D_MODEL, D_MODEL), lambda l, j: (l, 0, j)),
                  pl.BlockSpec((1, 1, D_MODEL), lambda l, j: (l, 0, j))],
        out_specs=pl.BlockSpec((1, rows, D_MODEL), lambda l, j: (l, 0, j)),
        out_shape=jax.ShapeDtypeStruct((depth, rows, 6 * D_MODEL), F32),
        compiler_params=_cparams(("arbitrary", "arbitrary")),
        name="ada_mod",
    )(c_pad, ada_w, ada_b.reshape(depth, 1, 6 * D_MODEL))


def _rms_mod(x, g, sc, sh):
    ms = jnp.mean(x * x, axis=-1, keepdims=True)
    return (x * lax.rsqrt(ms + EPS) * g) * (1.0 + sc) + sh


def _norm1(x_ref, mod_ref, g_ref, b):
    sh = mod_ref[pl.ds(b, 1), 0:D_MODEL]
    sc = mod_ref[pl.ds(b, 1), D_MODEL:2 * D_MODEL]
    return _rms_mod(x_ref[...], g_ref[...], sc, sh).astype(BF16)


def _proj_kernel(x_ref, mod_ref, g_ref, w_ref, o_ref, h_ref, *, tiles_per_batch):
    @pl.when(pl.program_id(1) == 0)
    def _():
        h_ref[...] = _norm1(x_ref, mod_ref, g_ref, pl.program_id(0) // tiles_per_batch)

    o_ref[...] = jnp.dot(h_ref[...], w_ref[...], preferred_element_type=F32).astype(o_ref.dtype)


def _norm_project(x2, mod_l, g, w_all, layer, out_dtype, tn, seq, name):
    n = x2.shape[0]
    width = w_all.shape[2]
    tm = min(1024, seq)
    return pl.pallas_call(
        functools.partial(_proj_kernel, tiles_per_batch=seq // tm),
        grid=(n // tm, width // tn),
        in_specs=[pl.BlockSpec((tm, D_MODEL), lambda i, j: (i, 0)),
                  pl.BlockSpec(mod_l.shape, lambda i, j: (0, 0)),
                  pl.BlockSpec((1, D_MODEL), lambda i, j: (0, 0)),
                  pl.BlockSpec((None, D_MODEL, tn), lambda i, j: (layer, 0, j))],
        out_specs=pl.BlockSpec((tm, tn), lambda i, j: (i, j)),
        out_shape=jax.ShapeDtypeStruct((n, width), out_dtype),
        scratch_shapes=[pltpu.VMEM((tm, D_MODEL), BF16)],
        compiler_params=_cparams(("arbitrary", "arbitrary")),
        name=name,
    )(x2, mod_l, g.reshape(1, D_MODEL), w_all)


def _proj_t_kernel(x_ref, mod_ref, g_ref, wt_ref, o_ref, *, tiles_per_batch, chunk):
    h = _norm1(x_ref, mod_ref, g_ref, pl.program_id(0) // tiles_per_batch)
    res = lax.dot_general(wt_ref[...], h, NT_DIMS, preferred_element_type=F32)
    for ci in range(o_ref.shape[0]):
        o_ref[ci] = res[:, ci * chunk:(ci + 1) * chunk].astype(o_ref.dtype)


def _norm_project_t(x2, mod_l, g, wt_all, layer, chunk, seq, name):
    n = x2.shape[0]
    cols = wt_all.shape[1]
    tm = min(1024, seq)
    return pl.pallas_call(
        functools.partial(_proj_t_kernel, tiles_per_batch=seq // tm, chunk=chunk),
        grid=(n // tm,),
        in_specs=[pl.BlockSpec((tm, D_MODEL), lambda i: (i, 0)),
                  pl.BlockSpec(mod_l.shape, lambda i: (0, 0)),
                  pl.BlockSpec((1, D_MODEL), lambda i: (0, 0)),
                  pl.BlockSpec((None, cols, D_MODEL), lambda i: (layer, 0, 0))],
        out_specs=pl.BlockSpec((tm // chunk, cols, chunk), lambda i: (i, 0, 0)),
        out_shape=jax.ShapeDtypeStruct((n // chunk, cols, chunk), BF16),
        compiler_params=_cparams(("arbitrary",)),
        name=name,
    )(x2, mod_l, g.reshape(1, D_MODEL), wt_all)


def _dsa_kernel(q_ref, iq_ref, iw_ref, k_ref, vt_ref, ik_ref, bias_ref, o_ref,
                key_ref, hi_ref, lo_ref, madd_ref, qt_ref, iqt_ref, iwt_ref, m_ref, l_ref, acc_ref, s_ref,
                *, tq, topk):
    qi = pl.program_id(1)
    nck = qi + 1
    idx_scale = (IDX_HEADS * IDX_DIM) ** -0.5

    for h in range(A_HEADS):
        hs = slice(h * LANES, (h + 1) * LANES)
        qt_ref[hs, :] = q_ref[:, hs].astype(F32).T.astype(BF16)
        iqt_ref[hs, :] = iq_ref[:, hs].T.astype(BF16)
    iwt_ref[...] = (iw_ref[...] * idx_scale).T

    krow = lax.broadcasted_iota(jnp.int32, (tq, tq), 0)
    qcol = lax.broadcasted_iota(jnp.int32, (tq, tq), 1)

    def score_chunk(c, carry):
        off = pl.multiple_of(c * tq, tq)
        ikc = ik_ref[pl.ds(off, tq), :].astype(BF16)
        acc = jnp.zeros((tq, tq), F32)
        for h in range(IDX_HEADS):
            s = jnp.dot(ikc, iqt_ref[h * LANES:(h + 1) * LANES, :], preferred_element_type=F32)
            acc = acc + jnp.maximum(s, 0.0) * iwt_ref[h:h + 1, :]
        acc = jnp.where(acc == 0.0, 0.0, acc)
        acc = jnp.where(krow + (c - qi) * tq <= qcol, acc, -jnp.inf)
        kb = pltpu.bitcast(acc, jnp.int32)
        key = jnp.where(kb < 0, kb ^ jnp.int32(0x7FFFFFFF), kb)
        key_ref[c] = key
        hi_ref[c] = jnp.right_shift(key, 16).astype(jnp.int16)
        lo_ref[c] = ((key & 0xFFFF) - HALF_BIAS).astype(jnp.int16)
        return carry

    def paired_loop(count_, fn):
        def pair(i, carry):
            fn(2 * i, carry)
            fn(2 * i + 1, carry)
            return carry

        lax.fori_loop(0, count_ // 2, pair, 0)

        @pl.when(count_ % 2 == 1)
        def _():
            fn(count_ - 1, 0)

    paired_loop(nck, score_chunk)

    def count(ref, pred_fn, rows, zero, one):
        def body(c, parts):
            hit = jnp.where(pred_fn(ref[c]), one, zero)
            parts = list(parts)
            for r in range(tq // rows):
                parts[r % COUNT_CHAINS] = parts[r % COUNT_CHAINS] + hit[r * rows:(r + 1) * rows, :]
            return tuple(parts)

        parts = lax.fori_loop(0, nck, body, (jnp.full((rows, tq), zero),) * COUNT_CHAINS)
        return jnp.sum(sum(p.astype(F32) for p in parts), axis=0, keepdims=True)

    def count16(ref, pred_fn):
        return count(ref, pred_fn, 16, jnp.int16(0), jnp.int16(1))

    def count32(pred_fn):
        return count(key_ref, pred_fn, 8, jnp.float32(0.0), jnp.float32(1.0))

    def bisect16(ref, target):
        def bit_step(i, theta):
            cand = theta + jnp.left_shift(jnp.int32(1), 15 - i)
            cand16 = cand.astype(jnp.int16)
            return jnp.where(count16(ref, lambda k: k >= cand16) >= target, cand, theta)

        return lax.fori_loop(0, 16, bit_step, jnp.full((1, tq), -HALF_BIAS, jnp.int32))

    theta_hi = bisect16(hi_ref, float(topk))
    theta_hi16 = theta_hi.astype(jnp.int16)
    need_lo = topk - count16(hi_ref, lambda k: k > theta_hi16)

    def bucket_chunk(c, carry):
        lo_ref[c] = jnp.where(hi_ref[c] == theta_hi16, lo_ref[c], jnp.int16(-HALF_BIAS))
        return carry

    lax.fori_loop(0, nck, bucket_chunk, 0)
    theta_lo = bisect16(lo_ref, need_lo)
    theta = theta_hi * (2 * HALF_BIAS) + (theta_lo + HALF_BIAS)
    theta = jnp.maximum(theta, KEY_NEG_INF + 1)

    def mask_chunk(c, cnt):
        ge = key_ref[c] >= theta
        madd_ref[c] = jnp.where(ge, 0.0, MASK_NEG)
        return cnt + jnp.sum(jnp.where(ge, 1.0, 0.0), axis=0, keepdims=True)

    cnt_ge = lax.fori_loop(0, nck, mask_chunk, jnp.zeros((1, tq), F32))

    @pl.when(jnp.max(cnt_ge) > topk)
    def _():
        need_eq = topk - count32(lambda kc: kc > theta)
        incl = jnp.where(krow >= qcol, 1.0, 0.0).astype(BF16)

        def tie_chunk(c, run):
            kc = key_ref[c]
            eq = kc == theta
            eqf = jnp.where(eq, 1.0, 0.0)
            pref = jnp.dot(incl, eqf.astype(BF16), preferred_element_type=F32) + run
            eq_add = jnp.where(pref <= need_eq, 0.0, MASK_NEG)
            madd_ref[c] = jnp.where(eq, eq_add, jnp.where(kc > theta, 0.0, MASK_NEG))
            return run + jnp.sum(eqf, axis=0, keepdims=True)

        lax.fori_loop(0, nck, tie_chunk, jnp.zeros((1, tq), F32))

    m_ref[...] = jnp.full(m_ref.shape, -jnp.inf, F32)
    l_ref[...] = jnp.zeros(l_ref.shape, F32)
    acc_ref[...] = jnp.zeros(acc_ref.shape, F32)

    ones_rows = jnp.ones((BF16_ROWS, tq), BF16)

    head_slices = [slice(h * A_HEAD_DIM, (h + 1) * A_HEAD_DIM) for h in range(A_HEADS)]

    def logits(c, h):
        off = pl.multiple_of(c * tq, tq)
        s_ref[h] = jnp.dot(k_ref[pl.ds(off, tq), head_slices[h]], qt_ref[head_slices[h], :],
                           preferred_element_type=F32)

    def attend(c, h, lag):
        hs = head_slices[h]
        s = s_ref[h] + madd_ref[c]
        if lag is not None:
            s = s + bias_ref[h, lag]
        m_old = m_ref[h]
        m_new = jnp.maximum(m_old, jnp.max(s, axis=0, keepdims=True))
        alpha = jnp.exp2(m_old - m_new)
        p = jnp.exp2(s - m_new).astype(BF16)
        pv = jnp.dot(jnp.concatenate([vt_ref[c, hs, :], ones_rows], axis=0), p,
                     preferred_element_type=F32)
        l_ref[h] = alpha * l_ref[h] + pv[A_HEAD_DIM:A_HEAD_DIM + 1]
        acc_ref[h] = alpha * acc_ref[h] + pv[:A_HEAD_DIM]
        m_ref[h] = m_new

    def step(c, lag, prefetch):
        for h in range(A_HEADS):
            attend(c, h, lag)
            if prefetch:
                logits(c + 1, h)

    for h in range(A_HEADS):
        logits(0, h)

    def far_chunk(c, carry):
        step(c, None, True)
        return carry

    paired_loop(jnp.maximum(qi - 1, 0), far_chunk)

    @pl.when(qi >= 1)
    def _():
        step(qi - 1, 1, True)

    step(qi, 0, False)

    for h in range(A_HEADS):
        o = acc_ref[h] * (1.0 / l_ref[h])
        o_ref[:, h * A_HEAD_DIM:(h + 1) * A_HEAD_DIM] = o.T.astype(o_ref.dtype)


def _dsa_attention(p16, p32, vt, bias_tiles, bsz, seq):
    tq = min(DSA_TQ, seq)
    nq = seq // tq
    topk = min(TOPK_MAX, seq // 4)
    n = bsz * seq
    one = pl.Buffered(1)
    return pl.pallas_call(
        functools.partial(_dsa_kernel, tq=tq, topk=topk),
        grid=(bsz, nq),
        in_specs=[
            pl.BlockSpec((tq, A_WIDTH), lambda b, i: (b * nq + i, P16_AQ)),
            pl.BlockSpec((tq, 1024), lambda b, i: (b * nq + i, P32_IQ // 1024)),
            pl.BlockSpec((tq, LANES), lambda b, i: (b * nq + i, P32_IW // LANES)),
            pl.BlockSpec((seq, A_WIDTH), lambda b, i: (b, P16_AK), pipeline_mode=one),
            pl.BlockSpec((nq, A_WIDTH, tq), lambda b, i: (b, 0, 0), pipeline_mode=one),
            pl.BlockSpec((seq, LANES), lambda b, i: (b, P32_IK // LANES), pipeline_mode=one),
            pl.BlockSpec(bias_tiles.shape, lambda b, i: (0, 0, 0, 0), pipeline_mode=one),
        ],
        out_specs=pl.BlockSpec((tq, A_WIDTH), lambda b, i: (b * nq + i, 0)),
        out_shape=jax.ShapeDtypeStruct((n, A_WIDTH), BF16),
        scratch_shapes=[pltpu.VMEM((nq, tq, tq), jnp.int32),
                        pltpu.VMEM((nq, tq, tq), jnp.int16),
                        pltpu.VMEM((nq, tq, tq), jnp.int16),
                        pltpu.VMEM((nq, tq, tq), F32),
                        pltpu.VMEM((A_WIDTH, tq), BF16),
                        pltpu.VMEM((IDX_HEADS * LANES, tq), BF16),
                        pltpu.VMEM((LANES, tq), F32),
                        pltpu.VMEM((A_HEADS, 1, tq), F32),
                        pltpu.VMEM((A_HEADS, 1, tq), F32),
                        pltpu.VMEM((A_HEADS, A_HEAD_DIM, tq), F32),
                        pltpu.VMEM((A_HEADS, tq, tq), F32)],
        compiler_params=_cparams(("arbitrary", "arbitrary")),
        name="dsa_attention",
    )(p16, p32, p32, p16, vt, p32, bias_tiles)


def _t5_bucket(rel):
    max_exact = REL_BUCKETS // 2
    relf = jnp.maximum(rel, 1).astype(F32)
    large = max_exact + (jnp.log(relf / max_exact) / math.log(REL_MAX_DIST / max_exact)
                         * (REL_BUCKETS - max_exact)).astype(jnp.int32)
    large = jnp.minimum(large, REL_BUCKETS - 1)
    return jnp.where(rel < max_exact, rel, large)


def _bias_tiles(rel_bias, tq):
    assert tq >= REL_MAX_DIST
    key = jnp.arange(tq, dtype=jnp.int32)[:, None]
    qry = jnp.arange(tq, dtype=jnp.int32)[None, :]
    bucket = jnp.stack([_t5_bucket(jnp.maximum(lag * tq + qry - key, 0)) for lag in range(2)])
    rel = ((rel_bias - rel_bias[REL_BUCKETS - 1:REL_BUCKETS]) * LOG2_E).astype(F32)
    onehot = bucket[None] == jnp.arange(REL_BUCKETS, dtype=jnp.int32)[:, None, None, None]
    return jnp.sum(jnp.where(onehot[:, None], rel[:, :, None, None, None], 0.0), axis=0)


def _hgrn_kernel(q_ref, f_ref, i_ref, g_ref, lb_ref, ng_ref, tril_ref, o_ref, st_ref, *, rows):
    @pl.when(pl.program_id(1) == 0)
    def _():
        st_ref[...] = jnp.zeros_like(st_ref)

    lb = lb_ref[...]
    f = lb + (1.0 - lb) * jax.nn.sigmoid(f_ref[...])
    logf = jnp.log(f)
    kk = 1.0 - f
    g1 = logf.astype(BF16)
    r1 = logf - g1.astype(F32)
    g2 = r1.astype(BF16)
    g3 = (r1 - g2.astype(F32)).astype(BF16)
    tril = tril_ref[...]
    bcum = (jnp.dot(tril, g1, preferred_element_type=F32)
            + jnp.dot(tril, g2, preferred_element_type=F32)
            + jnp.dot(tril, g3, preferred_element_type=F32))

    srow = lax.broadcasted_iota(jnp.int32, (HGRN_C, B_HEAD_DIM), 0)
    trow = lax.broadcasted_iota(jnp.int32, (HGRN_SB, HGRN_C), 0)
    scol = lax.broadcasted_iota(jnp.int32, (HGRN_SB, HGRN_C), 1)
    ng = ng_ref[...]
    q = q_ref[...].astype(F32)
    qb_all = (q * jnp.exp(bcum)).astype(BF16)

    tiles = [(n, h) for n in range(rows // HGRN_C) for h in range(B_HEADS)]

    def rs(n):
        return slice(n * HGRN_C, (n + 1) * HGRN_C)

    def hs(h):
        return slice(h * B_HEAD_DIM, (h + 1) * B_HEAD_DIM)

    a_parts = {}
    for n, h in tiles:
        bc, qc, kc = bcum[rs(n), hs(h)], q[rs(n), hs(h)], kk[rs(n), hs(h)]
        for sb in range(HGRN_C // HGRN_SB):
            s0 = sb * HGRN_SB
            beta = bc[s0 - 1:s0] if sb > 0 else jnp.zeros((1, B_HEAD_DIM), F32)
            qs = (qc[s0:s0 + HGRN_SB] * jnp.exp(bc[s0:s0 + HGRN_SB] - beta)).astype(BF16)
            live = s0 + HGRN_SB
            ks = (kc[:live] * jnp.exp(beta - bc[:live])).astype(BF16)
            if live < HGRN_C:
                ks = jnp.concatenate([ks, jnp.zeros((HGRN_C - live, B_HEAD_DIM), BF16)], axis=0)
            a_parts[n, h, sb] = lax.dot_general(qs, ks, NT_DIMS, preferred_element_type=F32)
    intra, upd, dec = {}, {}, {}
    for n, h in tiles:
        a_rows = [jnp.where(scol <= trow + sb * HGRN_SB, a_parts[n, h, sb], 0.0)
                  for sb in range(HGRN_C // HGRN_SB)]
        attn = jnp.concatenate(a_rows, axis=0).astype(BF16)
        vc = i_ref[rs(n), hs(h)]
        intra[n, h] = jnp.dot(attn, vc, preferred_element_type=F32)
        bc = bcum[rs(n), hs(h)]
        blast = bc[HGRN_C - 1:HGRN_C]
        kdec = (kk[rs(n), hs(h)] * jnp.exp(blast - bc)).astype(BF16)
        upd[n, h] = lax.dot_general(vc, kdec, TN_DIMS, preferred_element_type=F32)
        dec[n, h] = jnp.exp(blast)
    for n, h in tiles:
        st = st_ref[h]
        o = intra[n, h] + lax.dot_general(qb_all[rs(n), hs(h)], st.astype(BF16), NT_DIMS,
                                          preferred_element_type=F32)
        st_ref[h] = st * dec[n, h] + upd[n, h]
        ms = jnp.mean(o * o, axis=-1, keepdims=True)
        on = o * lax.rsqrt(ms + EPS) * ng
        o_ref[rs(n), hs(h)] = (on * _silu(g_ref[rs(n), hs(h)].astype(F32))).astype(o_ref.dtype)


def _hgrn2(p16, p32, lb_l, norm_g, bsz, seq):
    rows = min(HGRN_L, seq)
    nj = seq // rows
    n = bsz * seq
    r = jnp.arange(rows, dtype=jnp.int32)
    tril = ((r[:, None] >= r[None, :]) & (r[:, None] // HGRN_C == r[None, :] // HGRN_C)).astype(BF16)

    def col(base):
        return lambda b, j: (b * nj + j, base)

    return pl.pallas_call(
        functools.partial(_hgrn_kernel, rows=rows),
        grid=(bsz, nj),
        in_specs=[
            pl.BlockSpec((rows, B_WIDTH), col(P16_BQ)),
            pl.BlockSpec((rows, B_WIDTH), col(P32_BF // B_WIDTH)),
            pl.BlockSpec((rows, B_WIDTH), col(P16_BI)),
            pl.BlockSpec((rows, B_WIDTH), col(P16_BG)),
            pl.BlockSpec((1, B_WIDTH), lambda b, j: (0, 0)),
            pl.BlockSpec((1, B_HEAD_DIM), lambda b, j: (0, 0)),
            pl.BlockSpec((rows, rows), lambda b, j: (0, 0)),
        ],
        out_specs=pl.BlockSpec((rows, B_WIDTH), lambda b, j: (b * nj + j, 0)),
        out_shape=jax.ShapeDtypeStruct((n, B_WIDTH), BF16),
        scratch_shapes=[pltpu.VMEM((B_HEADS, B_HEAD_DIM, B_HEAD_DIM), F32)],
        compiler_params=_cparams(("arbitrary", "arbitrary")),
        name="hgrn2",
    )(p16, p32, p16, p16, lb_l.reshape(1, B_WIDTH), norm_g.reshape(1, B_HEAD_DIM), tril)


def _ret_kernel(q_ref, k_ref, v_ref, g_ref, cos_ref, sin_ref, idec_ref, qdec_ref, kdec_ref, cdec_ref,
                o_ref, st_ref):
    @pl.when(pl.program_id(1) == 0)
    def _():
        st_ref[...] = jnp.zeros_like(st_ref)

    heads = range(C_HEADS)
    cos = jnp.concatenate([cos_ref[...]] * C_HEADS, axis=1)
    sin_signed = jnp.concatenate([sin_ref[...]] * C_HEADS, axis=1)
    even = lax.broadcasted_iota(jnp.int32, cos.shape, 1) % 2 == 0

    def rot(a):
        swapped = jnp.where(even, pltpu.roll(a, C_QK_WIDTH - 1, 1), pltpu.roll(a, 1, 1))
        return a * cos + swapped * sin_signed

    qr = rot(q_ref[...].astype(F32))
    kr = rot(k_ref[...].astype(F32))
    qk = [slice(h * C_QK_DIM, (h + 1) * C_QK_DIM) for h in heads]
    vs = [slice(h * C_V_DIM, (h + 1) * C_V_DIM) for h in heads]
    attn = [lax.dot_general(qr[:, qk[h]].astype(BF16), kr[:, qk[h]].astype(BF16), NT_DIMS,
                            preferred_element_type=F32) * idec_ref[h] for h in heads]
    inter = [jnp.dot((qr[:, qk[h]] * qdec_ref[h]).astype(BF16), st_ref[h].astype(BF16),
                     preferred_element_type=F32) for h in heads]
    intra = [jnp.dot(attn[h].astype(BF16), v_ref[:, vs[h]], preferred_element_type=F32) for h in heads]
    upd = [jnp.dot((kr[:, qk[h]] * kdec_ref[h]).T.astype(BF16), v_ref[:, vs[h]],
                   preferred_element_type=F32) for h in heads]
    for h in heads:
        st_ref[h] = cdec_ref[h, 0:1, :] * st_ref[h] + upd[h]
        o = intra[h] + inter[h]
        ms = jnp.mean(o * o, axis=-1, keepdims=True)
        o_ref[:, vs[h]] = (_silu(g_ref[:, vs[h]].astype(F32)) * (o * lax.rsqrt(ms + EPS))).astype(o_ref.dtype)


def _retention_tables(seq):
    pos = jnp.arange(seq, dtype=F32)
    theta = jnp.repeat(1.0 / (10000.0 ** jnp.linspace(0.0, 1.0, C_QK_DIM // 2)), 2)
    ang = pos[:, None] * theta[None, :]
    pair_sign = jnp.where(jnp.arange(C_QK_DIM) % 2 == 0, -1.0, 1.0)
    log_gamma = jnp.log(1.0 - 2.0 ** (-5.0 - jnp.arange(C_HEADS, dtype=F32)))
    idx = jnp.arange(RET_C, dtype=F32)
    causal = idx[:, None] >= idx[None, :]
    idec = jnp.exp(jnp.where(causal[None], (idx[:, None] - idx[None, :])[None] * log_gamma[:, None, None],
                             -jnp.inf))
    qdec = jnp.exp((idx + 1.0)[None, :] * log_gamma[:, None])[..., None]
    kdec = jnp.exp((RET_C - 1.0 - idx)[None, :] * log_gamma[:, None])[..., None]
    cdec = jnp.exp(RET_C * log_gamma)[:, None, None]
    return (jnp.cos(ang), jnp.sin(ang) * pair_sign[None, :], idec,
            jnp.broadcast_to(qdec, (C_HEADS, RET_C, C_QK_DIM)),
            jnp.broadcast_to(kdec, (C_HEADS, RET_C, C_QK_DIM)),
            jnp.broadcast_to(cdec, (C_HEADS, 8, C_V_DIM)))


def _retention(p16, tables, bsz, seq):
    cos, sin, idec, qdec, kdec, cdec = tables
    nj = seq // RET_C
    n = bsz * seq
    v_blk = C_V_WIDTH // 1024

    def whole(a):
        return pl.BlockSpec(a.shape, lambda b, j: (0,) * a.ndim)

    return pl.pallas_call(
        _ret_kernel,
        grid=(bsz, nj),
        in_specs=[
            pl.BlockSpec((RET_C, C_QK_WIDTH), lambda b, j: (b * nj + j, P16_CQ)),
            pl.BlockSpec((RET_C, C_QK_WIDTH), lambda b, j: (b * nj + j, P16_CK)),
            pl.BlockSpec((RET_C, C_V_WIDTH), lambda b, j: (b * nj + j, P16_CV // v_blk)),
            pl.BlockSpec((RET_C, C_V_WIDTH), lambda b, j: (b * nj + j, P16_CG // v_blk)),
            pl.BlockSpec((RET_C, C_QK_DIM), lambda b, j: (j, 0)),
            pl.BlockSpec((RET_C, C_QK_DIM), lambda b, j: (j, 0)),
            whole(idec), whole(qdec), whole(kdec), whole(cdec),
        ],
        out_specs=pl.BlockSpec((RET_C, C_V_WIDTH), lambda b, j: (b * nj + j, 0)),
        out_shape=jax.ShapeDtypeStruct((n, C_V_WIDTH), BF16),
        scratch_shapes=[pltpu.VMEM((C_HEADS, C_QK_DIM, C_V_DIM), F32)],
        compiler_params=_cparams(("arbitrary", "arbitrary")),
        name="retention",
    )(p16, p16, p16, p16, cos, sin, idec, qdec, kdec, cdec)


def _merge_kernel(oa_ref, ob_ref, oc_ref, ga_ref, gb_ref, gc_ref, x_ref, mod_ref, g2_ref,
                  wa_ref, wb_ref, wc_ref, wo_ref, wrh_ref, wrl_ref, br_ref,
                  x1_ref, h2_ref, route_ref, *, tiles_per_batch):
    b = pl.program_id(0) // tiles_per_batch

    def gated(o_ref, w_ref, g_ref):
        y = jnp.dot(o_ref[...], w_ref[...], preferred_element_type=F32)
        return jax.nn.sigmoid(g_ref[...].astype(F32)) * y

    merged = gated(oa_ref, wa_ref, ga_ref) + gated(ob_ref, wb_ref, gb_ref) + gated(oc_ref, wc_ref, gc_ref)
    y = jnp.dot(merged.astype(BF16), wo_ref[...], preferred_element_type=F32)
    gt1 = mod_ref[pl.ds(b, 1), 2 * D_MODEL:3 * D_MODEL]
    x1 = x_ref[...] + gt1 * y
    x1_ref[...] = x1
    sh2 = mod_ref[pl.ds(b, 1), 3 * D_MODEL:4 * D_MODEL]
    sc2 = mod_ref[pl.ds(b, 1), 4 * D_MODEL:5 * D_MODEL]
    h2 = _rms_mod(x1, g2_ref[...], sc2, sh2)
    h_hi = h2.astype(BF16)
    words = _pack_bf16_pairs(h2)
    for j in range(h2_ref.shape[0]):
        h2_ref[j] = words[:, j * SC_SUB:(j + 1) * SC_SUB]
    h_lo = (h2 - h_hi.astype(F32)).astype(BF16)
    logits = (jnp.dot(h_hi, wrh_ref[...], preferred_element_type=F32)
              + jnp.dot(h_lo, wrh_ref[...], preferred_element_type=F32)
              + jnp.dot(h_hi, wrl_ref[...], preferred_element_type=F32)) + br_ref[...]
    lane = lax.broadcasted_iota(jnp.int32, logits.shape, 1).astype(F32)
    neg_inf = -jnp.inf

    def first_argmax(vals):
        top = jnp.max(vals, axis=-1, keepdims=True)
        idx = jnp.min(jnp.where(vals == top, lane, float(LANES)), axis=-1, keepdims=True)
        return top, idx

    gl = jnp.where(lane < N_GROUPS, logits, neg_inf)
    gmax, gsel = first_argmax(gl)
    gprob = 1.0 / jnp.sum(jnp.exp(gl - gmax), axis=-1, keepdims=True)
    lo = N_GROUPS + EXPERTS_PER_GROUP * gsel
    el = jnp.where((lane >= lo) & (lane < lo + EXPERTS_PER_GROUP), logits, neg_inf)
    v1, i1 = first_argmax(el)
    el2 = jnp.where(lane == i1, neg_inf, el)
    v2, i2 = first_argmax(el2)
    e2 = jnp.exp(v2 - v1)
    den = 1.0 + e2
    route_ref[...] = jnp.where(lane == 0.0, i1 - N_GROUPS,
                               jnp.where(lane == 1.0, i2 - N_GROUPS,
                                         jnp.where(lane == 2.0, gprob / den,
                                                   jnp.where(lane == 3.0, gprob * (e2 / den), 0.0))))


def _merge(o_a, o_b, o_c, p16, x2, mod_l, g2, wa, wb, wc, wo, wr_hi, wr_lo, br, seq):
    n = x2.shape[0]
    tm = min(512, seq)
    one = pl.Buffered(1)

    def rows(width, cb=0):
        return pl.BlockSpec((tm, width), lambda i: (i, cb))

    def whole(a):
        return pl.BlockSpec(a.shape, lambda i: (0,) * a.ndim, pipeline_mode=one)

    return pl.pallas_call(
        functools.partial(_merge_kernel, tiles_per_batch=seq // tm),
        grid=(n // tm,),
        in_specs=[rows(A_WIDTH), rows(B_WIDTH), rows(C_V_WIDTH),
                  rows(D_MODEL, P16_GA), rows(D_MODEL, P16_GB), rows(D_MODEL, P16_GC),
                  rows(D_MODEL), whole(mod_l), pl.BlockSpec((1, D_MODEL), lambda i: (0, 0)),
                  whole(wa), whole(wb), whole(wc), whole(wo), whole(wr_hi), whole(wr_lo), whole(br)],
        out_specs=[rows(D_MODEL), pl.BlockSpec((SC_PIECES, tm, SC_SUB), lambda i: (0, i, 0)), rows(LANES)],
        out_shape=[jax.ShapeDtypeStruct((n, D_MODEL), F32),
                   jax.ShapeDtypeStruct((SC_PIECES, n, SC_SUB), jnp.int32),
                   jax.ShapeDtypeStruct((n, LANES), F32)],
        compiler_params=_cparams(("arbitrary",)),
        name="merge_route",
    )(o_a, o_b, o_c, p16, p16, p16, x2, mod_l, g2.reshape(1, D_MODEL), wa, wb, wc, wo, wr_hi, wr_lo, br)


def _lane_pick(vals, lane, idx):
    return jnp.sum(jnp.where(lane == idx, vals, 0.0), axis=-1, keepdims=True)


def _rank_kernel(route_ref, rk_ref, cnt_ref, run_ref):
    @pl.when(pl.program_id(0) == 0)
    def _():
        run_ref[...] = jnp.zeros_like(run_ref)

    route = route_ref[...]
    tb = route.shape[0]
    lane = lax.broadcasted_iota(jnp.int32, route.shape, 1).astype(F32)
    e1 = _lane_pick(route, lane, 0.0)
    e2 = _lane_pick(route, lane, 1.0)
    sel = jnp.where((lane == e1) | (lane == e2), 1.0, 0.0)
    r = lax.broadcasted_iota(jnp.int32, (tb, tb), 0)
    c = lax.broadcasted_iota(jnp.int32, (tb, tb), 1)
    before = jnp.where(c < r, 1.0, 0.0).astype(BF16)
    rank = jnp.dot(before, sel.astype(BF16), preferred_element_type=F32) + run_ref[0:1, :]
    rk_ref[...] = jnp.where(lane == 0.0, _lane_pick(rank, lane, e1),
                            jnp.where(lane == 1.0, _lane_pick(rank, lane, e2), 0.0))
    run_ref[...] = run_ref[...] + jnp.sum(sel, axis=0, keepdims=True)
    cnt_ref[...] = run_ref[...]


def _expert_ranks(route):
    n = route.shape[0]
    tb = min(PLAN_TB, n)
    return pl.pallas_call(
        _rank_kernel,
        grid=(n // tb,),
        in_specs=[pl.BlockSpec((tb, LANES), lambda i: (i, 0))],
        out_specs=[pl.BlockSpec((tb, LANES), lambda i: (i, 0)),
                   pl.BlockSpec((8, LANES), lambda i: (0, 0))],
        out_shape=[jax.ShapeDtypeStruct((n, LANES), F32), jax.ShapeDtypeStruct((8, LANES), F32)],
        scratch_shapes=[pltpu.VMEM((8, LANES), F32)],
        compiler_params=_cparams(("arbitrary",)),
        name="expert_ranks",
    )(route)


def _plan_kernel(cnt_ref, route_ref, rk_ref, pos_ref, tmap_ref):
    lane_i = lax.broadcasted_iota(jnp.int32, (8, LANES), 1)
    cnt = jnp.where(lane_i < N_EXPERTS, cnt_ref[...], 0.0)
    padded = jnp.floor((cnt + (MOE_TM - 1)) * (1.0 / MOE_TM)) * MOE_TM
    r = lax.broadcasted_iota(jnp.int32, (LANES, LANES), 0)
    c = lax.broadcasted_iota(jnp.int32, (LANES, LANES), 1)
    base = jnp.dot(padded, jnp.where(r < c, 1.0, 0.0), precision=lax.Precision.HIGHEST,
                   preferred_element_type=F32)

    route = route_ref[...]
    lane = lax.broadcasted_iota(jnp.int32, route.shape, 1).astype(F32)
    rk = rk_ref[...]
    base_row = base[0:1, :]
    pos1 = _lane_pick(base_row, lane, _lane_pick(route, lane, 0.0)) + _lane_pick(rk, lane, 0.0)
    pos2 = _lane_pick(base_row, lane, _lane_pick(route, lane, 1.0)) + _lane_pick(rk, lane, 1.0)
    pos_ref[...] = jnp.where(lane == 0.0, pos1, jnp.where(lane == 1.0, pos2, 0.0)).astype(jnp.int32)

    @pl.when(pl.program_id(0) == 0)
    def _():
        nt = tmap_ref.shape[0]
        tlane = lax.broadcasted_iota(jnp.int32, (nt, LANES), 1)
        start = (lax.broadcasted_iota(jnp.int32, (nt, LANES), 0) * MOE_TM).astype(F32)
        end_row = jnp.where(tlane < N_EXPERTS, base_row + padded[0:1, :], 3e38)
        expert = jnp.sum(jnp.where(end_row <= start, 1.0, 0.0), axis=-1, keepdims=True)
        expert_c = jnp.minimum(expert, N_EXPERTS - 1.0)
        tl = tlane.astype(F32)
        left = _lane_pick(cnt[0:1, :], tl, expert_c) - (start[:, 0:1] - _lane_pick(base_row, tl, expert_c))
        valid = jnp.where(expert < N_EXPERTS, jnp.clip(left, 0.0, float(MOE_TM)), 0.0)
        tmap_ref[...] = jnp.where(tlane == 0, expert_c, jnp.where(tlane == 1, valid, 0.0)).astype(jnp.int32)


def _expert_plan(cnt, route, rk, n_tiles):
    n = route.shape[0]
    tb = min(PLAN_TB, n)
    nt_pad = -(-n_tiles // 8) * 8
    return pl.pallas_call(
        _plan_kernel,
        grid=(n // tb,),
        in_specs=[pl.BlockSpec((8, LANES), lambda i: (0, 0)),
                  pl.BlockSpec((tb, LANES), lambda i: (i, 0)),
                  pl.BlockSpec((tb, LANES), lambda i: (i, 0))],
        out_specs=[pl.BlockSpec((tb, LANES), lambda i: (i, 0)),
                   pl.BlockSpec((nt_pad, LANES), lambda i: (0, 0))],
        out_shape=[jax.ShapeDtypeStruct((n, LANES), jnp.int32),
                   jax.ShapeDtypeStruct((nt_pad, LANES), jnp.int32)],
        compiler_params=_cparams(("arbitrary",)),
        name="expert_plan",
    )(cnt, route, rk)


def _sc_mesh():
    return plsc.VectorSubcoreMesh(core_axis_name="c", subcore_axis_name="s")


def _sc_scatter_rows(src, idx, out_rows):
    m = idx.shape[0]
    n_src_win = src.shape[0] // SC_WINDOW

    @functools.partial(pl.kernel, out_type=jax.ShapeDtypeStruct((out_rows, src.shape[1]), src.dtype),
                       mesh=_sc_mesh(), scratch_types=[])
    def scatter(x_hbm, i_hbm, o_hbm):
        def body(x_vmem, i_vmem):
            pltpu.sync_copy(x_vmem, o_hbm.at[i_vmem.at[0]])

        pltpu.emit_pipeline(
            body, grid=(m // SC_WINDOW,),
            in_specs=[pl.BlockSpec((SC_WINDOW, src.shape[1]), lambda i: (i % n_src_win, 0)),
                      pl.BlockSpec((1, SC_WINDOW), lambda i: (0, i))],
            out_specs=[], core_axis_name=("c", "s"),
            dimension_semantics=(pltpu.PARALLEL,))(x_hbm, i_hbm)

    return scatter(src, idx.reshape(1, m))


def _sc_gather_rows(table, idx):
    m = idx.shape[0]

    @functools.partial(pl.kernel, out_type=jax.ShapeDtypeStruct((m, table.shape[1]), table.dtype),
                       mesh=_sc_mesh(), scratch_types=[])
    def gather(x_hbm, i_hbm, o_hbm):
        def body(i_vmem, o_vmem):
            pltpu.sync_copy(x_hbm.at[i_vmem.at[0]], o_vmem)

        pltpu.emit_pipeline(
            body, grid=(m // SC_WINDOW,),
            in_specs=[pl.BlockSpec((1, SC_WINDOW), lambda i: (0, i))],
            out_specs=[pl.BlockSpec((SC_WINDOW, table.shape[1]), lambda i: (i, 0))],
            core_axis_name=("c", "s"),
            dimension_semantics=(pltpu.PARALLEL,))(i_hbm, o_hbm)

    return gather(table, idx.reshape(1, m))


def _piece_row_index(pos, rows):
    return (jnp.arange(SC_PIECES, dtype=jnp.int32)[:, None] * rows + pos[None, :]).reshape(-1)


def _grouped_kernel(te_ref, tv_ref, x_ref, wg_ref, wu_ref, wd_ref, o_ref):
    valid = tv_ref[pl.program_id(0)]

    @pl.when(valid > 0)
    def _():
        words = jnp.concatenate([x_ref[j] for j in range(SC_PIECES)], axis=1)
        row = lax.broadcasted_iota(jnp.int32, words.shape, 0)
        words = jnp.where(row < valid, words, 0)
        x = _unpack_bf16_pairs(words).astype(BF16)
        a = jnp.dot(x, wg_ref[0].astype(BF16), preferred_element_type=F32)
        u = jnp.dot(x, wu_ref[0].astype(BF16), preferred_element_type=F32)
        hm = (_silu(a) * u).astype(BF16)
        out = _pack_bf16_pairs(jnp.dot(hm, wd_ref[0].astype(BF16), preferred_element_type=F32))
        for j in range(SC_PIECES):
            o_ref[j] = out[:, j * SC_SUB:(j + 1) * SC_SUB]

    @pl.when(valid <= 0)
    def _():
        o_ref[...] = jnp.zeros_like(o_ref)


def _grouped_experts(tile_expert, tile_valid, xs, wg, wu, wd, layer):
    n_tiles = tile_expert.shape[0]
    rows_block = pl.BlockSpec((SC_PIECES, MOE_TM, SC_SUB), lambda i, te, tv: (0, i, 0))
    return pl.pallas_call(
        _grouped_kernel,
        grid_spec=pltpu.PrefetchScalarGridSpec(
            num_scalar_prefetch=2,
            grid=(n_tiles,),
            in_specs=[rows_block,
                      pl.BlockSpec((None, 1, D_MODEL, D_EXPERT), lambda i, te, tv: (layer, te[i], 0, 0)),
                      pl.BlockSpec((None, 1, D_MODEL, D_EXPERT), lambda i, te, tv: (layer, te[i], 0, 0)),
                      pl.BlockSpec((None, 1, D_EXPERT, D_MODEL), lambda i, te, tv: (layer, te[i], 0, 0))],
            out_specs=rows_block),
        out_shape=jax.ShapeDtypeStruct(xs.shape, jnp.int32),
        compiler_params=_cparams(("arbitrary",)),
        name="grouped_experts",
    )(tile_expert, tile_valid, xs, wg, wu, wd)


def _combine_kernel(x1_ref, y_ref, route_ref, mod_ref, o_ref, *, tiles_per_batch):
    b = pl.program_id(0) // tiles_per_batch
    gt2 = mod_ref[pl.ds(b, 1), 5 * D_MODEL:6 * D_MODEL]
    route = route_ref[...]
    lane = lax.broadcasted_iota(jnp.int32, route.shape, 1).astype(F32)
    w1 = _lane_pick(route, lane, 2.0)
    w2 = _lane_pick(route, lane, 3.0)
    y1 = _unpack_bf16_pairs(jnp.concatenate([y_ref[0, j] for j in range(SC_PIECES)], axis=1))
    y2 = _unpack_bf16_pairs(jnp.concatenate([y_ref[1, j] for j in range(SC_PIECES)], axis=1))
    o_ref[...] = x1_ref[...] + gt2 * (w1 * y1 + w2 * y2)


def _combine(x1, y2, route, mod_l, seq):
    n = x1.shape[0]
    tm = min(1024, seq)
    return pl.pallas_call(
        functools.partial(_combine_kernel, tiles_per_batch=seq // tm),
        grid=(n // tm,),
        in_specs=[pl.BlockSpec((tm, D_MODEL), lambda i: (i, 0)),
                  pl.BlockSpec((2, SC_PIECES, tm, SC_SUB), lambda i: (0, 0, i, 0)),
                  pl.BlockSpec((tm, LANES), lambda i: (i, 0)),
                  pl.BlockSpec(mod_l.shape, lambda i: (0, 0))],
        out_specs=pl.BlockSpec((tm, D_MODEL), lambda i: (i, 0)),
        out_shape=jax.ShapeDtypeStruct((n, D_MODEL), F32),
        compiler_params=_cparams(("arbitrary",)),
        name="moe_combine",
    )(x1, y2, route, mod_l)


def _moe(h2, route, x1, mod_l, wg, wu, wd, layer, seq):
    n = h2.shape[1]
    n_tiles = (2 * n) // MOE_TM + N_EXPERTS
    rows = n_tiles * MOE_TM
    rk, cnt = _expert_ranks(route)
    pos, tmap = _expert_plan(cnt, route, rk, n_tiles)
    idx = jnp.concatenate([_piece_row_index(pos[:, 0], rows), _piece_row_index(pos[:, 1], rows)])
    xs = _sc_scatter_rows(h2.reshape(SC_PIECES * n, SC_SUB), idx, SC_PIECES * rows)
    ys = _grouped_experts(tmap[:n_tiles, 0], tmap[:n_tiles, 1], xs.reshape(SC_PIECES, rows, SC_SUB),
                          wg, wu, wd, layer)
    y2 = _sc_gather_rows(ys.reshape(SC_PIECES * rows, SC_SUB), idx).reshape(2, SC_PIECES, n, SC_SUB)
    return _combine(x1, y2, route, mod_l, seq)


def _final_norm_kernel(x_ref, g_ref, o_ref):
    x = x_ref[...]
    ms = jnp.mean(x * x, axis=-1, keepdims=True)
    o_ref[...] = x * lax.rsqrt(ms + EPS) * g_ref[...]


def _final_norm(x2, g, seq):
    n = x2.shape[0]
    tm = min(1024, seq)
    return pl.pallas_call(
        _final_norm_kernel,
        grid=(n // tm,),
        in_specs=[pl.BlockSpec((tm, D_MODEL), lambda i: (i, 0)),
                  pl.BlockSpec((1, D_MODEL), lambda i: (0, 0))],
        out_specs=pl.BlockSpec((tm, D_MODEL), lambda i: (i, 0)),
        out_shape=jax.ShapeDtypeStruct((n, D_MODEL), F32),
        compiler_params=_cparams(("arbitrary",)),
        name="final_norm",
    )(x2, g.reshape(1, D_MODEL))


_IN_OFFS = [sum(IN_SPLITS[:i]) for i in range(len(IN_SPLITS) + 1)]
(_AQ, _AK, _AV, _IQ, _IK, _IW, _BQ, _BF, _BI, _BG, _CQ, _CK, _CV, _CG, _GA, _GB, _GC) = range(len(IN_SPLITS))


def _pack_kernel(w_ref, w16_ref, w32_ref, wvt_ref):
    def cols(seg):
        return w_ref[0, :, _IN_OFFS[seg]:_IN_OFFS[seg + 1]]

    scale = {_AQ: A_HEAD_DIM ** -0.5 * LOG2_E, _CK: C_QK_DIM ** -0.5}
    at = 0
    for seg in (_CV, _CG, _AQ, _AK, _BQ, _BI, _BG, _CQ, _CK, _GA, _GB, _GC):
        v = cols(seg)
        if seg in scale:
            v = v * scale[seg]
        w16_ref[0, :, at:at + v.shape[1]] = v.astype(BF16)
        at += v.shape[1]

    rows = w_ref.shape[1]
    w32_ref[0, :, P32_BF:P32_BF + B_WIDTH] = cols(_BF).astype(BF16)
    iq = cols(_IQ)
    zeros = jnp.zeros((rows, LANES - IDX_DIM), F32)
    for h in range(IDX_HEADS):
        w32_ref[0, :, P32_IQ + h * LANES:P32_IQ + (h + 1) * LANES] = jnp.concatenate(
            [iq[:, h * IDX_DIM:(h + 1) * IDX_DIM], zeros], axis=1).astype(BF16)
    w32_ref[0, :, P32_IK:P32_IK + LANES] = jnp.concatenate([cols(_IK), zeros], axis=1).astype(BF16)
    w32_ref[0, :, P32_IW:P32_IW + LANES] = jnp.concatenate(
        [cols(_IW), jnp.zeros((rows, LANES - IDX_HEADS), F32)], axis=1).astype(BF16)
    wvt_ref[0] = cols(_AV).T.astype(BF16)


def _pack_w_in(w_in):
    depth, d, width = w_in.shape
    rows = LANES
    w16_width = 14 * 1024
    w32_width = P32_IW + LANES
    return pl.pallas_call(
        _pack_kernel,
        grid=(depth, d // rows),
        in_specs=[pl.BlockSpec((1, rows, width), lambda l, r: (l, r, 0))],
        out_specs=[pl.BlockSpec((1, rows, w16_width), lambda l, r: (l, r, 0)),
                   pl.BlockSpec((1, rows, w32_width), lambda l, r: (l, r, 0)),
                   pl.BlockSpec((1, A_WIDTH, rows), lambda l, r: (l, 0, r))],
        out_shape=[jax.ShapeDtypeStruct((depth, d, w16_width), BF16),
                   jax.ShapeDtypeStruct((depth, d, w32_width), BF16),
                   jax.ShapeDtypeStruct((depth, A_WIDTH, d), BF16)],
        compiler_params=_cparams(("arbitrary", "arbitrary")),
        name="pack_w_in",
    )(w_in)


def _split_bf16(w):
    hi = w.astype(BF16)
    return hi, (w - hi.astype(F32)).astype(BF16)


def kernel(x, c, rel_bias, hgrn_lb_raw, norm1_g, norm2_g, ada_w, ada_b, w_in, hgrn_norm_g, w_branch_a,
           w_branch_b, w_branch_c, w_out, router_group_w, router_group_b, router_expert_w,
           router_expert_b, expert_w_gate, expert_w_up, expert_w_down, final_norm_g):
    bsz, seq, _ = x.shape
    depth = w_in.shape[0]
    n = bsz * seq
    x2 = x.reshape(n, D_MODEL)
    tq = min(DSA_TQ, seq)

    lb_all = _hgrn_lower_bounds(hgrn_lb_raw)
    c_pad = jnp.pad(c, ((0, (-bsz) % 8), (0, 0)))
    mod = _ada_mod(c_pad, ada_w, ada_b)
    bias_tiles = _bias_tiles(rel_bias, tq)
    ret_tables = _retention_tables(seq)
    w16_all, w32_all, wvt_all = _pack_w_in(w_in)

    for l in range(depth):
        p16 = _norm_project(x2, mod[l], norm1_g[l], w16_all, l, BF16, 1024, seq, "proj_bf16")
        p32 = _norm_project(x2, mod[l], norm1_g[l], w32_all, l, F32, 768, seq, "proj_f32")
        vt = _norm_project_t(x2, mod[l], norm1_g[l], wvt_all, l, tq, seq, "proj_vt")
        o_a = _dsa_attention(p16, p32, vt, bias_tiles, bsz, seq)
        o_b = _hgrn2(p16, p32, lb_all[l], hgrn_norm_g[l], bsz, seq)
        o_c = _retention(p16, ret_tables, bsz, seq)
        wr = jnp.concatenate([router_group_w[l], router_expert_w[l],
                              jnp.zeros((D_MODEL, LANES - N_GROUPS - N_EXPERTS), F32)], axis=1)
        br = jnp.concatenate([router_group_b[l], router_expert_b[l],
                              jnp.zeros((LANES - N_GROUPS - N_EXPERTS,), F32)]).reshape(1, LANES)
        wr_hi, wr_lo = _split_bf16(wr)
        x1, h2, route = _merge(o_a, o_b, o_c, p16, x2, mod[l], norm2_g[l],
                               w_branch_a[l].astype(BF16), w_branch_b[l].astype(BF16),
                               w_branch_c[l].astype(BF16), w_out[l].astype(BF16),
                               wr_hi, wr_lo, br, seq)
        x2 = _moe(h2, route, x1, mod[l], expert_w_gate, expert_w_up, expert_w_down, l, seq)

    return _final_norm(x2, final_norm_g, seq).reshape(bsz, seq, D_MODEL)
```

```python
import functools
import math

import jax
import jax.numpy as jnp
from jax import lax
from jax.experimental import pallas as pl
from jax.experimental.pallas import tpu as pltpu
from jax.experimental.pallas import tpu_sc as plsc

F32 = jnp.float32
BF16 = jnp.bfloat16

D_MODEL = 1024
A_HEADS = 8
A_HEAD_DIM = 128
IDX_HEADS = 8
IDX_DIM = 64
TOPK_MAX = 256
REL_BUCKETS = 32
REL_MAX_DIST = 128
B_HEADS = 8
B_HEAD_DIM = 128
C_HEADS = 4
C_QK_DIM = 256
C_V_DIM = 512
N_GROUPS = 4
EXPERTS_PER_GROUP = 8
N_EXPERTS = 32
D_EXPERT = 512
EPS = 1e-6

A_WIDTH = A_HEADS * A_HEAD_DIM
B_WIDTH = B_HEADS * B_HEAD_DIM
C_QK_WIDTH = C_HEADS * C_QK_DIM
C_V_WIDTH = C_HEADS * C_V_DIM
IN_SPLITS = (A_WIDTH, A_WIDTH, A_WIDTH, IDX_HEADS * IDX_DIM, IDX_DIM, IDX_HEADS,
             B_WIDTH, B_WIDTH, B_WIDTH, B_WIDTH,
             C_QK_WIDTH, C_QK_WIDTH, C_V_WIDTH, C_V_WIDTH,
             D_MODEL, D_MODEL, D_MODEL)

LANES = 128
BF16_ROWS = 16
VMEM_LIMIT = 56 * 1024 * 1024

P16_CV, P16_CG = 0, 2
P16_AQ, P16_AK, P16_BQ, P16_BI, P16_BG, P16_CQ, P16_CK, P16_GA, P16_GB, P16_GC = range(4, 14)
P32_BF = 0
P32_IQ = 1024
P32_IK = 2048
P32_IW = 2176

DSA_TQ = 256
HGRN_L = 256
HGRN_C = 64
HGRN_SB = 16
HGRN_MAX_LOG_DECAY = 80.0
RET_C = 256
KEY_NEG_INF = -2139095041
HALF_BIAS = 32768
MASK_NEG = -1e30
LOG2_E = math.log2(math.e)
COUNT_CHAINS = 4
MOE_TM = 256
PLAN_TB = 1024
SC_WINDOW = 128
SC_SUB = 256
SC_PIECES = D_MODEL // 2 // SC_SUB

NT_DIMS = (((1,), (1,)), ((), ()))
TN_DIMS = (((0,), (0,)), ((), ()))


def _cparams(sem):
    return pltpu.CompilerParams(dimension_semantics=sem, vmem_limit_bytes=VMEM_LIMIT)


def _silu(x):
    return x * jax.nn.sigmoid(x)


def _pack_bf16_pairs(x):
    k = x.shape[1] // 2
    bits = pltpu.bitcast(x.astype(BF16).astype(F32), jnp.int32)
    return (bits[:, :k] & jnp.int32(-65536)) | lax.shift_right_logical(bits[:, k:], 16)


def _unpack_bf16_pairs(words):
    hi = pltpu.bitcast(words & jnp.int32(-65536), F32)
    lo = pltpu.bitcast(lax.shift_left(words, 16), F32)
    return jnp.concatenate([hi, lo], axis=1)


def _lb_kernel(raw_ref, o_ref):
    raw = raw_ref[...]
    m = jnp.max(raw, axis=0, keepdims=True)
    e = jnp.exp(raw - m)
    soft = e / jnp.sum(e, axis=0, keepdims=True)
    run = jnp.zeros_like(soft[0:1])
    for l in range(raw.shape[0]):
        run = run + soft[l:l + 1]
        o_ref[l:l + 1, :] = run - soft[0:1]


def _hgrn_lower_bounds(raw):
    return pl.pallas_call(
        _lb_kernel, out_shape=jax.ShapeDtypeStruct(raw.shape, F32), name="hgrn_lb")(raw)


def _ada_kernel(c_ref, w_ref, b_ref, o_ref):
    a = _silu(c_ref[...])
    o_ref[0] = jnp.dot(a, w_ref[0], precision=lax.Precision.HIGHEST,
                       preferred_element_type=F32) + b_ref[0]


def _ada_mod(c_pad, ada_w, ada_b):
    depth = ada_w.shape[0]
    rows = c_pad.shape[0]
    return pl.pallas_call(
        _ada_kernel,
        grid=(depth, 6),
        in_specs=[pl.BlockSpec((rows, D_MODEL), lambda l, j: (0, 0)),
                  pl.BlockSpec((1, D_MODEL, D_MODEL), lambda l, j: (l, 0, j)),
                  pl.BlockSpec((1, 1, D_MODEL), lambda l, j: (l, 0, j))],
        out_specs=pl.BlockSpec((1, rows, D_MODEL), lambda l, j: (l, 0, j)),
        out_shape=jax.ShapeDtypeStruct((depth, rows, 6 * D_MODEL), F32),
        compiler_params=_cparams(("arbitrary", "arbitrary")),
        name="ada_mod",
    )(c_pad, ada_w, ada_b.reshape(depth, 1, 6 * D_MODEL))


def _rms_mod(x, g, sc, sh):
    ms = jnp.mean(x * x, axis=-1, keepdims=True)
    return (x * lax.rsqrt(ms + EPS) * g) * (1.0 + sc) + sh


def _norm1(x_ref, mod_ref, g_ref, b):
    sh = mod_ref[pl.ds(b, 1), 0:D_MODEL]
    sc = mod_ref[pl.ds(b, 1), D_MODEL:2 * D_MODEL]
    return _rms_mod(x_ref[...], g_ref[...], sc, sh).astype(BF16)


def _proj_kernel(x_ref, mod_ref, g_ref, w_ref, o_ref, h_ref, *, tiles_per_batch):
    @pl.when(pl.program_id(1) == 0)
    def _():
        h_ref[...] = _norm1(x_ref, mod_ref, g_ref, pl.program_id(0) // tiles_per_batch)

    o_ref[...] = jnp.dot(h_ref[...], w_ref[...], preferred_element_type=F32).astype(o_ref.dtype)


def _norm_project(x2, mod_l, g, w_all, layer, out_dtype, tn, seq, name):
    n = x2.shape[0]
    width = w_all.shape[2]
    tm = min(1024, seq)
    return pl.pallas_call(
        functools.partial(_proj_kernel, tiles_per_batch=seq // tm),
        grid=(n // tm, width // tn),
        in_specs=[pl.BlockSpec((tm, D_MODEL), lambda i, j: (i, 0)),
                  pl.BlockSpec(mod_l.shape, lambda i, j: (0, 0)),
                  pl.BlockSpec((1, D_MODEL), lambda i, j: (0, 0)),
                  pl.BlockSpec((None, D_MODEL, tn), lambda i, j: (layer, 0, j))],
        out_specs=pl.BlockSpec((tm, tn), lambda i, j: (i, j)),
        out_shape=jax.ShapeDtypeStruct((n, width), out_dtype),
        scratch_shapes=[pltpu.VMEM((tm, D_MODEL), BF16)],
        compiler_params=_cparams(("arbitrary", "arbitrary")),
        name=name,
    )(x2, mod_l, g.reshape(1, D_MODEL), w_all)


def _proj_t_kernel(x_ref, mod_ref, g_ref, wt_ref, o_ref, *, tiles_per_batch, chunk):
    h = _norm1(x_ref, mod_ref, g_ref, pl.program_id(0) // tiles_per_batch)
    res = lax.dot_general(wt_ref[...], h, NT_DIMS, preferred_element_type=F32)
    for ci in range(o_ref.shape[0]):
        o_ref[ci] = res[:, ci * chunk:(ci + 1) * chunk].astype(o_ref.dtype)


def _norm_project_t(x2, mod_l, g, wt_all, layer, chunk, seq, name):
    n = x2.shape[0]
    cols = wt_all.shape[1]
    tm = min(1024, seq)
    return pl.pallas_call(
        functools.partial(_proj_t_kernel, tiles_per_batch=seq // tm, chunk=chunk),
        grid=(n // tm,),
        in_specs=[pl.BlockSpec((tm, D_MODEL), lambda i: (i, 0)),
                  pl.BlockSpec(mod_l.shape, lambda i: (0, 0)),
                  pl.BlockSpec((1, D_MODEL), lambda i: (0, 0)),
                  pl.BlockSpec((None, cols, D_MODEL), lambda i: (layer, 0, 0))],
        out_specs=pl.BlockSpec((tm // chunk, cols, chunk), lambda i: (i, 0, 0)),
        out_shape=jax.ShapeDtypeStruct((n // chunk, cols, chunk), BF16),
        compiler_params=_cparams(("arbitrary",)),
        name=name,
    )(x2, mod_l, g.reshape(1, D_MODEL), wt_all)


def _dsa_kernel(q_ref, iq_ref, iw_ref, k_ref, vt_ref, ik_ref, bias_ref, o_ref,
                key_ref, hi_ref, lo_ref, madd_ref, qt_ref, iqt_ref, iwt_ref, m_ref, l_ref, acc_ref, s_ref,
                *, tq, topk):
    qi = pl.program_id(1)
    nck = qi + 1
    idx_scale = (IDX_HEADS * IDX_DIM) ** -0.5

    for h in range(A_HEADS):
        hs = slice(h * LANES, (h + 1) * LANES)
        qt_ref[hs, :] = q_ref[:, hs].astype(F32).T.astype(BF16)
        iqt_ref[hs, :] = iq_ref[:, hs].T.astype(BF16)
    iwt_ref[...] = (iw_ref[...] * idx_scale).T

    krow = lax.broadcasted_iota(jnp.int32, (tq, tq), 0)
    qcol = lax.broadcasted_iota(jnp.int32, (tq, tq), 1)

    def score_chunk(c, carry):
        off = pl.multiple_of(c * tq, tq)
        ikc = ik_ref[pl.ds(off, tq), :].astype(BF16)
        acc = jnp.zeros((tq, tq), F32)
        for h in range(IDX_HEADS):
            s = jnp.dot(ikc, iqt_ref[h * LANES:(h + 1) * LANES, :], preferred_element_type=F32)
            acc = acc + jnp.maximum(s, 0.0) * iwt_ref[h:h + 1, :]
        acc = jnp.where(acc == 0.0, 0.0, acc)
        acc = jnp.where(krow + (c - qi) * tq <= qcol, acc, -jnp.inf)
        kb = pltpu.bitcast(acc, jnp.int32)
        key = jnp.where(kb < 0, kb ^ jnp.int32(0x7FFFFFFF), kb)
        key_ref[c] = key
        hi_ref[c] = jnp.right_shift(key, 16).astype(jnp.int16)
        lo_ref[c] = ((key & 0xFFFF) - HALF_BIAS).astype(jnp.int16)
        return carry

    def paired_loop(count_, fn):
        def pair(i, carry):
            fn(2 * i, carry)
            fn(2 * i + 1, carry)
            return carry

        lax.fori_loop(0, count_ // 2, pair, 0)

        @pl.when(count_ % 2 == 1)
        def _():
            fn(count_ - 1, 0)

    paired_loop(nck, score_chunk)

    def count(ref, pred_fn, rows, zero, one):
        def body(c, parts):
            hit = jnp.where(pred_fn(ref[c]), one, zero)
            parts = list(parts)
            for r in range(tq // rows):
                parts[r % COUNT_CHAINS] = parts[r % COUNT_CHAINS] + hit[r * rows:(r + 1) * rows, :]
            return tuple(parts)

        parts = lax.fori_loop(0, nck, body, (jnp.full((rows, tq), zero),) * COUNT_CHAINS)
        return jnp.sum(sum(p.astype(F32) for p in parts), axis=0, keepdims=True)

    def count16(ref, pred_fn):
        return count(ref, pred_fn, 16, jnp.int16(0), jnp.int16(1))

    def count32(pred_fn):
        return count(key_ref, pred_fn, 8, jnp.float32(0.0), jnp.float32(1.0))

    def bisect16(ref, target):
        def bit_step(i, theta):
            cand = theta + jnp.left_shift(jnp.int32(1), 15 - i)
            cand16 = cand.astype(jnp.int16)
            return jnp.where(count16(ref, lambda k: k >= cand16) >= target, cand, theta)

        return lax.fori_loop(0, 16, bit_step, jnp.full((1, tq), -HALF_BIAS, jnp.int32))

    theta_hi = bisect16(hi_ref, float(topk))
    theta_hi16 = theta_hi.astype(jnp.int16)
    need_lo = topk - count16(hi_ref, lambda k: k > theta_hi16)

    def bucket_chunk(c, carry):
        lo_ref[c] = jnp.where(hi_ref[c] == theta_hi16, lo_ref[c], jnp.int16(-HALF_BIAS))
        return carry

    lax.fori_loop(0, nck, bucket_chunk, 0)
    theta_lo = bisect16(lo_ref, need_lo)
    theta = theta_hi * (2 * HALF_BIAS) + (theta_lo + HALF_BIAS)
    theta = jnp.maximum(theta, KEY_NEG_INF + 1)

    def mask_chunk(c, cnt):
        ge = key_ref[c] >= theta
        madd_ref[c] = jnp.where(ge, 0.0, MASK_NEG)
        return cnt + jnp.sum(jnp.where(ge, 1.0, 0.0), axis=0, keepdims=True)

    cnt_ge = lax.fori_loop(0, nck, mask_chunk, jnp.zeros((1, tq), F32))

    @pl.when(jnp.max(cnt_ge) > topk)
    def _():
        need_eq = topk - count32(lambda kc: kc > theta)
        incl = jnp.where(krow >= qcol, 1.0, 0.0).astype(BF16)

        def tie_chunk(c, run):
            kc = key_ref[c]
            eq = kc == theta
            eqf = jnp.where(eq, 1.0, 0.0)
            pref = jnp.dot(incl, eqf.astype(BF16), preferred_element_type=F32) + run
            eq_add = jnp.where(pref <= need_eq, 0.0, MASK_NEG)
            madd_ref[c] = jnp.where(eq, eq_add, jnp.where(kc > theta, 0.0, MASK_NEG))
            return run + jnp.sum(eqf, axis=0, keepdims=True)

        lax.fori_loop(0, nck, tie_chunk, jnp.zeros((1, tq), F32))

    m_ref[...] = jnp.full(m_ref.shape, -jnp.inf, F32)
    l_ref[...] = jnp.zeros(l_ref.shape, F32)
    acc_ref[...] = jnp.zeros(acc_ref.shape, F32)

    ones_rows = jnp.ones((BF16_ROWS, tq), BF16)

    head_slices = [slice(h * A_HEAD_DIM, (h + 1) * A_HEAD_DIM) for h in range(A_HEADS)]

    def logits(c, h):
        off = pl.multiple_of(c * tq, tq)
        s_ref[h] = jnp.dot(k_ref[pl.ds(off, tq), head_slices[h]], qt_ref[head_slices[h], :],
                           preferred_element_type=F32)

    def attend(c, h, lag):
        hs = head_slices[h]
        s = s_ref[h] + madd_ref[c]
        if lag is not None:
            s = s + bias_ref[h, lag]
        m_old = m_ref[h]
        m_new = jnp.maximum(m_old, jnp.max(s, axis=0, keepdims=True))
        alpha = jnp.exp2(m_old - m_new)
        p = jnp.exp2(s - m_new).astype(BF16)
        pv = jnp.dot(jnp.concatenate([vt_ref[c, hs, :], ones_rows], axis=0), p,
                     preferred_element_type=F32)
        l_ref[h] = alpha * l_ref[h] + pv[A_HEAD_DIM:A_HEAD_DIM + 1]
        acc_ref[h] = alpha * acc_ref[h] + pv[:A_HEAD_DIM]
        m_ref[h] = m_new

    def step(c, lag, prefetch):
        for h in range(A_HEADS):
            attend(c, h, lag)
            if prefetch:
                logits(c + 1, h)

    for h in range(A_HEADS):
        logits(0, h)

    def far_chunk(c, carry):
        step(c, None, True)
        return carry

    paired_loop(jnp.maximum(qi - 1, 0), far_chunk)

    @pl.when(qi >= 1)
    def _():
        step(qi - 1, 1, True)

    step(qi, 0, False)

    for h in range(A_HEADS):
        o = acc_ref[h] * (1.0 / l_ref[h])
        o_ref[:, h * A_HEAD_DIM:(h + 1) * A_HEAD_DIM] = o.T.astype(o_ref.dtype)


def _dsa_attention(p16, p32, vt, bias_tiles, bsz, seq):
    tq = min(DSA_TQ, seq)
    nq = seq // tq
    topk = min(TOPK_MAX, seq // 4)
    n = bsz * seq
    one = pl.Buffered(1)
    return pl.pallas_call(
        functools.partial(_dsa_kernel, tq=tq, topk=topk),
        grid=(bsz, nq),
        in_specs=[
            pl.BlockSpec((tq, A_WIDTH), lambda b, i: (b * nq + i, P16_AQ)),
            pl.BlockSpec((tq, 1024), lambda b, i: (b * nq + i, P32_IQ // 1024)),
            pl.BlockSpec((tq, LANES), lambda b, i: (b * nq + i, P32_IW // LANES)),
            pl.BlockSpec((seq, A_WIDTH), lambda b, i: (b, P16_AK), pipeline_mode=one),
            pl.BlockSpec((nq, A_WIDTH, tq), lambda b, i: (b, 0, 0), pipeline_mode=one),
            pl.BlockSpec((seq, LANES), lambda b, i: (b, P32_IK // LANES), pipeline_mode=one),
            pl.BlockSpec(bias_tiles.shape, lambda b, i: (0, 0, 0, 0), pipeline_mode=one),
        ],
        out_specs=pl.BlockSpec((tq, A_WIDTH), lambda b, i: (b * nq + i, 0)),
        out_shape=jax.ShapeDtypeStruct((n, A_WIDTH), BF16),
        scratch_shapes=[pltpu.VMEM((nq, tq, tq), jnp.int32),
                        pltpu.VMEM((nq, tq, tq), jnp.int16),
                        pltpu.VMEM((nq, tq, tq), jnp.int16),
                        pltpu.VMEM((nq, tq, tq), F32),
                        pltpu.VMEM((A_WIDTH, tq), BF16),
                        pltpu.VMEM((IDX_HEADS * LANES, tq), BF16),
                        pltpu.VMEM((LANES, tq), F32),
                        pltpu.VMEM((A_HEADS, 1, tq), F32),
                        pltpu.VMEM((A_HEADS, 1, tq), F32),
                        pltpu.VMEM((A_HEADS, A_HEAD_DIM, tq), F32),
                        pltpu.VMEM((A_HEADS, tq, tq), F32)],
        compiler_params=_cparams(("arbitrary", "arbitrary")),
        name="dsa_attention",
    )(p16, p32, p32, p16, vt, p32, bias_tiles)


def _t5_bucket(rel):
    max_exact = REL_BUCKETS // 2
    relf = jnp.maximum(rel, 1).astype(F32)
    large = max_exact + (jnp.log(relf / max_exact) / math.log(REL_MAX_DIST / max_exact)
                         * (REL_BUCKETS - max_exact)).astype(jnp.int32)
    large = jnp.minimum(large, REL_BUCKETS - 1)
    return jnp.where(rel < max_exact, rel, large)


def _bias_tiles(rel_bias, tq):
    assert tq >= REL_MAX_DIST
    key = jnp.arange(tq, dtype=jnp.int32)[:, None]
    qry = jnp.arange(tq, dtype=jnp.int32)[None, :]
    bucket = jnp.stack([_t5_bucket(jnp.maximum(lag * tq + qry - key, 0)) for lag in range(2)])
    rel = ((rel_bias - rel_bias[REL_BUCKETS - 1:REL_BUCKETS]) * LOG2_E).astype(F32)
    onehot = bucket[None] == jnp.arange(REL_BUCKETS, dtype=jnp.int32)[:, None, None, None]
    return jnp.sum(jnp.where(onehot[:, None], rel[:, :, None, None, None], 0.0), axis=0)


def _hgrn_kernel(q_ref, f_ref, i_ref, g_ref, lb_ref, ng_ref, tril_ref, o_ref, st_ref, attn_ref, stage_ref,
                 *, rows):
    @pl.when(pl.program_id(1) == 0)
    def _():
        st_ref[...] = jnp.zeros_like(st_ref)

    lb = lb_ref[...]
    f = lb + (1.0 - lb) * jax.nn.sigmoid(f_ref[...])
    logf = jnp.log(f)
    kk = 1.0 - f
    g1 = logf.astype(BF16)
    r1 = logf - g1.astype(F32)
    g2 = r1.astype(BF16)
    g3 = (r1 - g2.astype(F32)).astype(BF16)
    tril = tril_ref[...]
    bcum = (jnp.dot(tril, g1, preferred_element_type=F32)
            + jnp.dot(tril, g2, preferred_element_type=F32)
            + jnp.dot(tril, g3, preferred_element_type=F32))

    srow = lax.broadcasted_iota(jnp.int32, (HGRN_C, B_HEAD_DIM), 0)
    trow = lax.broadcasted_iota(jnp.int32, (HGRN_SB, HGRN_C), 0)
    scol = lax.broadcasted_iota(jnp.int32, (HGRN_SB, HGRN_C), 1)
    ng = ng_ref[...]
    q = q_ref[...].astype(F32)
    qb_all = (q * jnp.exp(bcum)).astype(BF16)

    tiles = [(n, h) for n in range(rows // HGRN_C) for h in range(B_HEADS)]

    def rs(n):
        return slice(n * HGRN_C, (n + 1) * HGRN_C)

    def hs(h):
        return slice(h * B_HEAD_DIM, (h + 1) * B_HEAD_DIM)

    worst = None
    for j in range(rows // HGRN_SB):
        r0 = j * HGRN_SB
        span = bcum[r0 + HGRN_SB - 1:r0 + HGRN_SB]
        if r0 % HGRN_C:
            span = span - bcum[r0 - 1:r0]
        worst = span if worst is None else jnp.minimum(worst, span)
    in_range = jnp.min(worst) > -HGRN_MAX_LOG_DECAY

    @pl.when(in_range)
    def _():
        a_parts = {}
        for n, h in tiles:
            bc, qc, kc = bcum[rs(n), hs(h)], q[rs(n), hs(h)], kk[rs(n), hs(h)]
            for sb in range(HGRN_C // HGRN_SB):
                s0 = sb * HGRN_SB
                beta = bc[s0 - 1:s0] if sb > 0 else jnp.zeros((1, B_HEAD_DIM), F32)
                qs = (qc[s0:s0 + HGRN_SB] * jnp.exp(bc[s0:s0 + HGRN_SB] - beta)).astype(BF16)
                live = s0 + HGRN_SB
                ks = (kc[:live] * jnp.exp(beta - bc[:live])).astype(BF16)
                if live < HGRN_C:
                    ks = jnp.concatenate([ks, jnp.zeros((HGRN_C - live, B_HEAD_DIM), BF16)], axis=0)
                a_parts[n, h, sb] = lax.dot_general(qs, ks, NT_DIMS, preferred_element_type=F32)
        for ti, (n, h) in enumerate(tiles):
            a_rows = [jnp.where(scol <= trow + sb * HGRN_SB, a_parts[n, h, sb], 0.0)
                      for sb in range(HGRN_C // HGRN_SB)]
            attn_ref[ti] = jnp.concatenate(a_rows, axis=0)

    @pl.when(jnp.logical_not(in_range))
    def _():
        for h in range(B_HEADS):
            stage_ref[0, h] = bcum[:, hs(h)]
            stage_ref[1, h] = q[:, hs(h)]
            stage_ref[2, h] = kk[:, hs(h)]
        t_idx = lax.broadcasted_iota(jnp.int32, (HGRN_C, HGRN_C), 0)
        s_idx = lax.broadcasted_iota(jnp.int32, (HGRN_C, HGRN_C), 1)

        def safe_tile(ti, carry):
            n, h = ti // B_HEADS, ti % B_HEADS
            r0 = pl.multiple_of(n * HGRN_C, HGRN_C)
            bc = stage_ref[0, h, pl.ds(r0, HGRN_C), :]
            qc = stage_ref[1, h, pl.ds(r0, HGRN_C), :]
            kc = stage_ref[2, h, pl.ds(r0, HGRN_C), :]
            acc = jnp.where(t_idx == s_idx, jnp.sum(qc * kc, axis=-1, keepdims=True), 0.0)
            block = HGRN_C
            while block >= 2:
                half = block // 2
                ref_row = (t_idx & -block) + (half - 1)
                bref = jnp.dot(jnp.where(s_idx == ref_row, 1.0, 0.0), bc, precision=lax.Precision.HIGHEST,
                               preferred_element_type=F32)
                second = (srow & (block - 1)) >= half
                qs = jnp.where(second, qc * jnp.exp(jnp.where(second, bc - bref, 0.0)), 0.0).astype(BF16)
                ks = jnp.where(second, 0.0, kc * jnp.exp(jnp.where(second, 0.0, bref - bc))).astype(BF16)
                a = lax.dot_general(qs, ks, NT_DIMS, preferred_element_type=F32)
                acc = acc + jnp.where((t_idx & -block) == (s_idx & -block), a, 0.0)
                block = half
            attn_ref[ti] = acc
            return carry

        lax.fori_loop(0, len(tiles), safe_tile, 0)

    intra, upd, dec = {}, {}, {}
    for ti, (n, h) in enumerate(tiles):
        attn = attn_ref[ti].astype(BF16)
        vc = i_ref[rs(n), hs(h)]
        intra[n, h] = jnp.dot(attn, vc, preferred_element_type=F32)
        bc = bcum[rs(n), hs(h)]
        blast = bc[HGRN_C - 1:HGRN_C]
        kdec = (kk[rs(n), hs(h)] * jnp.exp(blast - bc)).astype(BF16)
        upd[n, h] = lax.dot_general(vc, kdec, TN_DIMS, preferred_element_type=F32)
        dec[n, h] = jnp.exp(blast)
    for n, h in tiles:
        st = st_ref[h]
        o = intra[n, h] + lax.dot_general(qb_all[rs(n), hs(h)], st.astype(BF16), NT_DIMS,
                                          preferred_element_type=F32)
        st_ref[h] = st * dec[n, h] + upd[n, h]
        ms = jnp.mean(o * o, axis=-1, keepdims=True)
        on = o * lax.rsqrt(ms + EPS) * ng
        o_ref[rs(n), hs(h)] = (on * _silu(g_ref[rs(n), hs(h)].astype(F32))).astype(o_ref.dtype)


def _hgrn2(p16, p32, lb_l, norm_g, bsz, seq):
    rows = min(HGRN_L, seq)
    nj = seq // rows
    n = bsz * seq
    r = jnp.arange(rows, dtype=jnp.int32)
    tril = ((r[:, None] >= r[None, :]) & (r[:, None] // HGRN_C == r[None, :] // HGRN_C)).astype(BF16)

    def col(base):
        return lambda b, j: (b * nj + j, base)

    return pl.pallas_call(
        functools.partial(_hgrn_kernel, rows=rows),
        grid=(bsz, nj),
        in_specs=[
            pl.BlockSpec((rows, B_WIDTH), col(P16_BQ)),
            pl.BlockSpec((rows, B_WIDTH), col(P32_BF // B_WIDTH)),
            pl.BlockSpec((rows, B_WIDTH), col(P16_BI)),
            pl.BlockSpec((rows, B_WIDTH), col(P16_BG)),
            pl.BlockSpec((1, B_WIDTH), lambda b, j: (0, 0)),
            pl.BlockSpec((1, B_HEAD_DIM), lambda b, j: (0, 0)),
            pl.BlockSpec((rows, rows), lambda b, j: (0, 0)),
        ],
        out_specs=pl.BlockSpec((rows, B_WIDTH), lambda b, j: (b * nj + j, 0)),
        out_shape=jax.ShapeDtypeStruct((n, B_WIDTH), BF16),
        scratch_shapes=[pltpu.VMEM((B_HEADS, B_HEAD_DIM, B_HEAD_DIM), F32),
                        pltpu.VMEM((rows // HGRN_C * B_HEADS, HGRN_C, HGRN_C), F32),
                        pltpu.VMEM((3, B_HEADS, rows, B_HEAD_DIM), F32)],
        compiler_params=_cparams(("arbitrary", "arbitrary")),
        name="hgrn2",
    )(p16, p32, p16, p16, lb_l.reshape(1, B_WIDTH), norm_g.reshape(1, B_HEAD_DIM), tril)


def _ret_kernel(q_ref, k_ref, v_ref, g_ref, cos_ref, sin_ref, idec_ref, qdec_ref, kdec_ref, cdec_ref,
                o_ref, st_ref):
    @pl.when(pl.program_id(1) == 0)
    def _():
        st_ref[...] = jnp.zeros_like(st_ref)

    heads = range(C_HEADS)
    cos = jnp.concatenate([cos_ref[...]] * C_HEADS, axis=1)
    sin_signed = jnp.concatenate([sin_ref[...]] * C_HEADS, axis=1)
    even = lax.broadcasted_iota(jnp.int32, cos.shape, 1) % 2 == 0

    def rot(a):
        swapped = jnp.where(even, pltpu.roll(a, C_QK_WIDTH - 1, 1), pltpu.roll(a, 1, 1))
        return a * cos + swapped * sin_signed

    qr = rot(q_ref[...].astype(F32))
    kr = rot(k_ref[...].astype(F32))
    qk = [slice(h * C_QK_DIM, (h + 1) * C_QK_DIM) for h in heads]
    vs = [slice(h * C_V_DIM, (h + 1) * C_V_DIM) for h in heads]
    attn = [lax.dot_general(qr[:, qk[h]].astype(BF16), kr[:, qk[h]].astype(BF16), NT_DIMS,
                            preferred_element_type=F32) * idec_ref[h] for h in heads]
    inter = [jnp.dot((qr[:, qk[h]] * qdec_ref[h]).astype(BF16), st_ref[h].astype(BF16),
                     preferred_element_type=F32) for h in heads]
    intra = [jnp.dot(attn[h].astype(BF16), v_ref[:, vs[h]], preferred_element_type=F32) for h in heads]
    upd = [jnp.dot((kr[:, qk[h]] * kdec_ref[h]).T.astype(BF16), v_ref[:, vs[h]],
                   preferred_element_type=F32) for h in heads]
    for h in heads:
        st_ref[h] = cdec_ref[h, 0:1, :] * st_ref[h] + upd[h]
        o = intra[h] + inter[h]
        ms = jnp.mean(o * o, axis=-1, keepdims=True)
        o_ref[:, vs[h]] = (_silu(g_ref[:, vs[h]].astype(F32)) * (o * lax.rsqrt(ms + EPS))).astype(o_ref.dtype)


def _retention_tables(seq):
    pos = jnp.arange(seq, dtype=F32)
    theta = jnp.repeat(1.0 / (10000.0 ** jnp.linspace(0.0, 1.0, C_QK_DIM // 2)), 2)
    ang = pos[:, None] * theta[None, :]
    pair_sign = jnp.where(jnp.arange(C_QK_DIM) % 2 == 0, -1.0, 1.0)
    log_gamma = jnp.log(1.0 - 2.0 ** (-5.0 - jnp.arange(C_HEADS, dtype=F32)))
    idx = jnp.arange(RET_C, dtype=F32)
    causal = idx[:, None] >= idx[None, :]
    idec = jnp.exp(jnp.where(causal[None], (idx[:, None] - idx[None, :])[None] * log_gamma[:, None, None],
                             -jnp.inf))
    qdec = jnp.exp((idx + 1.0)[None, :] * log_gamma[:, None])[..., None]
    kdec = jnp.exp((RET_C - 1.0 - idx)[None, :] * log_gamma[:, None])[..., None]
    cdec = jnp.exp(RET_C * log_gamma)[:, None, None]
    return (jnp.cos(ang), jnp.sin(ang) * pair_sign[None, :], idec,
            jnp.broadcast_to(qdec, (C_HEADS, RET_C, C_QK_DIM)),
            jnp.broadcast_to(kdec, (C_HEADS, RET_C, C_QK_DIM)),
            jnp.broadcast_to(cdec, (C_HEADS, 8, C_V_DIM)))


def _retention(p16, tables, bsz, seq):
    cos, sin, idec, qdec, kdec, cdec = tables
    nj = seq // RET_C
    n = bsz * seq
    v_blk = C_V_WIDTH // 1024

    def whole(a):
        return pl.BlockSpec(a.shape, lambda b, j: (0,) * a.ndim)

    return pl.pallas_call(
        _ret_kernel,
        grid=(bsz, nj),
        in_specs=[
            pl.BlockSpec((RET_C, C_QK_WIDTH), lambda b, j: (b * nj + j, P16_CQ)),
            pl.BlockSpec((RET_C, C_QK_WIDTH), lambda b, j: (b * nj + j, P16_CK)),
            pl.BlockSpec((RET_C, C_V_WIDTH), lambda b, j: (b * nj + j, P16_CV // v_blk)),
            pl.BlockSpec((RET_C, C_V_WIDTH), lambda b, j: (b * nj + j, P16_CG // v_blk)),
            pl.BlockSpec((RET_C, C_QK_DIM), lambda b, j: (j, 0)),
            pl.BlockSpec((RET_C, C_QK_DIM), lambda b, j: (j, 0)),
            whole(idec), whole(qdec), whole(kdec), whole(cdec),
        ],
        out_specs=pl.BlockSpec((RET_C, C_V_WIDTH), lambda b, j: (b * nj + j, 0)),
        out_shape=jax.ShapeDtypeStruct((n, C_V_WIDTH), BF16),
        scratch_shapes=[pltpu.VMEM((C_HEADS, C_QK_DIM, C_V_DIM), F32)],
        compiler_params=_cparams(("arbitrary", "arbitrary")),
        name="retention",
    )(p16, p16, p16, p16, cos, sin, idec, qdec, kdec, cdec)


def _merge_kernel(oa_ref, ob_ref, oc_ref, ga_ref, gb_ref, gc_ref, x_ref, mod_ref, g2_ref,
                  wa_ref, wb_ref, wc_ref, wo_ref, wrh_ref, wrl_ref, br_ref,
                  x1_ref, h2_ref, route_ref, *, tiles_per_batch):
    b = pl.program_id(0) // tiles_per_batch

    def gated(o_ref, w_ref, g_ref):
        y = jnp.dot(o_ref[...], w_ref[...], preferred_element_type=F32)
        return jax.nn.sigmoid(g_ref[...].astype(F32)) * y

    merged = gated(oa_ref, wa_ref, ga_ref) + gated(ob_ref, wb_ref, gb_ref) + gated(oc_ref, wc_ref, gc_ref)
    y = jnp.dot(merged.astype(BF16), wo_ref[...], preferred_element_type=F32)
    gt1 = mod_ref[pl.ds(b, 1), 2 * D_MODEL:3 * D_MODEL]
    x1 = x_ref[...] + gt1 * y
    x1_ref[...] = x1
    sh2 = mod_ref[pl.ds(b, 1), 3 * D_MODEL:4 * D_MODEL]
    sc2 = mod_ref[pl.ds(b, 1), 4 * D_MODEL:5 * D_MODEL]
    h2 = _rms_mod(x1, g2_ref[...], sc2, sh2)
    h_hi = h2.astype(BF16)
    words = _pack_bf16_pairs(h2)
    for j in range(h2_ref.shape[0]):
        h2_ref[j] = words[:, j * SC_SUB:(j + 1) * SC_SUB]
    h_lo = (h2 - h_hi.astype(F32)).astype(BF16)
    logits = (jnp.dot(h_hi, wrh_ref[...], preferred_element_type=F32)
              + jnp.dot(h_lo, wrh_ref[...], preferred_element_type=F32)
              + jnp.dot(h_hi, wrl_ref[...], preferred_element_type=F32)) + br_ref[...]
    lane = lax.broadcasted_iota(jnp.int32, logits.shape, 1).astype(F32)
    neg_inf = -jnp.inf

    def first_argmax(vals):
        top = jnp.max(vals, axis=-1, keepdims=True)
        idx = jnp.min(jnp.where(vals == top, lane, float(LANES)), axis=-1, keepdims=True)
        return top, idx

    gl = jnp.where(lane < N_GROUPS, logits, neg_inf)
    gmax, gsel = first_argmax(gl)
    gprob = 1.0 / jnp.sum(jnp.exp(gl - gmax), axis=-1, keepdims=True)
    lo = N_GROUPS + EXPERTS_PER_GROUP * gsel
    el = jnp.where((lane >= lo) & (lane < lo + EXPERTS_PER_GROUP), logits, neg_inf)
    v1, i1 = first_argmax(el)
    el2 = jnp.where(lane == i1, neg_inf, el)
    v2, i2 = first_argmax(el2)
    e2 = jnp.exp(v2 - v1)
    den = 1.0 + e2
    route_ref[...] = jnp.where(lane == 0.0, i1 - N_GROUPS,
                               jnp.where(lane == 1.0, i2 - N_GROUPS,
                                         jnp.where(lane == 2.0, gprob / den,
                                                   jnp.where(lane == 3.0, gprob * (e2 / den), 0.0))))


def _merge(o_a, o_b, o_c, p16, x2, mod_l, g2, wa, wb, wc, wo, wr_hi, wr_lo, br, seq):
    n = x2.shape[0]
    tm = min(512, seq)
    one = pl.Buffered(1)

    def rows(width, cb=0):
        return pl.BlockSpec((tm, width), lambda i: (i, cb))

    def whole(a):
        return pl.BlockSpec(a.shape, lambda i: (0,) * a.ndim, pipeline_mode=one)

    return pl.pallas_call(
        functools.partial(_merge_kernel, tiles_per_batch=seq // tm),
        grid=(n // tm,),
        in_specs=[rows(A_WIDTH), rows(B_WIDTH), rows(C_V_WIDTH),
                  rows(D_MODEL, P16_GA), rows(D_MODEL, P16_GB), rows(D_MODEL, P16_GC),
                  rows(D_MODEL), whole(mod_l), pl.BlockSpec((1, D_MODEL), lambda i: (0, 0)),
                  whole(wa), whole(wb), whole(wc), whole(wo), whole(wr_hi), whole(wr_lo), whole(br)],
        out_specs=[rows(D_MODEL), pl.BlockSpec((SC_PIECES, tm, SC_SUB), lambda i: (0, i, 0)), rows(LANES)],
        out_shape=[jax.ShapeDtypeStruct((n, D_MODEL), F32),
                   jax.ShapeDtypeStruct((SC_PIECES, n, SC_SUB), jnp.int32),
                   jax.ShapeDtypeStruct((n, LANES), F32)],
        compiler_params=_cparams(("arbitrary",)),
        name="merge_route",
    )(o_a, o_b, o_c, p16, p16, p16, x2, mod_l, g2.reshape(1, D_MODEL), wa, wb, wc, wo, wr_hi, wr_lo, br)


def _lane_pick(vals, lane, idx):
    return jnp.sum(jnp.where(lane == idx, vals, 0.0), axis=-1, keepdims=True)


def _rank_kernel(route_ref, rk_ref, cnt_ref, run_ref):
    @pl.when(pl.program_id(0) == 0)
    def _():
        run_ref[...] = jnp.zeros_like(run_ref)

    route = route_ref[...]
    tb = route.shape[0]
    lane = lax.broadcasted_iota(jnp.int32, route.shape, 1).astype(F32)
    e1 = _lane_pick(route, lane, 0.0)
    e2 = _lane_pick(route, lane, 1.0)
    sel = jnp.where((lane == e1) | (lane == e2), 1.0, 0.0)
    r = lax.broadcasted_iota(jnp.int32, (tb, tb), 0)
    c = lax.broadcasted_iota(jnp.int32, (tb, tb), 1)
    before = jnp.where(c < r, 1.0, 0.0).astype(BF16)
    rank = jnp.dot(before, sel.astype(BF16), preferred_element_type=F32) + run_ref[0:1, :]
    rk_ref[...] = jnp.where(lane == 0.0, _lane_pick(rank, lane, e1),
                            jnp.where(lane == 1.0, _lane_pick(rank, lane, e2), 0.0))
    run_ref[...] = run_ref[...] + jnp.sum(sel, axis=0, keepdims=True)
    cnt_ref[...] = run_ref[...]


def _expert_ranks(route):
    n = route.shape[0]
    tb = min(PLAN_TB, n)
    return pl.pallas_call(
        _rank_kernel,
        grid=(n // tb,),
        in_specs=[pl.BlockSpec((tb, LANES), lambda i: (i, 0))],
        out_specs=[pl.BlockSpec((tb, LANES), lambda i: (i, 0)),
                   pl.BlockSpec((8, LANES), lambda i: (0, 0))],
        out_shape=[jax.ShapeDtypeStruct((n, LANES), F32), jax.ShapeDtypeStruct((8, LANES), F32)],
        scratch_shapes=[pltpu.VMEM((8, LANES), F32)],
        compiler_params=_cparams(("arbitrary",)),
        name="expert_ranks",
    )(route)


def _plan_kernel(cnt_ref, route_ref, rk_ref, pos_ref, tmap_ref):
    lane_i = lax.broadcasted_iota(jnp.int32, (8, LANES), 1)
    cnt = jnp.where(lane_i < N_EXPERTS, cnt_ref[...], 0.0)
    padded = jnp.floor((cnt + (MOE_TM - 1)) * (1.0 / MOE_TM)) * MOE_TM
    r = lax.broadcasted_iota(jnp.int32, (LANES, LANES), 0)
    c = lax.broadcasted_iota(jnp.int32, (LANES, LANES), 1)
    base = jnp.dot(padded, jnp.where(r < c, 1.0, 0.0), precision=lax.Precision.HIGHEST,
                   preferred_element_type=F32)

    route = route_ref[...]
    lane = lax.broadcasted_iota(jnp.int32, route.shape, 1).astype(F32)
    rk = rk_ref[...]
    base_row = base[0:1, :]
    pos1 = _lane_pick(base_row, lane, _lane_pick(route, lane, 0.0)) + _lane_pick(rk, lane, 0.0)
    pos2 = _lane_pick(base_row, lane, _lane_pick(route, lane, 1.0)) + _lane_pick(rk, lane, 1.0)
    pos_ref[...] = jnp.where(lane == 0.0, pos1, jnp.where(lane == 1.0, pos2, 0.0)).astype(jnp.int32)

    @pl.when(pl.program_id(0) == 0)
    def _():
        nt = tmap_ref.shape[0]
        tlane = lax.broadcasted_iota(jnp.int32, (nt, LANES), 1)
        start = (lax.broadcasted_iota(jnp.int32, (nt, LANES), 0) * MOE_TM).astype(F32)
        end_row = jnp.where(tlane < N_EXPERTS, base_row + padded[0:1, :], 3e38)
        expert = jnp.sum(jnp.where(end_row <= start, 1.0, 0.0), axis=-1, keepdims=True)
        expert_c = jnp.minimum(expert, N_EXPERTS - 1.0)
        tl = tlane.astype(F32)
        left = _lane_pick(cnt[0:1, :], tl, expert_c) - (start[:, 0:1] - _lane_pick(base_row, tl, expert_c))
        valid = jnp.where(expert < N_EXPERTS, jnp.clip(left, 0.0, float(MOE_TM)), 0.0)
        tmap_ref[...] = jnp.where(tlane == 0, expert_c, jnp.where(tlane == 1, valid, 0.0)).astype(jnp.int32)


def _expert_plan(cnt, route, rk, n_tiles):
    n = route.shape[0]
    tb = min(PLAN_TB, n)
    nt_pad = -(-n_tiles // 8) * 8
    return pl.pallas_call(
        _plan_kernel,
        grid=(n // tb,),
        in_specs=[pl.BlockSpec((8, LANES), lambda i: (0, 0)),
                  pl.BlockSpec((tb, LANES), lambda i: (i, 0)),
                  pl.BlockSpec((tb, LANES), lambda i: (i, 0))],
        out_specs=[pl.BlockSpec((tb, LANES), lambda i: (i, 0)),
                   pl.BlockSpec((nt_pad, LANES), lambda i: (0, 0))],
        out_shape=[jax.ShapeDtypeStruct((n, LANES), jnp.int32),
                   jax.ShapeDtypeStruct((nt_pad, LANES), jnp.int32)],
        compiler_params=_cparams(("arbitrary",)),
        name="expert_plan",
    )(cnt, route, rk)


def _sc_mesh():
    return plsc.VectorSubcoreMesh(core_axis_name="c", subcore_axis_name="s")


def _sc_scatter_rows(src, idx, out_rows):
    m = idx.shape[0]
    n_src_win = src.shape[0] // SC_WINDOW

    @functools.partial(pl.kernel, out_type=jax.ShapeDtypeStruct((out_rows, src.shape[1]), src.dtype),
                       mesh=_sc_mesh(), scratch_types=[])
    def scatter(x_hbm, i_hbm, o_hbm):
        def body(x_vmem, i_vmem):
            pltpu.sync_copy(x_vmem, o_hbm.at[i_vmem.at[0]])

        pltpu.emit_pipeline(
            body, grid=(m // SC_WINDOW,),
            in_specs=[pl.BlockSpec((SC_WINDOW, src.shape[1]), lambda i: (i % n_src_win, 0)),
                      pl.BlockSpec((1, SC_WINDOW), lambda i: (0, i))],
            out_specs=[], core_axis_name=("c", "s"),
            dimension_semantics=(pltpu.PARALLEL,))(x_hbm, i_hbm)

    return scatter(src, idx.reshape(1, m))


def _sc_gather_rows(table, idx):
    m = idx.shape[0]

    @functools.partial(pl.kernel, out_type=jax.ShapeDtypeStruct((m, table.shape[1]), table.dtype),
                       mesh=_sc_mesh(), scratch_types=[])
    def gather(x_hbm, i_hbm, o_hbm):
        def body(i_vmem, o_vmem):
            pltpu.sync_copy(x_hbm.at[i_vmem.at[0]], o_vmem)

        pltpu.emit_pipeline(
            body, grid=(m // SC_WINDOW,),
            in_specs=[pl.BlockSpec((1, SC_WINDOW), lambda i: (0, i))],
            out_specs=[pl.BlockSpec((SC_WINDOW, table.shape[1]), lambda i: (i, 0))],
            core_axis_name=("c", "s"),
            dimension_semantics=(pltpu.PARALLEL,))(i_hbm, o_hbm)

    return gather(table, idx.reshape(1, m))


def _piece_row_index(pos, rows):
    return (jnp.arange(SC_PIECES, dtype=jnp.int32)[:, None] * rows + pos[None, :]).reshape(-1)


def _grouped_kernel(te_ref, tv_ref, x_ref, wg_ref, wu_ref, wd_ref, o_ref):
    valid = tv_ref[pl.program_id(0)]

    @pl.when(valid > 0)
    def _():
        words = jnp.concatenate([x_ref[j] for j in range(SC_PIECES)], axis=1)
        row = lax.broadcasted_iota(jnp.int32, words.shape, 0)
        words = jnp.where(row < valid, words, 0)
        x = _unpack_bf16_pairs(words).astype(BF16)
        a = jnp.dot(x, wg_ref[0].astype(BF16), preferred_element_type=F32)
        u = jnp.dot(x, wu_ref[0].astype(BF16), preferred_element_type=F32)
        hm = (_silu(a) * u).astype(BF16)
        out = _pack_bf16_pairs(jnp.dot(hm, wd_ref[0].astype(BF16), preferred_element_type=F32))
        for j in range(SC_PIECES):
            o_ref[j] = out[:, j * SC_SUB:(j + 1) * SC_SUB]

    @pl.when(valid <= 0)
    def _():
        o_ref[...] = jnp.zeros_like(o_ref)


def _grouped_experts(tile_expert, tile_valid, xs, wg, wu, wd, layer):
    n_tiles = tile_expert.shape[0]
    rows_block = pl.BlockSpec((SC_PIECES, MOE_TM, SC_SUB), lambda i, te, tv: (0, i, 0))
    return pl.pallas_call(
        _grouped_kernel,
        grid_spec=pltpu.PrefetchScalarGridSpec(
            num_scalar_prefetch=2,
            grid=(n_tiles,),
            in_specs=[rows_block,
                      pl.BlockSpec((None, 1, D_MODEL, D_EXPERT), lambda i, te, tv: (layer, te[i], 0, 0)),
                      pl.BlockSpec((None, 1, D_MODEL, D_EXPERT), lambda i, te, tv: (layer, te[i], 0, 0)),
                      pl.BlockSpec((None, 1, D_EXPERT, D_MODEL), lambda i, te, tv: (layer, te[i], 0, 0))],
            out_specs=rows_block),
        out_shape=jax.ShapeDtypeStruct(xs.shape, jnp.int32),
        compiler_params=_cparams(("arbitrary",)),
        name="grouped_experts",
    )(tile_expert, tile_valid, xs, wg, wu, wd)


def _combine_kernel(x1_ref, y_ref, route_ref, mod_ref, o_ref, *, tiles_per_batch):
    b = pl.program_id(0) // tiles_per_batch
    gt2 = mod_ref[pl.ds(b, 1), 5 * D_MODEL:6 * D_MODEL]
    route = route_ref[...]
    lane = lax.broadcasted_iota(jnp.int32, route.shape, 1).astype(F32)
    w1 = _lane_pick(route, lane, 2.0)
    w2 = _lane_pick(route, lane, 3.0)
    y1 = _unpack_bf16_pairs(jnp.concatenate([y_ref[0, j] for j in range(SC_PIECES)], axis=1))
    y2 = _unpack_bf16_pairs(jnp.concatenate([y_ref[1, j] for j in range(SC_PIECES)], axis=1))
    o_ref[...] = x1_ref[...] + gt2 * (w1 * y1 + w2 * y2)


def _combine(x1, y2, route, mod_l, seq):
    n = x1.shape[0]
    tm = min(1024, seq)
    return pl.pallas_call(
        functools.partial(_combine_kernel, tiles_per_batch=seq // tm),
        grid=(n // tm,),
        in_specs=[pl.BlockSpec((tm, D_MODEL), lambda i: (i, 0)),
                  pl.BlockSpec((2, SC_PIECES, tm, SC_SUB), lambda i: (0, 0, i, 0)),
                  pl.BlockSpec((tm, LANES), lambda i: (i, 0)),
                  pl.BlockSpec(mod_l.shape, lambda i: (0, 0))],
        out_specs=pl.BlockSpec((tm, D_MODEL), lambda i: (i, 0)),
        out_shape=jax.ShapeDtypeStruct((n, D_MODEL), F32),
        compiler_params=_cparams(("arbitrary",)),
        name="moe_combine",
    )(x1, y2, route, mod_l)


def _moe(h2, route, x1, mod_l, wg, wu, wd, layer, seq):
    n = h2.shape[1]
    n_tiles = (2 * n) // MOE_TM + N_EXPERTS
    rows = n_tiles * MOE_TM
    rk, cnt = _expert_ranks(route)
    pos, tmap = _expert_plan(cnt, route, rk, n_tiles)
    idx = jnp.concatenate([_piece_row_index(pos[:, 0], rows), _piece_row_index(pos[:, 1], rows)])
    xs = _sc_scatter_rows(h2.reshape(SC_PIECES * n, SC_SUB), idx, SC_PIECES * rows)
    ys = _grouped_experts(tmap[:n_tiles, 0], tmap[:n_tiles, 1], xs.reshape(SC_PIECES, rows, SC_SUB),
                          wg, wu, wd, layer)
    y2 = _sc_gather_rows(ys.reshape(SC_PIECES * rows, SC_SUB), idx).reshape(2, SC_PIECES, n, SC_SUB)
    return _combine(x1, y2, route, mod_l, seq)


def _final_norm_kernel(x_ref, g_ref, o_ref):
    x = x_ref[...]
    ms = jnp.mean(x * x, axis=-1, keepdims=True)
    o_ref[...] = x * lax.rsqrt(ms + EPS) * g_ref[...]


def _final_norm(x2, g, seq):
    n = x2.shape[0]
    tm = min(1024, seq)
    return pl.pallas_call(
        _final_norm_kernel,
        grid=(n // tm,),
        in_specs=[pl.BlockSpec((tm, D_MODEL), lambda i: (i, 0)),
                  pl.BlockSpec((1, D_MODEL), lambda i: (0, 0))],
        out_specs=pl.BlockSpec((tm, D_MODEL), lambda i: (i, 0)),
        out_shape=jax.ShapeDtypeStruct((n, D_MODEL), F32),
        compiler_params=_cparams(("arbitrary",)),
        name="final_norm",
    )(x2, g.reshape(1, D_MODEL))


_IN_OFFS = [sum(IN_SPLITS[:i]) for i in range(len(IN_SPLITS) + 1)]
(_AQ, _AK, _AV, _IQ, _IK, _IW, _BQ, _BF, _BI, _BG, _CQ, _CK, _CV, _CG, _GA, _GB, _GC) = range(len(IN_SPLITS))


def _pack_kernel(w_ref, w16_ref, w32_ref, wvt_ref):
    def cols(seg):
        return w_ref[0, :, _IN_OFFS[seg]:_IN_OFFS[seg + 1]]

    scale = {_AQ: A_HEAD_DIM ** -0.5 * LOG2_E, _CK: C_QK_DIM ** -0.5}
    at = 0
    for seg in (_CV, _CG, _AQ, _AK, _BQ, _BI, _BG, _CQ, _CK, _GA, _GB, _GC):
        v = cols(seg)
        if seg in scale:
            v = v * scale[seg]
        w16_ref[0, :, at:at + v.shape[1]] = v.astype(BF16)
        at += v.shape[1]

    rows = w_ref.shape[1]
    w32_ref[0, :, P32_BF:P32_BF + B_WIDTH] = cols(_BF).astype(BF16)
    iq = cols(_IQ)
    zeros = jnp.zeros((rows, LANES - IDX_DIM), F32)
    for h in range(IDX_HEADS):
        w32_ref[0, :, P32_IQ + h * LANES:P32_IQ + (h + 1) * LANES] = jnp.concatenate(
            [iq[:, h * IDX_DIM:(h + 1) * IDX_DIM], zeros], axis=1).astype(BF16)
    w32_ref[0, :, P32_IK:P32_IK + LANES] = jnp.concatenate([cols(_IK), zeros], axis=1).astype(BF16)
    w32_ref[0, :, P32_IW:P32_IW + LANES] = jnp.concatenate(
        [cols(_IW), jnp.zeros((rows, LANES - IDX_HEADS), F32)], axis=1).astype(BF16)
    wvt_ref[0] = cols(_AV).T.astype(BF16)


def _pack_w_in(w_in):
    depth, d, width = w_in.shape
    rows = LANES
    w16_width = 14 * 1024
    w32_width = P32_IW + LANES
    return pl.pallas_call(
        _pack_kernel,
        grid=(depth, d // rows),
        in_specs=[pl.BlockSpec((1, rows, width), lambda l, r: (l, r, 0))],
        out_specs=[pl.BlockSpec((1, rows, w16_width), lambda l, r: (l, r, 0)),
                   pl.BlockSpec((1, rows, w32_width), lambda l, r: (l, r, 0)),
                   pl.BlockSpec((1, A_WIDTH, rows), lambda l, r: (l, 0, r))],
        out_shape=[jax.ShapeDtypeStruct((depth, d, w16_width), BF16),
                   jax.ShapeDtypeStruct((depth, d, w32_width), BF16),
                   jax.ShapeDtypeStruct((depth, A_WIDTH, d), BF16)],
        compiler_params=_cparams(("arbitrary", "arbitrary")),
        name="pack_w_in",
    )(w_in)


def _split_bf16(w):
    hi = w.astype(BF16)
    return hi, (w - hi.astype(F32)).astype(BF16)


def kernel(x, c, rel_bias, hgrn_lb_raw, norm1_g, norm2_g, ada_w, ada_b, w_in, hgrn_norm_g, w_branch_a,
           w_branch_b, w_branch_c, w_out, router_group_w, router_group_b, router_expert_w,
           router_expert_b, expert_w_gate, expert_w_up, expert_w_down, final_norm_g):
    bsz, seq, _ = x.shape
    depth = w_in.shape[0]
    n = bsz * seq
    x2 = x.reshape(n, D_MODEL)
    tq = min(DSA_TQ, seq)

    lb_all = _hgrn_lower_bounds(hgrn_lb_raw)
    c_pad = jnp.pad(c, ((0, (-bsz) % 8), (0, 0)))
    mod = _ada_mod(c_pad, ada_w, ada_b)
    bias_tiles = _bias_tiles(rel_bias, tq)
    ret_tables = _retention_tables(seq)
    w16_all, w32_all, wvt_all = _pack_w_in(w_in)

    for l in range(depth):
        p16 = _norm_project(x2, mod[l], norm1_g[l], w16_all, l, BF16, 1024, seq, "proj_bf16")
        p32 = _norm_project(x2, mod[l], norm1_g[l], w32_all, l, F32, 768, seq, "proj_f32")
        vt = _norm_project_t(x2, mod[l], norm1_g[l], wvt_all, l, tq, seq, "proj_vt")
        o_a = _dsa_attention(p16, p32, vt, bias_tiles, bsz, seq)
        o_b = _hgrn2(p16, p32, lb_all[l], hgrn_norm_g[l], bsz, seq)
        o_c = _retention(p16, ret_tables, bsz, seq)
        wr = jnp.concatenate([router_group_w[l], router_expert_w[l],
                              jnp.zeros((D_MODEL, LANES - N_GROUPS - N_EXPERTS), F32)], axis=1)
        br = jnp.concatenate([router_group_b[l], router_expert_b[l],
                              jnp.zeros((LANES - N_GROUPS - N_EXPERTS,), F32)]).reshape(1, LANES)
        wr_hi, wr_lo = _split_bf16(wr)
        x1, h2, route = _merge(o_a, o_b, o_c, p16, x2, mod[l], norm2_g[l],
                               w_branch_a[l].astype(BF16), w_branch_b[l].astype(BF16),
                               w_branch_c[l].astype(BF16), w_out[l].astype(BF16),
                               wr_hi, wr_lo, br, seq)
        x2 = _moe(h2, route, x1, mod[l], expert_w_gate, expert_w_up, expert_w_down, l, seq)

    return _final_norm(x2, final_norm_g, seq).reshape(bsz, seq, D_MODEL)
```

```python
import functools
import math

import jax
import jax.numpy as jnp
from jax import lax
from jax.experimental import pallas as pl
from jax.experimental.pallas import tpu as pltpu
from jax.experimental.pallas import tpu_sc as plsc

F32 = jnp.float32
BF16 = jnp.bfloat16

D_MODEL = 1024
A_HEADS = 8
A_HEAD_DIM = 128
IDX_HEADS = 8
IDX_DIM = 64
TOPK_MAX = 256
REL_BUCKETS = 32
REL_MAX_DIST = 128
B_HEADS = 8
B_HEAD_DIM = 128
C_HEADS = 4
C_QK_DIM = 256
C_V_DIM = 512
N_GROUPS = 4
EXPERTS_PER_GROUP = 8
N_EXPERTS = 32
D_EXPERT = 512
EPS = 1e-6

A_WIDTH = A_HEADS * A_HEAD_DIM
B_WIDTH = B_HEADS * B_HEAD_DIM
C_QK_WIDTH = C_HEADS * C_QK_DIM
C_V_WIDTH = C_HEADS * C_V_DIM
IN_SPLITS = (A_WIDTH, A_WIDTH, A_WIDTH, IDX_HEADS * IDX_DIM, IDX_DIM, IDX_HEADS,
             B_WIDTH, B_WIDTH, B_WIDTH, B_WIDTH,
             C_QK_WIDTH, C_QK_WIDTH, C_V_WIDTH, C_V_WIDTH,
             D_MODEL, D_MODEL, D_MODEL)

LANES = 128
BF16_ROWS = 16
VMEM_LIMIT = 56 * 1024 * 1024

P16_CV, P16_CG = 0, 2
P16_AQ, P16_AK, P16_BQ, P16_BI, P16_BG, P16_CQ, P16_CK, P16_GA, P16_GB, P16_GC = range(4, 14)
P32_BF = 0
P32_IQ = 1024
P32_IK = 2048
P32_IW = 2176

DSA_TQ = 256
HGRN_L = 256
HGRN_C = 64
HGRN_SB = 16
HGRN_MAX_LOG_DECAY = 80.0
RET_C = 256
KEY_NEG_INF = -2139095041
HALF_BIAS = 32768
MASK_NEG = -1e30
LOG2_E = math.log2(math.e)
COUNT_CHAINS = 4
MOE_TM = 256
PLAN_TB = 1024
SC_WINDOW = 128
SC_SUB = 256
SC_PIECES = D_MODEL // 2 // SC_SUB

NT_DIMS = (((1,), (1,)), ((), ()))
TN_DIMS = (((0,), (0,)), ((), ()))


def _cparams(sem):
    return pltpu.CompilerParams(dimension_semantics=sem, vmem_limit_bytes=VMEM_LIMIT)


def _silu(x):
    return x * jax.nn.sigmoid(x)


def _pack_bf16_pairs(x):
    k = x.shape[1] // 2
    bits = pltpu.bitcast(x.astype(BF16).astype(F32), jnp.int32)
    return (bits[:, :k] & jnp.int32(-65536)) | lax.shift_right_logical(bits[:, k:], 16)


def _unpack_bf16_pairs(words):
    hi = pltpu.bitcast(words & jnp.int32(-65536), F32)
    lo = pltpu.bitcast(lax.shift_left(words, 16), F32)
    return jnp.concatenate([hi, lo], axis=1)


def _lb_kernel(raw_ref, o_ref):
    raw = raw_ref[...]
    m = jnp.max(raw, axis=0, keepdims=True)
    e = jnp.exp(raw - m)
    soft = e / jnp.sum(e, axis=0, keepdims=True)
    run = jnp.zeros_like(soft[0:1])
    for l in range(raw.shape[0]):
        run = run + soft[l:l + 1]
        o_ref[l:l + 1, :] = run - soft[0:1]


def _hgrn_lower_bounds(raw):
    return pl.pallas_call(
        _lb_kernel, out_shape=jax.ShapeDtypeStruct(raw.shape, F32), name="hgrn_lb")(raw)


def _ada_kernel(c_ref, w_ref, b_ref, o_ref):
    a = _silu(c_ref[...])
    o_ref[0] = jnp.dot(a, w_ref[0], precision=lax.Precision.HIGHEST,
                       preferred_element_type=F32) + b_ref[0]


def _ada_mod(c_pad, ada_w, ada_b):
    depth = ada_w.shape[0]
    rows = c_pad.shape[0]
    return pl.pallas_call(
        _ada_kernel,
        grid=(depth, 6),
        in_specs=[pl.BlockSpec((rows, D_MODEL), lambda l, j: (0, 0)),
                  pl.BlockSpec((1, D_MODEL, D_MODEL), lambda l, j: (l, 0, j)),
                  pl.BlockSpec((1, 1, D_MODEL), lambda l, j: (l, 0, j))],
        out_specs=pl.BlockSpec((1, rows, D_MODEL), lambda l, j: (l, 0, j)),
        out_shape=jax.ShapeDtypeStruct((depth, rows, 6 * D_MODEL), F32),
        compiler_params=_cparams(("arbitrary", "arbitrary")),
        name="ada_mod",
    )(c_pad, ada_w, ada_b.reshape(depth, 1, 6 * D_MODEL))


def _rms_mod(x, g, sc, sh):
    ms = jnp.mean(x * x, axis=-1, keepdims=True)
    return (x * lax.rsqrt(ms + EPS) * g) * (1.0 + sc) + sh


def _norm1(x_ref, mod_ref, g_ref, b):
    sh = mod_ref[pl.ds(b, 1), 0:D_MODEL]
    sc = mod_ref[pl.ds(b, 1), D_MODEL:2 * D_MODEL]
    return _rms_mod(x_ref[...], g_ref[...], sc, sh).astype(BF16)


def _proj_kernel(x_ref, mod_ref, g_ref, w_ref, o_ref, h_ref, *, tiles_per_batch):
    @pl.when(pl.program_id(1) == 0)
    def _():
        h_ref[...] = _norm1(x_ref, mod_ref, g_ref, pl.program_id(0) // tiles_per_batch)

    o_ref[...] = jnp.dot(h_ref[...], w_ref[...], preferred_element_type=F32).astype(o_ref.dtype)


def _norm_project(x2, mod_l, g, w_all, layer, out_dtype, tn, seq, name):
    n = x2.shape[0]
    width = w_all.shape[2]
    tm = min(1024, seq)
    return pl.pallas_call(
        functools.partial(_proj_kernel, tiles_per_batch=seq // tm),
        grid=(n // tm, width // tn),
        in_specs=[pl.BlockSpec((tm, D_MODEL), lambda i, j: (i, 0)),
                  pl.BlockSpec(mod_l.shape, lambda i, j: (0, 0)),
                  pl.BlockSpec((1, D_MODEL), lambda i, j: (0, 0)),
                  pl.BlockSpec((None, D_MODEL, tn), lambda i, j: (layer, 0, j))],
        out_specs=pl.BlockSpec((tm, tn), lambda i, j: (i, j)),
        out_shape=jax.ShapeDtypeStruct((n, width), out_dtype),
        scratch_shapes=[pltpu.VMEM((tm, D_MODEL), BF16)],
        compiler_params=_cparams(("arbitrary", "arbitrary")),
        name=name,
    )(x2, mod_l, g.reshape(1, D_MODEL), w_all)


def _proj_t_kernel(x_ref, mod_ref, g_ref, wt_ref, o_ref, *, tiles_per_batch, chunk):
    h = _norm1(x_ref, mod_ref, g_ref, pl.program_id(0) // tiles_per_batch)
    res = lax.dot_general(wt_ref[...], h, NT_DIMS, preferred_element_type=F32)
    for ci in range(o_ref.shape[0]):
        o_ref[ci] = res[:, ci * chunk:(ci + 1) * chunk].astype(o_ref.dtype)


def _norm_project_t(x2, mod_l, g, wt_all, layer, chunk, seq, name):
    n = x2.shape[0]
    cols = wt_all.shape[1]
    tm = min(1024, seq)
    return pl.pallas_call(
        functools.partial(_proj_t_kernel, tiles_per_batch=seq // tm, chunk=chunk),
        grid=(n // tm,),
        in_specs=[pl.BlockSpec((tm, D_MODEL), lambda i: (i, 0)),
                  pl.BlockSpec(mod_l.shape, lambda i: (0, 0)),
                  pl.BlockSpec((1, D_MODEL), lambda i: (0, 0)),
                  pl.BlockSpec((None, cols, D_MODEL), lambda i: (layer, 0, 0))],
        out_specs=pl.BlockSpec((tm // chunk, cols, chunk), lambda i: (i, 0, 0)),
        out_shape=jax.ShapeDtypeStruct((n // chunk, cols, chunk), BF16),
        compiler_params=_cparams(("arbitrary",)),
        name=name,
    )(x2, mod_l, g.reshape(1, D_MODEL), wt_all)


def _dsa_kernel(q_ref, iq_ref, iw_ref, k_ref, vt_ref, ik_ref, bias_ref, o_ref,
                key_ref, hi_ref, lo_ref, madd_ref, qt_ref, iqt_ref, iwt_ref, m_ref, l_ref, acc_ref, s_ref,
                *, tq, topk):
    qi = pl.program_id(1)
    nck = qi + 1
    idx_scale = (IDX_HEADS * IDX_DIM) ** -0.5

    for h in range(A_HEADS):
        hs = slice(h * LANES, (h + 1) * LANES)
        qt_ref[hs, :] = q_ref[:, hs].astype(F32).T.astype(BF16)
        iqt_ref[hs, :] = iq_ref[:, hs].T.astype(BF16)
    iwt_ref[...] = (iw_ref[...] * idx_scale).T

    krow = lax.broadcasted_iota(jnp.int32, (tq, tq), 0)
    qcol = lax.broadcasted_iota(jnp.int32, (tq, tq), 1)

    def score_chunk(c, carry):
        off = pl.multiple_of(c * tq, tq)
        ikc = ik_ref[pl.ds(off, tq), :].astype(BF16)
        acc = jnp.zeros((tq, tq), F32)
        for h in range(IDX_HEADS):
            s = jnp.dot(ikc, iqt_ref[h * LANES:(h + 1) * LANES, :], preferred_element_type=F32)
            acc = acc + jnp.maximum(s, 0.0) * iwt_ref[h:h + 1, :]
        acc = jnp.where(acc == 0.0, 0.0, acc)
        acc = jnp.where(krow + (c - qi) * tq <= qcol, acc, -jnp.inf)
        kb = pltpu.bitcast(acc, jnp.int32)
        key = jnp.where(kb < 0, kb ^ jnp.int32(0x7FFFFFFF), kb)
        key_ref[c] = key
        hi_ref[c] = jnp.right_shift(key, 16).astype(jnp.int16)
        lo_ref[c] = ((key & 0xFFFF) - HALF_BIAS).astype(jnp.int16)
        return carry

    def paired_loop(count_, fn):
        def pair(i, carry):
            fn(2 * i, carry)
            fn(2 * i + 1, carry)
            return carry

        lax.fori_loop(0, count_ // 2, pair, 0)

        @pl.when(count_ % 2 == 1)
        def _():
            fn(count_ - 1, 0)

    paired_loop(nck, score_chunk)

    def count(ref, pred_fn, rows, zero, one):
        def body(c, parts):
            hit = jnp.where(pred_fn(ref[c]), one, zero)
            parts = list(parts)
            for r in range(tq // rows):
                parts[r % COUNT_CHAINS] = parts[r % COUNT_CHAINS] + hit[r * rows:(r + 1) * rows, :]
            return tuple(parts)

        parts = lax.fori_loop(0, nck, body, (jnp.full((rows, tq), zero),) * COUNT_CHAINS)
        return jnp.sum(sum(p.astype(F32) for p in parts), axis=0, keepdims=True)

    def count16(ref, pred_fn):
        return count(ref, pred_fn, 16, jnp.int16(0), jnp.int16(1))

    def count32(pred_fn):
        return count(key_ref, pred_fn, 8, jnp.float32(0.0), jnp.float32(1.0))

    def bisect16(ref, target):
        def bit_step(i, theta):
            cand = theta + jnp.left_shift(jnp.int32(1), 15 - i)
            cand16 = cand.astype(jnp.int16)
            return jnp.where(count16(ref, lambda k: k >= cand16) >= target, cand, theta)

        return lax.fori_loop(0, 16, bit_step, jnp.full((1, tq), -HALF_BIAS, jnp.int32))

    theta_hi = bisect16(hi_ref, float(topk))
    theta_hi16 = theta_hi.astype(jnp.int16)
    need_lo = topk - count16(hi_ref, lambda k: k > theta_hi16)

    def bucket_chunk(c, carry):
        lo_ref[c] = jnp.where(hi_ref[c] == theta_hi16, lo_ref[c], jnp.int16(-HALF_BIAS))
        return carry

    lax.fori_loop(0, nck, bucket_chunk, 0)
    theta_lo = bisect16(lo_ref, need_lo)
    theta = theta_hi * (2 * HALF_BIAS) + (theta_lo + HALF_BIAS)
    theta = jnp.maximum(theta, KEY_NEG_INF + 1)

    def mask_chunk(c, cnt):
        ge = key_ref[c] >= theta
        madd_ref[c] = jnp.where(ge, 0.0, MASK_NEG)
        return cnt + jnp.sum(jnp.where(ge, 1.0, 0.0), axis=0, keepdims=True)

    cnt_ge = lax.fori_loop(0, nck, mask_chunk, jnp.zeros((1, tq), F32))

    @pl.when(jnp.max(cnt_ge) > topk)
    def _():
        need_eq = topk - count32(lambda kc: kc > theta)
        incl = jnp.where(krow >= qcol, 1.0, 0.0).astype(BF16)

        def tie_chunk(c, run):
            kc = key_ref[c]
            eq = kc == theta
            eqf = jnp.where(eq, 1.0, 0.0)
            pref = jnp.dot(incl, eqf.astype(BF16), preferred_element_type=F32) + run
            eq_add = jnp.where(pref <= need_eq, 0.0, MASK_NEG)
            madd_ref[c] = jnp.where(eq, eq_add, jnp.where(kc > theta, 0.0, MASK_NEG))
            return run + jnp.sum(eqf, axis=0, keepdims=True)

        lax.fori_loop(0, nck, tie_chunk, jnp.zeros((1, tq), F32))

    m_ref[...] = jnp.full(m_ref.shape, -jnp.inf, F32)
    l_ref[...] = jnp.zeros(l_ref.shape, F32)
    acc_ref[...] = jnp.zeros(acc_ref.shape, F32)

    ones_rows = jnp.ones((BF16_ROWS, tq), BF16)

    head_slices = [slice(h * A_HEAD_DIM, (h + 1) * A_HEAD_DIM) for h in range(A_HEADS)]

    def logits(c, h):
        off = pl.multiple_of(c * tq, tq)
        s_ref[h] = jnp.dot(k_ref[pl.ds(off, tq), head_slices[h]], qt_ref[head_slices[h], :],
                           preferred_element_type=F32)

    def attend(c, h, lag):
        hs = head_slices[h]
        s = s_ref[h] + madd_ref[c]
        if lag is not None:
            s = s + bias_ref[h, lag]
        m_old = m_ref[h]
        m_new = jnp.maximum(m_old, jnp.max(s, axis=0, keepdims=True))
        alpha = jnp.exp2(m_old - m_new)
        p = jnp.exp2(s - m_new).astype(BF16)
        pv = jnp.dot(jnp.concatenate([vt_ref[c, hs, :], ones_rows], axis=0), p,
                     preferred_element_type=F32)
        l_ref[h] = alpha * l_ref[h] + pv[A_HEAD_DIM:A_HEAD_DIM + 1]
        acc_ref[h] = alpha * acc_ref[h] + pv[:A_HEAD_DIM]
        m_ref[h] = m_new

    def step(c, lag, prefetch):
        for h in range(A_HEADS):
            attend(c, h, lag)
            if prefetch:
                logits(c + 1, h)

    for h in range(A_HEADS):
        logits(0, h)

    def far_chunk(c, carry):
        step(c, None, True)
        return carry

    paired_loop(jnp.maximum(qi - 1, 0), far_chunk)

    @pl.when(qi >= 1)
    def _():
        step(qi - 1, 1, True)

    step(qi, 0, False)

    for h in range(A_HEADS):
        o = acc_ref[h] * (1.0 / l_ref[h])
        o_ref[:, h * A_HEAD_DIM:(h + 1) * A_HEAD_DIM] = o.T.astype(o_ref.dtype)


def _dsa_attention(p16, p32, vt, bias_tiles, bsz, seq):
    tq = min(DSA_TQ, seq)
    nq = seq // tq
    topk = min(TOPK_MAX, seq // 4)
    n = bsz * seq
    one = pl.Buffered(1)
    return pl.pallas_call(
        functools.partial(_dsa_kernel, tq=tq, topk=topk),
        grid=(bsz, nq),
        in_specs=[
            pl.BlockSpec((tq, A_WIDTH), lambda b, i: (b * nq + i, P16_AQ)),
            pl.BlockSpec((tq, 1024), lambda b, i: (b * nq + i, P32_IQ // 1024)),
            pl.BlockSpec((tq, LANES), lambda b, i: (b * nq + i, P32_IW // LANES)),
            pl.BlockSpec((seq, A_WIDTH), lambda b, i: (b, P16_AK), pipeline_mode=one),
            pl.BlockSpec((nq, A_WIDTH, tq), lambda b, i: (b, 0, 0), pipeline_mode=one),
            pl.BlockSpec((seq, LANES), lambda b, i: (b, P32_IK // LANES), pipeline_mode=one),
            pl.BlockSpec(bias_tiles.shape, lambda b, i: (0, 0, 0, 0), pipeline_mode=one),
        ],
        out_specs=pl.BlockSpec((tq, A_WIDTH), lambda b, i: (b * nq + i, 0)),
        out_shape=jax.ShapeDtypeStruct((n, A_WIDTH), BF16),
        scratch_shapes=[pltpu.VMEM((nq, tq, tq), jnp.int32),
                        pltpu.VMEM((nq, tq, tq), jnp.int16),
                        pltpu.VMEM((nq, tq, tq), jnp.int16),
                        pltpu.VMEM((nq, tq, tq), F32),
                        pltpu.VMEM((A_WIDTH, tq), BF16),
                        pltpu.VMEM((IDX_HEADS * LANES, tq), BF16),
                        pltpu.VMEM((LANES, tq), F32),
                        pltpu.VMEM((A_HEADS, 1, tq), F32),
                        pltpu.VMEM((A_HEADS, 1, tq), F32),
                        pltpu.VMEM((A_HEADS, A_HEAD_DIM, tq), F32),
                        pltpu.VMEM((A_HEADS, tq, tq), F32)],
        compiler_params=_cparams(("arbitrary", "arbitrary")),
        name="dsa_attention",
    )(p16, p32, p32, p16, vt, p32, bias_tiles)


def _t5_bucket(rel):
    max_exact = REL_BUCKETS // 2
    relf = jnp.maximum(rel, 1).astype(F32)
    large = max_exact + (jnp.log(relf / max_exact) / math.log(REL_MAX_DIST / max_exact)
                         * (REL_BUCKETS - max_exact)).astype(jnp.int32)
    large = jnp.minimum(large, REL_BUCKETS - 1)
    return jnp.where(rel < max_exact, rel, large)


def _bias_tiles(rel_bias, tq):
    assert tq >= REL_MAX_DIST
    key = jnp.arange(tq, dtype=jnp.int32)[:, None]
    qry = jnp.arange(tq, dtype=jnp.int32)[None, :]
    bucket = jnp.stack([_t5_bucket(jnp.maximum(lag * tq + qry - key, 0)) for lag in range(2)])
    rel = ((rel_bias - rel_bias[REL_BUCKETS - 1:REL_BUCKETS]) * LOG2_E).astype(F32)
    onehot = bucket[None] == jnp.arange(REL_BUCKETS, dtype=jnp.int32)[:, None, None, None]
    return jnp.sum(jnp.where(onehot[:, None], rel[:, :, None, None, None], 0.0), axis=0)


def _hgrn_kernel(q_ref, f_ref, i_ref, g_ref, lb_ref, ng_ref, tril_ref, o_ref, st_ref, attn_ref, stage_ref,
                 *, rows):
    @pl.when(pl.program_id(1) == 0)
    def _():
        st_ref[...] = jnp.zeros_like(st_ref)

    lb = lb_ref[...]
    f = lb + (1.0 - lb) * jax.nn.sigmoid(f_ref[...])
    logf = jnp.log(f)
    kk = 1.0 - f
    g1 = logf.astype(BF16)
    r1 = logf - g1.astype(F32)
    g2 = r1.astype(BF16)
    g3 = (r1 - g2.astype(F32)).astype(BF16)
    tril = tril_ref[...]
    bcum = (jnp.dot(tril, g1, preferred_element_type=F32)
            + jnp.dot(tril, g2, preferred_element_type=F32)
            + jnp.dot(tril, g3, preferred_element_type=F32))

    srow = lax.broadcasted_iota(jnp.int32, (HGRN_C, B_HEAD_DIM), 0)
    trow = lax.broadcasted_iota(jnp.int32, (HGRN_SB, HGRN_C), 0)
    scol = lax.broadcasted_iota(jnp.int32, (HGRN_SB, HGRN_C), 1)
    ng = ng_ref[...]
    q = q_ref[...].astype(F32)
    qb_all = (q * jnp.exp(bcum)).astype(BF16)

    tiles = [(n, h) for n in range(rows // HGRN_C) for h in range(B_HEADS)]

    def rs(n):
        return slice(n * HGRN_C, (n + 1) * HGRN_C)

    def hs(h):
        return slice(h * B_HEAD_DIM, (h + 1) * B_HEAD_DIM)

    worst = None
    for j in range(rows // HGRN_SB):
        r0 = j * HGRN_SB
        span = bcum[r0 + HGRN_SB - 1:r0 + HGRN_SB]
        if r0 % HGRN_C:
            span = span - bcum[r0 - 1:r0]
        worst = span if worst is None else jnp.minimum(worst, span)
    in_range = jnp.min(worst) > -HGRN_MAX_LOG_DECAY

    @pl.when(in_range)
    def _():
        a_parts = {}
        for n, h in tiles:
            bc, qc, kc = bcum[rs(n), hs(h)], q[rs(n), hs(h)], kk[rs(n), hs(h)]
            for sb in range(HGRN_C // HGRN_SB):
                s0 = sb * HGRN_SB
                beta = bc[s0 - 1:s0] if sb > 0 else jnp.zeros((1, B_HEAD_DIM), F32)
                qs = (qc[s0:s0 + HGRN_SB] * jnp.exp(bc[s0:s0 + HGRN_SB] - beta)).astype(BF16)
                live = s0 + HGRN_SB
                ks = (kc[:live] * jnp.exp(beta - bc[:live])).astype(BF16)
                if live < HGRN_C:
                    ks = jnp.concatenate([ks, jnp.zeros((HGRN_C - live, B_HEAD_DIM), BF16)], axis=0)
                a_parts[n, h, sb] = lax.dot_general(qs, ks, NT_DIMS, preferred_element_type=F32)
        for ti, (n, h) in enumerate(tiles):
            a_rows = [jnp.where(scol <= trow + sb * HGRN_SB, a_parts[n, h, sb], 0.0)
                      for sb in range(HGRN_C // HGRN_SB)]
            attn_ref[ti] = jnp.concatenate(a_rows, axis=0)

    @pl.when(jnp.logical_not(in_range))
    def _():
        for h in range(B_HEADS):
            stage_ref[0, h] = bcum[:, hs(h)]
            stage_ref[1, h] = q[:, hs(h)]
            stage_ref[2, h] = kk[:, hs(h)]
        t_idx = lax.broadcasted_iota(jnp.int32, (HGRN_C, HGRN_C), 0)
        s_idx = lax.broadcasted_iota(jnp.int32, (HGRN_C, HGRN_C), 1)

        def safe_tile(ti, carry):
            n, h = ti // B_HEADS, ti % B_HEADS
            r0 = pl.multiple_of(n * HGRN_C, HGRN_C)
            bc = stage_ref[0, h, pl.ds(r0, HGRN_C), :]
            qc = stage_ref[1, h, pl.ds(r0, HGRN_C), :]
            kc = stage_ref[2, h, pl.ds(r0, HGRN_C), :]
            acc = jnp.where(t_idx == s_idx, jnp.sum(qc * kc, axis=-1, keepdims=True), 0.0)
            block = HGRN_C
            while block >= 2:
                half = block // 2
                ref_row = (t_idx & -block) + (half - 1)
                bref = jnp.dot(jnp.where(s_idx == ref_row, 1.0, 0.0), bc, precision=lax.Precision.HIGHEST,
                               preferred_element_type=F32)
                second = (srow & (block - 1)) >= half
                qs = jnp.where(second, qc * jnp.exp(jnp.where(second, bc - bref, 0.0)), 0.0).astype(BF16)
                ks = jnp.where(second, 0.0, kc * jnp.exp(jnp.where(second, 0.0, bref - bc))).astype(BF16)
                a = lax.dot_general(qs, ks, NT_DIMS, preferred_element_type=F32)
                acc = acc + jnp.where((t_idx & -block) == (s_idx & -block), a, 0.0)
                block = half
            attn_ref[ti] = acc
            return carry

        lax.fori_loop(0, len(tiles), safe_tile, 0)

    intra, upd, dec = {}, {}, {}
    for ti, (n, h) in enumerate(tiles):
        attn = attn_ref[ti].astype(BF16)
        vc = i_ref[rs(n), hs(h)]
        intra[n, h] = jnp.dot(attn, vc, preferred_element_type=F32)
        bc = bcum[rs(n), hs(h)]
        blast = bc[HGRN_C - 1:HGRN_C]
        kdec = (kk[rs(n), hs(h)] * jnp.exp(blast - bc)).astype(BF16)
        upd[n, h] = lax.dot_general(vc, kdec, TN_DIMS, preferred_element_type=F32)
        dec[n, h] = jnp.exp(blast)
    for n, h in tiles:
        st = st_ref[h]
        o = intra[n, h] + lax.dot_general(qb_all[rs(n), hs(h)], st.astype(BF16), NT_DIMS,
                                          preferred_element_type=F32)
        st_ref[h] = st * dec[n, h] + upd[n, h]
        ms = jnp.mean(o * o, axis=-1, keepdims=True)
        on = o * lax.rsqrt(ms + EPS) * ng
        o_ref[rs(n), hs(h)] = (on * _silu(g_ref[rs(n), hs(h)].astype(F32))).astype(o_ref.dtype)


def _hgrn2(p16, p32, lb_l, norm_g, bsz, seq):
    rows = min(HGRN_L, seq)
    nj = seq // rows
    n = bsz * seq
    r = jnp.arange(rows, dtype=jnp.int32)
    tril = ((r[:, None] >= r[None, :]) & (r[:, None] // HGRN_C == r[None, :] // HGRN_C)).astype(BF16)

    def col(base):
        return lambda b, j: (b * nj + j, base)

    return pl.pallas_call(
        functools.partial(_hgrn_kernel, rows=rows),
        grid=(bsz, nj),
        in_specs=[
            pl.BlockSpec((rows, B_WIDTH), col(P16_BQ)),
            pl.BlockSpec((rows, B_WIDTH), col(P32_BF // B_WIDTH)),
            pl.BlockSpec((rows, B_WIDTH), col(P16_BI)),
            pl.BlockSpec((rows, B_WIDTH), col(P16_BG)),
            pl.BlockSpec((1, B_WIDTH), lambda b, j: (0, 0)),
            pl.BlockSpec((1, B_HEAD_DIM), lambda b, j: (0, 0)),
            pl.BlockSpec((rows, rows), lambda b, j: (0, 0)),
        ],
        out_specs=pl.BlockSpec((rows, B_WIDTH), lambda b, j: (b * nj + j, 0)),
        out_shape=jax.ShapeDtypeStruct((n, B_WIDTH), BF16),
        scratch_shapes=[pltpu.VMEM((B_HEADS, B_HEAD_DIM, B_HEAD_DIM), F32),
                        pltpu.VMEM((rows // HGRN_C * B_HEADS, HGRN_C, HGRN_C), F32),
                        pltpu.VMEM((3, B_HEADS, rows, B_HEAD_DIM), F32)],
        compiler_params=_cparams(("arbitrary", "arbitrary")),
        name="hgrn2",
    )(p16, p32, p16, p16, lb_l.reshape(1, B_WIDTH), norm_g.reshape(1, B_HEAD_DIM), tril)


def _ret_kernel(q_ref, k_ref, v_ref, g_ref, cos_ref, sin_ref, idec_ref, qdec_ref, kdec_ref, cdec_ref,
                o_ref, st_ref):
    @pl.when(pl.program_id(1) == 0)
    def _():
        st_ref[...] = jnp.zeros_like(st_ref)

    heads = range(C_HEADS)
    cos = jnp.concatenate([cos_ref[...]] * C_HEADS, axis=1)
    sin_signed = jnp.concatenate([sin_ref[...]] * C_HEADS, axis=1)
    even = lax.broadcasted_iota(jnp.int32, cos.shape, 1) % 2 == 0

    def rot(a):
        swapped = jnp.where(even, pltpu.roll(a, C_QK_WIDTH - 1, 1), pltpu.roll(a, 1, 1))
        return a * cos + swapped * sin_signed

    qr = rot(q_ref[...].astype(F32))
    kr = rot(k_ref[...].astype(F32))
    qk = [slice(h * C_QK_DIM, (h + 1) * C_QK_DIM) for h in heads]
    vs = [slice(h * C_V_DIM, (h + 1) * C_V_DIM) for h in heads]
    attn = [lax.dot_general(qr[:, qk[h]].astype(BF16), kr[:, qk[h]].astype(BF16), NT_DIMS,
                            preferred_element_type=F32) * idec_ref[h] for h in heads]
    inter = [jnp.dot((qr[:, qk[h]] * qdec_ref[h]).astype(BF16), st_ref[h].astype(BF16),
                     preferred_element_type=F32) for h in heads]
    intra = [jnp.dot(attn[h].astype(BF16), v_ref[:, vs[h]], preferred_element_type=F32) for h in heads]
    upd = [jnp.dot((kr[:, qk[h]] * kdec_ref[h]).T.astype(BF16), v_ref[:, vs[h]],
                   preferred_element_type=F32) for h in heads]
    for h in heads:
        st_ref[h] = cdec_ref[h, 0:1, :] * st_ref[h] + upd[h]
        o = intra[h] + inter[h]
        ms = jnp.mean(o * o, axis=-1, keepdims=True)
        o_ref[:, vs[h]] = (_silu(g_ref[:, vs[h]].astype(F32)) * (o * lax.rsqrt(ms + EPS))).astype(o_ref.dtype)


def _retention_tables(seq):
    pos = jnp.arange(seq, dtype=F32)
    theta = jnp.repeat(1.0 / (10000.0 ** jnp.linspace(0.0, 1.0, C_QK_DIM // 2)), 2)
    ang = pos[:, None] * theta[None, :]
    pair_sign = jnp.where(jnp.arange(C_QK_DIM) % 2 == 0, -1.0, 1.0)
    log_gamma = jnp.log(1.0 - 2.0 ** (-5.0 - jnp.arange(C_HEADS, dtype=F32)))
    idx = jnp.arange(RET_C, dtype=F32)
    causal = idx[:, None] >= idx[None, :]
    idec = jnp.exp(jnp.where(causal[None], (idx[:, None] - idx[None, :])[None] * log_gamma[:, None, None],
                             -jnp.inf))
    qdec = jnp.exp((idx + 1.0)[None, :] * log_gamma[:, None])[..., None]
    kdec = jnp.exp((RET_C - 1.0 - idx)[None, :] * log_gamma[:, None])[..., None]
    cdec = jnp.exp(RET_C * log_gamma)[:, None, None]
    return (jnp.cos(ang), jnp.sin(ang) * pair_sign[None, :], idec,
            jnp.broadcast_to(qdec, (C_HEADS, RET_C, C_QK_DIM)),
            jnp.broadcast_to(kdec, (C_HEADS, RET_C, C_QK_DIM)),
            jnp.broadcast_to(cdec, (C_HEADS, 8, C_V_DIM)))


def _retention(p16, tables, bsz, seq):
    cos, sin, idec, qdec, kdec, cdec = tables
    nj = seq // RET_C
    n = bsz * seq
    v_blk = C_V_WIDTH // 1024

    def whole(a):
        return pl.BlockSpec(a.shape, lambda b, j: (0,) * a.ndim)

    return pl.pallas_call(
        _ret_kernel,
        grid=(bsz, nj),
        in_specs=[
            pl.BlockSpec((RET_C, C_QK_WIDTH), lambda b, j: (b * nj + j, P16_CQ)),
            pl.BlockSpec((RET_C, C_QK_WIDTH), lambda b, j: (b * nj + j, P16_CK)),
            pl.BlockSpec((RET_C, C_V_WIDTH), lambda b, j: (b * nj + j, P16_CV // v_blk)),
            pl.BlockSpec((RET_C, C_V_WIDTH), lambda b, j: (b * nj + j, P16_CG // v_blk)),
            pl.BlockSpec((RET_C, C_QK_DIM), lambda b, j: (j, 0)),
            pl.BlockSpec((RET_C, C_QK_DIM), lambda b, j: (j, 0)),
            whole(idec), whole(qdec), whole(kdec), whole(cdec),
        ],
        out_specs=pl.BlockSpec((RET_C, C_V_WIDTH), lambda b, j: (b * nj + j, 0)),
        out_shape=jax.ShapeDtypeStruct((n, C_V_WIDTH), BF16),
        scratch_shapes=[pltpu.VMEM((C_HEADS, C_QK_DIM, C_V_DIM), F32)],
        compiler_params=_cparams(("arbitrary", "arbitrary")),
        name="retention",
    )(p16, p16, p16, p16, cos, sin, idec, qdec, kdec, cdec)


def _merge_kernel(oa_ref, ob_ref, oc_ref, ga_ref, gb_ref, gc_ref, x_ref, mod_ref, g2_ref,
                  wa_ref, wb_ref, wc_ref, wo_ref, wrh_ref, wrl_ref, br_ref,
                  x1_ref, h2_ref, route_ref, *, tiles_per_batch):
    b = pl.program_id(0) // tiles_per_batch

    def gated(o_ref, w_ref, g_ref):
        y = jnp.dot(o_ref[...], w_ref[...], preferred_element_type=F32)
        return jax.nn.sigmoid(g_ref[...].astype(F32)) * y

    merged = gated(oa_ref, wa_ref, ga_ref) + gated(ob_ref, wb_ref, gb_ref) + gated(oc_ref, wc_ref, gc_ref)
    y = jnp.dot(merged.astype(BF16), wo_ref[...], preferred_element_type=F32)
    gt1 = mod_ref[pl.ds(b, 1), 2 * D_MODEL:3 * D_MODEL]
    x1 = x_ref[...] + gt1 * y
    x1_ref[...] = x1
    sh2 = mod_ref[pl.ds(b, 1), 3 * D_MODEL:4 * D_MODEL]
    sc2 = mod_ref[pl.ds(b, 1), 4 * D_MODEL:5 * D_MODEL]
    h2 = _rms_mod(x1, g2_ref[...], sc2, sh2)
    h_hi = h2.astype(BF16)
    words = _pack_bf16_pairs(h2)
    for j in range(h2_ref.shape[0]):
        h2_ref[j] = words[:, j * SC_SUB:(j + 1) * SC_SUB]
    h_lo = (h2 - h_hi.astype(F32)).astype(BF16)
    logits = (jnp.dot(h_hi, wrh_ref[...], preferred_element_type=F32)
              + jnp.dot(h_lo, wrh_ref[...], preferred_element_type=F32)
              + jnp.dot(h_hi, wrl_ref[...], preferred_element_type=F32)) + br_ref[...]
    lane = lax.broadcasted_iota(jnp.int32, logits.shape, 1).astype(F32)
    neg_inf = -jnp.inf

    def first_argmax(vals):
        top = jnp.max(vals, axis=-1, keepdims=True)
        idx = jnp.min(jnp.where(vals == top, lane, float(LANES)), axis=-1, keepdims=True)
        return top, idx

    gl = jnp.where(lane < N_GROUPS, logits, neg_inf)
    gmax, gsel = first_argmax(gl)
    gprob = 1.0 / jnp.sum(jnp.exp(gl - gmax), axis=-1, keepdims=True)
    lo = N_GROUPS + EXPERTS_PER_GROUP * gsel
    el = jnp.where((lane >= lo) & (lane < lo + EXPERTS_PER_GROUP), logits, neg_inf)
    v1, i1 = first_argmax(el)
    el2 = jnp.where(lane == i1, neg_inf, el)
    v2, i2 = first_argmax(el2)
    e2 = jnp.exp(v2 - v1)
    den = 1.0 + e2
    route_ref[...] = jnp.where(lane == 0.0, i1 - N_GROUPS,
                               jnp.where(lane == 1.0, i2 - N_GROUPS,
                                         jnp.where(lane == 2.0, gprob / den,
                                                   jnp.where(lane == 3.0, gprob * (e2 / den), 0.0))))


def _merge(o_a, o_b, o_c, p16, x2, mod_l, g2, wa, wb, wc, wo, wr_hi, wr_lo, br, seq):
    n = x2.shape[0]
    tm = min(512, seq)
    one = pl.Buffered(1)

    def rows(width, cb=0):
        return pl.BlockSpec((tm, width), lambda i: (i, cb))

    def whole(a):
        return pl.BlockSpec(a.shape, lambda i: (0,) * a.ndim, pipeline_mode=one)

    return pl.pallas_call(
        functools.partial(_merge_kernel, tiles_per_batch=seq // tm),
        grid=(n // tm,),
        in_specs=[rows(A_WIDTH), rows(B_WIDTH), rows(C_V_WIDTH),
                  rows(D_MODEL, P16_GA), rows(D_MODEL, P16_GB), rows(D_MODEL, P16_GC),
                  rows(D_MODEL), whole(mod_l), pl.BlockSpec((1, D_MODEL), lambda i: (0, 0)),
                  whole(wa), whole(wb), whole(wc), whole(wo), whole(wr_hi), whole(wr_lo), whole(br)],
        out_specs=[rows(D_MODEL), pl.BlockSpec((SC_PIECES, tm, SC_SUB), lambda i: (0, i, 0)), rows(LANES)],
        out_shape=[jax.ShapeDtypeStruct((n, D_MODEL), F32),
                   jax.ShapeDtypeStruct((SC_PIECES, n, SC_SUB), jnp.int32),
                   jax.ShapeDtypeStruct((n, LANES), F32)],
        compiler_params=_cparams(("arbitrary",)),
        name="merge_route",
    )(o_a, o_b, o_c, p16, p16, p16, x2, mod_l, g2.reshape(1, D_MODEL), wa, wb, wc, wo, wr_hi, wr_lo, br)


def _lane_pick(vals, lane, idx):
    return jnp.sum(jnp.where(lane == idx, vals, 0.0), axis=-1, keepdims=True)


def _rank_kernel(route_ref, rk_ref, cnt_ref, run_ref):
    @pl.when(pl.program_id(0) == 0)
    def _():
        run_ref[...] = jnp.zeros_like(run_ref)

    route = route_ref[...]
    tb = route.shape[0]
    lane = lax.broadcasted_iota(jnp.int32, route.shape, 1).astype(F32)
    e1 = _lane_pick(route, lane, 0.0)
    e2 = _lane_pick(route, lane, 1.0)
    sel = jnp.where((lane == e1) | (lane == e2), 1.0, 0.0)
    r = lax.broadcasted_iota(jnp.int32, (tb, tb), 0)
    c = lax.broadcasted_iota(jnp.int32, (tb, tb), 1)
    before = jnp.where(c < r, 1.0, 0.0).astype(BF16)
    rank = jnp.dot(before, sel.astype(BF16), preferred_element_type=F32) + run_ref[0:1, :]
    rk_ref[...] = jnp.where(lane == 0.0, _lane_pick(rank, lane, e1),
                            jnp.where(lane == 1.0, _lane_pick(rank, lane, e2), 0.0))
    run_ref[...] = run_ref[...] + jnp.sum(sel, axis=0, keepdims=True)
    cnt_ref[...] = run_ref[...]


def _expert_ranks(route):
    n = route.shape[0]
    tb = min(PLAN_TB, n)
    return pl.pallas_call(
        _rank_kernel,
        grid=(n // tb,),
        in_specs=[pl.BlockSpec((tb, LANES), lambda i: (i, 0))],
        out_specs=[pl.BlockSpec((tb, LANES), lambda i: (i, 0)),
                   pl.BlockSpec((8, LANES), lambda i: (0, 0))],
        out_shape=[jax.ShapeDtypeStruct((n, LANES), F32), jax.ShapeDtypeStruct((8, LANES), F32)],
        scratch_shapes=[pltpu.VMEM((8, LANES), F32)],
        compiler_params=_cparams(("arbitrary",)),
        name="expert_ranks",
    )(route)


def _plan_kernel(cnt_ref, route_ref, rk_ref, pos_ref, tmap_ref):
    lane_i = lax.broadcasted_iota(jnp.int32, (8, LANES), 1)
    cnt = jnp.where(lane_i < N_EXPERTS, cnt_ref[...], 0.0)
    padded = jnp.floor((cnt + (MOE_TM - 1)) * (1.0 / MOE_TM)) * MOE_TM
    r = lax.broadcasted_iota(jnp.int32, (LANES, LANES), 0)
    c = lax.broadcasted_iota(jnp.int32, (LANES, LANES), 1)
    base = jnp.dot(padded, jnp.where(r < c, 1.0, 0.0), precision=lax.Precision.HIGHEST,
                   preferred_element_type=F32)

    route = route_ref[...]
    lane = lax.broadcasted_iota(jnp.int32, route.shape, 1).astype(F32)
    rk = rk_ref[...]
    base_row = base[0:1, :]
    pos1 = _lane_pick(base_row, lane, _lane_pick(route, lane, 0.0)) + _lane_pick(rk, lane, 0.0)
    pos2 = _lane_pick(base_row, lane, _lane_pick(route, lane, 1.0)) + _lane_pick(rk, lane, 1.0)
    pos_ref[...] = jnp.where(lane == 0.0, pos1, jnp.where(lane == 1.0, pos2, 0.0)).astype(jnp.int32)

    @pl.when(pl.program_id(0) == 0)
    def _():
        nt = tmap_ref.shape[0]
        tlane = lax.broadcasted_iota(jnp.int32, (nt, LANES), 1)
        start = (lax.broadcasted_iota(jnp.int32, (nt, LANES), 0) * MOE_TM).astype(F32)
        end_row = jnp.where(tlane < N_EXPERTS, base_row + padded[0:1, :], 3e38)
        expert = jnp.sum(jnp.where(end_row <= start, 1.0, 0.0), axis=-1, keepdims=True)
        expert_c = jnp.minimum(expert, N_EXPERTS - 1.0)
        tl = tlane.astype(F32)
        left = _lane_pick(cnt[0:1, :], tl, expert_c) - (start[:, 0:1] - _lane_pick(base_row, tl, expert_c))
        valid = jnp.where(expert < N_EXPERTS, jnp.clip(left, 0.0, float(MOE_TM)), 0.0)
        tmap_ref[...] = jnp.where(tlane == 0, expert_c, jnp.where(tlane == 1, valid, 0.0)).astype(jnp.int32)


def _expert_plan(cnt, route, rk, n_tiles):
    n = route.shape[0]
    tb = min(PLAN_TB, n)
    nt_pad = -(-n_tiles // 8) * 8
    return pl.pallas_call(
        _plan_kernel,
        grid=(n // tb,),
        in_specs=[pl.BlockSpec((8, LANES), lambda i: (0, 0)),
                  pl.BlockSpec((tb, LANES), lambda i: (i, 0)),
                  pl.BlockSpec((tb, LANES), lambda i: (i, 0))],
        out_specs=[pl.BlockSpec((tb, LANES), lambda i: (i, 0)),
                   pl.BlockSpec((nt_pad, LANES), lambda i: (0, 0))],
        out_shape=[jax.ShapeDtypeStruct((n, LANES), jnp.int32),
                   jax.ShapeDtypeStruct((nt_pad, LANES), jnp.int32)],
        compiler_params=_cparams(("arbitrary",)),
        name="expert_plan",
    )(cnt, route, rk)


def _sc_mesh():
    return plsc.VectorSubcoreMesh(core_axis_name="c", subcore_axis_name="s")


def _sc_scatter_rows(src, idx, out_rows):
    m = idx.shape[0]
    n_src_win = src.shape[0] // SC_WINDOW

    @functools.partial(pl.kernel, out_type=jax.ShapeDtypeStruct((out_rows, src.shape[1]), src.dtype),
                       mesh=_sc_mesh(), scratch_types=[])
    def scatter(x_hbm, i_hbm, o_hbm):
        def body(x_vmem, i_vmem):
            pltpu.sync_copy(x_vmem, o_hbm.at[i_vmem.at[0]])

        pltpu.emit_pipeline(
            body, grid=(m // SC_WINDOW,),
            in_specs=[pl.BlockSpec((SC_WINDOW, src.shape[1]), lambda i: (i % n_src_win, 0)),
                      pl.BlockSpec((1, SC_WINDOW), lambda i: (0, i))],
            out_specs=[], core_axis_name=("c", "s"),
            dimension_semantics=(pltpu.PARALLEL,))(x_hbm, i_hbm)

    return scatter(src, idx.reshape(1, m))


def _sc_gather_rows(table, idx):
    m = idx.shape[0]

    @functools.partial(pl.kernel, out_type=jax.ShapeDtypeStruct((m, table.shape[1]), table.dtype),
                       mesh=_sc_mesh(), scratch_types=[])
    def gather(x_hbm, i_hbm, o_hbm):
        def body(i_vmem, o_vmem):
            pltpu.sync_copy(x_hbm.at[i_vmem.at[0]], o_vmem)

        pltpu.emit_pipeline(
            body, grid=(m // SC_WINDOW,),
            in_specs=[pl.BlockSpec((1, SC_WINDOW), lambda i: (0, i))],
            out_specs=[pl.BlockSpec((SC_WINDOW, table.shape[1]), lambda i: (i, 0))],
            core_axis_name=("c", "s"),
            dimension_semantics=(pltpu.PARALLEL,))(i_hbm, o_hbm)

    return gather(table, idx.reshape(1, m))


def _piece_row_index(pos, rows):
    return (jnp.arange(SC_PIECES, dtype=jnp.int32)[:, None] * rows + pos[None, :]).reshape(-1)


def _grouped_kernel(te_ref, tv_ref, x_ref, wg_ref, wu_ref, wd_ref, o_ref, wg16_ref, wu16_ref, wd16_ref):
    i = pl.program_id(0)
    valid = tv_ref[i]

    @pl.when((i == 0) | (te_ref[i] != te_ref[jnp.maximum(i - 1, 0)]))
    def _():
        wg16_ref[...] = wg_ref[0].astype(BF16)
        wu16_ref[...] = wu_ref[0].astype(BF16)
        wd16_ref[...] = wd_ref[0].astype(BF16)

    @pl.when(valid > 0)
    def _():
        words = jnp.concatenate([x_ref[j] for j in range(SC_PIECES)], axis=1)
        row = lax.broadcasted_iota(jnp.int32, words.shape, 0)
        words = jnp.where(row < valid, words, 0)
        x = _unpack_bf16_pairs(words).astype(BF16)
        a = jnp.dot(x, wg16_ref[...], preferred_element_type=F32)
        u = jnp.dot(x, wu16_ref[...], preferred_element_type=F32)
        hm = (_silu(a) * u).astype(BF16)
        out = _pack_bf16_pairs(jnp.dot(hm, wd16_ref[...], preferred_element_type=F32))
        for j in range(SC_PIECES):
            o_ref[j] = out[:, j * SC_SUB:(j + 1) * SC_SUB]

    @pl.when(valid <= 0)
    def _():
        o_ref[...] = jnp.zeros_like(o_ref)


def _grouped_experts(tile_expert, tile_valid, xs, wg, wu, wd, layer):
    n_tiles = tile_expert.shape[0]
    rows_block = pl.BlockSpec((SC_PIECES, MOE_TM, SC_SUB), lambda i, te, tv: (0, i, 0))
    return pl.pallas_call(
        _grouped_kernel,
        grid_spec=pltpu.PrefetchScalarGridSpec(
            num_scalar_prefetch=2,
            grid=(n_tiles,),
            in_specs=[rows_block,
                      pl.BlockSpec((None, 1, D_MODEL, D_EXPERT), lambda i, te, tv: (layer, te[i], 0, 0)),
                      pl.BlockSpec((None, 1, D_MODEL, D_EXPERT), lambda i, te, tv: (layer, te[i], 0, 0)),
                      pl.BlockSpec((None, 1, D_EXPERT, D_MODEL), lambda i, te, tv: (layer, te[i], 0, 0))],
            out_specs=rows_block,
            scratch_shapes=[pltpu.VMEM((D_MODEL, D_EXPERT), BF16),
                            pltpu.VMEM((D_MODEL, D_EXPERT), BF16),
                            pltpu.VMEM((D_EXPERT, D_MODEL), BF16)]),
        out_shape=jax.ShapeDtypeStruct(xs.shape, jnp.int32),
        compiler_params=_cparams(("arbitrary",)),
        name="grouped_experts",
    )(tile_expert, tile_valid, xs, wg, wu, wd)


def _combine_kernel(x1_ref, y_ref, route_ref, mod_ref, o_ref, *, tiles_per_batch):
    b = pl.program_id(0) // tiles_per_batch
    gt2 = mod_ref[pl.ds(b, 1), 5 * D_MODEL:6 * D_MODEL]
    route = route_ref[...]
    lane = lax.broadcasted_iota(jnp.int32, route.shape, 1).astype(F32)
    w1 = _lane_pick(route, lane, 2.0)
    w2 = _lane_pick(route, lane, 3.0)
    y1 = _unpack_bf16_pairs(jnp.concatenate([y_ref[0, j] for j in range(SC_PIECES)], axis=1))
    y2 = _unpack_bf16_pairs(jnp.concatenate([y_ref[1, j] for j in range(SC_PIECES)], axis=1))
    o_ref[...] = x1_ref[...] + gt2 * (w1 * y1 + w2 * y2)


def _combine(x1, y2, route, mod_l, seq):
    n = x1.shape[0]
    tm = min(1024, seq)
    return pl.pallas_call(
        functools.partial(_combine_kernel, tiles_per_batch=seq // tm),
        grid=(n // tm,),
        in_specs=[pl.BlockSpec((tm, D_MODEL), lambda i: (i, 0)),
                  pl.BlockSpec((2, SC_PIECES, tm, SC_SUB), lambda i: (0, 0, i, 0)),
                  pl.BlockSpec((tm, LANES), lambda i: (i, 0)),
                  pl.BlockSpec(mod_l.shape, lambda i: (0, 0))],
        out_specs=pl.BlockSpec((tm, D_MODEL), lambda i: (i, 0)),
        out_shape=jax.ShapeDtypeStruct((n, D_MODEL), F32),
        compiler_params=_cparams(("arbitrary",)),
        name="moe_combine",
    )(x1, y2, route, mod_l)


def _moe(h2, route, x1, mod_l, wg, wu, wd, layer, seq):
    n = h2.shape[1]
    n_tiles = (2 * n) // MOE_TM + N_EXPERTS
    rows = n_tiles * MOE_TM
    rk, cnt = _expert_ranks(route)
    pos, tmap = _expert_plan(cnt, route, rk, n_tiles)
    idx = jnp.concatenate([_piece_row_index(pos[:, 0], rows), _piece_row_index(pos[:, 1], rows)])
    xs = _sc_scatter_rows(h2.reshape(SC_PIECES * n, SC_SUB), idx, SC_PIECES * rows)
    ys = _grouped_experts(tmap[:n_tiles, 0], tmap[:n_tiles, 1], xs.reshape(SC_PIECES, rows, SC_SUB),
                          wg, wu, wd, layer)
    y2 = _sc_gather_rows(ys.reshape(SC_PIECES * rows, SC_SUB), idx).reshape(2, SC_PIECES, n, SC_SUB)
    return _combine(x1, y2, route, mod_l, seq)


def _final_norm_kernel(x_ref, g_ref, o_ref):
    x = x_ref[...]
    ms = jnp.mean(x * x, axis=-1, keepdims=True)
    o_ref[...] = x * lax.rsqrt(ms + EPS) * g_ref[...]


def _final_norm(x2, g, seq):
    n = x2.shape[0]
    tm = min(1024, seq)
    return pl.pallas_call(
        _final_norm_kernel,
        grid=(n // tm,),
        in_specs=[pl.BlockSpec((tm, D_MODEL), lambda i: (i, 0)),
                  pl.BlockSpec((1, D_MODEL), lambda i: (0, 0))],
        out_specs=pl.BlockSpec((tm, D_MODEL), lambda i: (i, 0)),
        out_shape=jax.ShapeDtypeStruct((n, D_MODEL), F32),
        compiler_params=_cparams(("arbitrary",)),
        name="final_norm",
    )(x2, g.reshape(1, D_MODEL))


_IN_OFFS = [sum(IN_SPLITS[:i]) for i in range(len(IN_SPLITS) + 1)]
(_AQ, _AK, _AV, _IQ, _IK, _IW, _BQ, _BF, _BI, _BG, _CQ, _CK, _CV, _CG, _GA, _GB, _GC) = range(len(IN_SPLITS))


def _pack_kernel(w_ref, w16_ref, w32_ref, wvt_ref):
    def cols(seg):
        return w_ref[0, :, _IN_OFFS[seg]:_IN_OFFS[seg + 1]]

    scale = {_AQ: A_HEAD_DIM ** -0.5 * LOG2_E, _CK: C_QK_DIM ** -0.5}
    at = 0
    for seg in (_CV, _CG, _AQ, _AK, _BQ, _BI, _BG, _CQ, _CK, _GA, _GB, _GC):
        v = cols(seg)
        if seg in scale:
            v = v * scale[seg]
        w16_ref[0, :, at:at + v.shape[1]] = v.astype(BF16)
        at += v.shape[1]

    rows = w_ref.shape[1]
    w32_ref[0, :, P32_BF:P32_BF + B_WIDTH] = cols(_BF).astype(BF16)
    iq = cols(_IQ)
    zeros = jnp.zeros((rows, LANES - IDX_DIM), F32)
    for h in range(IDX_HEADS):
        w32_ref[0, :, P32_IQ + h * LANES:P32_IQ + (h + 1) * LANES] = jnp.concatenate(
            [iq[:, h * IDX_DIM:(h + 1) * IDX_DIM], zeros], axis=1).astype(BF16)
    w32_ref[0, :, P32_IK:P32_IK + LANES] = jnp.concatenate([cols(_IK), zeros], axis=1).astype(BF16)
    w32_ref[0, :, P32_IW:P32_IW + LANES] = jnp.concatenate(
        [cols(_IW), jnp.zeros((rows, LANES - IDX_HEADS), F32)], axis=1).astype(BF16)
    wvt_ref[0] = cols(_AV).T.astype(BF16)


def _pack_w_in(w_in):
    depth, d, width = w_in.shape
    rows = LANES
    w16_width = 14 * 1024
    w32_width = P32_IW + LANES
    return pl.pallas_call(
        _pack_kernel,
        grid=(depth, d // rows),
        in_specs=[pl.BlockSpec((1, rows, width), lambda l, r: (l, r, 0))],
        out_specs=[pl.BlockSpec((1, rows, w16_width), lambda l, r: (l, r, 0)),
                   pl.BlockSpec((1, rows, w32_width), lambda l, r: (l, r, 0)),
                   pl.BlockSpec((1, A_WIDTH, rows), lambda l, r: (l, 0, r))],
        out_shape=[jax.ShapeDtypeStruct((depth, d, w16_width), BF16),
                   jax.ShapeDtypeStruct((depth, d, w32_width), BF16),
                   jax.ShapeDtypeStruct((depth, A_WIDTH, d), BF16)],
        compiler_params=_cparams(("arbitrary", "arbitrary")),
        name="pack_w_in",
    )(w_in)


def _split_bf16(w):
    hi = w.astype(BF16)
    return hi, (w - hi.astype(F32)).astype(BF16)


def kernel(x, c, rel_bias, hgrn_lb_raw, norm1_g, norm2_g, ada_w, ada_b, w_in, hgrn_norm_g, w_branch_a,
           w_branch_b, w_branch_c, w_out, router_group_w, router_group_b, router_expert_w,
           router_expert_b, expert_w_gate, expert_w_up, expert_w_down, final_norm_g):
    bsz, seq, _ = x.shape
    depth = w_in.shape[0]
    n = bsz * seq
    x2 = x.reshape(n, D_MODEL)
    tq = min(DSA_TQ, seq)

    lb_all = _hgrn_lower_bounds(hgrn_lb_raw)
    c_pad = jnp.pad(c, ((0, (-bsz) % 8), (0, 0)))
    mod = _ada_mod(c_pad, ada_w, ada_b)
    bias_tiles = _bias_tiles(rel_bias, tq)
    ret_tables = _retention_tables(seq)
    w16_all, w32_all, wvt_all = _pack_w_in(w_in)

    for l in range(depth):
        p16 = _norm_project(x2, mod[l], norm1_g[l], w16_all, l, BF16, 1024, seq, "proj_bf16")
        p32 = _norm_project(x2, mod[l], norm1_g[l], w32_all, l, F32, 768, seq, "proj_f32")
        vt = _norm_project_t(x2, mod[l], norm1_g[l], wvt_all, l, tq, seq, "proj_vt")
        o_a = _dsa_attention(p16, p32, vt, bias_tiles, bsz, seq)
        o_b = _hgrn2(p16, p32, lb_all[l], hgrn_norm_g[l], bsz, seq)
        o_c = _retention(p16, ret_tables, bsz, seq)
        wr = jnp.concatenate([router_group_w[l], router_expert_w[l],
                              jnp.zeros((D_MODEL, LANES - N_GROUPS - N_EXPERTS), F32)], axis=1)
        br = jnp.concatenate([router_group_b[l], router_expert_b[l],
                              jnp.zeros((LANES - N_GROUPS - N_EXPERTS,), F32)]).reshape(1, LANES)
        wr_hi, wr_lo = _split_bf16(wr)
        x1, h2, route = _merge(o_a, o_b, o_c, p16, x2, mod[l], norm2_g[l],
                               w_branch_a[l].astype(BF16), w_branch_b[l].astype(BF16),
                               w_branch_c[l].astype(BF16), w_out[l].astype(BF16),
                               wr_hi, wr_lo, br, seq)
        x2 = _moe(h2, route, x1, mod[l], expert_w_gate, expert_w_up, expert_w_down, l, seq)

    return _final_norm(x2, final_norm_g, seq).reshape(bsz, seq, D_MODEL)
```

```python
import functools
import math

import jax
import jax.numpy as jnp
from jax import lax
from jax.experimental import pallas as pl
from jax.experimental.pallas import tpu as pltpu
from jax.experimental.pallas import tpu_sc as plsc

F32 = jnp.float32
BF16 = jnp.bfloat16

D_MODEL = 1024
A_HEADS = 8
A_HEAD_DIM = 128
IDX_HEADS = 8
IDX_DIM = 64
TOPK_MAX = 256
REL_BUCKETS = 32
REL_MAX_DIST = 128
B_HEADS = 8
B_HEAD_DIM = 128
C_HEADS = 4
C_QK_DIM = 256
C_V_DIM = 512
N_GROUPS = 4
EXPERTS_PER_GROUP = 8
N_EXPERTS = 32
D_EXPERT = 512
EPS = 1e-6

A_WIDTH = A_HEADS * A_HEAD_DIM
B_WIDTH = B_HEADS * B_HEAD_DIM
C_QK_WIDTH = C_HEADS * C_QK_DIM
C_V_WIDTH = C_HEADS * C_V_DIM
IN_SPLITS = (A_WIDTH, A_WIDTH, A_WIDTH, IDX_HEADS * IDX_DIM, IDX_DIM, IDX_HEADS,
             B_WIDTH, B_WIDTH, B_WIDTH, B_WIDTH,
             C_QK_WIDTH, C_QK_WIDTH, C_V_WIDTH, C_V_WIDTH,
             D_MODEL, D_MODEL, D_MODEL)

LANES = 128
BF16_ROWS = 16
VMEM_LIMIT = 56 * 1024 * 1024

P16_CV, P16_CG = 0, 2
P16_AK, P16_BQ, P16_BI, P16_BG, P16_CQ, P16_CK, P16_GA, P16_GB, P16_GC = range(4, 13)
P32_BF = 0
P32_IK = 1024
P32_IW = 1152

DSA_TQ = 256
HGRN_L = 256
HGRN_C = 64
HGRN_SB = 16
HGRN_MAX_LOG_DECAY = 80.0
RET_C = 256
KEY_NEG_INF = -2139095041
HALF_BIAS = 32768
MASK_NEG = -1e30
LOG2_E = math.log2(math.e)
COUNT_CHAINS = 4
MOE_TM = 256
PLAN_TB = 1024
SC_WINDOW = 128
SC_SUB = 256
SC_PIECES = D_MODEL // 2 // SC_SUB

NT_DIMS = (((1,), (1,)), ((), ()))
TN_DIMS = (((0,), (0,)), ((), ()))


def _cparams(sem):
    return pltpu.CompilerParams(dimension_semantics=sem, vmem_limit_bytes=VMEM_LIMIT)


def _silu(x):
    return x * jax.nn.sigmoid(x)


def _pack_bf16_pairs(x):
    k = x.shape[1] // 2
    bits = pltpu.bitcast(x.astype(BF16).astype(F32), jnp.int32)
    return (bits[:, :k] & jnp.int32(-65536)) | lax.shift_right_logical(bits[:, k:], 16)


def _unpack_bf16_pairs(words):
    hi = pltpu.bitcast(words & jnp.int32(-65536), F32)
    lo = pltpu.bitcast(lax.shift_left(words, 16), F32)
    return jnp.concatenate([hi, lo], axis=1)


def _lb_kernel(raw_ref, o_ref):
    raw = raw_ref[...]
    m = jnp.max(raw, axis=0, keepdims=True)
    e = jnp.exp(raw - m)
    soft = e / jnp.sum(e, axis=0, keepdims=True)
    run = jnp.zeros_like(soft[0:1])
    for l in range(raw.shape[0]):
        run = run + soft[l:l + 1]
        o_ref[l:l + 1, :] = run - soft[0:1]


def _hgrn_lower_bounds(raw):
    return pl.pallas_call(
        _lb_kernel, out_shape=jax.ShapeDtypeStruct(raw.shape, F32), name="hgrn_lb")(raw)


def _ada_kernel(c_ref, w_ref, b_ref, o_ref):
    a = _silu(c_ref[...])
    o_ref[0] = jnp.dot(a, w_ref[0], precision=lax.Precision.HIGHEST,
                       preferred_element_type=F32) + b_ref[0]


def _ada_mod(c_pad, ada_w, ada_b):
    depth = ada_w.shape[0]
    rows = c_pad.shape[0]
    return pl.pallas_call(
        _ada_kernel,
        grid=(depth, 6),
        in_specs=[pl.BlockSpec((rows, D_MODEL), lambda l, j: (0, 0)),
                  pl.BlockSpec((1, D_MODEL, D_MODEL), lambda l, j: (l, 0, j)),
                  pl.BlockSpec((1, 1, D_MODEL), lambda l, j: (l, 0, j))],
        out_specs=pl.BlockSpec((1, rows, D_MODEL), lambda l, j: (l, 0, j)),
        out_shape=jax.ShapeDtypeStruct((depth, rows, 6 * D_MODEL), F32),
        compiler_params=_cparams(("arbitrary", "arbitrary")),
        name="ada_mod",
    )(c_pad, ada_w, ada_b.reshape(depth, 1, 6 * D_MODEL))


def _rms_mod(x, g, sc, sh):
    ms = jnp.mean(x * x, axis=-1, keepdims=True)
    return (x * lax.rsqrt(ms + EPS) * g) * (1.0 + sc) + sh


def _norm1(x_ref, mod_ref, g_ref, b):
    sh = mod_ref[pl.ds(b, 1), 0:D_MODEL]
    sc = mod_ref[pl.ds(b, 1), D_MODEL:2 * D_MODEL]
    return _rms_mod(x_ref[...], g_ref[...], sc, sh).astype(BF16)


def _proj_kernel(x_ref, mod_ref, g_ref, w_ref, o_ref, h_ref, *, tiles_per_batch):
    @pl.when(pl.program_id(1) == 0)
    def _():
        h_ref[...] = _norm1(x_ref, mod_ref, g_ref, pl.program_id(0) // tiles_per_batch)

    o_ref[...] = jnp.dot(h_ref[...], w_ref[...], preferred_element_type=F32).astype(o_ref.dtype)


def _norm_project(x2, mod_l, g, w_all, layer, out_dtype, tn, seq, name):
    n = x2.shape[0]
    width = w_all.shape[2]
    tm = min(1024, seq)
    return pl.pallas_call(
        functools.partial(_proj_kernel, tiles_per_batch=seq // tm),
        grid=(n // tm, width // tn),
        in_specs=[pl.BlockSpec((tm, D_MODEL), lambda i, j: (i, 0)),
                  pl.BlockSpec(mod_l.shape, lambda i, j: (0, 0)),
                  pl.BlockSpec((1, D_MODEL), lambda i, j: (0, 0)),
                  pl.BlockSpec((None, D_MODEL, tn), lambda i, j: (layer, 0, j))],
        out_specs=pl.BlockSpec((tm, tn), lambda i, j: (i, j)),
        out_shape=jax.ShapeDtypeStruct((n, width), out_dtype),
        scratch_shapes=[pltpu.VMEM((tm, D_MODEL), BF16)],
        compiler_params=_cparams(("arbitrary", "arbitrary")),
        name=name,
    )(x2, mod_l, g.reshape(1, D_MODEL), w_all)


def _proj_t_kernel(x_ref, mod_ref, g_ref, wt_ref, o_ref, *, tiles_per_batch, chunk):
    h = _norm1(x_ref, mod_ref, g_ref, pl.program_id(0) // tiles_per_batch)
    res = lax.dot_general(wt_ref[...], h, NT_DIMS, preferred_element_type=F32)
    for ci in range(o_ref.shape[0]):
        o_ref[ci] = res[:, ci * chunk:(ci + 1) * chunk].astype(o_ref.dtype)


def _norm_project_t(x2, mod_l, g, wt_all, layer, chunk, seq, name):
    n = x2.shape[0]
    cols = wt_all.shape[1]
    tm = min(1024, seq)
    return pl.pallas_call(
        functools.partial(_proj_t_kernel, tiles_per_batch=seq // tm, chunk=chunk),
        grid=(n // tm,),
        in_specs=[pl.BlockSpec((tm, D_MODEL), lambda i: (i, 0)),
                  pl.BlockSpec(mod_l.shape, lambda i: (0, 0)),
                  pl.BlockSpec((1, D_MODEL), lambda i: (0, 0)),
                  pl.BlockSpec((None, cols, D_MODEL), lambda i: (layer, 0, 0))],
        out_specs=pl.BlockSpec((tm // chunk, cols, chunk), lambda i: (i, 0, 0)),
        out_shape=jax.ShapeDtypeStruct((n // chunk, cols, chunk), BF16),
        compiler_params=_cparams(("arbitrary",)),
        name=name,
    )(x2, mod_l, g.reshape(1, D_MODEL), wt_all)


def _dsa_kernel(qiq_ref, iw_ref, k_ref, vt_ref, ik_ref, bias_ref, o_ref,
                key_ref, hi_ref, lo_ref, madd_ref, iwt_ref, m_ref, l_ref, acc_ref, s_ref, *, tq, topk):
    qi = pl.program_id(1)
    nck = qi + 1
    idx_scale = (IDX_HEADS * IDX_DIM) ** -0.5

    iwt_ref[...] = (iw_ref[...] * idx_scale).T

    krow = lax.broadcasted_iota(jnp.int32, (tq, tq), 0)
    qcol = lax.broadcasted_iota(jnp.int32, (tq, tq), 1)

    def score_chunk(c, carry):
        off = pl.multiple_of(c * tq, tq)
        ikc = ik_ref[pl.ds(off, tq), :].astype(BF16)[:, :IDX_DIM]
        acc = jnp.zeros((tq, tq), F32)
        for h in range(IDX_HEADS):
            s = jnp.dot(ikc, qiq_ref[0, A_WIDTH + h * IDX_DIM:A_WIDTH + (h + 1) * IDX_DIM, :],
                        preferred_element_type=F32)
            acc = acc + jnp.maximum(s, 0.0) * iwt_ref[h:h + 1, :]
        acc = jnp.where(acc == 0.0, 0.0, acc)
        acc = jnp.where(krow + (c - qi) * tq <= qcol, acc, -jnp.inf)
        kb = pltpu.bitcast(acc, jnp.int32)
        key = jnp.where(kb < 0, kb ^ jnp.int32(0x7FFFFFFF), kb)
        key_ref[c] = key
        hi_ref[c] = jnp.right_shift(key, 16).astype(jnp.int16)
        lo_ref[c] = ((key & 0xFFFF) - HALF_BIAS).astype(jnp.int16)
        return carry

    def paired_loop(count_, fn):
        def pair(i, carry):
            fn(2 * i, carry)
            fn(2 * i + 1, carry)
            return carry

        lax.fori_loop(0, count_ // 2, pair, 0)

        @pl.when(count_ % 2 == 1)
        def _():
            fn(count_ - 1, 0)

    paired_loop(nck, score_chunk)

    def count(ref, pred_fn, rows, zero, one):
        def body(c, parts):
            hit = jnp.where(pred_fn(ref[c]), one, zero)
            parts = list(parts)
            for r in range(tq // rows):
                parts[r % COUNT_CHAINS] = parts[r % COUNT_CHAINS] + hit[r * rows:(r + 1) * rows, :]
            return tuple(parts)

        parts = lax.fori_loop(0, nck, body, (jnp.full((rows, tq), zero),) * COUNT_CHAINS)
        return jnp.sum(sum(p.astype(F32) for p in parts), axis=0, keepdims=True)

    def count16(ref, pred_fn):
        return count(ref, pred_fn, 16, jnp.int16(0), jnp.int16(1))

    def count32(pred_fn):
        return count(key_ref, pred_fn, 8, jnp.float32(0.0), jnp.float32(1.0))

    def bisect16(ref, target):
        def bit_step(i, theta):
            cand = theta + jnp.left_shift(jnp.int32(1), 15 - i)
            cand16 = cand.astype(jnp.int16)
            return jnp.where(count16(ref, lambda k: k >= cand16) >= target, cand, theta)

        return lax.fori_loop(0, 16, bit_step, jnp.full((1, tq), -HALF_BIAS, jnp.int32))

    theta_hi = bisect16(hi_ref, float(topk))
    theta_hi16 = theta_hi.astype(jnp.int16)
    need_lo = topk - count16(hi_ref, lambda k: k > theta_hi16)

    def bucket_chunk(c, carry):
        lo_ref[c] = jnp.where(hi_ref[c] == theta_hi16, lo_ref[c], jnp.int16(-HALF_BIAS))
        return carry

    lax.fori_loop(0, nck, bucket_chunk, 0)
    theta_lo = bisect16(lo_ref, need_lo)
    theta = theta_hi * (2 * HALF_BIAS) + (theta_lo + HALF_BIAS)
    theta = jnp.maximum(theta, KEY_NEG_INF + 1)

    def mask_chunk(c, cnt):
        ge = key_ref[c] >= theta
        madd_ref[c] = jnp.where(ge, 0.0, MASK_NEG)
        return cnt + jnp.sum(jnp.where(ge, 1.0, 0.0), axis=0, keepdims=True)

    cnt_ge = lax.fori_loop(0, nck, mask_chunk, jnp.zeros((1, tq), F32))

    @pl.when(jnp.max(cnt_ge) > topk)
    def _():
        need_eq = topk - count32(lambda kc: kc > theta)
        incl = jnp.where(krow >= qcol, 1.0, 0.0).astype(BF16)

        def tie_chunk(c, run):
            kc = key_ref[c]
            eq = kc == theta
            eqf = jnp.where(eq, 1.0, 0.0)
            pref = jnp.dot(incl, eqf.astype(BF16), preferred_element_type=F32) + run
            eq_add = jnp.where(pref <= need_eq, 0.0, MASK_NEG)
            madd_ref[c] = jnp.where(eq, eq_add, jnp.where(kc > theta, 0.0, MASK_NEG))
            return run + jnp.sum(eqf, axis=0, keepdims=True)

        lax.fori_loop(0, nck, tie_chunk, jnp.zeros((1, tq), F32))

    m_ref[...] = jnp.full(m_ref.shape, -jnp.inf, F32)
    l_ref[...] = jnp.zeros(l_ref.shape, F32)
    acc_ref[...] = jnp.zeros(acc_ref.shape, F32)

    ones_rows = jnp.ones((BF16_ROWS, tq), BF16)

    head_slices = [slice(h * A_HEAD_DIM, (h + 1) * A_HEAD_DIM) for h in range(A_HEADS)]

    def logits(c, h):
        off = pl.multiple_of(c * tq, tq)
        s_ref[h] = jnp.dot(k_ref[pl.ds(off, tq), head_slices[h]], qiq_ref[0, head_slices[h], :],
                           preferred_element_type=F32)

    def attend(c, h, lag):
        hs = head_slices[h]
        s = s_ref[h] + madd_ref[c]
        if lag is not None:
            s = s + bias_ref[h, lag]
        m_old = m_ref[h]
        m_new = jnp.maximum(m_old, jnp.max(s, axis=0, keepdims=True))
        alpha = jnp.exp2(m_old - m_new)
        p = jnp.exp2(s - m_new).astype(BF16)
        pv = jnp.dot(jnp.concatenate([vt_ref[c, hs, :], ones_rows], axis=0), p,
                     preferred_element_type=F32)
        l_ref[h] = alpha * l_ref[h] + pv[A_HEAD_DIM:A_HEAD_DIM + 1]
        acc_ref[h] = alpha * acc_ref[h] + pv[:A_HEAD_DIM]
        m_ref[h] = m_new

    def step(c, lag, prefetch):
        for h in range(A_HEADS):
            attend(c, h, lag)
            if prefetch:
                logits(c + 1, h)

    for h in range(A_HEADS):
        logits(0, h)

    def far_chunk(c, carry):
        step(c, None, True)
        return carry

    paired_loop(jnp.maximum(qi - 1, 0), far_chunk)

    @pl.when(qi >= 1)
    def _():
        step(qi - 1, 1, True)

    step(qi, 0, False)

    for h in range(A_HEADS):
        o = acc_ref[h] * (1.0 / l_ref[h])
        o_ref[:, h * A_HEAD_DIM:(h + 1) * A_HEAD_DIM] = o.T.astype(o_ref.dtype)


def _dsa_attention(qiq_t, p16, p32, vt, bias_tiles, bsz, seq):
    tq = min(DSA_TQ, seq)
    nq = seq // tq
    topk = min(TOPK_MAX, seq // 4)
    n = bsz * seq
    one = pl.Buffered(1)
    return pl.pallas_call(
        functools.partial(_dsa_kernel, tq=tq, topk=topk),
        grid=(bsz, nq),
        in_specs=[
            pl.BlockSpec((1, qiq_t.shape[1], tq), lambda b, i: (b * nq + i, 0, 0)),
            pl.BlockSpec((tq, LANES), lambda b, i: (b * nq + i, P32_IW // LANES)),
            pl.BlockSpec((seq, A_WIDTH), lambda b, i: (b, P16_AK), pipeline_mode=one),
            pl.BlockSpec((nq, A_WIDTH, tq), lambda b, i: (b, 0, 0), pipeline_mode=one),
            pl.BlockSpec((seq, LANES), lambda b, i: (b, P32_IK // LANES), pipeline_mode=one),
            pl.BlockSpec(bias_tiles.shape, lambda b, i: (0, 0, 0, 0), pipeline_mode=one),
        ],
        out_specs=pl.BlockSpec((tq, A_WIDTH), lambda b, i: (b * nq + i, 0)),
        out_shape=jax.ShapeDtypeStruct((n, A_WIDTH), BF16),
        scratch_shapes=[pltpu.VMEM((nq, tq, tq), jnp.int32),
                        pltpu.VMEM((nq, tq, tq), jnp.int16),
                        pltpu.VMEM((nq, tq, tq), jnp.int16),
                        pltpu.VMEM((nq, tq, tq), F32),
                        pltpu.VMEM((LANES, tq), F32),
                        pltpu.VMEM((A_HEADS, 1, tq), F32),
                        pltpu.VMEM((A_HEADS, 1, tq), F32),
                        pltpu.VMEM((A_HEADS, A_HEAD_DIM, tq), F32),
                        pltpu.VMEM((A_HEADS, tq, tq), F32)],
        compiler_params=_cparams(("arbitrary", "arbitrary")),
        name="dsa_attention",
    )(qiq_t, p32, p16, vt, p32, bias_tiles)


def _t5_bucket(rel):
    max_exact = REL_BUCKETS // 2
    relf = jnp.maximum(rel, 1).astype(F32)
    large = max_exact + (jnp.log(relf / max_exact) / math.log(REL_MAX_DIST / max_exact)
                         * (REL_BUCKETS - max_exact)).astype(jnp.int32)
    large = jnp.minimum(large, REL_BUCKETS - 1)
    return jnp.where(rel < max_exact, rel, large)


def _bias_tiles(rel_bias, tq):
    assert tq >= REL_MAX_DIST
    key = jnp.arange(tq, dtype=jnp.int32)[:, None]
    qry = jnp.arange(tq, dtype=jnp.int32)[None, :]
    bucket = jnp.stack([_t5_bucket(jnp.maximum(lag * tq + qry - key, 0)) for lag in range(2)])
    rel = ((rel_bias - rel_bias[REL_BUCKETS - 1:REL_BUCKETS]) * LOG2_E).astype(F32)
    onehot = bucket[None] == jnp.arange(REL_BUCKETS, dtype=jnp.int32)[:, None, None, None]
    return jnp.sum(jnp.where(onehot[:, None], rel[:, :, None, None, None], 0.0), axis=0)


def _hgrn_kernel(q_ref, f_ref, i_ref, g_ref, lb_ref, ng_ref, tril_ref, o_ref, st_ref, attn_ref, stage_ref,
                 *, rows):
    @pl.when(pl.program_id(1) == 0)
    def _():
        st_ref[...] = jnp.zeros_like(st_ref)

    lb = lb_ref[...]
    f = lb + (1.0 - lb) * jax.nn.sigmoid(f_ref[...])
    logf = jnp.log(f)
    kk = 1.0 - f
    g1 = logf.astype(BF16)
    r1 = logf - g1.astype(F32)
    g2 = r1.astype(BF16)
    g3 = (r1 - g2.astype(F32)).astype(BF16)
    tril = tril_ref[...]
    bcum = (jnp.dot(tril, g1, preferred_element_type=F32)
            + jnp.dot(tril, g2, preferred_element_type=F32)
            + jnp.dot(tril, g3, preferred_element_type=F32))

    srow = lax.broadcasted_iota(jnp.int32, (HGRN_C, B_HEAD_DIM), 0)
    trow = lax.broadcasted_iota(jnp.int32, (HGRN_SB, HGRN_C), 0)
    scol = lax.broadcasted_iota(jnp.int32, (HGRN_SB, HGRN_C), 1)
    ng = ng_ref[...]
    q = q_ref[...].astype(F32)
    qb_all = (q * jnp.exp(bcum)).astype(BF16)

    tiles = [(n, h) for n in range(rows // HGRN_C) for h in range(B_HEADS)]

    def rs(n):
        return slice(n * HGRN_C, (n + 1) * HGRN_C)

    def hs(h):
        return slice(h * B_HEAD_DIM, (h + 1) * B_HEAD_DIM)

    worst = None
    for j in range(rows // HGRN_SB):
        r0 = j * HGRN_SB
        span = bcum[r0 + HGRN_SB - 1:r0 + HGRN_SB]
        if r0 % HGRN_C:
            span = span - bcum[r0 - 1:r0]
        worst = span if worst is None else jnp.minimum(worst, span)
    in_range = jnp.min(worst) > -HGRN_MAX_LOG_DECAY

    @pl.when(in_range)
    def _():
        a_parts = {}
        for n, h in tiles:
            bc, qc, kc = bcum[rs(n), hs(h)], q[rs(n), hs(h)], kk[rs(n), hs(h)]
            for sb in range(HGRN_C // HGRN_SB):
                s0 = sb * HGRN_SB
                beta = bc[s0 - 1:s0] if sb > 0 else jnp.zeros((1, B_HEAD_DIM), F32)
                qs = (qc[s0:s0 + HGRN_SB] * jnp.exp(bc[s0:s0 + HGRN_SB] - beta)).astype(BF16)
                live = s0 + HGRN_SB
                ks = (kc[:live] * jnp.exp(beta - bc[:live])).astype(BF16)
                if live < HGRN_C:
                    ks = jnp.concatenate([ks, jnp.zeros((HGRN_C - live, B_HEAD_DIM), BF16)], axis=0)
                a_parts[n, h, sb] = lax.dot_general(qs, ks, NT_DIMS, preferred_element_type=F32)
        for ti, (n, h) in enumerate(tiles):
            a_rows = [jnp.where(scol <= trow + sb * HGRN_SB, a_parts[n, h, sb], 0.0)
                      for sb in range(HGRN_C // HGRN_SB)]
            attn_ref[ti] = jnp.concatenate(a_rows, axis=0)

    @pl.when(jnp.logical_not(in_range))
    def _():
        for h in range(B_HEADS):
            stage_ref[0, h] = bcum[:, hs(h)]
            stage_ref[1, h] = q[:, hs(h)]
            stage_ref[2, h] = kk[:, hs(h)]
        t_idx = lax.broadcasted_iota(jnp.int32, (HGRN_C, HGRN_C), 0)
        s_idx = lax.broadcasted_iota(jnp.int32, (HGRN_C, HGRN_C), 1)

        def safe_tile(ti, carry):
            n, h = ti // B_HEADS, ti % B_HEADS
            r0 = pl.multiple_of(n * HGRN_C, HGRN_C)
            bc = stage_ref[0, h, pl.ds(r0, HGRN_C), :]
            qc = stage_ref[1, h, pl.ds(r0, HGRN_C), :]
            kc = stage_ref[2, h, pl.ds(r0, HGRN_C), :]
            acc = jnp.where(t_idx == s_idx, jnp.sum(qc * kc, axis=-1, keepdims=True), 0.0)
            block = HGRN_C
            while block >= 2:
                half = block // 2
                ref_row = (t_idx & -block) + (half - 1)
                bref = jnp.dot(jnp.where(s_idx == ref_row, 1.0, 0.0), bc, precision=lax.Precision.HIGHEST,
                               preferred_element_type=F32)
                second = (srow & (block - 1)) >= half
                qs = jnp.where(second, qc * jnp.exp(jnp.where(second, bc - bref, 0.0)), 0.0).astype(BF16)
                ks = jnp.where(second, 0.0, kc * jnp.exp(jnp.where(second, 0.0, bref - bc))).astype(BF16)
                a = lax.dot_general(qs, ks, NT_DIMS, preferred_element_type=F32)
                acc = acc + jnp.where((t_idx & -block) == (s_idx & -block), a, 0.0)
                block = half
            attn_ref[ti] = acc
            return carry

        lax.fori_loop(0, len(tiles), safe_tile, 0)

    intra, upd, dec = {}, {}, {}
    for ti, (n, h) in enumerate(tiles):
        attn = attn_ref[ti].astype(BF16)
        vc = i_ref[rs(n), hs(h)]
        intra[n, h] = jnp.dot(attn, vc, preferred_element_type=F32)
        bc = bcum[rs(n), hs(h)]
        blast = bc[HGRN_C - 1:HGRN_C]
        kdec = (kk[rs(n), hs(h)] * jnp.exp(blast - bc)).astype(BF16)
        upd[n, h] = lax.dot_general(vc, kdec, TN_DIMS, preferred_element_type=F32)
        dec[n, h] = jnp.exp(blast)
    for n, h in tiles:
        st = st_ref[h]
        o = intra[n, h] + lax.dot_general(qb_all[rs(n), hs(h)], st.astype(BF16), NT_DIMS,
                                          preferred_element_type=F32)
        st_ref[h] = st * dec[n, h] + upd[n, h]
        ms = jnp.mean(o * o, axis=-1, keepdims=True)
        on = o * lax.rsqrt(ms + EPS) * ng
        o_ref[rs(n), hs(h)] = (on * _silu(g_ref[rs(n), hs(h)].astype(F32))).astype(o_ref.dtype)


def _hgrn2(p16, p32, lb_l, norm_g, bsz, seq):
    rows = min(HGRN_L, seq)
    nj = seq // rows
    n = bsz * seq
    r = jnp.arange(rows, dtype=jnp.int32)
    tril = ((r[:, None] >= r[None, :]) & (r[:, None] // HGRN_C == r[None, :] // HGRN_C)).astype(BF16)

    def col(base):
        return lambda b, j: (b * nj + j, base)

    return pl.pallas_call(
        functools.partial(_hgrn_kernel, rows=rows),
        grid=(bsz, nj),
        in_specs=[
            pl.BlockSpec((rows, B_WIDTH), col(P16_BQ)),
            pl.BlockSpec((rows, B_WIDTH), col(P32_BF // B_WIDTH)),
            pl.BlockSpec((rows, B_WIDTH), col(P16_BI)),
            pl.BlockSpec((rows, B_WIDTH), col(P16_BG)),
            pl.BlockSpec((1, B_WIDTH), lambda b, j: (0, 0)),
            pl.BlockSpec((1, B_HEAD_DIM), lambda b, j: (0, 0)),
            pl.BlockSpec((rows, rows), lambda b, j: (0, 0)),
        ],
        out_specs=pl.BlockSpec((rows, B_WIDTH), lambda b, j: (b * nj + j, 0)),
        out_shape=jax.ShapeDtypeStruct((n, B_WIDTH), BF16),
        scratch_shapes=[pltpu.VMEM((B_HEADS, B_HEAD_DIM, B_HEAD_DIM), F32),
                        pltpu.VMEM((rows // HGRN_C * B_HEADS, HGRN_C, HGRN_C), F32),
                        pltpu.VMEM((3, B_HEADS, rows, B_HEAD_DIM), F32)],
        compiler_params=_cparams(("arbitrary", "arbitrary")),
        name="hgrn2",
    )(p16, p32, p16, p16, lb_l.reshape(1, B_WIDTH), norm_g.reshape(1, B_HEAD_DIM), tril)


def _ret_kernel(q_ref, k_ref, v_ref, g_ref, cos_ref, sin_ref, idec_ref, qdec_ref, kdec_ref, cdec_ref,
                o_ref, st_ref):
    @pl.when(pl.program_id(1) == 0)
    def _():
        st_ref[...] = jnp.zeros_like(st_ref)

    heads = range(C_HEADS)
    cos = jnp.concatenate([cos_ref[...]] * C_HEADS, axis=1)
    sin_signed = jnp.concatenate([sin_ref[...]] * C_HEADS, axis=1)
    even = lax.broadcasted_iota(jnp.int32, cos.shape, 1) % 2 == 0

    def rot(a):
        swapped = jnp.where(even, pltpu.roll(a, C_QK_WIDTH - 1, 1), pltpu.roll(a, 1, 1))
        return a * cos + swapped * sin_signed

    qr = rot(q_ref[...].astype(F32))
    kr = rot(k_ref[...].astype(F32))
    qk = [slice(h * C_QK_DIM, (h + 1) * C_QK_DIM) for h in heads]
    vs = [slice(h * C_V_DIM, (h + 1) * C_V_DIM) for h in heads]
    attn = [lax.dot_general(qr[:, qk[h]].astype(BF16), kr[:, qk[h]].astype(BF16), NT_DIMS,
                            preferred_element_type=F32) * idec_ref[h] for h in heads]
    inter = [jnp.dot((qr[:, qk[h]] * qdec_ref[h]).astype(BF16), st_ref[h].astype(BF16),
                     preferred_element_type=F32) for h in heads]
    intra = [jnp.dot(attn[h].astype(BF16), v_ref[:, vs[h]], preferred_element_type=F32) for h in heads]
    upd = [jnp.dot((kr[:, qk[h]] * kdec_ref[h]).T.astype(BF16), v_ref[:, vs[h]],
                   preferred_element_type=F32) for h in heads]
    for h in heads:
        st_ref[h] = cdec_ref[h, 0:1, :] * st_ref[h] + upd[h]
        o = intra[h] + inter[h]
        ms = jnp.mean(o * o, axis=-1, keepdims=True)
        o_ref[:, vs[h]] = (_silu(g_ref[:, vs[h]].astype(F32)) * (o * lax.rsqrt(ms + EPS))).astype(o_ref.dtype)


def _retention_tables(seq):
    pos = jnp.arange(seq, dtype=F32)
    theta = jnp.repeat(1.0 / (10000.0 ** jnp.linspace(0.0, 1.0, C_QK_DIM // 2)), 2)
    ang = pos[:, None] * theta[None, :]
    pair_sign = jnp.where(jnp.arange(C_QK_DIM) % 2 == 0, -1.0, 1.0)
    log_gamma = jnp.log(1.0 - 2.0 ** (-5.0 - jnp.arange(C_HEADS, dtype=F32)))
    idx = jnp.arange(RET_C, dtype=F32)
    causal = idx[:, None] >= idx[None, :]
    idec = jnp.exp(jnp.where(causal[None], (idx[:, None] - idx[None, :])[None] * log_gamma[:, None, None],
                             -jnp.inf))
    qdec = jnp.exp((idx + 1.0)[None, :] * log_gamma[:, None])[..., None]
    kdec = jnp.exp((RET_C - 1.0 - idx)[None, :] * log_gamma[:, None])[..., None]
    cdec = jnp.exp(RET_C * log_gamma)[:, None, None]
    return (jnp.cos(ang), jnp.sin(ang) * pair_sign[None, :], idec,
            jnp.broadcast_to(qdec, (C_HEADS, RET_C, C_QK_DIM)),
            jnp.broadcast_to(kdec, (C_HEADS, RET_C, C_QK_DIM)),
            jnp.broadcast_to(cdec, (C_HEADS, 8, C_V_DIM)))


def _retention(p16, tables, bsz, seq):
    cos, sin, idec, qdec, kdec, cdec = tables
    nj = seq // RET_C
    n = bsz * seq
    v_blk = C_V_WIDTH // 1024

    def whole(a):
        return pl.BlockSpec(a.shape, lambda b, j: (0,) * a.ndim)

    return pl.pallas_call(
        _ret_kernel,
        grid=(bsz, nj),
        in_specs=[
            pl.BlockSpec((RET_C, C_QK_WIDTH), lambda b, j: (b * nj + j, P16_CQ)),
            pl.BlockSpec((RET_C, C_QK_WIDTH), lambda b, j: (b * nj + j, P16_CK)),
            pl.BlockSpec((RET_C, C_V_WIDTH), lambda b, j: (b * nj + j, P16_CV // v_blk)),
            pl.BlockSpec((RET_C, C_V_WIDTH), lambda b, j: (b * nj + j, P16_CG // v_blk)),
            pl.BlockSpec((RET_C, C_QK_DIM), lambda b, j: (j, 0)),
            pl.BlockSpec((RET_C, C_QK_DIM), lambda b, j: (j, 0)),
            whole(idec), whole(qdec), whole(kdec), whole(cdec),
        ],
        out_specs=pl.BlockSpec((RET_C, C_V_WIDTH), lambda b, j: (b * nj + j, 0)),
        out_shape=jax.ShapeDtypeStruct((n, C_V_WIDTH), BF16),
        scratch_shapes=[pltpu.VMEM((C_HEADS, C_QK_DIM, C_V_DIM), F32)],
        compiler_params=_cparams(("arbitrary", "arbitrary")),
        name="retention",
    )(p16, p16, p16, p16, cos, sin, idec, qdec, kdec, cdec)


def _merge_kernel(oa_ref, ob_ref, oc_ref, ga_ref, gb_ref, gc_ref, x_ref, mod_ref, g2_ref,
                  wa_ref, wb_ref, wc_ref, wo_ref, wrh_ref, wrl_ref, br_ref,
                  x1_ref, h2_ref, route_ref, *, tiles_per_batch):
    b = pl.program_id(0) // tiles_per_batch

    def gated(o_ref, w_ref, g_ref):
        y = jnp.dot(o_ref[...], w_ref[...], preferred_element_type=F32)
        return jax.nn.sigmoid(g_ref[...].astype(F32)) * y

    merged = gated(oa_ref, wa_ref, ga_ref) + gated(ob_ref, wb_ref, gb_ref) + gated(oc_ref, wc_ref, gc_ref)
    y = jnp.dot(merged.astype(BF16), wo_ref[...], preferred_element_type=F32)
    gt1 = mod_ref[pl.ds(b, 1), 2 * D_MODEL:3 * D_MODEL]
    x1 = x_ref[...] + gt1 * y
    x1_ref[...] = x1
    sh2 = mod_ref[pl.ds(b, 1), 3 * D_MODEL:4 * D_MODEL]
    sc2 = mod_ref[pl.ds(b, 1), 4 * D_MODEL:5 * D_MODEL]
    h2 = _rms_mod(x1, g2_ref[...], sc2, sh2)
    h_hi = h2.astype(BF16)
    words = _pack_bf16_pairs(h2)
    for j in range(h2_ref.shape[0]):
        h2_ref[j] = words[:, j * SC_SUB:(j + 1) * SC_SUB]
    h_lo = (h2 - h_hi.astype(F32)).astype(BF16)
    logits = (jnp.dot(h_hi, wrh_ref[...], preferred_element_type=F32)
              + jnp.dot(h_lo, wrh_ref[...], preferred_element_type=F32)
              + jnp.dot(h_hi, wrl_ref[...], preferred_element_type=F32)) + br_ref[...]
    lane = lax.broadcasted_iota(jnp.int32, logits.shape, 1).astype(F32)
    neg_inf = -jnp.inf

    def first_argmax(vals):
        top = jnp.max(vals, axis=-1, keepdims=True)
        idx = jnp.min(jnp.where(vals == top, lane, float(LANES)), axis=-1, keepdims=True)
        return top, idx

    gl = jnp.where(lane < N_GROUPS, logits, neg_inf)
    gmax, gsel = first_argmax(gl)
    gprob = 1.0 / jnp.sum(jnp.exp(gl - gmax), axis=-1, keepdims=True)
    lo = N_GROUPS + EXPERTS_PER_GROUP * gsel
    el = jnp.where((lane >= lo) & (lane < lo + EXPERTS_PER_GROUP), logits, neg_inf)
    v1, i1 = first_argmax(el)
    el2 = jnp.where(lane == i1, neg_inf, el)
    v2, i2 = first_argmax(el2)
    e2 = jnp.exp(v2 - v1)
    den = 1.0 + e2
    route_ref[...] = jnp.where(lane == 0.0, i1 - N_GROUPS,
                               jnp.where(lane == 1.0, i2 - N_GROUPS,
                                         jnp.where(lane == 2.0, gprob / den,
                                                   jnp.where(lane == 3.0, gprob * (e2 / den), 0.0))))


def _merge(o_a, o_b, o_c, p16, x2, mod_l, g2, wa, wb, wc, wo, wr_hi, wr_lo, br, seq):
    n = x2.shape[0]
    tm = min(512, seq)
    one = pl.Buffered(1)

    def rows(width, cb=0):
        return pl.BlockSpec((tm, width), lambda i: (i, cb))

    def whole(a):
        return pl.BlockSpec(a.shape, lambda i: (0,) * a.ndim, pipeline_mode=one)

    return pl.pallas_call(
        functools.partial(_merge_kernel, tiles_per_batch=seq // tm),
        grid=(n // tm,),
        in_specs=[rows(A_WIDTH), rows(B_WIDTH), rows(C_V_WIDTH),
                  rows(D_MODEL, P16_GA), rows(D_MODEL, P16_GB), rows(D_MODEL, P16_GC),
                  rows(D_MODEL), whole(mod_l), pl.BlockSpec((1, D_MODEL), lambda i: (0, 0)),
                  whole(wa), whole(wb), whole(wc), whole(wo), whole(wr_hi), whole(wr_lo), whole(br)],
        out_specs=[rows(D_MODEL), pl.BlockSpec((SC_PIECES, tm, SC_SUB), lambda i: (0, i, 0)), rows(LANES)],
        out_shape=[jax.ShapeDtypeStruct((n, D_MODEL), F32),
                   jax.ShapeDtypeStruct((SC_PIECES, n, SC_SUB), jnp.int32),
                   jax.ShapeDtypeStruct((n, LANES), F32)],
        compiler_params=_cparams(("arbitrary",)),
        name="merge_route",
    )(o_a, o_b, o_c, p16, p16, p16, x2, mod_l, g2.reshape(1, D_MODEL), wa, wb, wc, wo, wr_hi, wr_lo, br)


def _lane_pick(vals, lane, idx):
    return jnp.sum(jnp.where(lane == idx, vals, 0.0), axis=-1, keepdims=True)


def _rank_kernel(route_ref, rk_ref, cnt_ref, run_ref):
    @pl.when(pl.program_id(0) == 0)
    def _():
        run_ref[...] = jnp.zeros_like(run_ref)

    route = route_ref[...]
    tb = route.shape[0]
    lane = lax.broadcasted_iota(jnp.int32, route.shape, 1).astype(F32)
    e1 = _lane_pick(route, lane, 0.0)
    e2 = _lane_pick(route, lane, 1.0)
    sel = jnp.where((lane == e1) | (lane == e2), 1.0, 0.0)
    r = lax.broadcasted_iota(jnp.int32, (tb, tb), 0)
    c = lax.broadcasted_iota(jnp.int32, (tb, tb), 1)
    before = jnp.where(c < r, 1.0, 0.0).astype(BF16)
    rank = jnp.dot(before, sel.astype(BF16), preferred_element_type=F32) + run_ref[0:1, :]
    rk_ref[...] = jnp.where(lane == 0.0, _lane_pick(rank, lane, e1),
                            jnp.where(lane == 1.0, _lane_pick(rank, lane, e2), 0.0))
    run_ref[...] = run_ref[...] + jnp.sum(sel, axis=0, keepdims=True)
    cnt_ref[...] = run_ref[...]


def _expert_ranks(route):
    n = route.shape[0]
    tb = min(PLAN_TB, n)
    return pl.pallas_call(
        _rank_kernel,
        grid=(n // tb,),
        in_specs=[pl.BlockSpec((tb, LANES), lambda i: (i, 0))],
        out_specs=[pl.BlockSpec((tb, LANES), lambda i: (i, 0)),
                   pl.BlockSpec((8, LANES), lambda i: (0, 0))],
        out_shape=[jax.ShapeDtypeStruct((n, LANES), F32), jax.ShapeDtypeStruct((8, LANES), F32)],
        scratch_shapes=[pltpu.VMEM((8, LANES), F32)],
        compiler_params=_cparams(("arbitrary",)),
        name="expert_ranks",
    )(route)


def _plan_kernel(cnt_ref, route_ref, rk_ref, pos_ref, tmap_ref):
    lane_i = lax.broadcasted_iota(jnp.int32, (8, LANES), 1)
    cnt = jnp.where(lane_i < N_EXPERTS, cnt_ref[...], 0.0)
    padded = jnp.floor((cnt + (MOE_TM - 1)) * (1.0 / MOE_TM)) * MOE_TM
    r = lax.broadcasted_iota(jnp.int32, (LANES, LANES), 0)
    c = lax.broadcasted_iota(jnp.int32, (LANES, LANES), 1)
    base = jnp.dot(padded, jnp.where(r < c, 1.0, 0.0), precision=lax.Precision.HIGHEST,
                   preferred_element_type=F32)

    route = route_ref[...]
    lane = lax.broadcasted_iota(jnp.int32, route.shape, 1).astype(F32)
    rk = rk_ref[...]
    base_row = base[0:1, :]
    pos1 = _lane_pick(base_row, lane, _lane_pick(route, lane, 0.0)) + _lane_pick(rk, lane, 0.0)
    pos2 = _lane_pick(base_row, lane, _lane_pick(route, lane, 1.0)) + _lane_pick(rk, lane, 1.0)
    pos_ref[...] = jnp.where(lane == 0.0, pos1, jnp.where(lane == 1.0, pos2, 0.0)).astype(jnp.int32)

    @pl.when(pl.program_id(0) == 0)
    def _():
        nt = tmap_ref.shape[0]
        tlane = lax.broadcasted_iota(jnp.int32, (nt, LANES), 1)
        start = (lax.broadcasted_iota(jnp.int32, (nt, LANES), 0) * MOE_TM).astype(F32)
        end_row = jnp.where(tlane < N_EXPERTS, base_row + padded[0:1, :], 3e38)
        expert = jnp.sum(jnp.where(end_row <= start, 1.0, 0.0), axis=-1, keepdims=True)
        expert_c = jnp.minimum(expert, N_EXPERTS - 1.0)
        tl = tlane.astype(F32)
        left = _lane_pick(cnt[0:1, :], tl, expert_c) - (start[:, 0:1] - _lane_pick(base_row, tl, expert_c))
        valid = jnp.where(expert < N_EXPERTS, jnp.clip(left, 0.0, float(MOE_TM)), 0.0)
        tmap_ref[...] = jnp.where(tlane == 0, expert_c, jnp.where(tlane == 1, valid, 0.0)).astype(jnp.int32)


def _expert_plan(cnt, route, rk, n_tiles):
    n = route.shape[0]
    tb = min(PLAN_TB, n)
    nt_pad = -(-n_tiles // 8) * 8
    return pl.pallas_call(
        _plan_kernel,
        grid=(n // tb,),
        in_specs=[pl.BlockSpec((8, LANES), lambda i: (0, 0)),
                  pl.BlockSpec((tb, LANES), lambda i: (i, 0)),
                  pl.BlockSpec((tb, LANES), lambda i: (i, 0))],
        out_specs=[pl.BlockSpec((tb, LANES), lambda i: (i, 0)),
                   pl.BlockSpec((nt_pad, LANES), lambda i: (0, 0))],
        out_shape=[jax.ShapeDtypeStruct((n, LANES), jnp.int32),
                   jax.ShapeDtypeStruct((nt_pad, LANES), jnp.int32)],
        compiler_params=_cparams(("arbitrary",)),
        name="expert_plan",
    )(cnt, route, rk)


def _sc_mesh():
    return plsc.VectorSubcoreMesh(core_axis_name="c", subcore_axis_name="s")


def _sc_scatter_rows(src, idx, out_rows):
    m = idx.shape[0]
    n_src_win = src.shape[0] // SC_WINDOW

    @functools.partial(pl.kernel, out_type=jax.ShapeDtypeStruct((out_rows, src.shape[1]), src.dtype),
                       mesh=_sc_mesh(), scratch_types=[])
    def scatter(x_hbm, i_hbm, o_hbm):
        def body(x_vmem, i_vmem):
            pltpu.sync_copy(x_vmem, o_hbm.at[i_vmem.at[0]])

        pltpu.emit_pipeline(
            body, grid=(m // SC_WINDOW,),
            in_specs=[pl.BlockSpec((SC_WINDOW, src.shape[1]), lambda i: (i % n_src_win, 0)),
                      pl.BlockSpec((1, SC_WINDOW), lambda i: (0, i))],
            out_specs=[], core_axis_name=("c", "s"),
            dimension_semantics=(pltpu.PARALLEL,))(x_hbm, i_hbm)

    return scatter(src, idx.reshape(1, m))


def _sc_gather_rows(table, idx):
    m = idx.shape[0]

    @functools.partial(pl.kernel, out_type=jax.ShapeDtypeStruct((m, table.shape[1]), table.dtype),
                       mesh=_sc_mesh(), scratch_types=[])
    def gather(x_hbm, i_hbm, o_hbm):
        def body(i_vmem, o_vmem):
            pltpu.sync_copy(x_hbm.at[i_vmem.at[0]], o_vmem)

        pltpu.emit_pipeline(
            body, grid=(m // SC_WINDOW,),
            in_specs=[pl.BlockSpec((1, SC_WINDOW), lambda i: (0, i))],
            out_specs=[pl.BlockSpec((SC_WINDOW, table.shape[1]), lambda i: (i, 0))],
            core_axis_name=("c", "s"),
            dimension_semantics=(pltpu.PARALLEL,))(i_hbm, o_hbm)

    return gather(table, idx.reshape(1, m))


def _piece_row_index(pos, rows):
    return (jnp.arange(SC_PIECES, dtype=jnp.int32)[:, None] * rows + pos[None, :]).reshape(-1)


def _grouped_kernel(te_ref, tv_ref, x_ref, wg_ref, wu_ref, wd_ref, o_ref):
    valid = tv_ref[pl.program_id(0)]

    @pl.when(valid > 0)
    def _():
        words = jnp.concatenate([x_ref[j] for j in range(SC_PIECES)], axis=1)
        row = lax.broadcasted_iota(jnp.int32, words.shape, 0)
        words = jnp.where(row < valid, words, 0)
        x = _unpack_bf16_pairs(words).astype(BF16)
        a = jnp.dot(x, wg_ref[0].astype(BF16), preferred_element_type=F32)
        u = jnp.dot(x, wu_ref[0].astype(BF16), preferred_element_type=F32)
        hm = (_silu(a) * u).astype(BF16)
        out = _pack_bf16_pairs(jnp.dot(hm, wd_ref[0].astype(BF16), preferred_element_type=F32))
        for j in range(SC_PIECES):
            o_ref[j] = out[:, j * SC_SUB:(j + 1) * SC_SUB]

    @pl.when(valid <= 0)
    def _():
        o_ref[...] = jnp.zeros_like(o_ref)


def _grouped_experts(tile_expert, tile_valid, xs, wg, wu, wd, layer):
    n_tiles = tile_expert.shape[0]
    rows_block = pl.BlockSpec((SC_PIECES, MOE_TM, SC_SUB), lambda i, te, tv: (0, i, 0))
    return pl.pallas_call(
        _grouped_kernel,
        grid_spec=pltpu.PrefetchScalarGridSpec(
            num_scalar_prefetch=2,
            grid=(n_tiles,),
            in_specs=[rows_block,
                      pl.BlockSpec((None, 1, D_MODEL, D_EXPERT), lambda i, te, tv: (layer, te[i], 0, 0)),
                      pl.BlockSpec((None, 1, D_MODEL, D_EXPERT), lambda i, te, tv: (layer, te[i], 0, 0)),
                      pl.BlockSpec((None, 1, D_EXPERT, D_MODEL), lambda i, te, tv: (layer, te[i], 0, 0))],
            out_specs=rows_block),
        out_shape=jax.ShapeDtypeStruct(xs.shape, jnp.int32),
        compiler_params=_cparams(("arbitrary",)),
        name="grouped_experts",
    )(tile_expert, tile_valid, xs, wg, wu, wd)


def _combine_kernel(x1_ref, y_ref, route_ref, mod_ref, o_ref, *, tiles_per_batch):
    b = pl.program_id(0) // tiles_per_batch
    gt2 = mod_ref[pl.ds(b, 1), 5 * D_MODEL:6 * D_MODEL]
    route = route_ref[...]
    lane = lax.broadcasted_iota(jnp.int32, route.shape, 1).astype(F32)
    w1 = _lane_pick(route, lane, 2.0)
    w2 = _lane_pick(route, lane, 3.0)
    y1 = _unpack_bf16_pairs(jnp.concatenate([y_ref[0, j] for j in range(SC_PIECES)], axis=1))
    y2 = _unpack_bf16_pairs(jnp.concatenate([y_ref[1, j] for j in range(SC_PIECES)], axis=1))
    o_ref[...] = x1_ref[...] + gt2 * (w1 * y1 + w2 * y2)


def _combine(x1, y2, route, mod_l, seq):
    n = x1.shape[0]
    tm = min(1024, seq)
    return pl.pallas_call(
        functools.partial(_combine_kernel, tiles_per_batch=seq // tm),
        grid=(n // tm,),
        in_specs=[pl.BlockSpec((tm, D_MODEL), lambda i: (i, 0)),
                  pl.BlockSpec((2, SC_PIECES, tm, SC_SUB), lambda i: (0, 0, i, 0)),
                  pl.BlockSpec((tm, LANES), lambda i: (i, 0)),
                  pl.BlockSpec(mod_l.shape, lambda i: (0, 0))],
        out_specs=pl.BlockSpec((tm, D_MODEL), lambda i: (i, 0)),
        out_shape=jax.ShapeDtypeStruct((n, D_MODEL), F32),
        compiler_params=_cparams(("arbitrary",)),
        name="moe_combine",
    )(x1, y2, route, mod_l)


def _moe(h2, route, x1, mod_l, wg, wu, wd, layer, seq):
    n = h2.shape[1]
    n_tiles = (2 * n) // MOE_TM + N_EXPERTS
    rows = n_tiles * MOE_TM
    rk, cnt = _expert_ranks(route)
    pos, tmap = _expert_plan(cnt, route, rk, n_tiles)
    idx = jnp.concatenate([_piece_row_index(pos[:, 0], rows), _piece_row_index(pos[:, 1], rows)])
    xs = _sc_scatter_rows(h2.reshape(SC_PIECES * n, SC_SUB), idx, SC_PIECES * rows)
    ys = _grouped_experts(tmap[:n_tiles, 0], tmap[:n_tiles, 1], xs.reshape(SC_PIECES, rows, SC_SUB),
                          wg, wu, wd, layer)
    y2 = _sc_gather_rows(ys.reshape(SC_PIECES * rows, SC_SUB), idx).reshape(2, SC_PIECES, n, SC_SUB)
    return _combine(x1, y2, route, mod_l, seq)


def _final_norm_kernel(x_ref, g_ref, o_ref):
    x = x_ref[...]
    ms = jnp.mean(x * x, axis=-1, keepdims=True)
    o_ref[...] = x * lax.rsqrt(ms + EPS) * g_ref[...]


def _final_norm(x2, g, seq):
    n = x2.shape[0]
    tm = min(1024, seq)
    return pl.pallas_call(
        _final_norm_kernel,
        grid=(n // tm,),
        in_specs=[pl.BlockSpec((tm, D_MODEL), lambda i: (i, 0)),
                  pl.BlockSpec((1, D_MODEL), lambda i: (0, 0))],
        out_specs=pl.BlockSpec((tm, D_MODEL), lambda i: (i, 0)),
        out_shape=jax.ShapeDtypeStruct((n, D_MODEL), F32),
        compiler_params=_cparams(("arbitrary",)),
        name="final_norm",
    )(x2, g.reshape(1, D_MODEL))


_IN_OFFS = [sum(IN_SPLITS[:i]) for i in range(len(IN_SPLITS) + 1)]
(_AQ, _AK, _AV, _IQ, _IK, _IW, _BQ, _BF, _BI, _BG, _CQ, _CK, _CV, _CG, _GA, _GB, _GC) = range(len(IN_SPLITS))


def _pack_kernel(w_ref, w16_ref, w32_ref, wvt_ref, wqt_ref):
    def cols(seg):
        return w_ref[0, :, _IN_OFFS[seg]:_IN_OFFS[seg + 1]]

    scale = {_AQ: A_HEAD_DIM ** -0.5 * LOG2_E, _CK: C_QK_DIM ** -0.5}
    at = 0
    for seg in (_CV, _CG, _AK, _BQ, _BI, _BG, _CQ, _CK, _GA, _GB, _GC):
        v = cols(seg)
        if seg in scale:
            v = v * scale[seg]
        w16_ref[0, :, at:at + v.shape[1]] = v.astype(BF16)
        at += v.shape[1]

    rows = w_ref.shape[1]
    w32_ref[0, :, P32_BF:P32_BF + B_WIDTH] = cols(_BF).astype(BF16)
    zeros = jnp.zeros((rows, LANES - IDX_DIM), F32)
    w32_ref[0, :, P32_IK:P32_IK + LANES] = jnp.concatenate([cols(_IK), zeros], axis=1).astype(BF16)
    w32_ref[0, :, P32_IW:P32_IW + LANES] = jnp.concatenate(
        [cols(_IW), jnp.zeros((rows, LANES - IDX_HEADS), F32)], axis=1).astype(BF16)
    wvt_ref[0] = cols(_AV).T.astype(BF16)
    wqt_ref[0, 0:A_WIDTH, :] = (cols(_AQ) * scale[_AQ]).T.astype(BF16)
    wqt_ref[0, A_WIDTH:, :] = cols(_IQ).T.astype(BF16)


def _pack_w_in(w_in):
    depth, d, width = w_in.shape
    rows = LANES
    w16_width = 13 * 1024
    qiq_rows = A_WIDTH + IDX_HEADS * IDX_DIM
    w32_width = P32_IW + LANES
    return pl.pallas_call(
        _pack_kernel,
        grid=(depth, d // rows),
        in_specs=[pl.BlockSpec((1, rows, width), lambda l, r: (l, r, 0))],
        out_specs=[pl.BlockSpec((1, rows, w16_width), lambda l, r: (l, r, 0)),
                   pl.BlockSpec((1, rows, w32_width), lambda l, r: (l, r, 0)),
                   pl.BlockSpec((1, A_WIDTH, rows), lambda l, r: (l, 0, r)),
                   pl.BlockSpec((1, qiq_rows, rows), lambda l, r: (l, 0, r))],
        out_shape=[jax.ShapeDtypeStruct((depth, d, w16_width), BF16),
                   jax.ShapeDtypeStruct((depth, d, w32_width), BF16),
                   jax.ShapeDtypeStruct((depth, A_WIDTH, d), BF16),
                   jax.ShapeDtypeStruct((depth, qiq_rows, d), BF16)],
        compiler_params=_cparams(("arbitrary", "arbitrary")),
        name="pack_w_in",
    )(w_in)


def _split_bf16(w):
    hi = w.astype(BF16)
    return hi, (w - hi.astype(F32)).astype(BF16)


def kernel(x, c, rel_bias, hgrn_lb_raw, norm1_g, norm2_g, ada_w, ada_b, w_in, hgrn_norm_g, w_branch_a,
           w_branch_b, w_branch_c, w_out, router_group_w, router_group_b, router_expert_w,
           router_expert_b, expert_w_gate, expert_w_up, expert_w_down, final_norm_g):
    bsz, seq, _ = x.shape
    depth = w_in.shape[0]
    n = bsz * seq
    x2 = x.reshape(n, D_MODEL)
    tq = min(DSA_TQ, seq)

    lb_all = _hgrn_lower_bounds(hgrn_lb_raw)
    c_pad = jnp.pad(c, ((0, (-bsz) % 8), (0, 0)))
    mod = _ada_mod(c_pad, ada_w, ada_b)
    bias_tiles = _bias_tiles(rel_bias, tq)
    ret_tables = _retention_tables(seq)
    w16_all, w32_all, wvt_all, wqt_all = _pack_w_in(w_in)

    for l in range(depth):
        p16 = _norm_project(x2, mod[l], norm1_g[l], w16_all, l, BF16, 1024, seq, "proj_bf16")
        p32 = _norm_project(x2, mod[l], norm1_g[l], w32_all, l, F32, 640, seq, "proj_f32")
        vt = _norm_project_t(x2, mod[l], norm1_g[l], wvt_all, l, tq, seq, "proj_vt")
        qiq_t = _norm_project_t(x2, mod[l], norm1_g[l], wqt_all, l, tq, seq, "proj_qt")
        o_a = _dsa_attention(qiq_t, p16, p32, vt, bias_tiles, bsz, seq)
        o_b = _hgrn2(p16, p32, lb_all[l], hgrn_norm_g[l], bsz, seq)
        o_c = _retention(p16, ret_tables, bsz, seq)
        wr = jnp.concatenate([router_group_w[l], router_expert_w[l],
                              jnp.zeros((D_MODEL, LANES - N_GROUPS - N_EXPERTS), F32)], axis=1)
        br = jnp.concatenate([router_group_b[l], router_expert_b[l],
                              jnp.zeros((LANES - N_GROUPS - N_EXPERTS,), F32)]).reshape(1, LANES)
        wr_hi, wr_lo = _split_bf16(wr)
        x1, h2, route = _merge(o_a, o_b, o_c, p16, x2, mod[l], norm2_g[l],
                               w_branch_a[l].astype(BF16), w_branch_b[l].astype(BF16),
                               w_branch_c[l].astype(BF16), w_out[l].astype(BF16),
                               wr_hi, wr_lo, br, seq)
        x2 = _moe(h2, route, x1, mod[l], expert_w_gate, expert_w_up, expert_w_down, l, seq)

    return _final_norm(x2, final_norm_g, seq).reshape(bsz, seq, D_MODEL)
```

```python
import functools
import math

import jax
import jax.numpy as jnp
from jax import lax
from jax.experimental import pallas as pl
from jax.experimental.pallas import tpu as pltpu
from jax.experimental.pallas import tpu_sc as plsc

F32 = jnp.float32
BF16 = jnp.bfloat16

D_MODEL = 1024
A_HEADS = 8
A_HEAD_DIM = 128
IDX_HEADS = 8
IDX_DIM = 64
TOPK_MAX = 256
REL_BUCKETS = 32
REL_MAX_DIST = 128
B_HEADS = 8
B_HEAD_DIM = 128
C_HEADS = 4
C_QK_DIM = 256
C_V_DIM = 512
N_GROUPS = 4
EXPERTS_PER_GROUP = 8
N_EXPERTS = 32
D_EXPERT = 512
EPS = 1e-6

A_WIDTH = A_HEADS * A_HEAD_DIM
B_WIDTH = B_HEADS * B_HEAD_DIM
C_QK_WIDTH = C_HEADS * C_QK_DIM
C_V_WIDTH = C_HEADS * C_V_DIM
IN_SPLITS = (A_WIDTH, A_WIDTH, A_WIDTH, IDX_HEADS * IDX_DIM, IDX_DIM, IDX_HEADS,
             B_WIDTH, B_WIDTH, B_WIDTH, B_WIDTH,
             C_QK_WIDTH, C_QK_WIDTH, C_V_WIDTH, C_V_WIDTH,
             D_MODEL, D_MODEL, D_MODEL)

LANES = 128
BF16_ROWS = 16
VMEM_LIMIT = 56 * 1024 * 1024

P16_CV, P16_CG = 0, 2
P16_AK, P16_BQ, P16_BI, P16_BG, P16_CQ, P16_CK, P16_GA, P16_GB, P16_GC = range(4, 13)
P32_BF = 0
P32_IK = 1024
P32_IW = 1152

DSA_TQ = 256
HGRN_L = 256
HGRN_C = 64
HGRN_SB = 16
HGRN_MAX_LOG_DECAY = 80.0
RET_C = 256
KEY_NEG_INF = -2139095041
HALF_BIAS = 32768
MASK_NEG = -1e30
LOG2_E = math.log2(math.e)
COUNT_CHAINS = 4
MOE_TM = 256
PLAN_TB = 1024
SC_WINDOW = 128
SC_SUB = 256
SC_PIECES = D_MODEL // 2 // SC_SUB

NT_DIMS = (((1,), (1,)), ((), ()))
TN_DIMS = (((0,), (0,)), ((), ()))


def _cparams(sem):
    return pltpu.CompilerParams(dimension_semantics=sem, vmem_limit_bytes=VMEM_LIMIT)


def _silu(x):
    return x * jax.nn.sigmoid(x)


def _pack_bf16_pairs(x):
    k = x.shape[1] // 2
    bits = pltpu.bitcast(x.astype(BF16).astype(F32), jnp.int32)
    return (bits[:, :k] & jnp.int32(-65536)) | lax.shift_right_logical(bits[:, k:], 16)


def _unpack_bf16_pairs(words):
    hi = pltpu.bitcast(words & jnp.int32(-65536), F32)
    lo = pltpu.bitcast(lax.shift_left(words, 16), F32)
    return jnp.concatenate([hi, lo], axis=1)


def _lb_kernel(raw_ref, o_ref):
    raw = raw_ref[...]
    m = jnp.max(raw, axis=0, keepdims=True)
    e = jnp.exp(raw - m)
    soft = e / jnp.sum(e, axis=0, keepdims=True)
    run = jnp.zeros_like(soft[0:1])
    for l in range(raw.shape[0]):
        run = run + soft[l:l + 1]
        o_ref[l:l + 1, :] = run - soft[0:1]


def _hgrn_lower_bounds(raw):
    return pl.pallas_call(
        _lb_kernel, out_shape=jax.ShapeDtypeStruct(raw.shape, F32), name="hgrn_lb")(raw)


def _ada_kernel(c_ref, w_ref, b_ref, o_ref):
    a = _silu(c_ref[...])
    o_ref[0] = jnp.dot(a, w_ref[0], precision=lax.Precision.HIGHEST,
                       preferred_element_type=F32) + b_ref[0]


def _ada_mod(c_pad, ada_w, ada_b):
    depth = ada_w.shape[0]
    rows = c_pad.shape[0]
    return pl.pallas_call(
        _ada_kernel,
        grid=(depth, 6),
        in_specs=[pl.BlockSpec((rows, D_MODEL), lambda l, j: (0, 0)),
                  pl.BlockSpec((1, D_MODEL, D_MODEL), lambda l, j: (l, 0, j)),
                  pl.BlockSpec((1, 1, D_MODEL), lambda l, j: (l, 0, j))],
        out_specs=pl.BlockSpec((1, rows, D_MODEL), lambda l, j: (l, 0, j)),
        out_shape=jax.ShapeDtypeStruct((depth, rows, 6 * D_MODEL), F32),
        compiler_params=_cparams(("arbitrary", "arbitrary")),
        name="ada_mod",
    )(c_pad, ada_w, ada_b.reshape(depth, 1, 6 * D_MODEL))


def _rms_mod(x, g, sc, sh):
    ms = jnp.mean(x * x, axis=-1, keepdims=True)
    return (x * lax.rsqrt(ms + EPS) * g) * (1.0 + sc) + sh


def _norm1(x_ref, mod_ref, g_ref, b):
    sh = mod_ref[pl.ds(b, 1), 0:D_MODEL]
    sc = mod_ref[pl.ds(b, 1), D_MODEL:2 * D_MODEL]
    return _rms_mod(x_ref[...], g_ref[...], sc, sh).astype(BF16)


def _proj_all_kernel(x_ref, mod_ref, g_ref, w16_ref, w32_ref, wvt_ref, wqt_ref,
                     o16_ref, o32_ref, ovt_ref, oqt_ref, h_ref, *, tiles_per_batch, n16, n32, chunk):
    j = pl.program_id(1)

    @pl.when(j == 0)
    def _():
        h_ref[...] = _norm1(x_ref, mod_ref, g_ref, pl.program_id(0) // tiles_per_batch)

    @pl.when(j < n16)
    def _():
        o16_ref[...] = jnp.dot(h_ref[...], w16_ref[...], preferred_element_type=F32).astype(o16_ref.dtype)

    @pl.when((j >= n16) & (j < n16 + n32))
    def _():
        o32_ref[...] = jnp.dot(h_ref[...], w32_ref[...], preferred_element_type=F32)

    def transposed(wt_ref, o_ref):
        res = lax.dot_general(wt_ref[...], h_ref[...], NT_DIMS, preferred_element_type=F32)
        for ci in range(o_ref.shape[0]):
            o_ref[ci] = res[:, ci * chunk:(ci + 1) * chunk].astype(o_ref.dtype)

    @pl.when(j == n16 + n32)
    def _():
        transposed(wvt_ref, ovt_ref)

    @pl.when(j == n16 + n32 + 1)
    def _():
        transposed(wqt_ref, oqt_ref)


def _project_all(x2, mod_l, g, w16_all, w32_all, wvt_all, wqt_all, layer, chunk, seq):
    n = x2.shape[0]
    tm = min(1024, seq)
    t16, t32 = 1024, 640
    n16, n32 = w16_all.shape[2] // t16, w32_all.shape[2] // t32
    vt_rows, qt_rows = wvt_all.shape[1], wqt_all.shape[1]
    one = pl.Buffered(1)

    def c16(j):
        return jnp.minimum(j, n16 - 1)

    def c32(j):
        return jnp.clip(j - n16, 0, n32 - 1)

    return pl.pallas_call(
        functools.partial(_proj_all_kernel, tiles_per_batch=seq // tm, n16=n16, n32=n32, chunk=chunk),
        grid=(n // tm, n16 + n32 + 2),
        in_specs=[pl.BlockSpec((tm, D_MODEL), lambda i, j: (i, 0)),
                  pl.BlockSpec(mod_l.shape, lambda i, j: (0, 0)),
                  pl.BlockSpec((1, D_MODEL), lambda i, j: (0, 0)),
                  pl.BlockSpec((None, D_MODEL, t16), lambda i, j: (layer, 0, c16(j))),
                  pl.BlockSpec((None, D_MODEL, t32), lambda i, j: (layer, 0, c32(j))),
                  pl.BlockSpec((None, vt_rows, D_MODEL), lambda i, j: (layer, 0, 0), pipeline_mode=one),
                  pl.BlockSpec((None, qt_rows, D_MODEL), lambda i, j: (layer, 0, 0), pipeline_mode=one)],
        out_specs=[pl.BlockSpec((tm, t16), lambda i, j: (i, c16(j))),
                   pl.BlockSpec((tm, t32), lambda i, j: (i, c32(j))),
                   pl.BlockSpec((tm // chunk, vt_rows, chunk), lambda i, j: (i, 0, 0)),
                   pl.BlockSpec((tm // chunk, qt_rows, chunk), lambda i, j: (i, 0, 0))],
        out_shape=[jax.ShapeDtypeStruct((n, w16_all.shape[2]), BF16),
                   jax.ShapeDtypeStruct((n, w32_all.shape[2]), F32),
                   jax.ShapeDtypeStruct((n // chunk, vt_rows, chunk), BF16),
                   jax.ShapeDtypeStruct((n // chunk, qt_rows, chunk), BF16)],
        scratch_shapes=[pltpu.VMEM((tm, D_MODEL), BF16)],
        compiler_params=_cparams(("arbitrary", "arbitrary")),
        name="proj_all",
    )(x2, mod_l, g.reshape(1, D_MODEL), w16_all, w32_all, wvt_all, wqt_all)


def _dsa_kernel(qiq_ref, iw_ref, k_ref, vt_ref, ik_ref, bias_ref, o_ref,
                key_ref, hi_ref, lo_ref, madd_ref, iwt_ref, m_ref, l_ref, acc_ref, s_ref, *, tq, topk):
    qi = pl.program_id(1)
    nck = qi + 1
    idx_scale = (IDX_HEADS * IDX_DIM) ** -0.5

    iwt_ref[...] = (iw_ref[...] * idx_scale).T

    krow = lax.broadcasted_iota(jnp.int32, (tq, tq), 0)
    qcol = lax.broadcasted_iota(jnp.int32, (tq, tq), 1)

    def score_chunk(c, carry):
        off = pl.multiple_of(c * tq, tq)
        ikc = ik_ref[pl.ds(off, tq), :].astype(BF16)[:, :IDX_DIM]
        acc = jnp.zeros((tq, tq), F32)
        for h in range(IDX_HEADS):
            s = jnp.dot(ikc, qiq_ref[0, A_WIDTH + h * IDX_DIM:A_WIDTH + (h + 1) * IDX_DIM, :],
                        preferred_element_type=F32)
            acc = acc + jnp.maximum(s, 0.0) * iwt_ref[h:h + 1, :]
        acc = jnp.where(acc == 0.0, 0.0, acc)
        acc = jnp.where(krow + (c - qi) * tq <= qcol, acc, -jnp.inf)
        kb = pltpu.bitcast(acc, jnp.int32)
        key = jnp.where(kb < 0, kb ^ jnp.int32(0x7FFFFFFF), kb)
        key_ref[c] = key
        hi_ref[c] = jnp.right_shift(key, 16).astype(jnp.int16)
        lo_ref[c] = ((key & 0xFFFF) - HALF_BIAS).astype(jnp.int16)
        return carry

    def paired_loop(count_, fn):
        def pair(i, carry):
            fn(2 * i, carry)
            fn(2 * i + 1, carry)
            return carry

        lax.fori_loop(0, count_ // 2, pair, 0)

        @pl.when(count_ % 2 == 1)
        def _():
            fn(count_ - 1, 0)

    paired_loop(nck, score_chunk)

    def count(ref, pred_fn, rows, zero, one):
        def body(c, parts):
            hit = jnp.where(pred_fn(ref[c]), one, zero)
            parts = list(parts)
            for r in range(tq // rows):
                parts[r % COUNT_CHAINS] = parts[r % COUNT_CHAINS] + hit[r * rows:(r + 1) * rows, :]
            return tuple(parts)

        parts = lax.fori_loop(0, nck, body, (jnp.full((rows, tq), zero),) * COUNT_CHAINS)
        return jnp.sum(sum(p.astype(F32) for p in parts), axis=0, keepdims=True)

    def count16(ref, pred_fn):
        return count(ref, pred_fn, 16, jnp.int16(0), jnp.int16(1))

    def count32(pred_fn):
        return count(key_ref, pred_fn, 8, jnp.float32(0.0), jnp.float32(1.0))

    def bisect16(ref, target):
        def bit_step(i, theta):
            cand = theta + jnp.left_shift(jnp.int32(1), 15 - i)
            cand16 = cand.astype(jnp.int16)
            return jnp.where(count16(ref, lambda k: k >= cand16) >= target, cand, theta)

        return lax.fori_loop(0, 16, bit_step, jnp.full((1, tq), -HALF_BIAS, jnp.int32))

    theta_hi = bisect16(hi_ref, float(topk))
    theta_hi16 = theta_hi.astype(jnp.int16)
    need_lo = topk - count16(hi_ref, lambda k: k > theta_hi16)

    def bucket_chunk(c, carry):
        lo_ref[c] = jnp.where(hi_ref[c] == theta_hi16, lo_ref[c], jnp.int16(-HALF_BIAS))
        return carry

    lax.fori_loop(0, nck, bucket_chunk, 0)
    theta_lo = bisect16(lo_ref, need_lo)
    theta = theta_hi * (2 * HALF_BIAS) + (theta_lo + HALF_BIAS)
    theta = jnp.maximum(theta, KEY_NEG_INF + 1)

    def mask_chunk(c, cnt):
        ge = key_ref[c] >= theta
        madd_ref[c] = jnp.where(ge, 0.0, MASK_NEG)
        return cnt + jnp.sum(jnp.where(ge, 1.0, 0.0), axis=0, keepdims=True)

    cnt_ge = lax.fori_loop(0, nck, mask_chunk, jnp.zeros((1, tq), F32))

    @pl.when(jnp.max(cnt_ge) > topk)
    def _():
        need_eq = topk - count32(lambda kc: kc > theta)
        incl = jnp.where(krow >= qcol, 1.0, 0.0).astype(BF16)

        def tie_chunk(c, run):
            kc = key_ref[c]
            eq = kc == theta
            eqf = jnp.where(eq, 1.0, 0.0)
            pref = jnp.dot(incl, eqf.astype(BF16), preferred_element_type=F32) + run
            eq_add = jnp.where(pref <= need_eq, 0.0, MASK_NEG)
            madd_ref[c] = jnp.where(eq, eq_add, jnp.where(kc > theta, 0.0, MASK_NEG))
            return run + jnp.sum(eqf, axis=0, keepdims=True)

        lax.fori_loop(0, nck, tie_chunk, jnp.zeros((1, tq), F32))

    m_ref[...] = jnp.full(m_ref.shape, -jnp.inf, F32)
    l_ref[...] = jnp.zeros(l_ref.shape, F32)
    acc_ref[...] = jnp.zeros(acc_ref.shape, F32)

    ones_rows = jnp.ones((BF16_ROWS, tq), BF16)

    head_slices = [slice(h * A_HEAD_DIM, (h + 1) * A_HEAD_DIM) for h in range(A_HEADS)]

    def logits(c, h):
        off = pl.multiple_of(c * tq, tq)
        s_ref[h] = jnp.dot(k_ref[pl.ds(off, tq), head_slices[h]], qiq_ref[0, head_slices[h], :],
                           preferred_element_type=F32)

    def attend(c, h, lag):
        hs = head_slices[h]
        s = s_ref[h] + madd_ref[c]
        if lag is not None:
            s = s + bias_ref[h, lag]
        m_old = m_ref[h]
        m_new = jnp.maximum(m_old, jnp.max(s, axis=0, keepdims=True))
        alpha = jnp.exp2(m_old - m_new)
        p = jnp.exp2(s - m_new).astype(BF16)
        pv = jnp.dot(jnp.concatenate([vt_ref[c, hs, :], ones_rows], axis=0), p,
                     preferred_element_type=F32)
        l_ref[h] = alpha * l_ref[h] + pv[A_HEAD_DIM:A_HEAD_DIM + 1]
        acc_ref[h] = alpha * acc_ref[h] + pv[:A_HEAD_DIM]
        m_ref[h] = m_new

    def step(c, lag, prefetch):
        for h in range(A_HEADS):
            attend(c, h, lag)
            if prefetch:
                logits(c + 1, h)

    for h in range(A_HEADS):
        logits(0, h)

    def far_chunk(c, carry):
        step(c, None, True)
        return carry

    paired_loop(jnp.maximum(qi - 1, 0), far_chunk)

    @pl.when(qi >= 1)
    def _():
        step(qi - 1, 1, True)

    step(qi, 0, False)

    for h in range(A_HEADS):
        o = acc_ref[h] * (1.0 / l_ref[h])
        o_ref[:, h * A_HEAD_DIM:(h + 1) * A_HEAD_DIM] = o.T.astype(o_ref.dtype)


def _dsa_attention(qiq_t, p16, p32, vt, bias_tiles, bsz, seq):
    tq = min(DSA_TQ, seq)
    nq = seq // tq
    topk = min(TOPK_MAX, seq // 4)
    n = bsz * seq
    one = pl.Buffered(1)
    return pl.pallas_call(
        functools.partial(_dsa_kernel, tq=tq, topk=topk),
        grid=(bsz, nq),
        in_specs=[
            pl.BlockSpec((1, qiq_t.shape[1], tq), lambda b, i: (b * nq + i, 0, 0)),
            pl.BlockSpec((tq, LANES), lambda b, i: (b * nq + i, P32_IW // LANES)),
            pl.BlockSpec((seq, A_WIDTH), lambda b, i: (b, P16_AK), pipeline_mode=one),
            pl.BlockSpec((nq, A_WIDTH, tq), lambda b, i: (b, 0, 0), pipeline_mode=one),
            pl.BlockSpec((seq, LANES), lambda b, i: (b, P32_IK // LANES), pipeline_mode=one),
            pl.BlockSpec(bias_tiles.shape, lambda b, i: (0, 0, 0, 0), pipeline_mode=one),
        ],
        out_specs=pl.BlockSpec((tq, A_WIDTH), lambda b, i: (b * nq + i, 0)),
        out_shape=jax.ShapeDtypeStruct((n, A_WIDTH), BF16),
        scratch_shapes=[pltpu.VMEM((nq, tq, tq), jnp.int32),
                        pltpu.VMEM((nq, tq, tq), jnp.int16),
                        pltpu.VMEM((nq, tq, tq), jnp.int16),
                        pltpu.VMEM((nq, tq, tq), F32),
                        pltpu.VMEM((LANES, tq), F32),
                        pltpu.VMEM((A_HEADS, 1, tq), F32),
                        pltpu.VMEM((A_HEADS, 1, tq), F32),
                        pltpu.VMEM((A_HEADS, A_HEAD_DIM, tq), F32),
                        pltpu.VMEM((A_HEADS, tq, tq), F32)],
        compiler_params=_cparams(("arbitrary", "arbitrary")),
        name="dsa_attention",
    )(qiq_t, p32, p16, vt, p32, bias_tiles)


def _t5_bucket(rel):
    max_exact = REL_BUCKETS // 2
    relf = jnp.maximum(rel, 1).astype(F32)
    large = max_exact + (jnp.log(relf / max_exact) / math.log(REL_MAX_DIST / max_exact)
                         * (REL_BUCKETS - max_exact)).astype(jnp.int32)
    large = jnp.minimum(large, REL_BUCKETS - 1)
    return jnp.where(rel < max_exact, rel, large)


def _bias_tiles(rel_bias, tq):
    assert tq >= REL_MAX_DIST
    key = jnp.arange(tq, dtype=jnp.int32)[:, None]
    qry = jnp.arange(tq, dtype=jnp.int32)[None, :]
    bucket = jnp.stack([_t5_bucket(jnp.maximum(lag * tq + qry - key, 0)) for lag in range(2)])
    rel = ((rel_bias - rel_bias[REL_BUCKETS - 1:REL_BUCKETS]) * LOG2_E).astype(F32)
    onehot = bucket[None] == jnp.arange(REL_BUCKETS, dtype=jnp.int32)[:, None, None, None]
    return jnp.sum(jnp.where(onehot[:, None], rel[:, :, None, None, None], 0.0), axis=0)


def _hgrn_kernel(q_ref, f_ref, i_ref, g_ref, lb_ref, ng_ref, tril_ref, o_ref, st_ref, attn_ref, stage_ref,
                 *, rows):
    @pl.when(pl.program_id(1) == 0)
    def _():
        st_ref[...] = jnp.zeros_like(st_ref)

    lb = lb_ref[...]
    f = lb + (1.0 - lb) * jax.nn.sigmoid(f_ref[...])
    logf = jnp.log(f)
    kk = 1.0 - f
    g1 = logf.astype(BF16)
    r1 = logf - g1.astype(F32)
    g2 = r1.astype(BF16)
    g3 = (r1 - g2.astype(F32)).astype(BF16)
    tril = tril_ref[...]
    bcum = (jnp.dot(tril, g1, preferred_element_type=F32)
            + jnp.dot(tril, g2, preferred_element_type=F32)
            + jnp.dot(tril, g3, preferred_element_type=F32))

    srow = lax.broadcasted_iota(jnp.int32, (HGRN_C, B_HEAD_DIM), 0)
    trow = lax.broadcasted_iota(jnp.int32, (HGRN_SB, HGRN_C), 0)
    scol = lax.broadcasted_iota(jnp.int32, (HGRN_SB, HGRN_C), 1)
    ng = ng_ref[...]
    q = q_ref[...].astype(F32)
    qb_all = (q * jnp.exp(bcum)).astype(BF16)

    tiles = [(n, h) for n in range(rows // HGRN_C) for h in range(B_HEADS)]

    def rs(n):
        return slice(n * HGRN_C, (n + 1) * HGRN_C)

    def hs(h):
        return slice(h * B_HEAD_DIM, (h + 1) * B_HEAD_DIM)

    worst = None
    for j in range(rows // HGRN_SB):
        r0 = j * HGRN_SB
        span = bcum[r0 + HGRN_SB - 1:r0 + HGRN_SB]
        if r0 % HGRN_C:
            span = span - bcum[r0 - 1:r0]
        worst = span if worst is None else jnp.minimum(worst, span)
    in_range = jnp.min(worst) > -HGRN_MAX_LOG_DECAY

    @pl.when(in_range)
    def _():
        a_parts = {}
        for n, h in tiles:
            bc, qc, kc = bcum[rs(n), hs(h)], q[rs(n), hs(h)], kk[rs(n), hs(h)]
            for sb in range(HGRN_C // HGRN_SB):
                s0 = sb * HGRN_SB
                beta = bc[s0 - 1:s0] if sb > 0 else jnp.zeros((1, B_HEAD_DIM), F32)
                qs = (qc[s0:s0 + HGRN_SB] * jnp.exp(bc[s0:s0 + HGRN_SB] - beta)).astype(BF16)
                live = s0 + HGRN_SB
                ks = (kc[:live] * jnp.exp(beta - bc[:live])).astype(BF16)
                if live < HGRN_C:
                    ks = jnp.concatenate([ks, jnp.zeros((HGRN_C - live, B_HEAD_DIM), BF16)], axis=0)
                a_parts[n, h, sb] = lax.dot_general(qs, ks, NT_DIMS, preferred_element_type=F32)
        for ti, (n, h) in enumerate(tiles):
            a_rows = [jnp.where(scol <= trow + sb * HGRN_SB, a_parts[n, h, sb], 0.0)
                      for sb in range(HGRN_C // HGRN_SB)]
            attn_ref[ti] = jnp.concatenate(a_rows, axis=0)

    @pl.when(jnp.logical_not(in_range))
    def _():
        for h in range(B_HEADS):
            stage_ref[0, h] = bcum[:, hs(h)]
            stage_ref[1, h] = q[:, hs(h)]
            stage_ref[2, h] = kk[:, hs(h)]
        t_idx = lax.broadcasted_iota(jnp.int32, (HGRN_C, HGRN_C), 0)
        s_idx = lax.broadcasted_iota(jnp.int32, (HGRN_C, HGRN_C), 1)

        def safe_tile(ti, carry):
            n, h = ti // B_HEADS, ti % B_HEADS
            r0 = pl.multiple_of(n * HGRN_C, HGRN_C)
            bc = stage_ref[0, h, pl.ds(r0, HGRN_C), :]
            qc = stage_ref[1, h, pl.ds(r0, HGRN_C), :]
            kc = stage_ref[2, h, pl.ds(r0, HGRN_C), :]
            acc = jnp.where(t_idx == s_idx, jnp.sum(qc * kc, axis=-1, keepdims=True), 0.0)
            block = HGRN_C
            while block >= 2:
                half = block // 2
                ref_row = (t_idx & -block) + (half - 1)
                bref = jnp.dot(jnp.where(s_idx == ref_row, 1.0, 0.0), bc, precision=lax.Precision.HIGHEST,
                               preferred_element_type=F32)
                second = (srow & (block - 1)) >= half
                qs = jnp.where(second, qc * jnp.exp(jnp.where(second, bc - bref, 0.0)), 0.0).astype(BF16)
                ks = jnp.where(second, 0.0, kc * jnp.exp(jnp.where(second, 0.0, bref - bc))).astype(BF16)
                a = lax.dot_general(qs, ks, NT_DIMS, preferred_element_type=F32)
                acc = acc + jnp.where((t_idx & -block) == (s_idx & -block), a, 0.0)
                block = half
            attn_ref[ti] = acc
            return carry

        lax.fori_loop(0, len(tiles), safe_tile, 0)

    intra, upd, dec = {}, {}, {}
    for ti, (n, h) in enumerate(tiles):
        attn = attn_ref[ti].astype(BF16)
        vc = i_ref[rs(n), hs(h)]
        intra[n, h] = jnp.dot(attn, vc, preferred_element_type=F32)
        bc = bcum[rs(n), hs(h)]
        blast = bc[HGRN_C - 1:HGRN_C]
        kdec = (kk[rs(n), hs(h)] * jnp.exp(blast - bc)).astype(BF16)
        upd[n, h] = lax.dot_general(vc, kdec, TN_DIMS, preferred_element_type=F32)
        dec[n, h] = jnp.exp(blast)
    for n, h in tiles:
        st = st_ref[h]
        o = intra[n, h] + lax.dot_general(qb_all[rs(n), hs(h)], st.astype(BF16), NT_DIMS,
                                          preferred_element_type=F32)
        st_ref[h] = st * dec[n, h] + upd[n, h]
        ms = jnp.mean(o * o, axis=-1, keepdims=True)
        on = o * lax.rsqrt(ms + EPS) * ng
        o_ref[rs(n), hs(h)] = (on * _silu(g_ref[rs(n), hs(h)].astype(F32))).astype(o_ref.dtype)


def _hgrn2(p16, p32, lb_l, norm_g, bsz, seq):
    rows = min(HGRN_L, seq)
    nj = seq // rows
    n = bsz * seq
    r = jnp.arange(rows, dtype=jnp.int32)
    tril = ((r[:, None] >= r[None, :]) & (r[:, None] // HGRN_C == r[None, :] // HGRN_C)).astype(BF16)

    def col(base):
        return lambda b, j: (b * nj + j, base)

    return pl.pallas_call(
        functools.partial(_hgrn_kernel, rows=rows),
        grid=(bsz, nj),
        in_specs=[
            pl.BlockSpec((rows, B_WIDTH), col(P16_BQ)),
            pl.BlockSpec((rows, B_WIDTH), col(P32_BF // B_WIDTH)),
            pl.BlockSpec((rows, B_WIDTH), col(P16_BI)),
            pl.BlockSpec((rows, B_WIDTH), col(P16_BG)),
            pl.BlockSpec((1, B_WIDTH), lambda b, j: (0, 0)),
            pl.BlockSpec((1, B_HEAD_DIM), lambda b, j: (0, 0)),
            pl.BlockSpec((rows, rows), lambda b, j: (0, 0)),
        ],
        out_specs=pl.BlockSpec((rows, B_WIDTH), lambda b, j: (b * nj + j, 0)),
        out_shape=jax.ShapeDtypeStruct((n, B_WIDTH), BF16),
        scratch_shapes=[pltpu.VMEM((B_HEADS, B_HEAD_DIM, B_HEAD_DIM), F32),
                        pltpu.VMEM((rows // HGRN_C * B_HEADS, HGRN_C, HGRN_C), F32),
                        pltpu.VMEM((3, B_HEADS, rows, B_HEAD_DIM), F32)],
        compiler_params=_cparams(("arbitrary", "arbitrary")),
        name="hgrn2",
    )(p16, p32, p16, p16, lb_l.reshape(1, B_WIDTH), norm_g.reshape(1, B_HEAD_DIM), tril)


def _ret_kernel(q_ref, k_ref, v_ref, g_ref, cos_ref, sin_ref, idec_ref, qdec_ref, kdec_ref, cdec_ref,
                o_ref, st_ref):
    @pl.when(pl.program_id(1) == 0)
    def _():
        st_ref[...] = jnp.zeros_like(st_ref)

    heads = range(C_HEADS)
    cos = jnp.concatenate([cos_ref[...]] * C_HEADS, axis=1)
    sin_signed = jnp.concatenate([sin_ref[...]] * C_HEADS, axis=1)
    even = lax.broadcasted_iota(jnp.int32, cos.shape, 1) % 2 == 0

    def rot(a):
        swapped = jnp.where(even, pltpu.roll(a, C_QK_WIDTH - 1, 1), pltpu.roll(a, 1, 1))
        return a * cos + swapped * sin_signed

    qr = rot(q_ref[...].astype(F32))
    kr = rot(k_ref[...].astype(F32))
    qk = [slice(h * C_QK_DIM, (h + 1) * C_QK_DIM) for h in heads]
    vs = [slice(h * C_V_DIM, (h + 1) * C_V_DIM) for h in heads]
    attn = [lax.dot_general(qr[:, qk[h]].astype(BF16), kr[:, qk[h]].astype(BF16), NT_DIMS,
                            preferred_element_type=F32) * idec_ref[h] for h in heads]
    inter = [jnp.dot((qr[:, qk[h]] * qdec_ref[h]).astype(BF16), st_ref[h].astype(BF16),
                     preferred_element_type=F32) for h in heads]
    intra = [jnp.dot(attn[h].astype(BF16), v_ref[:, vs[h]], preferred_element_type=F32) for h in heads]
    upd = [jnp.dot((kr[:, qk[h]] * kdec_ref[h]).T.astype(BF16), v_ref[:, vs[h]],
                   preferred_element_type=F32) for h in heads]
    for h in heads:
        st_ref[h] = cdec_ref[h, 0:1, :] * st_ref[h] + upd[h]
        o = intra[h] + inter[h]
        ms = jnp.mean(o * o, axis=-1, keepdims=True)
        o_ref[:, vs[h]] = (_silu(g_ref[:, vs[h]].astype(F32)) * (o * lax.rsqrt(ms + EPS))).astype(o_ref.dtype)


def _retention_tables(seq):
    pos = jnp.arange(seq, dtype=F32)
    theta = jnp.repeat(1.0 / (10000.0 ** jnp.linspace(0.0, 1.0, C_QK_DIM // 2)), 2)
    ang = pos[:, None] * theta[None, :]
    pair_sign = jnp.where(jnp.arange(C_QK_DIM) % 2 == 0, -1.0, 1.0)
    log_gamma = jnp.log(1.0 - 2.0 ** (-5.0 - jnp.arange(C_HEADS, dtype=F32)))
    idx = jnp.arange(RET_C, dtype=F32)
    causal = idx[:, None] >= idx[None, :]
    idec = jnp.exp(jnp.where(causal[None], (idx[:, None] - idx[None, :])[None] * log_gamma[:, None, None],
                             -jnp.inf))
    qdec = jnp.exp((idx + 1.0)[None, :] * log_gamma[:, None])[..., None]
    kdec = jnp.exp((RET_C - 1.0 - idx)[None, :] * log_gamma[:, None])[..., None]
    cdec = jnp.exp(RET_C * log_gamma)[:, None, None]
    return (jnp.cos(ang), jnp.sin(ang) * pair_sign[None, :], idec,
            jnp.broadcast_to(qdec, (C_HEADS, RET_C, C_QK_DIM)),
            jnp.broadcast_to(kdec, (C_HEADS, RET_C, C_QK_DIM)),
            jnp.broadcast_to(cdec, (C_HEADS, 8, C_V_DIM)))


def _retention(p16, tables, bsz, seq):
    cos, sin, idec, qdec, kdec, cdec = tables
    nj = seq // RET_C
    n = bsz * seq
    v_blk = C_V_WIDTH // 1024

    def whole(a):
        return pl.BlockSpec(a.shape, lambda b, j: (0,) * a.ndim)

    return pl.pallas_call(
        _ret_kernel,
        grid=(bsz, nj),
        in_specs=[
            pl.BlockSpec((RET_C, C_QK_WIDTH), lambda b, j: (b * nj + j, P16_CQ)),
            pl.BlockSpec((RET_C, C_QK_WIDTH), lambda b, j: (b * nj + j, P16_CK)),
            pl.BlockSpec((RET_C, C_V_WIDTH), lambda b, j: (b * nj + j, P16_CV // v_blk)),
            pl.BlockSpec((RET_C, C_V_WIDTH), lambda b, j: (b * nj + j, P16_CG // v_blk)),
            pl.BlockSpec((RET_C, C_QK_DIM), lambda b, j: (j, 0)),
            pl.BlockSpec((RET_C, C_QK_DIM), lambda b, j: (j, 0)),
            whole(idec), whole(qdec), whole(kdec), whole(cdec),
        ],
        out_specs=pl.BlockSpec((RET_C, C_V_WIDTH), lambda b, j: (b * nj + j, 0)),
        out_shape=jax.ShapeDtypeStruct((n, C_V_WIDTH), BF16),
        scratch_shapes=[pltpu.VMEM((C_HEADS, C_QK_DIM, C_V_DIM), F32)],
        compiler_params=_cparams(("arbitrary", "arbitrary")),
        name="retention",
    )(p16, p16, p16, p16, cos, sin, idec, qdec, kdec, cdec)


def _merge_kernel(oa_ref, ob_ref, oc_ref, ga_ref, gb_ref, gc_ref, x_ref, mod_ref, g2_ref,
                  wa_ref, wb_ref, wc_ref, wo_ref, wrh_ref, wrl_ref, br_ref,
                  x1_ref, h2_ref, route_ref, *, tiles_per_batch):
    b = pl.program_id(0) // tiles_per_batch

    def gated(o_ref, w_ref, g_ref):
        y = jnp.dot(o_ref[...], w_ref[...], preferred_element_type=F32)
        return jax.nn.sigmoid(g_ref[...].astype(F32)) * y

    merged = gated(oa_ref, wa_ref, ga_ref) + gated(ob_ref, wb_ref, gb_ref) + gated(oc_ref, wc_ref, gc_ref)
    y = jnp.dot(merged.astype(BF16), wo_ref[...], preferred_element_type=F32)
    gt1 = mod_ref[pl.ds(b, 1), 2 * D_MODEL:3 * D_MODEL]
    x1 = x_ref[...] + gt1 * y
    x1_ref[...] = x1
    sh2 = mod_ref[pl.ds(b, 1), 3 * D_MODEL:4 * D_MODEL]
    sc2 = mod_ref[pl.ds(b, 1), 4 * D_MODEL:5 * D_MODEL]
    h2 = _rms_mod(x1, g2_ref[...], sc2, sh2)
    h_hi = h2.astype(BF16)
    words = _pack_bf16_pairs(h2)
    for j in range(h2_ref.shape[0]):
        h2_ref[j] = words[:, j * SC_SUB:(j + 1) * SC_SUB]
    h_lo = (h2 - h_hi.astype(F32)).astype(BF16)
    logits = (jnp.dot(h_hi, wrh_ref[...], preferred_element_type=F32)
              + jnp.dot(h_lo, wrh_ref[...], preferred_element_type=F32)
              + jnp.dot(h_hi, wrl_ref[...], preferred_element_type=F32)) + br_ref[...]
    lane = lax.broadcasted_iota(jnp.int32, logits.shape, 1).astype(F32)
    neg_inf = -jnp.inf

    def first_argmax(vals):
        top = jnp.max(vals, axis=-1, keepdims=True)
        idx = jnp.min(jnp.where(vals == top, lane, float(LANES)), axis=-1, keepdims=True)
        return top, idx

    gl = jnp.where(lane < N_GROUPS, logits, neg_inf)
    gmax, gsel = first_argmax(gl)
    gprob = 1.0 / jnp.sum(jnp.exp(gl - gmax), axis=-1, keepdims=True)
    lo = N_GROUPS + EXPERTS_PER_GROUP * gsel
    el = jnp.where((lane >= lo) & (lane < lo + EXPERTS_PER_GROUP), logits, neg_inf)
    v1, i1 = first_argmax(el)
    el2 = jnp.where(lane == i1, neg_inf, el)
    v2, i2 = first_argmax(el2)
    e2 = jnp.exp(v2 - v1)
    den = 1.0 + e2
    route_ref[...] = jnp.where(lane == 0.0, i1 - N_GROUPS,
                               jnp.where(lane == 1.0, i2 - N_GROUPS,
                                         jnp.where(lane == 2.0, gprob / den,
                                                   jnp.where(lane == 3.0, gprob * (e2 / den), 0.0))))


def _merge(o_a, o_b, o_c, p16, x2, mod_l, g2, wa, wb, wc, wo, wr_hi, wr_lo, br, seq):
    n = x2.shape[0]
    tm = min(512, seq)
    one = pl.Buffered(1)

    def rows(width, cb=0):
        return pl.BlockSpec((tm, width), lambda i: (i, cb))

    def whole(a):
        return pl.BlockSpec(a.shape, lambda i: (0,) * a.ndim, pipeline_mode=one)

    return pl.pallas_call(
        functools.partial(_merge_kernel, tiles_per_batch=seq // tm),
        grid=(n // tm,),
        in_specs=[rows(A_WIDTH), rows(B_WIDTH), rows(C_V_WIDTH),
                  rows(D_MODEL, P16_GA), rows(D_MODEL, P16_GB), rows(D_MODEL, P16_GC),
                  rows(D_MODEL), whole(mod_l), pl.BlockSpec((1, D_MODEL), lambda i: (0, 0)),
                  whole(wa), whole(wb), whole(wc), whole(wo), whole(wr_hi), whole(wr_lo), whole(br)],
        out_specs=[rows(D_MODEL), pl.BlockSpec((SC_PIECES, tm, SC_SUB), lambda i: (0, i, 0)), rows(LANES)],
        out_shape=[jax.ShapeDtypeStruct((n, D_MODEL), F32),
                   jax.ShapeDtypeStruct((SC_PIECES, n, SC_SUB), jnp.int32),
                   jax.ShapeDtypeStruct((n, LANES), F32)],
        compiler_params=_cparams(("arbitrary",)),
        name="merge_route",
    )(o_a, o_b, o_c, p16, p16, p16, x2, mod_l, g2.reshape(1, D_MODEL), wa, wb, wc, wo, wr_hi, wr_lo, br)


def _lane_pick(vals, lane, idx):
    return jnp.sum(jnp.where(lane == idx, vals, 0.0), axis=-1, keepdims=True)


def _rank_kernel(route_ref, rk_ref, cnt_ref, run_ref):
    @pl.when(pl.program_id(0) == 0)
    def _():
        run_ref[...] = jnp.zeros_like(run_ref)

    route = route_ref[...]
    tb = route.shape[0]
    lane = lax.broadcasted_iota(jnp.int32, route.shape, 1).astype(F32)
    e1 = _lane_pick(route, lane, 0.0)
    e2 = _lane_pick(route, lane, 1.0)
    sel = jnp.where((lane == e1) | (lane == e2), 1.0, 0.0)
    r = lax.broadcasted_iota(jnp.int32, (tb, tb), 0)
    c = lax.broadcasted_iota(jnp.int32, (tb, tb), 1)
    before = jnp.where(c < r, 1.0, 0.0).astype(BF16)
    rank = jnp.dot(before, sel.astype(BF16), preferred_element_type=F32) + run_ref[0:1, :]
    rk_ref[...] = jnp.where(lane == 0.0, _lane_pick(rank, lane, e1),
                            jnp.where(lane == 1.0, _lane_pick(rank, lane, e2), 0.0))
    run_ref[...] = run_ref[...] + jnp.sum(sel, axis=0, keepdims=True)
    cnt_ref[...] = run_ref[...]


def _expert_ranks(route):
    n = route.shape[0]
    tb = min(PLAN_TB, n)
    return pl.pallas_call(
        _rank_kernel,
        grid=(n // tb,),
        in_specs=[pl.BlockSpec((tb, LANES), lambda i: (i, 0))],
        out_specs=[pl.BlockSpec((tb, LANES), lambda i: (i, 0)),
                   pl.BlockSpec((8, LANES), lambda i: (0, 0))],
        out_shape=[jax.ShapeDtypeStruct((n, LANES), F32), jax.ShapeDtypeStruct((8, LANES), F32)],
        scratch_shapes=[pltpu.VMEM((8, LANES), F32)],
        compiler_params=_cparams(("arbitrary",)),
        name="expert_ranks",
    )(route)


def _plan_kernel(cnt_ref, route_ref, rk_ref, pos_ref, tmap_ref):
    lane_i = lax.broadcasted_iota(jnp.int32, (8, LANES), 1)
    cnt = jnp.where(lane_i < N_EXPERTS, cnt_ref[...], 0.0)
    padded = jnp.floor((cnt + (MOE_TM - 1)) * (1.0 / MOE_TM)) * MOE_TM
    r = lax.broadcasted_iota(jnp.int32, (LANES, LANES), 0)
    c = lax.broadcasted_iota(jnp.int32, (LANES, LANES), 1)
    base = jnp.dot(padded, jnp.where(r < c, 1.0, 0.0), precision=lax.Precision.HIGHEST,
                   preferred_element_type=F32)

    route = route_ref[...]
    lane = lax.broadcasted_iota(jnp.int32, route.shape, 1).astype(F32)
    rk = rk_ref[...]
    base_row = base[0:1, :]
    pos1 = _lane_pick(base_row, lane, _lane_pick(route, lane, 0.0)) + _lane_pick(rk, lane, 0.0)
    pos2 = _lane_pick(base_row, lane, _lane_pick(route, lane, 1.0)) + _lane_pick(rk, lane, 1.0)
    pos_ref[...] = jnp.where(lane == 0.0, pos1, jnp.where(lane == 1.0, pos2, 0.0)).astype(jnp.int32)

    @pl.when(pl.program_id(0) == 0)
    def _():
        nt = tmap_ref.shape[0]
        tlane = lax.broadcasted_iota(jnp.int32, (nt, LANES), 1)
        start = (lax.broadcasted_iota(jnp.int32, (nt, LANES), 0) * MOE_TM).astype(F32)
        end_row = jnp.where(tlane < N_EXPERTS, base_row + padded[0:1, :], 3e38)
        expert = jnp.sum(jnp.where(end_row <= start, 1.0, 0.0), axis=-1, keepdims=True)
        expert_c = jnp.minimum(expert, N_EXPERTS - 1.0)
        tl = tlane.astype(F32)
        left = _lane_pick(cnt[0:1, :], tl, expert_c) - (start[:, 0:1] - _lane_pick(base_row, tl, expert_c))
        valid = jnp.where(expert < N_EXPERTS, jnp.clip(left, 0.0, float(MOE_TM)), 0.0)
        tmap_ref[...] = jnp.where(tlane == 0, expert_c, jnp.where(tlane == 1, valid, 0.0)).astype(jnp.int32)


def _expert_plan(cnt, route, rk, n_tiles):
    n = route.shape[0]
    tb = min(PLAN_TB, n)
    nt_pad = -(-n_tiles // 8) * 8
    return pl.pallas_call(
        _plan_kernel,
        grid=(n // tb,),
        in_specs=[pl.BlockSpec((8, LANES), lambda i: (0, 0)),
                  pl.BlockSpec((tb, LANES), lambda i: (i, 0)),
                  pl.BlockSpec((tb, LANES), lambda i: (i, 0))],
        out_specs=[pl.BlockSpec((tb, LANES), lambda i: (i, 0)),
                   pl.BlockSpec((nt_pad, LANES), lambda i: (0, 0))],
        out_shape=[jax.ShapeDtypeStruct((n, LANES), jnp.int32),
                   jax.ShapeDtypeStruct((nt_pad, LANES), jnp.int32)],
        compiler_params=_cparams(("arbitrary",)),
        name="expert_plan",
    )(cnt, route, rk)


def _sc_mesh():
    return plsc.VectorSubcoreMesh(core_axis_name="c", subcore_axis_name="s")


def _sc_scatter_rows(src, idx, out_rows):
    m = idx.shape[0]
    n_src_win = src.shape[0] // SC_WINDOW

    @functools.partial(pl.kernel, out_type=jax.ShapeDtypeStruct((out_rows, src.shape[1]), src.dtype),
                       mesh=_sc_mesh(), scratch_types=[])
    def scatter(x_hbm, i_hbm, o_hbm):
        def body(x_vmem, i_vmem):
            pltpu.sync_copy(x_vmem, o_hbm.at[i_vmem.at[0]])

        pltpu.emit_pipeline(
            body, grid=(m // SC_WINDOW,),
            in_specs=[pl.BlockSpec((SC_WINDOW, src.shape[1]), lambda i: (i % n_src_win, 0)),
                      pl.BlockSpec((1, SC_WINDOW), lambda i: (0, i))],
            out_specs=[], core_axis_name=("c", "s"),
            dimension_semantics=(pltpu.PARALLEL,))(x_hbm, i_hbm)

    return scatter(src, idx.reshape(1, m))


def _sc_gather_rows(table, idx):
    m = idx.shape[0]

    @functools.partial(pl.kernel, out_type=jax.ShapeDtypeStruct((m, table.shape[1]), table.dtype),
                       mesh=_sc_mesh(), scratch_types=[])
    def gather(x_hbm, i_hbm, o_hbm):
        def body(i_vmem, o_vmem):
            pltpu.sync_copy(x_hbm.at[i_vmem.at[0]], o_vmem)

        pltpu.emit_pipeline(
            body, grid=(m // SC_WINDOW,),
            in_specs=[pl.BlockSpec((1, SC_WINDOW), lambda i: (0, i))],
            out_specs=[pl.BlockSpec((SC_WINDOW, table.shape[1]), lambda i: (i, 0))],
            core_axis_name=("c", "s"),
            dimension_semantics=(pltpu.PARALLEL,))(i_hbm, o_hbm)

    return gather(table, idx.reshape(1, m))


def _piece_row_index(pos, rows):
    return (jnp.arange(SC_PIECES, dtype=jnp.int32)[:, None] * rows + pos[None, :]).reshape(-1)


def _grouped_kernel(te_ref, tv_ref, x_ref, wg_ref, wu_ref, wd_ref, o_ref):
    valid = tv_ref[pl.program_id(0)]

    @pl.when(valid > 0)
    def _():
        words = jnp.concatenate([x_ref[j] for j in range(SC_PIECES)], axis=1)
        row = lax.broadcasted_iota(jnp.int32, words.shape, 0)
        words = jnp.where(row < valid, words, 0)
        x = _unpack_bf16_pairs(words).astype(BF16)
        a = jnp.dot(x, wg_ref[0].astype(BF16), preferred_element_type=F32)
        u = jnp.dot(x, wu_ref[0].astype(BF16), preferred_element_type=F32)
        hm = (_silu(a) * u).astype(BF16)
        out = _pack_bf16_pairs(jnp.dot(hm, wd_ref[0].astype(BF16), preferred_element_type=F32))
        for j in range(SC_PIECES):
            o_ref[j] = out[:, j * SC_SUB:(j + 1) * SC_SUB]

    @pl.when(valid <= 0)
    def _():
        o_ref[...] = jnp.zeros_like(o_ref)


def _grouped_experts(tile_expert, tile_valid, xs, wg, wu, wd, layer):
    n_tiles = tile_expert.shape[0]
    rows_block = pl.BlockSpec((SC_PIECES, MOE_TM, SC_SUB), lambda i, te, tv: (0, i, 0))
    return pl.pallas_call(
        _grouped_kernel,
        grid_spec=pltpu.PrefetchScalarGridSpec(
            num_scalar_prefetch=2,
            grid=(n_tiles,),
            in_specs=[rows_block,
                      pl.BlockSpec((None, 1, D_MODEL, D_EXPERT), lambda i, te, tv: (layer, te[i], 0, 0)),
                      pl.BlockSpec((None, 1, D_MODEL, D_EXPERT), lambda i, te, tv: (layer, te[i], 0, 0)),
                      pl.BlockSpec((None, 1, D_EXPERT, D_MODEL), lambda i, te, tv: (layer, te[i], 0, 0))],
            out_specs=rows_block),
        out_shape=jax.ShapeDtypeStruct(xs.shape, jnp.int32),
        compiler_params=_cparams(("arbitrary",)),
        name="grouped_experts",
    )(tile_expert, tile_valid, xs, wg, wu, wd)


def _combine_kernel(x1_ref, y_ref, route_ref, mod_ref, o_ref, *, tiles_per_batch):
    b = pl.program_id(0) // tiles_per_batch
    gt2 = mod_ref[pl.ds(b, 1), 5 * D_MODEL:6 * D_MODEL]
    route = route_ref[...]
    lane = lax.broadcasted_iota(jnp.int32, route.shape, 1).astype(F32)
    w1 = _lane_pick(route, lane, 2.0)
    w2 = _lane_pick(route, lane, 3.0)
    y1 = _unpack_bf16_pairs(jnp.concatenate([y_ref[0, j] for j in range(SC_PIECES)], axis=1))
    y2 = _unpack_bf16_pairs(jnp.concatenate([y_ref[1, j] for j in range(SC_PIECES)], axis=1))
    o_ref[...] = x1_ref[...] + gt2 * (w1 * y1 + w2 * y2)


def _combine(x1, y2, route, mod_l, seq):
    n = x1.shape[0]
    tm = min(1024, seq)
    return pl.pallas_call(
        functools.partial(_combine_kernel, tiles_per_batch=seq // tm),
        grid=(n // tm,),
        in_specs=[pl.BlockSpec((tm, D_MODEL), lambda i: (i, 0)),
                  pl.BlockSpec((2, SC_PIECES, tm, SC_SUB), lambda i: (0, 0, i, 0)),
                  pl.BlockSpec((tm, LANES), lambda i: (i, 0)),
                  pl.BlockSpec(mod_l.shape, lambda i: (0, 0))],
        out_specs=pl.BlockSpec((tm, D_MODEL), lambda i: (i, 0)),
        out_shape=jax.ShapeDtypeStruct((n, D_MODEL), F32),
        compiler_params=_cparams(("arbitrary",)),
        name="moe_combine",
    )(x1, y2, route, mod_l)


def _moe(h2, route, x1, mod_l, wg, wu, wd, layer, seq):
    n = h2.shape[1]
    n_tiles = (2 * n) // MOE_TM + N_EXPERTS
    rows = n_tiles * MOE_TM
    rk, cnt = _expert_ranks(route)
    pos, tmap = _expert_plan(cnt, route, rk, n_tiles)
    idx = jnp.concatenate([_piece_row_index(pos[:, 0], rows), _piece_row_index(pos[:, 1], rows)])
    xs = _sc_scatter_rows(h2.reshape(SC_PIECES * n, SC_SUB), idx, SC_PIECES * rows)
    ys = _grouped_experts(tmap[:n_tiles, 0], tmap[:n_tiles, 1], xs.reshape(SC_PIECES, rows, SC_SUB),
                          wg, wu, wd, layer)
    y2 = _sc_gather_rows(ys.reshape(SC_PIECES * rows, SC_SUB), idx).reshape(2, SC_PIECES, n, SC_SUB)
    return _combine(x1, y2, route, mod_l, seq)


def _final_norm_kernel(x_ref, g_ref, o_ref):
    x = x_ref[...]
    ms = jnp.mean(x * x, axis=-1, keepdims=True)
    o_ref[...] = x * lax.rsqrt(ms + EPS) * g_ref[...]


def _final_norm(x2, g, seq):
    n = x2.shape[0]
    tm = min(1024, seq)
    return pl.pallas_call(
        _final_norm_kernel,
        grid=(n // tm,),
        in_specs=[pl.BlockSpec((tm, D_MODEL), lambda i: (i, 0)),
                  pl.BlockSpec((1, D_MODEL), lambda i: (0, 0))],
        out_specs=pl.BlockSpec((tm, D_MODEL), lambda i: (i, 0)),
        out_shape=jax.ShapeDtypeStruct((n, D_MODEL), F32),
        compiler_params=_cparams(("arbitrary",)),
        name="final_norm",
    )(x2, g.reshape(1, D_MODEL))


_IN_OFFS = [sum(IN_SPLITS[:i]) for i in range(len(IN_SPLITS) + 1)]
(_AQ, _AK, _AV, _IQ, _IK, _IW, _BQ, _BF, _BI, _BG, _CQ, _CK, _CV, _CG, _GA, _GB, _GC) = range(len(IN_SPLITS))


def _pack_kernel(w_ref, w16_ref, w32_ref, wvt_ref, wqt_ref):
    def cols(seg):
        return w_ref[0, :, _IN_OFFS[seg]:_IN_OFFS[seg + 1]]

    scale = {_AQ: A_HEAD_DIM ** -0.5 * LOG2_E, _CK: C_QK_DIM ** -0.5}
    at = 0
    for seg in (_CV, _CG, _AK, _BQ, _BI, _BG, _CQ, _CK, _GA, _GB, _GC):
        v = cols(seg)
        if seg in scale:
            v = v * scale[seg]
        w16_ref[0, :, at:at + v.shape[1]] = v.astype(BF16)
        at += v.shape[1]

    rows = w_ref.shape[1]
    w32_ref[0, :, P32_BF:P32_BF + B_WIDTH] = cols(_BF).astype(BF16)
    zeros = jnp.zeros((rows, LANES - IDX_DIM), F32)
    w32_ref[0, :, P32_IK:P32_IK + LANES] = jnp.concatenate([cols(_IK), zeros], axis=1).astype(BF16)
    w32_ref[0, :, P32_IW:P32_IW + LANES] = jnp.concatenate(
        [cols(_IW), jnp.zeros((rows, LANES - IDX_HEADS), F32)], axis=1).astype(BF16)
    wvt_ref[0] = cols(_AV).T.astype(BF16)
    wqt_ref[0, 0:A_WIDTH, :] = (cols(_AQ) * scale[_AQ]).T.astype(BF16)
    wqt_ref[0, A_WIDTH:, :] = cols(_IQ).T.astype(BF16)


def _pack_w_in(w_in):
    depth, d, width = w_in.shape
    rows = LANES
    w16_width = 13 * 1024
    qiq_rows = A_WIDTH + IDX_HEADS * IDX_DIM
    w32_width = P32_IW + LANES
    return pl.pallas_call(
        _pack_kernel,
        grid=(depth, d // rows),
        in_specs=[pl.BlockSpec((1, rows, width), lambda l, r: (l, r, 0))],
        out_specs=[pl.BlockSpec((1, rows, w16_width), lambda l, r: (l, r, 0)),
                   pl.BlockSpec((1, rows, w32_width), lambda l, r: (l, r, 0)),
                   pl.BlockSpec((1, A_WIDTH, rows), lambda l, r: (l, 0, r)),
                   pl.BlockSpec((1, qiq_rows, rows), lambda l, r: (l, 0, r))],
        out_shape=[jax.ShapeDtypeStruct((depth, d, w16_width), BF16),
                   jax.ShapeDtypeStruct((depth, d, w32_width), BF16),
                   jax.ShapeDtypeStruct((depth, A_WIDTH, d), BF16),
                   jax.ShapeDtypeStruct((depth, qiq_rows, d), BF16)],
        compiler_params=_cparams(("arbitrary", "arbitrary")),
        name="pack_w_in",
    )(w_in)


def _split_bf16(w):
    hi = w.astype(BF16)
    return hi, (w - hi.astype(F32)).astype(BF16)


def kernel(x, c, rel_bias, hgrn_lb_raw, norm1_g, norm2_g, ada_w, ada_b, w_in, hgrn_norm_g, w_branch_a,
           w_branch_b, w_branch_c, w_out, router_group_w, router_group_b, router_expert_w,
           router_expert_b, expert_w_gate, expert_w_up, expert_w_down, final_norm_g):
    bsz, seq, _ = x.shape
    depth = w_in.shape[0]
    n = bsz * seq
    x2 = x.reshape(n, D_MODEL)
    tq = min(DSA_TQ, seq)

    lb_all = _hgrn_lower_bounds(hgrn_lb_raw)
    c_pad = jnp.pad(c, ((0, (-bsz) % 8), (0, 0)))
    mod = _ada_mod(c_pad, ada_w, ada_b)
    bias_tiles = _bias_tiles(rel_bias, tq)
    ret_tables = _retention_tables(seq)
    w16_all, w32_all, wvt_all, wqt_all = _pack_w_in(w_in)

    for l in range(depth):
        p16, p32, vt, qiq_t = _project_all(x2, mod[l], norm1_g[l], w16_all, w32_all, wvt_all, wqt_all,
                                           l, tq, seq)
        o_a = _dsa_attention(qiq_t, p16, p32, vt, bias_tiles, bsz, seq)
        o_b = _hgrn2(p16, p32, lb_all[l], hgrn_norm_g[l], bsz, seq)
        o_c = _retention(p16, ret_tables, bsz, seq)
        wr = jnp.concatenate([router_group_w[l], router_expert_w[l],
                              jnp.zeros((D_MODEL, LANES - N_GROUPS - N_EXPERTS), F32)], axis=1)
        br = jnp.concatenate([router_group_b[l], router_expert_b[l],
                              jnp.zeros((LANES - N_GROUPS - N_EXPERTS,), F32)]).reshape(1, LANES)
        wr_hi, wr_lo = _split_bf16(wr)
        x1, h2, route = _merge(o_a, o_b, o_c, p16, x2, mod[l], norm2_g[l],
                               w_branch_a[l].astype(BF16), w_branch_b[l].astype(BF16),
                               w_branch_c[l].astype(BF16), w_out[l].astype(BF16),
                               wr_hi, wr_lo, br, seq)
        x2 = _moe(h2, route, x1, mod[l], expert_w_gate, expert_w_up, expert_w_down, l, seq)

    return _final_norm(x2, final_norm_g, seq).reshape(bsz, seq, D_MODEL)
```

```python
import functools
import math

import jax
import jax.numpy as jnp
from jax import lax
from jax.experimental import pallas as pl
from jax.experimental.pallas import tpu as pltpu
from jax.experimental.pallas import tpu_sc as plsc

F32 = jnp.float32
BF16 = jnp.bfloat16

D_MODEL = 1024
A_HEADS = 8
A_HEAD_DIM = 128
IDX_HEADS = 8
IDX_DIM = 64
TOPK_MAX = 256
REL_BUCKETS = 32
REL_MAX_DIST = 128
B_HEADS = 8
B_HEAD_DIM = 128
C_HEADS = 4
C_QK_DIM = 256
C_V_DIM = 512
N_GROUPS = 4
EXPERTS_PER_GROUP = 8
N_EXPERTS = 32
D_EXPERT = 512
EPS = 1e-6

A_WIDTH = A_HEADS * A_HEAD_DIM
B_WIDTH = B_HEADS * B_HEAD_DIM
C_QK_WIDTH = C_HEADS * C_QK_DIM
C_V_WIDTH = C_HEADS * C_V_DIM
IN_SPLITS = (A_WIDTH, A_WIDTH, A_WIDTH, IDX_HEADS * IDX_DIM, IDX_DIM, IDX_HEADS,
             B_WIDTH, B_WIDTH, B_WIDTH, B_WIDTH,
             C_QK_WIDTH, C_QK_WIDTH, C_V_WIDTH, C_V_WIDTH,
             D_MODEL, D_MODEL, D_MODEL)

LANES = 128
BF16_ROWS = 16
VMEM_LIMIT = 56 * 1024 * 1024

P16_CV, P16_CG = 0, 2
P16_AK, P16_BQ, P16_BI, P16_BG, P16_CQ, P16_CK, P16_GA, P16_GB, P16_GC = range(4, 13)
P32_BF = 0
P32_IK = 1024
P32_IW = 1152

DSA_TQ = 256
HGRN_L = 256
HGRN_C = 64
HGRN_SB = 16
HGRN_MAX_LOG_DECAY = 80.0
RET_C = 256
KEY_NEG_INF = -2139095041
HALF_BIAS = 32768
MASK_NEG = -1e30
LOG2_E = math.log2(math.e)
COUNT_CHAINS = 4
MOE_TM = 256
PLAN_TB = 1024
SC_WINDOW = 128
SC_SUB = 256
SC_PIECES = D_MODEL // 2 // SC_SUB

NT_DIMS = (((1,), (1,)), ((), ()))
TN_DIMS = (((0,), (0,)), ((), ()))


def _cparams(sem):
    return pltpu.CompilerParams(dimension_semantics=sem, vmem_limit_bytes=VMEM_LIMIT)


def _silu(x):
    return x * jax.nn.sigmoid(x)


def _pack_bf16_pairs(x):
    k = x.shape[1] // 2
    bits = pltpu.bitcast(x.astype(BF16).astype(F32), jnp.int32)
    return (bits[:, :k] & jnp.int32(-65536)) | lax.shift_right_logical(bits[:, k:], 16)


def _unpack_bf16_pairs(words):
    hi = pltpu.bitcast(words & jnp.int32(-65536), F32)
    lo = pltpu.bitcast(lax.shift_left(words, 16), F32)
    return jnp.concatenate([hi, lo], axis=1)


def _lb_kernel(raw_ref, o_ref):
    raw = raw_ref[...]
    m = jnp.max(raw, axis=0, keepdims=True)
    e = jnp.exp(raw - m)
    soft = e / jnp.sum(e, axis=0, keepdims=True)
    run = jnp.zeros_like(soft[0:1])
    for l in range(raw.shape[0]):
        run = run + soft[l:l + 1]
        o_ref[l:l + 1, :] = run - soft[0:1]


def _hgrn_lower_bounds(raw):
    return pl.pallas_call(
        _lb_kernel, out_shape=jax.ShapeDtypeStruct(raw.shape, F32), name="hgrn_lb")(raw)


def _ada_kernel(c_ref, w_ref, b_ref, o_ref):
    a = _silu(c_ref[...])
    o_ref[0] = jnp.dot(a, w_ref[0], precision=lax.Precision.HIGHEST,
                       preferred_element_type=F32) + b_ref[0]


def _ada_mod(c_pad, ada_w, ada_b):
    depth = ada_w.shape[0]
    rows = c_pad.shape[0]
    return pl.pallas_call(
        _ada_kernel,
        grid=(depth, 6),
        in_specs=[pl.BlockSpec((rows, D_MODEL), lambda l, j: (0, 0)),
                  pl.BlockSpec((1, D_MODEL, D_MODEL), lambda l, j: (l, 0, j)),
                  pl.BlockSpec((1, 1, D_MODEL), lambda l, j: (l, 0, j))],
        out_specs=pl.BlockSpec((1, rows, D_MODEL), lambda l, j: (l, 0, j)),
        out_shape=jax.ShapeDtypeStruct((depth, rows, 6 * D_MODEL), F32),
        compiler_params=_cparams(("arbitrary", "arbitrary")),
        name="ada_mod",
    )(c_pad, ada_w, ada_b.reshape(depth, 1, 6 * D_MODEL))


def _rms_mod(x, g, sc, sh):
    ms = jnp.mean(x * x, axis=-1, keepdims=True)
    return (x * lax.rsqrt(ms + EPS) * g) * (1.0 + sc) + sh


def _norm1(x_ref, mod_ref, g_ref, b):
    sh = mod_ref[pl.ds(b, 1), 0:D_MODEL]
    sc = mod_ref[pl.ds(b, 1), D_MODEL:2 * D_MODEL]
    return _rms_mod(x_ref[...], g_ref[...], sc, sh).astype(BF16)


def _proj_all_kernel(x_ref, mod_ref, g_ref, w16_ref, w32_ref, wvt_ref, wqt_ref,
                     o16_ref, o32_ref, ovt_ref, oqt_ref, h_ref, *, tiles_per_batch, n16, n32, chunk):
    j = pl.program_id(1)

    @pl.when(j == 0)
    def _():
        h_ref[...] = _norm1(x_ref, mod_ref, g_ref, pl.program_id(0) // tiles_per_batch)

    @pl.when(j < n16)
    def _():
        o16_ref[...] = jnp.dot(h_ref[...], w16_ref[...], preferred_element_type=F32).astype(o16_ref.dtype)

    @pl.when((j >= n16) & (j < n16 + n32))
    def _():
        o32_ref[...] = jnp.dot(h_ref[...], w32_ref[...], preferred_element_type=F32)

    def transposed(wt_ref, o_ref):
        res = lax.dot_general(wt_ref[...], h_ref[...], NT_DIMS, preferred_element_type=F32)
        for ci in range(o_ref.shape[0]):
            o_ref[ci] = res[:, ci * chunk:(ci + 1) * chunk].astype(o_ref.dtype)

    @pl.when(j == n16 + n32)
    def _():
        transposed(wvt_ref, ovt_ref)

    @pl.when(j == n16 + n32 + 1)
    def _():
        transposed(wqt_ref, oqt_ref)


def _project_all(x2, mod_l, g, w16_all, w32_all, wvt_all, wqt_all, layer, chunk, seq):
    n = x2.shape[0]
    tm = min(1024, seq)
    t16, t32 = 1024, 640
    n16, n32 = w16_all.shape[2] // t16, w32_all.shape[2] // t32
    vt_rows, qt_rows = wvt_all.shape[1], wqt_all.shape[1]
    one = pl.Buffered(1)

    def c16(j):
        return jnp.minimum(j, n16 - 1)

    def c32(j):
        return jnp.clip(j - n16, 0, n32 - 1)

    return pl.pallas_call(
        functools.partial(_proj_all_kernel, tiles_per_batch=seq // tm, n16=n16, n32=n32, chunk=chunk),
        grid=(n // tm, n16 + n32 + 2),
        in_specs=[pl.BlockSpec((tm, D_MODEL), lambda i, j: (i, 0)),
                  pl.BlockSpec(mod_l.shape, lambda i, j: (0, 0)),
                  pl.BlockSpec((1, D_MODEL), lambda i, j: (0, 0)),
                  pl.BlockSpec((None, D_MODEL, t16), lambda i, j: (layer, 0, c16(j))),
                  pl.BlockSpec((None, D_MODEL, t32), lambda i, j: (layer, 0, c32(j))),
                  pl.BlockSpec((None, vt_rows, D_MODEL), lambda i, j: (layer, 0, 0), pipeline_mode=one),
                  pl.BlockSpec((None, qt_rows, D_MODEL), lambda i, j: (layer, 0, 0), pipeline_mode=one)],
        out_specs=[pl.BlockSpec((tm, t16), lambda i, j: (i, c16(j))),
                   pl.BlockSpec((tm, t32), lambda i, j: (i, c32(j))),
                   pl.BlockSpec((tm // chunk, vt_rows, chunk), lambda i, j: (i, 0, 0)),
                   pl.BlockSpec((tm // chunk, qt_rows, chunk), lambda i, j: (i, 0, 0))],
        out_shape=[jax.ShapeDtypeStruct((n, w16_all.shape[2]), BF16),
                   jax.ShapeDtypeStruct((n, w32_all.shape[2]), F32),
                   jax.ShapeDtypeStruct((n // chunk, vt_rows, chunk), BF16),
                   jax.ShapeDtypeStruct((n // chunk, qt_rows, chunk), BF16)],
        scratch_shapes=[pltpu.VMEM((tm, D_MODEL), BF16)],
        compiler_params=_cparams(("arbitrary", "arbitrary")),
        name="proj_all",
    )(x2, mod_l, g.reshape(1, D_MODEL), w16_all, w32_all, wvt_all, wqt_all)


def _dsa_kernel(qiq_ref, iw_ref, k_ref, vt_ref, ik_ref, bias_ref, o_ref,
                key_ref, hi_ref, lo_ref, madd_ref, iwt_ref, m_ref, l_ref, acc_ref, s_ref, *, tq, topk):
    qi = pl.program_id(1)
    nck = qi + 1
    idx_scale = (IDX_HEADS * IDX_DIM) ** -0.5

    iwt_ref[...] = (iw_ref[...] * idx_scale).T

    krow = lax.broadcasted_iota(jnp.int32, (tq, tq), 0)
    qcol = lax.broadcasted_iota(jnp.int32, (tq, tq), 1)

    def score_chunk(c, carry):
        off = pl.multiple_of(c * tq, tq)
        ikc = ik_ref[pl.ds(off, tq), :].astype(BF16)[:, :IDX_DIM]
        acc = jnp.zeros((tq, tq), F32)
        for h in range(IDX_HEADS):
            s = jnp.dot(ikc, qiq_ref[0, A_WIDTH + h * IDX_DIM:A_WIDTH + (h + 1) * IDX_DIM, :],
                        preferred_element_type=F32)
            acc = acc + jnp.maximum(s, 0.0) * iwt_ref[h:h + 1, :]
        acc = jnp.where(acc == 0.0, 0.0, acc)
        acc = jnp.where(krow + (c - qi) * tq <= qcol, acc, -jnp.inf)
        kb = pltpu.bitcast(acc, jnp.int32)
        key = jnp.where(kb < 0, kb ^ jnp.int32(0x7FFFFFFF), kb)
        key_ref[c] = key
        hi_ref[c] = jnp.right_shift(key, 16).astype(jnp.int16)
        lo_ref[c] = ((key & 0xFFFF) - HALF_BIAS).astype(jnp.int16)
        return carry

    def paired_loop(count_, fn):
        def pair(i, carry):
            fn(2 * i, carry)
            fn(2 * i + 1, carry)
            return carry

        lax.fori_loop(0, count_ // 2, pair, 0)

        @pl.when(count_ % 2 == 1)
        def _():
            fn(count_ - 1, 0)

    paired_loop(nck, score_chunk)

    half_min = jnp.int16(-HALF_BIAS)

    @pl.when(nck % 2 == 1)
    def _():
        hi_ref[nck] = jnp.full((tq, tq), half_min)
        lo_ref[nck] = jnp.full((tq, tq), half_min)

    def add_hits(parts, hit, rows):
        parts = list(parts)
        for r in range(tq // rows):
            parts[r % COUNT_CHAINS] = parts[r % COUNT_CHAINS] + hit[r * rows:(r + 1) * rows, :]
        return tuple(parts)

    def total(parts):
        return jnp.sum(sum(p.astype(F32) for p in parts), axis=0, keepdims=True)

    def count16(ref, pred_fn, visit=None):
        def body(i, parts):
            for c in (2 * i, 2 * i + 1):
                chunk = ref[c]
                if visit is not None:
                    visit(c, chunk)
                parts = add_hits(parts, jnp.where(pred_fn(chunk), jnp.int16(1), jnp.int16(0)), BF16_ROWS)
            return parts

        return total(lax.fori_loop(0, (nck + 1) // 2, body,
                                   (jnp.zeros((BF16_ROWS, tq), jnp.int16),) * COUNT_CHAINS))

    def count32(pred_fn):
        def body(c, parts):
            return add_hits(parts, jnp.where(pred_fn(key_ref[c]), 1.0, 0.0), 8)

        return total(lax.fori_loop(0, nck, body, (jnp.zeros((8, tq), F32),) * COUNT_CHAINS))

    def bisect16(ref, target):
        def bit_step(i, theta):
            cand = theta + jnp.left_shift(jnp.int32(1), 15 - i)
            cand16 = cand.astype(jnp.int16)
            return jnp.where(count16(ref, lambda k: k >= cand16) >= target, cand, theta)

        return lax.fori_loop(0, 16, bit_step, jnp.full((1, tq), -HALF_BIAS, jnp.int32))

    theta_hi = bisect16(hi_ref, float(topk))
    theta_hi16 = theta_hi.astype(jnp.int16)

    def keep_bucket(c, hi_chunk):
        lo_ref[c] = jnp.where(hi_chunk == theta_hi16, lo_ref[c], half_min)

    need_lo = topk - count16(hi_ref, lambda k: k > theta_hi16, visit=keep_bucket)
    theta_lo = bisect16(lo_ref, need_lo)
    theta = theta_hi * (2 * HALF_BIAS) + (theta_lo + HALF_BIAS)
    theta = jnp.maximum(theta, KEY_NEG_INF + 1)

    def mask_chunk(c, cnt):
        ge = key_ref[c] >= theta
        madd_ref[c] = jnp.where(ge, 0.0, MASK_NEG)
        return cnt + jnp.sum(jnp.where(ge, 1.0, 0.0), axis=0, keepdims=True)

    cnt_ge = lax.fori_loop(0, nck, mask_chunk, jnp.zeros((1, tq), F32))

    @pl.when(jnp.max(cnt_ge) > topk)
    def _():
        need_eq = topk - count32(lambda kc: kc > theta)
        incl = jnp.where(krow >= qcol, 1.0, 0.0).astype(BF16)

        def tie_chunk(c, run):
            kc = key_ref[c]
            eq = kc == theta
            eqf = jnp.where(eq, 1.0, 0.0)
            pref = jnp.dot(incl, eqf.astype(BF16), preferred_element_type=F32) + run
            eq_add = jnp.where(pref <= need_eq, 0.0, MASK_NEG)
            madd_ref[c] = jnp.where(eq, eq_add, jnp.where(kc > theta, 0.0, MASK_NEG))
            return run + jnp.sum(eqf, axis=0, keepdims=True)

        lax.fori_loop(0, nck, tie_chunk, jnp.zeros((1, tq), F32))

    m_ref[...] = jnp.full(m_ref.shape, -jnp.inf, F32)
    l_ref[...] = jnp.zeros(l_ref.shape, F32)
    acc_ref[...] = jnp.zeros(acc_ref.shape, F32)

    ones_rows = jnp.ones((BF16_ROWS, tq), BF16)

    head_slices = [slice(h * A_HEAD_DIM, (h + 1) * A_HEAD_DIM) for h in range(A_HEADS)]

    def logits(c, h):
        off = pl.multiple_of(c * tq, tq)
        s_ref[h] = jnp.dot(k_ref[pl.ds(off, tq), head_slices[h]], qiq_ref[0, head_slices[h], :],
                           preferred_element_type=F32)

    def attend(c, h, lag):
        hs = head_slices[h]
        s = s_ref[h] + madd_ref[c]
        if lag is not None:
            s = s + bias_ref[h, lag]
        m_old = m_ref[h]
        m_new = jnp.maximum(m_old, jnp.max(s, axis=0, keepdims=True))
        alpha = jnp.exp2(m_old - m_new)
        p = jnp.exp2(s - m_new).astype(BF16)
        pv = jnp.dot(jnp.concatenate([vt_ref[c, hs, :], ones_rows], axis=0), p,
                     preferred_element_type=F32)
        l_ref[h] = alpha * l_ref[h] + pv[A_HEAD_DIM:A_HEAD_DIM + 1]
        acc_ref[h] = alpha * acc_ref[h] + pv[:A_HEAD_DIM]
        m_ref[h] = m_new

    def step(c, lag, prefetch):
        for h in range(A_HEADS):
            attend(c, h, lag)
            if prefetch:
                logits(c + 1, h)

    for h in range(A_HEADS):
        logits(0, h)

    def far_chunk(c, carry):
        step(c, None, True)
        return carry

    paired_loop(jnp.maximum(qi - 1, 0), far_chunk)

    @pl.when(qi >= 1)
    def _():
        step(qi - 1, 1, True)

    step(qi, 0, False)

    for h in range(A_HEADS):
        o = acc_ref[h] * (1.0 / l_ref[h])
        o_ref[:, h * A_HEAD_DIM:(h + 1) * A_HEAD_DIM] = o.T.astype(o_ref.dtype)


def _dsa_attention(qiq_t, p16, p32, vt, bias_tiles, bsz, seq):
    tq = min(DSA_TQ, seq)
    nq = seq // tq
    topk = min(TOPK_MAX, seq // 4)
    n = bsz * seq
    one = pl.Buffered(1)
    return pl.pallas_call(
        functools.partial(_dsa_kernel, tq=tq, topk=topk),
        grid=(bsz, nq),
        in_specs=[
            pl.BlockSpec((1, qiq_t.shape[1], tq), lambda b, i: (b * nq + i, 0, 0)),
            pl.BlockSpec((tq, LANES), lambda b, i: (b * nq + i, P32_IW // LANES)),
            pl.BlockSpec((seq, A_WIDTH), lambda b, i: (b, P16_AK), pipeline_mode=one),
            pl.BlockSpec((nq, A_WIDTH, tq), lambda b, i: (b, 0, 0), pipeline_mode=one),
            pl.BlockSpec((seq, LANES), lambda b, i: (b, P32_IK // LANES), pipeline_mode=one),
            pl.BlockSpec(bias_tiles.shape, lambda b, i: (0, 0, 0, 0), pipeline_mode=one),
        ],
        out_specs=pl.BlockSpec((tq, A_WIDTH), lambda b, i: (b * nq + i, 0)),
        out_shape=jax.ShapeDtypeStruct((n, A_WIDTH), BF16),
        scratch_shapes=[pltpu.VMEM((nq, tq, tq), jnp.int32),
                        pltpu.VMEM((nq + 1, tq, tq), jnp.int16),
                        pltpu.VMEM((nq + 1, tq, tq), jnp.int16),
                        pltpu.VMEM((nq, tq, tq), F32),
                        pltpu.VMEM((LANES, tq), F32),
                        pltpu.VMEM((A_HEADS, 1, tq), F32),
                        pltpu.VMEM((A_HEADS, 1, tq), F32),
                        pltpu.VMEM((A_HEADS, A_HEAD_DIM, tq), F32),
                        pltpu.VMEM((A_HEADS, tq, tq), F32)],
        compiler_params=_cparams(("arbitrary", "arbitrary")),
        name="dsa_attention",
    )(qiq_t, p32, p16, vt, p32, bias_tiles)


def _t5_bucket(rel):
    max_exact = REL_BUCKETS // 2
    relf = jnp.maximum(rel, 1).astype(F32)
    large = max_exact + (jnp.log(relf / max_exact) / math.log(REL_MAX_DIST / max_exact)
                         * (REL_BUCKETS - max_exact)).astype(jnp.int32)
    large = jnp.minimum(large, REL_BUCKETS - 1)
    return jnp.where(rel < max_exact, rel, large)


def _bias_tiles(rel_bias, tq):
    assert tq >= REL_MAX_DIST
    key = jnp.arange(tq, dtype=jnp.int32)[:, None]
    qry = jnp.arange(tq, dtype=jnp.int32)[None, :]
    bucket = jnp.stack([_t5_bucket(jnp.maximum(lag * tq + qry - key, 0)) for lag in range(2)])
    rel = ((rel_bias - rel_bias[REL_BUCKETS - 1:REL_BUCKETS]) * LOG2_E).astype(F32)
    onehot = bucket[None] == jnp.arange(REL_BUCKETS, dtype=jnp.int32)[:, None, None, None]
    return jnp.sum(jnp.where(onehot[:, None], rel[:, :, None, None, None], 0.0), axis=0)


def _hgrn_kernel(q_ref, f_ref, i_ref, g_ref, lb_ref, ng_ref, tril_ref, o_ref, st_ref, attn_ref, stage_ref,
                 *, rows):
    @pl.when(pl.program_id(1) == 0)
    def _():
        st_ref[...] = jnp.zeros_like(st_ref)

    lb = lb_ref[...]
    f = lb + (1.0 - lb) * jax.nn.sigmoid(f_ref[...])
    logf = jnp.log(f)
    kk = 1.0 - f
    g1 = logf.astype(BF16)
    r1 = logf - g1.astype(F32)
    g2 = r1.astype(BF16)
    g3 = (r1 - g2.astype(F32)).astype(BF16)
    tril = tril_ref[...]
    bcum = (jnp.dot(tril, g1, preferred_element_type=F32)
            + jnp.dot(tril, g2, preferred_element_type=F32)
            + jnp.dot(tril, g3, preferred_element_type=F32))

    srow = lax.broadcasted_iota(jnp.int32, (HGRN_C, B_HEAD_DIM), 0)
    trow = lax.broadcasted_iota(jnp.int32, (HGRN_SB, HGRN_C), 0)
    scol = lax.broadcasted_iota(jnp.int32, (HGRN_SB, HGRN_C), 1)
    ng = ng_ref[...]
    q = q_ref[...].astype(F32)
    qb_all = (q * jnp.exp(bcum)).astype(BF16)

    tiles = [(n, h) for n in range(rows // HGRN_C) for h in range(B_HEADS)]

    def rs(n):
        return slice(n * HGRN_C, (n + 1) * HGRN_C)

    def hs(h):
        return slice(h * B_HEAD_DIM, (h + 1) * B_HEAD_DIM)

    worst = None
    for j in range(rows // HGRN_SB):
        r0 = j * HGRN_SB
        span = bcum[r0 + HGRN_SB - 1:r0 + HGRN_SB]
        if r0 % HGRN_C:
            span = span - bcum[r0 - 1:r0]
        worst = span if worst is None else jnp.minimum(worst, span)
    in_range = jnp.min(worst) > -HGRN_MAX_LOG_DECAY

    @pl.when(in_range)
    def _():
        a_parts = {}
        for n, h in tiles:
            bc, qc, kc = bcum[rs(n), hs(h)], q[rs(n), hs(h)], kk[rs(n), hs(h)]
            for sb in range(HGRN_C // HGRN_SB):
                s0 = sb * HGRN_SB
                beta = bc[s0 - 1:s0] if sb > 0 else jnp.zeros((1, B_HEAD_DIM), F32)
                qs = (qc[s0:s0 + HGRN_SB] * jnp.exp(bc[s0:s0 + HGRN_SB] - beta)).astype(BF16)
                live = s0 + HGRN_SB
                ks = (kc[:live] * jnp.exp(beta - bc[:live])).astype(BF16)
                if live < HGRN_C:
                    ks = jnp.concatenate([ks, jnp.zeros((HGRN_C - live, B_HEAD_DIM), BF16)], axis=0)
                a_parts[n, h, sb] = lax.dot_general(qs, ks, NT_DIMS, preferred_element_type=F32)
        for ti, (n, h) in enumerate(tiles):
            a_rows = [jnp.where(scol <= trow + sb * HGRN_SB, a_parts[n, h, sb], 0.0)
                      for sb in range(HGRN_C // HGRN_SB)]
            attn_ref[ti] = jnp.concatenate(a_rows, axis=0)

    @pl.when(jnp.logical_not(in_range))
    def _():
        for h in range(B_HEADS):
            stage_ref[0, h] = bcum[:, hs(h)]
            stage_ref[1, h] = q[:, hs(h)]
            stage_ref[2, h] = kk[:, hs(h)]
        t_idx = lax.broadcasted_iota(jnp.int32, (HGRN_C, HGRN_C), 0)
        s_idx = lax.broadcasted_iota(jnp.int32, (HGRN_C, HGRN_C), 1)

        def safe_tile(ti, carry):
            n, h = ti // B_HEADS, ti % B_HEADS
            r0 = pl.multiple_of(n * HGRN_C, HGRN_C)
            bc = stage_ref[0, h, pl.ds(r0, HGRN_C), :]
            qc = stage_ref[1, h, pl.ds(r0, HGRN_C), :]
            kc = stage_ref[2, h, pl.ds(r0, HGRN_C), :]
            acc = jnp.where(t_idx == s_idx, jnp.sum(qc * kc, axis=-1, keepdims=True), 0.0)
            block = HGRN_C
            while block >= 2:
                half = block // 2
                ref_row = (t_idx & -block) + (half - 1)
                bref = jnp.dot(jnp.where(s_idx == ref_row, 1.0, 0.0), bc, precision=lax.Precision.HIGHEST,
                               preferred_element_type=F32)
                second = (srow & (block - 1)) >= half
                qs = jnp.where(second, qc * jnp.exp(jnp.where(second, bc - bref, 0.0)), 0.0).astype(BF16)
                ks = jnp.where(second, 0.0, kc * jnp.exp(jnp.where(second, 0.0, bref - bc))).astype(BF16)
                a = lax.dot_general(qs, ks, NT_DIMS, preferred_element_type=F32)
                acc = acc + jnp.where((t_idx & -block) == (s_idx & -block), a, 0.0)
                block = half
            attn_ref[ti] = acc
            return carry

        lax.fori_loop(0, len(tiles), safe_tile, 0)

    intra, upd, dec = {}, {}, {}
    for ti, (n, h) in enumerate(tiles):
        attn = attn_ref[ti].astype(BF16)
        vc = i_ref[rs(n), hs(h)]
        intra[n, h] = jnp.dot(attn, vc, preferred_element_type=F32)
        bc = bcum[rs(n), hs(h)]
        blast = bc[HGRN_C - 1:HGRN_C]
        kdec = (kk[rs(n), hs(h)] * jnp.exp(blast - bc)).astype(BF16)
        upd[n, h] = lax.dot_general(vc, kdec, TN_DIMS, preferred_element_type=F32)
        dec[n, h] = jnp.exp(blast)
    for n, h in tiles:
        st = st_ref[h]
        o = intra[n, h] + lax.dot_general(qb_all[rs(n), hs(h)], st.astype(BF16), NT_DIMS,
                                          preferred_element_type=F32)
        st_ref[h] = st * dec[n, h] + upd[n, h]
        ms = jnp.mean(o * o, axis=-1, keepdims=True)
        on = o * lax.rsqrt(ms + EPS) * ng
        o_ref[rs(n), hs(h)] = (on * _silu(g_ref[rs(n), hs(h)].astype(F32))).astype(o_ref.dtype)


def _hgrn2(p16, p32, lb_l, norm_g, bsz, seq):
    rows = min(HGRN_L, seq)
    nj = seq // rows
    n = bsz * seq
    r = jnp.arange(rows, dtype=jnp.int32)
    tril = ((r[:, None] >= r[None, :]) & (r[:, None] // HGRN_C == r[None, :] // HGRN_C)).astype(BF16)

    def col(base):
        return lambda b, j: (b * nj + j, base)

    return pl.pallas_call(
        functools.partial(_hgrn_kernel, rows=rows),
        grid=(bsz, nj),
        in_specs=[
            pl.BlockSpec((rows, B_WIDTH), col(P16_BQ)),
            pl.BlockSpec((rows, B_WIDTH), col(P32_BF // B_WIDTH)),
            pl.BlockSpec((rows, B_WIDTH), col(P16_BI)),
            pl.BlockSpec((rows, B_WIDTH), col(P16_BG)),
            pl.BlockSpec((1, B_WIDTH), lambda b, j: (0, 0)),
            pl.BlockSpec((1, B_HEAD_DIM), lambda b, j: (0, 0)),
            pl.BlockSpec((rows, rows), lambda b, j: (0, 0)),
        ],
        out_specs=pl.BlockSpec((rows, B_WIDTH), lambda b, j: (b * nj + j, 0)),
        out_shape=jax.ShapeDtypeStruct((n, B_WIDTH), BF16),
        scratch_shapes=[pltpu.VMEM((B_HEADS, B_HEAD_DIM, B_HEAD_DIM), F32),
                        pltpu.VMEM((rows // HGRN_C * B_HEADS, HGRN_C, HGRN_C), F32),
                        pltpu.VMEM((3, B_HEADS, rows, B_HEAD_DIM), F32)],
        compiler_params=_cparams(("arbitrary", "arbitrary")),
        name="hgrn2",
    )(p16, p32, p16, p16, lb_l.reshape(1, B_WIDTH), norm_g.reshape(1, B_HEAD_DIM), tril)


def _ret_kernel(q_ref, k_ref, v_ref, g_ref, cos_ref, sin_ref, idec_ref, qdec_ref, kdec_ref, cdec_ref,
                o_ref, st_ref):
    @pl.when(pl.program_id(1) == 0)
    def _():
        st_ref[...] = jnp.zeros_like(st_ref)

    heads = range(C_HEADS)
    cos = jnp.concatenate([cos_ref[...]] * C_HEADS, axis=1)
    sin_signed = jnp.concatenate([sin_ref[...]] * C_HEADS, axis=1)
    even = lax.broadcasted_iota(jnp.int32, cos.shape, 1) % 2 == 0

    def rot(a):
        swapped = jnp.where(even, pltpu.roll(a, C_QK_WIDTH - 1, 1), pltpu.roll(a, 1, 1))
        return a * cos + swapped * sin_signed

    qr = rot(q_ref[...].astype(F32))
    kr = rot(k_ref[...].astype(F32))
    qk = [slice(h * C_QK_DIM, (h + 1) * C_QK_DIM) for h in heads]
    vs = [slice(h * C_V_DIM, (h + 1) * C_V_DIM) for h in heads]
    attn = [lax.dot_general(qr[:, qk[h]].astype(BF16), kr[:, qk[h]].astype(BF16), NT_DIMS,
                            preferred_element_type=F32) * idec_ref[h] for h in heads]
    inter = [jnp.dot((qr[:, qk[h]] * qdec_ref[h]).astype(BF16), st_ref[h].astype(BF16),
                     preferred_element_type=F32) for h in heads]
    intra = [jnp.dot(attn[h].astype(BF16), v_ref[:, vs[h]], preferred_element_type=F32) for h in heads]
    upd = [jnp.dot((kr[:, qk[h]] * kdec_ref[h]).T.astype(BF16), v_ref[:, vs[h]],
                   preferred_element_type=F32) for h in heads]
    for h in heads:
        st_ref[h] = cdec_ref[h, 0:1, :] * st_ref[h] + upd[h]
        o = intra[h] + inter[h]
        ms = jnp.mean(o * o, axis=-1, keepdims=True)
        o_ref[:, vs[h]] = (_silu(g_ref[:, vs[h]].astype(F32)) * (o * lax.rsqrt(ms + EPS))).astype(o_ref.dtype)


def _retention_tables(seq):
    pos = jnp.arange(seq, dtype=F32)
    theta = jnp.repeat(1.0 / (10000.0 ** jnp.linspace(0.0, 1.0, C_QK_DIM // 2)), 2)
    ang = pos[:, None] * theta[None, :]
    pair_sign = jnp.where(jnp.arange(C_QK_DIM) % 2 == 0, -1.0, 1.0)
    log_gamma = jnp.log(1.0 - 2.0 ** (-5.0 - jnp.arange(C_HEADS, dtype=F32)))
    idx = jnp.arange(RET_C, dtype=F32)
    causal = idx[:, None] >= idx[None, :]
    idec = jnp.exp(jnp.where(causal[None], (idx[:, None] - idx[None, :])[None] * log_gamma[:, None, None],
                             -jnp.inf))
    qdec = jnp.exp((idx + 1.0)[None, :] * log_gamma[:, None])[..., None]
    kdec = jnp.exp((RET_C - 1.0 - idx)[None, :] * log_gamma[:, None])[..., None]
    cdec = jnp.exp(RET_C * log_gamma)[:, None, None]
    return (jnp.cos(ang), jnp.sin(ang) * pair_sign[None, :], idec,
            jnp.broadcast_to(qdec, (C_HEADS, RET_C, C_QK_DIM)),
            jnp.broadcast_to(kdec, (C_HEADS, RET_C, C_QK_DIM)),
            jnp.broadcast_to(cdec, (C_HEADS, 8, C_V_DIM)))


def _retention(p16, tables, bsz, seq):
    cos, sin, idec, qdec, kdec, cdec = tables
    nj = seq // RET_C
    n = bsz * seq
    v_blk = C_V_WIDTH // 1024

    def whole(a):
        return pl.BlockSpec(a.shape, lambda b, j: (0,) * a.ndim)

    return pl.pallas_call(
        _ret_kernel,
        grid=(bsz, nj),
        in_specs=[
            pl.BlockSpec((RET_C, C_QK_WIDTH), lambda b, j: (b * nj + j, P16_CQ)),
            pl.BlockSpec((RET_C, C_QK_WIDTH), lambda b, j: (b * nj + j, P16_CK)),
            pl.BlockSpec((RET_C, C_V_WIDTH), lambda b, j: (b * nj + j, P16_CV // v_blk)),
            pl.BlockSpec((RET_C, C_V_WIDTH), lambda b, j: (b * nj + j, P16_CG // v_blk)),
            pl.BlockSpec((RET_C, C_QK_DIM), lambda b, j: (j, 0)),
            pl.BlockSpec((RET_C, C_QK_DIM), lambda b, j: (j, 0)),
            whole(idec), whole(qdec), whole(kdec), whole(cdec),
        ],
        out_specs=pl.BlockSpec((RET_C, C_V_WIDTH), lambda b, j: (b * nj + j, 0)),
        out_shape=jax.ShapeDtypeStruct((n, C_V_WIDTH), BF16),
        scratch_shapes=[pltpu.VMEM((C_HEADS, C_QK_DIM, C_V_DIM), F32)],
        compiler_params=_cparams(("arbitrary", "arbitrary")),
        name="retention",
    )(p16, p16, p16, p16, cos, sin, idec, qdec, kdec, cdec)


def _merge_kernel(oa_ref, ob_ref, oc_ref, ga_ref, gb_ref, gc_ref, x_ref, mod_ref, g2_ref,
                  wa_ref, wb_ref, wc_ref, wo_ref, wrh_ref, wrl_ref, br_ref,
                  x1_ref, h2_ref, route_ref, *, tiles_per_batch):
    b = pl.program_id(0) // tiles_per_batch

    def gated(o_ref, w_ref, g_ref):
        y = jnp.dot(o_ref[...], w_ref[...], preferred_element_type=F32)
        return jax.nn.sigmoid(g_ref[...].astype(F32)) * y

    merged = gated(oa_ref, wa_ref, ga_ref) + gated(ob_ref, wb_ref, gb_ref) + gated(oc_ref, wc_ref, gc_ref)
    y = jnp.dot(merged.astype(BF16), wo_ref[...], preferred_element_type=F32)
    gt1 = mod_ref[pl.ds(b, 1), 2 * D_MODEL:3 * D_MODEL]
    x1 = x_ref[...] + gt1 * y
    x1_ref[...] = x1
    sh2 = mod_ref[pl.ds(b, 1), 3 * D_MODEL:4 * D_MODEL]
    sc2 = mod_ref[pl.ds(b, 1), 4 * D_MODEL:5 * D_MODEL]
    h2 = _rms_mod(x1, g2_ref[...], sc2, sh2)
    h_hi = h2.astype(BF16)
    words = _pack_bf16_pairs(h2)
    for j in range(h2_ref.shape[0]):
        h2_ref[j] = words[:, j * SC_SUB:(j + 1) * SC_SUB]
    h_lo = (h2 - h_hi.astype(F32)).astype(BF16)
    logits = (jnp.dot(h_hi, wrh_ref[...], preferred_element_type=F32)
              + jnp.dot(h_lo, wrh_ref[...], preferred_element_type=F32)
              + jnp.dot(h_hi, wrl_ref[...], preferred_element_type=F32)) + br_ref[...]
    lane = lax.broadcasted_iota(jnp.int32, logits.shape, 1).astype(F32)
    neg_inf = -jnp.inf

    def first_argmax(vals):
        top = jnp.max(vals, axis=-1, keepdims=True)
        idx = jnp.min(jnp.where(vals == top, lane, float(LANES)), axis=-1, keepdims=True)
        return top, idx

    gl = jnp.where(lane < N_GROUPS, logits, neg_inf)
    gmax, gsel = first_argmax(gl)
    gprob = 1.0 / jnp.sum(jnp.exp(gl - gmax), axis=-1, keepdims=True)
    lo = N_GROUPS + EXPERTS_PER_GROUP * gsel
    el = jnp.where((lane >= lo) & (lane < lo + EXPERTS_PER_GROUP), logits, neg_inf)
    v1, i1 = first_argmax(el)
    el2 = jnp.where(lane == i1, neg_inf, el)
    v2, i2 = first_argmax(el2)
    e2 = jnp.exp(v2 - v1)
    den = 1.0 + e2
    route_ref[...] = jnp.where(lane == 0.0, i1 - N_GROUPS,
                               jnp.where(lane == 1.0, i2 - N_GROUPS,
                                         jnp.where(lane == 2.0, gprob / den,
                                                   jnp.where(lane == 3.0, gprob * (e2 / den), 0.0))))


def _merge(o_a, o_b, o_c, p16, x2, mod_l, g2, wa, wb, wc, wo, wr_hi, wr_lo, br, seq):
    n = x2.shape[0]
    tm = min(512, seq)
    one = pl.Buffered(1)

    def rows(width, cb=0):
        return pl.BlockSpec((tm, width), lambda i: (i, cb))

    def whole(a):
        return pl.BlockSpec(a.shape, lambda i: (0,) * a.ndim, pipeline_mode=one)

    return pl.pallas_call(
        functools.partial(_merge_kernel, tiles_per_batch=seq // tm),
        grid=(n // tm,),
        in_specs=[rows(A_WIDTH), rows(B_WIDTH), rows(C_V_WIDTH),
                  rows(D_MODEL, P16_GA), rows(D_MODEL, P16_GB), rows(D_MODEL, P16_GC),
                  rows(D_MODEL), whole(mod_l), pl.BlockSpec((1, D_MODEL), lambda i: (0, 0)),
                  whole(wa), whole(wb), whole(wc), whole(wo), whole(wr_hi), whole(wr_lo), whole(br)],
        out_specs=[rows(D_MODEL), pl.BlockSpec((SC_PIECES, tm, SC_SUB), lambda i: (0, i, 0)), rows(LANES)],
        out_shape=[jax.ShapeDtypeStruct((n, D_MODEL), F32),
                   jax.ShapeDtypeStruct((SC_PIECES, n, SC_SUB), jnp.int32),
                   jax.ShapeDtypeStruct((n, LANES), F32)],
        compiler_params=_cparams(("arbitrary",)),
        name="merge_route",
    )(o_a, o_b, o_c, p16, p16, p16, x2, mod_l, g2.reshape(1, D_MODEL), wa, wb, wc, wo, wr_hi, wr_lo, br)


def _lane_pick(vals, lane, idx):
    return jnp.sum(jnp.where(lane == idx, vals, 0.0), axis=-1, keepdims=True)


def _rank_kernel(route_ref, rk_ref, cnt_ref, run_ref):
    @pl.when(pl.program_id(0) == 0)
    def _():
        run_ref[...] = jnp.zeros_like(run_ref)

    route = route_ref[...]
    tb = route.shape[0]
    lane = lax.broadcasted_iota(jnp.int32, route.shape, 1).astype(F32)
    e1 = _lane_pick(route, lane, 0.0)
    e2 = _lane_pick(route, lane, 1.0)
    sel = jnp.where((lane == e1) | (lane == e2), 1.0, 0.0)
    r = lax.broadcasted_iota(jnp.int32, (tb, tb), 0)
    c = lax.broadcasted_iota(jnp.int32, (tb, tb), 1)
    before = jnp.where(c < r, 1.0, 0.0).astype(BF16)
    rank = jnp.dot(before, sel.astype(BF16), preferred_element_type=F32) + run_ref[0:1, :]
    rk_ref[...] = jnp.where(lane == 0.0, _lane_pick(rank, lane, e1),
                            jnp.where(lane == 1.0, _lane_pick(rank, lane, e2), 0.0))
    run_ref[...] = run_ref[...] + jnp.sum(sel, axis=0, keepdims=True)
    cnt_ref[...] = run_ref[...]


def _expert_ranks(route):
    n = route.shape[0]
    tb = min(PLAN_TB, n)
    return pl.pallas_call(
        _rank_kernel,
        grid=(n // tb,),
        in_specs=[pl.BlockSpec((tb, LANES), lambda i: (i, 0))],
        out_specs=[pl.BlockSpec((tb, LANES), lambda i: (i, 0)),
                   pl.BlockSpec((8, LANES), lambda i: (0, 0))],
        out_shape=[jax.ShapeDtypeStruct((n, LANES), F32), jax.ShapeDtypeStruct((8, LANES), F32)],
        scratch_shapes=[pltpu.VMEM((8, LANES), F32)],
        compiler_params=_cparams(("arbitrary",)),
        name="expert_ranks",
    )(route)


def _plan_kernel(cnt_ref, route_ref, rk_ref, pos_ref, tmap_ref):
    lane_i = lax.broadcasted_iota(jnp.int32, (8, LANES), 1)
    cnt = jnp.where(lane_i < N_EXPERTS, cnt_ref[...], 0.0)
    padded = jnp.floor((cnt + (MOE_TM - 1)) * (1.0 / MOE_TM)) * MOE_TM
    r = lax.broadcasted_iota(jnp.int32, (LANES, LANES), 0)
    c = lax.broadcasted_iota(jnp.int32, (LANES, LANES), 1)
    base = jnp.dot(padded, jnp.where(r < c, 1.0, 0.0), precision=lax.Precision.HIGHEST,
                   preferred_element_type=F32)

    route = route_ref[...]
    lane = lax.broadcasted_iota(jnp.int32, route.shape, 1).astype(F32)
    rk = rk_ref[...]
    base_row = base[0:1, :]
    pos1 = _lane_pick(base_row, lane, _lane_pick(route, lane, 0.0)) + _lane_pick(rk, lane, 0.0)
    pos2 = _lane_pick(base_row, lane, _lane_pick(route, lane, 1.0)) + _lane_pick(rk, lane, 1.0)
    pos_ref[...] = jnp.where(lane == 0.0, pos1, jnp.where(lane == 1.0, pos2, 0.0)).astype(jnp.int32)

    @pl.when(pl.program_id(0) == 0)
    def _():
        nt = tmap_ref.shape[0]
        tlane = lax.broadcasted_iota(jnp.int32, (nt, LANES), 1)
        start = (lax.broadcasted_iota(jnp.int32, (nt, LANES), 0) * MOE_TM).astype(F32)
        end_row = jnp.where(tlane < N_EXPERTS, base_row + padded[0:1, :], 3e38)
        expert = jnp.sum(jnp.where(end_row <= start, 1.0, 0.0), axis=-1, keepdims=True)
        expert_c = jnp.minimum(expert, N_EXPERTS - 1.0)
        tl = tlane.astype(F32)
        left = _lane_pick(cnt[0:1, :], tl, expert_c) - (start[:, 0:1] - _lane_pick(base_row, tl, expert_c))
        valid = jnp.where(expert < N_EXPERTS, jnp.clip(left, 0.0, float(MOE_TM)), 0.0)
        tmap_ref[...] = jnp.where(tlane == 0, expert_c, jnp.where(tlane == 1, valid, 0.0)).astype(jnp.int32)


def _expert_plan(cnt, route, rk, n_tiles):
    n = route.shape[0]
    tb = min(PLAN_TB, n)
    nt_pad = -(-n_tiles // 8) * 8
    return pl.pallas_call(
        _plan_kernel,
        grid=(n // tb,),
        in_specs=[pl.BlockSpec((8, LANES), lambda i: (0, 0)),
                  pl.BlockSpec((tb, LANES), lambda i: (i, 0)),
                  pl.BlockSpec((tb, LANES), lambda i: (i, 0))],
        out_specs=[pl.BlockSpec((tb, LANES), lambda i: (i, 0)),
                   pl.BlockSpec((nt_pad, LANES), lambda i: (0, 0))],
        out_shape=[jax.ShapeDtypeStruct((n, LANES), jnp.int32),
                   jax.ShapeDtypeStruct((nt_pad, LANES), jnp.int32)],
        compiler_params=_cparams(("arbitrary",)),
        name="expert_plan",
    )(cnt, route, rk)


def _sc_mesh():
    return plsc.VectorSubcoreMesh(core_axis_name="c", subcore_axis_name="s")


def _sc_scatter_rows(src, idx, out_rows):
    m = idx.shape[0]
    n_src_win = src.shape[0] // SC_WINDOW

    @functools.partial(pl.kernel, out_type=jax.ShapeDtypeStruct((out_rows, src.shape[1]), src.dtype),
                       mesh=_sc_mesh(), scratch_types=[])
    def scatter(x_hbm, i_hbm, o_hbm):
        def body(x_vmem, i_vmem):
            pltpu.sync_copy(x_vmem, o_hbm.at[i_vmem.at[0]])

        pltpu.emit_pipeline(
            body, grid=(m // SC_WINDOW,),
            in_specs=[pl.BlockSpec((SC_WINDOW, src.shape[1]), lambda i: (i % n_src_win, 0)),
                      pl.BlockSpec((1, SC_WINDOW), lambda i: (0, i))],
            out_specs=[], core_axis_name=("c", "s"),
            dimension_semantics=(pltpu.PARALLEL,))(x_hbm, i_hbm)

    return scatter(src, idx.reshape(1, m))


def _sc_gather_rows(table, idx):
    m = idx.shape[0]

    @functools.partial(pl.kernel, out_type=jax.ShapeDtypeStruct((m, table.shape[1]), table.dtype),
                       mesh=_sc_mesh(), scratch_types=[])
    def gather(x_hbm, i_hbm, o_hbm):
        def body(i_vmem, o_vmem):
            pltpu.sync_copy(x_hbm.at[i_vmem.at[0]], o_vmem)

        pltpu.emit_pipeline(
            body, grid=(m // SC_WINDOW,),
            in_specs=[pl.BlockSpec((1, SC_WINDOW), lambda i: (0, i))],
            out_specs=[pl.BlockSpec((SC_WINDOW, table.shape[1]), lambda i: (i, 0))],
            core_axis_name=("c", "s"),
            dimension_semantics=(pltpu.PARALLEL,))(i_hbm, o_hbm)

    return gather(table, idx.reshape(1, m))


def _piece_row_index(pos, rows):
    return (jnp.arange(SC_PIECES, dtype=jnp.int32)[:, None] * rows + pos[None, :]).reshape(-1)


def _grouped_kernel(te_ref, tv_ref, x_ref, wg_ref, wu_ref, wd_ref, o_ref):
    valid = tv_ref[pl.program_id(0)]

    @pl.when(valid > 0)
    def _():
        words = jnp.concatenate([x_ref[j] for j in range(SC_PIECES)], axis=1)
        row = lax.broadcasted_iota(jnp.int32, words.shape, 0)
        words = jnp.where(row < valid, words, 0)
        x = _unpack_bf16_pairs(words).astype(BF16)
        a = jnp.dot(x, wg_ref[0].astype(BF16), preferred_element_type=F32)
        u = jnp.dot(x, wu_ref[0].astype(BF16), preferred_element_type=F32)
        hm = (_silu(a) * u).astype(BF16)
        out = _pack_bf16_pairs(jnp.dot(hm, wd_ref[0].astype(BF16), preferred_element_type=F32))
        for j in range(SC_PIECES):
            o_ref[j] = out[:, j * SC_SUB:(j + 1) * SC_SUB]

    @pl.when(valid <= 0)
    def _():
        o_ref[...] = jnp.zeros_like(o_ref)


def _grouped_experts(tile_expert, tile_valid, xs, wg, wu, wd, layer):
    n_tiles = tile_expert.shape[0]
    rows_block = pl.BlockSpec((SC_PIECES, MOE_TM, SC_SUB), lambda i, te, tv: (0, i, 0))
    return pl.pallas_call(
        _grouped_kernel,
        grid_spec=pltpu.PrefetchScalarGridSpec(
            num_scalar_prefetch=2,
            grid=(n_tiles,),
            in_specs=[rows_block,
                      pl.BlockSpec((None, 1, D_MODEL, D_EXPERT), lambda i, te, tv: (layer, te[i], 0, 0)),
                      pl.BlockSpec((None, 1, D_MODEL, D_EXPERT), lambda i, te, tv: (layer, te[i], 0, 0)),
                      pl.BlockSpec((None, 1, D_EXPERT, D_MODEL), lambda i, te, tv: (layer, te[i], 0, 0))],
            out_specs=rows_block),
        out_shape=jax.ShapeDtypeStruct(xs.shape, jnp.int32),
        compiler_params=_cparams(("arbitrary",)),
        name="grouped_experts",
    )(tile_expert, tile_valid, xs, wg, wu, wd)


def _combine_kernel(x1_ref, y_ref, route_ref, mod_ref, o_ref, *, tiles_per_batch):
    b = pl.program_id(0) // tiles_per_batch
    gt2 = mod_ref[pl.ds(b, 1), 5 * D_MODEL:6 * D_MODEL]
    route = route_ref[...]
    lane = lax.broadcasted_iota(jnp.int32, route.shape, 1).astype(F32)
    w1 = _lane_pick(route, lane, 2.0)
    w2 = _lane_pick(route, lane, 3.0)
    y1 = _unpack_bf16_pairs(jnp.concatenate([y_ref[0, j] for j in range(SC_PIECES)], axis=1))
    y2 = _unpack_bf16_pairs(jnp.concatenate([y_ref[1, j] for j in range(SC_PIECES)], axis=1))
    o_ref[...] = x1_ref[...] + gt2 * (w1 * y1 + w2 * y2)


def _combine(x1, y2, route, mod_l, seq):
    n = x1.shape[0]
    tm = min(1024, seq)
    return pl.pallas_call(
        functools.partial(_combine_kernel, tiles_per_batch=seq // tm),
        grid=(n // tm,),
        in_specs=[pl.BlockSpec((tm, D_MODEL), lambda i: (i, 0)),
                  pl.BlockSpec((2, SC_PIECES, tm, SC_SUB), lambda i: (0, 0, i, 0)),
                  pl.BlockSpec((tm, LANES), lambda i: (i, 0)),
                  pl.BlockSpec(mod_l.shape, lambda i: (0, 0))],
        out_specs=pl.BlockSpec((tm, D_MODEL), lambda i: (i, 0)),
        out_shape=jax.ShapeDtypeStruct((n, D_MODEL), F32),
        compiler_params=_cparams(("arbitrary",)),
        name="moe_combine",
    )(x1, y2, route, mod_l)


def _moe(h2, route, x1, mod_l, wg, wu, wd, layer, seq):
    n = h2.shape[1]
    n_tiles = (2 * n) // MOE_TM + N_EXPERTS
    rows = n_tiles * MOE_TM
    rk, cnt = _expert_ranks(route)
    pos, tmap = _expert_plan(cnt, route, rk, n_tiles)
    idx = jnp.concatenate([_piece_row_index(pos[:, 0], rows), _piece_row_index(pos[:, 1], rows)])
    xs = _sc_scatter_rows(h2.reshape(SC_PIECES * n, SC_SUB), idx, SC_PIECES * rows)
    ys = _grouped_experts(tmap[:n_tiles, 0], tmap[:n_tiles, 1], xs.reshape(SC_PIECES, rows, SC_SUB),
                          wg, wu, wd, layer)
    y2 = _sc_gather_rows(ys.reshape(SC_PIECES * rows, SC_SUB), idx).reshape(2, SC_PIECES, n, SC_SUB)
    return _combine(x1, y2, route, mod_l, seq)


def _final_norm_kernel(x_ref, g_ref, o_ref):
    x = x_ref[...]
    ms = jnp.mean(x * x, axis=-1, keepdims=True)
    o_ref[...] = x * lax.rsqrt(ms + EPS) * g_ref[...]


def _final_norm(x2, g, seq):
    n = x2.shape[0]
    tm = min(1024, seq)
    return pl.pallas_call(
        _final_norm_kernel,
        grid=(n // tm,),
        in_specs=[pl.BlockSpec((tm, D_MODEL), lambda i: (i, 0)),
                  pl.BlockSpec((1, D_MODEL), lambda i: (0, 0))],
        out_specs=pl.BlockSpec((tm, D_MODEL), lambda i: (i, 0)),
        out_shape=jax.ShapeDtypeStruct((n, D_MODEL), F32),
        compiler_params=_cparams(("arbitrary",)),
        name="final_norm",
    )(x2, g.reshape(1, D_MODEL))


_IN_OFFS = [sum(IN_SPLITS[:i]) for i in range(len(IN_SPLITS) + 1)]
(_AQ, _AK, _AV, _IQ, _IK, _IW, _BQ, _BF, _BI, _BG, _CQ, _CK, _CV, _CG, _GA, _GB, _GC) = range(len(IN_SPLITS))


def _pack_kernel(w_ref, w16_ref, w32_ref, wvt_ref, wqt_ref):
    def cols(seg):
        return w_ref[0, :, _IN_OFFS[seg]:_IN_OFFS[seg + 1]]

    scale = {_AQ: A_HEAD_DIM ** -0.5 * LOG2_E, _CK: C_QK_DIM ** -0.5}
    at = 0
    for seg in (_CV, _CG, _AK, _BQ, _BI, _BG, _CQ, _CK, _GA, _GB, _GC):
        v = cols(seg)
        if seg in scale:
            v = v * scale[seg]
        w16_ref[0, :, at:at + v.shape[1]] = v.astype(BF16)
        at += v.shape[1]

    rows = w_ref.shape[1]
    w32_ref[0, :, P32_BF:P32_BF + B_WIDTH] = cols(_BF).astype(BF16)
    zeros = jnp.zeros((rows, LANES - IDX_DIM), F32)
    w32_ref[0, :, P32_IK:P32_IK + LANES] = jnp.concatenate([cols(_IK), zeros], axis=1).astype(BF16)
    w32_ref[0, :, P32_IW:P32_IW + LANES] = jnp.concatenate(
        [cols(_IW), jnp.zeros((rows, LANES - IDX_HEADS), F32)], axis=1).astype(BF16)
    wvt_ref[0] = cols(_AV).T.astype(BF16)
    wqt_ref[0, 0:A_WIDTH, :] = (cols(_AQ) * scale[_AQ]).T.astype(BF16)
    wqt_ref[0, A_WIDTH:, :] = cols(_IQ).T.astype(BF16)


def _pack_w_in(w_in):
    depth, d, width = w_in.shape
    rows = LANES
    w16_width = 13 * 1024
    qiq_rows = A_WIDTH + IDX_HEADS * IDX_DIM
    w32_width = P32_IW + LANES
    return pl.pallas_call(
        _pack_kernel,
        grid=(depth, d // rows),
        in_specs=[pl.BlockSpec((1, rows, width), lambda l, r: (l, r, 0))],
        out_specs=[pl.BlockSpec((1, rows, w16_width), lambda l, r: (l, r, 0)),
                   pl.BlockSpec((1, rows, w32_width), lambda l, r: (l, r, 0)),
                   pl.BlockSpec((1, A_WIDTH, rows), lambda l, r: (l, 0, r)),
                   pl.BlockSpec((1, qiq_rows, rows), lambda l, r: (l, 0, r))],
        out_shape=[jax.ShapeDtypeStruct((depth, d, w16_width), BF16),
                   jax.ShapeDtypeStruct((depth, d, w32_width), BF16),
                   jax.ShapeDtypeStruct((depth, A_WIDTH, d), BF16),
                   jax.ShapeDtypeStruct((depth, qiq_rows, d), BF16)],
        compiler_params=_cparams(("arbitrary", "arbitrary")),
        name="pack_w_in",
    )(w_in)


def _split_bf16(w):
    hi = w.astype(BF16)
    return hi, (w - hi.astype(F32)).astype(BF16)


def kernel(x, c, rel_bias, hgrn_lb_raw, norm1_g, norm2_g, ada_w, ada_b, w_in, hgrn_norm_g, w_branch_a,
           w_branch_b, w_branch_c, w_out, router_group_w, router_group_b, router_expert_w,
           router_expert_b, expert_w_gate, expert_w_up, expert_w_down, final_norm_g):
    bsz, seq, _ = x.shape
    depth = w_in.shape[0]
    n = bsz * seq
    x2 = x.reshape(n, D_MODEL)
    tq = min(DSA_TQ, seq)

    lb_all = _hgrn_lower_bounds(hgrn_lb_raw)
    c_pad = jnp.pad(c, ((0, (-bsz) % 8), (0, 0)))
    mod = _ada_mod(c_pad, ada_w, ada_b)
    bias_tiles = _bias_tiles(rel_bias, tq)
    ret_tables = _retention_tables(seq)
    w16_all, w32_all, wvt_all, wqt_all = _pack_w_in(w_in)

    for l in range(depth):
        p16, p32, vt, qiq_t = _project_all(x2, mod[l], norm1_g[l], w16_all, w32_all, wvt_all, wqt_all,
                                           l, tq, seq)
        o_a = _dsa_attention(qiq_t, p16, p32, vt, bias_tiles, bsz, seq)
        o_b = _hgrn2(p16, p32, lb_all[l], hgrn_norm_g[l], bsz, seq)
        o_c = _retention(p16, ret_tables, bsz, seq)
        wr = jnp.concatenate([router_group_w[l], router_expert_w[l],
                              jnp.zeros((D_MODEL, LANES - N_GROUPS - N_EXPERTS), F32)], axis=1)
        br = jnp.concatenate([router_group_b[l], router_expert_b[l],
                              jnp.zeros((LANES - N_GROUPS - N_EXPERTS,), F32)]).reshape(1, LANES)
        wr_hi, wr_lo = _split_bf16(wr)
        x1, h2, route = _merge(o_a, o_b, o_c, p16, x2, mod[l], norm2_g[l],
                               w_branch_a[l].astype(BF16), w_branch_b[l].astype(BF16),
                               w_branch_c[l].astype(BF16), w_out[l].astype(BF16),
                               wr_hi, wr_lo, br, seq)
        x2 = _moe(h2, route, x1, mod[l], expert_w_gate, expert_w_up, expert_w_down, l, seq)

    return _final_norm(x2, final_norm_g, seq).reshape(bsz, seq, D_MODEL)
```

```python
import functools
import math

import jax
import jax.numpy as jnp
from jax import lax
from jax.experimental import pallas as pl
from jax.experimental.pallas import tpu as pltpu
from jax.experimental.pallas import tpu_sc as plsc

F32 = jnp.float32
BF16 = jnp.bfloat16

D_MODEL = 1024
A_HEADS = 8
A_HEAD_DIM = 128
IDX_HEADS = 8
IDX_DIM = 64
TOPK_MAX = 256
REL_BUCKETS = 32
REL_MAX_DIST = 128
B_HEADS = 8
B_HEAD_DIM = 128
C_HEADS = 4
C_QK_DIM = 256
C_V_DIM = 512
N_GROUPS = 4
EXPERTS_PER_GROUP = 8
N_EXPERTS = 32
D_EXPERT = 512
EPS = 1e-6

A_WIDTH = A_HEADS * A_HEAD_DIM
B_WIDTH = B_HEADS * B_HEAD_DIM
C_QK_WIDTH = C_HEADS * C_QK_DIM
C_V_WIDTH = C_HEADS * C_V_DIM
IN_SPLITS = (A_WIDTH, A_WIDTH, A_WIDTH, IDX_HEADS * IDX_DIM, IDX_DIM, IDX_HEADS,
             B_WIDTH, B_WIDTH, B_WIDTH, B_WIDTH,
             C_QK_WIDTH, C_QK_WIDTH, C_V_WIDTH, C_V_WIDTH,
             D_MODEL, D_MODEL, D_MODEL)

LANES = 128
BF16_ROWS = 16
VMEM_LIMIT = 56 * 1024 * 1024

P16_CV, P16_CG = 0, 2
P16_AK, P16_BQ, P16_BI, P16_BG, P16_CQ, P16_CK, P16_GA, P16_GB, P16_GC = range(4, 13)
P32_BF = 0
P32_IK = 1024
P32_IW = 1152

DSA_TQ = 256
HGRN_L = 256
HGRN_C = 64
HGRN_SB = 16
HGRN_MAX_LOG_DECAY = 80.0
RET_C = 256
KEY_NEG_INF = -2139095041
HALF_BIAS = 32768
MASK_NEG = -1e30
LOG2_E = math.log2(math.e)
COUNT_CHAINS = 4
MOE_TM = 256
PLAN_TB = 1024
SC_WINDOW = 128
SC_SUB = 256
SC_PIECES = D_MODEL // 2 // SC_SUB

NT_DIMS = (((1,), (1,)), ((), ()))
TN_DIMS = (((0,), (0,)), ((), ()))


def _cparams(sem):
    return pltpu.CompilerParams(dimension_semantics=sem, vmem_limit_bytes=VMEM_LIMIT)


def _silu(x):
    return x * jax.nn.sigmoid(x)


def _pack_bf16_pairs(x):
    k = x.shape[1] // 2
    bits = pltpu.bitcast(x.astype(BF16).astype(F32), jnp.int32)
    return (bits[:, :k] & jnp.int32(-65536)) | lax.shift_right_logical(bits[:, k:], 16)


def _unpack_bf16_pairs(words):
    hi = pltpu.bitcast(words & jnp.int32(-65536), F32)
    lo = pltpu.bitcast(lax.shift_left(words, 16), F32)
    return jnp.concatenate([hi, lo], axis=1)


def _lb_kernel(raw_ref, o_ref):
    raw = raw_ref[...]
    m = jnp.max(raw, axis=0, keepdims=True)
    e = jnp.exp(raw - m)
    soft = e / jnp.sum(e, axis=0, keepdims=True)
    run = jnp.zeros_like(soft[0:1])
    for l in range(raw.shape[0]):
        run = run + soft[l:l + 1]
        o_ref[l:l + 1, :] = run - soft[0:1]


def _hgrn_lower_bounds(raw):
    return pl.pallas_call(
        _lb_kernel, out_shape=jax.ShapeDtypeStruct(raw.shape, F32), name="hgrn_lb")(raw)


def _ada_kernel(c_ref, w_ref, b_ref, o_ref):
    a = _silu(c_ref[...])
    o_ref[0] = jnp.dot(a, w_ref[0], precision=lax.Precision.HIGHEST,
                       preferred_element_type=F32) + b_ref[0]


def _ada_mod(c_pad, ada_w, ada_b):
    depth = ada_w.shape[0]
    rows = c_pad.shape[0]
    return pl.pallas_call(
        _ada_kernel,
        grid=(depth, 6),
        in_specs=[pl.BlockSpec((rows, D_MODEL), lambda l, j: (0, 0)),
                  pl.BlockSpec((1, D_MODEL, D_MODEL), lambda l, j: (l, 0, j)),
                  pl.BlockSpec((1, 1, D_MODEL), lambda l, j: (l, 0, j))],
        out_specs=pl.BlockSpec((1, rows, D_MODEL), lambda l, j: (l, 0, j)),
        out_shape=jax.ShapeDtypeStruct((depth, rows, 6 * D_MODEL), F32),
        compiler_params=_cparams(("arbitrary", "arbitrary")),
        name="ada_mod",
    )(c_pad, ada_w, ada_b.reshape(depth, 1, 6 * D_MODEL))


def _rms_mod(x, g, sc, sh):
    ms = jnp.mean(x * x, axis=-1, keepdims=True)
    return (x * lax.rsqrt(ms + EPS) * g) * (1.0 + sc) + sh


def _norm1(x_ref, mod_ref, g_ref, b):
    sh = mod_ref[pl.ds(b, 1), 0:D_MODEL]
    sc = mod_ref[pl.ds(b, 1), D_MODEL:2 * D_MODEL]
    return _rms_mod(x_ref[...], g_ref[...], sc, sh).astype(BF16)


def _proj_all_kernel(x_ref, mod_ref, g_ref, w16_ref, w32_ref, wvt_ref, wqt_ref,
                     o16_ref, o32_ref, ovt_ref, oqt_ref, h_ref, *, tiles_per_batch, n16, n32, chunk):
    j = pl.program_id(1)

    @pl.when(j == 0)
    def _():
        h_ref[...] = _norm1(x_ref, mod_ref, g_ref, pl.program_id(0) // tiles_per_batch)

    @pl.when(j < n16)
    def _():
        o16_ref[...] = jnp.dot(h_ref[...], w16_ref[...], preferred_element_type=F32).astype(o16_ref.dtype)

    @pl.when((j >= n16) & (j < n16 + n32))
    def _():
        o32_ref[...] = jnp.dot(h_ref[...], w32_ref[...], preferred_element_type=F32)

    def transposed(wt_ref, o_ref):
        res = lax.dot_general(wt_ref[...], h_ref[...], NT_DIMS, preferred_element_type=F32)
        for ci in range(o_ref.shape[0]):
            o_ref[ci] = res[:, ci * chunk:(ci + 1) * chunk].astype(o_ref.dtype)

    @pl.when(j == n16 + n32)
    def _():
        transposed(wvt_ref, ovt_ref)

    @pl.when(j == n16 + n32 + 1)
    def _():
        transposed(wqt_ref, oqt_ref)


def _project_all(x2, mod_l, g, w16_all, w32_all, wvt_all, wqt_all, layer, chunk, seq):
    n = x2.shape[0]
    tm = min(1024, seq)
    t16, t32 = 1024, 640
    n16, n32 = w16_all.shape[2] // t16, w32_all.shape[2] // t32
    vt_rows, qt_rows = wvt_all.shape[1], wqt_all.shape[1]
    one = pl.Buffered(1)

    def c16(j):
        return jnp.minimum(j, n16 - 1)

    def c32(j):
        return jnp.clip(j - n16, 0, n32 - 1)

    return pl.pallas_call(
        functools.partial(_proj_all_kernel, tiles_per_batch=seq // tm, n16=n16, n32=n32, chunk=chunk),
        grid=(n // tm, n16 + n32 + 2),
        in_specs=[pl.BlockSpec((tm, D_MODEL), lambda i, j: (i, 0)),
                  pl.BlockSpec(mod_l.shape, lambda i, j: (0, 0)),
                  pl.BlockSpec((1, D_MODEL), lambda i, j: (0, 0)),
                  pl.BlockSpec((None, D_MODEL, t16), lambda i, j: (layer, 0, c16(j))),
                  pl.BlockSpec((None, D_MODEL, t32), lambda i, j: (layer, 0, c32(j))),
                  pl.BlockSpec((None, vt_rows, D_MODEL), lambda i, j: (layer, 0, 0), pipeline_mode=one),
                  pl.BlockSpec((None, qt_rows, D_MODEL), lambda i, j: (layer, 0, 0), pipeline_mode=one)],
        out_specs=[pl.BlockSpec((tm, t16), lambda i, j: (i, c16(j))),
                   pl.BlockSpec((tm, t32), lambda i, j: (i, c32(j))),
                   pl.BlockSpec((tm // chunk, vt_rows, chunk), lambda i, j: (i, 0, 0)),
                   pl.BlockSpec((tm // chunk, qt_rows, chunk), lambda i, j: (i, 0, 0))],
        out_shape=[jax.ShapeDtypeStruct((n, w16_all.shape[2]), BF16),
                   jax.ShapeDtypeStruct((n, w32_all.shape[2]), F32),
                   jax.ShapeDtypeStruct((n // chunk, vt_rows, chunk), BF16),
                   jax.ShapeDtypeStruct((n // chunk, qt_rows, chunk), BF16)],
        scratch_shapes=[pltpu.VMEM((tm, D_MODEL), BF16)],
        compiler_params=_cparams(("arbitrary", "arbitrary")),
        name="proj_all",
    )(x2, mod_l, g.reshape(1, D_MODEL), w16_all, w32_all, wvt_all, wqt_all)


def _dsa_kernel(qiq_ref, iw_ref, k_ref, vt_ref, ik_ref, bias_ref, o_ref,
                key_ref, hi_ref, lo_ref, madd_ref, iwt_ref, m_ref, l_ref, acc_ref, s_ref, *, tq, topk):
    qi = pl.program_id(1)
    nck = qi + 1
    idx_scale = (IDX_HEADS * IDX_DIM) ** -0.5

    iwt_ref[...] = (iw_ref[...] * idx_scale).T

    krow = lax.broadcasted_iota(jnp.int32, (tq, tq), 0)
    qcol = lax.broadcasted_iota(jnp.int32, (tq, tq), 1)

    def score_chunk(c, carry):
        off = pl.multiple_of(c * tq, tq)
        ikc = ik_ref[pl.ds(off, tq), :].astype(BF16)[:, :IDX_DIM]
        acc = jnp.zeros((tq, tq), F32)
        for h in range(IDX_HEADS):
            s = jnp.dot(ikc, qiq_ref[0, A_WIDTH + h * IDX_DIM:A_WIDTH + (h + 1) * IDX_DIM, :],
                        preferred_element_type=F32)
            acc = acc + jnp.maximum(s, 0.0) * iwt_ref[h:h + 1, :]
        acc = jnp.where(acc == 0.0, 0.0, acc)
        acc = jnp.where(krow + (c - qi) * tq <= qcol, acc, -jnp.inf)
        kb = pltpu.bitcast(acc, jnp.int32)
        key = jnp.where(kb < 0, kb ^ jnp.int32(0x7FFFFFFF), kb)
        key_ref[c] = key
        hi_ref[c] = jnp.right_shift(key, 16).astype(jnp.int16)
        lo_ref[c] = ((key & 0xFFFF) - HALF_BIAS).astype(jnp.int16)
        return carry

    def paired_loop(count_, fn):
        def pair(i, carry):
            fn(2 * i, carry)
            fn(2 * i + 1, carry)
            return carry

        lax.fori_loop(0, count_ // 2, pair, 0)

        @pl.when(count_ % 2 == 1)
        def _():
            fn(count_ - 1, 0)

    paired_loop(nck, score_chunk)

    half_min = jnp.int16(-HALF_BIAS)

    @pl.when(nck % 2 == 1)
    def _():
        hi_ref[nck] = jnp.full((tq, tq), half_min)
        lo_ref[nck] = jnp.full((tq, tq), half_min)

    def add_hits(parts, hit, rows):
        parts = list(parts)
        for r in range(tq // rows):
            parts[r % COUNT_CHAINS] = parts[r % COUNT_CHAINS] + hit[r * rows:(r + 1) * rows, :]
        return tuple(parts)

    def total(parts):
        return jnp.sum(sum(p.astype(F32) for p in parts), axis=0, keepdims=True)

    def count16(ref, pred_fn, visit=None):
        def body(i, parts):
            for c in (2 * i, 2 * i + 1):
                chunk = ref[c]
                if visit is not None:
                    visit(c, chunk)
                parts = add_hits(parts, jnp.where(pred_fn(chunk), jnp.int16(1), jnp.int16(0)), BF16_ROWS)
            return parts

        return total(lax.fori_loop(0, (nck + 1) // 2, body,
                                   (jnp.zeros((BF16_ROWS, tq), jnp.int16),) * COUNT_CHAINS))

    def count32(pred_fn):
        def body(c, parts):
            return add_hits(parts, jnp.where(pred_fn(key_ref[c]), 1.0, 0.0), 8)

        return total(lax.fori_loop(0, nck, body, (jnp.zeros((8, tq), F32),) * COUNT_CHAINS))

    def bisect16(ref, target):
        def bit_step(i, theta):
            cand = theta + jnp.left_shift(jnp.int32(1), 15 - i)
            cand16 = cand.astype(jnp.int16)
            return jnp.where(count16(ref, lambda k: k >= cand16) >= target, cand, theta)

        return lax.fori_loop(0, 16, bit_step, jnp.full((1, tq), -HALF_BIAS, jnp.int32))

    theta_hi = bisect16(hi_ref, float(topk))
    theta_hi16 = theta_hi.astype(jnp.int16)

    def keep_bucket(c, hi_chunk):
        lo_ref[c] = jnp.where(hi_chunk == theta_hi16, lo_ref[c], half_min)

    need_lo = topk - count16(hi_ref, lambda k: k > theta_hi16, visit=keep_bucket)
    theta_lo = bisect16(lo_ref, need_lo)
    theta = theta_hi * (2 * HALF_BIAS) + (theta_lo + HALF_BIAS)
    theta = jnp.maximum(theta, KEY_NEG_INF + 1)

    def mask_chunk(c, cnt):
        ge = key_ref[c] >= theta
        madd_ref[c] = jnp.where(ge, 0.0, MASK_NEG)
        return cnt + jnp.sum(jnp.where(ge, 1.0, 0.0), axis=0, keepdims=True)

    cnt_ge = lax.fori_loop(0, nck, mask_chunk, jnp.zeros((1, tq), F32))

    @pl.when(jnp.max(cnt_ge) > topk)
    def _():
        need_eq = topk - count32(lambda kc: kc > theta)
        incl = jnp.where(krow >= qcol, 1.0, 0.0).astype(BF16)

        def tie_chunk(c, run):
            kc = key_ref[c]
            eq = kc == theta
            eqf = jnp.where(eq, 1.0, 0.0)
            pref = jnp.dot(incl, eqf.astype(BF16), preferred_element_type=F32) + run
            eq_add = jnp.where(pref <= need_eq, 0.0, MASK_NEG)
            madd_ref[c] = jnp.where(eq, eq_add, jnp.where(kc > theta, 0.0, MASK_NEG))
            return run + jnp.sum(eqf, axis=0, keepdims=True)

        lax.fori_loop(0, nck, tie_chunk, jnp.zeros((1, tq), F32))

    m_ref[...] = jnp.full(m_ref.shape, -jnp.inf, F32)
    l_ref[...] = jnp.zeros(l_ref.shape, F32)
    acc_ref[...] = jnp.zeros(acc_ref.shape, F32)

    ones_rows = jnp.ones((BF16_ROWS, tq), BF16)

    head_slices = [slice(h * A_HEAD_DIM, (h + 1) * A_HEAD_DIM) for h in range(A_HEADS)]

    def logits(c, h):
        off = pl.multiple_of(c * tq, tq)
        s_ref[h] = jnp.dot(k_ref[pl.ds(off, tq), head_slices[h]], qiq_ref[0, head_slices[h], :],
                           preferred_element_type=F32)

    def attend(c, h, lag):
        hs = head_slices[h]
        s = s_ref[h] + madd_ref[c]
        if lag is not None:
            s = s + bias_ref[h, lag]
        m_old = m_ref[h]
        m_new = jnp.maximum(m_old, jnp.max(s, axis=0, keepdims=True))
        alpha = jnp.exp2(m_old - m_new)
        p = jnp.exp2(s - m_new).astype(BF16)
        pv = jnp.dot(jnp.concatenate([vt_ref[c, hs, :], ones_rows], axis=0), p,
                     preferred_element_type=F32)
        l_ref[h] = alpha * l_ref[h] + pv[A_HEAD_DIM:A_HEAD_DIM + 1]
        acc_ref[h] = alpha * acc_ref[h] + pv[:A_HEAD_DIM]
        m_ref[h] = m_new

    def step(c, lag, prefetch):
        for h in range(A_HEADS):
            attend(c, h, lag)
            if prefetch:
                logits(c + 1, h)

    for h in range(A_HEADS):
        logits(0, h)

    def far_chunk(c, carry):
        step(c, None, True)
        return carry

    paired_loop(jnp.maximum(qi - 1, 0), far_chunk)

    @pl.when(qi >= 1)
    def _():
        step(qi - 1, 1, True)

    step(qi, 0, False)

    for h in range(A_HEADS):
        o = acc_ref[h] * (1.0 / l_ref[h])
        o_ref[:, h * A_HEAD_DIM:(h + 1) * A_HEAD_DIM] = o.T.astype(o_ref.dtype)


def _dsa_attention(qiq_t, p16, p32, vt, bias_tiles, bsz, seq):
    tq = min(DSA_TQ, seq)
    nq = seq // tq
    topk = min(TOPK_MAX, seq // 4)
    n = bsz * seq
    one = pl.Buffered(1)
    return pl.pallas_call(
        functools.partial(_dsa_kernel, tq=tq, topk=topk),
        grid=(bsz, nq),
        in_specs=[
            pl.BlockSpec((1, qiq_t.shape[1], tq), lambda b, i: (b * nq + i, 0, 0)),
            pl.BlockSpec((tq, LANES), lambda b, i: (b * nq + i, P32_IW // LANES)),
            pl.BlockSpec((seq, A_WIDTH), lambda b, i: (b, P16_AK), pipeline_mode=one),
            pl.BlockSpec((nq, A_WIDTH, tq), lambda b, i: (b, 0, 0), pipeline_mode=one),
            pl.BlockSpec((seq, LANES), lambda b, i: (b, P32_IK // LANES), pipeline_mode=one),
            pl.BlockSpec(bias_tiles.shape, lambda b, i: (0, 0, 0, 0), pipeline_mode=one),
        ],
        out_specs=pl.BlockSpec((tq, A_WIDTH), lambda b, i: (b * nq + i, 0)),
        out_shape=jax.ShapeDtypeStruct((n, A_WIDTH), BF16),
        scratch_shapes=[pltpu.VMEM((nq, tq, tq), jnp.int32),
                        pltpu.VMEM((nq + 1, tq, tq), jnp.int16),
                        pltpu.VMEM((nq + 1, tq, tq), jnp.int16),
                        pltpu.VMEM((nq, tq, tq), F32),
                        pltpu.VMEM((LANES, tq), F32),
                        pltpu.VMEM((A_HEADS, 1, tq), F32),
                        pltpu.VMEM((A_HEADS, 1, tq), F32),
                        pltpu.VMEM((A_HEADS, A_HEAD_DIM, tq), F32),
                        pltpu.VMEM((A_HEADS, tq, tq), F32)],
        compiler_params=_cparams(("arbitrary", "arbitrary")),
        name="dsa_attention",
    )(qiq_t, p32, p16, vt, p32, bias_tiles)


def _t5_bucket(rel):
    max_exact = REL_BUCKETS // 2
    relf = jnp.maximum(rel, 1).astype(F32)
    large = max_exact + (jnp.log(relf / max_exact) / math.log(REL_MAX_DIST / max_exact)
                         * (REL_BUCKETS - max_exact)).astype(jnp.int32)
    large = jnp.minimum(large, REL_BUCKETS - 1)
    return jnp.where(rel < max_exact, rel, large)


def _bias_tiles(rel_bias, tq):
    assert tq >= REL_MAX_DIST
    key = jnp.arange(tq, dtype=jnp.int32)[:, None]
    qry = jnp.arange(tq, dtype=jnp.int32)[None, :]
    bucket = jnp.stack([_t5_bucket(jnp.maximum(lag * tq + qry - key, 0)) for lag in range(2)])
    rel = ((rel_bias - rel_bias[REL_BUCKETS - 1:REL_BUCKETS]) * LOG2_E).astype(F32)
    onehot = bucket[None] == jnp.arange(REL_BUCKETS, dtype=jnp.int32)[:, None, None, None]
    return jnp.sum(jnp.where(onehot[:, None], rel[:, :, None, None, None], 0.0), axis=0)


def _hgrn_kernel(q_ref, f_ref, i_ref, g_ref, lb_ref, ng_ref, tril_ref, o_ref, st_ref, attn_ref, stage_ref,
                 *, rows):
    @pl.when(pl.program_id(1) == 0)
    def _():
        st_ref[...] = jnp.zeros_like(st_ref)

    lb = lb_ref[...]
    f = lb + (1.0 - lb) * jax.nn.sigmoid(f_ref[...])
    logf = jnp.log(f)
    kk = 1.0 - f
    g1 = logf.astype(BF16)
    r1 = logf - g1.astype(F32)
    g2 = r1.astype(BF16)
    g3 = (r1 - g2.astype(F32)).astype(BF16)
    tril = tril_ref[...]
    bcum = (jnp.dot(tril, g1, preferred_element_type=F32)
            + jnp.dot(tril, g2, preferred_element_type=F32)
            + jnp.dot(tril, g3, preferred_element_type=F32))

    srow = lax.broadcasted_iota(jnp.int32, (HGRN_C, B_HEAD_DIM), 0)
    trow = lax.broadcasted_iota(jnp.int32, (HGRN_SB, HGRN_C), 0)
    scol = lax.broadcasted_iota(jnp.int32, (HGRN_SB, HGRN_C), 1)
    ng = ng_ref[...]
    q = q_ref[...].astype(F32)
    qb_all = (q * jnp.exp(bcum)).astype(BF16)

    tiles = [(n, h) for n in range(rows // HGRN_C) for h in range(B_HEADS)]

    def rs(n):
        return slice(n * HGRN_C, (n + 1) * HGRN_C)

    def hs(h):
        return slice(h * B_HEAD_DIM, (h + 1) * B_HEAD_DIM)

    worst = None
    for j in range(rows // HGRN_SB):
        r0 = j * HGRN_SB
        span = bcum[r0 + HGRN_SB - 1:r0 + HGRN_SB]
        if r0 % HGRN_C:
            span = span - bcum[r0 - 1:r0]
        worst = span if worst is None else jnp.minimum(worst, span)
    in_range = jnp.min(worst) > -HGRN_MAX_LOG_DECAY

    @pl.when(in_range)
    def _():
        a_parts = {}
        for n, h in tiles:
            bc, qc, kc = bcum[rs(n), hs(h)], q[rs(n), hs(h)], kk[rs(n), hs(h)]
            for sb in range(HGRN_C // HGRN_SB):
                s0 = sb * HGRN_SB
                beta = bc[s0 - 1:s0] if sb > 0 else jnp.zeros((1, B_HEAD_DIM), F32)
                qs = (qc[s0:s0 + HGRN_SB] * jnp.exp(bc[s0:s0 + HGRN_SB] - beta)).astype(BF16)
                live = s0 + HGRN_SB
                ks = (kc[:live] * jnp.exp(beta - bc[:live])).astype(BF16)
                if live < HGRN_C:
                    ks = jnp.concatenate([ks, jnp.zeros((HGRN_C - live, B_HEAD_DIM), BF16)], axis=0)
                a_parts[n, h, sb] = lax.dot_general(qs, ks, NT_DIMS, preferred_element_type=F32)
        for ti, (n, h) in enumerate(tiles):
            a_rows = [jnp.where(scol <= trow + sb * HGRN_SB, a_parts[n, h, sb], 0.0)
                      for sb in range(HGRN_C // HGRN_SB)]
            attn_ref[ti] = jnp.concatenate(a_rows, axis=0)

    @pl.when(jnp.logical_not(in_range))
    def _():
        for h in range(B_HEADS):
            stage_ref[0, h] = bcum[:, hs(h)]
            stage_ref[1, h] = q[:, hs(h)]
            stage_ref[2, h] = kk[:, hs(h)]
        t_idx = lax.broadcasted_iota(jnp.int32, (HGRN_C, HGRN_C), 0)
        s_idx = lax.broadcasted_iota(jnp.int32, (HGRN_C, HGRN_C), 1)

        def safe_tile(ti, carry):
            n, h = ti // B_HEADS, ti % B_HEADS
            r0 = pl.multiple_of(n * HGRN_C, HGRN_C)
            bc = stage_ref[0, h, pl.ds(r0, HGRN_C), :]
            qc = stage_ref[1, h, pl.ds(r0, HGRN_C), :]
            kc = stage_ref[2, h, pl.ds(r0, HGRN_C), :]
            acc = jnp.where(t_idx == s_idx, jnp.sum(qc * kc, axis=-1, keepdims=True), 0.0)
            block = HGRN_C
            while block >= 2:
                half = block // 2
                ref_row = (t_idx & -block) + (half - 1)
                bref = jnp.dot(jnp.where(s_idx == ref_row, 1.0, 0.0), bc, precision=lax.Precision.HIGHEST,
                               preferred_element_type=F32)
                second = (srow & (block - 1)) >= half
                qs = jnp.where(second, qc * jnp.exp(jnp.where(second, bc - bref, 0.0)), 0.0).astype(BF16)
                ks = jnp.where(second, 0.0, kc * jnp.exp(jnp.where(second, 0.0, bref - bc))).astype(BF16)
                a = lax.dot_general(qs, ks, NT_DIMS, preferred_element_type=F32)
                acc = acc + jnp.where((t_idx & -block) == (s_idx & -block), a, 0.0)
                block = half
            attn_ref[ti] = acc
            return carry

        lax.fori_loop(0, len(tiles), safe_tile, 0)

    intra, upd, dec = {}, {}, {}
    for ti, (n, h) in enumerate(tiles):
        attn = attn_ref[ti].astype(BF16)
        vc = i_ref[rs(n), hs(h)]
        intra[n, h] = jnp.dot(attn, vc, preferred_element_type=F32)
        bc = bcum[rs(n), hs(h)]
        blast = bc[HGRN_C - 1:HGRN_C]
        kdec = (kk[rs(n), hs(h)] * jnp.exp(blast - bc)).astype(BF16)
        upd[n, h] = lax.dot_general(vc, kdec, TN_DIMS, preferred_element_type=F32)
        dec[n, h] = jnp.exp(blast)
    for n, h in tiles:
        st = st_ref[h]
        o = intra[n, h] + lax.dot_general(qb_all[rs(n), hs(h)], st.astype(BF16), NT_DIMS,
                                          preferred_element_type=F32)
        st_ref[h] = st * dec[n, h] + upd[n, h]
        ms = jnp.mean(o * o, axis=-1, keepdims=True)
        on = o * lax.rsqrt(ms + EPS) * ng
        o_ref[rs(n), hs(h)] = (on * _silu(g_ref[rs(n), hs(h)].astype(F32))).astype(o_ref.dtype)


def _hgrn2(p16, p32, lb_l, norm_g, bsz, seq):
    rows = min(HGRN_L, seq)
    nj = seq // rows
    n = bsz * seq
    r = jnp.arange(rows, dtype=jnp.int32)
    tril = ((r[:, None] >= r[None, :]) & (r[:, None] // HGRN_C == r[None, :] // HGRN_C)).astype(BF16)

    def col(base):
        return lambda b, j: (b * nj + j, base)

    return pl.pallas_call(
        functools.partial(_hgrn_kernel, rows=rows),
        grid=(bsz, nj),
        in_specs=[
            pl.BlockSpec((rows, B_WIDTH), col(P16_BQ)),
            pl.BlockSpec((rows, B_WIDTH), col(P32_BF // B_WIDTH)),
            pl.BlockSpec((rows, B_WIDTH), col(P16_BI)),
            pl.BlockSpec((rows, B_WIDTH), col(P16_BG)),
            pl.BlockSpec((1, B_WIDTH), lambda b, j: (0, 0)),
            pl.BlockSpec((1, B_HEAD_DIM), lambda b, j: (0, 0)),
            pl.BlockSpec((rows, rows), lambda b, j: (0, 0)),
        ],
        out_specs=pl.BlockSpec((rows, B_WIDTH), lambda b, j: (b * nj + j, 0)),
        out_shape=jax.ShapeDtypeStruct((n, B_WIDTH), BF16),
        scratch_shapes=[pltpu.VMEM((B_HEADS, B_HEAD_DIM, B_HEAD_DIM), F32),
                        pltpu.VMEM((rows // HGRN_C * B_HEADS, HGRN_C, HGRN_C), F32),
                        pltpu.VMEM((3, B_HEADS, rows, B_HEAD_DIM), F32)],
        compiler_params=_cparams(("arbitrary", "arbitrary")),
        name="hgrn2",
    )(p16, p32, p16, p16, lb_l.reshape(1, B_WIDTH), norm_g.reshape(1, B_HEAD_DIM), tril)


def _ret_kernel(q_ref, k_ref, v_ref, g_ref, cos_ref, sin_ref, idec_ref, qdec_ref, kdec_ref, cdec_ref,
                o_ref, st_ref):
    @pl.when(pl.program_id(1) == 0)
    def _():
        st_ref[...] = jnp.zeros_like(st_ref)

    heads = range(C_HEADS)
    cos = jnp.concatenate([cos_ref[...]] * C_HEADS, axis=1)
    sin_signed = jnp.concatenate([sin_ref[...]] * C_HEADS, axis=1)
    even = lax.broadcasted_iota(jnp.int32, cos.shape, 1) % 2 == 0

    def rot(a):
        swapped = jnp.where(even, pltpu.roll(a, C_QK_WIDTH - 1, 1), pltpu.roll(a, 1, 1))
        return a * cos + swapped * sin_signed

    qr = rot(q_ref[...].astype(F32))
    kr = rot(k_ref[...].astype(F32))
    qk = [slice(h * C_QK_DIM, (h + 1) * C_QK_DIM) for h in heads]
    vs = [slice(h * C_V_DIM, (h + 1) * C_V_DIM) for h in heads]
    attn = [lax.dot_general(qr[:, qk[h]].astype(BF16), kr[:, qk[h]].astype(BF16), NT_DIMS,
                            preferred_element_type=F32) * idec_ref[h] for h in heads]
    inter = [jnp.dot((qr[:, qk[h]] * qdec_ref[h]).astype(BF16), st_ref[h].astype(BF16),
                     preferred_element_type=F32) for h in heads]
    intra = [jnp.dot(attn[h].astype(BF16), v_ref[:, vs[h]], preferred_element_type=F32) for h in heads]
    upd = [jnp.dot((kr[:, qk[h]] * kdec_ref[h]).T.astype(BF16), v_ref[:, vs[h]],
                   preferred_element_type=F32) for h in heads]
    for h in heads:
        st_ref[h] = cdec_ref[h, 0:1, :] * st_ref[h] + upd[h]
        o = intra[h] + inter[h]
        ms = jnp.mean(o * o, axis=-1, keepdims=True)
        o_ref[:, vs[h]] = (_silu(g_ref[:, vs[h]].astype(F32)) * (o * lax.rsqrt(ms + EPS))).astype(o_ref.dtype)


def _retention_tables(seq):
    pos = jnp.arange(seq, dtype=F32)
    theta = jnp.repeat(1.0 / (10000.0 ** jnp.linspace(0.0, 1.0, C_QK_DIM // 2)), 2)
    ang = pos[:, None] * theta[None, :]
    pair_sign = jnp.where(jnp.arange(C_QK_DIM) % 2 == 0, -1.0, 1.0)
    log_gamma = jnp.log(1.0 - 2.0 ** (-5.0 - jnp.arange(C_HEADS, dtype=F32)))
    idx = jnp.arange(RET_C, dtype=F32)
    causal = idx[:, None] >= idx[None, :]
    idec = jnp.exp(jnp.where(causal[None], (idx[:, None] - idx[None, :])[None] * log_gamma[:, None, None],
                             -jnp.inf))
    qdec = jnp.exp((idx + 1.0)[None, :] * log_gamma[:, None])[..., None]
    kdec = jnp.exp((RET_C - 1.0 - idx)[None, :] * log_gamma[:, None])[..., None]
    cdec = jnp.exp(RET_C * log_gamma)[:, None, None]
    return (jnp.cos(ang), jnp.sin(ang) * pair_sign[None, :], idec,
            jnp.broadcast_to(qdec, (C_HEADS, RET_C, C_QK_DIM)),
            jnp.broadcast_to(kdec, (C_HEADS, RET_C, C_QK_DIM)),
            jnp.broadcast_to(cdec, (C_HEADS, 8, C_V_DIM)))


def _retention(p16, tables, bsz, seq):
    cos, sin, idec, qdec, kdec, cdec = tables
    nj = seq // RET_C
    n = bsz * seq
    v_blk = C_V_WIDTH // 1024

    def whole(a):
        return pl.BlockSpec(a.shape, lambda b, j: (0,) * a.ndim)

    return pl.pallas_call(
        _ret_kernel,
        grid=(bsz, nj),
        in_specs=[
            pl.BlockSpec((RET_C, C_QK_WIDTH), lambda b, j: (b * nj + j, P16_CQ)),
            pl.BlockSpec((RET_C, C_QK_WIDTH), lambda b, j: (b * nj + j, P16_CK)),
            pl.BlockSpec((RET_C, C_V_WIDTH), lambda b, j: (b * nj + j, P16_CV // v_blk)),
            pl.BlockSpec((RET_C, C_V_WIDTH), lambda b, j: (b * nj + j, P16_CG // v_blk)),
            pl.BlockSpec((RET_C, C_QK_DIM), lambda b, j: (j, 0)),
            pl.BlockSpec((RET_C, C_QK_DIM), lambda b, j: (j, 0)),
            whole(idec), whole(qdec), whole(kdec), whole(cdec),
        ],
        out_specs=pl.BlockSpec((RET_C, C_V_WIDTH), lambda b, j: (b * nj + j, 0)),
        out_shape=jax.ShapeDtypeStruct((n, C_V_WIDTH), BF16),
        scratch_shapes=[pltpu.VMEM((C_HEADS, C_QK_DIM, C_V_DIM), F32)],
        compiler_params=_cparams(("arbitrary", "arbitrary")),
        name="retention",
    )(p16, p16, p16, p16, cos, sin, idec, qdec, kdec, cdec)


def _merge_kernel(oa_ref, ob_ref, oc_ref, ga_ref, gb_ref, gc_ref, x_ref, mod_ref, g2_ref,
                  wa_ref, wb_ref, wc_ref, wo_ref, wrh_ref, wrl_ref, br_ref,
                  x1_ref, h2_ref, route_ref, *, tiles_per_batch):
    b = pl.program_id(0) // tiles_per_batch

    def gated(o_ref, w_ref, g_ref):
        y = jnp.dot(o_ref[...], w_ref[...], preferred_element_type=F32)
        return jax.nn.sigmoid(g_ref[...].astype(F32)) * y

    merged = gated(oa_ref, wa_ref, ga_ref) + gated(ob_ref, wb_ref, gb_ref) + gated(oc_ref, wc_ref, gc_ref)
    y = jnp.dot(merged.astype(BF16), wo_ref[...], preferred_element_type=F32)
    gt1 = mod_ref[pl.ds(b, 1), 2 * D_MODEL:3 * D_MODEL]
    x1 = x_ref[...] + gt1 * y
    x1_ref[...] = x1
    sh2 = mod_ref[pl.ds(b, 1), 3 * D_MODEL:4 * D_MODEL]
    sc2 = mod_ref[pl.ds(b, 1), 4 * D_MODEL:5 * D_MODEL]
    h2 = _rms_mod(x1, g2_ref[...], sc2, sh2)
    h_hi = h2.astype(BF16)
    words = _pack_bf16_pairs(h2)
    for j in range(h2_ref.shape[0]):
        h2_ref[j] = words[:, j * SC_SUB:(j + 1) * SC_SUB]
    h_lo = (h2 - h_hi.astype(F32)).astype(BF16)
    logits = (jnp.dot(h_hi, wrh_ref[...], preferred_element_type=F32)
              + jnp.dot(h_lo, wrh_ref[...], preferred_element_type=F32)
              + jnp.dot(h_hi, wrl_ref[...], preferred_element_type=F32)) + br_ref[...]
    lane = lax.broadcasted_iota(jnp.int32, logits.shape, 1).astype(F32)
    neg_inf = -jnp.inf

    def first_argmax(vals):
        top = jnp.max(vals, axis=-1, keepdims=True)
        idx = jnp.min(jnp.where(vals == top, lane, float(LANES)), axis=-1, keepdims=True)
        return top, idx

    gl = jnp.where(lane < N_GROUPS, logits, neg_inf)
    gmax, gsel = first_argmax(gl)
    gprob = 1.0 / jnp.sum(jnp.exp(gl - gmax), axis=-1, keepdims=True)
    lo = N_GROUPS + EXPERTS_PER_GROUP * gsel
    el = jnp.where((lane >= lo) & (lane < lo + EXPERTS_PER_GROUP), logits, neg_inf)
    v1, i1 = first_argmax(el)
    el2 = jnp.where(lane == i1, neg_inf, el)
    v2, i2 = first_argmax(el2)
    e2 = jnp.exp(v2 - v1)
    den = 1.0 + e2
    route_ref[...] = jnp.where(lane == 0.0, i1 - N_GROUPS,
                               jnp.where(lane == 1.0, i2 - N_GROUPS,
                                         jnp.where(lane == 2.0, gprob / den,
                                                   jnp.where(lane == 3.0, gprob * (e2 / den), 0.0))))


def _merge(o_a, o_b, o_c, p16, x2, mod_l, g2, wa, wb, wc, wo, wr_hi, wr_lo, br, seq):
    n = x2.shape[0]
    tm = min(512, seq)
    one = pl.Buffered(1)

    def rows(width, cb=0):
        return pl.BlockSpec((tm, width), lambda i: (i, cb))

    def whole(a):
        return pl.BlockSpec(a.shape, lambda i: (0,) * a.ndim, pipeline_mode=one)

    return pl.pallas_call(
        functools.partial(_merge_kernel, tiles_per_batch=seq // tm),
        grid=(n // tm,),
        in_specs=[rows(A_WIDTH), rows(B_WIDTH), rows(C_V_WIDTH),
                  rows(D_MODEL, P16_GA), rows(D_MODEL, P16_GB), rows(D_MODEL, P16_GC),
                  rows(D_MODEL), whole(mod_l), pl.BlockSpec((1, D_MODEL), lambda i: (0, 0)),
                  whole(wa), whole(wb), whole(wc), whole(wo), whole(wr_hi), whole(wr_lo), whole(br)],
        out_specs=[rows(D_MODEL), pl.BlockSpec((SC_PIECES, tm, SC_SUB), lambda i: (0, i, 0)), rows(LANES)],
        out_shape=[jax.ShapeDtypeStruct((n, D_MODEL), F32),
                   jax.ShapeDtypeStruct((SC_PIECES, n, SC_SUB), jnp.int32),
                   jax.ShapeDtypeStruct((n, LANES), F32)],
        compiler_params=_cparams(("arbitrary",)),
        name="merge_route",
    )(o_a, o_b, o_c, p16, p16, p16, x2, mod_l, g2.reshape(1, D_MODEL), wa, wb, wc, wo, wr_hi, wr_lo, br)


def _lane_pick(vals, lane, idx):
    return jnp.sum(jnp.where(lane == idx, vals, 0.0), axis=-1, keepdims=True)


def _rank_kernel(route_ref, rk_ref, cnt_ref, run_ref):
    @pl.when(pl.program_id(0) == 0)
    def _():
        run_ref[...] = jnp.zeros_like(run_ref)

    route = route_ref[...]
    tb = route.shape[0]
    lane = lax.broadcasted_iota(jnp.int32, route.shape, 1).astype(F32)
    e1 = _lane_pick(route, lane, 0.0)
    e2 = _lane_pick(route, lane, 1.0)
    sel = jnp.where((lane == e1) | (lane == e2), 1.0, 0.0)
    r = lax.broadcasted_iota(jnp.int32, (tb, tb), 0)
    c = lax.broadcasted_iota(jnp.int32, (tb, tb), 1)
    before = jnp.where(c < r, 1.0, 0.0).astype(BF16)
    rank = jnp.dot(before, sel.astype(BF16), preferred_element_type=F32) + run_ref[0:1, :]
    rk_ref[...] = jnp.where(lane == 0.0, _lane_pick(rank, lane, e1),
                            jnp.where(lane == 1.0, _lane_pick(rank, lane, e2), 0.0))
    run_ref[...] = run_ref[...] + jnp.sum(sel, axis=0, keepdims=True)
    cnt_ref[...] = run_ref[...]


def _expert_ranks(route):
    n = route.shape[0]
    tb = min(PLAN_TB, n)
    return pl.pallas_call(
        _rank_kernel,
        grid=(n // tb,),
        in_specs=[pl.BlockSpec((tb, LANES), lambda i: (i, 0))],
        out_specs=[pl.BlockSpec((tb, LANES), lambda i: (i, 0)),
                   pl.BlockSpec((8, LANES), lambda i: (0, 0))],
        out_shape=[jax.ShapeDtypeStruct((n, LANES), F32), jax.ShapeDtypeStruct((8, LANES), F32)],
        scratch_shapes=[pltpu.VMEM((8, LANES), F32)],
        compiler_params=_cparams(("arbitrary",)),
        name="expert_ranks",
    )(route)


def _plan_kernel(cnt_ref, route_ref, rk_ref, pos_ref, tmap_ref):
    lane_i = lax.broadcasted_iota(jnp.int32, (8, LANES), 1)
    cnt = jnp.where(lane_i < N_EXPERTS, cnt_ref[...], 0.0)
    padded = jnp.floor((cnt + (MOE_TM - 1)) * (1.0 / MOE_TM)) * MOE_TM
    r = lax.broadcasted_iota(jnp.int32, (LANES, LANES), 0)
    c = lax.broadcasted_iota(jnp.int32, (LANES, LANES), 1)
    base = jnp.dot(padded, jnp.where(r < c, 1.0, 0.0), precision=lax.Precision.HIGHEST,
                   preferred_element_type=F32)

    route = route_ref[...]
    lane = lax.broadcasted_iota(jnp.int32, route.shape, 1).astype(F32)
    rk = rk_ref[...]
    base_row = base[0:1, :]
    pos1 = _lane_pick(base_row, lane, _lane_pick(route, lane, 0.0)) + _lane_pick(rk, lane, 0.0)
    pos2 = _lane_pick(base_row, lane, _lane_pick(route, lane, 1.0)) + _lane_pick(rk, lane, 1.0)
    pos_ref[...] = jnp.where(lane == 0.0, pos1, jnp.where(lane == 1.0, pos2, 0.0)).astype(jnp.int32)

    @pl.when(pl.program_id(0) == 0)
    def _():
        nt = tmap_ref.shape[0]
        tlane = lax.broadcasted_iota(jnp.int32, (nt, LANES), 1)
        start = (lax.broadcasted_iota(jnp.int32, (nt, LANES), 0) * MOE_TM).astype(F32)
        end_row = jnp.where(tlane < N_EXPERTS, base_row + padded[0:1, :], 3e38)
        expert = jnp.sum(jnp.where(end_row <= start, 1.0, 0.0), axis=-1, keepdims=True)
        expert_c = jnp.minimum(expert, N_EXPERTS - 1.0)
        tl = tlane.astype(F32)
        left = _lane_pick(cnt[0:1, :], tl, expert_c) - (start[:, 0:1] - _lane_pick(base_row, tl, expert_c))
        valid = jnp.where(expert < N_EXPERTS, jnp.clip(left, 0.0, float(MOE_TM)), 0.0)
        tmap_ref[...] = jnp.where(tlane == 0, expert_c, jnp.where(tlane == 1, valid, 0.0)).astype(jnp.int32)


def _expert_plan(cnt, route, rk, n_tiles):
    n = route.shape[0]
    tb = min(PLAN_TB, n)
    nt_pad = -(-n_tiles // 8) * 8
    return pl.pallas_call(
        _plan_kernel,
        grid=(n // tb,),
        in_specs=[pl.BlockSpec((8, LANES), lambda i: (0, 0)),
                  pl.BlockSpec((tb, LANES), lambda i: (i, 0)),
                  pl.BlockSpec((tb, LANES), lambda i: (i, 0))],
        out_specs=[pl.BlockSpec((tb, LANES), lambda i: (i, 0)),
                   pl.BlockSpec((nt_pad, LANES), lambda i: (0, 0))],
        out_shape=[jax.ShapeDtypeStruct((n, LANES), jnp.int32),
                   jax.ShapeDtypeStruct((nt_pad, LANES), jnp.int32)],
        compiler_params=_cparams(("arbitrary",)),
        name="expert_plan",
    )(cnt, route, rk)


def _sc_mesh():
    return plsc.VectorSubcoreMesh(core_axis_name="c", subcore_axis_name="s")


def _sc_scatter_rows(src, idx, out_rows):
    m = idx.shape[0]
    n_src_win = src.shape[0] // SC_WINDOW

    @functools.partial(pl.kernel, out_type=jax.ShapeDtypeStruct((out_rows, src.shape[1]), src.dtype),
                       mesh=_sc_mesh(), scratch_types=[])
    def scatter(x_hbm, i_hbm, o_hbm):
        def body(x_vmem, i_vmem):
            pltpu.sync_copy(x_vmem, o_hbm.at[i_vmem.at[0]])

        pltpu.emit_pipeline(
            body, grid=(m // SC_WINDOW,),
            in_specs=[pl.BlockSpec((SC_WINDOW, src.shape[1]), lambda i: (i % n_src_win, 0)),
                      pl.BlockSpec((1, SC_WINDOW), lambda i: (0, i))],
            out_specs=[], core_axis_name=("c", "s"),
            dimension_semantics=(pltpu.PARALLEL,))(x_hbm, i_hbm)

    return scatter(src, idx.reshape(1, m))


def _sc_gather_rows(table, idx):
    m = idx.shape[0]

    @functools.partial(pl.kernel, out_type=jax.ShapeDtypeStruct((m, table.shape[1]), table.dtype),
                       mesh=_sc_mesh(), scratch_types=[])
    def gather(x_hbm, i_hbm, o_hbm):
        def body(i_vmem, o_vmem):
            pltpu.sync_copy(x_hbm.at[i_vmem.at[0]], o_vmem)

        pltpu.emit_pipeline(
            body, grid=(m // SC_WINDOW,),
            in_specs=[pl.BlockSpec((1, SC_WINDOW), lambda i: (0, i))],
            out_specs=[pl.BlockSpec((SC_WINDOW, table.shape[1]), lambda i: (i, 0))],
            core_axis_name=("c", "s"),
            dimension_semantics=(pltpu.PARALLEL,))(i_hbm, o_hbm)

    return gather(table, idx.reshape(1, m))


def _piece_row_index(pos, rows):
    return (jnp.arange(SC_PIECES, dtype=jnp.int32)[:, None] * rows + pos[None, :]).reshape(-1)


def _grouped_kernel(te_ref, tv_ref, x_ref, wg_ref, wu_ref, wd_ref, o_ref):
    valid = tv_ref[pl.program_id(0)]

    @pl.when(valid > 0)
    def _():
        words = jnp.concatenate([x_ref[j] for j in range(SC_PIECES)], axis=1)
        row = lax.broadcasted_iota(jnp.int32, words.shape, 0)
        words = jnp.where(row < valid, words, 0)
        x = _unpack_bf16_pairs(words).astype(BF16)
        a = jnp.dot(x, wg_ref[0].astype(BF16), preferred_element_type=F32)
        u = jnp.dot(x, wu_ref[0].astype(BF16), preferred_element_type=F32)
        hm = (_silu(a) * u).astype(BF16)
        out = _pack_bf16_pairs(jnp.dot(hm, wd_ref[0].astype(BF16), preferred_element_type=F32))
        for j in range(SC_PIECES):
            o_ref[j] = out[:, j * SC_SUB:(j + 1) * SC_SUB]

    @pl.when(valid <= 0)
    def _():
        o_ref[...] = jnp.zeros_like(o_ref)


def _grouped_experts(tile_expert, tile_valid, xs, wg, wu, wd, layer):
    n_tiles = tile_expert.shape[0]
    rows_block = pl.BlockSpec((SC_PIECES, MOE_TM, SC_SUB), lambda i, te, tv: (0, i, 0))
    return pl.pallas_call(
        _grouped_kernel,
        grid_spec=pltpu.PrefetchScalarGridSpec(
            num_scalar_prefetch=2,
            grid=(n_tiles,),
            in_specs=[rows_block,
                      pl.BlockSpec((None, 1, D_MODEL, D_EXPERT), lambda i, te, tv: (layer, te[i], 0, 0)),
                      pl.BlockSpec((None, 1, D_MODEL, D_EXPERT), lambda i, te, tv: (layer, te[i], 0, 0)),
                      pl.BlockSpec((None, 1, D_EXPERT, D_MODEL), lambda i, te, tv: (layer, te[i], 0, 0))],
            out_specs=rows_block),
        out_shape=jax.ShapeDtypeStruct(xs.shape, jnp.int32),
        compiler_params=_cparams(("arbitrary",)),
        name="grouped_experts",
    )(tile_expert, tile_valid, xs, wg, wu, wd)


def _combine_kernel(x1_ref, y_ref, route_ref, mod_ref, o_ref, *, tiles_per_batch):
    b = pl.program_id(0) // tiles_per_batch
    gt2 = mod_ref[pl.ds(b, 1), 5 * D_MODEL:6 * D_MODEL]
    route = route_ref[...]
    lane = lax.broadcasted_iota(jnp.int32, route.shape, 1).astype(F32)
    w1 = _lane_pick(route, lane, 2.0)
    w2 = _lane_pick(route, lane, 3.0)
    y1 = _unpack_bf16_pairs(jnp.concatenate([y_ref[0, j] for j in range(SC_PIECES)], axis=1))
    y2 = _unpack_bf16_pairs(jnp.concatenate([y_ref[1, j] for j in range(SC_PIECES)], axis=1))
    o_ref[...] = x1_ref[...] + gt2 * (w1 * y1 + w2 * y2)


def _combine(x1, y2, route, mod_l, seq):
    n = x1.shape[0]
    tm = min(1024, seq)
    return pl.pallas_call(
        functools.partial(_combine_kernel, tiles_per_batch=seq // tm),
        grid=(n // tm,),
        in_specs=[pl.BlockSpec((tm, D_MODEL), lambda i: (i, 0)),
                  pl.BlockSpec((2, SC_PIECES, tm, SC_SUB), lambda i: (0, 0, i, 0)),
                  pl.BlockSpec((tm, LANES), lambda i: (i, 0)),
                  pl.BlockSpec(mod_l.shape, lambda i: (0, 0))],
        out_specs=pl.BlockSpec((tm, D_MODEL), lambda i: (i, 0)),
        out_shape=jax.ShapeDtypeStruct((n, D_MODEL), F32),
        compiler_params=_cparams(("arbitrary",)),
        name="moe_combine",
    )(x1, y2, route, mod_l)


def _moe(h2, route, x1, mod_l, wg, wu, wd, layer, seq):
    n = h2.shape[1]
    n_tiles = (2 * n) // MOE_TM + N_EXPERTS
    rows = n_tiles * MOE_TM
    rk, cnt = _expert_ranks(route)
    pos, tmap = _expert_plan(cnt, route, rk, n_tiles)
    idx = jnp.concatenate([_piece_row_index(pos[:, 0], rows), _piece_row_index(pos[:, 1], rows)])
    xs = _sc_scatter_rows(h2.reshape(SC_PIECES * n, SC_SUB), idx, SC_PIECES * rows)
    ys = _grouped_experts(tmap[:n_tiles, 0], tmap[:n_tiles, 1], xs.reshape(SC_PIECES, rows, SC_SUB),
                          wg, wu, wd, layer)
    y2 = _sc_gather_rows(ys.reshape(SC_PIECES * rows, SC_SUB), idx).reshape(2, SC_PIECES, n, SC_SUB)
    return _combine(x1, y2, route, mod_l, seq)


def _final_norm_kernel(x_ref, g_ref, o_ref):
    x = x_ref[...]
    ms = jnp.mean(x * x, axis=-1, keepdims=True)
    o_ref[...] = x * lax.rsqrt(ms + EPS) * g_ref[...]


def _final_norm(x2, g, seq):
    n = x2.shape[0]
    tm = min(1024, seq)
    return pl.pallas_call(
        _final_norm_kernel,
        grid=(n // tm,),
        in_specs=[pl.BlockSpec((tm, D_MODEL), lambda i: (i, 0)),
                  pl.BlockSpec((1, D_MODEL), lambda i: (0, 0))],
        out_specs=pl.BlockSpec((tm, D_MODEL), lambda i: (i, 0)),
        out_shape=jax.ShapeDtypeStruct((n, D_MODEL), F32),
        compiler_params=_cparams(("arbitrary",)),
        name="final_norm",
    )(x2, g.reshape(1, D_MODEL))


_IN_OFFS = [sum(IN_SPLITS[:i]) for i in range(len(IN_SPLITS) + 1)]
(_AQ, _AK, _AV, _IQ, _IK, _IW, _BQ, _BF, _BI, _BG, _CQ, _CK, _CV, _CG, _GA, _GB, _GC) = range(len(IN_SPLITS))


def _pack_kernel(wt_ref, w16_ref, w32_ref, wvt_ref, wqt_ref):
    def rows_of(seg):
        return wt_ref[0, _IN_OFFS[seg]:_IN_OFFS[seg + 1], :]

    def padded_t(seg):
        v = rows_of(seg)
        v = jnp.concatenate([v, jnp.zeros((LANES - v.shape[0], v.shape[1]), F32)], axis=0)
        return v.T.astype(BF16)

    scale = {_AQ: A_HEAD_DIM ** -0.5 * LOG2_E, _CK: C_QK_DIM ** -0.5}
    at = 0
    for seg in (_CV, _CG, _AK, _BQ, _BI, _BG, _CQ, _CK, _GA, _GB, _GC):
        v = rows_of(seg)
        if seg in scale:
            v = v * scale[seg]
        for c0 in range(0, v.shape[0], 1024):
            w16_ref[0, :, at + c0:at + c0 + 1024] = v[c0:c0 + 1024].T.astype(BF16)
        at += v.shape[0]

    w32_ref[0, :, P32_BF:P32_BF + B_WIDTH] = rows_of(_BF).T.astype(BF16)
    w32_ref[0, :, P32_IK:P32_IK + LANES] = padded_t(_IK)
    w32_ref[0, :, P32_IW:P32_IW + LANES] = padded_t(_IW)
    wvt_ref[0] = rows_of(_AV).astype(BF16)
    wqt_ref[0, 0:A_WIDTH, :] = (rows_of(_AQ) * scale[_AQ]).astype(BF16)
    wqt_ref[0, A_WIDTH:, :] = rows_of(_IQ).astype(BF16)


def _pack_w_in(w_in):
    depth, d, width = w_in.shape
    rows = LANES
    w16_width = 13 * 1024
    qiq_rows = A_WIDTH + IDX_HEADS * IDX_DIM
    w32_width = P32_IW + LANES
    w_in_t = jnp.swapaxes(w_in, 1, 2)
    return pl.pallas_call(
        _pack_kernel,
        grid=(depth, d // rows),
        in_specs=[pl.BlockSpec((1, width, rows), lambda l, r: (l, 0, r))],
        out_specs=[pl.BlockSpec((1, rows, w16_width), lambda l, r: (l, r, 0)),
                   pl.BlockSpec((1, rows, w32_width), lambda l, r: (l, r, 0)),
                   pl.BlockSpec((1, A_WIDTH, rows), lambda l, r: (l, 0, r)),
                   pl.BlockSpec((1, qiq_rows, rows), lambda l, r: (l, 0, r))],
        out_shape=[jax.ShapeDtypeStruct((depth, d, w16_width), BF16),
                   jax.ShapeDtypeStruct((depth, d, w32_width), BF16),
                   jax.ShapeDtypeStruct((depth, A_WIDTH, d), BF16),
                   jax.ShapeDtypeStruct((depth, qiq_rows, d), BF16)],
        compiler_params=_cparams(("arbitrary", "arbitrary")),
        name="pack_w_in",
    )(w_in_t)


def _split_bf16(w):
    hi = w.astype(BF16)
    return hi, (w - hi.astype(F32)).astype(BF16)


def kernel(x, c, rel_bias, hgrn_lb_raw, norm1_g, norm2_g, ada_w, ada_b, w_in, hgrn_norm_g, w_branch_a,
           w_branch_b, w_branch_c, w_out, router_group_w, router_group_b, router_expert_w,
           router_expert_b, expert_w_gate, expert_w_up, expert_w_down, final_norm_g):
    bsz, seq, _ = x.shape
    depth = w_in.shape[0]
    n = bsz * seq
    x2 = x.reshape(n, D_MODEL)
    tq = min(DSA_TQ, seq)

    lb_all = _hgrn_lower_bounds(hgrn_lb_raw)
    c_pad = jnp.pad(c, ((0, (-bsz) % 8), (0, 0)))
    mod = _ada_mod(c_pad, ada_w, ada_b)
    bias_tiles = _bias_tiles(rel_bias, tq)
    ret_tables = _retention_tables(seq)
    w16_all, w32_all, wvt_all, wqt_all = _pack_w_in(w_in)

    for l in range(depth):
        p16, p32, vt, qiq_t = _project_all(x2, mod[l], norm1_g[l], w16_all, w32_all, wvt_all, wqt_all,
                                           l, tq, seq)
        o_a = _dsa_attention(qiq_t, p16, p32, vt, bias_tiles, bsz, seq)
        o_b = _hgrn2(p16, p32, lb_all[l], hgrn_norm_g[l], bsz, seq)
        o_c = _retention(p16, ret_tables, bsz, seq)
        wr = jnp.concatenate([router_group_w[l], router_expert_w[l],
                              jnp.zeros((D_MODEL, LANES - N_GROUPS - N_EXPERTS), F32)], axis=1)
        br = jnp.concatenate([router_group_b[l], router_expert_b[l],
                              jnp.zeros((LANES - N_GROUPS - N_EXPERTS,), F32)]).reshape(1, LANES)
        wr_hi, wr_lo = _split_bf16(wr)
        x1, h2, route = _merge(o_a, o_b, o_c, p16, x2, mod[l], norm2_g[l],
                               w_branch_a[l].astype(BF16), w_branch_b[l].astype(BF16),
                               w_branch_c[l].astype(BF16), w_out[l].astype(BF16),
                               wr_hi, wr_lo, br, seq)
        x2 = _moe(h2, route, x1, mod[l], expert_w_gate, expert_w_up, expert_w_down, l, seq)

    return _final_norm(x2, final_norm_g, seq).reshape(bsz, seq, D_MODEL)
```

```python
import functools
import math

import jax
import jax.numpy as jnp
from jax import lax
from jax.experimental import pallas as pl
from jax.experimental.pallas import tpu as pltpu
from jax.experimental.pallas import tpu_sc as plsc

F32 = jnp.float32
BF16 = jnp.bfloat16

D_MODEL = 1024
A_HEADS = 8
A_HEAD_DIM = 128
IDX_HEADS = 8
IDX_DIM = 64
TOPK_MAX = 256
REL_BUCKETS = 32
REL_MAX_DIST = 128
B_HEADS = 8
B_HEAD_DIM = 128
C_HEADS = 4
C_QK_DIM = 256
C_V_DIM = 512
N_GROUPS = 4
EXPERTS_PER_GROUP = 8
N_EXPERTS = 32
D_EXPERT = 512
EPS = 1e-6

A_WIDTH = A_HEADS * A_HEAD_DIM
B_WIDTH = B_HEADS * B_HEAD_DIM
C_QK_WIDTH = C_HEADS * C_QK_DIM
C_V_WIDTH = C_HEADS * C_V_DIM
IN_SPLITS = (A_WIDTH, A_WIDTH, A_WIDTH, IDX_HEADS * IDX_DIM, IDX_DIM, IDX_HEADS,
             B_WIDTH, B_WIDTH, B_WIDTH, B_WIDTH,
             C_QK_WIDTH, C_QK_WIDTH, C_V_WIDTH, C_V_WIDTH,
             D_MODEL, D_MODEL, D_MODEL)

LANES = 128
BF16_ROWS = 16
VMEM_LIMIT = 56 * 1024 * 1024

P16_CV, P16_CG = 0, 2
P16_AK, P16_BQ, P16_BI, P16_BG, P16_CQ, P16_CK, P16_GA, P16_GB, P16_GC = range(4, 13)
P32_BF = 0
P32_IK = 1024
P32_IW = 1152

DSA_TQ = 256
HGRN_L = 256
HGRN_C = 64
HGRN_SB = 16
HGRN_MAX_LOG_DECAY = 80.0
RET_C = 256
KEY_NEG_INF = -2139095041
HALF_BIAS = 32768
MASK_NEG = -1e30
LOG2_E = math.log2(math.e)
COUNT_CHAINS = 4
MOE_TM = 512
PLAN_TB = 1024
SC_WINDOW = 128
SC_SUB = 256
SC_PIECES = D_MODEL // 2 // SC_SUB

NT_DIMS = (((1,), (1,)), ((), ()))
TN_DIMS = (((0,), (0,)), ((), ()))


def _cparams(sem):
    return pltpu.CompilerParams(dimension_semantics=sem, vmem_limit_bytes=VMEM_LIMIT)


def _silu(x):
    return x * jax.nn.sigmoid(x)


def _pack_bf16_pairs(x):
    k = x.shape[1] // 2
    bits = pltpu.bitcast(x.astype(BF16).astype(F32), jnp.int32)
    return (bits[:, :k] & jnp.int32(-65536)) | lax.shift_right_logical(bits[:, k:], 16)


def _unpack_bf16_pairs(words):
    hi = pltpu.bitcast(words & jnp.int32(-65536), F32)
    lo = pltpu.bitcast(lax.shift_left(words, 16), F32)
    return jnp.concatenate([hi, lo], axis=1)


def _lb_kernel(raw_ref, o_ref):
    raw = raw_ref[...]
    m = jnp.max(raw, axis=0, keepdims=True)
    e = jnp.exp(raw - m)
    soft = e / jnp.sum(e, axis=0, keepdims=True)
    run = jnp.zeros_like(soft[0:1])
    for l in range(raw.shape[0]):
        run = run + soft[l:l + 1]
        o_ref[l:l + 1, :] = run - soft[0:1]


def _hgrn_lower_bounds(raw):
    return pl.pallas_call(
        _lb_kernel, out_shape=jax.ShapeDtypeStruct(raw.shape, F32), name="hgrn_lb")(raw)


def _ada_kernel(c_ref, w_ref, b_ref, o_ref):
    a = _silu(c_ref[...])
    o_ref[0] = jnp.dot(a, w_ref[0], precision=lax.Precision.HIGHEST,
                       preferred_element_type=F32) + b_ref[0]


def _ada_mod(c_pad, ada_w, ada_b):
    depth = ada_w.shape[0]
    rows = c_pad.shape[0]
    return pl.pallas_call(
        _ada_kernel,
        grid=(depth, 6),
        in_specs=[pl.BlockSpec((rows, D_MODEL), lambda l, j: (0, 0)),
                  pl.BlockSpec((1, D_MODEL, D_MODEL), lambda l, j: (l, 0, j)),
                  pl.BlockSpec((1, 1, D_MODEL), lambda l, j: (l, 0, j))],
        out_specs=pl.BlockSpec((1, rows, D_MODEL), lambda l, j: (l, 0, j)),
        out_shape=jax.ShapeDtypeStruct((depth, rows, 6 * D_MODEL), F32),
        compiler_params=_cparams(("arbitrary", "arbitrary")),
        name="ada_mod",
    )(c_pad, ada_w, ada_b.reshape(depth, 1, 6 * D_MODEL))


def _rms_mod(x, g, sc, sh):
    ms = jnp.mean(x * x, axis=-1, keepdims=True)
    return (x * lax.rsqrt(ms + EPS) * g) * (1.0 + sc) + sh


def _norm1(x_ref, mod_ref, g_ref, b):
    sh = mod_ref[pl.ds(b, 1), 0:D_MODEL]
    sc = mod_ref[pl.ds(b, 1), D_MODEL:2 * D_MODEL]
    return _rms_mod(x_ref[...], g_ref[...], sc, sh).astype(BF16)


def _proj_all_kernel(x_ref, mod_ref, g_ref, w16_ref, w32_ref, wvt_ref, wqt_ref,
                     o16_ref, o32_ref, ovt_ref, oqt_ref, h_ref, *, tiles_per_batch, n16, n32, chunk):
    j = pl.program_id(1)

    @pl.when(j == 0)
    def _():
        h_ref[...] = _norm1(x_ref, mod_ref, g_ref, pl.program_id(0) // tiles_per_batch)

    @pl.when(j < n16)
    def _():
        o16_ref[...] = jnp.dot(h_ref[...], w16_ref[...], preferred_element_type=F32).astype(o16_ref.dtype)

    @pl.when((j >= n16) & (j < n16 + n32))
    def _():
        o32_ref[...] = jnp.dot(h_ref[...], w32_ref[...], preferred_element_type=F32)

    def transposed(wt_ref, o_ref):
        res = lax.dot_general(wt_ref[...], h_ref[...], NT_DIMS, preferred_element_type=F32)
        for ci in range(o_ref.shape[0]):
            o_ref[ci] = res[:, ci * chunk:(ci + 1) * chunk].astype(o_ref.dtype)

    @pl.when(j == n16 + n32)
    def _():
        transposed(wvt_ref, ovt_ref)

    @pl.when(j == n16 + n32 + 1)
    def _():
        transposed(wqt_ref, oqt_ref)


def _project_all(x2, mod_l, g, w16_all, w32_all, wvt_all, wqt_all, layer, chunk, seq):
    n = x2.shape[0]
    tm = min(1024, seq)
    t16, t32 = 1024, 640
    n16, n32 = w16_all.shape[2] // t16, w32_all.shape[2] // t32
    vt_rows, qt_rows = wvt_all.shape[1], wqt_all.shape[1]
    one = pl.Buffered(1)

    def c16(j):
        return jnp.minimum(j, n16 - 1)

    def c32(j):
        return jnp.clip(j - n16, 0, n32 - 1)

    return pl.pallas_call(
        functools.partial(_proj_all_kernel, tiles_per_batch=seq // tm, n16=n16, n32=n32, chunk=chunk),
        grid=(n // tm, n16 + n32 + 2),
        in_specs=[pl.BlockSpec((tm, D_MODEL), lambda i, j: (i, 0)),
                  pl.BlockSpec(mod_l.shape, lambda i, j: (0, 0)),
                  pl.BlockSpec((1, D_MODEL), lambda i, j: (0, 0)),
                  pl.BlockSpec((None, D_MODEL, t16), lambda i, j: (layer, 0, c16(j))),
                  pl.BlockSpec((None, D_MODEL, t32), lambda i, j: (layer, 0, c32(j))),
                  pl.BlockSpec((None, vt_rows, D_MODEL), lambda i, j: (layer, 0, 0), pipeline_mode=one),
                  pl.BlockSpec((None, qt_rows, D_MODEL), lambda i, j: (layer, 0, 0), pipeline_mode=one)],
        out_specs=[pl.BlockSpec((tm, t16), lambda i, j: (i, c16(j))),
                   pl.BlockSpec((tm, t32), lambda i, j: (i, c32(j))),
                   pl.BlockSpec((tm // chunk, vt_rows, chunk), lambda i, j: (i, 0, 0)),
                   pl.BlockSpec((tm // chunk, qt_rows, chunk), lambda i, j: (i, 0, 0))],
        out_shape=[jax.ShapeDtypeStruct((n, w16_all.shape[2]), BF16),
                   jax.ShapeDtypeStruct((n, w32_all.shape[2]), F32),
                   jax.ShapeDtypeStruct((n // chunk, vt_rows, chunk), BF16),
                   jax.ShapeDtypeStruct((n // chunk, qt_rows, chunk), BF16)],
        scratch_shapes=[pltpu.VMEM((tm, D_MODEL), BF16)],
        compiler_params=_cparams(("arbitrary", "arbitrary")),
        name="proj_all",
    )(x2, mod_l, g.reshape(1, D_MODEL), w16_all, w32_all, wvt_all, wqt_all)


def _dsa_kernel(qiq_ref, iw_ref, k_ref, vt_ref, ik_ref, bias_ref, o_ref,
                key_ref, hi_ref, lo_ref, madd_ref, iwt_ref, m_ref, l_ref, acc_ref, s_ref, *, tq, topk):
    qi = pl.program_id(1)
    nck = qi + 1
    idx_scale = (IDX_HEADS * IDX_DIM) ** -0.5

    iwt_ref[...] = (iw_ref[...] * idx_scale).T

    krow = lax.broadcasted_iota(jnp.int32, (tq, tq), 0)
    qcol = lax.broadcasted_iota(jnp.int32, (tq, tq), 1)

    def score_chunk(c, carry):
        off = pl.multiple_of(c * tq, tq)
        ikc = ik_ref[pl.ds(off, tq), :].astype(BF16)[:, :IDX_DIM]
        acc = jnp.zeros((tq, tq), F32)
        for h in range(IDX_HEADS):
            s = jnp.dot(ikc, qiq_ref[0, A_WIDTH + h * IDX_DIM:A_WIDTH + (h + 1) * IDX_DIM, :],
                        preferred_element_type=F32)
            acc = acc + jnp.maximum(s, 0.0) * iwt_ref[h:h + 1, :]
        acc = jnp.where(acc == 0.0, 0.0, acc)
        acc = jnp.where(krow + (c - qi) * tq <= qcol, acc, -jnp.inf)
        kb = pltpu.bitcast(acc, jnp.int32)
        key = jnp.where(kb < 0, kb ^ jnp.int32(0x7FFFFFFF), kb)
        key_ref[c] = key
        hi_ref[c] = jnp.right_shift(key, 16).astype(jnp.int16)
        lo_ref[c] = ((key & 0xFFFF) - HALF_BIAS).astype(jnp.int16)
        return carry

    def paired_loop(count_, fn):
        def pair(i, carry):
            fn(2 * i, carry)
            fn(2 * i + 1, carry)
            return carry

        lax.fori_loop(0, count_ // 2, pair, 0)

        @pl.when(count_ % 2 == 1)
        def _():
            fn(count_ - 1, 0)

    paired_loop(nck, score_chunk)

    half_min = jnp.int16(-HALF_BIAS)

    @pl.when(nck % 2 == 1)
    def _():
        hi_ref[nck] = jnp.full((tq, tq), half_min)
        lo_ref[nck] = jnp.full((tq, tq), half_min)

    def add_hits(parts, hit, rows):
        parts = list(parts)
        for r in range(tq // rows):
            parts[r % COUNT_CHAINS] = parts[r % COUNT_CHAINS] + hit[r * rows:(r + 1) * rows, :]
        return tuple(parts)

    def total(parts):
        return jnp.sum(sum(p.astype(F32) for p in parts), axis=0, keepdims=True)

    def count16(ref, pred_fn, visit=None):
        def body(i, parts):
            for c in (2 * i, 2 * i + 1):
                chunk = ref[c]
                if visit is not None:
                    visit(c, chunk)
                parts = add_hits(parts, jnp.where(pred_fn(chunk), jnp.int16(1), jnp.int16(0)), BF16_ROWS)
            return parts

        return total(lax.fori_loop(0, (nck + 1) // 2, body,
                                   (jnp.zeros((BF16_ROWS, tq), jnp.int16),) * COUNT_CHAINS))

    def count32(pred_fn):
        def body(c, parts):
            return add_hits(parts, jnp.where(pred_fn(key_ref[c]), 1.0, 0.0), 8)

        return total(lax.fori_loop(0, nck, body, (jnp.zeros((8, tq), F32),) * COUNT_CHAINS))

    def bisect16(ref, target):
        def bit_step(i, theta):
            cand = theta + jnp.left_shift(jnp.int32(1), 15 - i)
            cand16 = cand.astype(jnp.int16)
            return jnp.where(count16(ref, lambda k: k >= cand16) >= target, cand, theta)

        return lax.fori_loop(0, 16, bit_step, jnp.full((1, tq), -HALF_BIAS, jnp.int32))

    theta_hi = bisect16(hi_ref, float(topk))
    theta_hi16 = theta_hi.astype(jnp.int16)

    def keep_bucket(c, hi_chunk):
        lo_ref[c] = jnp.where(hi_chunk == theta_hi16, lo_ref[c], half_min)

    need_lo = topk - count16(hi_ref, lambda k: k > theta_hi16, visit=keep_bucket)
    theta_lo = bisect16(lo_ref, need_lo)
    theta = theta_hi * (2 * HALF_BIAS) + (theta_lo + HALF_BIAS)
    theta = jnp.maximum(theta, KEY_NEG_INF + 1)

    def mask_chunk(c, cnt):
        ge = key_ref[c] >= theta
        madd_ref[c] = jnp.where(ge, 0.0, MASK_NEG)
        return cnt + jnp.sum(jnp.where(ge, 1.0, 0.0), axis=0, keepdims=True)

    cnt_ge = lax.fori_loop(0, nck, mask_chunk, jnp.zeros((1, tq), F32))

    @pl.when(jnp.max(cnt_ge) > topk)
    def _():
        need_eq = topk - count32(lambda kc: kc > theta)
        incl = jnp.where(krow >= qcol, 1.0, 0.0).astype(BF16)

        def tie_chunk(c, run):
            kc = key_ref[c]
            eq = kc == theta
            eqf = jnp.where(eq, 1.0, 0.0)
            pref = jnp.dot(incl, eqf.astype(BF16), preferred_element_type=F32) + run
            eq_add = jnp.where(pref <= need_eq, 0.0, MASK_NEG)
            madd_ref[c] = jnp.where(eq, eq_add, jnp.where(kc > theta, 0.0, MASK_NEG))
            return run + jnp.sum(eqf, axis=0, keepdims=True)

        lax.fori_loop(0, nck, tie_chunk, jnp.zeros((1, tq), F32))

    m_ref[...] = jnp.full(m_ref.shape, -jnp.inf, F32)
    l_ref[...] = jnp.zeros(l_ref.shape, F32)
    acc_ref[...] = jnp.zeros(acc_ref.shape, F32)

    ones_rows = jnp.ones((BF16_ROWS, tq), BF16)

    head_slices = [slice(h * A_HEAD_DIM, (h + 1) * A_HEAD_DIM) for h in range(A_HEADS)]

    def logits(c, h):
        off = pl.multiple_of(c * tq, tq)
        s_ref[h] = jnp.dot(k_ref[pl.ds(off, tq), head_slices[h]], qiq_ref[0, head_slices[h], :],
                           preferred_element_type=F32)

    def attend(c, h, lag):
        hs = head_slices[h]
        s = s_ref[h] + madd_ref[c]
        if lag is not None:
            s = s + bias_ref[h, lag]
        m_old = m_ref[h]
        m_new = jnp.maximum(m_old, jnp.max(s, axis=0, keepdims=True))
        alpha = jnp.exp2(m_old - m_new)
        p = jnp.exp2(s - m_new).astype(BF16)
        pv = jnp.dot(jnp.concatenate([vt_ref[c, hs, :], ones_rows], axis=0), p,
                     preferred_element_type=F32)
        l_ref[h] = alpha * l_ref[h] + pv[A_HEAD_DIM:A_HEAD_DIM + 1]
        acc_ref[h] = alpha * acc_ref[h] + pv[:A_HEAD_DIM]
        m_ref[h] = m_new

    def step(c, lag, prefetch):
        for h in range(A_HEADS):
            attend(c, h, lag)
            if prefetch:
                logits(c + 1, h)

    for h in range(A_HEADS):
        logits(0, h)

    def far_chunk(c, carry):
        step(c, None, True)
        return carry

    paired_loop(jnp.maximum(qi - 1, 0), far_chunk)

    @pl.when(qi >= 1)
    def _():
        step(qi - 1, 1, True)

    step(qi, 0, False)

    for h in range(A_HEADS):
        o = acc_ref[h] * (1.0 / l_ref[h])
        o_ref[:, h * A_HEAD_DIM:(h + 1) * A_HEAD_DIM] = o.T.astype(o_ref.dtype)


def _dsa_attention(qiq_t, p16, p32, vt, bias_tiles, bsz, seq):
    tq = min(DSA_TQ, seq)
    nq = seq // tq
    topk = min(TOPK_MAX, seq // 4)
    n = bsz * seq
    one = pl.Buffered(1)
    return pl.pallas_call(
        functools.partial(_dsa_kernel, tq=tq, topk=topk),
        grid=(bsz, nq),
        in_specs=[
            pl.BlockSpec((1, qiq_t.shape[1], tq), lambda b, i: (b * nq + i, 0, 0)),
            pl.BlockSpec((tq, LANES), lambda b, i: (b * nq + i, P32_IW // LANES)),
            pl.BlockSpec((seq, A_WIDTH), lambda b, i: (b, P16_AK), pipeline_mode=one),
            pl.BlockSpec((nq, A_WIDTH, tq), lambda b, i: (b, 0, 0), pipeline_mode=one),
            pl.BlockSpec((seq, LANES), lambda b, i: (b, P32_IK // LANES), pipeline_mode=one),
            pl.BlockSpec(bias_tiles.shape, lambda b, i: (0, 0, 0, 0), pipeline_mode=one),
        ],
        out_specs=pl.BlockSpec((tq, A_WIDTH), lambda b, i: (b * nq + i, 0)),
        out_shape=jax.ShapeDtypeStruct((n, A_WIDTH), BF16),
        scratch_shapes=[pltpu.VMEM((nq, tq, tq), jnp.int32),
                        pltpu.VMEM((nq + 1, tq, tq), jnp.int16),
                        pltpu.VMEM((nq + 1, tq, tq), jnp.int16),
                        pltpu.VMEM((nq, tq, tq), F32),
                        pltpu.VMEM((LANES, tq), F32),
                        pltpu.VMEM((A_HEADS, 1, tq), F32),
                        pltpu.VMEM((A_HEADS, 1, tq), F32),
                        pltpu.VMEM((A_HEADS, A_HEAD_DIM, tq), F32),
                        pltpu.VMEM((A_HEADS, tq, tq), F32)],
        compiler_params=_cparams(("arbitrary", "arbitrary")),
        name="dsa_attention",
    )(qiq_t, p32, p16, vt, p32, bias_tiles)


def _t5_bucket(rel):
    max_exact = REL_BUCKETS // 2
    relf = jnp.maximum(rel, 1).astype(F32)
    large = max_exact + (jnp.log(relf / max_exact) / math.log(REL_MAX_DIST / max_exact)
                         * (REL_BUCKETS - max_exact)).astype(jnp.int32)
    large = jnp.minimum(large, REL_BUCKETS - 1)
    return jnp.where(rel < max_exact, rel, large)


def _bias_tiles(rel_bias, tq):
    assert tq >= REL_MAX_DIST
    key = jnp.arange(tq, dtype=jnp.int32)[:, None]
    qry = jnp.arange(tq, dtype=jnp.int32)[None, :]
    bucket = jnp.stack([_t5_bucket(jnp.maximum(lag * tq + qry - key, 0)) for lag in range(2)])
    rel = ((rel_bias - rel_bias[REL_BUCKETS - 1:REL_BUCKETS]) * LOG2_E).astype(F32)
    onehot = bucket[None] == jnp.arange(REL_BUCKETS, dtype=jnp.int32)[:, None, None, None]
    return jnp.sum(jnp.where(onehot[:, None], rel[:, :, None, None, None], 0.0), axis=0)


def _hgrn_kernel(q_ref, f_ref, i_ref, g_ref, lb_ref, ng_ref, tril_ref, o_ref, st_ref, attn_ref, stage_ref,
                 *, rows):
    @pl.when(pl.program_id(1) == 0)
    def _():
        st_ref[...] = jnp.zeros_like(st_ref)

    lb = lb_ref[...]
    f = lb + (1.0 - lb) * jax.nn.sigmoid(f_ref[...])
    logf = jnp.log(f)
    kk = 1.0 - f
    g1 = logf.astype(BF16)
    r1 = logf - g1.astype(F32)
    g2 = r1.astype(BF16)
    g3 = (r1 - g2.astype(F32)).astype(BF16)
    tril = tril_ref[...]
    bcum = (jnp.dot(tril, g1, preferred_element_type=F32)
            + jnp.dot(tril, g2, preferred_element_type=F32)
            + jnp.dot(tril, g3, preferred_element_type=F32))

    srow = lax.broadcasted_iota(jnp.int32, (HGRN_C, B_HEAD_DIM), 0)
    trow = lax.broadcasted_iota(jnp.int32, (HGRN_SB, HGRN_C), 0)
    scol = lax.broadcasted_iota(jnp.int32, (HGRN_SB, HGRN_C), 1)
    ng = ng_ref[...]
    q = q_ref[...].astype(F32)
    qb_all = (q * jnp.exp(bcum)).astype(BF16)

    tiles = [(n, h) for n in range(rows // HGRN_C) for h in range(B_HEADS)]

    def rs(n):
        return slice(n * HGRN_C, (n + 1) * HGRN_C)

    def hs(h):
        return slice(h * B_HEAD_DIM, (h + 1) * B_HEAD_DIM)

    worst = None
    for j in range(rows // HGRN_SB):
        r0 = j * HGRN_SB
        span = bcum[r0 + HGRN_SB - 1:r0 + HGRN_SB]
        if r0 % HGRN_C:
            span = span - bcum[r0 - 1:r0]
        worst = span if worst is None else jnp.minimum(worst, span)
    in_range = jnp.min(worst) > -HGRN_MAX_LOG_DECAY

    @pl.when(in_range)
    def _():
        a_parts = {}
        for n, h in tiles:
            bc, qc, kc = bcum[rs(n), hs(h)], q[rs(n), hs(h)], kk[rs(n), hs(h)]
            for sb in range(HGRN_C // HGRN_SB):
                s0 = sb * HGRN_SB
                beta = bc[s0 - 1:s0] if sb > 0 else jnp.zeros((1, B_HEAD_DIM), F32)
                qs = (qc[s0:s0 + HGRN_SB] * jnp.exp(bc[s0:s0 + HGRN_SB] - beta)).astype(BF16)
                live = s0 + HGRN_SB
                ks = (kc[:live] * jnp.exp(beta - bc[:live])).astype(BF16)
                if live < HGRN_C:
                    ks = jnp.concatenate([ks, jnp.zeros((HGRN_C - live, B_HEAD_DIM), BF16)], axis=0)
                a_parts[n, h, sb] = lax.dot_general(qs, ks, NT_DIMS, preferred_element_type=F32)
        for ti, (n, h) in enumerate(tiles):
            a_rows = [jnp.where(scol <= trow + sb * HGRN_SB, a_parts[n, h, sb], 0.0)
                      for sb in range(HGRN_C // HGRN_SB)]
            attn_ref[ti] = jnp.concatenate(a_rows, axis=0)

    @pl.when(jnp.logical_not(in_range))
    def _():
        for h in range(B_HEADS):
            stage_ref[0, h] = bcum[:, hs(h)]
            stage_ref[1, h] = q[:, hs(h)]
            stage_ref[2, h] = kk[:, hs(h)]
        t_idx = lax.broadcasted_iota(jnp.int32, (HGRN_C, HGRN_C), 0)
        s_idx = lax.broadcasted_iota(jnp.int32, (HGRN_C, HGRN_C), 1)

        def safe_tile(ti, carry):
            n, h = ti // B_HEADS, ti % B_HEADS
            r0 = pl.multiple_of(n * HGRN_C, HGRN_C)
            bc = stage_ref[0, h, pl.ds(r0, HGRN_C), :]
            qc = stage_ref[1, h, pl.ds(r0, HGRN_C), :]
            kc = stage_ref[2, h, pl.ds(r0, HGRN_C), :]
            acc = jnp.where(t_idx == s_idx, jnp.sum(qc * kc, axis=-1, keepdims=True), 0.0)
            block = HGRN_C
            while block >= 2:
                half = block // 2
                ref_row = (t_idx & -block) + (half - 1)
                bref = jnp.dot(jnp.where(s_idx == ref_row, 1.0, 0.0), bc, precision=lax.Precision.HIGHEST,
                               preferred_element_type=F32)
                second = (srow & (block - 1)) >= half
                qs = jnp.where(second, qc * jnp.exp(jnp.where(second, bc - bref, 0.0)), 0.0).astype(BF16)
                ks = jnp.where(second, 0.0, kc * jnp.exp(jnp.where(second, 0.0, bref - bc))).astype(BF16)
                a = lax.dot_general(qs, ks, NT_DIMS, preferred_element_type=F32)
                acc = acc + jnp.where((t_idx & -block) == (s_idx & -block), a, 0.0)
                block = half
            attn_ref[ti] = acc
            return carry

        lax.fori_loop(0, len(tiles), safe_tile, 0)

    intra, upd, dec = {}, {}, {}
    for ti, (n, h) in enumerate(tiles):
        attn = attn_ref[ti].astype(BF16)
        vc = i_ref[rs(n), hs(h)]
        intra[n, h] = jnp.dot(attn, vc, preferred_element_type=F32)
        bc = bcum[rs(n), hs(h)]
        blast = bc[HGRN_C - 1:HGRN_C]
        kdec = (kk[rs(n), hs(h)] * jnp.exp(blast - bc)).astype(BF16)
        upd[n, h] = lax.dot_general(vc, kdec, TN_DIMS, preferred_element_type=F32)
        dec[n, h] = jnp.exp(blast)
    for n, h in tiles:
        st = st_ref[h]
        o = intra[n, h] + lax.dot_general(qb_all[rs(n), hs(h)], st.astype(BF16), NT_DIMS,
                                          preferred_element_type=F32)
        st_ref[h] = st * dec[n, h] + upd[n, h]
        ms = jnp.mean(o * o, axis=-1, keepdims=True)
        on = o * lax.rsqrt(ms + EPS) * ng
        o_ref[rs(n), hs(h)] = (on * _silu(g_ref[rs(n), hs(h)].astype(F32))).astype(o_ref.dtype)


def _hgrn2(p16, p32, lb_l, norm_g, bsz, seq):
    rows = min(HGRN_L, seq)
    nj = seq // rows
    n = bsz * seq
    r = jnp.arange(rows, dtype=jnp.int32)
    tril = ((r[:, None] >= r[None, :]) & (r[:, None] // HGRN_C == r[None, :] // HGRN_C)).astype(BF16)

    def col(base):
        return lambda b, j: (b * nj + j, base)

    return pl.pallas_call(
        functools.partial(_hgrn_kernel, rows=rows),
        grid=(bsz, nj),
        in_specs=[
            pl.BlockSpec((rows, B_WIDTH), col(P16_BQ)),
            pl.BlockSpec((rows, B_WIDTH), col(P32_BF // B_WIDTH)),
            pl.BlockSpec((rows, B_WIDTH), col(P16_BI)),
            pl.BlockSpec((rows, B_WIDTH), col(P16_BG)),
            pl.BlockSpec((1, B_WIDTH), lambda b, j: (0, 0)),
            pl.BlockSpec((1, B_HEAD_DIM), lambda b, j: (0, 0)),
            pl.BlockSpec((rows, rows), lambda b, j: (0, 0)),
        ],
        out_specs=pl.BlockSpec((rows, B_WIDTH), lambda b, j: (b * nj + j, 0)),
        out_shape=jax.ShapeDtypeStruct((n, B_WIDTH), BF16),
        scratch_shapes=[pltpu.VMEM((B_HEADS, B_HEAD_DIM, B_HEAD_DIM), F32),
                        pltpu.VMEM((rows // HGRN_C * B_HEADS, HGRN_C, HGRN_C), F32),
                        pltpu.VMEM((3, B_HEADS, rows, B_HEAD_DIM), F32)],
        compiler_params=_cparams(("arbitrary", "arbitrary")),
        name="hgrn2",
    )(p16, p32, p16, p16, lb_l.reshape(1, B_WIDTH), norm_g.reshape(1, B_HEAD_DIM), tril)


def _ret_kernel(q_ref, k_ref, v_ref, g_ref, cos_ref, sin_ref, idec_ref, qdec_ref, kdec_ref, cdec_ref,
                o_ref, st_ref):
    @pl.when(pl.program_id(1) == 0)
    def _():
        st_ref[...] = jnp.zeros_like(st_ref)

    heads = range(C_HEADS)
    cos = jnp.concatenate([cos_ref[...]] * C_HEADS, axis=1)
    sin_signed = jnp.concatenate([sin_ref[...]] * C_HEADS, axis=1)
    even = lax.broadcasted_iota(jnp.int32, cos.shape, 1) % 2 == 0

    def rot(a):
        swapped = jnp.where(even, pltpu.roll(a, C_QK_WIDTH - 1, 1), pltpu.roll(a, 1, 1))
        return a * cos + swapped * sin_signed

    qr = rot(q_ref[...].astype(F32))
    kr = rot(k_ref[...].astype(F32))
    qk = [slice(h * C_QK_DIM, (h + 1) * C_QK_DIM) for h in heads]
    vs = [slice(h * C_V_DIM, (h + 1) * C_V_DIM) for h in heads]
    attn = [lax.dot_general(qr[:, qk[h]].astype(BF16), kr[:, qk[h]].astype(BF16), NT_DIMS,
                            preferred_element_type=F32) * idec_ref[h] for h in heads]
    inter = [jnp.dot((qr[:, qk[h]] * qdec_ref[h]).astype(BF16), st_ref[h].astype(BF16),
                     preferred_element_type=F32) for h in heads]
    intra = [jnp.dot(attn[h].astype(BF16), v_ref[:, vs[h]], preferred_element_type=F32) for h in heads]
    upd = [jnp.dot((kr[:, qk[h]] * kdec_ref[h]).T.astype(BF16), v_ref[:, vs[h]],
                   preferred_element_type=F32) for h in heads]
    for h in heads:
        st_ref[h] = cdec_ref[h, 0:1, :] * st_ref[h] + upd[h]
        o = intra[h] + inter[h]
        ms = jnp.mean(o * o, axis=-1, keepdims=True)
        o_ref[:, vs[h]] = (_silu(g_ref[:, vs[h]].astype(F32)) * (o * lax.rsqrt(ms + EPS))).astype(o_ref.dtype)


def _retention_tables(seq):
    pos = jnp.arange(seq, dtype=F32)
    theta = jnp.repeat(1.0 / (10000.0 ** jnp.linspace(0.0, 1.0, C_QK_DIM // 2)), 2)
    ang = pos[:, None] * theta[None, :]
    pair_sign = jnp.where(jnp.arange(C_QK_DIM) % 2 == 0, -1.0, 1.0)
    log_gamma = jnp.log(1.0 - 2.0 ** (-5.0 - jnp.arange(C_HEADS, dtype=F32)))
    idx = jnp.arange(RET_C, dtype=F32)
    causal = idx[:, None] >= idx[None, :]
    idec = jnp.exp(jnp.where(causal[None], (idx[:, None] - idx[None, :])[None] * log_gamma[:, None, None],
                             -jnp.inf))
    qdec = jnp.exp((idx + 1.0)[None, :] * log_gamma[:, None])[..., None]
    kdec = jnp.exp((RET_C - 1.0 - idx)[None, :] * log_gamma[:, None])[..., None]
    cdec = jnp.exp(RET_C * log_gamma)[:, None, None]
    return (jnp.cos(ang), jnp.sin(ang) * pair_sign[None, :], idec,
            jnp.broadcast_to(qdec, (C_HEADS, RET_C, C_QK_DIM)),
            jnp.broadcast_to(kdec, (C_HEADS, RET_C, C_QK_DIM)),
            jnp.broadcast_to(cdec, (C_HEADS, 8, C_V_DIM)))


def _retention(p16, tables, bsz, seq):
    cos, sin, idec, qdec, kdec, cdec = tables
    nj = seq // RET_C
    n = bsz * seq
    v_blk = C_V_WIDTH // 1024

    def whole(a):
        return pl.BlockSpec(a.shape, lambda b, j: (0,) * a.ndim)

    return pl.pallas_call(
        _ret_kernel,
        grid=(bsz, nj),
        in_specs=[
            pl.BlockSpec((RET_C, C_QK_WIDTH), lambda b, j: (b * nj + j, P16_CQ)),
            pl.BlockSpec((RET_C, C_QK_WIDTH), lambda b, j: (b * nj + j, P16_CK)),
            pl.BlockSpec((RET_C, C_V_WIDTH), lambda b, j: (b * nj + j, P16_CV // v_blk)),
            pl.BlockSpec((RET_C, C_V_WIDTH), lambda b, j: (b * nj + j, P16_CG // v_blk)),
            pl.BlockSpec((RET_C, C_QK_DIM), lambda b, j: (j, 0)),
            pl.BlockSpec((RET_C, C_QK_DIM), lambda b, j: (j, 0)),
            whole(idec), whole(qdec), whole(kdec), whole(cdec),
        ],
        out_specs=pl.BlockSpec((RET_C, C_V_WIDTH), lambda b, j: (b * nj + j, 0)),
        out_shape=jax.ShapeDtypeStruct((n, C_V_WIDTH), BF16),
        scratch_shapes=[pltpu.VMEM((C_HEADS, C_QK_DIM, C_V_DIM), F32)],
        compiler_params=_cparams(("arbitrary", "arbitrary")),
        name="retention",
    )(p16, p16, p16, p16, cos, sin, idec, qdec, kdec, cdec)


def _merge_kernel(oa_ref, ob_ref, oc_ref, ga_ref, gb_ref, gc_ref, x_ref, mod_ref, g2_ref,
                  wa_ref, wb_ref, wc_ref, wo_ref, wrh_ref, wrl_ref, br_ref,
                  x1_ref, h2_ref, route_ref, *, tiles_per_batch):
    b = pl.program_id(0) // tiles_per_batch

    def gated(o_ref, w_ref, g_ref):
        y = jnp.dot(o_ref[...], w_ref[...], preferred_element_type=F32)
        return jax.nn.sigmoid(g_ref[...].astype(F32)) * y

    merged = gated(oa_ref, wa_ref, ga_ref) + gated(ob_ref, wb_ref, gb_ref) + gated(oc_ref, wc_ref, gc_ref)
    y = jnp.dot(merged.astype(BF16), wo_ref[...], preferred_element_type=F32)
    gt1 = mod_ref[pl.ds(b, 1), 2 * D_MODEL:3 * D_MODEL]
    x1 = x_ref[...] + gt1 * y
    x1_ref[...] = x1
    sh2 = mod_ref[pl.ds(b, 1), 3 * D_MODEL:4 * D_MODEL]
    sc2 = mod_ref[pl.ds(b, 1), 4 * D_MODEL:5 * D_MODEL]
    h2 = _rms_mod(x1, g2_ref[...], sc2, sh2)
    h_hi = h2.astype(BF16)
    words = _pack_bf16_pairs(h2)
    for j in range(h2_ref.shape[0]):
        h2_ref[j] = words[:, j * SC_SUB:(j + 1) * SC_SUB]
    h_lo = (h2 - h_hi.astype(F32)).astype(BF16)
    logits = (jnp.dot(h_hi, wrh_ref[...], preferred_element_type=F32)
              + jnp.dot(h_lo, wrh_ref[...], preferred_element_type=F32)
              + jnp.dot(h_hi, wrl_ref[...], preferred_element_type=F32)) + br_ref[...]
    lane = lax.broadcasted_iota(jnp.int32, logits.shape, 1).astype(F32)
    neg_inf = -jnp.inf

    def first_argmax(vals):
        top = jnp.max(vals, axis=-1, keepdims=True)
        idx = jnp.min(jnp.where(vals == top, lane, float(LANES)), axis=-1, keepdims=True)
        return top, idx

    gl = jnp.where(lane < N_GROUPS, logits, neg_inf)
    gmax, gsel = first_argmax(gl)
    gprob = 1.0 / jnp.sum(jnp.exp(gl - gmax), axis=-1, keepdims=True)
    lo = N_GROUPS + EXPERTS_PER_GROUP * gsel
    el = jnp.where((lane >= lo) & (lane < lo + EXPERTS_PER_GROUP), logits, neg_inf)
    v1, i1 = first_argmax(el)
    el2 = jnp.where(lane == i1, neg_inf, el)
    v2, i2 = first_argmax(el2)
    e2 = jnp.exp(v2 - v1)
    den = 1.0 + e2
    route_ref[...] = jnp.where(lane == 0.0, i1 - N_GROUPS,
                               jnp.where(lane == 1.0, i2 - N_GROUPS,
                                         jnp.where(lane == 2.0, gprob / den,
                                                   jnp.where(lane == 3.0, gprob * (e2 / den), 0.0))))


def _merge(o_a, o_b, o_c, p16, x2, mod_l, g2, wa, wb, wc, wo, wr_hi, wr_lo, br, seq):
    n = x2.shape[0]
    tm = min(512, seq)
    one = pl.Buffered(1)

    def rows(width, cb=0):
        return pl.BlockSpec((tm, width), lambda i: (i, cb))

    def whole(a):
        return pl.BlockSpec(a.shape, lambda i: (0,) * a.ndim, pipeline_mode=one)

    return pl.pallas_call(
        functools.partial(_merge_kernel, tiles_per_batch=seq // tm),
        grid=(n // tm,),
        in_specs=[rows(A_WIDTH), rows(B_WIDTH), rows(C_V_WIDTH),
                  rows(D_MODEL, P16_GA), rows(D_MODEL, P16_GB), rows(D_MODEL, P16_GC),
                  rows(D_MODEL), whole(mod_l), pl.BlockSpec((1, D_MODEL), lambda i: (0, 0)),
                  whole(wa), whole(wb), whole(wc), whole(wo), whole(wr_hi), whole(wr_lo), whole(br)],
        out_specs=[rows(D_MODEL), pl.BlockSpec((SC_PIECES, tm, SC_SUB), lambda i: (0, i, 0)), rows(LANES)],
        out_shape=[jax.ShapeDtypeStruct((n, D_MODEL), F32),
                   jax.ShapeDtypeStruct((SC_PIECES, n, SC_SUB), jnp.int32),
                   jax.ShapeDtypeStruct((n, LANES), F32)],
        compiler_params=_cparams(("arbitrary",)),
        name="merge_route",
    )(o_a, o_b, o_c, p16, p16, p16, x2, mod_l, g2.reshape(1, D_MODEL), wa, wb, wc, wo, wr_hi, wr_lo, br)


def _lane_pick(vals, lane, idx):
    return jnp.sum(jnp.where(lane == idx, vals, 0.0), axis=-1, keepdims=True)


def _rank_kernel(route_ref, rk_ref, cnt_ref, run_ref):
    @pl.when(pl.program_id(0) == 0)
    def _():
        run_ref[...] = jnp.zeros_like(run_ref)

    route = route_ref[...]
    tb = route.shape[0]
    lane = lax.broadcasted_iota(jnp.int32, route.shape, 1).astype(F32)
    e1 = _lane_pick(route, lane, 0.0)
    e2 = _lane_pick(route, lane, 1.0)
    sel = jnp.where((lane == e1) | (lane == e2), 1.0, 0.0)
    r = lax.broadcasted_iota(jnp.int32, (tb, tb), 0)
    c = lax.broadcasted_iota(jnp.int32, (tb, tb), 1)
    before = jnp.where(c < r, 1.0, 0.0).astype(BF16)
    rank = jnp.dot(before, sel.astype(BF16), preferred_element_type=F32) + run_ref[0:1, :]
    rk_ref[...] = jnp.where(lane == 0.0, _lane_pick(rank, lane, e1),
                            jnp.where(lane == 1.0, _lane_pick(rank, lane, e2), 0.0))
    run_ref[...] = run_ref[...] + jnp.sum(sel, axis=0, keepdims=True)
    cnt_ref[...] = run_ref[...]


def _expert_ranks(route):
    n = route.shape[0]
    tb = min(PLAN_TB, n)
    return pl.pallas_call(
        _rank_kernel,
        grid=(n // tb,),
        in_specs=[pl.BlockSpec((tb, LANES), lambda i: (i, 0))],
        out_specs=[pl.BlockSpec((tb, LANES), lambda i: (i, 0)),
                   pl.BlockSpec((8, LANES), lambda i: (0, 0))],
        out_shape=[jax.ShapeDtypeStruct((n, LANES), F32), jax.ShapeDtypeStruct((8, LANES), F32)],
        scratch_shapes=[pltpu.VMEM((8, LANES), F32)],
        compiler_params=_cparams(("arbitrary",)),
        name="expert_ranks",
    )(route)


def _plan_kernel(cnt_ref, route_ref, rk_ref, pos_ref, tmap_ref):
    lane_i = lax.broadcasted_iota(jnp.int32, (8, LANES), 1)
    cnt = jnp.where(lane_i < N_EXPERTS, cnt_ref[...], 0.0)
    padded = jnp.floor((cnt + (MOE_TM - 1)) * (1.0 / MOE_TM)) * MOE_TM
    r = lax.broadcasted_iota(jnp.int32, (LANES, LANES), 0)
    c = lax.broadcasted_iota(jnp.int32, (LANES, LANES), 1)
    base = jnp.dot(padded, jnp.where(r < c, 1.0, 0.0), precision=lax.Precision.HIGHEST,
                   preferred_element_type=F32)

    route = route_ref[...]
    lane = lax.broadcasted_iota(jnp.int32, route.shape, 1).astype(F32)
    rk = rk_ref[...]
    base_row = base[0:1, :]
    pos1 = _lane_pick(base_row, lane, _lane_pick(route, lane, 0.0)) + _lane_pick(rk, lane, 0.0)
    pos2 = _lane_pick(base_row, lane, _lane_pick(route, lane, 1.0)) + _lane_pick(rk, lane, 1.0)
    pos_ref[...] = jnp.where(lane == 0.0, pos1, jnp.where(lane == 1.0, pos2, 0.0)).astype(jnp.int32)

    @pl.when(pl.program_id(0) == 0)
    def _():
        nt = tmap_ref.shape[0]
        tlane = lax.broadcasted_iota(jnp.int32, (nt, LANES), 1)
        start = (lax.broadcasted_iota(jnp.int32, (nt, LANES), 0) * MOE_TM).astype(F32)
        end_row = jnp.where(tlane < N_EXPERTS, base_row + padded[0:1, :], 3e38)
        expert = jnp.sum(jnp.where(end_row <= start, 1.0, 0.0), axis=-1, keepdims=True)
        expert_c = jnp.minimum(expert, N_EXPERTS - 1.0)
        tl = tlane.astype(F32)
        left = _lane_pick(cnt[0:1, :], tl, expert_c) - (start[:, 0:1] - _lane_pick(base_row, tl, expert_c))
        valid = jnp.where(expert < N_EXPERTS, jnp.clip(left, 0.0, float(MOE_TM)), 0.0)
        tmap_ref[...] = jnp.where(tlane == 0, expert_c, jnp.where(tlane == 1, valid, 0.0)).astype(jnp.int32)


def _expert_plan(cnt, route, rk, n_tiles):
    n = route.shape[0]
    tb = min(PLAN_TB, n)
    nt_pad = -(-n_tiles // 8) * 8
    return pl.pallas_call(
        _plan_kernel,
        grid=(n // tb,),
        in_specs=[pl.BlockSpec((8, LANES), lambda i: (0, 0)),
                  pl.BlockSpec((tb, LANES), lambda i: (i, 0)),
                  pl.BlockSpec((tb, LANES), lambda i: (i, 0))],
        out_specs=[pl.BlockSpec((tb, LANES), lambda i: (i, 0)),
                   pl.BlockSpec((nt_pad, LANES), lambda i: (0, 0))],
        out_shape=[jax.ShapeDtypeStruct((n, LANES), jnp.int32),
                   jax.ShapeDtypeStruct((nt_pad, LANES), jnp.int32)],
        compiler_params=_cparams(("arbitrary",)),
        name="expert_plan",
    )(cnt, route, rk)


def _sc_mesh():
    return plsc.VectorSubcoreMesh(core_axis_name="c", subcore_axis_name="s")


def _sc_scatter_rows(src, idx, out_rows):
    m = idx.shape[0]
    n_src_win = src.shape[0] // SC_WINDOW

    @functools.partial(pl.kernel, out_type=jax.ShapeDtypeStruct((out_rows, src.shape[1]), src.dtype),
                       mesh=_sc_mesh(), scratch_types=[])
    def scatter(x_hbm, i_hbm, o_hbm):
        def body(x_vmem, i_vmem):
            pltpu.sync_copy(x_vmem, o_hbm.at[i_vmem.at[0]])

        pltpu.emit_pipeline(
            body, grid=(m // SC_WINDOW,),
            in_specs=[pl.BlockSpec((SC_WINDOW, src.shape[1]), lambda i: (i % n_src_win, 0)),
                      pl.BlockSpec((1, SC_WINDOW), lambda i: (0, i))],
            out_specs=[], core_axis_name=("c", "s"),
            dimension_semantics=(pltpu.PARALLEL,))(x_hbm, i_hbm)

    return scatter(src, idx.reshape(1, m))


def _sc_gather_rows(table, idx):
    m = idx.shape[0]

    @functools.partial(pl.kernel, out_type=jax.ShapeDtypeStruct((m, table.shape[1]), table.dtype),
                       mesh=_sc_mesh(), scratch_types=[])
    def gather(x_hbm, i_hbm, o_hbm):
        def body(i_vmem, o_vmem):
            pltpu.sync_copy(x_hbm.at[i_vmem.at[0]], o_vmem)

        pltpu.emit_pipeline(
            body, grid=(m // SC_WINDOW,),
            in_specs=[pl.BlockSpec((1, SC_WINDOW), lambda i: (0, i))],
            out_specs=[pl.BlockSpec((SC_WINDOW, table.shape[1]), lambda i: (i, 0))],
            core_axis_name=("c", "s"),
            dimension_semantics=(pltpu.PARALLEL,))(i_hbm, o_hbm)

    return gather(table, idx.reshape(1, m))


def _piece_row_index(pos, rows):
    return (jnp.arange(SC_PIECES, dtype=jnp.int32)[:, None] * rows + pos[None, :]).reshape(-1)


def _grouped_kernel(te_ref, tv_ref, x_ref, wg_ref, wu_ref, wd_ref, o_ref):
    valid = tv_ref[pl.program_id(0)]

    @pl.when(valid > 0)
    def _():
        words = jnp.concatenate([x_ref[j] for j in range(SC_PIECES)], axis=1)
        row = lax.broadcasted_iota(jnp.int32, words.shape, 0)
        words = jnp.where(row < valid, words, 0)
        x = _unpack_bf16_pairs(words).astype(BF16)
        a = jnp.dot(x, wg_ref[0].astype(BF16), preferred_element_type=F32)
        u = jnp.dot(x, wu_ref[0].astype(BF16), preferred_element_type=F32)
        hm = (_silu(a) * u).astype(BF16)
        out = _pack_bf16_pairs(jnp.dot(hm, wd_ref[0].astype(BF16), preferred_element_type=F32))
        for j in range(SC_PIECES):
            o_ref[j] = out[:, j * SC_SUB:(j + 1) * SC_SUB]

    @pl.when(valid <= 0)
    def _():
        o_ref[...] = jnp.zeros_like(o_ref)


def _grouped_experts(tile_expert, tile_valid, xs, wg, wu, wd, layer):
    n_tiles = tile_expert.shape[0]
    rows_block = pl.BlockSpec((SC_PIECES, MOE_TM, SC_SUB), lambda i, te, tv: (0, i, 0))
    return pl.pallas_call(
        _grouped_kernel,
        grid_spec=pltpu.PrefetchScalarGridSpec(
            num_scalar_prefetch=2,
            grid=(n_tiles,),
            in_specs=[rows_block,
                      pl.BlockSpec((None, 1, D_MODEL, D_EXPERT), lambda i, te, tv: (layer, te[i], 0, 0)),
                      pl.BlockSpec((None, 1, D_MODEL, D_EXPERT), lambda i, te, tv: (layer, te[i], 0, 0)),
                      pl.BlockSpec((None, 1, D_EXPERT, D_MODEL), lambda i, te, tv: (layer, te[i], 0, 0))],
            out_specs=rows_block),
        out_shape=jax.ShapeDtypeStruct(xs.shape, jnp.int32),
        compiler_params=_cparams(("arbitrary",)),
        name="grouped_experts",
    )(tile_expert, tile_valid, xs, wg, wu, wd)


def _combine_kernel(x1_ref, y_ref, route_ref, mod_ref, o_ref, *, tiles_per_batch):
    b = pl.program_id(0) // tiles_per_batch
    gt2 = mod_ref[pl.ds(b, 1), 5 * D_MODEL:6 * D_MODEL]
    route = route_ref[...]
    lane = lax.broadcasted_iota(jnp.int32, route.shape, 1).astype(F32)
    w1 = _lane_pick(route, lane, 2.0)
    w2 = _lane_pick(route, lane, 3.0)
    y1 = _unpack_bf16_pairs(jnp.concatenate([y_ref[0, j] for j in range(SC_PIECES)], axis=1))
    y2 = _unpack_bf16_pairs(jnp.concatenate([y_ref[1, j] for j in range(SC_PIECES)], axis=1))
    o_ref[...] = x1_ref[...] + gt2 * (w1 * y1 + w2 * y2)


def _combine(x1, y2, route, mod_l, seq):
    n = x1.shape[0]
    tm = min(1024, seq)
    return pl.pallas_call(
        functools.partial(_combine_kernel, tiles_per_batch=seq // tm),
        grid=(n // tm,),
        in_specs=[pl.BlockSpec((tm, D_MODEL), lambda i: (i, 0)),
                  pl.BlockSpec((2, SC_PIECES, tm, SC_SUB), lambda i: (0, 0, i, 0)),
                  pl.BlockSpec((tm, LANES), lambda i: (i, 0)),
                  pl.BlockSpec(mod_l.shape, lambda i: (0, 0))],
        out_specs=pl.BlockSpec((tm, D_MODEL), lambda i: (i, 0)),
        out_shape=jax.ShapeDtypeStruct((n, D_MODEL), F32),
        compiler_params=_cparams(("arbitrary",)),
        name="moe_combine",
    )(x1, y2, route, mod_l)


def _moe(h2, route, x1, mod_l, wg, wu, wd, layer, seq):
    n = h2.shape[1]
    n_tiles = (2 * n) // MOE_TM + N_EXPERTS
    rows = n_tiles * MOE_TM
    rk, cnt = _expert_ranks(route)
    pos, tmap = _expert_plan(cnt, route, rk, n_tiles)
    idx = jnp.concatenate([_piece_row_index(pos[:, 0], rows), _piece_row_index(pos[:, 1], rows)])
    xs = _sc_scatter_rows(h2.reshape(SC_PIECES * n, SC_SUB), idx, SC_PIECES * rows)
    ys = _grouped_experts(tmap[:n_tiles, 0], tmap[:n_tiles, 1], xs.reshape(SC_PIECES, rows, SC_SUB),
                          wg, wu, wd, layer)
    y2 = _sc_gather_rows(ys.reshape(SC_PIECES * rows, SC_SUB), idx).reshape(2, SC_PIECES, n, SC_SUB)
    return _combine(x1, y2, route, mod_l, seq)


def _final_norm_kernel(x_ref, g_ref, o_ref):
    x = x_ref[...]
    ms = jnp.mean(x * x, axis=-1, keepdims=True)
    o_ref[...] = x * lax.rsqrt(ms + EPS) * g_ref[...]


def _final_norm(x2, g, seq):
    n = x2.shape[0]
    tm = min(1024, seq)
    return pl.pallas_call(
        _final_norm_kernel,
        grid=(n // tm,),
        in_specs=[pl.BlockSpec((tm, D_MODEL), lambda i: (i, 0)),
                  pl.BlockSpec((1, D_MODEL), lambda i: (0, 0))],
        out_specs=pl.BlockSpec((tm, D_MODEL), lambda i: (i, 0)),
        out_shape=jax.ShapeDtypeStruct((n, D_MODEL), F32),
        compiler_params=_cparams(("arbitrary",)),
        name="final_norm",
    )(x2, g.reshape(1, D_MODEL))


_IN_OFFS = [sum(IN_SPLITS[:i]) for i in range(len(IN_SPLITS) + 1)]
(_AQ, _AK, _AV, _IQ, _IK, _IW, _BQ, _BF, _BI, _BG, _CQ, _CK, _CV, _CG, _GA, _GB, _GC) = range(len(IN_SPLITS))


def _pack_kernel(wt_ref, w16_ref, w32_ref, wvt_ref, wqt_ref):
    def rows_of(seg):
        return wt_ref[0, _IN_OFFS[seg]:_IN_OFFS[seg + 1], :]

    def padded_t(seg):
        v = rows_of(seg)
        v = jnp.concatenate([v, jnp.zeros((LANES - v.shape[0], v.shape[1]), F32)], axis=0)
        return v.T.astype(BF16)

    scale = {_AQ: A_HEAD_DIM ** -0.5 * LOG2_E, _CK: C_QK_DIM ** -0.5}
    at = 0
    for seg in (_CV, _CG, _AK, _BQ, _BI, _BG, _CQ, _CK, _GA, _GB, _GC):
        v = rows_of(seg)
        if seg in scale:
            v = v * scale[seg]
        for c0 in range(0, v.shape[0], 1024):
            w16_ref[0, :, at + c0:at + c0 + 1024] = v[c0:c0 + 1024].T.astype(BF16)
        at += v.shape[0]

    w32_ref[0, :, P32_BF:P32_BF + B_WIDTH] = rows_of(_BF).T.astype(BF16)
    w32_ref[0, :, P32_IK:P32_IK + LANES] = padded_t(_IK)
    w32_ref[0, :, P32_IW:P32_IW + LANES] = padded_t(_IW)
    wvt_ref[0] = rows_of(_AV).astype(BF16)
    wqt_ref[0, 0:A_WIDTH, :] = (rows_of(_AQ) * scale[_AQ]).astype(BF16)
    wqt_ref[0, A_WIDTH:, :] = rows_of(_IQ).astype(BF16)


def _pack_w_in(w_in):
    depth, d, width = w_in.shape
    rows = LANES
    w16_width = 13 * 1024
    qiq_rows = A_WIDTH + IDX_HEADS * IDX_DIM
    w32_width = P32_IW + LANES
    w_in_t = jnp.swapaxes(w_in, 1, 2)
    return pl.pallas_call(
        _pack_kernel,
        grid=(depth, d // rows),
        in_specs=[pl.BlockSpec((1, width, rows), lambda l, r: (l, 0, r))],
        out_specs=[pl.BlockSpec((1, rows, w16_width), lambda l, r: (l, r, 0)),
                   pl.BlockSpec((1, rows, w32_width), lambda l, r: (l, r, 0)),
                   pl.BlockSpec((1, A_WIDTH, rows), lambda l, r: (l, 0, r)),
                   pl.BlockSpec((1, qiq_rows, rows), lambda l, r: (l, 0, r))],
        out_shape=[jax.ShapeDtypeStruct((depth, d, w16_width), BF16),
                   jax.ShapeDtypeStruct((depth, d, w32_width), BF16),
                   jax.ShapeDtypeStruct((depth, A_WIDTH, d), BF16),
                   jax.ShapeDtypeStruct((depth, qiq_rows, d), BF16)],
        compiler_params=_cparams(("arbitrary", "arbitrary")),
        name="pack_w_in",
    )(w_in_t)


def _split_bf16(w):
    hi = w.astype(BF16)
    return hi, (w - hi.astype(F32)).astype(BF16)


def kernel(x, c, rel_bias, hgrn_lb_raw, norm1_g, norm2_g, ada_w, ada_b, w_in, hgrn_norm_g, w_branch_a,
           w_branch_b, w_branch_c, w_out, router_group_w, router_group_b, router_expert_w,
           router_expert_b, expert_w_gate, expert_w_up, expert_w_down, final_norm_g):
    bsz, seq, _ = x.shape
    depth = w_in.shape[0]
    n = bsz * seq
    x2 = x.reshape(n, D_MODEL)
    tq = min(DSA_TQ, seq)

    lb_all = _hgrn_lower_bounds(hgrn_lb_raw)
    c_pad = jnp.pad(c, ((0, (-bsz) % 8), (0, 0)))
    mod = _ada_mod(c_pad, ada_w, ada_b)
    bias_tiles = _bias_tiles(rel_bias, tq)
    ret_tables = _retention_tables(seq)
    w16_all, w32_all, wvt_all, wqt_all = _pack_w_in(w_in)

    for l in range(depth):
        p16, p32, vt, qiq_t = _project_all(x2, mod[l], norm1_g[l], w16_all, w32_all, wvt_all, wqt_all,
                                           l, tq, seq)
        o_a = _dsa_attention(qiq_t, p16, p32, vt, bias_tiles, bsz, seq)
        o_b = _hgrn2(p16, p32, lb_all[l], hgrn_norm_g[l], bsz, seq)
        o_c = _retention(p16, ret_tables, bsz, seq)
        wr = jnp.concatenate([router_group_w[l], router_expert_w[l],
                              jnp.zeros((D_MODEL, LANES - N_GROUPS - N_EXPERTS), F32)], axis=1)
        br = jnp.concatenate([router_group_b[l], router_expert_b[l],
                              jnp.zeros((LANES - N_GROUPS - N_EXPERTS,), F32)]).reshape(1, LANES)
        wr_hi, wr_lo = _split_bf16(wr)
        x1, h2, route = _merge(o_a, o_b, o_c, p16, x2, mod[l], norm2_g[l],
                               w_branch_a[l].astype(BF16), w_branch_b[l].astype(BF16),
                               w_branch_c[l].astype(BF16), w_out[l].astype(BF16),
                               wr_hi, wr_lo, br, seq)
        x2 = _moe(h2, route, x1, mod[l], expert_w_gate, expert_w_up, expert_w_down, l, seq)

    return _final_norm(x2, final_norm_g, seq).reshape(bsz, seq, D_MODEL)
```
